```python
import jax, jax.numpy as jnp
from jax import lax
import numpy as np

D_MODEL = 1024
BATCH = 8
SEQ = 2048
DEPTH = 1

HGRN_HEADS = 8
HGRN_EXPAND = 128
HGRN_FWD = HGRN_HEADS * HGRN_EXPAND
HGRN_IN = D_MODEL
HGRN_VDIM = HGRN_IN // HGRN_HEADS
HGRN_SCALE = HGRN_EXPAND ** -0.5
CHUNK = 32
RWKV_HEAD = 64
RWKV_DIM = D_MODEL
RWKV_HEADS = RWKV_DIM // RWKV_HEAD
W_LORA = 64
A_LORA = 64
G_LORA = 128
GN_EPS = 1e-5 * RWKV_HEAD
D_FF = 2816
CONV_W = 3
EPS = 1e-6

HGRN_COLS = 2 * HGRN_FWD + 2 * HGRN_IN
RWKV_COLS = 3 * RWKV_DIM + W_LORA + A_LORA + G_LORA
GATE_COLS = 2 * D_MODEL
IN_COLS = HGRN_COLS + RWKV_COLS + GATE_COLS

kernel_name = "hgrn2_rwkv7_gated_hybrid_convffn"

F32 = jnp.float32


def _rmsnorm(x, g):
    xf = x.astype(F32)
    y = xf * lax.rsqrt(jnp.mean(xf * xf, axis=-1, keepdims=True) + EPS)
    return (y * g.astype(F32)).astype(x.dtype)


def _split(z, sizes):
    outs, off = [], 0
    for s in sizes:
        outs.append(z[..., off:off + s])
        off += s
    return outs


def _shift1(z):
    return jnp.pad(z[:, :-1], ((0, 0), (1, 0), (0, 0)))


def _causal_dwconv(h, w, b):
    S = h.shape[1]
    hp = jnp.pad(h, ((0, 0), (CONV_W - 1, 0), (0, 0)))
    out = b
    for j in range(CONV_W):
        out = out + w[j] * hp[:, j:j + S]
    return out


def _hgrn2_chunkwise(q, f_log, k, v):
    B, S, H, K = q.shape
    V = v.shape[-1]
    N = S // CHUNK

    def chunk(t):
        return t.reshape(B, N, CHUNK, H, t.shape[-1]).transpose(0, 3, 1, 2, 4)

    qc, gc, kc, vc = chunk(q), chunk(f_log), chunk(k), chunk(v)
    b = jnp.cumsum(gc, axis=3)
    b_ref = b[:, :, :, CHUNK // 2 - 1:CHUNK // 2, :]
    q_in = qc * jnp.exp(b - b_ref)
    k_in = kc * jnp.exp(b_ref - b)
    scores = jnp.einsum('bhnck,bhndk->bhncd', q_in, k_in)
    mask = jnp.tril(jnp.ones((CHUNK, CHUNK), dtype=bool))
    scores = jnp.where(mask, scores, 0.0)
    o_intra = jnp.einsum('bhncd,bhndv->bhncv', scores, vc)

    b_last = b[:, :, :, -1, :]
    u = jnp.einsum('bhnck,bhncv->bhnkv', kc * jnp.exp(b_last[:, :, :, None, :] - b), vc)
    decay = jnp.exp(b_last)

    def step(state, inp):
        d, u_n = inp
        return d[..., None] * state + u_n, state

    s0 = jnp.zeros((B, H, K, V), F32)
    _, s_prev = lax.scan(step, s0, (decay.transpose(2, 0, 1, 3), u.transpose(2, 0, 1, 3, 4)))
    s_prev = s_prev.transpose(1, 2, 0, 3, 4)
    o_inter = jnp.einsum('bhnck,bhnkv->bhncv', qc * jnp.exp(b), s_prev)
    o = o_intra + o_inter
    return o.transpose(0, 2, 3, 1, 4).reshape(B, S, H, V)


def _rwkv7_scan(r, w, k, v, a_vec, b_vec):
    B, S, H, N = r.shape

    def step(state, inp):
        r_t, w_t, k_t, v_t, a_t, b_t = inp
        sa = jnp.einsum('bhvk,bhk->bhv', state, a_t)
        state = (state * w_t[:, :, None, :] + sa[..., None] * b_t[:, :, None, :]
                 + v_t[..., None] * k_t[:, :, None, :])
        y = jnp.einsum('bhvk,bhk->bhv', state, r_t)
        return state, y

    xs = tuple(t.astype(F32).transpose(1, 0, 2, 3) for t in (r, w, k, v, a_vec, b_vec))
    _, y = lax.scan(step, jnp.zeros((B, H, N, N), F32), xs)
    return y.transpose(1, 0, 2, 3)


def _token_mixer(xn, lb, w_in, hgrn_gnorm, w_branch_a, rwkv_mu, rwkv_w0, rwkv_w2, rwkv_a0,
                 rwkv_a2, rwkv_g2, rwkv_k_k, rwkv_k_a, rwkv_r_k, rwkv_ln_w, rwkv_ln_b,
                 w_branch_b, w_out):
    B, S, _ = xn.shape
    z = xn @ w_in
    z_h, z_r, z_g = _split(z, [HGRN_COLS, RWKV_COLS, GATE_COLS])

    hq, hf, hi, hg = _split(z_h, [HGRN_FWD, HGRN_FWD, HGRN_IN, HGRN_IN])
    q = jax.nn.silu(hq.astype(F32)).reshape(B, S, HGRN_HEADS, HGRN_EXPAND) * HGRN_SCALE
    f = lb + (1.0 - lb) * jax.nn.sigmoid(hf.astype(F32))
    f = f.reshape(B, S, HGRN_HEADS, HGRN_EXPAND)
    k_h = 1.0 - f
    vi = hi.astype(F32).reshape(B, S, HGRN_HEADS, HGRN_VDIM)
    o_a = _hgrn2_chunkwise(q, jnp.log(f), k_h, vi)
    o_a = o_a * lax.rsqrt(jnp.mean(o_a * o_a, axis=-1, keepdims=True) + EPS)
    o_a = o_a * hgrn_gnorm.astype(F32).reshape(HGRN_HEADS, HGRN_VDIM)
    o_a = o_a.reshape(B, S, HGRN_IN) * jax.nn.silu(hg.astype(F32))
    y_a = o_a.astype(xn.dtype) @ w_branch_a

    z_r = z_r + rwkv_mu * (_shift1(z_r) - z_r)
    rr, kr, vr, wz, az, gz = _split(z_r, [RWKV_DIM, RWKV_DIM, RWKV_DIM, W_LORA, A_LORA, G_LORA])
    w_log = -jax.nn.softplus(-(rwkv_w0 + jnp.tanh(wz) @ rwkv_w2).astype(F32)) - 0.5
    decay = jnp.exp(-jnp.exp(w_log))
    a = jax.nn.sigmoid((rwkv_a0 + az @ rwkv_a2).astype(F32))
    g = jax.nn.sigmoid(gz) @ rwkv_g2
    kr = kr.astype(F32)
    kk = (kr * rwkv_k_k).reshape(B, S, RWKV_HEADS, RWKV_HEAD)
    kk = kk / jnp.maximum(jnp.linalg.norm(kk, axis=-1, keepdims=True), 1e-12)
    kr = kr * (1.0 + (a - 1.0) * rwkv_k_a)

    def heads(t):
        return t.astype(F32).reshape(B, S, RWKV_HEADS, RWKV_HEAD)

    r_h, k_r, v_r, w_h, a_h = heads(rr), heads(kr), heads(vr), heads(decay), heads(a)
    y = _rwkv7_scan(r_h, w_h, k_r, v_r, -kk, kk * a_h)
    mu_y = jnp.mean(y, axis=-1, keepdims=True)
    var_y = jnp.mean(jnp.square(y - mu_y), axis=-1, keepdims=True)
    y = ((y - mu_y) * lax.rsqrt(var_y + GN_EPS)).reshape(B, S, RWKV_DIM)
    y = y * rwkv_ln_w.astype(F32) + rwkv_ln_b.astype(F32)
    bonus = jnp.sum(r_h * k_r * rwkv_r_k.astype(F32), axis=-1, keepdims=True) * v_r
    o_b = (y + bonus.reshape(B, S, RWKV_DIM)) * g.astype(F32)
    y_b = o_b.astype(xn.dtype) @ w_branch_b

    ga, gb = _split(z_g, [D_MODEL, D_MODEL])
    merged = jax.nn.sigmoid(ga) * y_a + jax.nn.sigmoid(gb) * y_b
    return merged @ w_out


def _conv_ffn(xn, w_up, conv_w, conv_b, w_down):
    hu = xn @ w_up
    hc = _causal_dwconv(hu, conv_w, conv_b)
    gate, val = _split(hc, [D_FF, D_FF])
    return (jax.nn.silu(gate) * val) @ w_down


def _fwd_setup_inputs(seed: int = 0) -> dict:
    key = jax.random.key(seed)
    ks = jax.random.split(key, 32)
    L = DEPTH

    def nrm(k, shape, scale):
        return jax.random.normal(k, shape, F32) * scale

    def gain(k, shape):
        return 1.0 + 0.02 * jax.random.normal(k, shape, F32)

    return {
        "x": jax.random.normal(ks[0], (BATCH, SEQ, D_MODEL), F32),
        "attn_pre_norm": gain(ks[1], (L, D_MODEL)),
        "w_in": nrm(ks[2], (L, D_MODEL, IN_COLS), D_MODEL ** -0.5),
        "hgrn_lb": nrm(ks[3], (DEPTH + 1, HGRN_FWD), 0.1),
        "hgrn_gnorm": gain(ks[4], (L, HGRN_IN)),
        "w_branch_a": nrm(ks[5], (L, HGRN_IN, D_MODEL), HGRN_IN ** -0.5),
        "rwkv_mu": jax.random.uniform(ks[6], (L, RWKV_COLS), F32),
        "rwkv_w0": jax.random.uniform(ks[7], (L, RWKV_DIM), F32, minval=-6.0, maxval=0.0),
        "rwkv_w2": nrm(ks[8], (L, W_LORA, RWKV_DIM), 0.1 * W_LORA ** -0.5),
        "rwkv_a0": nrm(ks[9], (L, RWKV_DIM), 0.1),
        "rwkv_a2": nrm(ks[10], (L, A_LORA, RWKV_DIM), 0.5 * A_LORA ** -0.5),
        "rwkv_g2": nrm(ks[11], (L, G_LORA, RWKV_DIM), G_LORA ** -0.5),
        "rwkv_k_k": 0.85 + 0.02 * jax.random.normal(ks[12], (L, RWKV_DIM), F32),
        "rwkv_k_a": gain(ks[13], (L, RWKV_DIM)),
        "rwkv_r_k": nrm(ks[14], (L, RWKV_HEADS, RWKV_HEAD), 0.1),
        "rwkv_ln_w": gain(ks[15], (L, RWKV_DIM)),
        "rwkv_ln_b": nrm(ks[16], (L, RWKV_DIM), 0.01),
        "w_branch_b": nrm(ks[17], (L, RWKV_DIM, D_MODEL), RWKV_DIM ** -0.5),
        "w_out": nrm(ks[18], (L, D_MODEL, D_MODEL), D_MODEL ** -0.5),
        "attn_post_norm": gain(ks[19], (L, D_MODEL)),
        "ffn_pre_norm": gain(ks[20], (L, D_MODEL)),
        "w_up": nrm(ks[21], (L, D_MODEL, 2 * D_FF), D_MODEL ** -0.5),
        "conv_w": nrm(ks[22], (L, CONV_W, 2 * D_FF), CONV_W ** -0.5),
        "conv_b": nrm(ks[23], (L, 2 * D_FF), 0.01),
        "w_down": nrm(ks[24], (L, D_FF, D_MODEL), D_FF ** -0.5),
        "ffn_post_norm": gain(ks[25], (L, D_MODEL)),
    }


def _fwd_reference(x, attn_pre_norm, w_in, hgrn_lb, hgrn_gnorm, w_branch_a, rwkv_mu, rwkv_w0,
              rwkv_w2, rwkv_a0, rwkv_a2, rwkv_g2, rwkv_k_k, rwkv_k_a, rwkv_r_k, rwkv_ln_w,
              rwkv_ln_b, w_branch_b, w_out, attn_post_norm, ffn_pre_norm, w_up, conv_w,
              conv_b, w_down, ffn_post_norm):
    lb_table = jnp.cumsum(jax.nn.softmax(hgrn_lb.astype(F32), axis=0), axis=0)
    h = x
    for l in range(DEPTH):
        xn = _rmsnorm(h, attn_pre_norm[l])
        mix = _token_mixer(xn, lb_table[l], w_in[l], hgrn_gnorm[l], w_branch_a[l], rwkv_mu[l],
                           rwkv_w0[l], rwkv_w2[l], rwkv_a0[l], rwkv_a2[l], rwkv_g2[l],
                           rwkv_k_k[l], rwkv_k_a[l], rwkv_r_k[l], rwkv_ln_w[l], rwkv_ln_b[l],
                           w_branch_b[l], w_out[l])
        h = h + _rmsnorm(mix, attn_post_norm[l])
        xn = _rmsnorm(h, ffn_pre_norm[l])
        ff = _conv_ffn(xn, w_up[l], conv_w[l], conv_b[l], w_down[l])
        h = h + _rmsnorm(ff, ffn_post_norm[l])
    return h


import jax as _jax
import jax.numpy as _jnp

TWIN_FORMAT = 'train_step'
FWD_PARAMS = ['x', 'attn_pre_norm', 'w_in', 'hgrn_lb', 'hgrn_gnorm', 'w_branch_a', 'rwkv_mu', 'rwkv_w0', 'rwkv_w2', 'rwkv_a0', 'rwkv_a2', 'rwkv_g2', 'rwkv_k_k', 'rwkv_k_a', 'rwkv_r_k', 'rwkv_ln_w', 'rwkv_ln_b', 'w_branch_b', 'w_out', 'attn_post_norm', 'ffn_pre_norm', 'w_up', 'conv_w', 'conv_b', 'w_down', 'ffn_post_norm']
TWIN_WEIGHTS = ['attn_pre_norm', 'w_in', 'hgrn_lb', 'hgrn_gnorm', 'w_branch_a', 'rwkv_mu', 'rwkv_w0', 'rwkv_w2', 'rwkv_a0', 'rwkv_a2', 'rwkv_g2', 'rwkv_k_k', 'rwkv_k_a', 'rwkv_r_k', 'rwkv_ln_w', 'rwkv_ln_b', 'w_branch_b', 'w_out', 'attn_post_norm', 'ffn_pre_norm', 'w_up', 'conv_w', 'conv_b', 'w_down', 'ffn_post_norm']
TWIN_DIFF_INPUT = 'x'
TWIN_INPUTS = ['x', 'attn_pre_norm', 'w_in', 'hgrn_lb', 'hgrn_gnorm', 'w_branch_a', 'rwkv_mu', 'rwkv_w0', 'rwkv_w2', 'rwkv_a0', 'rwkv_a2', 'rwkv_g2', 'rwkv_k_k', 'rwkv_k_a', 'rwkv_r_k', 'rwkv_ln_w', 'rwkv_ln_b', 'w_branch_b', 'w_out', 'attn_post_norm', 'ffn_pre_norm', 'w_up', 'conv_w', 'conv_b', 'w_down', 'ffn_post_norm', 'loss_target', 'm_attn_pre_norm', 'm_w_in', 'm_hgrn_lb', 'm_hgrn_gnorm', 'm_w_branch_a', 'm_rwkv_mu', 'm_rwkv_w0', 'm_rwkv_w2', 'm_rwkv_a0', 'm_rwkv_a2', 'm_rwkv_g2', 'm_rwkv_k_k', 'm_rwkv_k_a', 'm_rwkv_r_k', 'm_rwkv_ln_w', 'm_rwkv_ln_b', 'm_w_branch_b', 'm_w_out', 'm_attn_post_norm', 'm_ffn_pre_norm', 'm_w_up', 'm_conv_w', 'm_conv_b', 'm_w_down', 'm_ffn_post_norm', 'v_attn_pre_norm', 'v_w_in', 'v_hgrn_lb', 'v_hgrn_gnorm', 'v_w_branch_a', 'v_rwkv_mu', 'v_rwkv_w0', 'v_rwkv_w2', 'v_rwkv_a0', 'v_rwkv_a2', 'v_rwkv_g2', 'v_rwkv_k_k', 'v_rwkv_k_a', 'v_rwkv_r_k', 'v_rwkv_ln_w', 'v_rwkv_ln_b', 'v_w_branch_b', 'v_w_out', 'v_attn_post_norm', 'v_ffn_pre_norm', 'v_w_up', 'v_conv_w', 'v_conv_b', 'v_w_down', 'v_ffn_post_norm']
TWIN_OUTPUTS = ['loss', 'grad_x', 'grad_attn_pre_norm', 'grad_w_in', 'grad_hgrn_lb', 'grad_hgrn_gnorm', 'grad_w_branch_a', 'grad_rwkv_mu', 'grad_rwkv_w0', 'grad_rwkv_w2', 'grad_rwkv_a0', 'grad_rwkv_a2', 'grad_rwkv_g2', 'grad_rwkv_k_k', 'grad_rwkv_k_a', 'grad_rwkv_r_k', 'grad_rwkv_ln_w', 'grad_rwkv_ln_b', 'grad_w_branch_b', 'grad_w_out', 'grad_attn_post_norm', 'grad_ffn_pre_norm', 'grad_w_up', 'grad_conv_w', 'grad_conv_b', 'grad_w_down', 'grad_ffn_post_norm', 'delta_attn_pre_norm', 'delta_w_in', 'delta_hgrn_lb', 'delta_hgrn_gnorm', 'delta_w_branch_a', 'delta_rwkv_mu', 'delta_rwkv_w0', 'delta_rwkv_w2', 'delta_rwkv_a0', 'delta_rwkv_a2', 'delta_rwkv_g2', 'delta_rwkv_k_k', 'delta_rwkv_k_a', 'delta_rwkv_r_k', 'delta_rwkv_ln_w', 'delta_rwkv_ln_b', 'delta_w_branch_b', 'delta_w_out', 'delta_attn_post_norm', 'delta_ffn_pre_norm', 'delta_w_up', 'delta_conv_w', 'delta_conv_b', 'delta_w_down', 'delta_ffn_post_norm', 'new_m_attn_pre_norm', 'new_m_w_in', 'new_m_hgrn_lb', 'new_m_hgrn_gnorm', 'new_m_w_branch_a', 'new_m_rwkv_mu', 'new_m_rwkv_w0', 'new_m_rwkv_w2', 'new_m_rwkv_a0', 'new_m_rwkv_a2', 'new_m_rwkv_g2', 'new_m_rwkv_k_k', 'new_m_rwkv_k_a', 'new_m_rwkv_r_k', 'new_m_rwkv_ln_w', 'new_m_rwkv_ln_b', 'new_m_w_branch_b', 'new_m_w_out', 'new_m_attn_post_norm', 'new_m_ffn_pre_norm', 'new_m_w_up', 'new_m_conv_w', 'new_m_conv_b', 'new_m_w_down', 'new_m_ffn_post_norm', 'new_v_attn_pre_norm', 'new_v_w_in', 'new_v_hgrn_lb', 'new_v_hgrn_gnorm', 'new_v_w_branch_a', 'new_v_rwkv_mu', 'new_v_rwkv_w0', 'new_v_rwkv_w2', 'new_v_rwkv_a0', 'new_v_rwkv_a2', 'new_v_rwkv_g2', 'new_v_rwkv_k_k', 'new_v_rwkv_k_a', 'new_v_rwkv_r_k', 'new_v_rwkv_ln_w', 'new_v_rwkv_ln_b', 'new_v_w_branch_b', 'new_v_w_out', 'new_v_attn_post_norm', 'new_v_ffn_pre_norm', 'new_v_w_up', 'new_v_conv_w', 'new_v_conv_b', 'new_v_w_down', 'new_v_ffn_post_norm']
TWIN_LEAF_KINDS = {'loss': 'loss', 'grad_x': 'grad_x', 'grad_attn_pre_norm': 'grad_w', 'grad_w_in': 'grad_w', 'grad_hgrn_lb': 'grad_w', 'grad_hgrn_gnorm': 'grad_w', 'grad_w_branch_a': 'grad_w', 'grad_rwkv_mu': 'grad_w', 'grad_rwkv_w0': 'grad_w', 'grad_rwkv_w2': 'grad_w', 'grad_rwkv_a0': 'grad_w', 'grad_rwkv_a2': 'grad_w', 'grad_rwkv_g2': 'grad_w', 'grad_rwkv_k_k': 'grad_w', 'grad_rwkv_k_a': 'grad_w', 'grad_rwkv_r_k': 'grad_w', 'grad_rwkv_ln_w': 'grad_w', 'grad_rwkv_ln_b': 'grad_w', 'grad_w_branch_b': 'grad_w', 'grad_w_out': 'grad_w', 'grad_attn_post_norm': 'grad_w', 'grad_ffn_pre_norm': 'grad_w', 'grad_w_up': 'grad_w', 'grad_conv_w': 'grad_w', 'grad_conv_b': 'grad_w', 'grad_w_down': 'grad_w', 'grad_ffn_post_norm': 'grad_w', 'delta_attn_pre_norm': 'delta_w', 'delta_w_in': 'delta_w', 'delta_hgrn_lb': 'delta_w', 'delta_hgrn_gnorm': 'delta_w', 'delta_w_branch_a': 'delta_w', 'delta_rwkv_mu': 'delta_w', 'delta_rwkv_w0': 'delta_w', 'delta_rwkv_w2': 'delta_w', 'delta_rwkv_a0': 'delta_w', 'delta_rwkv_a2': 'delta_w', 'delta_rwkv_g2': 'delta_w', 'delta_rwkv_k_k': 'delta_w', 'delta_rwkv_k_a': 'delta_w', 'delta_rwkv_r_k': 'delta_w', 'delta_rwkv_ln_w': 'delta_w', 'delta_rwkv_ln_b': 'delta_w', 'delta_w_branch_b': 'delta_w', 'delta_w_out': 'delta_w', 'delta_attn_post_norm': 'delta_w', 'delta_ffn_pre_norm': 'delta_w', 'delta_w_up': 'delta_w', 'delta_conv_w': 'delta_w', 'delta_conv_b': 'delta_w', 'delta_w_down': 'delta_w', 'delta_ffn_post_norm': 'delta_w', 'new_m_attn_pre_norm': 'new_m', 'new_m_w_in': 'new_m', 'new_m_hgrn_lb': 'new_m', 'new_m_hgrn_gnorm': 'new_m', 'new_m_w_branch_a': 'new_m', 'new_m_rwkv_mu': 'new_m', 'new_m_rwkv_w0': 'new_m', 'new_m_rwkv_w2': 'new_m', 'new_m_rwkv_a0': 'new_m', 'new_m_rwkv_a2': 'new_m', 'new_m_rwkv_g2': 'new_m', 'new_m_rwkv_k_k': 'new_m', 'new_m_rwkv_k_a': 'new_m', 'new_m_rwkv_r_k': 'new_m', 'new_m_rwkv_ln_w': 'new_m', 'new_m_rwkv_ln_b': 'new_m', 'new_m_w_branch_b': 'new_m', 'new_m_w_out': 'new_m', 'new_m_attn_post_norm': 'new_m', 'new_m_ffn_pre_norm': 'new_m', 'new_m_w_up': 'new_m', 'new_m_conv_w': 'new_m', 'new_m_conv_b': 'new_m', 'new_m_w_down': 'new_m', 'new_m_ffn_post_norm': 'new_m', 'new_v_attn_pre_norm': 'new_v', 'new_v_w_in': 'new_v', 'new_v_hgrn_lb': 'new_v', 'new_v_hgrn_gnorm': 'new_v', 'new_v_w_branch_a': 'new_v', 'new_v_rwkv_mu': 'new_v', 'new_v_rwkv_w0': 'new_v', 'new_v_rwkv_w2': 'new_v', 'new_v_rwkv_a0': 'new_v', 'new_v_rwkv_a2': 'new_v', 'new_v_rwkv_g2': 'new_v', 'new_v_rwkv_k_k': 'new_v', 'new_v_rwkv_k_a': 'new_v', 'new_v_rwkv_r_k': 'new_v', 'new_v_rwkv_ln_w': 'new_v', 'new_v_rwkv_ln_b': 'new_v', 'new_v_w_branch_b': 'new_v', 'new_v_w_out': 'new_v', 'new_v_attn_post_norm': 'new_v', 'new_v_ffn_pre_norm': 'new_v', 'new_v_w_up': 'new_v', 'new_v_conv_w': 'new_v', 'new_v_conv_b': 'new_v', 'new_v_w_down': 'new_v', 'new_v_ffn_post_norm': 'new_v'}


def _forward(args):
    return _fwd_reference(*[args[k] for k in FWD_PARAMS])


def _output_shape():
    out = _jax.eval_shape(lambda: _forward(_fwd_setup_inputs(0)))
    return out.shape, out.dtype

N_MICROBATCH = 1
ADAM_LR = 0.001
ADAM_B1 = 0.9
ADAM_B2 = 0.999
ADAM_EPS = 1e-08
ADAM_WD = 0.01
ADAM_STEP = 10
PER_EXAMPLE_BATCH_AXIS = {'x': 0, 'loss_target': 0}
SHARED_INPUTS = []
_WEIGHT_DTYPES = {'attn_pre_norm': _jnp.float32, 'w_in': _jnp.float32, 'hgrn_lb': _jnp.float32, 'hgrn_gnorm': _jnp.float32, 'w_branch_a': _jnp.float32, 'rwkv_mu': _jnp.float32, 'rwkv_w0': _jnp.float32, 'rwkv_w2': _jnp.float32, 'rwkv_a0': _jnp.float32, 'rwkv_a2': _jnp.float32, 'rwkv_g2': _jnp.float32, 'rwkv_k_k': _jnp.float32, 'rwkv_k_a': _jnp.float32, 'rwkv_r_k': _jnp.float32, 'rwkv_ln_w': _jnp.float32, 'rwkv_ln_b': _jnp.float32, 'w_branch_b': _jnp.float32, 'w_out': _jnp.float32, 'attn_post_norm': _jnp.float32, 'ffn_pre_norm': _jnp.float32, 'w_up': _jnp.float32, 'conv_w': _jnp.float32, 'conv_b': _jnp.float32, 'w_down': _jnp.float32, 'ffn_post_norm': _jnp.float32}
MOMENT_SCALE = {'attn_pre_norm': 5.518030e-01, 'w_in': 1.812108e-01, 'hgrn_lb': 2.133818e-02, 'hgrn_gnorm': 2.369759e-01, 'w_branch_a': 2.390996e-01, 'rwkv_mu': 3.783520e-01, 'rwkv_w0': 8.433140e-02, 'rwkv_w2': 9.824761e-03, 'rwkv_a0': 8.590817e-02, 'rwkv_a2': 8.446533e-02, 'rwkv_g2': 2.168145e-01, 'rwkv_k_k': 2.695518e-01, 'rwkv_k_a': 2.306311e-01, 'rwkv_r_k': 4.656280e-01, 'rwkv_ln_w': 2.179504e-01, 'rwkv_ln_b': 2.753943e-01, 'w_branch_b': 2.200581e-01, 'w_out': 3.336786e-01, 'attn_post_norm': 1.602002e+01, 'ffn_pre_norm': 3.412231e-01, 'w_up': 1.444947e-01, 'conv_w': 1.475964e-01, 'conv_b': 2.119296e-01, 'w_down': 2.518338e-01, 'ffn_post_norm': 1.602715e+01}


def _to_microbatches(a, axis):
    t = _jnp.moveaxis(a, axis, 0)
    t = t.reshape((N_MICROBATCH, t.shape[0] // N_MICROBATCH) + t.shape[1:])
    return _jnp.moveaxis(t, 1, axis + 1)


def setup_inputs(seed: int = 0) -> dict:
    inp = _fwd_setup_inputs(seed)
    key = _jax.random.fold_in(_jax.random.key(seed), 7919)
    shape, _ = _output_shape()
    out = dict(inp)
    out["loss_target"] = _jax.random.normal(_jax.random.fold_in(key, 0), shape, _jnp.float32)
    for i, name in enumerate(TWIN_WEIGHTS):
        w = inp[name].astype(_jnp.float32)
        if MOMENT_SCALE is None:
            s = _jnp.sqrt(_jnp.mean(_jnp.square(w)) + 1e-30)
        else:
            s = MOMENT_SCALE[name]
        km, kv = _jax.random.split(_jax.random.fold_in(key, i + 1))
        out[name] = w
        out["m_" + name] = s * _jax.random.normal(km, w.shape, _jnp.float32)
        out["v_" + name] = (s * s) * _jax.random.uniform(kv, w.shape, _jnp.float32, 0.5, 1.5)
    if N_MICROBATCH > 1:
        for name, axis in PER_EXAMPLE_BATCH_AXIS.items():
            out[name] = _to_microbatches(out[name], axis)
    return {'x': out['x'], 'attn_pre_norm': out['attn_pre_norm'], 'w_in': out['w_in'], 'hgrn_lb': out['hgrn_lb'], 'hgrn_gnorm': out['hgrn_gnorm'], 'w_branch_a': out['w_branch_a'], 'rwkv_mu': out['rwkv_mu'], 'rwkv_w0': out['rwkv_w0'], 'rwkv_w2': out['rwkv_w2'], 'rwkv_a0': out['rwkv_a0'], 'rwkv_a2': out['rwkv_a2'], 'rwkv_g2': out['rwkv_g2'], 'rwkv_k_k': out['rwkv_k_k'], 'rwkv_k_a': out['rwkv_k_a'], 'rwkv_r_k': out['rwkv_r_k'], 'rwkv_ln_w': out['rwkv_ln_w'], 'rwkv_ln_b': out['rwkv_ln_b'], 'w_branch_b': out['w_branch_b'], 'w_out': out['w_out'], 'attn_post_norm': out['attn_post_norm'], 'ffn_pre_norm': out['ffn_pre_norm'], 'w_up': out['w_up'], 'conv_w': out['conv_w'], 'conv_b': out['conv_b'], 'w_down': out['w_down'], 'ffn_post_norm': out['ffn_post_norm'], 'loss_target': out['loss_target'], 'm_attn_pre_norm': out['m_attn_pre_norm'], 'm_w_in': out['m_w_in'], 'm_hgrn_lb': out['m_hgrn_lb'], 'm_hgrn_gnorm': out['m_hgrn_gnorm'], 'm_w_branch_a': out['m_w_branch_a'], 'm_rwkv_mu': out['m_rwkv_mu'], 'm_rwkv_w0': out['m_rwkv_w0'], 'm_rwkv_w2': out['m_rwkv_w2'], 'm_rwkv_a0': out['m_rwkv_a0'], 'm_rwkv_a2': out['m_rwkv_a2'], 'm_rwkv_g2': out['m_rwkv_g2'], 'm_rwkv_k_k': out['m_rwkv_k_k'], 'm_rwkv_k_a': out['m_rwkv_k_a'], 'm_rwkv_r_k': out['m_rwkv_r_k'], 'm_rwkv_ln_w': out['m_rwkv_ln_w'], 'm_rwkv_ln_b': out['m_rwkv_ln_b'], 'm_w_branch_b': out['m_w_branch_b'], 'm_w_out': out['m_w_out'], 'm_attn_post_norm': out['m_attn_post_norm'], 'm_ffn_pre_norm': out['m_ffn_pre_norm'], 'm_w_up': out['m_w_up'], 'm_conv_w': out['m_conv_w'], 'm_conv_b': out['m_conv_b'], 'm_w_down': out['m_w_down'], 'm_ffn_post_norm': out['m_ffn_post_norm'], 'v_attn_pre_norm': out['v_attn_pre_norm'], 'v_w_in': out['v_w_in'], 'v_hgrn_lb': out['v_hgrn_lb'], 'v_hgrn_gnorm': out['v_hgrn_gnorm'], 'v_w_branch_a': out['v_w_branch_a'], 'v_rwkv_mu': out['v_rwkv_mu'], 'v_rwkv_w0': out['v_rwkv_w0'], 'v_rwkv_w2': out['v_rwkv_w2'], 'v_rwkv_a0': out['v_rwkv_a0'], 'v_rwkv_a2': out['v_rwkv_a2'], 'v_rwkv_g2': out['v_rwkv_g2'], 'v_rwkv_k_k': out['v_rwkv_k_k'], 'v_rwkv_k_a': out['v_rwkv_k_a'], 'v_rwkv_r_k': out['v_rwkv_r_k'], 'v_rwkv_ln_w': out['v_rwkv_ln_w'], 'v_rwkv_ln_b': out['v_rwkv_ln_b'], 'v_w_branch_b': out['v_w_branch_b'], 'v_w_out': out['v_w_out'], 'v_attn_post_norm': out['v_attn_post_norm'], 'v_ffn_pre_norm': out['v_ffn_pre_norm'], 'v_w_up': out['v_w_up'], 'v_conv_w': out['v_conv_w'], 'v_conv_b': out['v_conv_b'], 'v_w_down': out['v_w_down'], 'v_ffn_post_norm': out['v_ffn_post_norm']}


def _loss(weights, diff, rest, loss_target):
    with _jax.named_scope("forward"):
        args = {**rest, TWIN_DIFF_INPUT: diff, **{k: w.astype(_WEIGHT_DTYPES[k]) for k, w in weights.items()}}
        y = _forward(args)
    with _jax.named_scope("loss_head"):
        err = _jnp.square(y.astype(_jnp.float32) - loss_target)
        return 0.5 * _jnp.sum(_jnp.mean(err, axis=-1)) if err.ndim else 0.5 * err


def _adamw(w, g, m, v):
    m = ADAM_B1 * m + (1.0 - ADAM_B1) * g
    v = ADAM_B2 * v + (1.0 - ADAM_B2) * _jnp.square(g)
    m_hat = m / (1.0 - ADAM_B1 ** ADAM_STEP)
    v_hat = v / (1.0 - ADAM_B2 ** ADAM_STEP)
    delta = -ADAM_LR * (m_hat / (_jnp.sqrt(v_hat) + ADAM_EPS) + ADAM_WD * w)
    return delta, m, v


def reference(x, attn_pre_norm, w_in, hgrn_lb, hgrn_gnorm, w_branch_a, rwkv_mu, rwkv_w0, rwkv_w2, rwkv_a0, rwkv_a2, rwkv_g2, rwkv_k_k, rwkv_k_a, rwkv_r_k, rwkv_ln_w, rwkv_ln_b, w_branch_b, w_out, attn_post_norm, ffn_pre_norm, w_up, conv_w, conv_b, w_down, ffn_post_norm, loss_target, m_attn_pre_norm, m_w_in, m_hgrn_lb, m_hgrn_gnorm, m_w_branch_a, m_rwkv_mu, m_rwkv_w0, m_rwkv_w2, m_rwkv_a0, m_rwkv_a2, m_rwkv_g2, m_rwkv_k_k, m_rwkv_k_a, m_rwkv_r_k, m_rwkv_ln_w, m_rwkv_ln_b, m_w_branch_b, m_w_out, m_attn_post_norm, m_ffn_pre_norm, m_w_up, m_conv_w, m_conv_b, m_w_down, m_ffn_post_norm, v_attn_pre_norm, v_w_in, v_hgrn_lb, v_hgrn_gnorm, v_w_branch_a, v_rwkv_mu, v_rwkv_w0, v_rwkv_w2, v_rwkv_a0, v_rwkv_a2, v_rwkv_g2, v_rwkv_k_k, v_rwkv_k_a, v_rwkv_r_k, v_rwkv_ln_w, v_rwkv_ln_b, v_w_branch_b, v_w_out, v_attn_post_norm, v_ffn_pre_norm, v_w_up, v_conv_w, v_conv_b, v_w_down, v_ffn_post_norm):
    given = dict(x=x, attn_pre_norm=attn_pre_norm, w_in=w_in, hgrn_lb=hgrn_lb, hgrn_gnorm=hgrn_gnorm, w_branch_a=w_branch_a, rwkv_mu=rwkv_mu, rwkv_w0=rwkv_w0, rwkv_w2=rwkv_w2, rwkv_a0=rwkv_a0, rwkv_a2=rwkv_a2, rwkv_g2=rwkv_g2, rwkv_k_k=rwkv_k_k, rwkv_k_a=rwkv_k_a, rwkv_r_k=rwkv_r_k, rwkv_ln_w=rwkv_ln_w, rwkv_ln_b=rwkv_ln_b, w_branch_b=w_branch_b, w_out=w_out, attn_post_norm=attn_post_norm, ffn_pre_norm=ffn_pre_norm, w_up=w_up, conv_w=conv_w, conv_b=conv_b, w_down=w_down, ffn_post_norm=ffn_post_norm, loss_target=loss_target, m_attn_pre_norm=m_attn_pre_norm, m_w_in=m_w_in, m_hgrn_lb=m_hgrn_lb, m_hgrn_gnorm=m_hgrn_gnorm, m_w_branch_a=m_w_branch_a, m_rwkv_mu=m_rwkv_mu, m_rwkv_w0=m_rwkv_w0, m_rwkv_w2=m_rwkv_w2, m_rwkv_a0=m_rwkv_a0, m_rwkv_a2=m_rwkv_a2, m_rwkv_g2=m_rwkv_g2, m_rwkv_k_k=m_rwkv_k_k, m_rwkv_k_a=m_rwkv_k_a, m_rwkv_r_k=m_rwkv_r_k, m_rwkv_ln_w=m_rwkv_ln_w, m_rwkv_ln_b=m_rwkv_ln_b, m_w_branch_b=m_w_branch_b, m_w_out=m_w_out, m_attn_post_norm=m_attn_post_norm, m_ffn_pre_norm=m_ffn_pre_norm, m_w_up=m_w_up, m_conv_w=m_conv_w, m_conv_b=m_conv_b, m_w_down=m_w_down, m_ffn_post_norm=m_ffn_post_norm, v_attn_pre_norm=v_attn_pre_norm, v_w_in=v_w_in, v_hgrn_lb=v_hgrn_lb, v_hgrn_gnorm=v_hgrn_gnorm, v_w_branch_a=v_w_branch_a, v_rwkv_mu=v_rwkv_mu, v_rwkv_w0=v_rwkv_w0, v_rwkv_w2=v_rwkv_w2, v_rwkv_a0=v_rwkv_a0, v_rwkv_a2=v_rwkv_a2, v_rwkv_g2=v_rwkv_g2, v_rwkv_k_k=v_rwkv_k_k, v_rwkv_k_a=v_rwkv_k_a, v_rwkv_r_k=v_rwkv_r_k, v_rwkv_ln_w=v_rwkv_ln_w, v_rwkv_ln_b=v_rwkv_ln_b, v_w_branch_b=v_w_branch_b, v_w_out=v_w_out, v_attn_post_norm=v_attn_post_norm, v_ffn_pre_norm=v_ffn_pre_norm, v_w_up=v_w_up, v_conv_w=v_conv_w, v_conv_b=v_conv_b, v_w_down=v_w_down, v_ffn_post_norm=v_ffn_post_norm)
    weights = {n: given[n] for n in TWIN_WEIGHTS}
    shared = {n: given[n] for n in SHARED_INPUTS}
    per_example = {n: given[n] for n in ['x']}
    grad_fn = _jax.value_and_grad(_loss, argnums=(0, 1))

    def one_microbatch(ex, loss_target):
        ex = dict(ex)
        diff = ex.pop(TWIN_DIFF_INPUT)
        return grad_fn(weights, diff, {**shared, **ex}, loss_target)

    if N_MICROBATCH == 1:
        loss, (grad_w, grad_x) = one_microbatch(per_example, given["loss_target"])
    else:
        def body(carry, xs):
            loss_sum, grad_sum = carry
            l_k, (gw_k, gx_k) = one_microbatch(xs[0], xs[1])
            with _jax.named_scope("update"):
                return (loss_sum + l_k, _jax.tree.map(_jnp.add, grad_sum, gw_k)), gx_k

        init = (_jnp.zeros((), _jnp.float32), _jax.tree.map(_jnp.zeros_like, weights))
        (loss, grad_w), grad_x = _jax.lax.scan(body, init, (per_example, given["loss_target"]))
    with _jax.named_scope("update"):
        delta_w, new_m, new_v = {}, {}, {}
        for n in TWIN_WEIGHTS:
            delta_w[n], new_m[n], new_v[n] = _adamw(weights[n], grad_w[n], given["m_" + n], given["v_" + n])
    return (loss, grad_x, *[grad_w[n] for n in TWIN_WEIGHTS], *[delta_w[n] for n in TWIN_WEIGHTS],
            *[new_m[n] for n in TWIN_WEIGHTS], *[new_v[n] for n in TWIN_WEIGHTS])
```

```python
import functools

import jax
import jax.numpy as jnp
from jax import lax
from jax.experimental import pallas as pl
from jax.experimental.pallas import tpu as pltpu

F32, BF16 = jnp.float32, jnp.bfloat16
MESH = pl.DeviceIdType.MESH

D_MODEL = 1024
HGRN_HEADS = 8
HGRN_K = 128
HGRN_SCALE = HGRN_K ** -0.5
CHUNK = 32
RWKV_HEAD = 64
LORA = 256
D_FF = 2816
EPS = 1e-6
GN_EPS = 1e-5 * RWKV_HEAD
N_SHARD = 4
ADAM_LR, ADAM_B1, ADAM_B2, ADAM_EPS, ADAM_WD, ADAM_STEP = 0.001, 0.9, 0.999, 1e-08, 0.01, 10

LANES = 128
VMEM_LIMIT = 56 * 1024 * 1024
SCAN_TB = 8
SCAN_GROUP = 256

C_HQ, C_HF, C_HI, C_HG = 0, 1024, 2048, 3072
C_R = 4096
R_COLS = 3328
C_G = 7424
IN_COLS = 9472


def _params(sem=None, **kw):
    return pltpu.CompilerParams(dimension_semantics=sem, vmem_limit_bytes=VMEM_LIMIT, **kw)


def _seg_matrix(n, seg):
    r = lax.broadcasted_iota(jnp.int32, (n, n), 0) // seg
    c = lax.broadcasted_iota(jnp.int32, (n, n), 1) // seg
    return (r == c).astype(BF16)


def _split3(x):
    hi = x.astype(BF16)
    r1 = x - hi.astype(F32)
    mid = r1.astype(BF16)
    lo = (r1 - mid.astype(F32)).astype(BF16)
    return hi, mid, lo


def _segsum_impl(x, seg):
    e = _seg_matrix(LANES, seg)
    outs = []
    for g in range(x.shape[1] // LANES):
        hi, mid, lo = _split3(x[:, g * LANES:(g + 1) * LANES])
        outs.append(jnp.dot(hi, e, preferred_element_type=F32) + jnp.dot(mid, e, preferred_element_type=F32)
                    + jnp.dot(lo, e, preferred_element_type=F32))
    return outs[0] if len(outs) == 1 else jnp.concatenate(outs, axis=1)


def _make_segsum(seg):
    @jax.custom_vjp
    def f(x):
        return _segsum_impl(x, seg)

    f.defvjp(lambda x: (_segsum_impl(x, seg), None), lambda _, ct: (_segsum_impl(ct, seg),))
    return f


_segsum64 = _make_segsum(RWKV_HEAD)
_segsum128 = _make_segsum(HGRN_K)


@jax.custom_vjp
def _bdot(x, w):
    return jnp.dot(x.astype(BF16), w.astype(BF16), preferred_element_type=F32)


def _bdot_fwd(x, w):
    return _bdot(x, w), (x, w)


def _bdot_bwd(res, ct):
    x, w = res
    ctb = ct.astype(BF16)
    dx = lax.dot_general(ctb, w.astype(BF16), (((1,), (1,)), ((), ())), preferred_element_type=F32)
    dw = lax.dot_general(x.astype(BF16), ctb, (((0,), (0,)), ((), ())), preferred_element_type=F32)
    return dx, dw


_bdot.defvjp(_bdot_fwd, _bdot_bwd)


def _sigmoid(x):
    return 1.0 / (1.0 + jnp.exp(-x))


def _silu(x):
    return x * _sigmoid(x)


def _softplus(x):
    return jnp.maximum(x, 0.0) + jnp.log(1.0 + jnp.exp(-jnp.abs(x)))


def _rms(x, g):
    return x * lax.rsqrt(jnp.mean(x * x, axis=-1, keepdims=True) + EPS) * g


def _fn_norm(t, p):
    return [_rms(t[0], p[0])]


def _fn_hgates(t, p):
    hq, hf = t
    lb2 = p[0]
    m = jnp.max(lb2, axis=0, keepdims=True)
    e = jnp.exp(lb2 - m)
    first = lax.broadcasted_iota(jnp.int32, e.shape, 0) == 0
    lb = jnp.sum(jnp.where(first, e, 0.0), axis=0, keepdims=True) / jnp.sum(e, axis=0, keepdims=True)
    f = lb + (1.0 - lb) * _sigmoid(hf)
    return [_silu(hq) * HGRN_SCALE, jnp.log(f), 1.0 - f]


def _fn_hpost(t, p):
    o, hg = t
    ms = _segsum128(o * o) * (1.0 / HGRN_K)
    return [o * lax.rsqrt(ms + EPS) * p[0] * _silu(hg)]


def _fn_rprep(t, p):
    kr, lora = t
    w0, w2p, a0, a2p, g2p, k_k, k_a = p
    pre_w = w0 + _bdot(jnp.tanh(lora), w2p)
    w_log = -_softplus(-pre_w) - 0.5
    decay = jnp.exp(-jnp.exp(w_log))
    a = _sigmoid(a0 + _bdot(lora, a2p))
    g = _bdot(_sigmoid(lora), g2p)
    kk = kr * k_k
    kk = kk / jnp.maximum(jnp.sqrt(_segsum64(kk * kk)), 1e-12)
    kr2 = kr * (1.0 + (a - 1.0) * k_a)
    return [decay, kr2, -kk, kk * a, g]


def _fn_rpost(t, p):
    y, r, kr2, v, g = t
    ln_w, ln_b, r_k = p
    mu = _segsum64(y) * (1.0 / RWKV_HEAD)
    yc = y - mu
    var = _segsum64(yc * yc) * (1.0 / RWKV_HEAD)
    yn = yc * lax.rsqrt(var + GN_EPS) * ln_w + ln_b
    bonus = _segsum64(r * kr2 * r_k) * v
    return [(yn + bonus) * g]


def _fn_merge(t, p):
    ga, gb, ya, yb = t
    return [_sigmoid(ga) * ya + _sigmoid(gb) * yb]


def _fn_res1(t, p):
    x, mix = t
    h1 = x + _rms(mix, p[0])
    return [h1, _rms(h1, p[1])]


def _tok_call(name, fn, toks, params, outs, red_shapes=(), tm=128, col_grid=1):
    n_t, n_p, n_o = len(toks), len(params), len(outs)
    t_len = toks[0][0].shape[0]

    def body(*refs):
        tv = [r[...].astype(F32) for r in refs[:n_t]]
        pv = [r[...] for r in refs[n_t:n_t + n_p]]
        o, red = fn(tv, pv)
        for ref, val in zip(refs[n_t + n_p:n_t + n_p + n_o], o):
            ref[...] = val.astype(ref.dtype)
        red_refs = refs[n_t + n_p + n_o:]
        if red_refs:
            first = pl.program_id(0) == 0

            @pl.when(first)
            def _():
                for ref, val in zip(red_refs, red):
                    ref[...] = val

            @pl.when(jnp.logical_not(first))
            def _():
                for ref, val in zip(red_refs, red):
                    ref[...] += val

    in_specs = [pl.BlockSpec((tm, w), functools.partial(lambda i, j, c: (i, c + j), c=c)) for (_, w, c) in toks]
    in_specs += [pl.BlockSpec(p.shape, lambda i, j: (0, 0)) for p in params]
    out_specs = [pl.BlockSpec((tm, w), lambda i, j: (i, j)) for (w, _) in outs]
    out_specs += [pl.BlockSpec(s, lambda i, j: (0, 0)) for s in red_shapes]
    out_shape = [jax.ShapeDtypeStruct((t_len, w * col_grid), dt) for (w, dt) in outs]
    out_shape += [jax.ShapeDtypeStruct(s, F32) for s in red_shapes]
    return pl.pallas_call(
        body, name=name, grid=(t_len // tm, col_grid), in_specs=in_specs, out_specs=out_specs, out_shape=out_shape,
        compiler_params=_params(("arbitrary", "arbitrary")),
    )(*[a for (a, _, _) in toks], *params)


def _tok_fwd(name, fn, toks, params, outs, **kw):
    return _tok_call(name, lambda tv, pv: (fn(tv, pv), []), toks, params, outs, **kw)


def _tok_bwd(name, fn, toks, params, cts, want, add_to_first=None, **kw):
    n_t = len(toks)
    flat = [c for group in cts for c in group]
    extra = [] if add_to_first is None else [add_to_first]

    def bwd(tv, pv):
        prim, rest = tv[:n_t], tv[n_t:]
        ct, at = [], 0
        for group in cts:
            ct.append(functools.reduce(lambda u, v: u + v, rest[at:at + len(group)]))
            at += len(group)
        _, vjp = jax.vjp(lambda *a: tuple(fn(list(a[:n_t]), list(a[n_t:]))), *prim, *pv)
        g = vjp(tuple(ct))
        tok_grads = [g[i] for i in range(n_t) if want[i] is not None]
        if extra:
            tok_grads[0] = tok_grads[0] + rest[at]
        return tok_grads, list(g[n_t:])

    return _tok_call(name, bwd, list(toks) + flat + extra, params, [w for w in want if w is not None],
                     red_shapes=[p.shape for p in params], **kw)


def _mm(name, a, b, mode, out_dtype=F32, tm=512, tn=512, tk=None):
    if mode == "nn":
        (m, k), (_, n) = a.shape, b.shape
    elif mode == "nt":
        (m, k), (n, _) = a.shape, b.shape
    else:
        (k, m), (_, n) = a.shape, b.shape
    tk = k if tk is None else tk
    tm, tn = min(tm, m), min(tn, n)
    nk = k // tk
    assert m % tm == 0 and n % tn == 0 and k % tk == 0, (name, a.shape, b.shape, tm, tn, tk)
    a_spec = pl.BlockSpec((tk, tm), lambda i, j, q: (q, i)) if mode == "tn" else pl.BlockSpec((tm, tk), lambda i, j, q: (i, q))
    b_spec = pl.BlockSpec((tn, tk), lambda i, j, q: (j, q)) if mode == "nt" else pl.BlockSpec((tk, tn), lambda i, j, q: (q, j))
    dn = {"nn": (((1,), (0,)), ((), ())), "nt": (((1,), (1,)), ((), ())), "tn": (((0,), (0,)), ((), ()))}[mode]

    def body(a_ref, b_ref, o_ref, *acc):
        p = lax.dot_general(a_ref[...], b_ref[...], dn, preferred_element_type=F32)
        if nk == 1:
            o_ref[...] = p.astype(o_ref.dtype)
        else:
            q = pl.program_id(2)

            @pl.when(q == 0)
            def _():
                acc[0][...] = p

            @pl.when(q > 0)
            def _():
                acc[0][...] += p

            @pl.when(q == nk - 1)
            def _():
                o_ref[...] = acc[0][...].astype(o_ref.dtype)

    return pl.pallas_call(
        body, name=name, grid=(m // tm, n // tn, nk), in_specs=[a_spec, b_spec],
        out_specs=pl.BlockSpec((tm, tn), lambda i, j, q: (i, j)), out_shape=jax.ShapeDtypeStruct((m, n), out_dtype),
        scratch_shapes=[pltpu.VMEM((tm, tn), F32)] if nk > 1 else [],
        compiler_params=_params(("parallel", "parallel", "arbitrary")),
    )(a, b)


def _shift_down(z, n):
    rows = lax.broadcasted_iota(jnp.int32, z.shape, 0)
    return jnp.where(rows < n, 0.0, pltpu.roll(z, n, 0))


def _shift_up(z, n):
    t = z.shape[0]
    rows = lax.broadcasted_iota(jnp.int32, z.shape, 0)
    return jnp.where(rows >= t - n, 0.0, pltpu.roll(z, t - n, 0))


def _lerp_fwd(z, mu):
    t = z.shape[0]
    w = 256

    def body(z_ref, mu_ref, o_ref):
        zz = z_ref[...]
        o_ref[...] = zz + mu_ref[...] * (_shift_down(zz, 1) - zz)

    return pl.pallas_call(
        body, name="lerp_fwd", grid=(R_COLS // w,),
        in_specs=[pl.BlockSpec((t, w), lambda j: (0, C_R // w + j)), pl.BlockSpec((1, w), lambda j: (0, j))],
        out_specs=pl.BlockSpec((t, w), lambda j: (0, j)), out_shape=jax.ShapeDtypeStruct((t, R_COLS), F32),
        compiler_params=_params(("parallel",)),
    )(z, mu)


def _lerp_bwd(z, mu, dzl):
    t = z.shape[0]
    w = 256

    def body(z_ref, mu_ref, d_ref, dz_ref, dmu_ref):
        zz, m, d = z_ref[...], mu_ref[...], d_ref[...]
        dz_ref[...] = (d * (1.0 - m) + _shift_up(d * m, 1)).astype(dz_ref.dtype)
        dmu_ref[...] = jnp.sum(d * (_shift_down(zz, 1) - zz), axis=0, keepdims=True)

    return pl.pallas_call(
        body, name="lerp_bwd", grid=(R_COLS // w,),
        in_specs=[pl.BlockSpec((t, w), lambda j: (0, C_R // w + j)), pl.BlockSpec((1, w), lambda j: (0, j)),
                  pl.BlockSpec((t, w), lambda j: (0, j))],
        out_specs=[pl.BlockSpec((t, w), lambda j: (0, j)), pl.BlockSpec((1, w), lambda j: (0, j))],
        out_shape=[jax.ShapeDtypeStruct((t, R_COLS), BF16), jax.ShapeDtypeStruct((1, R_COLS), F32)],
        compiler_params=_params(("parallel",)),
    )(z, mu, dzl)


CONV_TILE = 256
N_CONV_TILES = D_FF // CONV_TILE


def _conv(h, w, b):
    return b + w[0:1, :] * _shift_down(h, 2) + w[1:2, :] * _shift_down(h, 1) + w[2:3, :] * h


def _conv_fwd(hu, conv_w, conv_b):
    t = hu.shape[0]
    n = N_CONV_TILES

    def body(hg_ref, hv_ref, wg_ref, wv_ref, bg_ref, bv_ref, o_ref):
        gate = _conv(hg_ref[...], wg_ref[...], bg_ref[...])
        val = _conv(hv_ref[...], wv_ref[...], bv_ref[...])
        o_ref[...] = (_silu(gate) * val).astype(o_ref.dtype)

    col = lambda off: pl.BlockSpec((t, CONV_TILE), lambda j: (0, j + off))
    wspec = lambda off: pl.BlockSpec((3, CONV_TILE), lambda j: (0, j + off))
    bspec = lambda off: pl.BlockSpec((1, CONV_TILE), lambda j: (0, j + off))
    return pl.pallas_call(
        body, name="conv_fwd", grid=(n,),
        in_specs=[col(0), col(n), wspec(0), wspec(n), bspec(0), bspec(n)],
        out_specs=pl.BlockSpec((t, CONV_TILE), lambda j: (0, j)), out_shape=jax.ShapeDtypeStruct((t, D_FF), BF16),
        compiler_params=_params(("parallel",)),
    )(hu, hu, conv_w, conv_w, conv_b, conv_b)


def _conv_bwd(hu, conv_w, conv_b, d_act):
    t = hu.shape[0]
    n = N_CONV_TILES

    def body(hg_ref, hv_ref, wg_ref, wv_ref, bg_ref, bv_ref, d_ref, dh_ref, dw_ref, db_ref):
        is_gate = pl.program_id(0) < n
        hg, hv = hg_ref[...], hv_ref[...]
        gate = _conv(hg, wg_ref[...], bg_ref[...])
        val = _conv(hv, wv_ref[...], bv_ref[...])
        d = d_ref[...]
        sg = _sigmoid(gate)
        d_gate = d * val * (sg * (1.0 + gate * (1.0 - sg)))
        d_val = d * (gate * sg)
        dc = jnp.where(is_gate, d_gate, d_val)
        h = jnp.where(is_gate, hg, hv)
        w = jnp.where(is_gate, wg_ref[...], wv_ref[...])
        dh = w[2:3, :] * dc + w[1:2, :] * _shift_up(dc, 1) + w[0:1, :] * _shift_up(dc, 2)
        dh_ref[...] = dh.astype(dh_ref.dtype)
        dw_ref[0:1, :] = jnp.sum(dc * _shift_down(h, 2), axis=0, keepdims=True)
        dw_ref[1:2, :] = jnp.sum(dc * _shift_down(h, 1), axis=0, keepdims=True)
        dw_ref[2:3, :] = jnp.sum(dc * h, axis=0, keepdims=True)
        db_ref[...] = jnp.sum(dc, axis=0, keepdims=True)

    gcol = lambda rows: pl.BlockSpec((rows, CONV_TILE), lambda j: (0, j % n))
    vcol = lambda rows: pl.BlockSpec((rows, CONV_TILE), lambda j: (0, j % n + n))
    own = lambda rows: pl.BlockSpec((rows, CONV_TILE), lambda j: (0, j))
    return pl.pallas_call(
        body, name="conv_bwd", grid=(2 * n,),
        in_specs=[gcol(t), vcol(t), gcol(3), vcol(3), gcol(1), vcol(1), gcol(t)],
        out_specs=[own(t), own(3), own(1)],
        out_shape=[jax.ShapeDtypeStruct((t, 2 * D_FF), BF16), jax.ShapeDtypeStruct((3, 2 * D_FF), F32),
                   jax.ShapeDtypeStruct((1, 2 * D_FF), F32)],
        compiler_params=_params(("parallel",)),
    )(hu, hu, conv_w, conv_w, conv_b, conv_b, d_act)


_HI = lax.Precision.HIGHEST


def _dot_hi(a, b, dn):
    return lax.dot_general(a, b, dn, precision=_HI, preferred_element_type=F32)


_NN = (((1,), (0,)), ((), ()))
_NT = (((1,), (1,)), ((), ()))
_TN = (((0,), (0,)), ((), ()))


def _hgrn_chunk(qc, gc, kc, vc, st):
    r = lax.broadcasted_iota(jnp.int32, (CHUNK, CHUNK), 0)
    c = lax.broadcasted_iota(jnp.int32, (CHUNK, CHUNK), 1)
    tril = r >= c
    b = _dot_hi(tril.astype(F32), gc, _NN)
    rows = lax.broadcasted_iota(jnp.int32, b.shape, 0)
    b_ref = jnp.sum(jnp.where(rows == CHUNK // 2 - 1, b, 0.0), axis=0, keepdims=True)
    b_last = jnp.sum(jnp.where(rows == CHUNK - 1, b, 0.0), axis=0, keepdims=True)
    q_in = qc * jnp.exp(b - b_ref)
    k_in = kc * jnp.exp(b_ref - b)
    scores = jnp.where(tril, _dot_hi(q_in, k_in, _NT), 0.0)
    o = _dot_hi(scores, vc, _NN) + _dot_hi(qc * jnp.exp(b), st, _NT)
    u_t = _dot_hi(vc, kc * jnp.exp(b_last - b), _TN)
    return o, st * jnp.exp(b_last) + u_t


def _hgrn_fwd(q, logf, kh, z):
    t = q.shape[0]
    nc = t // CHUNK

    def body(q_ref, g_ref, k_ref, v_ref, o_ref, st_ref):
        def chunk(n, st):
            rows = pl.ds(pl.multiple_of(n * CHUNK, CHUNK), CHUNK)
            st_ref[0, n] = st
            o, st_new = _hgrn_chunk(q_ref[rows, :], g_ref[rows, :], k_ref[rows, :], v_ref[rows, :], st)
            o_ref[rows, :] = o
            return st_new

        lax.fori_loop(0, nc, chunk, jnp.zeros((HGRN_K, HGRN_K), F32))

    head = pl.BlockSpec((t, HGRN_K), lambda h: (0, h))
    return pl.pallas_call(
        body, name="hgrn_fwd", grid=(HGRN_HEADS,),
        in_specs=[head, head, head, pl.BlockSpec((t, HGRN_K), lambda h: (0, C_HI // HGRN_K + h))],
        out_specs=[head, pl.BlockSpec((1, nc, HGRN_K, HGRN_K), lambda h: (h, 0, 0, 0))],
        out_shape=[jax.ShapeDtypeStruct((t, D_MODEL), F32), jax.ShapeDtypeStruct((HGRN_HEADS, nc, HGRN_K, HGRN_K), F32)],
        compiler_params=_params(("parallel",)),
    )(q, logf, kh, z)


def _hgrn_bwd(q, logf, kh, z, states, d_o):
    t = q.shape[0]
    nc = t // CHUNK

    def body(q_ref, g_ref, k_ref, v_ref, st_ref, do_ref, dq_ref, dg_ref, dk_ref, dv_ref):
        def chunk(i, d_st):
            n = nc - 1 - i
            rows = pl.ds(pl.multiple_of(n * CHUNK, CHUNK), CHUNK)
            _, vjp = jax.vjp(_hgrn_chunk, q_ref[rows, :], g_ref[rows, :], k_ref[rows, :], v_ref[rows, :], st_ref[0, n])
            dq, dg, dk, dv, d_prev = vjp((do_ref[rows, :], d_st))
            dq_ref[rows, :] = dq
            dg_ref[rows, :] = dg
            dk_ref[rows, :] = dk
            dv_ref[rows, :] = dv
            return d_prev

        lax.fori_loop(0, nc, chunk, jnp.zeros((HGRN_K, HGRN_K), F32))

    head = pl.BlockSpec((t, HGRN_K), lambda h: (0, h))
    out = jax.ShapeDtypeStruct((t, D_MODEL), F32)
    return pl.pallas_call(
        body, name="hgrn_bwd", grid=(HGRN_HEADS,),
        in_specs=[head, head, head, pl.BlockSpec((t, HGRN_K), lambda h: (0, C_HI // HGRN_K + h)),
                  pl.BlockSpec((1, nc, HGRN_K, HGRN_K), lambda h: (h, 0, 0, 0)), head],
        out_specs=[head, head, head, head], out_shape=[out, out, out, out],
        compiler_params=_params(("parallel",)),
    )(q, logf, kh, z, states, d_o)


def _split2(x):
    hi = x.astype(BF16)
    return hi, (x - hi.astype(F32)).astype(BF16)


def _seg_bcast(x, e):
    hi, lo = _split2(x)
    return jnp.dot(hi, e, preferred_element_type=F32) + jnp.dot(lo, e, preferred_element_type=F32)


def _row_to_col(row, diag, e):
    hi, lo = _split2(row)
    zero = jnp.zeros((), BF16)
    return (jnp.dot(jnp.where(diag, hi, zero), e, preferred_element_type=F32)
            + jnp.dot(jnp.where(diag, lo, zero), e, preferred_element_type=F32))


def _col_to_row(col, diag):
    return jnp.sum(jnp.where(diag, col, 0.0), axis=0, keepdims=True)


def _scan_consts():
    e = _seg_matrix(SCAN_GROUP, RWKV_HEAD)
    i = lax.broadcasted_iota(jnp.int32, (RWKV_HEAD, SCAN_GROUP), 0)
    l = lax.broadcasted_iota(jnp.int32, (RWKV_HEAD, SCAN_GROUP), 1)
    return e, (l % RWKV_HEAD) == i


def _rwkv_fwd(zl, w, k, a, b):
    t = zl.shape[0]
    ng = D_MODEL // SCAN_GROUP

    def body(r_ref, w_ref, k_ref, v_ref, a_ref, b_ref, y_ref, st_ref, s_ref):
        @pl.when(pl.program_id(0) == 0)
        def _():
            s_ref[...] = jnp.zeros_like(s_ref)

        e, diag = _scan_consts()
        for g in range(ng):
            sl = slice(g * SCAN_GROUP, (g + 1) * SCAN_GROUP)
            s = s_ref[:, sl]
            for i in range(SCAN_TB):
                row = lambda ref: ref[i:i + 1, sl]
                sa = _seg_bcast(s * row(a_ref), e)
                vc = _row_to_col(row(v_ref), diag, e)
                s = s * row(w_ref) + sa * row(b_ref) + vc * row(k_ref)
                st_ref[i, :, sl] = s
                y_ref[i:i + 1, sl] = _col_to_row(_seg_bcast(s * row(r_ref), e), diag)
            s_ref[:, sl] = s

    blk = pl.BlockSpec((SCAN_TB, D_MODEL), lambda n: (n, 0))
    v_blk = pl.BlockSpec((SCAN_TB, D_MODEL), lambda n: (n, 2))
    return pl.pallas_call(
        body, name="rwkv_fwd", grid=(t // SCAN_TB,), in_specs=[blk, blk, blk, v_blk, blk, blk],
        out_specs=[blk, pl.BlockSpec((SCAN_TB, RWKV_HEAD, D_MODEL), lambda n: (n, 0, 0))],
        out_shape=[jax.ShapeDtypeStruct((t, D_MODEL), F32), jax.ShapeDtypeStruct((t, RWKV_HEAD, D_MODEL), F32)],
        scratch_shapes=[pltpu.VMEM((RWKV_HEAD, D_MODEL), F32)],
        compiler_params=_params(("arbitrary",)),
    )(zl, w, k, zl, a, b)


def _rwkv_bwd(zl, w, k, a, b, states, d_y):
    t = zl.shape[0]
    nb = t // SCAN_TB
    ng = D_MODEL // SCAN_GROUP

    def body(r_ref, w_ref, k_ref, v_ref, a_ref, b_ref, st_ref, prev_ref, dy_ref,
             dr_ref, dw_ref, dk_ref, dv_ref, da_ref, db_ref, ds_ref):
        @pl.when(pl.program_id(0) == 0)
        def _():
            ds_ref[...] = jnp.zeros_like(ds_ref)

        has_prev = (pl.program_id(0) < nb - 1).astype(F32)
        e, diag = _scan_consts()
        for g in range(ng):
            sl = slice(g * SCAN_GROUP, (g + 1) * SCAN_GROUP)
            ds = ds_ref[:, sl]
            for i in range(SCAN_TB - 1, -1, -1):
                row = lambda ref: ref[i:i + 1, sl]
                s_t = st_ref[i, :, sl]
                s_prev = st_ref[i - 1, :, sl] if i > 0 else prev_ref[0, :, sl] * has_prev
                dy_col = _row_to_col(row(dy_ref), diag, e)
                ds = ds + dy_col * row(r_ref)
                dr_ref[i:i + 1, sl] = jnp.sum(s_t * dy_col, axis=0, keepdims=True)
                sa = _seg_bcast(s_prev * row(a_ref), e)
                vc = _row_to_col(row(v_ref), diag, e)
                dw_ref[i:i + 1, sl] = jnp.sum(ds * s_prev, axis=0, keepdims=True)
                db_ref[i:i + 1, sl] = jnp.sum(ds * sa, axis=0, keepdims=True)
                dk_ref[i:i + 1, sl] = jnp.sum(ds * vc, axis=0, keepdims=True)
                dv_ref[i:i + 1, sl] = _col_to_row(_seg_bcast(ds * row(k_ref), e), diag)
                dsa = _seg_bcast(ds * row(b_ref), e)
                da_ref[i:i + 1, sl] = jnp.sum(s_prev * dsa, axis=0, keepdims=True)
                ds = ds * row(w_ref) + dsa * row(a_ref)
            ds_ref[:, sl] = ds

    blk = pl.BlockSpec((SCAN_TB, D_MODEL), lambda n: (nb - 1 - n, 0))
    v_blk = pl.BlockSpec((SCAN_TB, D_MODEL), lambda n: (nb - 1 - n, 2))
    out = jax.ShapeDtypeStruct((t, D_MODEL), F32)
    return pl.pallas_call(
        body, name="rwkv_bwd", grid=(nb,),
        in_specs=[blk, blk, blk, v_blk, blk, blk] + [
            pl.BlockSpec((SCAN_TB, RWKV_HEAD, D_MODEL), lambda n: (nb - 1 - n, 0, 0)),
            pl.BlockSpec((1, RWKV_HEAD, D_MODEL), lambda n: (jnp.maximum((nb - 1 - n) * SCAN_TB - 1, 0), 0, 0)),
            blk],
        out_specs=[blk] * 6, out_shape=[out] * 6,
        scratch_shapes=[pltpu.VMEM((RWKV_HEAD, D_MODEL), F32)],
        compiler_params=_params(("arbitrary",)),
    )(zl, w, k, zl, a, b, states, states, d_y)


def _loss_head(h1, ff, target, g_post):
    def fn(tv, pv):
        a, f, tgt = tv
        h2, vjp = jax.vjp(lambda a_, f_, g_: a_ + _rms(f_, g_), a, f, pv[0])
        err = h2 - tgt
        loss = 0.5 * jnp.sum(jnp.mean(err * err, axis=-1, keepdims=True), axis=0, keepdims=True)
        d_a, d_f, d_g = vjp(err * (1.0 / D_MODEL))
        return [d_a, d_f], [loss, d_g]

    return _tok_call("loss_head", fn, [(h1, D_MODEL, 0), (ff, D_MODEL, 0), (target, D_MODEL, 0)], [g_post],
                     [(D_MODEL, F32), (D_MODEL, BF16)], red_shapes=[(1, 1), (1, D_MODEL)])


def _sum_call(name, terms, rows_per_block=None):
    a0, i0 = terms[0]
    r, c = a0.shape[-2:]
    tr = rows_per_block or r

    def body(*refs):
        acc = refs[0][...].reshape(tr, c)
        for ref in refs[1:-1]:
            acc = acc + ref[...].reshape(tr, c)
        refs[-1][...] = acc

    def spec(arr, idx):
        if arr.ndim == 2:
            return pl.BlockSpec((tr, c), lambda i: (i, 0))
        return pl.BlockSpec((1, tr, c), functools.partial(lambda i, idx: (idx, i, 0), idx=idx))

    return pl.pallas_call(
        body, name=name, grid=(r // tr,), in_specs=[spec(a, i) for a, i in terms],
        out_specs=pl.BlockSpec((tr, c), lambda i: (i, 0)), out_shape=jax.ShapeDtypeStruct((r, c), F32),
        compiler_params=_params(("parallel",)),
    )(*[a for a, _ in terms])


def _adamw_math(w, g, m, v):
    m2 = ADAM_B1 * m + (1.0 - ADAM_B1) * g
    v2 = ADAM_B2 * v + (1.0 - ADAM_B2) * (g * g)
    m_hat = m2 / (1.0 - ADAM_B1 ** ADAM_STEP)
    v_hat = v2 / (1.0 - ADAM_B2 ** ADAM_STEP)
    return -ADAM_LR * (m_hat / (jnp.sqrt(v_hat) + ADAM_EPS) + ADAM_WD * w), m2, v2


def _adamw(name, w, g, m, v, bm, bn, g_transposed=False):
    r, c = w.shape

    def body(w_ref, g_ref, m_ref, v_ref, go_ref, d_ref, mo_ref, vo_ref):
        g = g_ref[...].T if g_transposed else g_ref[...]
        d, m2, v2 = _adamw_math(w_ref[...], g, m_ref[...], v_ref[...])
        go_ref[...] = g
        d_ref[...] = d
        mo_ref[...] = m2
        vo_ref[...] = v2

    blk = pl.BlockSpec((bm, bn), lambda i, j: (i, j))
    g_blk = pl.BlockSpec((bn, bm), lambda i, j: (j, i)) if g_transposed else blk
    out = jax.ShapeDtypeStruct((r, c), F32)
    return pl.pallas_call(
        body, name=name, grid=(pl.cdiv(r, bm), pl.cdiv(c, bn)), in_specs=[blk, g_blk, blk, blk],
        out_specs=[blk] * 4, out_shape=[out] * 4, compiler_params=_params(("parallel", "parallel")),
    )(w, g, m, v)


ANY = pl.BlockSpec(memory_space=pl.ANY)


def _place():
    x, y, c = lax.axis_index("x"), lax.axis_index("y"), lax.axis_index("c")
    chips = [(1 - x, y), (x, 1 - y), (1 - x, 1 - y)]
    return x, y, c, chips


def _sibling():
    return (lax.axis_index("x"), lax.axis_index("y"), 1 - lax.axis_index("c"))


def _wait_all(local, remote):
    for cp in local:
        cp.wait()
    for cp in remote:
        cp.wait_send()


def _gather_shards(arrs):
    n = len(arrs)

    def body(*refs):
        ins, outs = refs[:n], refs[n:2 * n]
        send_sems, recv_sems, local_sems = refs[2 * n:]
        x, y, c, chips = _place()
        me, sibling = (x, y, c), _sibling()
        local, remote = [], []
        for k in range(n):
            r = ins[k].shape[0]
            h = r // 2

            def rows(px, py, pc, k=k, r=r, h=h):
                return outs[k].at[pl.ds((2 * px + py) * r + pc * h, h), :]

            def copy(j, block, to, src=None, k=k, rows=rows):
                return pltpu.make_async_remote_copy(
                    src_ref=rows(*block) if src is None else src, dst_ref=rows(*block),
                    send_sem=send_sems.at[k, j], recv_sem=recv_sems.at[k, j], device_id=to, device_id_type=MESH)

            mine = pltpu.make_async_copy(ins[k], outs[k].at[pl.ds((2 * x + y) * r, r), :], local_sems.at[k])
            mine.start()
            local.append(mine)
            first = [copy(j, me, (*chip, c), src=ins[k].at[pl.ds(c * h, h), :]) for j, chip in enumerate(chips)]
            for cp in first:
                cp.start()
            passed = [copy(3 + j, (*chip, c), sibling) for j, chip in enumerate(chips)]
            for j, chip in enumerate(chips):
                copy(j, (*chip, c), me).wait_recv()
                passed[j].start()
            for j, chip in enumerate(chips):
                copy(3 + j, (*chip, 1 - c), me).wait_recv()
            remote += first + passed
        _wait_all(local, remote)

    return pl.pallas_call(
        body, name="gather_shards", in_specs=[ANY] * n, out_specs=[ANY] * n,
        out_shape=[jax.ShapeDtypeStruct((N_SHARD * a.shape[0], a.shape[1]), a.dtype) for a in arrs],
        scratch_shapes=[pltpu.SemaphoreType.DMA((n, 6)), pltpu.SemaphoreType.DMA((n, 6)), pltpu.SemaphoreType.DMA((n,))],
    )(*arrs)


def _allgather8(slab):
    r, cols = slab.shape

    def body(in_ref, out_ref, send_sems, recv_sems, local_sem):
        x, y, c, _ = _place()
        mine = pltpu.make_async_copy(in_ref, out_ref.at[4 * x + 2 * y + c], local_sem)
        mine.start()
        sends = []
        for rel in range(1, 8):
            dx, dy, dc = rel >> 2 & 1, rel >> 1 & 1, rel & 1
            peer = (x ^ dx, y ^ dy, c ^ dc)
            cp = pltpu.make_async_remote_copy(
                src_ref=in_ref, dst_ref=out_ref.at[4 * x + 2 * y + c], send_sem=send_sems.at[rel - 1],
                recv_sem=recv_sems.at[rel - 1], device_id=peer, device_id_type=MESH)
            cp.start()
            sends.append(cp)
        for rel in range(1, 8):
            dx, dy, dc = rel >> 2 & 1, rel >> 1 & 1, rel & 1
            pltpu.make_async_remote_copy(
                src_ref=in_ref, dst_ref=out_ref.at[4 * (x ^ dx) + 2 * (y ^ dy) + (c ^ dc)], send_sem=send_sems.at[rel - 1],
                recv_sem=recv_sems.at[rel - 1], device_id=(x, y, c), device_id_type=MESH).wait_recv()
        _wait_all([mine], sends)

    return pl.pallas_call(
        body, name="allgather8", in_specs=[ANY], out_specs=ANY, out_shape=jax.ShapeDtypeStruct((8, r, cols), slab.dtype),
        scratch_shapes=[pltpu.SemaphoreType.DMA((7,)), pltpu.SemaphoreType.DMA((7,)), pltpu.SemaphoreType.DMA],
    )(slab)


def _rs_sibling(grads):
    n = len(grads)

    def body(*refs):
        ins, outs = refs[:n], refs[n:2 * n]
        send_sems, recv_sems = refs[2 * n:]
        c = lax.axis_index("c")
        sends = []
        for k in range(n):
            for s in range(N_SHARD):
                cp = pltpu.make_async_remote_copy(
                    src_ref=ins[k].at[2 * s + 1 - c], dst_ref=outs[k].at[s], send_sem=send_sems.at[k, s],
                    recv_sem=recv_sems.at[k, s], device_id=_sibling(), device_id_type=MESH)
                cp.start()
                sends.append(cp)
        for cp in sends:
            cp.wait_recv()
        _wait_all([], sends)

    return pl.pallas_call(
        body, name="rs_sibling", in_specs=[ANY] * n, out_specs=[ANY] * n,
        out_shape=[jax.ShapeDtypeStruct((N_SHARD,) + a.shape[1:], a.dtype) for a in grads],
        scratch_shapes=[pltpu.SemaphoreType.DMA((n, N_SHARD)), pltpu.SemaphoreType.DMA((n, N_SHARD))],
    )(*grads)


def _rs_chips(parts):
    n = len(parts)

    def body(*refs):
        ins, outs = refs[:n], refs[n:2 * n]
        send_sems, recv_sems = refs[2 * n:]
        x, y, c, chips = _place()
        sends = []
        for k in range(n):
            for j, (px, py) in enumerate(chips):
                cp = pltpu.make_async_remote_copy(
                    src_ref=ins[k].at[2 * px + py], dst_ref=outs[k].at[j], send_sem=send_sems.at[k, j],
                    recv_sem=recv_sems.at[k, j], device_id=(px, py, c), device_id_type=MESH)
                cp.start()
                sends.append(cp)
        for cp in sends:
            cp.wait_recv()
        _wait_all([], sends)

    return pl.pallas_call(
        body, name="rs_chips", in_specs=[ANY] * n, out_specs=[ANY] * n,
        out_shape=[jax.ShapeDtypeStruct((3,) + a.shape[1:], a.dtype) for a in parts],
        scratch_shapes=[pltpu.SemaphoreType.DMA((n, 3)), pltpu.SemaphoreType.DMA((n, 3))],
    )(*parts)


def _rs_finish(halves):
    n = len(halves)

    def body(*refs):
        ins, outs = refs[:n], refs[n:2 * n]
        send_sems, recv_sems, local_sems = refs[2 * n:]
        c = lax.axis_index("c")
        local, sends = [], []
        for k in range(n):
            mine = pltpu.make_async_copy(ins[k], outs[k].at[c], local_sems.at[k])
            mine.start()
            local.append(mine)
            cp = pltpu.make_async_remote_copy(
                src_ref=ins[k], dst_ref=outs[k].at[c], send_sem=send_sems.at[k], recv_sem=recv_sems.at[k],
                device_id=_sibling(), device_id_type=MESH)
            cp.start()
            sends.append(cp)
        for k in range(n):
            pltpu.make_async_remote_copy(
                src_ref=ins[k], dst_ref=outs[k].at[1 - c], send_sem=send_sems.at[k], recv_sem=recv_sems.at[k],
                device_id=_sibling(), device_id_type=MESH).wait_recv()
        _wait_all(local, sends)

    return pl.pallas_call(
        body, name="rs_finish", in_specs=[ANY] * n, out_specs=[ANY] * n,
        out_shape=[jax.ShapeDtypeStruct((2,) + a.shape, a.dtype) for a in halves],
        scratch_shapes=[pltpu.SemaphoreType.DMA((n,)), pltpu.SemaphoreType.DMA((n,)), pltpu.SemaphoreType.DMA((n,))],
    )(*halves)


def _sum3d(name, terms, lead, scalars, out_lead=None, tr=None):
    h, c = terms[0][0].shape[1:]
    tr = tr or h
    n_lead = 1 if lead is None else lead

    def body(s_ref, *refs):
        acc = refs[0][...]
        for ref in refs[1:-1]:
            acc = acc + ref[...]
        refs[-1][...] = acc.reshape(refs[-1].shape)

    in_specs = [pl.BlockSpec((1, tr, c), functools.partial(lambda l, i, s_ref, f: (f(l, s_ref), i, 0), f=f)) for _, f in terms]
    if lead is None:
        out_spec = pl.BlockSpec((tr, c), lambda l, i, s_ref: (i, 0))
        out_shape = jax.ShapeDtypeStruct((h, c), F32)
    else:
        out_spec = pl.BlockSpec((1, tr, c), lambda l, i, s_ref: (l, i, 0))
        out_shape = jax.ShapeDtypeStruct((lead, h, c), F32)
    return pl.pallas_call(
        body, name=name,
        grid_spec=pltpu.PrefetchScalarGridSpec(num_scalar_prefetch=1, grid=(n_lead, h // tr), in_specs=in_specs, out_specs=out_spec),
        out_shape=out_shape, compiler_params=_params(("parallel", "parallel")),
    )(scalars, *[a for a, _ in terms])


def _reduce_scatter(grads, place):
    g8 = [a.reshape(2 * N_SHARD, a.shape[0] // (2 * N_SHARD), a.shape[1]) for a in grads]
    from_sibling = _rs_sibling(g8)
    parts = [_sum3d(f"rs_add1_{k}", [(g8[k], lambda l, s: 2 * l + s[0]), (from_sibling[k], lambda l, s: l)], N_SHARD, place,
                    tr=g8[k].shape[1] // 2)
             for k in range(len(grads))]
    from_chips = _rs_chips(parts)
    halves = [_sum3d(f"rs_add2_{k}", [(parts[k], lambda l, s: s[1])] + [(from_chips[k], functools.partial(lambda l, s, j: j, j=j))
                                                                    for j in range(3)], None, place, tr=parts[k].shape[1] // 2)
              for k in range(len(grads))]
    whole = _rs_finish(halves)
    return [w.reshape(2 * w.shape[1], w.shape[2]) for w in whole]


def _device_step(x, target, p):
    t = x.shape[0]
    d = D_MODEL
    tok = lambda arr, c=0, w=d: (arr, w, c)
    f32x = lambda n: [(d, F32)] * n
    rp_params = [p["w0"], p["w2p"], p["a0"], p["a2p"], p["g2p"], p["k_k"], p["k_a"]]
    post_params = [p["ln_w"], p["ln_b"], p["r_k"]]
    g = {}

    (xn,) = _tok_fwd("norm1_fwd", _fn_norm, [tok(x)], [p["g1"]], [(d, BF16)])
    z = _mm("in_proj", xn, p["w_in_t"], "nt", tm=t, tn=256)
    q, logf, kh = _tok_fwd("hgates_fwd", _fn_hgates, [tok(z, 0), tok(z, 1)], [p["lb2"]], f32x(3))
    o_raw, h_states = _hgrn_fwd(q, logf, kh, z)
    zl = _lerp_fwd(z, p["mu"])
    lora = tok(zl, 3 * d // LORA, LORA)
    decay, kr2, avec, bvec, gate = _tok_fwd("rprep_fwd", _fn_rprep, [tok(zl, 1), lora], rp_params, f32x(5))
    y, r_states = _rwkv_fwd(zl, decay, kr2, avec, bvec)
    (o_a,) = _tok_fwd("hpost_fwd", _fn_hpost, [tok(o_raw), tok(z, 3)], [p["gnorm"]], [(d, BF16)])
    post_toks = [tok(y), tok(zl, 0), tok(kr2), tok(zl, 2), tok(gate)]
    (o_b,) = _tok_fwd("rpost_fwd", _fn_rpost, post_toks, post_params, [(d, BF16)])
    y_a = _mm("branch_a", o_a, p["w_a"], "nn")
    y_b = _mm("branch_b", o_b, p["w_b"], "nn")
    merge_toks = [tok(z, C_G // 256, 256), tok(z, (C_G + d) // 256, 256), tok(y_a, 0, 256), tok(y_b, 0, 256)]
    (merged,) = _tok_fwd("merge_fwd", _fn_merge, merge_toks, [], [(256, BF16)], col_grid=4)
    mix = _mm("out_proj", merged, p["w_out"], "nn")
    h1, xn2 = _tok_fwd("res1_fwd", _fn_res1, [tok(x), tok(mix)], [p["g_post1"], p["g_pre2"]], [(d, F32), (d, BF16)])
    hu = _mm("up_proj", xn2, p["w_up_t"], "nt", tm=t, tn=512)
    act = _conv_fwd(hu, p["conv_w"], p["conv_b"])
    ff = _mm("down_proj", act, p["w_down"], "nn")
    d_h1, d_ff, loss, g["g_post2"] = _loss_head(h1, ff, target, p["g_post2"])

    d_act = _mm("d_act", d_ff, p["w_down"], "nt", tn=256)
    g["w_down"] = _mm("dw_down", act, d_ff, "tn", tm=256, tn=1024)
    d_hu, g["conv_w"], g["conv_b"] = _conv_bwd(hu, p["conv_w"], p["conv_b"], d_act)
    d_xn2 = _mm("d_xn2", d_hu, p["w_up_t"], "nn", tk=D_FF)
    g["w_up_t"] = _mm("dw_up", d_hu, xn2, "tn", tm=512, tn=1024)
    d_x_res, d_mix, g["g_post1"], g["g_pre2"] = _tok_bwd(
        "res1_bwd", _fn_res1, [tok(x), tok(mix)], [p["g_post1"], p["g_pre2"]], [[tok(d_h1)], [tok(d_xn2)]],
        [(d, F32), (d, BF16)])
    d_merged = _mm("d_merged", d_mix, p["w_out"], "nt")
    g["w_out"] = _mm("dw_out", merged, d_mix, "tn")
    d_ga, d_gb, d_ya, d_yb = _tok_bwd("merge_bwd", _fn_merge, merge_toks, [], [[tok(d_merged, 0, 256)]],
                                      [(256, BF16)] * 4, col_grid=4)
    d_oa = _mm("d_oa", d_ya, p["w_a"], "nt")
    g["w_a"] = _mm("dw_a", o_a, d_ya, "tn")
    d_ob = _mm("d_ob", d_yb, p["w_b"], "nt")
    g["w_b"] = _mm("dw_b", o_b, d_yb, "tn")
    d_oraw, d_hg, g["gnorm"] = _tok_bwd("hpost_bwd", _fn_hpost, [tok(o_raw), tok(z, 3)], [p["gnorm"]], [[tok(d_oa)]],
                                        [(d, F32), (d, BF16)])
    d_y, d_r1, d_kr2_1, d_v1, d_gate, g["ln_w"], g["ln_b"], g["r_k"] = _tok_bwd(
        "rpost_bwd", _fn_rpost, post_toks, post_params, [[tok(d_ob)]], f32x(5))
    d_r2, d_decay, d_kr2_2, d_v2, d_avec, d_bvec = _rwkv_bwd(zl, decay, kr2, avec, bvec, r_states, d_y)
    prep = _tok_bwd("rprep_bwd", _fn_rprep, [tok(zl, 1), lora], rp_params,
                    [[tok(d_decay)], [tok(d_kr2_1), tok(d_kr2_2)], [tok(d_avec)], [tok(d_bvec)], [tok(d_gate)]],
                    [(d, F32), (LORA, F32)])
    d_kr, d_lora = prep[:2]
    g["w0"], g["w2p"], g["a0"], g["a2p"], g["g2p"], g["k_k"], g["k_a"] = prep[2:]
    d_zl = jnp.concatenate([d_r1 + d_r2, d_kr, d_v1 + d_v2, d_lora], axis=1)
    dz_r, g["mu"] = _lerp_bwd(z, p["mu"], d_zl)
    d_q, d_logf, d_kh, d_vi = _hgrn_bwd(q, logf, kh, z, h_states, d_oraw)
    d_hq, d_hf, g["lb2"] = _tok_bwd("hgates_bwd", _fn_hgates, [tok(z, 0), tok(z, 1)], [p["lb2"]],
                                    [[tok(d_q)], [tok(d_logf)], [tok(d_kh)]], [(d, BF16)] * 2)
    dz = jnp.concatenate([d_hq, d_hf, d_vi.astype(BF16), d_hg, dz_r, d_ga, d_gb], axis=1)
    d_xn = _mm("d_xn", dz, p["w_in_t"], "nn", tk=IN_COLS // 2)
    g["w_in_t"] = _mm("dw_in", dz, xn, "tn", tm=256, tn=1024)
    grad_x, g["g1"] = _tok_bwd("norm1_bwd", _fn_norm, [tok(x)], [p["g1"]], [[tok(d_xn)]], [(d, F32)],
                               add_to_first=tok(d_x_res))
    return loss, grad_x, g


_WEIGHTS = ["attn_pre_norm", "w_in", "hgrn_lb", "hgrn_gnorm", "w_branch_a", "rwkv_mu", "rwkv_w0", "rwkv_w2", "rwkv_a0",
            "rwkv_a2", "rwkv_g2", "rwkv_k_k", "rwkv_k_a", "rwkv_r_k", "rwkv_ln_w", "rwkv_ln_b", "w_branch_b", "w_out",
            "attn_post_norm", "ffn_pre_norm", "w_up", "conv_w", "conv_b", "w_down", "ffn_post_norm"]
_REPLICATED = [("attn_pre_norm", "g1"), ("hgrn_lb", "lb2"), ("hgrn_gnorm", "gnorm"), ("rwkv_mu", "mu"), ("rwkv_w0", "w0"),
               ("rwkv_a0", "a0"), ("rwkv_k_k", "k_k"), ("rwkv_k_a", "k_a"), ("rwkv_r_k", "r_k"), ("rwkv_ln_w", "ln_w"),
               ("rwkv_ln_b", "ln_b"), ("attn_post_norm", "g_post1"), ("ffn_pre_norm", "g_pre2"), ("conv_b", "conv_b"),
               ("ffn_post_norm", "g_post2")]
SLAB_COLS = 1024


def _pack(arrays):
    pieces, total = [], 0
    for a in arrays:
        flat = a.reshape(-1)
        rows = -(-flat.shape[0] // SLAB_COLS)
        pieces.append(jnp.pad(flat, (0, rows * SLAB_COLS - flat.shape[0])).reshape(rows, SLAB_COLS))
        total += rows
    if total % 8:
        pieces.append(jnp.zeros((8 - total % 8, SLAB_COLS), F32))
    return jnp.concatenate(pieces, axis=0)


def _unpack(slab, shapes):
    out, at = [], 0
    for s in shapes:
        size = 1
        for dim in s:
            size *= dim
        rows = -(-size // SLAB_COLS)
        out.append(slab[at:at + rows].reshape(-1)[:size].reshape(s))
        at += rows
    return out


def kernel(x, attn_pre_norm, w_in, hgrn_lb, hgrn_gnorm, w_branch_a, rwkv_mu, rwkv_w0, rwkv_w2, rwkv_a0, rwkv_a2, rwkv_g2, rwkv_k_k, rwkv_k_a, rwkv_r_k, rwkv_ln_w, rwkv_ln_b, w_branch_b, w_out, attn_post_norm, ffn_pre_norm, w_up, conv_w, conv_b, w_down, ffn_post_norm, loss_target, m_attn_pre_norm, m_w_in, m_hgrn_lb, m_hgrn_gnorm, m_w_branch_a, m_rwkv_mu, m_rwkv_w0, m_rwkv_w2, m_rwkv_a0, m_rwkv_a2, m_rwkv_g2, m_rwkv_k_k, m_rwkv_k_a, m_rwkv_r_k, m_rwkv_ln_w, m_rwkv_ln_b, m_w_branch_b, m_w_out, m_attn_post_norm, m_ffn_pre_norm, m_w_up, m_conv_w, m_conv_b, m_w_down, m_ffn_post_norm, v_attn_pre_norm, v_w_in, v_hgrn_lb, v_hgrn_gnorm, v_w_branch_a, v_rwkv_mu, v_rwkv_w0, v_rwkv_w2, v_rwkv_a0, v_rwkv_a2, v_rwkv_g2, v_rwkv_k_k, v_rwkv_k_a, v_rwkv_r_k, v_rwkv_ln_w, v_rwkv_ln_b, v_w_branch_b, v_w_out, v_attn_post_norm, v_ffn_pre_norm, v_w_up, v_conv_w, v_conv_b, v_w_down, v_ffn_post_norm):
    given = dict(locals())
    w = {n: given[n] for n in _WEIGHTS}
    mom = {n: given["m_" + n] for n in _WEIGHTS}
    var = {n: given["v_" + n] for n in _WEIGHTS}
    shard = 2 * lax.axis_index("x") + lax.axis_index("y")
    place = jnp.stack([lax.axis_index("c"), shard]).astype(jnp.int32)
    row = lambda a: a.reshape(1, -1)
    lora_of = lambda d: jnp.concatenate([d["rwkv_w2"][0], d["rwkv_a2"][0], d["rwkv_g2"][0]], axis=0)

    shards = [w["w_in"][0].T.astype(BF16), w["w_branch_a"][0].astype(BF16), w["w_branch_b"][0].astype(BF16),
              w["w_out"][0].astype(BF16), w["w_down"][0].astype(BF16), w["w_up"][0].T.astype(BF16),
              lora_of(w), jnp.pad(w["conv_w"][0], ((0, 13), (0, 0)))]
    w_in_t, w_a, w_b, w_o, w_dn, w_up_t, lora_g, conv_g = _gather_shards(shards)
    lora_full = lora_g.reshape(N_SHARD, LORA, 256).transpose(1, 0, 2).reshape(LORA, D_MODEL)
    conv_full = conv_g.reshape(N_SHARD, 16, 2 * D_FF // N_SHARD)[:, :3].transpose(1, 0, 2).reshape(3, 2 * D_FF)
    lrow = lax.broadcasted_iota(jnp.int32, (LORA, 1), 0)
    p = {
        "g1": row(w["attn_pre_norm"]), "lb2": w["hgrn_lb"], "gnorm": row(w["hgrn_gnorm"]), "w_in_t": w_in_t, "w_a": w_a,
        "mu": row(w["rwkv_mu"]), "w0": row(w["rwkv_w0"]), "a0": row(w["rwkv_a0"]),
        "w2p": jnp.where(lrow < 64, lora_full, 0.0), "a2p": jnp.where((lrow >= 64) & (lrow < 128), lora_full, 0.0),
        "g2p": jnp.where(lrow >= 128, lora_full, 0.0),
        "k_k": row(w["rwkv_k_k"]), "k_a": row(w["rwkv_k_a"]), "r_k": row(w["rwkv_r_k"]), "ln_w": row(w["rwkv_ln_w"]),
        "ln_b": row(w["rwkv_ln_b"]), "w_b": w_b, "w_out": w_o, "g_post1": row(w["attn_post_norm"]),
        "g_pre2": row(w["ffn_pre_norm"]), "w_up_t": w_up_t, "conv_w": conv_full, "conv_b": row(w["conv_b"]),
        "w_down": w_dn, "g_post2": row(w["ffn_post_norm"]),
    }

    loss, grad_x, g = _device_step(x[0], loss_target[0], p)

    g_in_t, g_a, g_b, g_o, g_dn, g_up_t = _reduce_scatter(
        [g["w_in_t"], g["w_a"], g["w_b"], g["w_out"], g["w_down"], g["w_up_t"]], place)
    small = [g[key] for _, key in _REPLICATED] + [g["conv_w"], g["w2p"][0:64], g["a2p"][64:128], g["g2p"][128:256]]
    gathered = _allgather8(_pack(small))
    summed = _sum3d("small_sum", [(gathered, functools.partial(lambda l, s, i: i, i=i)) for i in range(8)], None, place)
    rep_shapes = [w[n].shape for n, _ in _REPLICATED]
    n_rep_rows = sum(-(-w[n].size // SLAB_COLS) for n, _ in _REPLICATED)
    conv_sum, w2_sum, a2_sum, g2_sum = _unpack(summed[n_rep_rows:], [(3, 2 * D_FF), (64, D_MODEL), (64, D_MODEL), (128, D_MODEL)])
    conv_grad = lax.dynamic_slice_in_dim(conv_sum, shard * (2 * D_FF // N_SHARD), 2 * D_FF // N_SHARD, axis=1)
    lora_grad = lax.dynamic_slice_in_dim(jnp.concatenate([w2_sum, a2_sum, g2_sum], axis=0), shard * 256, 256, axis=1)

    res = {}

    def put(name, outs, shape=None):
        res[name] = [o.reshape(w[name].shape if shape is None else shape) for o in outs]

    put("w_in", _adamw("adamw_w_in", w["w_in"][0], g_in_t, mom["w_in"][0], var["w_in"][0], 1024, 128, g_transposed=True))
    put("w_up", _adamw("adamw_w_up", w["w_up"][0], g_up_t, mom["w_up"][0], var["w_up"][0], 1024, 128, g_transposed=True))
    for name, grad in (("w_branch_a", g_a), ("w_branch_b", g_b), ("w_out", g_o)):
        put(name, _adamw("adamw_" + name, w[name][0], grad, mom[name][0], var[name][0], 256, 1024))
    put("w_down", _adamw("adamw_w_down", w["w_down"][0], g_dn, mom["w_down"][0], var["w_down"][0], 176, 1024))
    put("conv_w", _adamw("adamw_conv_w", w["conv_w"][0], conv_grad, mom["conv_w"][0], var["conv_w"][0], 3, 2 * D_FF // N_SHARD))
    lora_out = _adamw("adamw_lora", lora_of(w), lora_grad, lora_of(mom), lora_of(var), LORA, 256)
    for name, lo, hi in (("rwkv_w2", 0, 64), ("rwkv_a2", 64, 128), ("rwkv_g2", 128, 256)):
        put(name, [o[lo:hi] for o in lora_out])
    rep_names = [n for n, _ in _REPLICATED]
    rep_out = _adamw("adamw_small", _pack([w[n] for n in rep_names]), summed[:n_rep_rows], _pack([mom[n] for n in rep_names]),
                     _pack([var[n] for n in rep_names]), n_rep_rows, SLAB_COLS)
    for name, parts in zip(rep_names, zip(*[_unpack(o, rep_shapes) for o in rep_out])):
        put(name, list(parts))

    loss = lax.psum(loss[0, 0], ("x", "y", "c"))
    return (loss, grad_x[None], *[res[n][0] for n in _WEIGHTS], *[res[n][1] for n in _WEIGHTS],
            *[res[n][2] for n in _WEIGHTS], *[res[n][3] for n in _WEIGHTS])
```

```python
import functools

import jax
import jax.numpy as jnp
from jax import lax
from jax.experimental import pallas as pl
from jax.experimental.pallas import tpu as pltpu

F32, BF16 = jnp.float32, jnp.bfloat16
MESH = pl.DeviceIdType.MESH

D_MODEL = 1024
HGRN_HEADS = 8
HGRN_K = 128
HGRN_SCALE = HGRN_K ** -0.5
CHUNK = 32
RWKV_HEAD = 64
LORA = 256
D_FF = 2816
EPS = 1e-6
GN_EPS = 1e-5 * RWKV_HEAD
N_SHARD = 4
ADAM_LR, ADAM_B1, ADAM_B2, ADAM_EPS, ADAM_WD, ADAM_STEP = 0.001, 0.9, 0.999, 1e-08, 0.01, 10

LANES = 128
VMEM_LIMIT = 56 * 1024 * 1024
SCAN_TB = 8
SCAN_GROUP = 256

C_HQ, C_HF, C_HI, C_HG = 0, 1024, 2048, 3072
C_R = 4096
R_COLS = 3328
C_G = 7424
IN_COLS = 9472


def _params(sem=None, **kw):
    return pltpu.CompilerParams(dimension_semantics=sem, vmem_limit_bytes=VMEM_LIMIT, **kw)


def _seg_matrix(n, seg):
    r = lax.broadcasted_iota(jnp.int32, (n, n), 0) // seg
    c = lax.broadcasted_iota(jnp.int32, (n, n), 1) // seg
    return (r == c).astype(BF16)


def _split3(x):
    hi = x.astype(BF16)
    r1 = x - hi.astype(F32)
    mid = r1.astype(BF16)
    lo = (r1 - mid.astype(F32)).astype(BF16)
    return hi, mid, lo


def _segsum_impl(x, seg):
    e = _seg_matrix(LANES, seg)
    outs = []
    for g in range(x.shape[1] // LANES):
        hi, mid, lo = _split3(x[:, g * LANES:(g + 1) * LANES])
        outs.append(jnp.dot(hi, e, preferred_element_type=F32) + jnp.dot(mid, e, preferred_element_type=F32)
                    + jnp.dot(lo, e, preferred_element_type=F32))
    return outs[0] if len(outs) == 1 else jnp.concatenate(outs, axis=1)


def _make_segsum(seg):
    @jax.custom_vjp
    def f(x):
        return _segsum_impl(x, seg)

    f.defvjp(lambda x: (_segsum_impl(x, seg), None), lambda _, ct: (_segsum_impl(ct, seg),))
    return f


_segsum64 = _make_segsum(RWKV_HEAD)
_segsum128 = _make_segsum(HGRN_K)


@jax.custom_vjp
def _bdot(x, w):
    return jnp.dot(x.astype(BF16), w.astype(BF16), preferred_element_type=F32)


def _bdot_fwd(x, w):
    return _bdot(x, w), (x, w)


def _bdot_bwd(res, ct):
    x, w = res
    ctb = ct.astype(BF16)
    dx = lax.dot_general(ctb, w.astype(BF16), (((1,), (1,)), ((), ())), preferred_element_type=F32)
    dw = lax.dot_general(x.astype(BF16), ctb, (((0,), (0,)), ((), ())), preferred_element_type=F32)
    return dx, dw


_bdot.defvjp(_bdot_fwd, _bdot_bwd)


def _sigmoid(x):
    return 1.0 / (1.0 + jnp.exp(-x))


def _silu(x):
    return x * _sigmoid(x)


def _softplus(x):
    return jnp.maximum(x, 0.0) + jnp.log(1.0 + jnp.exp(-jnp.abs(x)))


def _rms(x, g):
    return x * lax.rsqrt(jnp.mean(x * x, axis=-1, keepdims=True) + EPS) * g


def _fn_norm(t, p):
    return [_rms(t[0], p[0])]


def _fn_hgates(t, p):
    hq, hf = t
    lb2 = p[0]
    m = jnp.max(lb2, axis=0, keepdims=True)
    e = jnp.exp(lb2 - m)
    first = lax.broadcasted_iota(jnp.int32, e.shape, 0) == 0
    lb = jnp.sum(jnp.where(first, e, 0.0), axis=0, keepdims=True) / jnp.sum(e, axis=0, keepdims=True)
    f = lb + (1.0 - lb) * _sigmoid(hf)
    return [_silu(hq) * HGRN_SCALE, jnp.log(f), 1.0 - f]


def _fn_hpost(t, p):
    o, hg = t
    ms = _segsum128(o * o) * (1.0 / HGRN_K)
    return [o * lax.rsqrt(ms + EPS) * p[0] * _silu(hg)]


def _fn_rprep(t, p):
    kr, lora = t
    w0, w2p, a0, a2p, g2p, k_k, k_a = p
    pre_w = w0 + _bdot(jnp.tanh(lora), w2p)
    w_log = -_softplus(-pre_w) - 0.5
    decay = jnp.exp(-jnp.exp(w_log))
    a = _sigmoid(a0 + _bdot(lora, a2p))
    g = _bdot(_sigmoid(lora), g2p)
    kk = kr * k_k
    kk = kk / jnp.maximum(jnp.sqrt(_segsum64(kk * kk)), 1e-12)
    kr2 = kr * (1.0 + (a - 1.0) * k_a)
    return [decay, kr2, -kk, kk * a, g]


def _fn_rpost(t, p):
    y, r, kr2, v, g = t
    ln_w, ln_b, r_k = p
    mu = _segsum64(y) * (1.0 / RWKV_HEAD)
    yc = y - mu
    var = _segsum64(yc * yc) * (1.0 / RWKV_HEAD)
    yn = yc * lax.rsqrt(var + GN_EPS) * ln_w + ln_b
    bonus = _segsum64(r * kr2 * r_k) * v
    return [(yn + bonus) * g]


def _fn_merge(t, p):
    ga, gb, ya, yb = t
    return [_sigmoid(ga) * ya + _sigmoid(gb) * yb]


def _fn_res1(t, p):
    x, mix = t
    h1 = x + _rms(mix, p[0])
    return [h1, _rms(h1, p[1])]


def _tok_call(name, fn, toks, params, outs, red_shapes=(), tm=128, col_grid=1):
    n_t, n_p, n_o = len(toks), len(params), len(outs)
    t_len = toks[0][0].shape[0]

    def body(*refs):
        tv = [r[...].astype(F32) for r in refs[:n_t]]
        pv = [r[...] for r in refs[n_t:n_t + n_p]]
        o, red = fn(tv, pv)
        for ref, val in zip(refs[n_t + n_p:n_t + n_p + n_o], o):
            ref[...] = val.astype(ref.dtype)
        red_refs = refs[n_t + n_p + n_o:]
        if red_refs:
            first = pl.program_id(0) == 0

            @pl.when(first)
            def _():
                for ref, val in zip(red_refs, red):
                    ref[...] = val

            @pl.when(jnp.logical_not(first))
            def _():
                for ref, val in zip(red_refs, red):
                    ref[...] += val

    in_specs = [pl.BlockSpec((tm, w), functools.partial(lambda i, j, c: (i, c + j), c=c)) for (_, w, c) in toks]
    in_specs += [pl.BlockSpec(p.shape, lambda i, j: (0, 0)) for p in params]
    out_specs = [pl.BlockSpec((tm, w), lambda i, j: (i, j)) for (w, _) in outs]
    out_specs += [pl.BlockSpec(s, lambda i, j: (0, 0)) for s in red_shapes]
    out_shape = [jax.ShapeDtypeStruct((t_len, w * col_grid), dt) for (w, dt) in outs]
    out_shape += [jax.ShapeDtypeStruct(s, F32) for s in red_shapes]
    return pl.pallas_call(
        body, name=name, grid=(t_len // tm, col_grid), in_specs=in_specs, out_specs=out_specs, out_shape=out_shape,
        compiler_params=_params(("arbitrary", "arbitrary")),
    )(*[a for (a, _, _) in toks], *params)


def _tok_fwd(name, fn, toks, params, outs, **kw):
    return _tok_call(name, lambda tv, pv: (fn(tv, pv), []), toks, params, outs, **kw)


def _tok_bwd(name, fn, toks, params, cts, want, add_to_first=None, **kw):
    n_t = len(toks)
    flat = [c for group in cts for c in group]
    extra = [] if add_to_first is None else [add_to_first]

    def bwd(tv, pv):
        prim, rest = tv[:n_t], tv[n_t:]
        ct, at = [], 0
        for group in cts:
            ct.append(functools.reduce(lambda u, v: u + v, rest[at:at + len(group)]))
            at += len(group)
        _, vjp = jax.vjp(lambda *a: tuple(fn(list(a[:n_t]), list(a[n_t:]))), *prim, *pv)
        g = vjp(tuple(ct))
        tok_grads = [g[i] for i in range(n_t) if want[i] is not None]
        if extra:
            tok_grads[0] = tok_grads[0] + rest[at]
        return tok_grads, list(g[n_t:])

    return _tok_call(name, bwd, list(toks) + flat + extra, params, [w for w in want if w is not None],
                     red_shapes=[p.shape for p in params], **kw)


def _mm(name, a, b, mode, out_dtype=F32, tm=512, tn=512, tk=None):
    if mode == "nn":
        (m, k), (_, n) = a.shape, b.shape
    elif mode == "nt":
        (m, k), (n, _) = a.shape, b.shape
    else:
        (k, m), (_, n) = a.shape, b.shape
    tk = k if tk is None else tk
    tm, tn = min(tm, m), min(tn, n)
    nk = k // tk
    assert m % tm == 0 and n % tn == 0 and k % tk == 0, (name, a.shape, b.shape, tm, tn, tk)
    a_spec = pl.BlockSpec((tk, tm), lambda i, j, q: (q, i)) if mode == "tn" else pl.BlockSpec((tm, tk), lambda i, j, q: (i, q))
    b_spec = pl.BlockSpec((tn, tk), lambda i, j, q: (j, q)) if mode == "nt" else pl.BlockSpec((tk, tn), lambda i, j, q: (q, j))
    dn = {"nn": (((1,), (0,)), ((), ())), "nt": (((1,), (1,)), ((), ())), "tn": (((0,), (0,)), ((), ()))}[mode]

    def body(a_ref, b_ref, o_ref, *acc):
        p = lax.dot_general(a_ref[...], b_ref[...], dn, preferred_element_type=F32)
        if nk == 1:
            o_ref[...] = p.astype(o_ref.dtype)
        else:
            q = pl.program_id(2)

            @pl.when(q == 0)
            def _():
                acc[0][...] = p

            @pl.when(q > 0)
            def _():
                acc[0][...] += p

            @pl.when(q == nk - 1)
            def _():
                o_ref[...] = acc[0][...].astype(o_ref.dtype)

    return pl.pallas_call(
        body, name=name, grid=(m // tm, n // tn, nk), in_specs=[a_spec, b_spec],
        out_specs=pl.BlockSpec((tm, tn), lambda i, j, q: (i, j)), out_shape=jax.ShapeDtypeStruct((m, n), out_dtype),
        scratch_shapes=[pltpu.VMEM((tm, tn), F32)] if nk > 1 else [],
        compiler_params=_params(("parallel", "parallel", "arbitrary")),
    )(a, b)


def _shift_down(z, n):
    rows = lax.broadcasted_iota(jnp.int32, z.shape, 0)
    return jnp.where(rows < n, 0.0, pltpu.roll(z, n, 0))


def _shift_up(z, n):
    t = z.shape[0]
    rows = lax.broadcasted_iota(jnp.int32, z.shape, 0)
    return jnp.where(rows >= t - n, 0.0, pltpu.roll(z, t - n, 0))


def _lerp_fwd(z, mu):
    t = z.shape[0]
    w = 256

    def body(z_ref, mu_ref, o_ref):
        zz = z_ref[...]
        o_ref[...] = zz + mu_ref[...] * (_shift_down(zz, 1) - zz)

    return pl.pallas_call(
        body, name="lerp_fwd", grid=(R_COLS // w,),
        in_specs=[pl.BlockSpec((t, w), lambda j: (0, C_R // w + j)), pl.BlockSpec((1, w), lambda j: (0, j))],
        out_specs=pl.BlockSpec((t, w), lambda j: (0, j)), out_shape=jax.ShapeDtypeStruct((t, R_COLS), F32),
        compiler_params=_params(("parallel",)),
    )(z, mu)


def _lerp_bwd(z, mu, dzl):
    t = z.shape[0]
    w = 256

    def body(z_ref, mu_ref, d_ref, dz_ref, dmu_ref):
        zz, m, d = z_ref[...], mu_ref[...], d_ref[...]
        dz_ref[...] = (d * (1.0 - m) + _shift_up(d * m, 1)).astype(dz_ref.dtype)
        dmu_ref[...] = jnp.sum(d * (_shift_down(zz, 1) - zz), axis=0, keepdims=True)

    return pl.pallas_call(
        body, name="lerp_bwd", grid=(R_COLS // w,),
        in_specs=[pl.BlockSpec((t, w), lambda j: (0, C_R // w + j)), pl.BlockSpec((1, w), lambda j: (0, j)),
                  pl.BlockSpec((t, w), lambda j: (0, j))],
        out_specs=[pl.BlockSpec((t, w), lambda j: (0, j)), pl.BlockSpec((1, w), lambda j: (0, j))],
        out_shape=[jax.ShapeDtypeStruct((t, R_COLS), BF16), jax.ShapeDtypeStruct((1, R_COLS), F32)],
        compiler_params=_params(("parallel",)),
    )(z, mu, dzl)


CONV_TILE = 256
N_CONV_TILES = D_FF // CONV_TILE


def _conv(h, w, b):
    return b + w[0:1, :] * _shift_down(h, 2) + w[1:2, :] * _shift_down(h, 1) + w[2:3, :] * h


def _conv_fwd(hu, conv_w, conv_b):
    t = hu.shape[0]
    n = N_CONV_TILES

    def body(hg_ref, hv_ref, wg_ref, wv_ref, bg_ref, bv_ref, o_ref):
        gate = _conv(hg_ref[...], wg_ref[...], bg_ref[...])
        val = _conv(hv_ref[...], wv_ref[...], bv_ref[...])
        o_ref[...] = (_silu(gate) * val).astype(o_ref.dtype)

    col = lambda off: pl.BlockSpec((t, CONV_TILE), lambda j: (0, j + off))
    wspec = lambda off: pl.BlockSpec((3, CONV_TILE), lambda j: (0, j + off))
    bspec = lambda off: pl.BlockSpec((1, CONV_TILE), lambda j: (0, j + off))
    return pl.pallas_call(
        body, name="conv_fwd", grid=(n,),
        in_specs=[col(0), col(n), wspec(0), wspec(n), bspec(0), bspec(n)],
        out_specs=pl.BlockSpec((t, CONV_TILE), lambda j: (0, j)), out_shape=jax.ShapeDtypeStruct((t, D_FF), BF16),
        compiler_params=_params(("parallel",)),
    )(hu, hu, conv_w, conv_w, conv_b, conv_b)


def _conv_bwd(hu, conv_w, conv_b, d_act):
    t = hu.shape[0]
    n = N_CONV_TILES

    def body(hg_ref, hv_ref, wg_ref, wv_ref, bg_ref, bv_ref, d_ref, dh_ref, dw_ref, db_ref):
        is_gate = pl.program_id(0) < n
        hg, hv = hg_ref[...], hv_ref[...]
        gate = _conv(hg, wg_ref[...], bg_ref[...])
        val = _conv(hv, wv_ref[...], bv_ref[...])
        d = d_ref[...]
        sg = _sigmoid(gate)
        d_gate = d * val * (sg * (1.0 + gate * (1.0 - sg)))
        d_val = d * (gate * sg)
        dc = jnp.where(is_gate, d_gate, d_val)
        h = jnp.where(is_gate, hg, hv)
        w = jnp.where(is_gate, wg_ref[...], wv_ref[...])
        dh = w[2:3, :] * dc + w[1:2, :] * _shift_up(dc, 1) + w[0:1, :] * _shift_up(dc, 2)
        dh_ref[...] = dh.astype(dh_ref.dtype)
        dw_ref[0:1, :] = jnp.sum(dc * _shift_down(h, 2), axis=0, keepdims=True)
        dw_ref[1:2, :] = jnp.sum(dc * _shift_down(h, 1), axis=0, keepdims=True)
        dw_ref[2:3, :] = jnp.sum(dc * h, axis=0, keepdims=True)
        db_ref[...] = jnp.sum(dc, axis=0, keepdims=True)

    gcol = lambda rows: pl.BlockSpec((rows, CONV_TILE), lambda j: (0, j % n))
    vcol = lambda rows: pl.BlockSpec((rows, CONV_TILE), lambda j: (0, j % n + n))
    own = lambda rows: pl.BlockSpec((rows, CONV_TILE), lambda j: (0, j))
    return pl.pallas_call(
        body, name="conv_bwd", grid=(2 * n,),
        in_specs=[gcol(t), vcol(t), gcol(3), vcol(3), gcol(1), vcol(1), gcol(t)],
        out_specs=[own(t), own(3), own(1)],
        out_shape=[jax.ShapeDtypeStruct((t, 2 * D_FF), BF16), jax.ShapeDtypeStruct((3, 2 * D_FF), F32),
                   jax.ShapeDtypeStruct((1, 2 * D_FF), F32)],
        compiler_params=_params(("parallel",)),
    )(hu, hu, conv_w, conv_w, conv_b, conv_b, d_act)


_HI = lax.Precision.HIGHEST


def _dot_hi(a, b, dn):
    return lax.dot_general(a, b, dn, precision=_HI, preferred_element_type=F32)


_NN = (((1,), (0,)), ((), ()))
_NT = (((1,), (1,)), ((), ()))
_TN = (((0,), (0,)), ((), ()))


def _hgrn_chunk(qc, gc, kc, vc, st):
    r = lax.broadcasted_iota(jnp.int32, (CHUNK, CHUNK), 0)
    c = lax.broadcasted_iota(jnp.int32, (CHUNK, CHUNK), 1)
    tril = r >= c
    b = _dot_hi(tril.astype(F32), gc, _NN)
    rows = lax.broadcasted_iota(jnp.int32, b.shape, 0)
    b_ref = jnp.sum(jnp.where(rows == CHUNK // 2 - 1, b, 0.0), axis=0, keepdims=True)
    b_last = jnp.sum(jnp.where(rows == CHUNK - 1, b, 0.0), axis=0, keepdims=True)
    q_in = qc * jnp.exp(b - b_ref)
    k_in = kc * jnp.exp(b_ref - b)
    scores = jnp.where(tril, _dot_hi(q_in, k_in, _NT), 0.0)
    o = _dot_hi(scores, vc, _NN) + _dot_hi(qc * jnp.exp(b), st, _NT)
    u_t = _dot_hi(vc, kc * jnp.exp(b_last - b), _TN)
    return o, st * jnp.exp(b_last) + u_t


def _hgrn_fwd(q, logf, kh, z):
    t = q.shape[0]
    nc = t // CHUNK

    def body(q_ref, g_ref, k_ref, v_ref, o_ref, st_ref):
        def chunk(n, st):
            rows = pl.ds(pl.multiple_of(n * CHUNK, CHUNK), CHUNK)
            st_ref[0, n] = st
            o, st_new = _hgrn_chunk(q_ref[rows, :], g_ref[rows, :], k_ref[rows, :], v_ref[rows, :], st)
            o_ref[rows, :] = o
            return st_new

        lax.fori_loop(0, nc, chunk, jnp.zeros((HGRN_K, HGRN_K), F32))

    head = pl.BlockSpec((t, HGRN_K), lambda h: (0, h))
    return pl.pallas_call(
        body, name="hgrn_fwd", grid=(HGRN_HEADS,),
        in_specs=[head, head, head, pl.BlockSpec((t, HGRN_K), lambda h: (0, C_HI // HGRN_K + h))],
        out_specs=[head, pl.BlockSpec((1, nc, HGRN_K, HGRN_K), lambda h: (h, 0, 0, 0))],
        out_shape=[jax.ShapeDtypeStruct((t, D_MODEL), F32), jax.ShapeDtypeStruct((HGRN_HEADS, nc, HGRN_K, HGRN_K), F32)],
        compiler_params=_params(("parallel",)),
    )(q, logf, kh, z)


def _hgrn_bwd(q, logf, kh, z, states, d_o):
    t = q.shape[0]
    nc = t // CHUNK

    def body(q_ref, g_ref, k_ref, v_ref, st_ref, do_ref, dq_ref, dg_ref, dk_ref, dv_ref):
        def chunk(i, d_st):
            n = nc - 1 - i
            rows = pl.ds(pl.multiple_of(n * CHUNK, CHUNK), CHUNK)
            _, vjp = jax.vjp(_hgrn_chunk, q_ref[rows, :], g_ref[rows, :], k_ref[rows, :], v_ref[rows, :], st_ref[0, n])
            dq, dg, dk, dv, d_prev = vjp((do_ref[rows, :], d_st))
            dq_ref[rows, :] = dq
            dg_ref[rows, :] = dg
            dk_ref[rows, :] = dk
            dv_ref[rows, :] = dv
            return d_prev

        lax.fori_loop(0, nc, chunk, jnp.zeros((HGRN_K, HGRN_K), F32))

    head = pl.BlockSpec((t, HGRN_K), lambda h: (0, h))
    out = jax.ShapeDtypeStruct((t, D_MODEL), F32)
    return pl.pallas_call(
        body, name="hgrn_bwd", grid=(HGRN_HEADS,),
        in_specs=[head, head, head, pl.BlockSpec((t, HGRN_K), lambda h: (0, C_HI // HGRN_K + h)),
                  pl.BlockSpec((1, nc, HGRN_K, HGRN_K), lambda h: (h, 0, 0, 0)), head],
        out_specs=[head, head, head, head], out_shape=[out, out, out, out],
        compiler_params=_params(("parallel",)),
    )(q, logf, kh, z, states, d_o)


def _split2(x):
    hi = x.astype(BF16)
    return hi, (x - hi.astype(F32)).astype(BF16)


def _seg_bcast(xs, e):
    parts = [t for x in xs for t in _split2(x)]
    out = jnp.dot(jnp.concatenate(parts, axis=0), e, preferred_element_type=F32)
    n = RWKV_HEAD
    return [out[2 * i * n:(2 * i + 1) * n] + out[(2 * i + 1) * n:(2 * i + 2) * n] for i in range(len(xs))]


def _rows_to_cols(rows, diag, e):
    zero = jnp.zeros((), BF16)
    parts = [jnp.where(diag, t, zero) for row in rows for t in _split2(row)]
    out = jnp.dot(jnp.concatenate(parts, axis=0), e, preferred_element_type=F32)
    n = RWKV_HEAD
    return [out[2 * i * n:(2 * i + 1) * n] + out[(2 * i + 1) * n:(2 * i + 2) * n] for i in range(len(rows))]


def _col_to_row(col, diag):
    return jnp.sum(jnp.where(diag, col, 0.0), axis=0, keepdims=True)


def _scan_consts():
    e = _seg_matrix(SCAN_GROUP, RWKV_HEAD)
    i = lax.broadcasted_iota(jnp.int32, (RWKV_HEAD, SCAN_GROUP), 0)
    l = lax.broadcasted_iota(jnp.int32, (RWKV_HEAD, SCAN_GROUP), 1)
    groups = [slice(g * SCAN_GROUP, (g + 1) * SCAN_GROUP) for g in range(D_MODEL // SCAN_GROUP)]
    return e, (l % RWKV_HEAD) == i, groups


def _rwkv_fwd(zl, w, k, a, b):
    t = zl.shape[0]
    steps = range(SCAN_TB)

    def body(r_ref, w_ref, k_ref, v_ref, a_ref, b_ref, y_ref, st_ref, s_ref):
        @pl.when(pl.program_id(0) == 0)
        def _():
            s_ref[...] = jnp.zeros_like(s_ref)

        e, diag, groups = _scan_consts()
        v_cols = [_rows_to_cols([v_ref[i:i + 1, sl] for i in steps], diag, e) for sl in groups]
        s = [s_ref[:, sl] for sl in groups]
        for i in steps:
            for g, sl in enumerate(groups):
                (sa,) = _seg_bcast([s[g] * a_ref[i:i + 1, sl]], e)
                s[g] = s[g] * w_ref[i:i + 1, sl] + sa * b_ref[i:i + 1, sl] + v_cols[g][i] * k_ref[i:i + 1, sl]
                st_ref[i, :, sl] = s[g]
        for g, sl in enumerate(groups):
            s_ref[:, sl] = s[g]
            y_cols = _seg_bcast([st_ref[i, :, sl] * r_ref[i:i + 1, sl] for i in steps], e)
            for i in steps:
                y_ref[i:i + 1, sl] = _col_to_row(y_cols[i], diag)

    blk = pl.BlockSpec((SCAN_TB, D_MODEL), lambda n: (n, 0))
    v_blk = pl.BlockSpec((SCAN_TB, D_MODEL), lambda n: (n, 2))
    return pl.pallas_call(
        body, name="rwkv_fwd", grid=(t // SCAN_TB,), in_specs=[blk, blk, blk, v_blk, blk, blk],
        out_specs=[blk, pl.BlockSpec((SCAN_TB, RWKV_HEAD, D_MODEL), lambda n: (n, 0, 0))],
        out_shape=[jax.ShapeDtypeStruct((t, D_MODEL), F32), jax.ShapeDtypeStruct((t, RWKV_HEAD, D_MODEL), F32)],
        scratch_shapes=[pltpu.VMEM((RWKV_HEAD, D_MODEL), F32)],
        compiler_params=_params(("arbitrary",)),
    )(zl, w, k, zl, a, b)


def _rwkv_bwd(zl, w, k, a, b, states, d_y):
    t = zl.shape[0]
    nb = t // SCAN_TB
    steps = range(SCAN_TB)

    def body(r_ref, w_ref, k_ref, v_ref, a_ref, b_ref, st_ref, prev_ref, dy_ref,
             dr_ref, dw_ref, dk_ref, dv_ref, da_ref, db_ref, ds_ref):
        @pl.when(pl.program_id(0) == 0)
        def _():
            ds_ref[...] = jnp.zeros_like(ds_ref)

        has_prev = (pl.program_id(0) < nb - 1).astype(F32)
        e, diag, groups = _scan_consts()
        colsum = lambda x: jnp.sum(x, axis=0, keepdims=True)

        def s_prev(i, sl):
            return st_ref[i - 1, :, sl] if i > 0 else prev_ref[0, :, sl] * has_prev

        dy_cols = [_rows_to_cols([dy_ref[i:i + 1, sl] for i in steps], diag, e) for sl in groups]
        v_cols = [_rows_to_cols([v_ref[i:i + 1, sl] for i in steps], diag, e) for sl in groups]
        sa_cols = [_seg_bcast([s_prev(i, sl) * a_ref[i:i + 1, sl] for i in steps], e) for sl in groups]
        ds = [ds_ref[:, sl] for sl in groups]
        dsk = [[None] * SCAN_TB for _ in groups]
        for i in reversed(steps):
            for g, sl in enumerate(groups):
                row = lambda ref: ref[i:i + 1, sl]
                sp = s_prev(i, sl)
                d = ds[g] + dy_cols[g][i] * row(r_ref)
                dr_ref[i:i + 1, sl] = colsum(st_ref[i, :, sl] * dy_cols[g][i])
                dw_ref[i:i + 1, sl] = colsum(d * sp)
                db_ref[i:i + 1, sl] = colsum(d * sa_cols[g][i])
                dk_ref[i:i + 1, sl] = colsum(d * v_cols[g][i])
                dsk[g][i] = d * row(k_ref)
                (dsa,) = _seg_bcast([d * row(b_ref)], e)
                da_ref[i:i + 1, sl] = colsum(sp * dsa)
                ds[g] = d * row(w_ref) + dsa * row(a_ref)
        for g, sl in enumerate(groups):
            ds_ref[:, sl] = ds[g]
            dv_cols = _seg_bcast(dsk[g], e)
            for i in steps:
                dv_ref[i:i + 1, sl] = _col_to_row(dv_cols[i], diag)

    blk = pl.BlockSpec((SCAN_TB, D_MODEL), lambda n: (nb - 1 - n, 0))
    v_blk = pl.BlockSpec((SCAN_TB, D_MODEL), lambda n: (nb - 1 - n, 2))
    out = jax.ShapeDtypeStruct((t, D_MODEL), F32)
    return pl.pallas_call(
        body, name="rwkv_bwd", grid=(nb,),
        in_specs=[blk, blk, blk, v_blk, blk, blk] + [
            pl.BlockSpec((SCAN_TB, RWKV_HEAD, D_MODEL), lambda n: (nb - 1 - n, 0, 0)),
            pl.BlockSpec((1, RWKV_HEAD, D_MODEL), lambda n: (jnp.maximum((nb - 1 - n) * SCAN_TB - 1, 0), 0, 0)),
            blk],
        out_specs=[blk] * 6, out_shape=[out] * 6,
        scratch_shapes=[pltpu.VMEM((RWKV_HEAD, D_MODEL), F32)],
        compiler_params=_params(("arbitrary",)),
    )(zl, w, k, zl, a, b, states, states, d_y)


def _loss_head(h1, ff, target, g_post):
    def fn(tv, pv):
        a, f, tgt = tv
        h2, vjp = jax.vjp(lambda a_, f_, g_: a_ + _rms(f_, g_), a, f, pv[0])
        err = h2 - tgt
        loss = 0.5 * jnp.sum(jnp.mean(err * err, axis=-1, keepdims=True), axis=0, keepdims=True)
        d_a, d_f, d_g = vjp(err * (1.0 / D_MODEL))
        return [d_a, d_f], [loss, d_g]

    return _tok_call("loss_head", fn, [(h1, D_MODEL, 0), (ff, D_MODEL, 0), (target, D_MODEL, 0)], [g_post],
                     [(D_MODEL, F32), (D_MODEL, BF16)], red_shapes=[(1, 1), (1, D_MODEL)])


def _sum_call(name, terms, rows_per_block=None):
    a0, i0 = terms[0]
    r, c = a0.shape[-2:]
    tr = rows_per_block or r

    def body(*refs):
        acc = refs[0][...].reshape(tr, c)
        for ref in refs[1:-1]:
            acc = acc + ref[...].reshape(tr, c)
        refs[-1][...] = acc

    def spec(arr, idx):
        if arr.ndim == 2:
            return pl.BlockSpec((tr, c), lambda i: (i, 0))
        return pl.BlockSpec((1, tr, c), functools.partial(lambda i, idx: (idx, i, 0), idx=idx))

    return pl.pallas_call(
        body, name=name, grid=(r // tr,), in_specs=[spec(a, i) for a, i in terms],
        out_specs=pl.BlockSpec((tr, c), lambda i: (i, 0)), out_shape=jax.ShapeDtypeStruct((r, c), F32),
        compiler_params=_params(("parallel",)),
    )(*[a for a, _ in terms])


def _adamw_math(w, g, m, v):
    m2 = ADAM_B1 * m + (1.0 - ADAM_B1) * g
    v2 = ADAM_B2 * v + (1.0 - ADAM_B2) * (g * g)
    m_hat = m2 / (1.0 - ADAM_B1 ** ADAM_STEP)
    v_hat = v2 / (1.0 - ADAM_B2 ** ADAM_STEP)
    return -ADAM_LR * (m_hat / (jnp.sqrt(v_hat) + ADAM_EPS) + ADAM_WD * w), m2, v2


def _adamw(name, w, g, m, v, bm, bn, g_transposed=False):
    r, c = w.shape

    def body(w_ref, g_ref, m_ref, v_ref, go_ref, d_ref, mo_ref, vo_ref):
        g = g_ref[...].T if g_transposed else g_ref[...]
        d, m2, v2 = _adamw_math(w_ref[...], g, m_ref[...], v_ref[...])
        go_ref[...] = g
        d_ref[...] = d
        mo_ref[...] = m2
        vo_ref[...] = v2

    blk = pl.BlockSpec((bm, bn), lambda i, j: (i, j))
    g_blk = pl.BlockSpec((bn, bm), lambda i, j: (j, i)) if g_transposed else blk
    out = jax.ShapeDtypeStruct((r, c), F32)
    return pl.pallas_call(
        body, name=name, grid=(pl.cdiv(r, bm), pl.cdiv(c, bn)), in_specs=[blk, g_blk, blk, blk],
        out_specs=[blk] * 4, out_shape=[out] * 4, compiler_params=_params(("parallel", "parallel")),
    )(w, g, m, v)


ANY = pl.BlockSpec(memory_space=pl.ANY)


def _place():
    x, y, c = lax.axis_index("x"), lax.axis_index("y"), lax.axis_index("c")
    chips = [(1 - x, y), (x, 1 - y), (1 - x, 1 - y)]
    return x, y, c, chips


def _sibling():
    return (lax.axis_index("x"), lax.axis_index("y"), 1 - lax.axis_index("c"))


def _wait_all(local, remote):
    for cp in local:
        cp.wait()
    for cp in remote:
        cp.wait_send()


def _place_shard(name, shard, place):
    r, cols = shard.shape
    tr = r // 4

    def body(s_ref, in_ref, out_ref):
        out_ref[...] = in_ref[...]

    return pl.pallas_call(
        body, name=name,
        grid_spec=pltpu.PrefetchScalarGridSpec(
            num_scalar_prefetch=1, grid=(4,), in_specs=[pl.BlockSpec((tr, cols), lambda i, s: (i, 0))],
            out_specs=pl.BlockSpec((tr, cols), lambda i, s: (4 * s[1] + i, 0))),
        out_shape=jax.ShapeDtypeStruct((N_SHARD * r, cols), shard.dtype), compiler_params=_params(("arbitrary",)),
    )(place, shard)


def _gather_shards(shards, placed):
    n = len(shards)

    def body(*refs):
        ins, outs = refs[:n], refs[2 * n:3 * n]
        send_sems, recv_sems = refs[3 * n:]
        x, y, c, chips = _place()
        me, sibling = (x, y, c), _sibling()

        def rows(k, px, py, pc):
            h = ins[k].shape[0] // 2
            return outs[k].at[pl.ds((2 * px + py) * 2 * h + pc * h, h), :]

        def copy(k, j, block, to, src=None):
            return pltpu.make_async_remote_copy(
                src_ref=rows(k, *block) if src is None else src, dst_ref=rows(k, *block),
                send_sem=send_sems.at[k, j], recv_sem=recv_sems.at[k, j], device_id=to, device_id_type=MESH)

        sends = []
        for k in range(n):
            h = ins[k].shape[0] // 2
            for j, chip in enumerate(chips):
                sends.append(copy(k, j, me, (*chip, c), src=ins[k].at[pl.ds(c * h, h), :]))
                sends[-1].start()
        for k in range(n):
            for j, chip in enumerate(chips):
                copy(k, j, (*chip, c), me).wait_recv()
                sends.append(copy(k, 3 + j, (*chip, c), sibling))
                sends[-1].start()
        for k in range(n):
            for j, chip in enumerate(chips):
                copy(k, 3 + j, (*chip, 1 - c), me).wait_recv()
        _wait_all([], sends)

    return pl.pallas_call(
        body, name="gather_shards", in_specs=[ANY] * (2 * n), out_specs=[ANY] * n,
        out_shape=[jax.ShapeDtypeStruct(a.shape, a.dtype) for a in placed],
        input_output_aliases={n + k: k for k in range(n)},
        scratch_shapes=[pltpu.SemaphoreType.DMA((n, 6)), pltpu.SemaphoreType.DMA((n, 6))],
    )(*shards, *placed)


def _exchange8(parts, landing):
    def body(in_ref, _, out_ref, send_sems, recv_sems):
        x, y, c, _ = _place()
        sends = []
        for rel in range(1, 8):
            dx, dy, dc = rel >> 2 & 1, rel >> 1 & 1, rel & 1
            cp = pltpu.make_async_remote_copy(
                src_ref=in_ref.at[2 * (x ^ dx) + (y ^ dy)], dst_ref=out_ref.at[4 * x + 2 * y + c],
                send_sem=send_sems.at[rel - 1], recv_sem=recv_sems.at[rel - 1],
                device_id=(x ^ dx, y ^ dy, c ^ dc), device_id_type=MESH)
            cp.start()
            sends.append(cp)
        for rel in range(1, 8):
            dx, dy, dc = rel >> 2 & 1, rel >> 1 & 1, rel & 1
            pltpu.make_async_remote_copy(
                src_ref=in_ref.at[0], dst_ref=out_ref.at[4 * (x ^ dx) + 2 * (y ^ dy) + (c ^ dc)],
                send_sem=send_sems.at[rel - 1], recv_sem=recv_sems.at[rel - 1],
                device_id=(x, y, c), device_id_type=MESH).wait_recv()
        _wait_all([], sends)

    return pl.pallas_call(
        body, name="exchange8", in_specs=[ANY, ANY], out_specs=ANY, out_shape=jax.ShapeDtypeStruct(landing.shape, landing.dtype),
        input_output_aliases={1: 0}, scratch_shapes=[pltpu.SemaphoreType.DMA((7,)), pltpu.SemaphoreType.DMA((7,))],
    )(parts, landing)


def _rs_sibling(grads):
    n = len(grads)

    def body(*refs):
        ins, outs = refs[:n], refs[n:2 * n]
        send_sems, recv_sems = refs[2 * n:]
        c = lax.axis_index("c")
        sends = []
        for k in range(n):
            for s in range(N_SHARD):
                cp = pltpu.make_async_remote_copy(
                    src_ref=ins[k].at[2 * s + 1 - c], dst_ref=outs[k].at[s], send_sem=send_sems.at[k, s],
                    recv_sem=recv_sems.at[k, s], device_id=_sibling(), device_id_type=MESH)
                cp.start()
                sends.append(cp)
        for cp in sends:
            cp.wait_recv()
        _wait_all([], sends)

    return pl.pallas_call(
        body, name="rs_sibling", in_specs=[ANY] * n, out_specs=[ANY] * n,
        out_shape=[jax.ShapeDtypeStruct((N_SHARD,) + a.shape[1:], a.dtype) for a in grads],
        scratch_shapes=[pltpu.SemaphoreType.DMA((n, N_SHARD)), pltpu.SemaphoreType.DMA((n, N_SHARD))],
    )(*grads)


def _rs_chips(parts):
    n = len(parts)

    def body(*refs):
        ins, outs = refs[:n], refs[n:2 * n]
        send_sems, recv_sems = refs[2 * n:]
        x, y, c, chips = _place()
        sends = []
        for k in range(n):
            for j, (px, py) in enumerate(chips):
                cp = pltpu.make_async_remote_copy(
                    src_ref=ins[k].at[2 * px + py], dst_ref=outs[k].at[j], send_sem=send_sems.at[k, j],
                    recv_sem=recv_sems.at[k, j], device_id=(px, py, c), device_id_type=MESH)
                cp.start()
                sends.append(cp)
        for cp in sends:
            cp.wait_recv()
        _wait_all([], sends)

    return pl.pallas_call(
        body, name="rs_chips", in_specs=[ANY] * n, out_specs=[ANY] * n,
        out_shape=[jax.ShapeDtypeStruct((3,) + a.shape[1:], a.dtype) for a in parts],
        scratch_shapes=[pltpu.SemaphoreType.DMA((n, 3)), pltpu.SemaphoreType.DMA((n, 3))],
    )(*parts)


def _rs_finish(bufs):
    n = len(bufs)

    def body(*refs):
        outs = refs[n:2 * n]
        send_sems, recv_sems = refs[2 * n:]
        c = lax.axis_index("c")
        sends = []
        for k in range(n):
            cp = pltpu.make_async_remote_copy(
                src_ref=outs[k].at[c], dst_ref=outs[k].at[c], send_sem=send_sems.at[k], recv_sem=recv_sems.at[k],
                device_id=_sibling(), device_id_type=MESH)
            cp.start()
            sends.append(cp)
        for k in range(n):
            pltpu.make_async_remote_copy(
                src_ref=outs[k].at[c], dst_ref=outs[k].at[1 - c], send_sem=send_sems.at[k], recv_sem=recv_sems.at[k],
                device_id=_sibling(), device_id_type=MESH).wait_recv()
        _wait_all([], sends)

    return pl.pallas_call(
        body, name="rs_finish", in_specs=[ANY] * n, out_specs=[ANY] * n,
        out_shape=[jax.ShapeDtypeStruct(a.shape, a.dtype) for a in bufs], input_output_aliases={k: k for k in range(n)},
        scratch_shapes=[pltpu.SemaphoreType.DMA((n,)), pltpu.SemaphoreType.DMA((n,))],
    )(*bufs)


def _sum3d(name, terms, scalars, grid_lead, out_lead, out_index, tr=None, out_dtype=F32):
    h, c = terms[0][0].shape[1:]
    tr = tr or h

    def body(s_ref, *refs):
        acc = refs[0][...].astype(F32)
        for ref in refs[1:-1]:
            acc = acc + ref[...].astype(F32)
        refs[-1][...] = acc.astype(refs[-1].dtype)

    in_specs = [pl.BlockSpec((1, tr, c), functools.partial(lambda l, i, s_ref, f: (f(l, s_ref), i, 0), f=f)) for _, f in terms]
    out_spec = pl.BlockSpec((1, tr, c), lambda l, i, s_ref: (out_index(l, s_ref), i, 0))
    return pl.pallas_call(
        body, name=name,
        grid_spec=pltpu.PrefetchScalarGridSpec(num_scalar_prefetch=1, grid=(grid_lead, h // tr), in_specs=in_specs, out_specs=out_spec),
        out_shape=jax.ShapeDtypeStruct((out_lead, h, c), out_dtype), compiler_params=_params(("arbitrary", "arbitrary")),
    )(scalars, *[a for a, _ in terms])


def _reduce_scatter(grads, place):
    g8 = [a.reshape(2 * N_SHARD, a.shape[0] // (2 * N_SHARD), a.shape[1]) for a in grads]
    from_sibling = _rs_sibling(g8)
    own = lambda l, s: 2 * l + s[0]
    parts = [_sum3d(f"rs_add1_{k}", [(g8[k], own), (from_sibling[k], lambda l, s: l)], place, N_SHARD, N_SHARD,
                    lambda l, s: l, tr=g8[k].shape[1] // 2, out_dtype=BF16)
             for k in range(len(grads))]
    from_chips = _rs_chips(parts)
    mine = [(lambda l, s: 2 * s[1] + s[0]), (lambda l, s: s[1])]
    bufs = [_sum3d(f"rs_add2_{k}", [(g8[k], mine[0]), (from_sibling[k], mine[1])]
                   + [(from_chips[k], functools.partial(lambda l, s, j: j, j=j)) for j in range(3)],
                   place, 1, 2, lambda l, s: s[0], tr=g8[k].shape[1] // 2)
            for k in range(len(grads))]
    whole = _rs_finish(bufs)
    return [w.reshape(2 * w.shape[1], w.shape[2]) for w in whole]


def _device_step(x, target, p):
    t = x.shape[0]
    d = D_MODEL
    tok = lambda arr, c=0, w=d: (arr, w, c)
    f32x = lambda n: [(d, F32)] * n
    rp_params = [p["w0"], p["w2p"], p["a0"], p["a2p"], p["g2p"], p["k_k"], p["k_a"]]
    post_params = [p["ln_w"], p["ln_b"], p["r_k"]]
    g = {}

    (xn,) = _tok_fwd("norm1_fwd", _fn_norm, [tok(x)], [p["g1"]], [(d, BF16)])
    z = _mm("in_proj", xn, p["w_in_t"], "nt", tm=t, tn=256)
    q, logf, kh = _tok_fwd("hgates_fwd", _fn_hgates, [tok(z, 0), tok(z, 1)], [p["lb2"]], f32x(3))
    o_raw, h_states = _hgrn_fwd(q, logf, kh, z)
    zl = _lerp_fwd(z, p["mu"])
    lora = tok(zl, 3 * d // LORA, LORA)
    decay, kr2, avec, bvec, gate = _tok_fwd("rprep_fwd", _fn_rprep, [tok(zl, 1), lora], rp_params, f32x(5))
    y, r_states = _rwkv_fwd(zl, decay, kr2, avec, bvec)
    (o_a,) = _tok_fwd("hpost_fwd", _fn_hpost, [tok(o_raw), tok(z, 3)], [p["gnorm"]], [(d, BF16)])
    post_toks = [tok(y), tok(zl, 0), tok(kr2), tok(zl, 2), tok(gate)]
    (o_b,) = _tok_fwd("rpost_fwd", _fn_rpost, post_toks, post_params, [(d, BF16)])
    y_a = _mm("branch_a", o_a, p["w_a"], "nn")
    y_b = _mm("branch_b", o_b, p["w_b"], "nn")
    merge_toks = [tok(z, C_G // 256, 256), tok(z, (C_G + d) // 256, 256), tok(y_a, 0, 256), tok(y_b, 0, 256)]
    (merged,) = _tok_fwd("merge_fwd", _fn_merge, merge_toks, [], [(256, BF16)], col_grid=4)
    mix = _mm("out_proj", merged, p["w_out"], "nn")
    h1, xn2 = _tok_fwd("res1_fwd", _fn_res1, [tok(x), tok(mix)], [p["g_post1"], p["g_pre2"]], [(d, F32), (d, BF16)])
    hu = _mm("up_proj", xn2, p["w_up_t"], "nt", tm=t, tn=512)
    act = _conv_fwd(hu, p["conv_w"], p["conv_b"])
    ff = _mm("down_proj", act, p["w_down"], "nn")
    d_h1, d_ff, loss, g["g_post2"] = _loss_head(h1, ff, target, p["g_post2"])

    d_act = _mm("d_act", d_ff, p["w_down"], "nt", tn=256)
    g["w_down"] = _mm("dw_down", act, d_ff, "tn", tm=256, tn=1024)
    d_hu, g["conv_w"], g["conv_b"] = _conv_bwd(hu, p["conv_w"], p["conv_b"], d_act)
    d_xn2 = _mm("d_xn2", d_hu, p["w_up_t"], "nn", tk=D_FF)
    g["w_up_t"] = _mm("dw_up", d_hu, xn2, "tn", tm=512, tn=1024)
    d_x_res, d_mix, g["g_post1"], g["g_pre2"] = _tok_bwd(
        "res1_bwd", _fn_res1, [tok(x), tok(mix)], [p["g_post1"], p["g_pre2"]], [[tok(d_h1)], [tok(d_xn2)]],
        [(d, F32), (d, BF16)])
    d_merged = _mm("d_merged", d_mix, p["w_out"], "nt")
    g["w_out"] = _mm("dw_out", merged, d_mix, "tn")
    d_ga, d_gb, d_ya, d_yb = _tok_bwd("merge_bwd", _fn_merge, merge_toks, [], [[tok(d_merged, 0, 256)]],
                                      [(256, BF16)] * 4, col_grid=4)
    d_oa = _mm("d_oa", d_ya, p["w_a"], "nt")
    g["w_a"] = _mm("dw_a", o_a, d_ya, "tn")
    d_ob = _mm("d_ob", d_yb, p["w_b"], "nt")
    g["w_b"] = _mm("dw_b", o_b, d_yb, "tn")
    d_oraw, d_hg, g["gnorm"] = _tok_bwd("hpost_bwd", _fn_hpost, [tok(o_raw), tok(z, 3)], [p["gnorm"]], [[tok(d_oa)]],
                                        [(d, F32), (d, BF16)])
    d_y, d_r1, d_kr2_1, d_v1, d_gate, g["ln_w"], g["ln_b"], g["r_k"] = _tok_bwd(
        "rpost_bwd", _fn_rpost, post_toks, post_params, [[tok(d_ob)]], f32x(5))
    d_r2, d_decay, d_kr2_2, d_v2, d_avec, d_bvec = _rwkv_bwd(zl, decay, kr2, avec, bvec, r_states, d_y)
    prep = _tok_bwd("rprep_bwd", _fn_rprep, [tok(zl, 1), lora], rp_params,
                    [[tok(d_decay)], [tok(d_kr2_1), tok(d_kr2_2)], [tok(d_avec)], [tok(d_bvec)], [tok(d_gate)]],
                    [(d, F32), (LORA, F32)])
    d_kr, d_lora = prep[:2]
    g["w0"], g["w2p"], g["a0"], g["a2p"], g["g2p"], g["k_k"], g["k_a"] = prep[2:]
    d_zl = jnp.concatenate([d_r1 + d_r2, d_kr, d_v1 + d_v2, d_lora], axis=1)
    dz_r, g["mu"] = _lerp_bwd(z, p["mu"], d_zl)
    d_q, d_logf, d_kh, d_vi = _hgrn_bwd(q, logf, kh, z, h_states, d_oraw)
    d_hq, d_hf, g["lb2"] = _tok_bwd("hgates_bwd", _fn_hgates, [tok(z, 0), tok(z, 1)], [p["lb2"]],
                                    [[tok(d_q)], [tok(d_logf)], [tok(d_kh)]], [(d, BF16)] * 2)
    dz = jnp.concatenate([d_hq, d_hf, d_vi.astype(BF16), d_hg, dz_r, d_ga, d_gb], axis=1)
    d_xn = _mm("d_xn", dz, p["w_in_t"], "nn", tk=IN_COLS // 2)
    g["w_in_t"] = _mm("dw_in", dz, xn, "tn", tm=256, tn=1024)
    grad_x, g["g1"] = _tok_bwd("norm1_bwd", _fn_norm, [tok(x)], [p["g1"]], [[tok(d_xn)]], [(d, F32)],
                               add_to_first=tok(d_x_res))
    return loss, grad_x, g


_WEIGHTS = ["attn_pre_norm", "w_in", "hgrn_lb", "hgrn_gnorm", "w_branch_a", "rwkv_mu", "rwkv_w0", "rwkv_w2", "rwkv_a0",
            "rwkv_a2", "rwkv_g2", "rwkv_k_k", "rwkv_k_a", "rwkv_r_k", "rwkv_ln_w", "rwkv_ln_b", "w_branch_b", "w_out",
            "attn_post_norm", "ffn_pre_norm", "w_up", "conv_w", "conv_b", "w_down", "ffn_post_norm"]
_REPLICATED = [("attn_pre_norm", "g1"), ("hgrn_lb", "lb2"), ("hgrn_gnorm", "gnorm"), ("rwkv_mu", "mu"), ("rwkv_w0", "w0"),
               ("rwkv_a0", "a0"), ("rwkv_k_k", "k_k"), ("rwkv_k_a", "k_a"), ("rwkv_r_k", "r_k"), ("rwkv_ln_w", "ln_w"),
               ("rwkv_ln_b", "ln_b"), ("attn_post_norm", "g_post1"), ("ffn_pre_norm", "g_pre2"), ("conv_b", "conv_b"),
               ("ffn_post_norm", "g_post2")]
SLAB_COLS = 1024


def _pack(arrays):
    pieces, total = [], 0
    for a in arrays:
        flat = a.reshape(-1)
        rows = -(-flat.shape[0] // SLAB_COLS)
        pieces.append(jnp.pad(flat, (0, rows * SLAB_COLS - flat.shape[0])).reshape(rows, SLAB_COLS))
        total += rows
    if total % 8:
        pieces.append(jnp.zeros((8 - total % 8, SLAB_COLS), F32))
    return jnp.concatenate(pieces, axis=0)


def _unpack(slab, shapes):
    out, at = [], 0
    for s in shapes:
        size = 1
        for dim in s:
            size *= dim
        rows = -(-size // SLAB_COLS)
        out.append(slab[at:at + rows].reshape(-1)[:size].reshape(s))
        at += rows
    return out


def kernel(x, attn_pre_norm, w_in, hgrn_lb, hgrn_gnorm, w_branch_a, rwkv_mu, rwkv_w0, rwkv_w2, rwkv_a0, rwkv_a2, rwkv_g2, rwkv_k_k, rwkv_k_a, rwkv_r_k, rwkv_ln_w, rwkv_ln_b, w_branch_b, w_out, attn_post_norm, ffn_pre_norm, w_up, conv_w, conv_b, w_down, ffn_post_norm, loss_target, m_attn_pre_norm, m_w_in, m_hgrn_lb, m_hgrn_gnorm, m_w_branch_a, m_rwkv_mu, m_rwkv_w0, m_rwkv_w2, m_rwkv_a0, m_rwkv_a2, m_rwkv_g2, m_rwkv_k_k, m_rwkv_k_a, m_rwkv_r_k, m_rwkv_ln_w, m_rwkv_ln_b, m_w_branch_b, m_w_out, m_attn_post_norm, m_ffn_pre_norm, m_w_up, m_conv_w, m_conv_b, m_w_down, m_ffn_post_norm, v_attn_pre_norm, v_w_in, v_hgrn_lb, v_hgrn_gnorm, v_w_branch_a, v_rwkv_mu, v_rwkv_w0, v_rwkv_w2, v_rwkv_a0, v_rwkv_a2, v_rwkv_g2, v_rwkv_k_k, v_rwkv_k_a, v_rwkv_r_k, v_rwkv_ln_w, v_rwkv_ln_b, v_w_branch_b, v_w_out, v_attn_post_norm, v_ffn_pre_norm, v_w_up, v_conv_w, v_conv_b, v_w_down, v_ffn_post_norm):
    given = dict(locals())
    w = {n: given[n] for n in _WEIGHTS}
    mom = {n: given["m_" + n] for n in _WEIGHTS}
    var = {n: given["v_" + n] for n in _WEIGHTS}
    shard = 2 * lax.axis_index("x") + lax.axis_index("y")
    place = jnp.stack([lax.axis_index("c"), shard]).astype(jnp.int32)
    row = lambda a: a.reshape(1, -1)
    lora_of = lambda d: jnp.concatenate([d["rwkv_w2"][0], d["rwkv_a2"][0], d["rwkv_g2"][0]], axis=0)

    shards = [w["w_in"][0].T.astype(BF16), w["w_branch_a"][0].astype(BF16), w["w_branch_b"][0].astype(BF16),
              w["w_out"][0].astype(BF16), w["w_down"][0].astype(BF16), w["w_up"][0].T.astype(BF16),
              lora_of(w), jnp.pad(w["conv_w"][0], ((0, 29), (0, 0)))]
    placed = [_place_shard(f"place_{k}", a, place) for k, a in enumerate(shards)]
    w_in_t, w_a, w_b, w_o, w_dn, w_up_t, lora_g, conv_g = _gather_shards(shards, placed)
    lora_full = lora_g.reshape(N_SHARD, LORA, 256).transpose(1, 0, 2).reshape(LORA, D_MODEL)
    conv_full = conv_g.reshape(N_SHARD, 32, 2 * D_FF // N_SHARD)[:, :3].transpose(1, 0, 2).reshape(3, 2 * D_FF)
    lrow = lax.broadcasted_iota(jnp.int32, (LORA, 1), 0)
    p = {
        "g1": row(w["attn_pre_norm"]), "lb2": w["hgrn_lb"], "gnorm": row(w["hgrn_gnorm"]), "w_in_t": w_in_t, "w_a": w_a,
        "mu": row(w["rwkv_mu"]), "w0": row(w["rwkv_w0"]), "a0": row(w["rwkv_a0"]),
        "w2p": jnp.where(lrow < 64, lora_full, 0.0), "a2p": jnp.where((lrow >= 64) & (lrow < 128), lora_full, 0.0),
        "g2p": jnp.where(lrow >= 128, lora_full, 0.0),
        "k_k": row(w["rwkv_k_k"]), "k_a": row(w["rwkv_k_a"]), "r_k": row(w["rwkv_r_k"]), "ln_w": row(w["rwkv_ln_w"]),
        "ln_b": row(w["rwkv_ln_b"]), "w_b": w_b, "w_out": w_o, "g_post1": row(w["attn_post_norm"]),
        "g_pre2": row(w["ffn_pre_norm"]), "w_up_t": w_up_t, "conv_w": conv_full, "conv_b": row(w["conv_b"]),
        "w_down": w_dn, "g_post2": row(w["ffn_post_norm"]),
    }

    loss, grad_x, g = _device_step(x[0], loss_target[0], p)

    g_in_t, g_a, g_b, g_o, g_dn, g_up_t = _reduce_scatter(
        [g["w_in_t"], g["w_a"], g["w_b"], g["w_out"], g["w_down"], g["w_up_t"]], place)
    rep_shapes = [w[n].shape for n, _ in _REPLICATED]
    rep = _pack([g[key] for _, key in _REPLICATED])
    n_rep_rows = rep.shape[0]
    cw = 2 * D_FF // N_SHARD
    lora_rows, conv_rows = LORA * 256 // SLAB_COLS, -(-3 * cw // SLAB_COLS)
    lora_g = jnp.concatenate([g["w2p"][0:64], g["a2p"][64:128], g["g2p"][128:256]], axis=0)
    lora_parts = lora_g.reshape(LORA, N_SHARD, 256).transpose(1, 0, 2).reshape(N_SHARD, lora_rows, SLAB_COLS)
    conv_parts = g["conv_w"].reshape(3, N_SHARD, cw).transpose(1, 0, 2).reshape(N_SHARD, 3 * cw)
    conv_parts = jnp.pad(conv_parts, ((0, 0), (0, conv_rows * SLAB_COLS - 3 * cw))).reshape(N_SHARD, conv_rows, SLAB_COLS)
    n_rows = n_rep_rows + lora_rows + conv_rows
    fill = jnp.zeros((N_SHARD, -n_rows % 8, SLAB_COLS), F32)
    parts = jnp.concatenate([jnp.broadcast_to(rep, (N_SHARD,) + rep.shape), lora_parts, conv_parts, fill], axis=1)
    me = 4 * lax.axis_index("x") + 2 * lax.axis_index("y") + lax.axis_index("c")
    landing = lax.dynamic_update_slice(jnp.zeros((8,) + parts.shape[1:], F32),
                                       lax.dynamic_index_in_dim(parts, shard, 0, keepdims=True), (me, 0, 0))
    gathered = _exchange8(parts, landing)
    summed = _sum3d("small_sum", [(gathered, functools.partial(lambda l, s, i: i, i=i)) for i in range(8)], place, 1, 1,
                    lambda l, s: 0)[0]
    lora_grad = summed[n_rep_rows:n_rep_rows + lora_rows].reshape(LORA, 256)
    conv_grad = summed[n_rep_rows + lora_rows:n_rows].reshape(-1)[:3 * cw].reshape(3, cw)

    res = {}

    def put(name, outs, shape=None):
        res[name] = [o.reshape(w[name].shape if shape is None else shape) for o in outs]

    put("w_in", _adamw("adamw_w_in", w["w_in"][0], g_in_t, mom["w_in"][0], var["w_in"][0], 1024, 128, g_transposed=True))
    put("w_up", _adamw("adamw_w_up", w["w_up"][0], g_up_t, mom["w_up"][0], var["w_up"][0], 1024, 128, g_transposed=True))
    for name, grad in (("w_branch_a", g_a), ("w_branch_b", g_b), ("w_out", g_o)):
        put(name, _adamw("adamw_" + name, w[name][0], grad, mom[name][0], var[name][0], 256, 1024))
    put("w_down", _adamw("adamw_w_down", w["w_down"][0], g_dn, mom["w_down"][0], var["w_down"][0], 176, 1024))
    put("conv_w", _adamw("adamw_conv_w", w["conv_w"][0], conv_grad, mom["conv_w"][0], var["conv_w"][0], 3, 2 * D_FF // N_SHARD))
    lora_out = _adamw("adamw_lora", lora_of(w), lora_grad, lora_of(mom), lora_of(var), LORA, 256)
    for name, lo, hi in (("rwkv_w2", 0, 64), ("rwkv_a2", 64, 128), ("rwkv_g2", 128, 256)):
        put(name, [o[lo:hi] for o in lora_out])
    rep_names = [n for n, _ in _REPLICATED]
    rep_out = _adamw("adamw_small", _pack([w[n] for n in rep_names]), summed[:n_rep_rows], _pack([mom[n] for n in rep_names]),
                     _pack([var[n] for n in rep_names]), n_rep_rows, SLAB_COLS)
    for name, parts in zip(rep_names, zip(*[_unpack(o, rep_shapes) for o in rep_out])):
        put(name, list(parts))

    loss = lax.psum(loss[0, 0], ("x", "y", "c"))
    return (loss, grad_x[None], *[res[n][0] for n in _WEIGHTS], *[res[n][1] for n in _WEIGHTS],
            *[res[n][2] for n in _WEIGHTS], *[res[n][3] for n in _WEIGHTS])
```

```python
import functools

import jax
import jax.numpy as jnp
from jax import lax
from jax.experimental import pallas as pl
from jax.experimental.pallas import tpu as pltpu

F32, BF16 = jnp.float32, jnp.bfloat16
MESH = pl.DeviceIdType.MESH

D_MODEL = 1024
HGRN_HEADS = 8
HGRN_K = 128
HGRN_SCALE = HGRN_K ** -0.5
CHUNK = 32
RWKV_HEAD = 64
LORA = 256
D_FF = 2816
EPS = 1e-6
GN_EPS = 1e-5 * RWKV_HEAD
N_SHARD = 4
ADAM_LR, ADAM_B1, ADAM_B2, ADAM_EPS, ADAM_WD, ADAM_STEP = 0.001, 0.9, 0.999, 1e-08, 0.01, 10

LANES = 128
VMEM_LIMIT = 56 * 1024 * 1024
SCAN_TB = 16
SCAN_GROUP = 256

C_HQ, C_HF, C_HI, C_HG = 0, 1024, 2048, 3072
C_R = 4096
R_COLS = 3328
C_G = 7424
IN_COLS = 9472


def _params(sem=None, **kw):
    return pltpu.CompilerParams(dimension_semantics=sem, vmem_limit_bytes=VMEM_LIMIT, **kw)


def _seg_matrix(n, seg):
    r = lax.broadcasted_iota(jnp.int32, (n, n), 0) // seg
    c = lax.broadcasted_iota(jnp.int32, (n, n), 1) // seg
    return (r == c).astype(BF16)


def _split3(x):
    hi = x.astype(BF16)
    r1 = x - hi.astype(F32)
    mid = r1.astype(BF16)
    lo = (r1 - mid.astype(F32)).astype(BF16)
    return hi, mid, lo


def _segsum_impl(x, seg):
    e = _seg_matrix(LANES, seg)
    outs = []
    for g in range(x.shape[1] // LANES):
        hi, mid, lo = _split3(x[:, g * LANES:(g + 1) * LANES])
        outs.append(jnp.dot(hi, e, preferred_element_type=F32) + jnp.dot(mid, e, preferred_element_type=F32)
                    + jnp.dot(lo, e, preferred_element_type=F32))
    return outs[0] if len(outs) == 1 else jnp.concatenate(outs, axis=1)


def _make_segsum(seg):
    @jax.custom_vjp
    def f(x):
        return _segsum_impl(x, seg)

    f.defvjp(lambda x: (_segsum_impl(x, seg), None), lambda _, ct: (_segsum_impl(ct, seg),))
    return f


_segsum64 = _make_segsum(RWKV_HEAD)
_segsum128 = _make_segsum(HGRN_K)


def _chunk_mm_impl(x, kind, transposed):
    n = x.shape[0]
    r = lax.broadcasted_iota(jnp.int32, (n, n), 1 if transposed else 0)
    c = lax.broadcasted_iota(jnp.int32, (n, n), 0 if transposed else 1)
    same = (r // CHUNK) == (c // CHUNK)
    if kind == "cumsum":
        m = same & (r >= c)
    else:
        m = same & (c % CHUNK == (CHUNK // 2 - 1 if kind == "mid" else CHUNK - 1))
    m = m.astype(BF16)
    hi, mid, lo = _split3(x)
    return (jnp.dot(m, hi, preferred_element_type=F32) + jnp.dot(m, mid, preferred_element_type=F32)
            + jnp.dot(m, lo, preferred_element_type=F32))


def _make_chunk_mm(kind):
    @jax.custom_vjp
    def f(x):
        return _chunk_mm_impl(x, kind, False)

    f.defvjp(lambda x: (_chunk_mm_impl(x, kind, False), None), lambda _, ct: (_chunk_mm_impl(ct, kind, True),))
    return f


_chunk_cumsum = _make_chunk_mm("cumsum")
_chunk_mid = _make_chunk_mm("mid")
_chunk_last = _make_chunk_mm("last")


@jax.custom_vjp
def _bdot(x, w):
    return jnp.dot(x.astype(BF16), w.astype(BF16), preferred_element_type=F32)


def _bdot_fwd(x, w):
    return _bdot(x, w), (x, w)


def _bdot_bwd(res, ct):
    x, w = res
    ctb = ct.astype(BF16)
    dx = lax.dot_general(ctb, w.astype(BF16), (((1,), (1,)), ((), ())), preferred_element_type=F32)
    dw = lax.dot_general(x.astype(BF16), ctb, (((0,), (0,)), ((), ())), preferred_element_type=F32)
    return dx, dw


_bdot.defvjp(_bdot_fwd, _bdot_bwd)


def _sigmoid(x):
    return 1.0 / (1.0 + jnp.exp(-x))


def _silu(x):
    return x * _sigmoid(x)


def _softplus(x):
    return jnp.maximum(x, 0.0) + jnp.log(1.0 + jnp.exp(-jnp.abs(x)))


def _rms(x, g):
    return x * lax.rsqrt(jnp.mean(x * x, axis=-1, keepdims=True) + EPS) * g


def _fn_norm(t, p):
    return [_rms(t[0], p[0])]


def _fn_hgates(t, p):
    hq, hf = t
    lb2 = p[0]
    m = jnp.max(lb2, axis=0, keepdims=True)
    e = jnp.exp(lb2 - m)
    first = lax.broadcasted_iota(jnp.int32, e.shape, 0) == 0
    lb = jnp.sum(jnp.where(first, e, 0.0), axis=0, keepdims=True) / jnp.sum(e, axis=0, keepdims=True)
    f = lb + (1.0 - lb) * _sigmoid(hf)
    q, k = _silu(hq) * HGRN_SCALE, 1.0 - f
    b = _chunk_cumsum(jnp.log(f))
    b_ref, b_last = _chunk_mid(b), _chunk_last(b)
    return [q * jnp.exp(b - b_ref), k * jnp.exp(b_ref - b), k * jnp.exp(b_last - b), q * jnp.exp(b), jnp.exp(b_last)]


def _fn_hpost(t, p):
    o, hg = t
    ms = _segsum128(o * o) * (1.0 / HGRN_K)
    return [o * lax.rsqrt(ms + EPS) * p[0] * _silu(hg)]


def _fn_rprep(t, p):
    kr, lora = t
    w0, w2p, a0, a2p, g2p, k_k, k_a = p
    pre_w = w0 + _bdot(jnp.tanh(lora), w2p)
    w_log = -_softplus(-pre_w) - 0.5
    decay = jnp.exp(-jnp.exp(w_log))
    a = _sigmoid(a0 + _bdot(lora, a2p))
    g = _bdot(_sigmoid(lora), g2p)
    kk = kr * k_k
    kk = kk / jnp.maximum(jnp.sqrt(_segsum64(kk * kk)), 1e-12)
    kr2 = kr * (1.0 + (a - 1.0) * k_a)
    return [decay, kr2, -kk, kk * a, g]


def _fn_rpost(t, p):
    y, r, kr2, v, g = t
    ln_w, ln_b, r_k = p
    mu = _segsum64(y) * (1.0 / RWKV_HEAD)
    yc = y - mu
    var = _segsum64(yc * yc) * (1.0 / RWKV_HEAD)
    yn = yc * lax.rsqrt(var + GN_EPS) * ln_w + ln_b
    bonus = _segsum64(r * kr2 * r_k) * v
    return [(yn + bonus) * g]


def _fn_merge(t, p):
    ga, gb, ya, yb = t
    return [_sigmoid(ga) * ya + _sigmoid(gb) * yb]


def _fn_res1(t, p):
    x, mix = t
    h1 = x + _rms(mix, p[0])
    return [h1, _rms(h1, p[1])]


def _tok_call(name, fn, toks, params, outs, red_shapes=(), tm=128, col_grid=1):
    n_t, n_p, n_o = len(toks), len(params), len(outs)
    t_len = toks[0][0].shape[0]

    def body(*refs):
        tv = [r[...].astype(F32) for r in refs[:n_t]]
        pv = [r[...] for r in refs[n_t:n_t + n_p]]
        o, red = fn(tv, pv)
        for ref, val in zip(refs[n_t + n_p:n_t + n_p + n_o], o):
            ref[...] = val.astype(ref.dtype)
        red_refs = refs[n_t + n_p + n_o:]
        if red_refs:
            first = pl.program_id(0) == 0

            @pl.when(first)
            def _():
                for ref, val in zip(red_refs, red):
                    ref[...] = val

            @pl.when(jnp.logical_not(first))
            def _():
                for ref, val in zip(red_refs, red):
                    ref[...] += val

    in_specs = [pl.BlockSpec((tm, w), functools.partial(lambda i, j, c: (i, c + j), c=c)) for (_, w, c) in toks]
    in_specs += [pl.BlockSpec(p.shape, lambda i, j: (0, 0)) for p in params]
    out_specs = [pl.BlockSpec((tm, w), lambda i, j: (i, j)) for (w, _) in outs]
    out_specs += [pl.BlockSpec(s, lambda i, j: (0, 0)) for s in red_shapes]
    out_shape = [jax.ShapeDtypeStruct((t_len, w * col_grid), dt) for (w, dt) in outs]
    out_shape += [jax.ShapeDtypeStruct(s, F32) for s in red_shapes]
    return pl.pallas_call(
        body, name=name, grid=(t_len // tm, col_grid), in_specs=in_specs, out_specs=out_specs, out_shape=out_shape,
        compiler_params=_params(("arbitrary", "arbitrary")),
    )(*[a for (a, _, _) in toks], *params)


def _tok_fwd(name, fn, toks, params, outs, **kw):
    return _tok_call(name, lambda tv, pv: (fn(tv, pv), []), toks, params, outs, **kw)


def _tok_bwd(name, fn, toks, params, cts, want, add_to_first=None, **kw):
    n_t = len(toks)
    flat = [c for group in cts for c in group]
    extra = [] if add_to_first is None else [add_to_first]

    def bwd(tv, pv):
        prim, rest = tv[:n_t], tv[n_t:]
        ct, at = [], 0
        for group in cts:
            ct.append(functools.reduce(lambda u, v: u + v, rest[at:at + len(group)]))
            at += len(group)
        _, vjp = jax.vjp(lambda *a: tuple(fn(list(a[:n_t]), list(a[n_t:]))), *prim, *pv)
        g = vjp(tuple(ct))
        tok_grads = [g[i] for i in range(n_t) if want[i] is not None]
        if extra:
            tok_grads[0] = tok_grads[0] + rest[at]
        return tok_grads, list(g[n_t:])

    return _tok_call(name, bwd, list(toks) + flat + extra, params, [w for w in want if w is not None],
                     red_shapes=[p.shape for p in params], **kw)


def _mm(name, a, b, mode, out_dtype=F32, tm=None, tn=None, tk=None):
    if mode == "nn":
        (m, k), (_, n) = a.shape, b.shape
    elif mode == "nt":
        (m, k), (n, _) = a.shape, b.shape
    else:
        (k, m), (_, n) = a.shape, b.shape
    tm = (512 if mode == "tn" else 2048) if tm is None else tm
    tn = (512 if mode == "tn" else 256) if tn is None else tn
    tk = k if tk is None else tk
    tm, tn = min(tm, m), min(tn, n)
    nk = k // tk
    assert m % tm == 0 and n % tn == 0 and k % tk == 0, (name, a.shape, b.shape, tm, tn, tk)
    a_spec = pl.BlockSpec((tk, tm), lambda i, j, q: (q, i)) if mode == "tn" else pl.BlockSpec((tm, tk), lambda i, j, q: (i, q))
    b_spec = pl.BlockSpec((tn, tk), lambda i, j, q: (j, q)) if mode == "nt" else pl.BlockSpec((tk, tn), lambda i, j, q: (q, j))
    dn = {"nn": (((1,), (0,)), ((), ())), "nt": (((1,), (1,)), ((), ())), "tn": (((0,), (0,)), ((), ()))}[mode]

    def body(a_ref, b_ref, o_ref, *acc):
        p = lax.dot_general(a_ref[...], b_ref[...], dn, preferred_element_type=F32)
        if nk == 1:
            o_ref[...] = p.astype(o_ref.dtype)
        else:
            q = pl.program_id(2)

            @pl.when(q == 0)
            def _():
                acc[0][...] = p

            @pl.when(q > 0)
            def _():
                acc[0][...] += p

            @pl.when(q == nk - 1)
            def _():
                o_ref[...] = acc[0][...].astype(o_ref.dtype)

    return pl.pallas_call(
        body, name=name, grid=(m // tm, n // tn, nk), in_specs=[a_spec, b_spec],
        out_specs=pl.BlockSpec((tm, tn), lambda i, j, q: (i, j)), out_shape=jax.ShapeDtypeStruct((m, n), out_dtype),
        scratch_shapes=[pltpu.VMEM((tm, tn), F32)] if nk > 1 else [],
        compiler_params=_params(("parallel", "parallel", "arbitrary")),
    )(a, b)


def _shift_down(z, n):
    rows = lax.broadcasted_iota(jnp.int32, z.shape, 0)
    return jnp.where(rows < n, 0.0, pltpu.roll(z, n, 0))


def _shift_up(z, n):
    t = z.shape[0]
    rows = lax.broadcasted_iota(jnp.int32, z.shape, 0)
    return jnp.where(rows >= t - n, 0.0, pltpu.roll(z, t - n, 0))


def _lerp_fwd(z, mu):
    t = z.shape[0]
    w = 256

    def body(z_ref, mu_ref, o_ref):
        zz = z_ref[...]
        o_ref[...] = zz + mu_ref[...] * (_shift_down(zz, 1) - zz)

    return pl.pallas_call(
        body, name="lerp_fwd", grid=(R_COLS // w,),
        in_specs=[pl.BlockSpec((t, w), lambda j: (0, C_R // w + j)), pl.BlockSpec((1, w), lambda j: (0, j))],
        out_specs=pl.BlockSpec((t, w), lambda j: (0, j)), out_shape=jax.ShapeDtypeStruct((t, R_COLS), F32),
        compiler_params=_params(("parallel",)),
    )(z, mu)


def _lerp_bwd(z, mu, dzl):
    t = z.shape[0]
    w = 256

    def body(z_ref, mu_ref, d_ref, dz_ref, dmu_ref):
        zz, m, d = z_ref[...], mu_ref[...], d_ref[...]
        dz_ref[...] = (d * (1.0 - m) + _shift_up(d * m, 1)).astype(dz_ref.dtype)
        dmu_ref[...] = jnp.sum(d * (_shift_down(zz, 1) - zz), axis=0, keepdims=True)

    return pl.pallas_call(
        body, name="lerp_bwd", grid=(R_COLS // w,),
        in_specs=[pl.BlockSpec((t, w), lambda j: (0, C_R // w + j)), pl.BlockSpec((1, w), lambda j: (0, j)),
                  pl.BlockSpec((t, w), lambda j: (0, j))],
        out_specs=[pl.BlockSpec((t, w), lambda j: (0, j)), pl.BlockSpec((1, w), lambda j: (0, j))],
        out_shape=[jax.ShapeDtypeStruct((t, R_COLS), BF16), jax.ShapeDtypeStruct((1, R_COLS), F32)],
        compiler_params=_params(("parallel",)),
    )(z, mu, dzl)


CONV_TILE = 256
N_CONV_TILES = D_FF // CONV_TILE


def _conv(h, w, b):
    return b + w[0:1, :] * _shift_down(h, 2) + w[1:2, :] * _shift_down(h, 1) + w[2:3, :] * h


def _conv_fwd(hu, conv_w, conv_b):
    t = hu.shape[0]
    n = N_CONV_TILES

    def body(hg_ref, hv_ref, wg_ref, wv_ref, bg_ref, bv_ref, o_ref):
        gate = _conv(hg_ref[...], wg_ref[...], bg_ref[...])
        val = _conv(hv_ref[...], wv_ref[...], bv_ref[...])
        o_ref[...] = (_silu(gate) * val).astype(o_ref.dtype)

    col = lambda off: pl.BlockSpec((t, CONV_TILE), lambda j: (0, j + off))
    wspec = lambda off: pl.BlockSpec((3, CONV_TILE), lambda j: (0, j + off))
    bspec = lambda off: pl.BlockSpec((1, CONV_TILE), lambda j: (0, j + off))
    return pl.pallas_call(
        body, name="conv_fwd", grid=(n,),
        in_specs=[col(0), col(n), wspec(0), wspec(n), bspec(0), bspec(n)],
        out_specs=pl.BlockSpec((t, CONV_TILE), lambda j: (0, j)), out_shape=jax.ShapeDtypeStruct((t, D_FF), BF16),
        compiler_params=_params(("parallel",)),
    )(hu, hu, conv_w, conv_w, conv_b, conv_b)


def _conv_bwd(hu, conv_w, conv_b, d_act):
    t = hu.shape[0]
    n = N_CONV_TILES

    def body(hg_ref, hv_ref, wg_ref, wv_ref, bg_ref, bv_ref, d_ref, dh_ref, dw_ref, db_ref):
        is_gate = pl.program_id(0) < n
        hg, hv = hg_ref[...], hv_ref[...]
        gate = _conv(hg, wg_ref[...], bg_ref[...])
        val = _conv(hv, wv_ref[...], bv_ref[...])
        d = d_ref[...]
        sg = _sigmoid(gate)
        d_gate = d * val * (sg * (1.0 + gate * (1.0 - sg)))
        d_val = d * (gate * sg)
        dc = jnp.where(is_gate, d_gate, d_val)
        h = jnp.where(is_gate, hg, hv)
        w = jnp.where(is_gate, wg_ref[...], wv_ref[...])
        dh = w[2:3, :] * dc + w[1:2, :] * _shift_up(dc, 1) + w[0:1, :] * _shift_up(dc, 2)
        dh_ref[...] = dh.astype(dh_ref.dtype)
        dw_ref[0:1, :] = jnp.sum(dc * _shift_down(h, 2), axis=0, keepdims=True)
        dw_ref[1:2, :] = jnp.sum(dc * _shift_down(h, 1), axis=0, keepdims=True)
        dw_ref[2:3, :] = jnp.sum(dc * h, axis=0, keepdims=True)
        db_ref[...] = jnp.sum(dc, axis=0, keepdims=True)

    gcol = lambda rows: pl.BlockSpec((rows, CONV_TILE), lambda j: (0, j % n))
    vcol = lambda rows: pl.BlockSpec((rows, CONV_TILE), lambda j: (0, j % n + n))
    own = lambda rows: pl.BlockSpec((rows, CONV_TILE), lambda j: (0, j))
    return pl.pallas_call(
        body, name="conv_bwd", grid=(2 * n,),
        in_specs=[gcol(t), vcol(t), gcol(3), vcol(3), gcol(1), vcol(1), gcol(t)],
        out_specs=[own(t), own(3), own(1)],
        out_shape=[jax.ShapeDtypeStruct((t, 2 * D_FF), BF16), jax.ShapeDtypeStruct((3, 2 * D_FF), F32),
                   jax.ShapeDtypeStruct((1, 2 * D_FF), F32)],
        compiler_params=_params(("parallel",)),
    )(hu, hu, conv_w, conv_w, conv_b, conv_b, d_act)


_NN = (((1,), (0,)), ((), ()))
_NT = (((1,), (1,)), ((), ()))
_TN = (((0,), (0,)), ((), ()))
HGRN_CB = 8


def _bf_dot(a, b, dn):
    return lax.dot_general(a.astype(BF16), b.astype(BF16), dn, preferred_element_type=F32)


def _tril():
    r = lax.broadcasted_iota(jnp.int32, (CHUNK, CHUNK), 0)
    return r >= lax.broadcasted_iota(jnp.int32, (CHUNK, CHUNK), 1)


def _hgrn_specs(t):
    rows = HGRN_CB * CHUNK
    head = pl.BlockSpec((rows, HGRN_K), lambda h, n: (n, h))
    v_head = pl.BlockSpec((rows, HGRN_K), lambda h, n: (n, C_HI // HGRN_K + h))
    mats = pl.BlockSpec((1, HGRN_CB, HGRN_K, HGRN_K), lambda h, n: (h, n, 0, 0))
    return head, v_head, mats, (HGRN_HEADS, t // rows)


def _hgrn_local_fwd(q_in, k_in, kd, z):
    t = q_in.shape[0]
    head, v_head, mats, grid = _hgrn_specs(t)

    def body(q_ref, k_ref, kd_ref, v_ref, o_ref, u_ref):
        tril = _tril()
        for n in range(HGRN_CB):
            rows = slice(n * CHUNK, (n + 1) * CHUNK)
            v = v_ref[rows, :]
            scores = jnp.where(tril, _bf_dot(q_ref[rows, :], k_ref[rows, :], _NT), 0.0)
            o_ref[rows, :] = _bf_dot(scores, v, _NN)
            u_ref[0, n] = _bf_dot(v, kd_ref[rows, :], _TN)

    return pl.pallas_call(
        body, name="hgrn_local_fwd", grid=grid, in_specs=[head, head, head, v_head], out_specs=[head, mats],
        out_shape=[jax.ShapeDtypeStruct((t, D_MODEL), F32),
                   jax.ShapeDtypeStruct((HGRN_HEADS, t // CHUNK, HGRN_K, HGRN_K), F32)],
        compiler_params=_params(("parallel", "parallel")),
    )(q_in, k_in, kd, z)


def _hgrn_state_specs(t, reverse=False):
    rows = HGRN_CB * CHUNK
    nb = t // rows
    at = (lambda n: nb - 1 - n) if reverse else (lambda n: n)
    tok = pl.BlockSpec((rows, D_MODEL), lambda n: (at(n), 0))
    mats = pl.BlockSpec((HGRN_HEADS, HGRN_CB, HGRN_K, HGRN_K), lambda n: (0, at(n), 0, 0))
    return tok, mats, nb


def _hgrn_state_fwd(o_intra, qe, dec, u):
    t = qe.shape[0]
    tok, mats, nb = _hgrn_state_specs(t)

    def body(oi_ref, qe_ref, dec_ref, u_ref, o_ref, st_ref, s_ref):
        @pl.when(pl.program_id(0) == 0)
        def _():
            s_ref[...] = jnp.zeros_like(s_ref)

        st = [s_ref[h] for h in range(HGRN_HEADS)]
        for n in range(HGRN_CB):
            rows = slice(n * CHUNK, (n + 1) * CHUNK)
            for h in range(HGRN_HEADS):
                cols = slice(h * HGRN_K, (h + 1) * HGRN_K)
                st_ref[h, n] = st[h]
                o_ref[rows, cols] = oi_ref[rows, cols] + _bf_dot(qe_ref[rows, cols], st[h], _NT)
                st[h] = st[h] * dec_ref[n * CHUNK:n * CHUNK + 1, cols] + u_ref[h, n]
        for h in range(HGRN_HEADS):
            s_ref[h] = st[h]

    return pl.pallas_call(
        body, name="hgrn_state_fwd", grid=(nb,), in_specs=[tok, tok, tok, mats], out_specs=[tok, mats],
        out_shape=[jax.ShapeDtypeStruct((t, D_MODEL), F32),
                   jax.ShapeDtypeStruct((HGRN_HEADS, t // CHUNK, HGRN_K, HGRN_K), F32)],
        scratch_shapes=[pltpu.VMEM((HGRN_HEADS, HGRN_K, HGRN_K), F32)],
        compiler_params=_params(("arbitrary",)),
    )(o_intra, qe, dec, u)


def _hgrn_state_bwd(d_o, qe, dec, states):
    t = qe.shape[0]
    tok_r, mats_r, nb = _hgrn_state_specs(t, reverse=True)

    def body(do_ref, qe_ref, dec_ref, st_ref, dqe_ref, ddec_ref, du_ref, d_ref):
        @pl.when(pl.program_id(0) == 0)
        def _():
            d_ref[...] = jnp.zeros_like(d_ref)

        first_row = lax.broadcasted_iota(jnp.int32, (CHUNK, HGRN_K), 0) == 0
        d = [d_ref[h] for h in range(HGRN_HEADS)]
        for n in reversed(range(HGRN_CB)):
            rows = slice(n * CHUNK, (n + 1) * CHUNK)
            for h in range(HGRN_HEADS):
                cols = slice(h * HGRN_K, (h + 1) * HGRN_K)
                st, do = st_ref[h, n], do_ref[rows, cols]
                du_ref[h, n] = d[h]
                ddec_ref[rows, cols] = jnp.where(first_row, jnp.sum(d[h] * st, axis=0, keepdims=True), 0.0)
                dqe_ref[rows, cols] = _bf_dot(do, st, _NN)
                d[h] = d[h] * dec_ref[n * CHUNK:n * CHUNK + 1, cols] + _bf_dot(do, qe_ref[rows, cols], _TN)
        for h in range(HGRN_HEADS):
            d_ref[h] = d[h]

    out = jax.ShapeDtypeStruct((t, D_MODEL), F32)
    return pl.pallas_call(
        body, name="hgrn_state_bwd", grid=(nb,), in_specs=[tok_r, tok_r, tok_r, mats_r], out_specs=[tok_r, tok_r, mats_r],
        out_shape=[out, out, jax.ShapeDtypeStruct((HGRN_HEADS, t // CHUNK, HGRN_K, HGRN_K), F32)],
        scratch_shapes=[pltpu.VMEM((HGRN_HEADS, HGRN_K, HGRN_K), F32)],
        compiler_params=_params(("arbitrary",)),
    )(d_o, qe, dec, states)


def _hgrn_local_bwd(q_in, k_in, kd, z, d_o, d_u):
    t = q_in.shape[0]
    head, v_head, mats, grid = _hgrn_specs(t)

    def body(q_ref, k_ref, kd_ref, v_ref, do_ref, du_ref, dq_ref, dk_ref, dkd_ref, dv_ref):
        tril = _tril()
        for n in range(HGRN_CB):
            rows = slice(n * CHUNK, (n + 1) * CHUNK)
            q, k, kd, v, do, du = q_ref[rows, :], k_ref[rows, :], kd_ref[rows, :], v_ref[rows, :], do_ref[rows, :], du_ref[0, n]
            scores = jnp.where(tril, _bf_dot(q, k, _NT), 0.0)
            d_scores = jnp.where(tril, _bf_dot(do, v, _NT), 0.0)
            dv_ref[rows, :] = _bf_dot(scores, do, _TN) + _bf_dot(kd, du, _NT)
            dkd_ref[rows, :] = _bf_dot(v, du, _NN)
            dq_ref[rows, :] = _bf_dot(d_scores, k, _NN)
            dk_ref[rows, :] = _bf_dot(d_scores, q, _TN)

    out = jax.ShapeDtypeStruct((t, D_MODEL), F32)
    return pl.pallas_call(
        body, name="hgrn_local_bwd", grid=grid, in_specs=[head, head, head, v_head, head, mats], out_specs=[head] * 4,
        out_shape=[out] * 4, compiler_params=_params(("parallel", "parallel")),
    )(q_in, k_in, kd, z, d_o, d_u)


def _split2(x):
    hi = x.astype(BF16)
    return hi, (x - hi.astype(F32)).astype(BF16)


def _seg_bcast(xs, e):
    parts = [t for x in xs for t in _split2(x)]
    out = jnp.dot(jnp.concatenate(parts, axis=0), e, preferred_element_type=F32)
    n = RWKV_HEAD
    return [out[2 * i * n:(2 * i + 1) * n] + out[(2 * i + 1) * n:(2 * i + 2) * n] for i in range(len(xs))]


def _rows_to_cols(rows, diag, e):
    zero = jnp.zeros((), BF16)
    parts = [jnp.where(diag, t, zero) for row in rows for t in _split2(row)]
    out = jnp.dot(jnp.concatenate(parts, axis=0), e, preferred_element_type=F32)
    n = RWKV_HEAD
    return [out[2 * i * n:(2 * i + 1) * n] + out[(2 * i + 1) * n:(2 * i + 2) * n] for i in range(len(rows))]


def _col_to_row(col, diag):
    return jnp.sum(jnp.where(diag, col, 0.0), axis=0, keepdims=True)


def _scan_consts():
    e = _seg_matrix(SCAN_GROUP, RWKV_HEAD)
    i = lax.broadcasted_iota(jnp.int32, (RWKV_HEAD, SCAN_GROUP), 0)
    l = lax.broadcasted_iota(jnp.int32, (RWKV_HEAD, SCAN_GROUP), 1)
    groups = [slice(g * SCAN_GROUP, (g + 1) * SCAN_GROUP) for g in range(D_MODEL // SCAN_GROUP)]
    return e, (l % RWKV_HEAD) == i, groups


def _rwkv_fwd(zl, w, k, a, b):
    t = zl.shape[0]
    steps = range(SCAN_TB)

    def body(r_ref, w_ref, k_ref, v_ref, a_ref, b_ref, y_ref, st_ref, s_ref):
        @pl.when(pl.program_id(0) == 0)
        def _():
            s_ref[...] = jnp.zeros_like(s_ref)

        e, diag, groups = _scan_consts()
        v_cols = [_rows_to_cols([v_ref[i:i + 1, sl] for i in steps], diag, e) for sl in groups]
        s = [s_ref[:, sl] for sl in groups]
        for i in steps:
            for g, sl in enumerate(groups):
                (sa,) = _seg_bcast([s[g] * a_ref[i:i + 1, sl]], e)
                s[g] = s[g] * w_ref[i:i + 1, sl] + sa * b_ref[i:i + 1, sl] + v_cols[g][i] * k_ref[i:i + 1, sl]
                st_ref[i, :, sl] = s[g]
        for g, sl in enumerate(groups):
            s_ref[:, sl] = s[g]
            y_cols = _seg_bcast([st_ref[i, :, sl] * r_ref[i:i + 1, sl] for i in steps], e)
            for i in steps:
                y_ref[i:i + 1, sl] = _col_to_row(y_cols[i], diag)

    blk = pl.BlockSpec((SCAN_TB, D_MODEL), lambda n: (n, 0))
    v_blk = pl.BlockSpec((SCAN_TB, D_MODEL), lambda n: (n, 2))
    return pl.pallas_call(
        body, name="rwkv_fwd", grid=(t // SCAN_TB,), in_specs=[blk, blk, blk, v_blk, blk, blk],
        out_specs=[blk, pl.BlockSpec((SCAN_TB, RWKV_HEAD, D_MODEL), lambda n: (n, 0, 0))],
        out_shape=[jax.ShapeDtypeStruct((t, D_MODEL), F32), jax.ShapeDtypeStruct((t, RWKV_HEAD, D_MODEL), F32)],
        scratch_shapes=[pltpu.VMEM((RWKV_HEAD, D_MODEL), F32)],
        compiler_params=_params(("arbitrary",)),
    )(zl, w, k, zl, a, b)


def _rwkv_bwd(zl, w, k, a, b, states, d_y):
    t = zl.shape[0]
    nb = t // SCAN_TB
    steps = range(SCAN_TB)

    def body(r_ref, w_ref, k_ref, v_ref, a_ref, b_ref, st_ref, prev_ref, dy_ref,
             dr_ref, dw_ref, dk_ref, dv_ref, da_ref, db_ref, ds_ref):
        @pl.when(pl.program_id(0) == 0)
        def _():
            ds_ref[...] = jnp.zeros_like(ds_ref)

        has_prev = (pl.program_id(0) < nb - 1).astype(F32)
        e, diag, groups = _scan_consts()
        colsum = lambda x: jnp.sum(x, axis=0, keepdims=True)

        def s_prev(i, sl):
            return st_ref[i - 1, :, sl] if i > 0 else prev_ref[0, :, sl] * has_prev

        dy_cols = [_rows_to_cols([dy_ref[i:i + 1, sl] for i in steps], diag, e) for sl in groups]
        v_cols = [_rows_to_cols([v_ref[i:i + 1, sl] for i in steps], diag, e) for sl in groups]
        sa_cols = [_seg_bcast([s_prev(i, sl) * a_ref[i:i + 1, sl] for i in steps], e) for sl in groups]
        ds = [ds_ref[:, sl] for sl in groups]
        dsk = [[None] * SCAN_TB for _ in groups]
        for i in reversed(steps):
            for g, sl in enumerate(groups):
                row = lambda ref: ref[i:i + 1, sl]
                sp = s_prev(i, sl)
                d = ds[g] + dy_cols[g][i] * row(r_ref)
                dr_ref[i:i + 1, sl] = colsum(st_ref[i, :, sl] * dy_cols[g][i])
                dw_ref[i:i + 1, sl] = colsum(d * sp)
                db_ref[i:i + 1, sl] = colsum(d * sa_cols[g][i])
                dk_ref[i:i + 1, sl] = colsum(d * v_cols[g][i])
                dsk[g][i] = d * row(k_ref)
                (dsa,) = _seg_bcast([d * row(b_ref)], e)
                da_ref[i:i + 1, sl] = colsum(sp * dsa)
                ds[g] = d * row(w_ref) + dsa * row(a_ref)
        for g, sl in enumerate(groups):
            ds_ref[:, sl] = ds[g]
            dv_cols = _seg_bcast(dsk[g], e)
            for i in steps:
                dv_ref[i:i + 1, sl] = _col_to_row(dv_cols[i], diag)

    blk = pl.BlockSpec((SCAN_TB, D_MODEL), lambda n: (nb - 1 - n, 0))
    v_blk = pl.BlockSpec((SCAN_TB, D_MODEL), lambda n: (nb - 1 - n, 2))
    out = jax.ShapeDtypeStruct((t, D_MODEL), F32)
    return pl.pallas_call(
        body, name="rwkv_bwd", grid=(nb,),
        in_specs=[blk, blk, blk, v_blk, blk, blk] + [
            pl.BlockSpec((SCAN_TB, RWKV_HEAD, D_MODEL), lambda n: (nb - 1 - n, 0, 0)),
            pl.BlockSpec((1, RWKV_HEAD, D_MODEL), lambda n: (jnp.maximum((nb - 1 - n) * SCAN_TB - 1, 0), 0, 0)),
            blk],
        out_specs=[blk] * 6, out_shape=[out] * 6,
        scratch_shapes=[pltpu.VMEM((RWKV_HEAD, D_MODEL), F32)],
        compiler_params=_params(("arbitrary",)),
    )(zl, w, k, zl, a, b, states, states, d_y)


def _loss_head(h1, ff, target, g_post):
    def fn(tv, pv):
        a, f, tgt = tv
        h2, vjp = jax.vjp(lambda a_, f_, g_: a_ + _rms(f_, g_), a, f, pv[0])
        err = h2 - tgt
        loss = 0.5 * jnp.sum(jnp.mean(err * err, axis=-1, keepdims=True), axis=0, keepdims=True)
        d_a, d_f, d_g = vjp(err * (1.0 / D_MODEL))
        return [d_a, d_f], [loss, d_g]

    return _tok_call("loss_head", fn, [(h1, D_MODEL, 0), (ff, D_MODEL, 0), (target, D_MODEL, 0)], [g_post],
                     [(D_MODEL, F32), (D_MODEL, BF16)], red_shapes=[(1, 1), (1, D_MODEL)])


def _sum_call(name, terms, rows_per_block=None):
    a0, i0 = terms[0]
    r, c = a0.shape[-2:]
    tr = rows_per_block or r

    def body(*refs):
        acc = refs[0][...].reshape(tr, c)
        for ref in refs[1:-1]:
            acc = acc + ref[...].reshape(tr, c)
        refs[-1][...] = acc

    def spec(arr, idx):
        if arr.ndim == 2:
            return pl.BlockSpec((tr, c), lambda i: (i, 0))
        return pl.BlockSpec((1, tr, c), functools.partial(lambda i, idx: (idx, i, 0), idx=idx))

    return pl.pallas_call(
        body, name=name, grid=(r // tr,), in_specs=[spec(a, i) for a, i in terms],
        out_specs=pl.BlockSpec((tr, c), lambda i: (i, 0)), out_shape=jax.ShapeDtypeStruct((r, c), F32),
        compiler_params=_params(("parallel",)),
    )(*[a for a, _ in terms])


def _adamw_math(w, g, m, v):
    m2 = ADAM_B1 * m + (1.0 - ADAM_B1) * g
    v2 = ADAM_B2 * v + (1.0 - ADAM_B2) * (g * g)
    m_hat = m2 / (1.0 - ADAM_B1 ** ADAM_STEP)
    v_hat = v2 / (1.0 - ADAM_B2 ** ADAM_STEP)
    return -ADAM_LR * (m_hat / (jnp.sqrt(v_hat) + ADAM_EPS) + ADAM_WD * w), m2, v2


def _adamw(name, w, g, m, v, bm, bn, g_transposed=False):
    r, c = w.shape

    def body(w_ref, g_ref, m_ref, v_ref, go_ref, d_ref, mo_ref, vo_ref):
        g = g_ref[...].T if g_transposed else g_ref[...]
        d, m2, v2 = _adamw_math(w_ref[...], g, m_ref[...], v_ref[...])
        go_ref[...] = g
        d_ref[...] = d
        mo_ref[...] = m2
        vo_ref[...] = v2

    blk = pl.BlockSpec((bm, bn), lambda i, j: (i, j))
    g_blk = pl.BlockSpec((bn, bm), lambda i, j: (j, i)) if g_transposed else blk
    out = jax.ShapeDtypeStruct((r, c), F32)
    return pl.pallas_call(
        body, name=name, grid=(pl.cdiv(r, bm), pl.cdiv(c, bn)), in_specs=[blk, g_blk, blk, blk],
        out_specs=[blk] * 4, out_shape=[out] * 4, compiler_params=_params(("parallel", "parallel")),
    )(w, g, m, v)


ANY = pl.BlockSpec(memory_space=pl.ANY)


def _place():
    x, y, c = lax.axis_index("x"), lax.axis_index("y"), lax.axis_index("c")
    chips = [(1 - x, y), (x, 1 - y), (1 - x, 1 - y)]
    return x, y, c, chips


def _sibling():
    return (lax.axis_index("x"), lax.axis_index("y"), 1 - lax.axis_index("c"))


def _wait_all(local, remote):
    for cp in local:
        cp.wait()
    for cp in remote:
        cp.wait_send()


def _place_shard(name, shard, place):
    r, cols = shard.shape
    tr = r // 4

    def body(s_ref, in_ref, out_ref):
        out_ref[...] = in_ref[...]

    return pl.pallas_call(
        body, name=name,
        grid_spec=pltpu.PrefetchScalarGridSpec(
            num_scalar_prefetch=1, grid=(4,), in_specs=[pl.BlockSpec((tr, cols), lambda i, s: (i, 0))],
            out_specs=pl.BlockSpec((tr, cols), lambda i, s: (4 * s[1] + i, 0))),
        out_shape=jax.ShapeDtypeStruct((N_SHARD * r, cols), shard.dtype), compiler_params=_params(("arbitrary",)),
    )(place, shard)


def _gather_shards(shards, placed):
    n = len(shards)

    def body(*refs):
        ins, outs = refs[:n], refs[2 * n:3 * n]
        send_sems, recv_sems = refs[3 * n:]
        x, y, c, chips = _place()
        me, sibling = (x, y, c), _sibling()

        def rows(k, px, py, pc):
            h = ins[k].shape[0] // 2
            return outs[k].at[pl.ds((2 * px + py) * 2 * h + pc * h, h), :]

        def copy(k, j, block, to, src=None):
            return pltpu.make_async_remote_copy(
                src_ref=rows(k, *block) if src is None else src, dst_ref=rows(k, *block),
                send_sem=send_sems.at[k, j], recv_sem=recv_sems.at[k, j], device_id=to, device_id_type=MESH)

        sends = []
        for k in range(n):
            h = ins[k].shape[0] // 2
            for j, chip in enumerate(chips):
                sends.append(copy(k, j, me, (*chip, c), src=ins[k].at[pl.ds(c * h, h), :]))
                sends[-1].start()
        for k in range(n):
            for j, chip in enumerate(chips):
                copy(k, j, (*chip, c), me).wait_recv()
                sends.append(copy(k, 3 + j, (*chip, c), sibling))
                sends[-1].start()
        for k in range(n):
            for j, chip in enumerate(chips):
                copy(k, 3 + j, (*chip, 1 - c), me).wait_recv()
        _wait_all([], sends)

    return pl.pallas_call(
        body, name="gather_shards", in_specs=[ANY] * (2 * n), out_specs=[ANY] * n,
        out_shape=[jax.ShapeDtypeStruct(a.shape, a.dtype) for a in placed],
        input_output_aliases={n + k: k for k in range(n)},
        scratch_shapes=[pltpu.SemaphoreType.DMA((n, 6)), pltpu.SemaphoreType.DMA((n, 6))],
    )(*shards, *placed)


def _exchange8(parts, landing):
    def body(in_ref, _, out_ref, send_sems, recv_sems):
        x, y, c, _ = _place()
        sends = []
        for rel in range(1, 8):
            dx, dy, dc = rel >> 2 & 1, rel >> 1 & 1, rel & 1
            cp = pltpu.make_async_remote_copy(
                src_ref=in_ref.at[2 * (x ^ dx) + (y ^ dy)], dst_ref=out_ref.at[4 * x + 2 * y + c],
                send_sem=send_sems.at[rel - 1], recv_sem=recv_sems.at[rel - 1],
                device_id=(x ^ dx, y ^ dy, c ^ dc), device_id_type=MESH)
            cp.start()
            sends.append(cp)
        for rel in range(1, 8):
            dx, dy, dc = rel >> 2 & 1, rel >> 1 & 1, rel & 1
            pltpu.make_async_remote_copy(
                src_ref=in_ref.at[0], dst_ref=out_ref.at[4 * (x ^ dx) + 2 * (y ^ dy) + (c ^ dc)],
                send_sem=send_sems.at[rel - 1], recv_sem=recv_sems.at[rel - 1],
                device_id=(x, y, c), device_id_type=MESH).wait_recv()
        _wait_all([], sends)

    return pl.pallas_call(
        body, name="exchange8", in_specs=[ANY, ANY], out_specs=ANY, out_shape=jax.ShapeDtypeStruct(landing.shape, landing.dtype),
        input_output_aliases={1: 0}, scratch_shapes=[pltpu.SemaphoreType.DMA((7,)), pltpu.SemaphoreType.DMA((7,))],
    )(parts, landing)


def _rs_sibling(grads):
    n = len(grads)

    def body(*refs):
        ins, outs = refs[:n], refs[n:2 * n]
        send_sems, recv_sems = refs[2 * n:]
        c = lax.axis_index("c")
        sends = []
        for k in range(n):
            for s in range(N_SHARD):
                cp = pltpu.make_async_remote_copy(
                    src_ref=ins[k].at[2 * s + 1 - c], dst_ref=outs[k].at[s], send_sem=send_sems.at[k, s],
                    recv_sem=recv_sems.at[k, s], device_id=_sibling(), device_id_type=MESH)
                cp.start()
                sends.append(cp)
        for cp in sends:
            cp.wait_recv()
        _wait_all([], sends)

    return pl.pallas_call(
        body, name="rs_sibling", in_specs=[ANY] * n, out_specs=[ANY] * n,
        out_shape=[jax.ShapeDtypeStruct((N_SHARD,) + a.shape[1:], a.dtype) for a in grads],
        scratch_shapes=[pltpu.SemaphoreType.DMA((n, N_SHARD)), pltpu.SemaphoreType.DMA((n, N_SHARD))],
    )(*grads)


def _rs_chips(parts):
    n = len(parts)

    def body(*refs):
        ins, outs = refs[:n], refs[n:2 * n]
        send_sems, recv_sems = refs[2 * n:]
        x, y, c, chips = _place()
        sends = []
        for k in range(n):
            for j, (px, py) in enumerate(chips):
                cp = pltpu.make_async_remote_copy(
                    src_ref=ins[k].at[2 * px + py], dst_ref=outs[k].at[j], send_sem=send_sems.at[k, j],
                    recv_sem=recv_sems.at[k, j], device_id=(px, py, c), device_id_type=MESH)
                cp.start()
                sends.append(cp)
        for cp in sends:
            cp.wait_recv()
        _wait_all([], sends)

    return pl.pallas_call(
        body, name="rs_chips", in_specs=[ANY] * n, out_specs=[ANY] * n,
        out_shape=[jax.ShapeDtypeStruct((3,) + a.shape[1:], a.dtype) for a in parts],
        scratch_shapes=[pltpu.SemaphoreType.DMA((n, 3)), pltpu.SemaphoreType.DMA((n, 3))],
    )(*parts)


def _rs_finish(bufs):
    n = len(bufs)

    def body(*refs):
        outs = refs[n:2 * n]
        send_sems, recv_sems = refs[2 * n:]
        c = lax.axis_index("c")
        sends = []
        for k in range(n):
            cp = pltpu.make_async_remote_copy(
                src_ref=outs[k].at[c], dst_ref=outs[k].at[c], send_sem=send_sems.at[k], recv_sem=recv_sems.at[k],
                device_id=_sibling(), device_id_type=MESH)
            cp.start()
            sends.append(cp)
        for k in range(n):
            pltpu.make_async_remote_copy(
                src_ref=outs[k].at[c], dst_ref=outs[k].at[1 - c], send_sem=send_sems.at[k], recv_sem=recv_sems.at[k],
                device_id=_sibling(), device_id_type=MESH).wait_recv()
        _wait_all([], sends)

    return pl.pallas_call(
        body, name="rs_finish", in_specs=[ANY] * n, out_specs=[ANY] * n,
        out_shape=[jax.ShapeDtypeStruct(a.shape, a.dtype) for a in bufs], input_output_aliases={k: k for k in range(n)},
        scratch_shapes=[pltpu.SemaphoreType.DMA((n,)), pltpu.SemaphoreType.DMA((n,))],
    )(*bufs)


def _sum3d(name, terms, scalars, grid_lead, out_lead, out_index, tr=None, out_dtype=F32):
    h, c = terms[0][0].shape[1:]
    tr = tr or h

    def body(s_ref, *refs):
        acc = refs[0][...].astype(F32)
        for ref in refs[1:-1]:
            acc = acc + ref[...].astype(F32)
        refs[-1][...] = acc.astype(refs[-1].dtype)

    in_specs = [pl.BlockSpec((1, tr, c), functools.partial(lambda l, i, s_ref, f: (f(l, s_ref), i, 0), f=f)) for _, f in terms]
    out_spec = pl.BlockSpec((1, tr, c), lambda l, i, s_ref: (out_index(l, s_ref), i, 0))
    return pl.pallas_call(
        body, name=name,
        grid_spec=pltpu.PrefetchScalarGridSpec(num_scalar_prefetch=1, grid=(grid_lead, h // tr), in_specs=in_specs, out_specs=out_spec),
        out_shape=jax.ShapeDtypeStruct((out_lead, h, c), out_dtype), compiler_params=_params(("arbitrary", "arbitrary")),
    )(scalars, *[a for a, _ in terms])


def _reduce_scatter(grads, place):
    g8 = [a.reshape(2 * N_SHARD, a.shape[0] // (2 * N_SHARD), a.shape[1]) for a in grads]
    from_sibling = _rs_sibling(g8)
    own = lambda l, s: 2 * l + s[0]
    parts = [_sum3d(f"rs_add1_{k}", [(g8[k], own), (from_sibling[k], lambda l, s: l)], place, N_SHARD, N_SHARD,
                    lambda l, s: l, tr=g8[k].shape[1] // 2, out_dtype=BF16)
             for k in range(len(grads))]
    from_chips = _rs_chips(parts)
    mine = [(lambda l, s: 2 * s[1] + s[0]), (lambda l, s: s[1])]
    bufs = [_sum3d(f"rs_add2_{k}", [(g8[k], mine[0]), (from_sibling[k], mine[1])]
                   + [(from_chips[k], functools.partial(lambda l, s, j: j, j=j)) for j in range(3)],
                   place, 1, 2, lambda l, s: s[0], tr=g8[k].shape[1] // 2)
            for k in range(len(grads))]
    whole = _rs_finish(bufs)
    return [w.reshape(2 * w.shape[1], w.shape[2]) for w in whole]


def _device_step(x, target, p):
    t = x.shape[0]
    d = D_MODEL
    tok = lambda arr, c=0, w=d: (arr, w, c)
    f32x = lambda n: [(d, F32)] * n
    rp_params = [p["w0"], p["w2p"], p["a0"], p["a2p"], p["g2p"], p["k_k"], p["k_a"]]
    post_params = [p["ln_w"], p["ln_b"], p["r_k"]]
    g = {}

    (xn,) = _tok_fwd("norm1_fwd", _fn_norm, [tok(x)], [p["g1"]], [(d, BF16)])
    z = _mm("in_proj", xn, p["w_in_t"], "nt", tm=t, tn=256)
    q_in, k_in, kd, qe, dec = _tok_fwd("hgates_fwd", _fn_hgates, [tok(z, 0), tok(z, 1)], [p["lb2"]], f32x(5))
    o_intra, u = _hgrn_local_fwd(q_in, k_in, kd, z)
    o_raw, h_states = _hgrn_state_fwd(o_intra, qe, dec, u)
    zl = _lerp_fwd(z, p["mu"])
    lora = tok(zl, 3 * d // LORA, LORA)
    decay, kr2, avec, bvec, gate = _tok_fwd("rprep_fwd", _fn_rprep, [tok(zl, 1), lora], rp_params, f32x(5))
    y, r_states = _rwkv_fwd(zl, decay, kr2, avec, bvec)
    (o_a,) = _tok_fwd("hpost_fwd", _fn_hpost, [tok(o_raw), tok(z, 3)], [p["gnorm"]], [(d, BF16)])
    post_toks = [tok(y), tok(zl, 0), tok(kr2), tok(zl, 2), tok(gate)]
    (o_b,) = _tok_fwd("rpost_fwd", _fn_rpost, post_toks, post_params, [(d, BF16)])
    y_a = _mm("branch_a", o_a, p["w_a"], "nn")
    y_b = _mm("branch_b", o_b, p["w_b"], "nn")
    merge_toks = [tok(z, C_G // 256, 256), tok(z, (C_G + d) // 256, 256), tok(y_a, 0, 256), tok(y_b, 0, 256)]
    (merged,) = _tok_fwd("merge_fwd", _fn_merge, merge_toks, [], [(256, BF16)], col_grid=4, tm=512)
    mix = _mm("out_proj", merged, p["w_out"], "nn")
    h1, xn2 = _tok_fwd("res1_fwd", _fn_res1, [tok(x), tok(mix)], [p["g_post1"], p["g_pre2"]], [(d, F32), (d, BF16)])
    hu = _mm("up_proj", xn2, p["w_up_t"], "nt", tm=t, tn=512)
    act = _conv_fwd(hu, p["conv_w"], p["conv_b"])
    ff = _mm("down_proj", act, p["w_down"], "nn")
    d_h1, d_ff, loss, g["g_post2"] = _loss_head(h1, ff, target, p["g_post2"])

    d_act = _mm("d_act", d_ff, p["w_down"], "nt")
    g["w_down"] = _mm("dw_down", act, d_ff, "tn", tm=256, tn=1024)
    d_hu, g["conv_w"], g["conv_b"] = _conv_bwd(hu, p["conv_w"], p["conv_b"], d_act)
    d_xn2 = _mm("d_xn2", d_hu, p["w_up_t"], "nn", tk=D_FF)
    g["w_up_t"] = _mm("dw_up", d_hu, xn2, "tn", tm=512, tn=1024)
    d_x_res, d_mix, g["g_post1"], g["g_pre2"] = _tok_bwd(
        "res1_bwd", _fn_res1, [tok(x), tok(mix)], [p["g_post1"], p["g_pre2"]], [[tok(d_h1)], [tok(d_xn2)]],
        [(d, F32), (d, BF16)])
    d_merged = _mm("d_merged", d_mix, p["w_out"], "nt")
    g["w_out"] = _mm("dw_out", merged, d_mix, "tn")
    d_ga, d_gb, d_ya, d_yb = _tok_bwd("merge_bwd", _fn_merge, merge_toks, [], [[tok(d_merged, 0, 256)]],
                                      [(256, BF16)] * 4, col_grid=4, tm=512)
    d_oa = _mm("d_oa", d_ya, p["w_a"], "nt")
    g["w_a"] = _mm("dw_a", o_a, d_ya, "tn")
    d_ob = _mm("d_ob", d_yb, p["w_b"], "nt")
    g["w_b"] = _mm("dw_b", o_b, d_yb, "tn")
    d_oraw, d_hg, g["gnorm"] = _tok_bwd("hpost_bwd", _fn_hpost, [tok(o_raw), tok(z, 3)], [p["gnorm"]], [[tok(d_oa)]],
                                        [(d, F32), (d, BF16)])
    d_y, d_r1, d_kr2_1, d_v1, d_gate, g["ln_w"], g["ln_b"], g["r_k"] = _tok_bwd(
        "rpost_bwd", _fn_rpost, post_toks, post_params, [[tok(d_ob)]], f32x(5))
    d_r2, d_decay, d_kr2_2, d_v2, d_avec, d_bvec = _rwkv_bwd(zl, decay, kr2, avec, bvec, r_states, d_y)
    prep = _tok_bwd("rprep_bwd", _fn_rprep, [tok(zl, 1), lora], rp_params,
                    [[tok(d_decay)], [tok(d_kr2_1), tok(d_kr2_2)], [tok(d_avec)], [tok(d_bvec)], [tok(d_gate)]],
                    [(d, F32), (LORA, F32)])
    d_kr, d_lora = prep[:2]
    g["w0"], g["w2p"], g["a0"], g["a2p"], g["g2p"], g["k_k"], g["k_a"] = prep[2:]
    d_zl = jnp.concatenate([d_r1 + d_r2, d_kr, d_v1 + d_v2, d_lora], axis=1)
    dz_r, g["mu"] = _lerp_bwd(z, p["mu"], d_zl)
    d_qe, d_dec, d_u = _hgrn_state_bwd(d_oraw, qe, dec, h_states)
    d_q_in, d_k_in, d_kd, d_vi = _hgrn_local_bwd(q_in, k_in, kd, z, d_oraw, d_u)
    d_hq, d_hf, g["lb2"] = _tok_bwd("hgates_bwd", _fn_hgates, [tok(z, 0), tok(z, 1)], [p["lb2"]],
                                    [[tok(d_q_in)], [tok(d_k_in)], [tok(d_kd)], [tok(d_qe)], [tok(d_dec)]], [(d, BF16)] * 2)
    dz = jnp.concatenate([d_hq, d_hf, d_vi.astype(BF16), d_hg, dz_r, d_ga, d_gb], axis=1)
    d_xn = _mm("d_xn", dz, p["w_in_t"], "nn", tm=1024, tn=512, tk=IN_COLS // 2)
    g["w_in_t"] = _mm("dw_in", dz, xn, "tn", tm=256, tn=1024)
    grad_x, g["g1"] = _tok_bwd("norm1_bwd", _fn_norm, [tok(x)], [p["g1"]], [[tok(d_xn)]], [(d, F32)],
                               add_to_first=tok(d_x_res))
    return loss, grad_x, g


_WEIGHTS = ["attn_pre_norm", "w_in", "hgrn_lb", "hgrn_gnorm", "w_branch_a", "rwkv_mu", "rwkv_w0", "rwkv_w2", "rwkv_a0",
            "rwkv_a2", "rwkv_g2", "rwkv_k_k", "rwkv_k_a", "rwkv_r_k", "rwkv_ln_w", "rwkv_ln_b", "w_branch_b", "w_out",
            "attn_post_norm", "ffn_pre_norm", "w_up", "conv_w", "conv_b", "w_down", "ffn_post_norm"]
_REPLICATED = [("attn_pre_norm", "g1"), ("hgrn_lb", "lb2"), ("hgrn_gnorm", "gnorm"), ("rwkv_mu", "mu"), ("rwkv_w0", "w0"),
               ("rwkv_a0", "a0"), ("rwkv_k_k", "k_k"), ("rwkv_k_a", "k_a"), ("rwkv_r_k", "r_k"), ("rwkv_ln_w", "ln_w"),
               ("rwkv_ln_b", "ln_b"), ("attn_post_norm", "g_post1"), ("ffn_pre_norm", "g_pre2"), ("conv_b", "conv_b"),
               ("ffn_post_norm", "g_post2")]
SLAB_COLS = 1024


def _pack(arrays):
    pieces, total = [], 0
    for a in arrays:
        flat = a.reshape(-1)
        rows = -(-flat.shape[0] // SLAB_COLS)
        pieces.append(jnp.pad(flat, (0, rows * SLAB_COLS - flat.shape[0])).reshape(rows, SLAB_COLS))
        total += rows
    if total % 8:
        pieces.append(jnp.zeros((8 - total % 8, SLAB_COLS), F32))
    return jnp.concatenate(pieces, axis=0)


def _unpack(slab, shapes):
    out, at = [], 0
    for s in shapes:
        size = 1
        for dim in s:
            size *= dim
        rows = -(-size // SLAB_COLS)
        out.append(slab[at:at + rows].reshape(-1)[:size].reshape(s))
        at += rows
    return out


def kernel(x, attn_pre_norm, w_in, hgrn_lb, hgrn_gnorm, w_branch_a, rwkv_mu, rwkv_w0, rwkv_w2, rwkv_a0, rwkv_a2, rwkv_g2, rwkv_k_k, rwkv_k_a, rwkv_r_k, rwkv_ln_w, rwkv_ln_b, w_branch_b, w_out, attn_post_norm, ffn_pre_norm, w_up, conv_w, conv_b, w_down, ffn_post_norm, loss_target, m_attn_pre_norm, m_w_in, m_hgrn_lb, m_hgrn_gnorm, m_w_branch_a, m_rwkv_mu, m_rwkv_w0, m_rwkv_w2, m_rwkv_a0, m_rwkv_a2, m_rwkv_g2, m_rwkv_k_k, m_rwkv_k_a, m_rwkv_r_k, m_rwkv_ln_w, m_rwkv_ln_b, m_w_branch_b, m_w_out, m_attn_post_norm, m_ffn_pre_norm, m_w_up, m_conv_w, m_conv_b, m_w_down, m_ffn_post_norm, v_attn_pre_norm, v_w_in, v_hgrn_lb, v_hgrn_gnorm, v_w_branch_a, v_rwkv_mu, v_rwkv_w0, v_rwkv_w2, v_rwkv_a0, v_rwkv_a2, v_rwkv_g2, v_rwkv_k_k, v_rwkv_k_a, v_rwkv_r_k, v_rwkv_ln_w, v_rwkv_ln_b, v_w_branch_b, v_w_out, v_attn_post_norm, v_ffn_pre_norm, v_w_up, v_conv_w, v_conv_b, v_w_down, v_ffn_post_norm):
    given = dict(locals())
    w = {n: given[n] for n in _WEIGHTS}
    mom = {n: given["m_" + n] for n in _WEIGHTS}
    var = {n: given["v_" + n] for n in _WEIGHTS}
    shard = 2 * lax.axis_index("x") + lax.axis_index("y")
    place = jnp.stack([lax.axis_index("c"), shard]).astype(jnp.int32)
    row = lambda a: a.reshape(1, -1)
    lora_of = lambda d: jnp.concatenate([d["rwkv_w2"][0], d["rwkv_a2"][0], d["rwkv_g2"][0]], axis=0)

    shards = [w["w_in"][0].T.astype(BF16), w["w_branch_a"][0].astype(BF16), w["w_branch_b"][0].astype(BF16),
              w["w_out"][0].astype(BF16), w["w_down"][0].astype(BF16), w["w_up"][0].T.astype(BF16),
              lora_of(w), jnp.pad(w["conv_w"][0], ((0, 29), (0, 0)))]
    placed = [_place_shard(f"place_{k}", a, place) for k, a in enumerate(shards)]
    w_in_t, w_a, w_b, w_o, w_dn, w_up_t, lora_g, conv_g = _gather_shards(shards, placed)
    lora_full = lora_g.reshape(N_SHARD, LORA, 256).transpose(1, 0, 2).reshape(LORA, D_MODEL)
    conv_full = conv_g.reshape(N_SHARD, 32, 2 * D_FF // N_SHARD)[:, :3].transpose(1, 0, 2).reshape(3, 2 * D_FF)
    lrow = lax.broadcasted_iota(jnp.int32, (LORA, 1), 0)
    p = {
        "g1": row(w["attn_pre_norm"]), "lb2": w["hgrn_lb"], "gnorm": row(w["hgrn_gnorm"]), "w_in_t": w_in_t, "w_a": w_a,
        "mu": row(w["rwkv_mu"]), "w0": row(w["rwkv_w0"]), "a0": row(w["rwkv_a0"]),
        "w2p": jnp.where(lrow < 64, lora_full, 0.0), "a2p": jnp.where((lrow >= 64) & (lrow < 128), lora_full, 0.0),
        "g2p": jnp.where(lrow >= 128, lora_full, 0.0),
        "k_k": row(w["rwkv_k_k"]), "k_a": row(w["rwkv_k_a"]), "r_k": row(w["rwkv_r_k"]), "ln_w": row(w["rwkv_ln_w"]),
        "ln_b": row(w["rwkv_ln_b"]), "w_b": w_b, "w_out": w_o, "g_post1": row(w["attn_post_norm"]),
        "g_pre2": row(w["ffn_pre_norm"]), "w_up_t": w_up_t, "conv_w": conv_full, "conv_b": row(w["conv_b"]),
        "w_down": w_dn, "g_post2": row(w["ffn_post_norm"]),
    }

    loss, grad_x, g = _device_step(x[0], loss_target[0], p)

    g_in_t, g_a, g_b, g_o, g_dn, g_up_t = _reduce_scatter(
        [g["w_in_t"], g["w_a"], g["w_b"], g["w_out"], g["w_down"], g["w_up_t"]], place)
    rep_shapes = [w[n].shape for n, _ in _REPLICATED]
    rep = _pack([g[key] for _, key in _REPLICATED])
    n_rep_rows = rep.shape[0]
    cw = 2 * D_FF // N_SHARD
    lora_rows, conv_rows = LORA * 256 // SLAB_COLS, -(-3 * cw // SLAB_COLS)
    lora_g = jnp.concatenate([g["w2p"][0:64], g["a2p"][64:128], g["g2p"][128:256]], axis=0)
    lora_parts = lora_g.reshape(LORA, N_SHARD, 256).transpose(1, 0, 2).reshape(N_SHARD, lora_rows, SLAB_COLS)
    conv_parts = g["conv_w"].reshape(3, N_SHARD, cw).transpose(1, 0, 2).reshape(N_SHARD, 3 * cw)
    conv_parts = jnp.pad(conv_parts, ((0, 0), (0, conv_rows * SLAB_COLS - 3 * cw))).reshape(N_SHARD, conv_rows, SLAB_COLS)
    n_rows = n_rep_rows + lora_rows + conv_rows
    fill = jnp.zeros((N_SHARD, -n_rows % 8, SLAB_COLS), F32)
    parts = jnp.concatenate([jnp.broadcast_to(rep, (N_SHARD,) + rep.shape), lora_parts, conv_parts, fill], axis=1)
    me = 4 * lax.axis_index("x") + 2 * lax.axis_index("y") + lax.axis_index("c")
    landing = lax.dynamic_update_slice(jnp.zeros((8,) + parts.shape[1:], F32),
                                       lax.dynamic_index_in_dim(parts, shard, 0, keepdims=True), (me, 0, 0))
    gathered = _exchange8(parts, landing)
    summed = _sum3d("small_sum", [(gathered, functools.partial(lambda l, s, i: i, i=i)) for i in range(8)], place, 1, 1,
                    lambda l, s: 0)[0]
    lora_grad = summed[n_rep_rows:n_rep_rows + lora_rows].reshape(LORA, 256)
    conv_grad = summed[n_rep_rows + lora_rows:n_rows].reshape(-1)[:3 * cw].reshape(3, cw)

    res = {}

    def put(name, outs, shape=None):
        res[name] = [o.reshape(w[name].shape if shape is None else shape) for o in outs]

    put("w_in", _adamw("adamw_w_in", w["w_in"][0], g_in_t, mom["w_in"][0], var["w_in"][0], 1024, 128, g_transposed=True))
    put("w_up", _adamw("adamw_w_up", w["w_up"][0], g_up_t, mom["w_up"][0], var["w_up"][0], 1024, 128, g_transposed=True))
    for name, grad in (("w_branch_a", g_a), ("w_branch_b", g_b), ("w_out", g_o)):
        put(name, _adamw("adamw_" + name, w[name][0], grad, mom[name][0], var[name][0], 256, 1024))
    put("w_down", _adamw("adamw_w_down", w["w_down"][0], g_dn, mom["w_down"][0], var["w_down"][0], 176, 1024))
    put("conv_w", _adamw("adamw_conv_w", w["conv_w"][0], conv_grad, mom["conv_w"][0], var["conv_w"][0], 3, 2 * D_FF // N_SHARD))
    lora_out = _adamw("adamw_lora", lora_of(w), lora_grad, lora_of(mom), lora_of(var), LORA, 256)
    for name, lo, hi in (("rwkv_w2", 0, 64), ("rwkv_a2", 64, 128), ("rwkv_g2", 128, 256)):
        put(name, [o[lo:hi] for o in lora_out])
    rep_names = [n for n, _ in _REPLICATED]
    rep_out = _adamw("adamw_small", _pack([w[n] for n in rep_names]), summed[:n_rep_rows], _pack([mom[n] for n in rep_names]),
                     _pack([var[n] for n in rep_names]), n_rep_rows, SLAB_COLS)
    for name, parts in zip(rep_names, zip(*[_unpack(o, rep_shapes) for o in rep_out])):
        put(name, list(parts))

    loss = lax.psum(loss[0, 0], ("x", "y", "c"))
    return (loss, grad_x[None], *[res[n][0] for n in _WEIGHTS], *[res[n][1] for n in _WEIGHTS],
            *[res[n][2] for n in _WEIGHTS], *[res[n][3] for n in _WEIGHTS])
```

```python
import functools

import jax
import jax.numpy as jnp
from jax import lax
from jax.experimental import pallas as pl
from jax.experimental.pallas import tpu as pltpu

F32, BF16 = jnp.float32, jnp.bfloat16
MESH = pl.DeviceIdType.MESH

D_MODEL = 1024
HGRN_HEADS = 8
HGRN_K = 128
HGRN_SCALE = HGRN_K ** -0.5
CHUNK = 32
RWKV_HEAD = 64
LORA = 256
D_FF = 2816
EPS = 1e-6
GN_EPS = 1e-5 * RWKV_HEAD
N_SHARD = 4
ADAM_LR, ADAM_B1, ADAM_B2, ADAM_EPS, ADAM_WD, ADAM_STEP = 0.001, 0.9, 0.999, 1e-08, 0.01, 10

LANES = 128
VMEM_LIMIT = 56 * 1024 * 1024
SCAN_TB = 16
SCAN_GROUP = 256

C_HQ, C_HF, C_HI, C_HG = 0, 1024, 2048, 3072
C_R = 4096
R_COLS = 3328
C_G = 7424
IN_COLS = 9472


def _params(sem=None, **kw):
    return pltpu.CompilerParams(dimension_semantics=sem, vmem_limit_bytes=VMEM_LIMIT, **kw)


def _seg_matrix(n, seg):
    r = lax.broadcasted_iota(jnp.int32, (n, n), 0) // seg
    c = lax.broadcasted_iota(jnp.int32, (n, n), 1) // seg
    return (r == c).astype(BF16)


def _split3(x):
    hi = x.astype(BF16)
    r1 = x - hi.astype(F32)
    mid = r1.astype(BF16)
    lo = (r1 - mid.astype(F32)).astype(BF16)
    return hi, mid, lo


def _segsum_impl(x, seg):
    e = _seg_matrix(LANES, seg)
    outs = []
    for g in range(x.shape[1] // LANES):
        hi, mid, lo = _split3(x[:, g * LANES:(g + 1) * LANES])
        outs.append(jnp.dot(hi, e, preferred_element_type=F32) + jnp.dot(mid, e, preferred_element_type=F32)
                    + jnp.dot(lo, e, preferred_element_type=F32))
    return outs[0] if len(outs) == 1 else jnp.concatenate(outs, axis=1)


def _make_segsum(seg):
    @jax.custom_vjp
    def f(x):
        return _segsum_impl(x, seg)

    f.defvjp(lambda x: (_segsum_impl(x, seg), None), lambda _, ct: (_segsum_impl(ct, seg),))
    return f


_segsum64 = _make_segsum(RWKV_HEAD)
_segsum128 = _make_segsum(HGRN_K)


def _chunk_mm_impl(x, kind, transposed):
    n = x.shape[0]
    r = lax.broadcasted_iota(jnp.int32, (n, n), 1 if transposed else 0)
    c = lax.broadcasted_iota(jnp.int32, (n, n), 0 if transposed else 1)
    same = (r // CHUNK) == (c // CHUNK)
    if kind == "cumsum":
        m = same & (r >= c)
    else:
        m = same & (c % CHUNK == (CHUNK // 2 - 1 if kind == "mid" else CHUNK - 1))
    m = m.astype(BF16)
    hi, mid, lo = _split3(x)
    return (jnp.dot(m, hi, preferred_element_type=F32) + jnp.dot(m, mid, preferred_element_type=F32)
            + jnp.dot(m, lo, preferred_element_type=F32))


def _make_chunk_mm(kind):
    @jax.custom_vjp
    def f(x):
        return _chunk_mm_impl(x, kind, False)

    f.defvjp(lambda x: (_chunk_mm_impl(x, kind, False), None), lambda _, ct: (_chunk_mm_impl(ct, kind, True),))
    return f


_chunk_cumsum = _make_chunk_mm("cumsum")
_chunk_mid = _make_chunk_mm("mid")
_chunk_last = _make_chunk_mm("last")


@jax.custom_vjp
def _bdot(x, w):
    return jnp.dot(x.astype(BF16), w.astype(BF16), preferred_element_type=F32)


def _bdot_fwd(x, w):
    return _bdot(x, w), (x, w)


def _bdot_bwd(res, ct):
    x, w = res
    ctb = ct.astype(BF16)
    dx = lax.dot_general(ctb, w.astype(BF16), (((1,), (1,)), ((), ())), preferred_element_type=F32)
    dw = lax.dot_general(x.astype(BF16), ctb, (((0,), (0,)), ((), ())), preferred_element_type=F32)
    return dx, dw


_bdot.defvjp(_bdot_fwd, _bdot_bwd)


def _sigmoid(x):
    return 1.0 / (1.0 + jnp.exp(-x))


def _silu(x):
    return x * _sigmoid(x)


def _softplus(x):
    return jnp.maximum(x, 0.0) + jnp.log(1.0 + jnp.exp(-jnp.abs(x)))


def _rms(x, g):
    return x * lax.rsqrt(jnp.mean(x * x, axis=-1, keepdims=True) + EPS) * g


def _fn_norm(t, p):
    return [_rms(t[0], p[0])]


def _fn_hgates(t, p):
    hq, hf = t
    lb2 = p[0]
    m = jnp.max(lb2, axis=0, keepdims=True)
    e = jnp.exp(lb2 - m)
    first = lax.broadcasted_iota(jnp.int32, e.shape, 0) == 0
    lb = jnp.sum(jnp.where(first, e, 0.0), axis=0, keepdims=True) / jnp.sum(e, axis=0, keepdims=True)
    f = lb + (1.0 - lb) * _sigmoid(hf)
    q, k = _silu(hq) * HGRN_SCALE, 1.0 - f
    b = _chunk_cumsum(jnp.log(f))
    b_ref, b_last = _chunk_mid(b), _chunk_last(b)
    return [q * jnp.exp(b - b_ref), k * jnp.exp(b_ref - b), k * jnp.exp(b_last - b), q * jnp.exp(b), jnp.exp(b_last)]


def _fn_hpost(t, p):
    o, hg = t
    ms = _segsum128(o * o) * (1.0 / HGRN_K)
    return [o * lax.rsqrt(ms + EPS) * p[0] * _silu(hg)]


def _fn_rprep(t, p):
    kr, lora = t
    w0, w2p, a0, a2p, g2p, k_k, k_a = p
    pre_w = w0 + _bdot(jnp.tanh(lora), w2p)
    w_log = -_softplus(-pre_w) - 0.5
    decay = jnp.exp(-jnp.exp(w_log))
    a = _sigmoid(a0 + _bdot(lora, a2p))
    g = _bdot(_sigmoid(lora), g2p)
    kk = kr * k_k
    kk = kk / jnp.maximum(jnp.sqrt(_segsum64(kk * kk)), 1e-12)
    kr2 = kr * (1.0 + (a - 1.0) * k_a)
    return [decay, kr2, -kk, kk * a, g]


def _fn_rpost(t, p):
    y, r, kr2, v, g = t
    ln_w, ln_b, r_k = p
    mu = _segsum64(y) * (1.0 / RWKV_HEAD)
    yc = y - mu
    var = _segsum64(yc * yc) * (1.0 / RWKV_HEAD)
    yn = yc * lax.rsqrt(var + GN_EPS) * ln_w + ln_b
    bonus = _segsum64(r * kr2 * r_k) * v
    return [(yn + bonus) * g]


def _fn_merge(t, p):
    ga, gb, ya, yb = t
    return [_sigmoid(ga) * ya + _sigmoid(gb) * yb]


def _fn_res1(t, p):
    x, mix = t
    h1 = x + _rms(mix, p[0])
    return [h1, _rms(h1, p[1])]


def _tok_call(name, fn, toks, params, outs, red_shapes=(), tm=128, col_grid=1):
    n_t, n_p, n_o = len(toks), len(params), len(outs)
    t_len = toks[0][0].shape[0]

    def body(*refs):
        tv = [r[...].astype(F32) for r in refs[:n_t]]
        pv = [r[...] for r in refs[n_t:n_t + n_p]]
        o, red = fn(tv, pv)
        for ref, val in zip(refs[n_t + n_p:n_t + n_p + n_o], o):
            ref[...] = val.astype(ref.dtype)
        red_refs = refs[n_t + n_p + n_o:]
        if red_refs:
            first = pl.program_id(0) == 0

            @pl.when(first)
            def _():
                for ref, val in zip(red_refs, red):
                    ref[...] = val

            @pl.when(jnp.logical_not(first))
            def _():
                for ref, val in zip(red_refs, red):
                    ref[...] += val

    in_specs = [pl.BlockSpec((tm, w), functools.partial(lambda i, j, c: (i, c + j), c=c)) for (_, w, c) in toks]
    in_specs += [pl.BlockSpec(p.shape, lambda i, j: (0, 0)) for p in params]
    out_specs = [pl.BlockSpec((tm, w), lambda i, j: (i, j)) for (w, _) in outs]
    out_specs += [pl.BlockSpec(s, lambda i, j: (0, 0)) for s in red_shapes]
    out_shape = [jax.ShapeDtypeStruct((t_len, w * col_grid), dt) for (w, dt) in outs]
    out_shape += [jax.ShapeDtypeStruct(s, F32) for s in red_shapes]
    return pl.pallas_call(
        body, name=name, grid=(t_len // tm, col_grid), in_specs=in_specs, out_specs=out_specs, out_shape=out_shape,
        compiler_params=_params(("arbitrary", "arbitrary")),
    )(*[a for (a, _, _) in toks], *params)


def _tok_fwd(name, fn, toks, params, outs, **kw):
    return _tok_call(name, lambda tv, pv: (fn(tv, pv), []), toks, params, outs, **kw)


def _tok_bwd(name, fn, toks, params, cts, want, add_to_first=None, **kw):
    n_t = len(toks)
    flat = [c for group in cts for c in group]
    extra = [] if add_to_first is None else [add_to_first]

    def bwd(tv, pv):
        prim, rest = tv[:n_t], tv[n_t:]
        ct, at = [], 0
        for group in cts:
            ct.append(functools.reduce(lambda u, v: u + v, rest[at:at + len(group)]))
            at += len(group)
        _, vjp = jax.vjp(lambda *a: tuple(fn(list(a[:n_t]), list(a[n_t:]))), *prim, *pv)
        g = vjp(tuple(ct))
        tok_grads = [g[i] for i in range(n_t) if want[i] is not None]
        if extra:
            tok_grads[0] = tok_grads[0] + rest[at]
        return tok_grads, list(g[n_t:])

    return _tok_call(name, bwd, list(toks) + flat + extra, params, [w for w in want if w is not None],
                     red_shapes=[p.shape for p in params], **kw)


def _mm(name, a, b, mode, out_dtype=F32, tm=None, tn=None, tk=None):
    if mode == "nn":
        (m, k), (_, n) = a.shape, b.shape
    elif mode == "nt":
        (m, k), (n, _) = a.shape, b.shape
    else:
        (k, m), (_, n) = a.shape, b.shape
    tm = (512 if mode == "tn" else 2048) if tm is None else tm
    tn = (512 if mode == "tn" else 256) if tn is None else tn
    tk = k if tk is None else tk
    tm, tn = min(tm, m), min(tn, n)
    nk = k // tk
    assert m % tm == 0 and n % tn == 0 and k % tk == 0, (name, a.shape, b.shape, tm, tn, tk)
    a_spec = pl.BlockSpec((tk, tm), lambda i, j, q: (q, i)) if mode == "tn" else pl.BlockSpec((tm, tk), lambda i, j, q: (i, q))
    b_spec = pl.BlockSpec((tn, tk), lambda i, j, q: (j, q)) if mode == "nt" else pl.BlockSpec((tk, tn), lambda i, j, q: (q, j))
    dn = {"nn": (((1,), (0,)), ((), ())), "nt": (((1,), (1,)), ((), ())), "tn": (((0,), (0,)), ((), ()))}[mode]

    def body(a_ref, b_ref, o_ref, *acc):
        p = lax.dot_general(a_ref[...], b_ref[...], dn, preferred_element_type=F32)
        if nk == 1:
            o_ref[...] = p.astype(o_ref.dtype)
        else:
            q = pl.program_id(2)

            @pl.when(q == 0)
            def _():
                acc[0][...] = p

            @pl.when(q > 0)
            def _():
                acc[0][...] += p

            @pl.when(q == nk - 1)
            def _():
                o_ref[...] = acc[0][...].astype(o_ref.dtype)

    return pl.pallas_call(
        body, name=name, grid=(m // tm, n // tn, nk), in_specs=[a_spec, b_spec],
        out_specs=pl.BlockSpec((tm, tn), lambda i, j, q: (i, j)), out_shape=jax.ShapeDtypeStruct((m, n), out_dtype),
        scratch_shapes=[pltpu.VMEM((tm, tn), F32)] if nk > 1 else [],
        compiler_params=_params(("parallel", "parallel", "arbitrary")),
    )(a, b)


def _shift_down(z, n):
    rows = lax.broadcasted_iota(jnp.int32, z.shape, 0)
    return jnp.where(rows < n, 0.0, pltpu.roll(z, n, 0))


def _shift_up(z, n):
    t = z.shape[0]
    rows = lax.broadcasted_iota(jnp.int32, z.shape, 0)
    return jnp.where(rows >= t - n, 0.0, pltpu.roll(z, t - n, 0))


def _lerp_fwd(z, mu):
    t = z.shape[0]
    w = 256

    def body(z_ref, mu_ref, o_ref):
        zz = z_ref[...]
        o_ref[...] = zz + mu_ref[...] * (_shift_down(zz, 1) - zz)

    return pl.pallas_call(
        body, name="lerp_fwd", grid=(R_COLS // w,),
        in_specs=[pl.BlockSpec((t, w), lambda j: (0, C_R // w + j)), pl.BlockSpec((1, w), lambda j: (0, j))],
        out_specs=pl.BlockSpec((t, w), lambda j: (0, j)), out_shape=jax.ShapeDtypeStruct((t, R_COLS), F32),
        compiler_params=_params(("parallel",)),
    )(z, mu)


def _lerp_bwd(z, mu, dzl):
    t = z.shape[0]
    w = 256

    def body(z_ref, mu_ref, d_ref, dz_ref, dmu_ref):
        zz, m, d = z_ref[...], mu_ref[...], d_ref[...]
        dz_ref[...] = (d * (1.0 - m) + _shift_up(d * m, 1)).astype(dz_ref.dtype)
        dmu_ref[...] = jnp.sum(d * (_shift_down(zz, 1) - zz), axis=0, keepdims=True)

    return pl.pallas_call(
        body, name="lerp_bwd", grid=(R_COLS // w,),
        in_specs=[pl.BlockSpec((t, w), lambda j: (0, C_R // w + j)), pl.BlockSpec((1, w), lambda j: (0, j)),
                  pl.BlockSpec((t, w), lambda j: (0, j))],
        out_specs=[pl.BlockSpec((t, w), lambda j: (0, j)), pl.BlockSpec((1, w), lambda j: (0, j))],
        out_shape=[jax.ShapeDtypeStruct((t, R_COLS), BF16), jax.ShapeDtypeStruct((1, R_COLS), F32)],
        compiler_params=_params(("parallel",)),
    )(z, mu, dzl)


CONV_TILE = 256
N_CONV_TILES = D_FF // CONV_TILE


def _conv(h, w, b):
    return b + w[0:1, :] * _shift_down(h, 2) + w[1:2, :] * _shift_down(h, 1) + w[2:3, :] * h


def _conv_fwd(hu, conv_w, conv_b):
    t = hu.shape[0]
    n = N_CONV_TILES

    def body(hg_ref, hv_ref, wg_ref, wv_ref, bg_ref, bv_ref, o_ref):
        gate = _conv(hg_ref[...], wg_ref[...], bg_ref[...])
        val = _conv(hv_ref[...], wv_ref[...], bv_ref[...])
        o_ref[...] = (_silu(gate) * val).astype(o_ref.dtype)

    col = lambda off: pl.BlockSpec((t, CONV_TILE), lambda j: (0, j + off))
    wspec = lambda off: pl.BlockSpec((3, CONV_TILE), lambda j: (0, j + off))
    bspec = lambda off: pl.BlockSpec((1, CONV_TILE), lambda j: (0, j + off))
    return pl.pallas_call(
        body, name="conv_fwd", grid=(n,),
        in_specs=[col(0), col(n), wspec(0), wspec(n), bspec(0), bspec(n)],
        out_specs=pl.BlockSpec((t, CONV_TILE), lambda j: (0, j)), out_shape=jax.ShapeDtypeStruct((t, D_FF), BF16),
        compiler_params=_params(("parallel",)),
    )(hu, hu, conv_w, conv_w, conv_b, conv_b)


def _conv_bwd(hu, conv_w, conv_b, d_act):
    t = hu.shape[0]
    n = N_CONV_TILES

    def body(hg_ref, hv_ref, wg_ref, wv_ref, bg_ref, bv_ref, d_ref, dh_ref, dw_ref, db_ref):
        is_gate = pl.program_id(0) < n
        hg, hv = hg_ref[...], hv_ref[...]
        gate = _conv(hg, wg_ref[...], bg_ref[...])
        val = _conv(hv, wv_ref[...], bv_ref[...])
        d = d_ref[...]
        sg = _sigmoid(gate)
        d_gate = d * val * (sg * (1.0 + gate * (1.0 - sg)))
        d_val = d * (gate * sg)
        dc = jnp.where(is_gate, d_gate, d_val)
        h = jnp.where(is_gate, hg, hv)
        w = jnp.where(is_gate, wg_ref[...], wv_ref[...])
        dh = w[2:3, :] * dc + w[1:2, :] * _shift_up(dc, 1) + w[0:1, :] * _shift_up(dc, 2)
        dh_ref[...] = dh.astype(dh_ref.dtype)
        dw_ref[0:1, :] = jnp.sum(dc * _shift_down(h, 2), axis=0, keepdims=True)
        dw_ref[1:2, :] = jnp.sum(dc * _shift_down(h, 1), axis=0, keepdims=True)
        dw_ref[2:3, :] = jnp.sum(dc * h, axis=0, keepdims=True)
        db_ref[...] = jnp.sum(dc, axis=0, keepdims=True)

    gcol = lambda rows: pl.BlockSpec((rows, CONV_TILE), lambda j: (0, j % n))
    vcol = lambda rows: pl.BlockSpec((rows, CONV_TILE), lambda j: (0, j % n + n))
    own = lambda rows: pl.BlockSpec((rows, CONV_TILE), lambda j: (0, j))
    return pl.pallas_call(
        body, name="conv_bwd", grid=(2 * n,),
        in_specs=[gcol(t), vcol(t), gcol(3), vcol(3), gcol(1), vcol(1), gcol(t)],
        out_specs=[own(t), own(3), own(1)],
        out_shape=[jax.ShapeDtypeStruct((t, 2 * D_FF), BF16), jax.ShapeDtypeStruct((3, 2 * D_FF), F32),
                   jax.ShapeDtypeStruct((1, 2 * D_FF), F32)],
        compiler_params=_params(("parallel",)),
    )(hu, hu, conv_w, conv_w, conv_b, conv_b, d_act)


_NN = (((1,), (0,)), ((), ()))
_NT = (((1,), (1,)), ((), ()))
_TN = (((0,), (0,)), ((), ()))
HGRN_CB = 8


def _bf_dot(a, b, dn):
    return lax.dot_general(a.astype(BF16), b.astype(BF16), dn, preferred_element_type=F32)


def _tril():
    r = lax.broadcasted_iota(jnp.int32, (CHUNK, CHUNK), 0)
    return r >= lax.broadcasted_iota(jnp.int32, (CHUNK, CHUNK), 1)


def _hgrn_specs(t):
    rows = HGRN_CB * CHUNK
    head = pl.BlockSpec((rows, HGRN_K), lambda h, n: (n, h))
    v_head = pl.BlockSpec((rows, HGRN_K), lambda h, n: (n, C_HI // HGRN_K + h))
    mats = pl.BlockSpec((1, HGRN_CB, HGRN_K, HGRN_K), lambda h, n: (h, n, 0, 0))
    return head, v_head, mats, (HGRN_HEADS, t // rows)


def _hgrn_local_fwd(q_in, k_in, kd, z):
    t = q_in.shape[0]
    head, v_head, mats, grid = _hgrn_specs(t)

    def body(q_ref, k_ref, kd_ref, v_ref, o_ref, u_ref):
        tril = _tril()
        for n in range(HGRN_CB):
            rows = slice(n * CHUNK, (n + 1) * CHUNK)
            v = v_ref[rows, :]
            scores = jnp.where(tril, _bf_dot(q_ref[rows, :], k_ref[rows, :], _NT), 0.0)
            o_ref[rows, :] = _bf_dot(scores, v, _NN)
            u_ref[0, n] = _bf_dot(v, kd_ref[rows, :], _TN)

    return pl.pallas_call(
        body, name="hgrn_local_fwd", grid=grid, in_specs=[head, head, head, v_head], out_specs=[head, mats],
        out_shape=[jax.ShapeDtypeStruct((t, D_MODEL), F32),
                   jax.ShapeDtypeStruct((HGRN_HEADS, t // CHUNK, HGRN_K, HGRN_K), F32)],
        compiler_params=_params(("parallel", "parallel")),
    )(q_in, k_in, kd, z)


def _hgrn_state_specs(t, reverse=False):
    rows = HGRN_CB * CHUNK
    nb = t // rows
    at = (lambda n: nb - 1 - n) if reverse else (lambda n: n)
    tok = pl.BlockSpec((rows, D_MODEL), lambda n: (at(n), 0))
    mats = pl.BlockSpec((HGRN_HEADS, HGRN_CB, HGRN_K, HGRN_K), lambda n: (0, at(n), 0, 0))
    return tok, mats, nb


def _hgrn_state_fwd(o_intra, qe, dec, u):
    t = qe.shape[0]
    tok, mats, nb = _hgrn_state_specs(t)

    def body(oi_ref, qe_ref, dec_ref, u_ref, o_ref, st_ref, s_ref):
        @pl.when(pl.program_id(0) == 0)
        def _():
            s_ref[...] = jnp.zeros_like(s_ref)

        st = [s_ref[h] for h in range(HGRN_HEADS)]
        for n in range(HGRN_CB):
            rows = slice(n * CHUNK, (n + 1) * CHUNK)
            for h in range(HGRN_HEADS):
                cols = slice(h * HGRN_K, (h + 1) * HGRN_K)
                st_ref[h, n] = st[h]
                o_ref[rows, cols] = oi_ref[rows, cols] + _bf_dot(qe_ref[rows, cols], st[h], _NT)
                st[h] = st[h] * dec_ref[n * CHUNK:n * CHUNK + 1, cols] + u_ref[h, n]
        for h in range(HGRN_HEADS):
            s_ref[h] = st[h]

    return pl.pallas_call(
        body, name="hgrn_state_fwd", grid=(nb,), in_specs=[tok, tok, tok, mats], out_specs=[tok, mats],
        out_shape=[jax.ShapeDtypeStruct((t, D_MODEL), F32),
                   jax.ShapeDtypeStruct((HGRN_HEADS, t // CHUNK, HGRN_K, HGRN_K), F32)],
        scratch_shapes=[pltpu.VMEM((HGRN_HEADS, HGRN_K, HGRN_K), F32)],
        compiler_params=_params(("arbitrary",)),
    )(o_intra, qe, dec, u)


def _hgrn_state_bwd(d_o, qe, dec, states):
    t = qe.shape[0]
    tok_r, mats_r, nb = _hgrn_state_specs(t, reverse=True)

    def body(do_ref, qe_ref, dec_ref, st_ref, dqe_ref, ddec_ref, du_ref, d_ref):
        @pl.when(pl.program_id(0) == 0)
        def _():
            d_ref[...] = jnp.zeros_like(d_ref)

        first_row = lax.broadcasted_iota(jnp.int32, (CHUNK, HGRN_K), 0) == 0
        d = [d_ref[h] for h in range(HGRN_HEADS)]
        for n in reversed(range(HGRN_CB)):
            rows = slice(n * CHUNK, (n + 1) * CHUNK)
            for h in range(HGRN_HEADS):
                cols = slice(h * HGRN_K, (h + 1) * HGRN_K)
                st, do = st_ref[h, n], do_ref[rows, cols]
                du_ref[h, n] = d[h]
                ddec_ref[rows, cols] = jnp.where(first_row, jnp.sum(d[h] * st, axis=0, keepdims=True), 0.0)
                dqe_ref[rows, cols] = _bf_dot(do, st, _NN)
                d[h] = d[h] * dec_ref[n * CHUNK:n * CHUNK + 1, cols] + _bf_dot(do, qe_ref[rows, cols], _TN)
        for h in range(HGRN_HEADS):
            d_ref[h] = d[h]

    out = jax.ShapeDtypeStruct((t, D_MODEL), F32)
    return pl.pallas_call(
        body, name="hgrn_state_bwd", grid=(nb,), in_specs=[tok_r, tok_r, tok_r, mats_r], out_specs=[tok_r, tok_r, mats_r],
        out_shape=[out, out, jax.ShapeDtypeStruct((HGRN_HEADS, t // CHUNK, HGRN_K, HGRN_K), F32)],
        scratch_shapes=[pltpu.VMEM((HGRN_HEADS, HGRN_K, HGRN_K), F32)],
        compiler_params=_params(("arbitrary",)),
    )(d_o, qe, dec, states)


def _hgrn_local_bwd(q_in, k_in, kd, z, d_o, d_u):
    t = q_in.shape[0]
    head, v_head, mats, grid = _hgrn_specs(t)

    def body(q_ref, k_ref, kd_ref, v_ref, do_ref, du_ref, dq_ref, dk_ref, dkd_ref, dv_ref):
        tril = _tril()
        for n in range(HGRN_CB):
            rows = slice(n * CHUNK, (n + 1) * CHUNK)
            q, k, kd, v, do, du = q_ref[rows, :], k_ref[rows, :], kd_ref[rows, :], v_ref[rows, :], do_ref[rows, :], du_ref[0, n]
            scores = jnp.where(tril, _bf_dot(q, k, _NT), 0.0)
            d_scores = jnp.where(tril, _bf_dot(do, v, _NT), 0.0)
            dv_ref[rows, :] = _bf_dot(scores, do, _TN) + _bf_dot(kd, du, _NT)
            dkd_ref[rows, :] = _bf_dot(v, du, _NN)
            dq_ref[rows, :] = _bf_dot(d_scores, k, _NN)
            dk_ref[rows, :] = _bf_dot(d_scores, q, _TN)

    out = jax.ShapeDtypeStruct((t, D_MODEL), F32)
    return pl.pallas_call(
        body, name="hgrn_local_bwd", grid=grid, in_specs=[head, head, head, v_head, head, mats], out_specs=[head] * 4,
        out_shape=[out] * 4, compiler_params=_params(("parallel", "parallel")),
    )(q_in, k_in, kd, z, d_o, d_u)


def _split2(x):
    hi = x.astype(BF16)
    return hi, (x - hi.astype(F32)).astype(BF16)


def _seg_bcast(xs, e):
    parts = [t for x in xs for t in _split2(x)]
    out = jnp.dot(jnp.concatenate(parts, axis=0), e, preferred_element_type=F32)
    n = RWKV_HEAD
    return [out[2 * i * n:(2 * i + 1) * n] + out[(2 * i + 1) * n:(2 * i + 2) * n] for i in range(len(xs))]


def _rows_to_cols(rows, diag, e):
    zero = jnp.zeros((), BF16)
    parts = [jnp.where(diag, t, zero) for row in rows for t in _split2(row)]
    out = jnp.dot(jnp.concatenate(parts, axis=0), e, preferred_element_type=F32)
    n = RWKV_HEAD
    return [out[2 * i * n:(2 * i + 1) * n] + out[(2 * i + 1) * n:(2 * i + 2) * n] for i in range(len(rows))]


def _col_to_row(col, diag):
    return jnp.sum(jnp.where(diag, col, 0.0), axis=0, keepdims=True)


_SCAN_PAIRS = ((0, 1), (2, 3))


def _scan_consts():
    e = _seg_matrix(SCAN_GROUP, RWKV_HEAD)
    i = lax.broadcasted_iota(jnp.int32, (RWKV_HEAD, SCAN_GROUP), 0)
    l = lax.broadcasted_iota(jnp.int32, (RWKV_HEAD, SCAN_GROUP), 1)
    groups = [slice(g * SCAN_GROUP, (g + 1) * SCAN_GROUP) for g in range(D_MODEL // SCAN_GROUP)]
    return e, (l % RWKV_HEAD) == i, groups


def _rwkv_fwd(zl, w, k, a, b, shards, placed):
    t = zl.shape[0]
    nb = t // SCAN_TB
    n = len(shards)
    steps = range(SCAN_TB)

    def body(*refs):
        scan(*refs[:6], *refs[6 + 2 * n:8 + 2 * n], refs[8 + 3 * n])
        gather = (refs[6:6 + n], refs[8 + 2 * n:8 + 3 * n], *refs[9 + 3 * n:])

        @pl.when(pl.program_id(0) == 0)
        def _():
            _gather_start(*gather)

        @pl.when(pl.program_id(0) == nb - 1)
        def _():
            _gather_finish(*gather)

    def scan(r_ref, w_ref, k_ref, v_ref, a_ref, b_ref, y_ref, st_ref, s_ref):
        @pl.when(pl.program_id(0) == 0)
        def _():
            s_ref[...] = jnp.zeros_like(s_ref)

        e, diag, groups = _scan_consts()
        v_cols = [_rows_to_cols([v_ref[i:i + 1, sl] for i in steps], diag, e) for sl in groups]
        s = [s_ref[:, sl] for sl in groups]
        for i in steps:
            for pair in _SCAN_PAIRS:
                sas = _seg_bcast([s[g] * a_ref[i:i + 1, groups[g]] for g in pair], e)
                for g, sa in zip(pair, sas):
                    sl = groups[g]
                    s[g] = s[g] * w_ref[i:i + 1, sl] + sa * b_ref[i:i + 1, sl] + v_cols[g][i] * k_ref[i:i + 1, sl]
                    st_ref[i, :, sl] = s[g]
        for g, sl in enumerate(groups):
            s_ref[:, sl] = s[g]
            y_cols = _seg_bcast([st_ref[i, :, sl] * r_ref[i:i + 1, sl] for i in steps], e)
            for i in steps:
                y_ref[i:i + 1, sl] = _col_to_row(y_cols[i], diag)

    blk = pl.BlockSpec((SCAN_TB, D_MODEL), lambda n: (n, 0))
    v_blk = pl.BlockSpec((SCAN_TB, D_MODEL), lambda n: (n, 2))
    outs = pl.pallas_call(
        body, name="rwkv_fwd", grid=(nb,), in_specs=[blk, blk, blk, v_blk, blk, blk] + [ANY] * (2 * n),
        out_specs=[blk, pl.BlockSpec((SCAN_TB, RWKV_HEAD, D_MODEL), lambda i: (i, 0, 0))] + [ANY] * n,
        out_shape=[jax.ShapeDtypeStruct((t, D_MODEL), F32), jax.ShapeDtypeStruct((t, RWKV_HEAD, D_MODEL), F32)]
        + [jax.ShapeDtypeStruct(p.shape, p.dtype) for p in placed],
        input_output_aliases={6 + n + i: 2 + i for i in range(n)},
        scratch_shapes=[pltpu.VMEM((RWKV_HEAD, D_MODEL), F32), pltpu.SemaphoreType.DMA((n, 6)), pltpu.SemaphoreType.DMA((n, 6))],
        compiler_params=_params(("arbitrary",)),
    )(zl, w, k, zl, a, b, *shards, *placed)
    return outs[0], outs[1], outs[2:]


def _rwkv_bwd(zl, w, k, a, b, states, d_y, parts):
    t = zl.shape[0]
    nb = t // SCAN_TB
    n = len(parts)
    steps = range(SCAN_TB)

    def body(*refs):
        scan(*refs[:9], *refs[9 + n:15 + n], refs[15 + 2 * n])
        exchange = (refs[9:9 + n], refs[15 + n:15 + 2 * n], *refs[16 + 2 * n:])

        @pl.when(pl.program_id(0) == 0)
        def _():
            _rs_chips_start(*exchange)

        @pl.when(pl.program_id(0) == nb - 1)
        def _():
            _rs_chips_finish(*exchange)

    def scan(r_ref, w_ref, k_ref, v_ref, a_ref, b_ref, st_ref, prev_ref, dy_ref,
             dr_ref, dw_ref, dk_ref, dv_ref, da_ref, db_ref, ds_ref):
        @pl.when(pl.program_id(0) == 0)
        def _():
            ds_ref[...] = jnp.zeros_like(ds_ref)

        has_prev = (pl.program_id(0) < nb - 1).astype(F32)
        e, diag, groups = _scan_consts()
        colsum = lambda x: jnp.sum(x, axis=0, keepdims=True)

        def s_prev(i, sl):
            return st_ref[i - 1, :, sl] if i > 0 else prev_ref[0, :, sl] * has_prev

        dy_cols = [_rows_to_cols([dy_ref[i:i + 1, sl] for i in steps], diag, e) for sl in groups]
        v_cols = [_rows_to_cols([v_ref[i:i + 1, sl] for i in steps], diag, e) for sl in groups]
        sa_cols = [_seg_bcast([s_prev(i, sl) * a_ref[i:i + 1, sl] for i in steps], e) for sl in groups]
        ds = [ds_ref[:, sl] for sl in groups]
        dsk = [[None] * SCAN_TB for _ in groups]
        for i in reversed(steps):
            for pair in _SCAN_PAIRS:
                d = {}
                for g in pair:
                    sl = groups[g]
                    d[g] = ds[g] + dy_cols[g][i] * r_ref[i:i + 1, sl]
                    dr_ref[i:i + 1, sl] = colsum(st_ref[i, :, sl] * dy_cols[g][i])
                    dw_ref[i:i + 1, sl] = colsum(d[g] * s_prev(i, sl))
                    db_ref[i:i + 1, sl] = colsum(d[g] * sa_cols[g][i])
                    dk_ref[i:i + 1, sl] = colsum(d[g] * v_cols[g][i])
                    dsk[g][i] = d[g] * k_ref[i:i + 1, sl]
                dsas = _seg_bcast([d[g] * b_ref[i:i + 1, groups[g]] for g in pair], e)
                for g, dsa in zip(pair, dsas):
                    sl = groups[g]
                    da_ref[i:i + 1, sl] = colsum(s_prev(i, sl) * dsa)
                    ds[g] = d[g] * w_ref[i:i + 1, sl] + dsa * a_ref[i:i + 1, sl]
        for g, sl in enumerate(groups):
            ds_ref[:, sl] = ds[g]
            dv_cols = _seg_bcast(dsk[g], e)
            for i in steps:
                dv_ref[i:i + 1, sl] = _col_to_row(dv_cols[i], diag)

    blk = pl.BlockSpec((SCAN_TB, D_MODEL), lambda n: (nb - 1 - n, 0))
    v_blk = pl.BlockSpec((SCAN_TB, D_MODEL), lambda n: (nb - 1 - n, 2))
    out = jax.ShapeDtypeStruct((t, D_MODEL), F32)
    outs = pl.pallas_call(
        body, name="rwkv_bwd", grid=(nb,),
        in_specs=[blk, blk, blk, v_blk, blk, blk] + [
            pl.BlockSpec((SCAN_TB, RWKV_HEAD, D_MODEL), lambda i: (nb - 1 - i, 0, 0)),
            pl.BlockSpec((1, RWKV_HEAD, D_MODEL), lambda i: (jnp.maximum((nb - 1 - i) * SCAN_TB - 1, 0), 0, 0)),
            blk] + [ANY] * n,
        out_specs=[blk] * 6 + [ANY] * n,
        out_shape=[out] * 6 + [jax.ShapeDtypeStruct((3,) + p.shape[1:], p.dtype) for p in parts],
        scratch_shapes=[pltpu.VMEM((RWKV_HEAD, D_MODEL), F32), pltpu.SemaphoreType.DMA((n, 3)), pltpu.SemaphoreType.DMA((n, 3))],
        compiler_params=_params(("arbitrary",)),
    )(zl, w, k, zl, a, b, states, states, d_y, *parts)
    return outs[:6], outs[6:]


def _loss_head(h1, ff, target, g_post):
    def fn(tv, pv):
        a, f, tgt = tv
        h2, vjp = jax.vjp(lambda a_, f_, g_: a_ + _rms(f_, g_), a, f, pv[0])
        err = h2 - tgt
        loss = 0.5 * jnp.sum(jnp.mean(err * err, axis=-1, keepdims=True), axis=0, keepdims=True)
        d_a, d_f, d_g = vjp(err * (1.0 / D_MODEL))
        return [d_a, d_f], [loss, d_g]

    return _tok_call("loss_head", fn, [(h1, D_MODEL, 0), (ff, D_MODEL, 0), (target, D_MODEL, 0)], [g_post],
                     [(D_MODEL, F32), (D_MODEL, BF16)], red_shapes=[(1, 1), (1, D_MODEL)])


def _sum_call(name, terms, rows_per_block=None):
    a0, i0 = terms[0]
    r, c = a0.shape[-2:]
    tr = rows_per_block or r

    def body(*refs):
        acc = refs[0][...].reshape(tr, c)
        for ref in refs[1:-1]:
            acc = acc + ref[...].reshape(tr, c)
        refs[-1][...] = acc

    def spec(arr, idx):
        if arr.ndim == 2:
            return pl.BlockSpec((tr, c), lambda i: (i, 0))
        return pl.BlockSpec((1, tr, c), functools.partial(lambda i, idx: (idx, i, 0), idx=idx))

    return pl.pallas_call(
        body, name=name, grid=(r // tr,), in_specs=[spec(a, i) for a, i in terms],
        out_specs=pl.BlockSpec((tr, c), lambda i: (i, 0)), out_shape=jax.ShapeDtypeStruct((r, c), F32),
        compiler_params=_params(("parallel",)),
    )(*[a for a, _ in terms])


def _adamw_math(w, g, m, v):
    m2 = ADAM_B1 * m + (1.0 - ADAM_B1) * g
    v2 = ADAM_B2 * v + (1.0 - ADAM_B2) * (g * g)
    m_hat = m2 / (1.0 - ADAM_B1 ** ADAM_STEP)
    v_hat = v2 / (1.0 - ADAM_B2 ** ADAM_STEP)
    return -ADAM_LR * (m_hat / (jnp.sqrt(v_hat) + ADAM_EPS) + ADAM_WD * w), m2, v2


def _adamw(name, w, g, m, v, bm, bn, g_transposed=False):
    r, c = w.shape

    def body(w_ref, g_ref, m_ref, v_ref, go_ref, d_ref, mo_ref, vo_ref):
        g = g_ref[...].T if g_transposed else g_ref[...]
        d, m2, v2 = _adamw_math(w_ref[...], g, m_ref[...], v_ref[...])
        go_ref[...] = g
        d_ref[...] = d
        mo_ref[...] = m2
        vo_ref[...] = v2

    blk = pl.BlockSpec((bm, bn), lambda i, j: (i, j))
    g_blk = pl.BlockSpec((bn, bm), lambda i, j: (j, i)) if g_transposed else blk
    out = jax.ShapeDtypeStruct((r, c), F32)
    return pl.pallas_call(
        body, name=name, grid=(pl.cdiv(r, bm), pl.cdiv(c, bn)), in_specs=[blk, g_blk, blk, blk],
        out_specs=[blk] * 4, out_shape=[out] * 4, compiler_params=_params(("parallel", "parallel")),
    )(w, g, m, v)


ANY = pl.BlockSpec(memory_space=pl.ANY)


def _place():
    x, y, c = lax.axis_index("x"), lax.axis_index("y"), lax.axis_index("c")
    chips = [(1 - x, y), (x, 1 - y), (1 - x, 1 - y)]
    return x, y, c, chips


def _sibling():
    return (lax.axis_index("x"), lax.axis_index("y"), 1 - lax.axis_index("c"))


def _wait_all(local, remote):
    for cp in local:
        cp.wait()
    for cp in remote:
        cp.wait_send()


def _place_shard(name, shard, place):
    r, cols = shard.shape
    tr = r // 4

    def body(s_ref, in_ref, out_ref):
        out_ref[...] = in_ref[...]

    return pl.pallas_call(
        body, name=name,
        grid_spec=pltpu.PrefetchScalarGridSpec(
            num_scalar_prefetch=1, grid=(4,), in_specs=[pl.BlockSpec((tr, cols), lambda i, s: (i, 0))],
            out_specs=pl.BlockSpec((tr, cols), lambda i, s: (4 * s[1] + i, 0))),
        out_shape=jax.ShapeDtypeStruct((N_SHARD * r, cols), shard.dtype), compiler_params=_params(("arbitrary",)),
    )(place, shard)


def _gather_copies(ins, outs, send_sems, recv_sems):
    x, y, c, chips = _place()
    me, sibling = (x, y, c), _sibling()

    def rows(k, px, py, pc):
        h = ins[k].shape[0] // 2
        return outs[k].at[pl.ds((2 * px + py) * 2 * h + pc * h, h), :]

    def copy(k, j, block, to, src=None):
        return pltpu.make_async_remote_copy(
            src_ref=rows(k, *block) if src is None else src, dst_ref=rows(k, *block),
            send_sem=send_sems.at[k, j], recv_sem=recv_sems.at[k, j], device_id=to, device_id_type=MESH)

    each = [(k, j, chip) for k in range(len(ins)) for j, chip in enumerate(chips)]
    half = lambda k: ins[k].at[pl.ds(c * (ins[k].shape[0] // 2), ins[k].shape[0] // 2), :]
    first = [copy(k, j, me, (*chip, c), src=half(k)) for k, j, chip in each]
    arrive = [copy(k, j, (*chip, c), me) for k, j, chip in each]
    passed = [copy(k, 3 + j, (*chip, c), sibling) for k, j, chip in each]
    landed = [copy(k, 3 + j, (*chip, 1 - c), me) for k, j, chip in each]
    return first, arrive, passed, landed


def _gather_start(ins, outs, send_sems, recv_sems):
    for cp in _gather_copies(ins, outs, send_sems, recv_sems)[0]:
        cp.start()


def _gather_finish(ins, outs, send_sems, recv_sems):
    first, arrive, passed, landed = _gather_copies(ins, outs, send_sems, recv_sems)
    for arrival, forward in zip(arrive, passed):
        arrival.wait_recv()
        forward.start()
    for cp in landed:
        cp.wait_recv()
    _wait_all([], first + passed)


def _gather_shards(shards, placed):
    n = len(shards)

    def body(*refs):
        ins, outs = refs[:n], refs[2 * n:3 * n]
        _gather_start(ins, outs, *refs[3 * n:])
        _gather_finish(ins, outs, *refs[3 * n:])

    return pl.pallas_call(
        body, name="gather_shards", in_specs=[ANY] * (2 * n), out_specs=[ANY] * n,
        out_shape=[jax.ShapeDtypeStruct(a.shape, a.dtype) for a in placed],
        input_output_aliases={n + k: k for k in range(n)},
        scratch_shapes=[pltpu.SemaphoreType.DMA((n, 6)), pltpu.SemaphoreType.DMA((n, 6))],
    )(*shards, *placed)


def _exchange8(parts, landing):
    def body(in_ref, _, out_ref, send_sems, recv_sems):
        x, y, c, _ = _place()
        sends = []
        for rel in range(1, 8):
            dx, dy, dc = rel >> 2 & 1, rel >> 1 & 1, rel & 1
            cp = pltpu.make_async_remote_copy(
                src_ref=in_ref.at[2 * (x ^ dx) + (y ^ dy)], dst_ref=out_ref.at[4 * x + 2 * y + c],
                send_sem=send_sems.at[rel - 1], recv_sem=recv_sems.at[rel - 1],
                device_id=(x ^ dx, y ^ dy, c ^ dc), device_id_type=MESH)
            cp.start()
            sends.append(cp)
        for rel in range(1, 8):
            dx, dy, dc = rel >> 2 & 1, rel >> 1 & 1, rel & 1
            pltpu.make_async_remote_copy(
                src_ref=in_ref.at[0], dst_ref=out_ref.at[4 * (x ^ dx) + 2 * (y ^ dy) + (c ^ dc)],
                send_sem=send_sems.at[rel - 1], recv_sem=recv_sems.at[rel - 1],
                device_id=(x, y, c), device_id_type=MESH).wait_recv()
        _wait_all([], sends)

    return pl.pallas_call(
        body, name="exchange8", in_specs=[ANY, ANY], out_specs=ANY, out_shape=jax.ShapeDtypeStruct(landing.shape, landing.dtype),
        input_output_aliases={1: 0}, scratch_shapes=[pltpu.SemaphoreType.DMA((7,)), pltpu.SemaphoreType.DMA((7,))],
    )(parts, landing)


def _rs_sibling(grads):
    n = len(grads)

    def body(*refs):
        ins, outs = refs[:n], refs[n:2 * n]
        send_sems, recv_sems = refs[2 * n:]
        c = lax.axis_index("c")
        sends = []
        for k in range(n):
            for s in range(N_SHARD):
                cp = pltpu.make_async_remote_copy(
                    src_ref=ins[k].at[2 * s + 1 - c], dst_ref=outs[k].at[s], send_sem=send_sems.at[k, s],
                    recv_sem=recv_sems.at[k, s], device_id=_sibling(), device_id_type=MESH)
                cp.start()
                sends.append(cp)
        for cp in sends:
            cp.wait_recv()
        _wait_all([], sends)

    return pl.pallas_call(
        body, name="rs_sibling", in_specs=[ANY] * n, out_specs=[ANY] * n,
        out_shape=[jax.ShapeDtypeStruct((N_SHARD,) + a.shape[1:], a.dtype) for a in grads],
        scratch_shapes=[pltpu.SemaphoreType.DMA((n, N_SHARD)), pltpu.SemaphoreType.DMA((n, N_SHARD))],
    )(*grads)


def _rs_chips_copies(ins, outs, send_sems, recv_sems):
    x, y, c, chips = _place()
    return [pltpu.make_async_remote_copy(
        src_ref=ins[k].at[2 * px + py], dst_ref=outs[k].at[j], send_sem=send_sems.at[k, j], recv_sem=recv_sems.at[k, j],
        device_id=(px, py, c), device_id_type=MESH) for k in range(len(ins)) for j, (px, py) in enumerate(chips)]


def _rs_chips_start(ins, outs, send_sems, recv_sems):
    for cp in _rs_chips_copies(ins, outs, send_sems, recv_sems):
        cp.start()


def _rs_chips_finish(ins, outs, send_sems, recv_sems):
    sends = _rs_chips_copies(ins, outs, send_sems, recv_sems)
    for cp in sends:
        cp.wait_recv()
    _wait_all([], sends)


def _rs_chips(parts):
    n = len(parts)

    def body(*refs):
        _rs_chips_start(refs[:n], refs[n:2 * n], *refs[2 * n:])
        _rs_chips_finish(refs[:n], refs[n:2 * n], *refs[2 * n:])

    return pl.pallas_call(
        body, name="rs_chips", in_specs=[ANY] * n, out_specs=[ANY] * n,
        out_shape=[jax.ShapeDtypeStruct((3,) + a.shape[1:], a.dtype) for a in parts],
        scratch_shapes=[pltpu.SemaphoreType.DMA((n, 3)), pltpu.SemaphoreType.DMA((n, 3))],
    )(*parts)


def _rs_finish(bufs):
    n = len(bufs)

    def body(*refs):
        outs = refs[n:2 * n]
        send_sems, recv_sems = refs[2 * n:]
        c = lax.axis_index("c")
        sends = []
        for k in range(n):
            cp = pltpu.make_async_remote_copy(
                src_ref=outs[k].at[c], dst_ref=outs[k].at[c], send_sem=send_sems.at[k], recv_sem=recv_sems.at[k],
                device_id=_sibling(), device_id_type=MESH)
            cp.start()
            sends.append(cp)
        for k in range(n):
            pltpu.make_async_remote_copy(
                src_ref=outs[k].at[c], dst_ref=outs[k].at[1 - c], send_sem=send_sems.at[k], recv_sem=recv_sems.at[k],
                device_id=_sibling(), device_id_type=MESH).wait_recv()
        _wait_all([], sends)

    return pl.pallas_call(
        body, name="rs_finish", in_specs=[ANY] * n, out_specs=[ANY] * n,
        out_shape=[jax.ShapeDtypeStruct(a.shape, a.dtype) for a in bufs], input_output_aliases={k: k for k in range(n)},
        scratch_shapes=[pltpu.SemaphoreType.DMA((n,)), pltpu.SemaphoreType.DMA((n,))],
    )(*bufs)


def _sum3d(name, terms, scalars, grid_lead, out_lead, out_index, tr=None, out_dtype=F32):
    h, c = terms[0][0].shape[1:]
    tr = tr or h

    def body(s_ref, *refs):
        acc = refs[0][...].astype(F32)
        for ref in refs[1:-1]:
            acc = acc + ref[...].astype(F32)
        refs[-1][...] = acc.astype(refs[-1].dtype)

    in_specs = [pl.BlockSpec((1, tr, c), functools.partial(lambda l, i, s_ref, f: (f(l, s_ref), i, 0), f=f)) for _, f in terms]
    out_spec = pl.BlockSpec((1, tr, c), lambda l, i, s_ref: (out_index(l, s_ref), i, 0))
    return pl.pallas_call(
        body, name=name,
        grid_spec=pltpu.PrefetchScalarGridSpec(num_scalar_prefetch=1, grid=(grid_lead, h // tr), in_specs=in_specs, out_specs=out_spec),
        out_shape=jax.ShapeDtypeStruct((out_lead, h, c), out_dtype), compiler_params=_params(("arbitrary", "arbitrary")),
    )(scalars, *[a for a, _ in terms])


def _reduce_scatter(grads, place):
    stage1 = _rs_stage1(grads, place, "w_in")
    return _rs_stage3(stage1, _rs_chips(stage1[2]), place, "w_in")


def _rs_stage1(grads, place, tag):
    g8 = [a.reshape(2 * N_SHARD, a.shape[0] // (2 * N_SHARD), a.shape[1]) for a in grads]
    from_sibling = _rs_sibling(g8)
    parts = [_sum3d(f"rs_add1_{tag}{k}", [(g8[k], lambda l, s: 2 * l + s[0]), (from_sibling[k], lambda l, s: l)], place,
                    N_SHARD, N_SHARD, lambda l, s: l, tr=g8[k].shape[1] // 2, out_dtype=BF16)
             for k in range(len(grads))]
    return g8, from_sibling, parts


def _rs_stage3(stage1, from_chips, place, tag):
    g8, from_sibling, _ = stage1
    mine = [(lambda l, s: 2 * s[1] + s[0]), (lambda l, s: s[1])]
    bufs = [_sum3d(f"rs_add2_{tag}{k}", [(g8[k], mine[0]), (from_sibling[k], mine[1])]
                   + [(from_chips[k], functools.partial(lambda l, s, j: j, j=j)) for j in range(3)],
                   place, 1, 2, lambda l, s: s[0], tr=g8[k].shape[1] // 2)
            for k in range(len(g8))]
    whole = _rs_finish(bufs)
    return [w.reshape(2 * w.shape[1], w.shape[2]) for w in whole]


_LATE = ["w_a", "w_b", "w_out", "w_down", "w_up_t"]


def _device_step(x, target, p, late_shards, late_placed, place):
    t = x.shape[0]
    d = D_MODEL
    tok = lambda arr, c=0, w=d: (arr, w, c)
    f32x = lambda n: [(d, F32)] * n
    rp_params = [p["w0"], p["w2p"], p["a0"], p["a2p"], p["g2p"], p["k_k"], p["k_a"]]
    post_params = [p["ln_w"], p["ln_b"], p["r_k"]]
    g = {}

    (xn,) = _tok_fwd("norm1_fwd", _fn_norm, [tok(x)], [p["g1"]], [(d, BF16)])
    z = _mm("in_proj", xn, p["w_in_t"], "nt", tm=t, tn=256)
    q_in, k_in, kd, qe, dec = _tok_fwd("hgates_fwd", _fn_hgates, [tok(z, 0), tok(z, 1)], [p["lb2"]], f32x(5))
    o_intra, u = _hgrn_local_fwd(q_in, k_in, kd, z)
    o_raw, h_states = _hgrn_state_fwd(o_intra, qe, dec, u)
    zl = _lerp_fwd(z, p["mu"])
    lora = tok(zl, 3 * d // LORA, LORA)
    decay, kr2, avec, bvec, gate = _tok_fwd("rprep_fwd", _fn_rprep, [tok(zl, 1), lora], rp_params, f32x(5))
    y, r_states, late = _rwkv_fwd(zl, decay, kr2, avec, bvec, late_shards, late_placed)
    p = dict(p, **dict(zip(_LATE, late)))
    (o_a,) = _tok_fwd("hpost_fwd", _fn_hpost, [tok(o_raw), tok(z, 3)], [p["gnorm"]], [(d, BF16)])
    post_toks = [tok(y), tok(zl, 0), tok(kr2), tok(zl, 2), tok(gate)]
    (o_b,) = _tok_fwd("rpost_fwd", _fn_rpost, post_toks, post_params, [(d, BF16)])
    y_a = _mm("branch_a", o_a, p["w_a"], "nn")
    y_b = _mm("branch_b", o_b, p["w_b"], "nn")
    merge_toks = [tok(z, C_G // 256, 256), tok(z, (C_G + d) // 256, 256), tok(y_a, 0, 256), tok(y_b, 0, 256)]
    (merged,) = _tok_fwd("merge_fwd", _fn_merge, merge_toks, [], [(256, BF16)], col_grid=4, tm=512)
    mix = _mm("out_proj", merged, p["w_out"], "nn")
    h1, xn2 = _tok_fwd("res1_fwd", _fn_res1, [tok(x), tok(mix)], [p["g_post1"], p["g_pre2"]], [(d, F32), (d, BF16)])
    hu = _mm("up_proj", xn2, p["w_up_t"], "nt", tm=t, tn=512)
    act = _conv_fwd(hu, p["conv_w"], p["conv_b"])
    ff = _mm("down_proj", act, p["w_down"], "nn")
    d_h1, d_ff, loss, g["g_post2"] = _loss_head(h1, ff, target, p["g_post2"])

    d_act = _mm("d_act", d_ff, p["w_down"], "nt")
    g["w_down"] = _mm("dw_down", act, d_ff, "tn", tm=256, tn=1024)
    d_hu, g["conv_w"], g["conv_b"] = _conv_bwd(hu, p["conv_w"], p["conv_b"], d_act)
    d_xn2 = _mm("d_xn2", d_hu, p["w_up_t"], "nn", tk=D_FF)
    g["w_up_t"] = _mm("dw_up", d_hu, xn2, "tn", tm=512, tn=1024)
    d_x_res, d_mix, g["g_post1"], g["g_pre2"] = _tok_bwd(
        "res1_bwd", _fn_res1, [tok(x), tok(mix)], [p["g_post1"], p["g_pre2"]], [[tok(d_h1)], [tok(d_xn2)]],
        [(d, F32), (d, BF16)])
    d_merged = _mm("d_merged", d_mix, p["w_out"], "nt")
    g["w_out"] = _mm("dw_out", merged, d_mix, "tn")
    d_ga, d_gb, d_ya, d_yb = _tok_bwd("merge_bwd", _fn_merge, merge_toks, [], [[tok(d_merged, 0, 256)]],
                                      [(256, BF16)] * 4, col_grid=4, tm=512)
    d_oa = _mm("d_oa", d_ya, p["w_a"], "nt")
    g["w_a"] = _mm("dw_a", o_a, d_ya, "tn")
    d_ob = _mm("d_ob", d_yb, p["w_b"], "nt")
    g["w_b"] = _mm("dw_b", o_b, d_yb, "tn")
    d_oraw, d_hg, g["gnorm"] = _tok_bwd("hpost_bwd", _fn_hpost, [tok(o_raw), tok(z, 3)], [p["gnorm"]], [[tok(d_oa)]],
                                        [(d, F32), (d, BF16)])
    d_y, d_r1, d_kr2_1, d_v1, d_gate, g["ln_w"], g["ln_b"], g["r_k"] = _tok_bwd(
        "rpost_bwd", _fn_rpost, post_toks, post_params, [[tok(d_ob)]], f32x(5))
    stage1 = _rs_stage1([g[n] for n in _LATE], place, "late")
    (d_r2, d_decay, d_kr2_2, d_v2, d_avec, d_bvec), from_chips = _rwkv_bwd(
        zl, decay, kr2, avec, bvec, r_states, d_y, stage1[2])
    g.update(zip(_LATE, _rs_stage3(stage1, from_chips, place, "late")))
    prep = _tok_bwd("rprep_bwd", _fn_rprep, [tok(zl, 1), lora], rp_params,
                    [[tok(d_decay)], [tok(d_kr2_1), tok(d_kr2_2)], [tok(d_avec)], [tok(d_bvec)], [tok(d_gate)]],
                    [(d, F32), (LORA, F32)])
    d_kr, d_lora = prep[:2]
    g["w0"], g["w2p"], g["a0"], g["a2p"], g["g2p"], g["k_k"], g["k_a"] = prep[2:]
    d_zl = jnp.concatenate([d_r1 + d_r2, d_kr, d_v1 + d_v2, d_lora], axis=1)
    dz_r, g["mu"] = _lerp_bwd(z, p["mu"], d_zl)
    d_qe, d_dec, d_u = _hgrn_state_bwd(d_oraw, qe, dec, h_states)
    d_q_in, d_k_in, d_kd, d_vi = _hgrn_local_bwd(q_in, k_in, kd, z, d_oraw, d_u)
    d_hq, d_hf, g["lb2"] = _tok_bwd("hgates_bwd", _fn_hgates, [tok(z, 0), tok(z, 1)], [p["lb2"]],
                                    [[tok(d_q_in)], [tok(d_k_in)], [tok(d_kd)], [tok(d_qe)], [tok(d_dec)]], [(d, BF16)] * 2)
    dz = jnp.concatenate([d_hq, d_hf, d_vi.astype(BF16), d_hg, dz_r, d_ga, d_gb], axis=1)
    d_xn = _mm("d_xn", dz, p["w_in_t"], "nn", tm=1024, tn=512, tk=IN_COLS // 2)
    g["w_in_t"] = _mm("dw_in", dz, xn, "tn", tm=256, tn=1024)
    grad_x, g["g1"] = _tok_bwd("norm1_bwd", _fn_norm, [tok(x)], [p["g1"]], [[tok(d_xn)]], [(d, F32)],
                               add_to_first=tok(d_x_res))
    return loss, grad_x, g


_WEIGHTS = ["attn_pre_norm", "w_in", "hgrn_lb", "hgrn_gnorm", "w_branch_a", "rwkv_mu", "rwkv_w0", "rwkv_w2", "rwkv_a0",
            "rwkv_a2", "rwkv_g2", "rwkv_k_k", "rwkv_k_a", "rwkv_r_k", "rwkv_ln_w", "rwkv_ln_b", "w_branch_b", "w_out",
            "attn_post_norm", "ffn_pre_norm", "w_up", "conv_w", "conv_b", "w_down", "ffn_post_norm"]
_REPLICATED = [("attn_pre_norm", "g1"), ("hgrn_lb", "lb2"), ("hgrn_gnorm", "gnorm"), ("rwkv_mu", "mu"), ("rwkv_w0", "w0"),
               ("rwkv_a0", "a0"), ("rwkv_k_k", "k_k"), ("rwkv_k_a", "k_a"), ("rwkv_r_k", "r_k"), ("rwkv_ln_w", "ln_w"),
               ("rwkv_ln_b", "ln_b"), ("attn_post_norm", "g_post1"), ("ffn_pre_norm", "g_pre2"), ("conv_b", "conv_b"),
               ("ffn_post_norm", "g_post2")]
SLAB_COLS = 1024


def _pack(arrays):
    pieces, total = [], 0
    for a in arrays:
        flat = a.reshape(-1)
        rows = -(-flat.shape[0] // SLAB_COLS)
        pieces.append(jnp.pad(flat, (0, rows * SLAB_COLS - flat.shape[0])).reshape(rows, SLAB_COLS))
        total += rows
    if total % 8:
        pieces.append(jnp.zeros((8 - total % 8, SLAB_COLS), F32))
    return jnp.concatenate(pieces, axis=0)


def _unpack(slab, shapes):
    out, at = [], 0
    for s in shapes:
        size = 1
        for dim in s:
            size *= dim
        rows = -(-size // SLAB_COLS)
        out.append(slab[at:at + rows].reshape(-1)[:size].reshape(s))
        at += rows
    return out


def kernel(x, attn_pre_norm, w_in, hgrn_lb, hgrn_gnorm, w_branch_a, rwkv_mu, rwkv_w0, rwkv_w2, rwkv_a0, rwkv_a2, rwkv_g2, rwkv_k_k, rwkv_k_a, rwkv_r_k, rwkv_ln_w, rwkv_ln_b, w_branch_b, w_out, attn_post_norm, ffn_pre_norm, w_up, conv_w, conv_b, w_down, ffn_post_norm, loss_target, m_attn_pre_norm, m_w_in, m_hgrn_lb, m_hgrn_gnorm, m_w_branch_a, m_rwkv_mu, m_rwkv_w0, m_rwkv_w2, m_rwkv_a0, m_rwkv_a2, m_rwkv_g2, m_rwkv_k_k, m_rwkv_k_a, m_rwkv_r_k, m_rwkv_ln_w, m_rwkv_ln_b, m_w_branch_b, m_w_out, m_attn_post_norm, m_ffn_pre_norm, m_w_up, m_conv_w, m_conv_b, m_w_down, m_ffn_post_norm, v_attn_pre_norm, v_w_in, v_hgrn_lb, v_hgrn_gnorm, v_w_branch_a, v_rwkv_mu, v_rwkv_w0, v_rwkv_w2, v_rwkv_a0, v_rwkv_a2, v_rwkv_g2, v_rwkv_k_k, v_rwkv_k_a, v_rwkv_r_k, v_rwkv_ln_w, v_rwkv_ln_b, v_w_branch_b, v_w_out, v_attn_post_norm, v_ffn_pre_norm, v_w_up, v_conv_w, v_conv_b, v_w_down, v_ffn_post_norm):
    given = dict(locals())
    w = {n: given[n] for n in _WEIGHTS}
    mom = {n: given["m_" + n] for n in _WEIGHTS}
    var = {n: given["v_" + n] for n in _WEIGHTS}
    shard = 2 * lax.axis_index("x") + lax.axis_index("y")
    place = jnp.stack([lax.axis_index("c"), shard]).astype(jnp.int32)
    row = lambda a: a.reshape(1, -1)
    lora_of = lambda d: jnp.concatenate([d["rwkv_w2"][0], d["rwkv_a2"][0], d["rwkv_g2"][0]], axis=0)

    shards = [w["w_in"][0].T.astype(BF16), lora_of(w), jnp.pad(w["conv_w"][0], ((0, 29), (0, 0)))]
    late_shards = [w["w_branch_a"][0].astype(BF16), w["w_branch_b"][0].astype(BF16), w["w_out"][0].astype(BF16),
                   w["w_down"][0].astype(BF16), w["w_up"][0].T.astype(BF16)]
    placed = [_place_shard(f"place_{k}", a, place) for k, a in enumerate(shards)]
    late_placed = [_place_shard(f"place_late_{k}", a, place) for k, a in enumerate(late_shards)]
    w_in_t, lora_g, conv_g = _gather_shards(shards, placed)
    lora_full = lora_g.reshape(N_SHARD, LORA, 256).transpose(1, 0, 2).reshape(LORA, D_MODEL)
    conv_full = conv_g.reshape(N_SHARD, 32, 2 * D_FF // N_SHARD)[:, :3].transpose(1, 0, 2).reshape(3, 2 * D_FF)
    lrow = lax.broadcasted_iota(jnp.int32, (LORA, 1), 0)
    p = {
        "g1": row(w["attn_pre_norm"]), "lb2": w["hgrn_lb"], "gnorm": row(w["hgrn_gnorm"]), "w_in_t": w_in_t,
        "mu": row(w["rwkv_mu"]), "w0": row(w["rwkv_w0"]), "a0": row(w["rwkv_a0"]),
        "w2p": jnp.where(lrow < 64, lora_full, 0.0), "a2p": jnp.where((lrow >= 64) & (lrow < 128), lora_full, 0.0),
        "g2p": jnp.where(lrow >= 128, lora_full, 0.0),
        "k_k": row(w["rwkv_k_k"]), "k_a": row(w["rwkv_k_a"]), "r_k": row(w["rwkv_r_k"]), "ln_w": row(w["rwkv_ln_w"]),
        "ln_b": row(w["rwkv_ln_b"]), "g_post1": row(w["attn_post_norm"]),
        "g_pre2": row(w["ffn_pre_norm"]), "conv_w": conv_full, "conv_b": row(w["conv_b"]),
        "g_post2": row(w["ffn_post_norm"]),
    }

    loss, grad_x, g = _device_step(x[0], loss_target[0], p, late_shards, late_placed, place)

    (g_in_t,) = _reduce_scatter([g["w_in_t"]], place)
    g_a, g_b, g_o, g_dn, g_up_t = [g[n] for n in _LATE]
    rep_shapes = [w[n].shape for n, _ in _REPLICATED]
    rep = _pack([g[key] for _, key in _REPLICATED])
    n_rep_rows = rep.shape[0]
    cw = 2 * D_FF // N_SHARD
    lora_rows, conv_rows = LORA * 256 // SLAB_COLS, -(-3 * cw // SLAB_COLS)
    lora_g = jnp.concatenate([g["w2p"][0:64], g["a2p"][64:128], g["g2p"][128:256]], axis=0)
    lora_parts = lora_g.reshape(LORA, N_SHARD, 256).transpose(1, 0, 2).reshape(N_SHARD, lora_rows, SLAB_COLS)
    conv_parts = g["conv_w"].reshape(3, N_SHARD, cw).transpose(1, 0, 2).reshape(N_SHARD, 3 * cw)
    conv_parts = jnp.pad(conv_parts, ((0, 0), (0, conv_rows * SLAB_COLS - 3 * cw))).reshape(N_SHARD, conv_rows, SLAB_COLS)
    n_rows = n_rep_rows + lora_rows + conv_rows
    fill = jnp.zeros((N_SHARD, -n_rows % 8, SLAB_COLS), F32)
    parts = jnp.concatenate([jnp.broadcast_to(rep, (N_SHARD,) + rep.shape), lora_parts, conv_parts, fill], axis=1)
    me = 4 * lax.axis_index("x") + 2 * lax.axis_index("y") + lax.axis_index("c")
    landing = lax.dynamic_update_slice(jnp.zeros((8,) + parts.shape[1:], F32),
                                       lax.dynamic_index_in_dim(parts, shard, 0, keepdims=True), (me, 0, 0))
    gathered = _exchange8(parts, landing)
    summed = _sum3d("small_sum", [(gathered, functools.partial(lambda l, s, i: i, i=i)) for i in range(8)], place, 1, 1,
                    lambda l, s: 0)[0]
    lora_grad = summed[n_rep_rows:n_rep_rows + lora_rows].reshape(LORA, 256)
    conv_grad = summed[n_rep_rows + lora_rows:n_rows].reshape(-1)[:3 * cw].reshape(3, cw)

    res = {}

    def put(name, outs, shape=None):
        res[name] = [o.reshape(w[name].shape if shape is None else shape) for o in outs]

    put("w_in", _adamw("adamw_w_in", w["w_in"][0], g_in_t, mom["w_in"][0], var["w_in"][0], 1024, 128, g_transposed=True))
    put("w_up", _adamw("adamw_w_up", w["w_up"][0], g_up_t, mom["w_up"][0], var["w_up"][0], 1024, 128, g_transposed=True))
    for name, grad in (("w_branch_a", g_a), ("w_branch_b", g_b), ("w_out", g_o)):
        put(name, _adamw("adamw_" + name, w[name][0], grad, mom[name][0], var[name][0], 256, 1024))
    put("w_down", _adamw("adamw_w_down", w["w_down"][0], g_dn, mom["w_down"][0], var["w_down"][0], 176, 1024))
    put("conv_w", _adamw("adamw_conv_w", w["conv_w"][0], conv_grad, mom["conv_w"][0], var["conv_w"][0], 3, 2 * D_FF // N_SHARD))
    lora_out = _adamw("adamw_lora", lora_of(w), lora_grad, lora_of(mom), lora_of(var), LORA, 256)
    for name, lo, hi in (("rwkv_w2", 0, 64), ("rwkv_a2", 64, 128), ("rwkv_g2", 128, 256)):
        put(name, [o[lo:hi] for o in lora_out])
    rep_names = [n for n, _ in _REPLICATED]
    rep_out = _adamw("adamw_small", _pack([w[n] for n in rep_names]), summed[:n_rep_rows], _pack([mom[n] for n in rep_names]),
                     _pack([var[n] for n in rep_names]), n_rep_rows, SLAB_COLS)
    for name, parts in zip(rep_names, zip(*[_unpack(o, rep_shapes) for o in rep_out])):
        put(name, list(parts))

    loss = lax.psum(loss[0, 0], ("x", "y", "c"))
    return (loss, grad_x[None], *[res[n][0] for n in _WEIGHTS], *[res[n][1] for n in _WEIGHTS],
            *[res[n][2] for n in _WEIGHTS], *[res[n][3] for n in _WEIGHTS])
```

```python
import functools

import jax
import jax.numpy as jnp
from jax import lax
from jax.experimental import pallas as pl
from jax.experimental.pallas import tpu as pltpu

F32, BF16 = jnp.float32, jnp.bfloat16
MESH = pl.DeviceIdType.MESH

D_MODEL = 1024
HGRN_HEADS = 8
HGRN_K = 128
HGRN_SCALE = HGRN_K ** -0.5
CHUNK = 32
RWKV_HEAD = 64
LORA = 256
D_FF = 2816
EPS = 1e-6
GN_EPS = 1e-5 * RWKV_HEAD
N_SHARD = 4
ADAM_LR, ADAM_B1, ADAM_B2, ADAM_EPS, ADAM_WD, ADAM_STEP = 0.001, 0.9, 0.999, 1e-08, 0.01, 10

LANES = 128
VMEM_LIMIT = 56 * 1024 * 1024
SCAN_TB = 16
SCAN_GROUP = 256

C_HQ, C_HF, C_HI, C_HG = 0, 1024, 2048, 3072
C_R = 4096
R_COLS = 3328
C_G = 7424
IN_COLS = 9472


def _params(sem=None, **kw):
    return pltpu.CompilerParams(dimension_semantics=sem, vmem_limit_bytes=VMEM_LIMIT, **kw)


def _seg_matrix(n, seg):
    r = lax.broadcasted_iota(jnp.int32, (n, n), 0) // seg
    c = lax.broadcasted_iota(jnp.int32, (n, n), 1) // seg
    return (r == c).astype(BF16)


def _split3(x):
    hi = x.astype(BF16)
    r1 = x - hi.astype(F32)
    mid = r1.astype(BF16)
    lo = (r1 - mid.astype(F32)).astype(BF16)
    return hi, mid, lo


def _segsum_impl(x, seg):
    e = _seg_matrix(LANES, seg)
    outs = []
    for g in range(x.shape[1] // LANES):
        hi, mid, lo = _split3(x[:, g * LANES:(g + 1) * LANES])
        outs.append(jnp.dot(hi, e, preferred_element_type=F32) + jnp.dot(mid, e, preferred_element_type=F32)
                    + jnp.dot(lo, e, preferred_element_type=F32))
    return outs[0] if len(outs) == 1 else jnp.concatenate(outs, axis=1)


def _make_segsum(seg):
    @jax.custom_vjp
    def f(x):
        return _segsum_impl(x, seg)

    f.defvjp(lambda x: (_segsum_impl(x, seg), None), lambda _, ct: (_segsum_impl(ct, seg),))
    return f


_segsum64 = _make_segsum(RWKV_HEAD)
_segsum128 = _make_segsum(HGRN_K)


def _chunk_mm_impl(x, kind, transposed):
    n = x.shape[0]
    r = lax.broadcasted_iota(jnp.int32, (n, n), 1 if transposed else 0)
    c = lax.broadcasted_iota(jnp.int32, (n, n), 0 if transposed else 1)
    same = (r // CHUNK) == (c // CHUNK)
    if kind == "cumsum":
        m = same & (r >= c)
    else:
        m = same & (c % CHUNK == (CHUNK // 2 - 1 if kind == "mid" else CHUNK - 1))
    m = m.astype(BF16)
    hi, mid, lo = _split3(x)
    return (jnp.dot(m, hi, preferred_element_type=F32) + jnp.dot(m, mid, preferred_element_type=F32)
            + jnp.dot(m, lo, preferred_element_type=F32))


def _make_chunk_mm(kind):
    @jax.custom_vjp
    def f(x):
        return _chunk_mm_impl(x, kind, False)

    f.defvjp(lambda x: (_chunk_mm_impl(x, kind, False), None), lambda _, ct: (_chunk_mm_impl(ct, kind, True),))
    return f


_chunk_cumsum = _make_chunk_mm("cumsum")
_chunk_mid = _make_chunk_mm("mid")
_chunk_last = _make_chunk_mm("last")


@jax.custom_vjp
def _bdot(x, w):
    return jnp.dot(x.astype(BF16), w.astype(BF16), preferred_element_type=F32)


def _bdot_fwd(x, w):
    return _bdot(x, w), (x, w)


def _bdot_bwd(res, ct):
    x, w = res
    ctb = ct.astype(BF16)
    dx = lax.dot_general(ctb, w.astype(BF16), (((1,), (1,)), ((), ())), preferred_element_type=F32)
    dw = lax.dot_general(x.astype(BF16), ctb, (((0,), (0,)), ((), ())), preferred_element_type=F32)
    return dx, dw


_bdot.defvjp(_bdot_fwd, _bdot_bwd)


def _sigmoid(x):
    return 1.0 / (1.0 + jnp.exp(-x))


def _silu(x):
    return x * _sigmoid(x)


def _softplus(x):
    return jnp.maximum(x, 0.0) + jnp.log(1.0 + jnp.exp(-jnp.abs(x)))


def _rms(x, g):
    return x * lax.rsqrt(jnp.mean(x * x, axis=-1, keepdims=True) + EPS) * g


def _fn_norm(t, p):
    return [_rms(t[0], p[0])]


def _fn_hgates(t, p):
    hq, hf = t
    lb2 = p[0]
    m = jnp.max(lb2, axis=0, keepdims=True)
    e = jnp.exp(lb2 - m)
    first = lax.broadcasted_iota(jnp.int32, e.shape, 0) == 0
    lb = jnp.sum(jnp.where(first, e, 0.0), axis=0, keepdims=True) / jnp.sum(e, axis=0, keepdims=True)
    f = lb + (1.0 - lb) * _sigmoid(hf)
    q, k = _silu(hq) * HGRN_SCALE, 1.0 - f
    b = _chunk_cumsum(jnp.log(f))
    b_ref, b_last = _chunk_mid(b), _chunk_last(b)
    return [q * jnp.exp(b - b_ref), k * jnp.exp(b_ref - b), k * jnp.exp(b_last - b), q * jnp.exp(b), jnp.exp(b_last)]


def _fn_hpost(t, p):
    o, hg = t
    ms = _segsum128(o * o) * (1.0 / HGRN_K)
    return [o * lax.rsqrt(ms + EPS) * p[0] * _silu(hg)]


def _fn_rprep(t, p):
    kr, lora = t
    w0, w2p, a0, a2p, g2p, k_k, k_a = p
    pre_w = w0 + _bdot(jnp.tanh(lora), w2p)
    w_log = -_softplus(-pre_w) - 0.5
    decay = jnp.exp(-jnp.exp(w_log))
    a = _sigmoid(a0 + _bdot(lora, a2p))
    g = _bdot(_sigmoid(lora), g2p)
    kk = kr * k_k
    kk = kk / jnp.maximum(jnp.sqrt(_segsum64(kk * kk)), 1e-12)
    kr2 = kr * (1.0 + (a - 1.0) * k_a)
    return [decay, kr2, -kk, kk * a, g]


def _fn_rpost(t, p):
    y, r, kr2, v, g = t
    ln_w, ln_b, r_k = p
    mu = _segsum64(y) * (1.0 / RWKV_HEAD)
    yc = y - mu
    var = _segsum64(yc * yc) * (1.0 / RWKV_HEAD)
    yn = yc * lax.rsqrt(var + GN_EPS) * ln_w + ln_b
    bonus = _segsum64(r * kr2 * r_k) * v
    return [(yn + bonus) * g]


def _fn_merge(t, p):
    ga, gb, ya, yb = t
    return [_sigmoid(ga) * ya + _sigmoid(gb) * yb]


def _fn_res1(t, p):
    x, mix = t
    h1 = x + _rms(mix, p[0])
    return [h1, _rms(h1, p[1])]


def _tok_call(name, fn, toks, params, outs, red_shapes=(), tm=128, col_grid=1):
    n_t, n_p, n_o = len(toks), len(params), len(outs)
    t_len = toks[0][0].shape[0]
    tm = min(tm, t_len)

    def body(*refs):
        tv = [r[...].astype(F32) for r in refs[:n_t]]
        pv = [r[...] for r in refs[n_t:n_t + n_p]]
        o, red = fn(tv, pv)
        for ref, val in zip(refs[n_t + n_p:n_t + n_p + n_o], o):
            ref[...] = val.astype(ref.dtype)
        red_refs = refs[n_t + n_p + n_o:]
        if red_refs:
            first = pl.program_id(0) == 0

            @pl.when(first)
            def _():
                for ref, val in zip(red_refs, red):
                    ref[...] = val

            @pl.when(jnp.logical_not(first))
            def _():
                for ref, val in zip(red_refs, red):
                    ref[...] += val

    in_specs = [pl.BlockSpec((tm, w), functools.partial(lambda i, j, c: (i, c + j), c=c)) for (_, w, c) in toks]
    in_specs += [pl.BlockSpec(p.shape, lambda i, j: (0, 0)) for p in params]
    out_specs = [pl.BlockSpec((tm, w), lambda i, j: (i, j)) for (w, _) in outs]
    out_specs += [pl.BlockSpec(s, lambda i, j: (0, 0)) for s in red_shapes]
    out_shape = [jax.ShapeDtypeStruct((t_len, w * col_grid), dt) for (w, dt) in outs]
    out_shape += [jax.ShapeDtypeStruct(s, F32) for s in red_shapes]
    return pl.pallas_call(
        body, name=name, grid=(t_len // tm, col_grid), in_specs=in_specs, out_specs=out_specs, out_shape=out_shape,
        compiler_params=_params(("arbitrary", "arbitrary")),
    )(*[a for (a, _, _) in toks], *params)


def _tok_fwd(name, fn, toks, params, outs, **kw):
    return _tok_call(name, lambda tv, pv: (fn(tv, pv), []), toks, params, outs, **kw)


def _tok_bwd(name, fn, toks, params, cts, want, add_to_first=None, **kw):
    n_t = len(toks)
    flat = [c for group in cts for c in group]
    extra = [] if add_to_first is None else [add_to_first]

    def bwd(tv, pv):
        prim, rest = tv[:n_t], tv[n_t:]
        ct, at = [], 0
        for group in cts:
            ct.append(functools.reduce(lambda u, v: u + v, rest[at:at + len(group)]))
            at += len(group)
        _, vjp = jax.vjp(lambda *a: tuple(fn(list(a[:n_t]), list(a[n_t:]))), *prim, *pv)
        g = vjp(tuple(ct))
        tok_grads = [g[i] for i in range(n_t) if want[i] is not None]
        if extra:
            tok_grads[0] = tok_grads[0] + rest[at]
        return tok_grads, list(g[n_t:])

    return _tok_call(name, bwd, list(toks) + flat + extra, params, [w for w in want if w is not None],
                     red_shapes=[p.shape for p in params], **kw)


def _mm(name, a, b, mode, out_dtype=F32, tm=None, tn=None, tk=None, riders=None, a_map=None, mk=None):
    if mode == "nn":
        (m, k), (_, n) = a.shape, b.shape
    elif mode == "nt":
        (m, k), (n, _) = a.shape, b.shape
    else:
        (k, m), (_, n) = a.shape, b.shape
    if mk is not None:
        m, k = mk
    tm = (512 if mode == "tn" else 2048) if tm is None else tm
    tn = (512 if mode == "tn" else 256) if tn is None else tn
    tk = k if tk is None else tk
    tm, tn = min(tm, m), min(tn, n)
    nk = k // tk
    assert m % tm == 0 and n % tn == 0 and k % tk == 0, (name, a.shape, b.shape, tm, tn, tk)
    a_spec = pl.BlockSpec((tk, tm), lambda i, j, q: (q, i)) if mode == "tn" else pl.BlockSpec((tm, tk), lambda i, j, q: (i, q))
    if a_map is not None:
        a_spec = pl.BlockSpec(a_spec.block_shape, a_map)
    b_spec = pl.BlockSpec((tn, tk), lambda i, j, q: (j, q)) if mode == "nt" else pl.BlockSpec((tk, tn), lambda i, j, q: (q, j))
    dn = {"nn": (((1,), (0,)), ((), ())), "nt": (((1,), (1,)), ((), ())), "tn": (((0,), (0,)), ((), ()))}[mode]
    grid = (m // tm, n // tn, nk)
    nr = 0 if riders is None else len(riders)

    def body(*refs):
        a_ref, b_ref, o_ref = refs[0], refs[1], refs[2 + nr]
        acc = refs[3 + 2 * nr] if nk > 1 else None
        if nr:
            exchange = (refs[2:2 + nr], refs[3 + nr:3 + 2 * nr], *refs[-2:])
            at = [pl.program_id(ax) for ax in range(3)]

            @pl.when((at[0] == 0) & (at[1] == 0) & (at[2] == 0))
            def _():
                _rs_chips_start(*exchange)

        p = lax.dot_general(a_ref[...], b_ref[...], dn, preferred_element_type=F32)
        if nk == 1:
            o_ref[...] = p.astype(o_ref.dtype)
        else:
            q = pl.program_id(2)

            @pl.when(q == 0)
            def _():
                acc[...] = p

            @pl.when(q > 0)
            def _():
                acc[...] += p

            @pl.when(q == nk - 1)
            def _():
                o_ref[...] = acc[...].astype(o_ref.dtype)

        if nr:
            @pl.when((at[0] == grid[0] - 1) & (at[1] == grid[1] - 1) & (at[2] == grid[2] - 1))
            def _():
                _rs_chips_finish(*exchange)

    scratch = [pltpu.VMEM((tm, tn), F32)] if nk > 1 else []
    out_specs = [pl.BlockSpec((tm, tn), lambda i, j, q: (i, j))]
    out_shape = [jax.ShapeDtypeStruct((m, n), out_dtype)]
    if nr:
        scratch += [pltpu.SemaphoreType.DMA((nr, 3)), pltpu.SemaphoreType.DMA((nr, 3))]
        out_specs += [ANY] * nr
        out_shape += [jax.ShapeDtypeStruct((3,) + r.shape[1:], r.dtype) for r in riders]
    outs = pl.pallas_call(
        body, name=name, grid=grid, in_specs=[a_spec, b_spec] + [ANY] * nr, out_specs=out_specs, out_shape=out_shape,
        scratch_shapes=scratch,
        compiler_params=_params(("arbitrary",) * 3 if nr else ("parallel", "parallel", "arbitrary")),
    )(a, b, *(riders or []))
    return (outs[0], outs[1:]) if nr else outs[0]


def _shift_down(z, n):
    rows = lax.broadcasted_iota(jnp.int32, z.shape, 0)
    return jnp.where(rows < n, 0.0, pltpu.roll(z, n, 0))


def _shift_up(z, n):
    t = z.shape[0]
    rows = lax.broadcasted_iota(jnp.int32, z.shape, 0)
    return jnp.where(rows >= t - n, 0.0, pltpu.roll(z, t - n, 0))


def _lerp_fwd(z, mu):
    t = z.shape[0]
    w = 256

    def body(z_ref, mu_ref, o_ref):
        zz = z_ref[...]
        o_ref[...] = zz + mu_ref[...] * (_shift_down(zz, 1) - zz)

    return pl.pallas_call(
        body, name="lerp_fwd", grid=(R_COLS // w,),
        in_specs=[pl.BlockSpec((t, w), lambda j: (0, C_R // w + j)), pl.BlockSpec((1, w), lambda j: (0, j))],
        out_specs=pl.BlockSpec((t, w), lambda j: (0, j)), out_shape=jax.ShapeDtypeStruct((t, R_COLS), F32),
        compiler_params=_params(("parallel",)),
    )(z, mu)


def _lerp_bwd(z, mu, d_r, d_k, d_v, d_lora):
    t = z.shape[0]
    w = 256
    per = D_MODEL // w

    def body(z_ref, mu_ref, r1_ref, r2_ref, k_ref, v1_ref, v2_ref, l_ref, dz_ref, dmu_ref):
        j = pl.program_id(0)
        zz, m = z_ref[...], mu_ref[...]
        d = jnp.where(j < per, r1_ref[...] + r2_ref[...],
                      jnp.where(j < 2 * per, k_ref[...], jnp.where(j < 3 * per, v1_ref[...] + v2_ref[...], l_ref[...])))
        dz_ref[...] = (d * (1.0 - m) + _shift_up(d * m, 1)).astype(dz_ref.dtype)
        dmu_ref[...] = jnp.sum(d * (_shift_down(zz, 1) - zz), axis=0, keepdims=True)

    piece = lambda first: pl.BlockSpec((t, w), lambda j: (0, jnp.clip(j - first, 0, per - 1)))
    return pl.pallas_call(
        body, name="lerp_bwd", grid=(R_COLS // w,),
        in_specs=[pl.BlockSpec((t, w), lambda j: (0, C_R // w + j)), pl.BlockSpec((1, w), lambda j: (0, j)),
                  piece(0), piece(0), piece(per), piece(2 * per), piece(2 * per), pl.BlockSpec((t, w), lambda j: (0, 0))],
        out_specs=[pl.BlockSpec((t, w), lambda j: (0, j)), pl.BlockSpec((1, w), lambda j: (0, j))],
        out_shape=[jax.ShapeDtypeStruct((t, R_COLS), BF16), jax.ShapeDtypeStruct((1, R_COLS), F32)],
        compiler_params=_params(("arbitrary",)),
    )(z, mu, *d_r, d_k, *d_v, d_lora)


CONV_TILE = 256
N_CONV_TILES = D_FF // CONV_TILE


def _conv(h, w, b):
    return b + w[0:1, :] * _shift_down(h, 2) + w[1:2, :] * _shift_down(h, 1) + w[2:3, :] * h


def _conv_fwd(hu, conv_w, conv_b):
    t = hu.shape[0]
    n = N_CONV_TILES

    def body(hg_ref, hv_ref, wg_ref, wv_ref, bg_ref, bv_ref, o_ref):
        gate = _conv(hg_ref[...], wg_ref[...], bg_ref[...])
        val = _conv(hv_ref[...], wv_ref[...], bv_ref[...])
        o_ref[...] = (_silu(gate) * val).astype(o_ref.dtype)

    col = lambda off: pl.BlockSpec((t, CONV_TILE), lambda j: (0, j + off))
    wspec = lambda off: pl.BlockSpec((3, CONV_TILE), lambda j: (0, j + off))
    bspec = lambda off: pl.BlockSpec((1, CONV_TILE), lambda j: (0, j + off))
    return pl.pallas_call(
        body, name="conv_fwd", grid=(n,),
        in_specs=[col(0), col(n), wspec(0), wspec(n), bspec(0), bspec(n)],
        out_specs=pl.BlockSpec((t, CONV_TILE), lambda j: (0, j)), out_shape=jax.ShapeDtypeStruct((t, D_FF), BF16),
        compiler_params=_params(("parallel",)),
    )(hu, hu, conv_w, conv_w, conv_b, conv_b)


def _conv_bwd(hu, conv_w, conv_b, d_act):
    t = hu.shape[0]
    n = N_CONV_TILES

    def body(hg_ref, hv_ref, wg_ref, wv_ref, bg_ref, bv_ref, d_ref, dh_ref, dw_ref, db_ref):
        hg, hv, wg, wv = hg_ref[...], hv_ref[...], wg_ref[...], wv_ref[...]
        gate = _conv(hg, wg, bg_ref[...])
        val = _conv(hv, wv, bv_ref[...])
        d = d_ref[...]
        sg = _sigmoid(gate)
        d_gate = d * val * (sg * (1.0 + gate * (1.0 - sg)))
        d_val = d * (gate * sg)
        for half, (dc, h, w) in enumerate(((d_gate, hg, wg), (d_val, hv, wv))):
            dh = w[2:3, :] * dc + w[1:2, :] * _shift_up(dc, 1) + w[0:1, :] * _shift_up(dc, 2)
            dh_ref[half] = dh.astype(dh_ref.dtype)
            dw_ref[half, 0:1, :] = jnp.sum(dc * _shift_down(h, 2), axis=0, keepdims=True)
            dw_ref[half, 1:2, :] = jnp.sum(dc * _shift_down(h, 1), axis=0, keepdims=True)
            dw_ref[half, 2:3, :] = jnp.sum(dc * h, axis=0, keepdims=True)
            db_ref[half] = jnp.sum(dc, axis=0, keepdims=True)

    gcol = lambda rows: pl.BlockSpec((rows, CONV_TILE), lambda j: (0, j))
    vcol = lambda rows: pl.BlockSpec((rows, CONV_TILE), lambda j: (0, j + n))
    both = lambda rows: pl.BlockSpec((2, rows, CONV_TILE), lambda j: (0, 0, j))
    return pl.pallas_call(
        body, name="conv_bwd", grid=(n,),
        in_specs=[gcol(t), vcol(t), gcol(3), vcol(3), gcol(1), vcol(1), gcol(t)],
        out_specs=[both(t), both(3), both(1)],
        out_shape=[jax.ShapeDtypeStruct((2, t, D_FF), BF16), jax.ShapeDtypeStruct((2, 3, D_FF), F32),
                   jax.ShapeDtypeStruct((2, 1, D_FF), F32)],
        compiler_params=_params(("parallel",)),
    )(hu, hu, conv_w, conv_w, conv_b, conv_b, d_act)


_NN = (((1,), (0,)), ((), ()))
_NT = (((1,), (1,)), ((), ()))
_TN = (((0,), (0,)), ((), ()))
HGRN_CB = 8


def _bf_dot(a, b, dn):
    return lax.dot_general(a.astype(BF16), b.astype(BF16), dn, preferred_element_type=F32)


def _tril():
    r = lax.broadcasted_iota(jnp.int32, (CHUNK, CHUNK), 0)
    return r >= lax.broadcasted_iota(jnp.int32, (CHUNK, CHUNK), 1)


def _hgrn_specs(t):
    rows = HGRN_CB * CHUNK
    head = pl.BlockSpec((rows, HGRN_K), lambda h, n: (n, h))
    v_head = pl.BlockSpec((rows, HGRN_K), lambda h, n: (n, C_HI // HGRN_K + h))
    mats = pl.BlockSpec((1, HGRN_CB, HGRN_K, HGRN_K), lambda h, n: (h, n, 0, 0))
    return head, v_head, mats, (HGRN_HEADS, t // rows)


def _hgrn_local_fwd(q_in, k_in, kd, z):
    t = q_in.shape[0]
    head, v_head, mats, grid = _hgrn_specs(t)

    def body(q_ref, k_ref, kd_ref, v_ref, o_ref, u_ref):
        tril = _tril()
        for n in range(HGRN_CB):
            rows = slice(n * CHUNK, (n + 1) * CHUNK)
            v = v_ref[rows, :]
            scores = jnp.where(tril, _bf_dot(q_ref[rows, :], k_ref[rows, :], _NT), 0.0)
            o_ref[rows, :] = _bf_dot(scores, v, _NN)
            u_ref[0, n] = _bf_dot(v, kd_ref[rows, :], _TN)

    return pl.pallas_call(
        body, name="hgrn_local_fwd", grid=grid, in_specs=[head, head, head, v_head], out_specs=[head, mats],
        out_shape=[jax.ShapeDtypeStruct((t, D_MODEL), F32),
                   jax.ShapeDtypeStruct((HGRN_HEADS, t // CHUNK, HGRN_K, HGRN_K), F32)],
        compiler_params=_params(("parallel", "parallel")),
    )(q_in, k_in, kd, z)


def _hgrn_state_specs(t, reverse=False):
    rows = HGRN_CB * CHUNK
    nb = t // rows
    at = (lambda n: nb - 1 - n) if reverse else (lambda n: n)
    tok = pl.BlockSpec((rows, D_MODEL), lambda n: (at(n), 0))
    mats = pl.BlockSpec((HGRN_HEADS, HGRN_CB, HGRN_K, HGRN_K), lambda n: (0, at(n), 0, 0))
    return tok, mats, nb


def _hgrn_state_fwd(o_intra, qe, dec, u):
    t = qe.shape[0]
    tok, mats, nb = _hgrn_state_specs(t)

    def body(oi_ref, qe_ref, dec_ref, u_ref, o_ref, st_ref, s_ref):
        @pl.when(pl.program_id(0) == 0)
        def _():
            s_ref[...] = jnp.zeros_like(s_ref)

        st = [s_ref[h] for h in range(HGRN_HEADS)]
        for n in range(HGRN_CB):
            rows = slice(n * CHUNK, (n + 1) * CHUNK)
            for h in range(HGRN_HEADS):
                cols = slice(h * HGRN_K, (h + 1) * HGRN_K)
                st_ref[h, n] = st[h]
                o_ref[rows, cols] = oi_ref[rows, cols] + _bf_dot(qe_ref[rows, cols], st[h], _NT)
                st[h] = st[h] * dec_ref[n * CHUNK:n * CHUNK + 1, cols] + u_ref[h, n]
        for h in range(HGRN_HEADS):
            s_ref[h] = st[h]

    return pl.pallas_call(
        body, name="hgrn_state_fwd", grid=(nb,), in_specs=[tok, tok, tok, mats], out_specs=[tok, mats],
        out_shape=[jax.ShapeDtypeStruct((t, D_MODEL), F32),
                   jax.ShapeDtypeStruct((HGRN_HEADS, t // CHUNK, HGRN_K, HGRN_K), F32)],
        scratch_shapes=[pltpu.VMEM((HGRN_HEADS, HGRN_K, HGRN_K), F32)],
        compiler_params=_params(("arbitrary",)),
    )(o_intra, qe, dec, u)


def _hgrn_state_bwd(d_o, qe, dec, states):
    t = qe.shape[0]
    tok_r, mats_r, nb = _hgrn_state_specs(t, reverse=True)

    def body(do_ref, qe_ref, dec_ref, st_ref, dqe_ref, ddec_ref, du_ref, d_ref):
        @pl.when(pl.program_id(0) == 0)
        def _():
            d_ref[...] = jnp.zeros_like(d_ref)

        first_row = lax.broadcasted_iota(jnp.int32, (CHUNK, HGRN_K), 0) == 0
        d = [d_ref[h] for h in range(HGRN_HEADS)]
        for n in reversed(range(HGRN_CB)):
            rows = slice(n * CHUNK, (n + 1) * CHUNK)
            for h in range(HGRN_HEADS):
                cols = slice(h * HGRN_K, (h + 1) * HGRN_K)
                st, do = st_ref[h, n], do_ref[rows, cols]
                du_ref[h, n] = d[h]
                ddec_ref[rows, cols] = jnp.where(first_row, jnp.sum(d[h] * st, axis=0, keepdims=True), 0.0)
                dqe_ref[rows, cols] = _bf_dot(do, st, _NN)
                d[h] = d[h] * dec_ref[n * CHUNK:n * CHUNK + 1, cols] + _bf_dot(do, qe_ref[rows, cols], _TN)
        for h in range(HGRN_HEADS):
            d_ref[h] = d[h]

    out = jax.ShapeDtypeStruct((t, D_MODEL), F32)
    return pl.pallas_call(
        body, name="hgrn_state_bwd", grid=(nb,), in_specs=[tok_r, tok_r, tok_r, mats_r], out_specs=[tok_r, tok_r, mats_r],
        out_shape=[out, out, jax.ShapeDtypeStruct((HGRN_HEADS, t // CHUNK, HGRN_K, HGRN_K), F32)],
        scratch_shapes=[pltpu.VMEM((HGRN_HEADS, HGRN_K, HGRN_K), F32)],
        compiler_params=_params(("arbitrary",)),
    )(d_o, qe, dec, states)


def _hgrn_local_bwd(q_in, k_in, kd, z, d_o, d_u):
    t = q_in.shape[0]
    head, v_head, mats, grid = _hgrn_specs(t)

    def body(q_ref, k_ref, kd_ref, v_ref, do_ref, du_ref, dq_ref, dk_ref, dkd_ref, dv_ref):
        tril = _tril()
        for n in range(HGRN_CB):
            rows = slice(n * CHUNK, (n + 1) * CHUNK)
            q, k, kd, v, do, du = q_ref[rows, :], k_ref[rows, :], kd_ref[rows, :], v_ref[rows, :], do_ref[rows, :], du_ref[0, n]
            scores = jnp.where(tril, _bf_dot(q, k, _NT), 0.0)
            d_scores = jnp.where(tril, _bf_dot(do, v, _NT), 0.0)
            dv_ref[rows, :] = _bf_dot(scores, do, _TN) + _bf_dot(kd, du, _NT)
            dkd_ref[rows, :] = _bf_dot(v, du, _NN)
            dq_ref[rows, :] = _bf_dot(d_scores, k, _NN)
            dk_ref[rows, :] = _bf_dot(d_scores, q, _TN)

    out = jax.ShapeDtypeStruct((t, D_MODEL), F32)
    return pl.pallas_call(
        body, name="hgrn_local_bwd", grid=grid, in_specs=[head, head, head, v_head, head, mats], out_specs=[head] * 4,
        out_shape=[out] * 4, compiler_params=_params(("parallel", "parallel")),
    )(q_in, k_in, kd, z, d_o, d_u)


def _split2(x):
    hi = x.astype(BF16)
    return hi, (x - hi.astype(F32)).astype(BF16)


def _seg_bcast(xs, e):
    parts = [t for x in xs for t in _split2(x)]
    out = jnp.dot(jnp.concatenate(parts, axis=0), e, preferred_element_type=F32)
    n = RWKV_HEAD
    return [out[2 * i * n:(2 * i + 1) * n] + out[(2 * i + 1) * n:(2 * i + 2) * n] for i in range(len(xs))]


def _rows_to_cols(rows, diag, e):
    zero = jnp.zeros((), BF16)
    parts = [jnp.where(diag, t, zero) for row in rows for t in _split2(row)]
    out = jnp.dot(jnp.concatenate(parts, axis=0), e, preferred_element_type=F32)
    n = RWKV_HEAD
    return [out[2 * i * n:(2 * i + 1) * n] + out[(2 * i + 1) * n:(2 * i + 2) * n] for i in range(len(rows))]


def _col_to_row(col, diag):
    return jnp.sum(jnp.where(diag, col, 0.0), axis=0, keepdims=True)


_SCAN_PAIRS = ((0, 1), (2, 3))


def _scan_consts():
    e = _seg_matrix(SCAN_GROUP, RWKV_HEAD)
    i = lax.broadcasted_iota(jnp.int32, (RWKV_HEAD, SCAN_GROUP), 0)
    l = lax.broadcasted_iota(jnp.int32, (RWKV_HEAD, SCAN_GROUP), 1)
    groups = [slice(g * SCAN_GROUP, (g + 1) * SCAN_GROUP) for g in range(D_MODEL // SCAN_GROUP)]
    return e, (l % RWKV_HEAD) == i, groups


def _rwkv_fwd(zl, w, k, a, b, shards, placed):
    t = zl.shape[0]
    nb = t // SCAN_TB
    n = len(shards)
    steps = range(SCAN_TB)

    def body(*refs):
        scan(*refs[:6], *refs[6 + 2 * n:8 + 2 * n], refs[8 + 3 * n])
        gather = (refs[6:6 + n], refs[8 + 2 * n:8 + 3 * n], *refs[9 + 3 * n:])

        @pl.when(pl.program_id(0) == 0)
        def _():
            _gather_start(*gather)

        @pl.when(pl.program_id(0) == nb - 1)
        def _():
            _gather_finish(*gather)

    def scan(r_ref, w_ref, k_ref, v_ref, a_ref, b_ref, y_ref, st_ref, s_ref):
        @pl.when(pl.program_id(0) == 0)
        def _():
            s_ref[...] = jnp.zeros_like(s_ref)

        e, diag, groups = _scan_consts()
        v_cols = [_rows_to_cols([v_ref[i:i + 1, sl] for i in steps], diag, e) for sl in groups]
        s = [s_ref[:, sl] for sl in groups]
        for i in steps:
            for pair in _SCAN_PAIRS:
                sas = _seg_bcast([s[g] * a_ref[i:i + 1, groups[g]] for g in pair], e)
                for g, sa in zip(pair, sas):
                    sl = groups[g]
                    s[g] = s[g] * w_ref[i:i + 1, sl] + sa * b_ref[i:i + 1, sl] + v_cols[g][i] * k_ref[i:i + 1, sl]
                    st_ref[i, :, sl] = s[g]
        for g, sl in enumerate(groups):
            s_ref[:, sl] = s[g]
            y_cols = _seg_bcast([st_ref[i, :, sl] * r_ref[i:i + 1, sl] for i in steps], e)
            for i in steps:
                y_ref[i:i + 1, sl] = _col_to_row(y_cols[i], diag)

    blk = pl.BlockSpec((SCAN_TB, D_MODEL), lambda n: (n, 0))
    v_blk = pl.BlockSpec((SCAN_TB, D_MODEL), lambda n: (n, 2))
    outs = pl.pallas_call(
        body, name="rwkv_fwd", grid=(nb,), in_specs=[blk, blk, blk, v_blk, blk, blk] + [ANY] * (2 * n),
        out_specs=[blk, pl.BlockSpec((SCAN_TB, RWKV_HEAD, D_MODEL), lambda i: (i, 0, 0))] + [ANY] * n,
        out_shape=[jax.ShapeDtypeStruct((t, D_MODEL), F32), jax.ShapeDtypeStruct((t, RWKV_HEAD, D_MODEL), F32)]
        + [jax.ShapeDtypeStruct(p.shape, p.dtype) for p in placed],
        input_output_aliases={6 + n + i: 2 + i for i in range(n)},
        scratch_shapes=[pltpu.VMEM((RWKV_HEAD, D_MODEL), F32), pltpu.SemaphoreType.DMA((n, 6)), pltpu.SemaphoreType.DMA((n, 6))],
        compiler_params=_params(("arbitrary",)),
    )(zl, w, k, zl, a, b, *shards, *placed)
    return outs[0], outs[1], outs[2:]


def _rwkv_bwd(zl, w, k, a, b, states, d_y, parts):
    t = zl.shape[0]
    nb = t // SCAN_TB
    n = len(parts)
    steps = range(SCAN_TB)

    def body(*refs):
        scan(*refs[:9], *refs[9 + n:15 + n], refs[15 + 2 * n])
        exchange = (refs[9:9 + n], refs[15 + n:15 + 2 * n], *refs[16 + 2 * n:])

        @pl.when(pl.program_id(0) == 0)
        def _():
            _rs_chips_start(*exchange)

        @pl.when(pl.program_id(0) == nb - 1)
        def _():
            _rs_chips_finish(*exchange)

    def scan(r_ref, w_ref, k_ref, v_ref, a_ref, b_ref, st_ref, prev_ref, dy_ref,
             dr_ref, dw_ref, dk_ref, dv_ref, da_ref, db_ref, ds_ref):
        @pl.when(pl.program_id(0) == 0)
        def _():
            ds_ref[...] = jnp.zeros_like(ds_ref)

        has_prev = (pl.program_id(0) < nb - 1).astype(F32)
        e, diag, groups = _scan_consts()
        colsum = lambda x: jnp.sum(x, axis=0, keepdims=True)

        def s_prev(i, sl):
            return st_ref[i - 1, :, sl] if i > 0 else prev_ref[0, :, sl] * has_prev

        dy_cols = [_rows_to_cols([dy_ref[i:i + 1, sl] for i in steps], diag, e) for sl in groups]
        v_cols = [_rows_to_cols([v_ref[i:i + 1, sl] for i in steps], diag, e) for sl in groups]
        sa_cols = [_seg_bcast([s_prev(i, sl) * a_ref[i:i + 1, sl] for i in steps], e) for sl in groups]
        ds = [ds_ref[:, sl] for sl in groups]
        dsk = [[None] * SCAN_TB for _ in groups]
        for i in reversed(steps):
            for pair in _SCAN_PAIRS:
                d = {}
                for g in pair:
                    sl = groups[g]
                    d[g] = ds[g] + dy_cols[g][i] * r_ref[i:i + 1, sl]
                    dr_ref[i:i + 1, sl] = colsum(st_ref[i, :, sl] * dy_cols[g][i])
                    dw_ref[i:i + 1, sl] = colsum(d[g] * s_prev(i, sl))
                    db_ref[i:i + 1, sl] = colsum(d[g] * sa_cols[g][i])
                    dk_ref[i:i + 1, sl] = colsum(d[g] * v_cols[g][i])
                    dsk[g][i] = d[g] * k_ref[i:i + 1, sl]
                dsas = _seg_bcast([d[g] * b_ref[i:i + 1, groups[g]] for g in pair], e)
                for g, dsa in zip(pair, dsas):
                    sl = groups[g]
                    da_ref[i:i + 1, sl] = colsum(s_prev(i, sl) * dsa)
                    ds[g] = d[g] * w_ref[i:i + 1, sl] + dsa * a_ref[i:i + 1, sl]
        for g, sl in enumerate(groups):
            ds_ref[:, sl] = ds[g]
            dv_cols = _seg_bcast(dsk[g], e)
            for i in steps:
                dv_ref[i:i + 1, sl] = _col_to_row(dv_cols[i], diag)

    blk = pl.BlockSpec((SCAN_TB, D_MODEL), lambda n: (nb - 1 - n, 0))
    v_blk = pl.BlockSpec((SCAN_TB, D_MODEL), lambda n: (nb - 1 - n, 2))
    out = jax.ShapeDtypeStruct((t, D_MODEL), F32)
    outs = pl.pallas_call(
        body, name="rwkv_bwd", grid=(nb,),
        in_specs=[blk, blk, blk, v_blk, blk, blk] + [
            pl.BlockSpec((SCAN_TB, RWKV_HEAD, D_MODEL), lambda i: (nb - 1 - i, 0, 0)),
            pl.BlockSpec((1, RWKV_HEAD, D_MODEL), lambda i: (jnp.maximum((nb - 1 - i) * SCAN_TB - 1, 0), 0, 0)),
            blk] + [ANY] * n,
        out_specs=[blk] * 6 + [ANY] * n,
        out_shape=[out] * 6 + [jax.ShapeDtypeStruct((3,) + p.shape[1:], p.dtype) for p in parts],
        scratch_shapes=[pltpu.VMEM((RWKV_HEAD, D_MODEL), F32), pltpu.SemaphoreType.DMA((n, 3)), pltpu.SemaphoreType.DMA((n, 3))],
        compiler_params=_params(("arbitrary",)),
    )(zl, w, k, zl, a, b, states, states, d_y, *parts)
    return outs[:6], outs[6:]


def _loss_head(h1, ff, target, g_post):
    def fn(tv, pv):
        a, f, tgt = tv
        h2, vjp = jax.vjp(lambda a_, f_, g_: a_ + _rms(f_, g_), a, f, pv[0])
        err = h2 - tgt
        loss = 0.5 * jnp.sum(jnp.mean(err * err, axis=-1, keepdims=True), axis=0, keepdims=True)
        d_a, d_f, d_g = vjp(err * (1.0 / D_MODEL))
        return [d_a, d_f], [loss, d_g]

    return _tok_call("loss_head", fn, [(h1, D_MODEL, 0), (ff, D_MODEL, 0), (target, D_MODEL, 0)], [g_post],
                     [(D_MODEL, F32), (D_MODEL, BF16)], red_shapes=[(1, 1), (1, D_MODEL)])


def _sum_call(name, terms, rows_per_block=None):
    a0, i0 = terms[0]
    r, c = a0.shape[-2:]
    tr = rows_per_block or r

    def body(*refs):
        acc = refs[0][...].reshape(tr, c)
        for ref in refs[1:-1]:
            acc = acc + ref[...].reshape(tr, c)
        refs[-1][...] = acc

    def spec(arr, idx):
        if arr.ndim == 2:
            return pl.BlockSpec((tr, c), lambda i: (i, 0))
        return pl.BlockSpec((1, tr, c), functools.partial(lambda i, idx: (idx, i, 0), idx=idx))

    return pl.pallas_call(
        body, name=name, grid=(r // tr,), in_specs=[spec(a, i) for a, i in terms],
        out_specs=pl.BlockSpec((tr, c), lambda i: (i, 0)), out_shape=jax.ShapeDtypeStruct((r, c), F32),
        compiler_params=_params(("parallel",)),
    )(*[a for a, _ in terms])


def _adamw_math(w, g, m, v):
    m2 = ADAM_B1 * m + (1.0 - ADAM_B1) * g
    v2 = ADAM_B2 * v + (1.0 - ADAM_B2) * (g * g)
    m_hat = m2 / (1.0 - ADAM_B1 ** ADAM_STEP)
    v_hat = v2 / (1.0 - ADAM_B2 ** ADAM_STEP)
    return -ADAM_LR * (m_hat / (jnp.sqrt(v_hat) + ADAM_EPS) + ADAM_WD * w), m2, v2


def _adamw(name, w, g, m, v, bm, bn, g_transposed=False):
    r, c = w.shape

    def body(w_ref, g_ref, m_ref, v_ref, go_ref, d_ref, mo_ref, vo_ref):
        g = g_ref[...].T if g_transposed else g_ref[...]
        d, m2, v2 = _adamw_math(w_ref[...], g, m_ref[...], v_ref[...])
        go_ref[...] = g
        d_ref[...] = d
        mo_ref[...] = m2
        vo_ref[...] = v2

    blk = pl.BlockSpec((bm, bn), lambda i, j: (i, j))
    g_blk = pl.BlockSpec((bn, bm), lambda i, j: (j, i)) if g_transposed else blk
    out = jax.ShapeDtypeStruct((r, c), F32)
    return pl.pallas_call(
        body, name=name, grid=(pl.cdiv(r, bm), pl.cdiv(c, bn)), in_specs=[blk, g_blk, blk, blk],
        out_specs=[blk] * 4, out_shape=[out] * 4, compiler_params=_params(("parallel", "parallel")),
    )(w, g, m, v)


ANY = pl.BlockSpec(memory_space=pl.ANY)


def _place():
    x, y, c = lax.axis_index("x"), lax.axis_index("y"), lax.axis_index("c")
    chips = [(1 - x, y), (x, 1 - y), (1 - x, 1 - y)]
    return x, y, c, chips


def _sibling():
    return (lax.axis_index("x"), lax.axis_index("y"), 1 - lax.axis_index("c"))


def _wait_all(local, remote):
    for cp in local:
        cp.wait()
    for cp in remote:
        cp.wait_send()


def _place_shard(name, shard, place):
    r, cols = shard.shape
    tr = r // 4

    def body(s_ref, in_ref, out_ref):
        out_ref[...] = in_ref[...]

    return pl.pallas_call(
        body, name=name,
        grid_spec=pltpu.PrefetchScalarGridSpec(
            num_scalar_prefetch=1, grid=(4,), in_specs=[pl.BlockSpec((tr, cols), lambda i, s: (i, 0))],
            out_specs=pl.BlockSpec((tr, cols), lambda i, s: (4 * s[1] + i, 0))),
        out_shape=jax.ShapeDtypeStruct((N_SHARD * r, cols), shard.dtype), compiler_params=_params(("arbitrary",)),
    )(place, shard)


def _gather_copies(ins, outs, send_sems, recv_sems):
    x, y, c, chips = _place()
    me, sibling = (x, y, c), _sibling()

    def rows(k, px, py, pc):
        h = ins[k].shape[0] // 2
        return outs[k].at[pl.ds((2 * px + py) * 2 * h + pc * h, h), :]

    def copy(k, j, block, to, src=None):
        return pltpu.make_async_remote_copy(
            src_ref=rows(k, *block) if src is None else src, dst_ref=rows(k, *block),
            send_sem=send_sems.at[k, j], recv_sem=recv_sems.at[k, j], device_id=to, device_id_type=MESH)

    each = [(k, j, chip) for k in range(len(ins)) for j, chip in enumerate(chips)]
    half = lambda k: ins[k].at[pl.ds(c * (ins[k].shape[0] // 2), ins[k].shape[0] // 2), :]
    first = [copy(k, j, me, (*chip, c), src=half(k)) for k, j, chip in each]
    arrive = [copy(k, j, (*chip, c), me) for k, j, chip in each]
    passed = [copy(k, 3 + j, (*chip, c), sibling) for k, j, chip in each]
    landed = [copy(k, 3 + j, (*chip, 1 - c), me) for k, j, chip in each]
    return first, arrive, passed, landed


def _gather_start(ins, outs, send_sems, recv_sems):
    for cp in _gather_copies(ins, outs, send_sems, recv_sems)[0]:
        cp.start()


def _gather_finish(ins, outs, send_sems, recv_sems):
    first, arrive, passed, landed = _gather_copies(ins, outs, send_sems, recv_sems)
    for arrival, forward in zip(arrive, passed):
        arrival.wait_recv()
        forward.start()
    for cp in landed:
        cp.wait_recv()
    _wait_all([], first + passed)


def _gather_shards(shards, placed):
    n = len(shards)

    def body(*refs):
        ins, outs = refs[:n], refs[2 * n:3 * n]
        _gather_start(ins, outs, *refs[3 * n:])
        _gather_finish(ins, outs, *refs[3 * n:])

    return pl.pallas_call(
        body, name="gather_shards", in_specs=[ANY] * (2 * n), out_specs=[ANY] * n,
        out_shape=[jax.ShapeDtypeStruct(a.shape, a.dtype) for a in placed],
        input_output_aliases={n + k: k for k in range(n)},
        scratch_shapes=[pltpu.SemaphoreType.DMA((n, 6)), pltpu.SemaphoreType.DMA((n, 6))],
    )(*shards, *placed)


def _exchange8(parts, landing):
    def body(in_ref, _, out_ref, send_sems, recv_sems):
        x, y, c, _ = _place()
        sends = []
        for rel in range(1, 8):
            dx, dy, dc = rel >> 2 & 1, rel >> 1 & 1, rel & 1
            cp = pltpu.make_async_remote_copy(
                src_ref=in_ref.at[2 * (x ^ dx) + (y ^ dy)], dst_ref=out_ref.at[4 * x + 2 * y + c],
                send_sem=send_sems.at[rel - 1], recv_sem=recv_sems.at[rel - 1],
                device_id=(x ^ dx, y ^ dy, c ^ dc), device_id_type=MESH)
            cp.start()
            sends.append(cp)
        for rel in range(1, 8):
            dx, dy, dc = rel >> 2 & 1, rel >> 1 & 1, rel & 1
            pltpu.make_async_remote_copy(
                src_ref=in_ref.at[0], dst_ref=out_ref.at[4 * (x ^ dx) + 2 * (y ^ dy) + (c ^ dc)],
                send_sem=send_sems.at[rel - 1], recv_sem=recv_sems.at[rel - 1],
                device_id=(x, y, c), device_id_type=MESH).wait_recv()
        _wait_all([], sends)

    return pl.pallas_call(
        body, name="exchange8", in_specs=[ANY, ANY], out_specs=ANY, out_shape=jax.ShapeDtypeStruct(landing.shape, landing.dtype),
        input_output_aliases={1: 0}, scratch_shapes=[pltpu.SemaphoreType.DMA((7,)), pltpu.SemaphoreType.DMA((7,))],
    )(parts, landing)


def _rs_sibling(grads):
    n = len(grads)

    def body(*refs):
        ins, outs = refs[:n], refs[n:2 * n]
        send_sems, recv_sems = refs[2 * n:]
        c = lax.axis_index("c")
        sends = []
        for k in range(n):
            for s in range(N_SHARD):
                cp = pltpu.make_async_remote_copy(
                    src_ref=ins[k].at[2 * s + 1 - c], dst_ref=outs[k].at[s], send_sem=send_sems.at[k, s],
                    recv_sem=recv_sems.at[k, s], device_id=_sibling(), device_id_type=MESH)
                cp.start()
                sends.append(cp)
        for cp in sends:
            cp.wait_recv()
        _wait_all([], sends)

    return pl.pallas_call(
        body, name="rs_sibling", in_specs=[ANY] * n, out_specs=[ANY] * n,
        out_shape=[jax.ShapeDtypeStruct((N_SHARD,) + a.shape[1:], a.dtype) for a in grads],
        scratch_shapes=[pltpu.SemaphoreType.DMA((n, N_SHARD)), pltpu.SemaphoreType.DMA((n, N_SHARD))],
    )(*grads)


def _rs_chips_copies(ins, outs, send_sems, recv_sems):
    x, y, c, chips = _place()
    return [pltpu.make_async_remote_copy(
        src_ref=ins[k].at[2 * px + py], dst_ref=outs[k].at[j], send_sem=send_sems.at[k, j], recv_sem=recv_sems.at[k, j],
        device_id=(px, py, c), device_id_type=MESH) for k in range(len(ins)) for j, (px, py) in enumerate(chips)]


def _rs_chips_start(ins, outs, send_sems, recv_sems):
    for cp in _rs_chips_copies(ins, outs, send_sems, recv_sems):
        cp.start()


def _rs_chips_finish(ins, outs, send_sems, recv_sems):
    sends = _rs_chips_copies(ins, outs, send_sems, recv_sems)
    for cp in sends:
        cp.wait_recv()
    _wait_all([], sends)


def _rs_finish(bufs):
    n = len(bufs)

    def body(*refs):
        outs = refs[n:2 * n]
        send_sems, recv_sems = refs[2 * n:]
        c = lax.axis_index("c")
        sends = []
        for k in range(n):
            cp = pltpu.make_async_remote_copy(
                src_ref=outs[k].at[c], dst_ref=outs[k].at[c], send_sem=send_sems.at[k], recv_sem=recv_sems.at[k],
                device_id=_sibling(), device_id_type=MESH)
            cp.start()
            sends.append(cp)
        for k in range(n):
            pltpu.make_async_remote_copy(
                src_ref=outs[k].at[c], dst_ref=outs[k].at[1 - c], send_sem=send_sems.at[k], recv_sem=recv_sems.at[k],
                device_id=_sibling(), device_id_type=MESH).wait_recv()
        _wait_all([], sends)

    return pl.pallas_call(
        body, name="rs_finish", in_specs=[ANY] * n, out_specs=[ANY] * n,
        out_shape=[jax.ShapeDtypeStruct(a.shape, a.dtype) for a in bufs], input_output_aliases={k: k for k in range(n)},
        scratch_shapes=[pltpu.SemaphoreType.DMA((n,)), pltpu.SemaphoreType.DMA((n,))],
    )(*bufs)


def _sum3d(name, terms, scalars, grid_lead, out_lead, out_index, tr=None, out_dtype=F32):
    h, c = terms[0][0].shape[1:]
    tr = tr or h

    def body(s_ref, *refs):
        acc = refs[0][...].astype(F32)
        for ref in refs[1:-1]:
            acc = acc + ref[...].astype(F32)
        refs[-1][...] = acc.astype(refs[-1].dtype)

    in_specs = [pl.BlockSpec((1, tr, c), functools.partial(lambda l, i, s_ref, f: (f(l, s_ref), i, 0), f=f)) for _, f in terms]
    out_spec = pl.BlockSpec((1, tr, c), lambda l, i, s_ref: (out_index(l, s_ref), i, 0))
    return pl.pallas_call(
        body, name=name,
        grid_spec=pltpu.PrefetchScalarGridSpec(num_scalar_prefetch=1, grid=(grid_lead, h // tr), in_specs=in_specs, out_specs=out_spec),
        out_shape=jax.ShapeDtypeStruct((out_lead, h, c), out_dtype), compiler_params=_params(("arbitrary", "arbitrary")),
    )(scalars, *[a for a, _ in terms])


def _rs_stage1(grads, place, tag):
    g8 = [a.reshape(2 * N_SHARD, a.shape[0] // (2 * N_SHARD), a.shape[1]) for a in grads]
    from_sibling = _rs_sibling(g8)
    parts = [_sum3d(f"rs_add1_{tag}{k}", [(g8[k], lambda l, s: 2 * l + s[0]), (from_sibling[k], lambda l, s: l)], place,
                    N_SHARD, N_SHARD, lambda l, s: l, tr=g8[k].shape[1] // 2, out_dtype=BF16)
             for k in range(len(grads))]
    return g8, from_sibling, parts


def _rs_stage3(stage1, from_chips, place, tag):
    g8, from_sibling, _ = stage1
    mine = [(lambda l, s: 2 * s[1] + s[0]), (lambda l, s: s[1])]
    bufs = [_sum3d(f"rs_add2_{tag}{k}", [(g8[k], mine[0]), (from_sibling[k], mine[1])]
                   + [(from_chips[k], functools.partial(lambda l, s, j: j, j=j)) for j in range(3)],
                   place, 1, 2, lambda l, s: s[0], tr=g8[k].shape[1] // 2)
            for k in range(len(g8))]
    whole = _rs_finish(bufs)
    return [w.reshape(2 * w.shape[1], w.shape[2]) for w in whole]


_LATE = ["w_a", "w_b", "w_out", "w_down", "w_up_t"]


def _device_step(x, target, p, late_shards, late_placed, place):
    t = x.shape[0]
    d = D_MODEL
    tok = lambda arr, c=0, w=d: (arr, w, c)
    f32x = lambda n: [(d, F32)] * n
    rp_params = [p["w0"], p["w2p"], p["a0"], p["a2p"], p["g2p"], p["k_k"], p["k_a"]]
    post_params = [p["ln_w"], p["ln_b"], p["r_k"]]
    g = {}

    (xn,) = _tok_fwd("norm1_fwd", _fn_norm, [tok(x)], [p["g1"]], [(d, BF16)])
    z = _mm("in_proj", xn, p["w_in_t"], "nt", tm=t, tn=256)
    q_in, k_in, kd, qe, dec = _tok_fwd("hgates_fwd", _fn_hgates, [tok(z, 0), tok(z, 1)], [p["lb2"]], f32x(5))
    o_intra, u = _hgrn_local_fwd(q_in, k_in, kd, z)
    o_raw, h_states = _hgrn_state_fwd(o_intra, qe, dec, u)
    zl = _lerp_fwd(z, p["mu"])
    lora = tok(zl, 3 * d // LORA, LORA)
    decay, kr2, avec, bvec, gate = _tok_fwd("rprep_fwd", _fn_rprep, [tok(zl, 1), lora], rp_params, f32x(5))
    y, r_states, late = _rwkv_fwd(zl, decay, kr2, avec, bvec, late_shards, late_placed)
    p = dict(p, **dict(zip(_LATE, late)))
    (o_a,) = _tok_fwd("hpost_fwd", _fn_hpost, [tok(o_raw), tok(z, 3)], [p["gnorm"]], [(d, BF16)])
    post_toks = [tok(y), tok(zl, 0), tok(kr2), tok(zl, 2), tok(gate)]
    (o_b,) = _tok_fwd("rpost_fwd", _fn_rpost, post_toks, post_params, [(d, BF16)])
    y_a = _mm("branch_a", o_a, p["w_a"], "nn")
    y_b = _mm("branch_b", o_b, p["w_b"], "nn")
    merge_toks = [tok(z, C_G // 256, 256), tok(z, (C_G + d) // 256, 256), tok(y_a, 0, 256), tok(y_b, 0, 256)]
    (merged,) = _tok_fwd("merge_fwd", _fn_merge, merge_toks, [], [(256, BF16)], col_grid=4, tm=512)
    mix = _mm("out_proj", merged, p["w_out"], "nn")
    h1, xn2 = _tok_fwd("res1_fwd", _fn_res1, [tok(x), tok(mix)], [p["g_post1"], p["g_pre2"]], [(d, F32), (d, BF16)])
    hu = _mm("up_proj", xn2, p["w_up_t"], "nt", tm=t, tn=512)
    act = _conv_fwd(hu, p["conv_w"], p["conv_b"])
    ff = _mm("down_proj", act, p["w_down"], "nn")
    d_h1, d_ff, loss, g["g_post2"] = _loss_head(h1, ff, target, p["g_post2"])

    d_act = _mm("d_act", d_ff, p["w_down"], "nt")
    g["w_down"] = _mm("dw_down", act, d_ff, "tn", tm=256, tn=1024)
    d_hu, d_cw, d_cb = _conv_bwd(hu, p["conv_w"], p["conv_b"], d_act)
    g["conv_w"], g["conv_b"] = d_cw.transpose(1, 0, 2).reshape(3, 2 * D_FF), d_cb.reshape(1, 2 * D_FF)
    d_hu = d_hu.reshape(2 * t, D_FF)
    d_xn2 = _mm("d_xn2", d_hu, p["w_up_t"], "nn", tm=t, tk=D_FF, mk=(t, 2 * D_FF), a_map=lambda i, j, q: (q, 0))
    g["w_up_t"] = _mm("dw_up", d_hu, xn2, "tn", tm=CONV_TILE, tn=1024, mk=(2 * D_FF, t),
                      a_map=lambda i, j, q: (i // N_CONV_TILES, i % N_CONV_TILES))
    d_x_res, d_mix, g["g_post1"], g["g_pre2"] = _tok_bwd(
        "res1_bwd", _fn_res1, [tok(x), tok(mix)], [p["g_post1"], p["g_pre2"]], [[tok(d_h1)], [tok(d_xn2)]],
        [(d, F32), (d, BF16)])
    d_merged = _mm("d_merged", d_mix, p["w_out"], "nt")
    g["w_out"] = _mm("dw_out", merged, d_mix, "tn")
    d_ga, d_gb, d_ya, d_yb = _tok_bwd("merge_bwd", _fn_merge, merge_toks, [], [[tok(d_merged, 0, 256)]],
                                      [(256, BF16)] * 4, col_grid=4, tm=512)
    d_oa = _mm("d_oa", d_ya, p["w_a"], "nt")
    g["w_a"] = _mm("dw_a", o_a, d_ya, "tn")
    d_ob = _mm("d_ob", d_yb, p["w_b"], "nt")
    g["w_b"] = _mm("dw_b", o_b, d_yb, "tn")
    d_oraw, d_hg, g["gnorm"] = _tok_bwd("hpost_bwd", _fn_hpost, [tok(o_raw), tok(z, 3)], [p["gnorm"]], [[tok(d_oa)]],
                                        [(d, F32), (d, BF16)])
    d_y, d_r1, d_kr2_1, d_v1, d_gate, g["ln_w"], g["ln_b"], g["r_k"] = _tok_bwd(
        "rpost_bwd", _fn_rpost, post_toks, post_params, [[tok(d_ob)]], f32x(5))
    stage1 = _rs_stage1([g[n] for n in _LATE], place, "late")
    (d_r2, d_decay, d_kr2_2, d_v2, d_avec, d_bvec), from_chips = _rwkv_bwd(
        zl, decay, kr2, avec, bvec, r_states, d_y, stage1[2])
    g.update(zip(_LATE, _rs_stage3(stage1, from_chips, place, "late")))
    prep = _tok_bwd("rprep_bwd", _fn_rprep, [tok(zl, 1), lora], rp_params,
                    [[tok(d_decay)], [tok(d_kr2_1), tok(d_kr2_2)], [tok(d_avec)], [tok(d_bvec)], [tok(d_gate)]],
                    [(d, F32), (LORA, F32)])
    d_kr, d_lora = prep[:2]
    g["w0"], g["w2p"], g["a0"], g["a2p"], g["g2p"], g["k_k"], g["k_a"] = prep[2:]
    dz_r, g["mu"] = _lerp_bwd(z, p["mu"], (d_r1, d_r2), d_kr, (d_v1, d_v2), d_lora)
    d_qe, d_dec, d_u = _hgrn_state_bwd(d_oraw, qe, dec, h_states)
    d_q_in, d_k_in, d_kd, d_vi = _hgrn_local_bwd(q_in, k_in, kd, z, d_oraw, d_u)
    d_hq, d_hf, g["lb2"] = _tok_bwd("hgates_bwd", _fn_hgates, [tok(z, 0), tok(z, 1)], [p["lb2"]],
                                    [[tok(d_q_in)], [tok(d_k_in)], [tok(d_kd)], [tok(d_qe)], [tok(d_dec)]], [(d, BF16)] * 2)
    dz = jnp.concatenate([d_hq, d_hf, d_vi.astype(BF16), d_hg, dz_r, d_ga, d_gb], axis=1)
    stage1 = _rs_stage1([_mm("dw_in", dz, xn, "tn", tm=256, tn=1024)], place, "w_in")
    d_xn, from_chips = _mm("d_xn", dz, p["w_in_t"], "nn", tm=1024, tn=512, tk=IN_COLS // 2, riders=stage1[2])
    (g["w_in_t"],) = _rs_stage3(stage1, from_chips, place, "w_in")
    grad_x, g["g1"] = _tok_bwd("norm1_bwd", _fn_norm, [tok(x)], [p["g1"]], [[tok(d_xn)]], [(d, F32)],
                               add_to_first=tok(d_x_res))
    return loss, grad_x, g


_WEIGHTS = ["attn_pre_norm", "w_in", "hgrn_lb", "hgrn_gnorm", "w_branch_a", "rwkv_mu", "rwkv_w0", "rwkv_w2", "rwkv_a0",
            "rwkv_a2", "rwkv_g2", "rwkv_k_k", "rwkv_k_a", "rwkv_r_k", "rwkv_ln_w", "rwkv_ln_b", "w_branch_b", "w_out",
            "attn_post_norm", "ffn_pre_norm", "w_up", "conv_w", "conv_b", "w_down", "ffn_post_norm"]
_REPLICATED = [("attn_pre_norm", "g1"), ("hgrn_lb", "lb2"), ("hgrn_gnorm", "gnorm"), ("rwkv_mu", "mu"), ("rwkv_w0", "w0"),
               ("rwkv_a0", "a0"), ("rwkv_k_k", "k_k"), ("rwkv_k_a", "k_a"), ("rwkv_r_k", "r_k"), ("rwkv_ln_w", "ln_w"),
               ("rwkv_ln_b", "ln_b"), ("attn_post_norm", "g_post1"), ("ffn_pre_norm", "g_pre2"), ("conv_b", "conv_b"),
               ("ffn_post_norm", "g_post2")]
SLAB_COLS = 1024


def _pack(arrays):
    pieces, total = [], 0
    for a in arrays:
        flat = a.reshape(-1)
        rows = -(-flat.shape[0] // SLAB_COLS)
        pieces.append(jnp.pad(flat, (0, rows * SLAB_COLS - flat.shape[0])).reshape(rows, SLAB_COLS))
        total += rows
    if total % 8:
        pieces.append(jnp.zeros((8 - total % 8, SLAB_COLS), F32))
    return jnp.concatenate(pieces, axis=0)


def _unpack(slab, shapes):
    out, at = [], 0
    for s in shapes:
        size = 1
        for dim in s:
            size *= dim
        rows = -(-size // SLAB_COLS)
        out.append(slab[at:at + rows].reshape(-1)[:size].reshape(s))
        at += rows
    return out


def kernel(x, attn_pre_norm, w_in, hgrn_lb, hgrn_gnorm, w_branch_a, rwkv_mu, rwkv_w0, rwkv_w2, rwkv_a0, rwkv_a2, rwkv_g2, rwkv_k_k, rwkv_k_a, rwkv_r_k, rwkv_ln_w, rwkv_ln_b, w_branch_b, w_out, attn_post_norm, ffn_pre_norm, w_up, conv_w, conv_b, w_down, ffn_post_norm, loss_target, m_attn_pre_norm, m_w_in, m_hgrn_lb, m_hgrn_gnorm, m_w_branch_a, m_rwkv_mu, m_rwkv_w0, m_rwkv_w2, m_rwkv_a0, m_rwkv_a2, m_rwkv_g2, m_rwkv_k_k, m_rwkv_k_a, m_rwkv_r_k, m_rwkv_ln_w, m_rwkv_ln_b, m_w_branch_b, m_w_out, m_attn_post_norm, m_ffn_pre_norm, m_w_up, m_conv_w, m_conv_b, m_w_down, m_ffn_post_norm, v_attn_pre_norm, v_w_in, v_hgrn_lb, v_hgrn_gnorm, v_w_branch_a, v_rwkv_mu, v_rwkv_w0, v_rwkv_w2, v_rwkv_a0, v_rwkv_a2, v_rwkv_g2, v_rwkv_k_k, v_rwkv_k_a, v_rwkv_r_k, v_rwkv_ln_w, v_rwkv_ln_b, v_w_branch_b, v_w_out, v_attn_post_norm, v_ffn_pre_norm, v_w_up, v_conv_w, v_conv_b, v_w_down, v_ffn_post_norm):
    given = dict(locals())
    w = {n: given[n] for n in _WEIGHTS}
    mom = {n: given["m_" + n] for n in _WEIGHTS}
    var = {n: given["v_" + n] for n in _WEIGHTS}
    shard = 2 * lax.axis_index("x") + lax.axis_index("y")
    place = jnp.stack([lax.axis_index("c"), shard]).astype(jnp.int32)
    row = lambda a: a.reshape(1, -1)
    lora_of = lambda d: jnp.concatenate([d["rwkv_w2"][0], d["rwkv_a2"][0], d["rwkv_g2"][0]], axis=0)

    shards = [w["w_in"][0].T.astype(BF16), lora_of(w), jnp.pad(w["conv_w"][0], ((0, 29), (0, 0)))]
    late_shards = [w["w_branch_a"][0].astype(BF16), w["w_branch_b"][0].astype(BF16), w["w_out"][0].astype(BF16),
                   w["w_down"][0].astype(BF16), w["w_up"][0].T.astype(BF16)]
    placed = [_place_shard(f"place_{k}", a, place) for k, a in enumerate(shards)]
    late_placed = [_place_shard(f"place_late_{k}", a, place) for k, a in enumerate(late_shards)]
    w_in_t, lora_g, conv_g = _gather_shards(shards, placed)
    lora_full = lora_g.reshape(N_SHARD, LORA, 256).transpose(1, 0, 2).reshape(LORA, D_MODEL)
    conv_full = conv_g.reshape(N_SHARD, 32, 2 * D_FF // N_SHARD)[:, :3].transpose(1, 0, 2).reshape(3, 2 * D_FF)
    lrow = lax.broadcasted_iota(jnp.int32, (LORA, 1), 0)
    p = {
        "g1": row(w["attn_pre_norm"]), "lb2": w["hgrn_lb"], "gnorm": row(w["hgrn_gnorm"]), "w_in_t": w_in_t,
        "mu": row(w["rwkv_mu"]), "w0": row(w["rwkv_w0"]), "a0": row(w["rwkv_a0"]),
        "w2p": jnp.where(lrow < 64, lora_full, 0.0), "a2p": jnp.where((lrow >= 64) & (lrow < 128), lora_full, 0.0),
        "g2p": jnp.where(lrow >= 128, lora_full, 0.0),
        "k_k": row(w["rwkv_k_k"]), "k_a": row(w["rwkv_k_a"]), "r_k": row(w["rwkv_r_k"]), "ln_w": row(w["rwkv_ln_w"]),
        "ln_b": row(w["rwkv_ln_b"]), "g_post1": row(w["attn_post_norm"]),
        "g_pre2": row(w["ffn_pre_norm"]), "conv_w": conv_full, "conv_b": row(w["conv_b"]),
        "g_post2": row(w["ffn_post_norm"]),
    }

    loss, grad_x, g = _device_step(x[0], loss_target[0], p, late_shards, late_placed, place)

    g_in_t = g["w_in_t"]
    g_a, g_b, g_o, g_dn, g_up_t = [g[n] for n in _LATE]
    rep_shapes = [w[n].shape for n, _ in _REPLICATED]
    rep = _pack([g[key] for _, key in _REPLICATED])
    n_rep_rows = rep.shape[0]
    cw = 2 * D_FF // N_SHARD
    lora_rows, conv_rows = LORA * 256 // SLAB_COLS, -(-3 * cw // SLAB_COLS)
    lora_g = jnp.concatenate([g["w2p"][0:64], g["a2p"][64:128], g["g2p"][128:256]], axis=0)
    lora_parts = lora_g.reshape(LORA, N_SHARD, 256).transpose(1, 0, 2).reshape(N_SHARD, lora_rows, SLAB_COLS)
    conv_parts = g["conv_w"].reshape(3, N_SHARD, cw).transpose(1, 0, 2).reshape(N_SHARD, 3 * cw)
    conv_parts = jnp.pad(conv_parts, ((0, 0), (0, conv_rows * SLAB_COLS - 3 * cw))).reshape(N_SHARD, conv_rows, SLAB_COLS)
    n_rows = n_rep_rows + lora_rows + conv_rows
    fill = jnp.zeros((N_SHARD, -n_rows % 8, SLAB_COLS), F32)
    parts = jnp.concatenate([jnp.broadcast_to(rep, (N_SHARD,) + rep.shape), lora_parts, conv_parts, fill], axis=1)
    me = 4 * lax.axis_index("x") + 2 * lax.axis_index("y") + lax.axis_index("c")
    landing = lax.dynamic_update_slice(jnp.zeros((8,) + parts.shape[1:], F32),
                                       lax.dynamic_index_in_dim(parts, shard, 0, keepdims=True), (me, 0, 0))
    gathered = _exchange8(parts, landing)
    summed = _sum3d("small_sum", [(gathered, functools.partial(lambda l, s, i: i, i=i)) for i in range(8)], place, 1, 1,
                    lambda l, s: 0)[0]
    lora_grad = summed[n_rep_rows:n_rep_rows + lora_rows].reshape(LORA, 256)
    conv_grad = summed[n_rep_rows + lora_rows:n_rows].reshape(-1)[:3 * cw].reshape(3, cw)

    res = {}

    def put(name, outs, shape=None):
        res[name] = [o.reshape(w[name].shape if shape is None else shape) for o in outs]

    put("w_in", _adamw("adamw_w_in", w["w_in"][0], g_in_t, mom["w_in"][0], var["w_in"][0], 1024, 128, g_transposed=True))
    put("w_up", _adamw("adamw_w_up", w["w_up"][0], g_up_t, mom["w_up"][0], var["w_up"][0], 1024, 128, g_transposed=True))
    for name, grad in (("w_branch_a", g_a), ("w_branch_b", g_b), ("w_out", g_o)):
        put(name, _adamw("adamw_" + name, w[name][0], grad, mom[name][0], var[name][0], 256, 1024))
    put("w_down", _adamw("adamw_w_down", w["w_down"][0], g_dn, mom["w_down"][0], var["w_down"][0], 176, 1024))
    put("conv_w", _adamw("adamw_conv_w", w["conv_w"][0], conv_grad, mom["conv_w"][0], var["conv_w"][0], 3, 2 * D_FF // N_SHARD))
    lora_out = _adamw("adamw_lora", lora_of(w), lora_grad, lora_of(mom), lora_of(var), LORA, 256)
    for name, lo, hi in (("rwkv_w2", 0, 64), ("rwkv_a2", 64, 128), ("rwkv_g2", 128, 256)):
        put(name, [o[lo:hi] for o in lora_out])
    rep_names = [n for n, _ in _REPLICATED]
    rep_out = _adamw("adamw_small", _pack([w[n] for n in rep_names]), summed[:n_rep_rows], _pack([mom[n] for n in rep_names]),
                     _pack([var[n] for n in rep_names]), n_rep_rows, SLAB_COLS)
    for name, parts in zip(rep_names, zip(*[_unpack(o, rep_shapes) for o in rep_out])):
        put(name, list(parts))

    loss = lax.psum(loss[0, 0], ("x", "y", "c"))
    return (loss, grad_x[None], *[res[n][0] for n in _WEIGHTS], *[res[n][1] for n in _WEIGHTS],
            *[res[n][2] for n in _WEIGHTS], *[res[n][3] for n in _WEIGHTS])
```

```python
import functools

import jax
import jax.numpy as jnp
from jax import lax
from jax.experimental import pallas as pl
from jax.experimental.pallas import tpu as pltpu

F32, BF16 = jnp.float32, jnp.bfloat16
MESH = pl.DeviceIdType.MESH

D_MODEL = 1024
HGRN_HEADS = 8
HGRN_K = 128
HGRN_SCALE = HGRN_K ** -0.5
CHUNK = 32
RWKV_HEAD = 64
LORA = 256
D_FF = 2816
EPS = 1e-6
GN_EPS = 1e-5 * RWKV_HEAD
N_SHARD = 4
ADAM_LR, ADAM_B1, ADAM_B2, ADAM_EPS, ADAM_WD, ADAM_STEP = 0.001, 0.9, 0.999, 1e-08, 0.01, 10

LANES = 128
VMEM_LIMIT = 56 * 1024 * 1024
SCAN_TB = 16
SCAN_GROUP = 256

C_HQ, C_HF, C_HI, C_HG = 0, 1024, 2048, 3072
C_R = 4096
R_COLS = 3328
C_G = 7424
IN_COLS = 9472


def _params(sem=None, **kw):
    return pltpu.CompilerParams(dimension_semantics=sem, vmem_limit_bytes=VMEM_LIMIT, **kw)


def _seg_matrix(n, seg):
    r = lax.broadcasted_iota(jnp.int32, (n, n), 0) // seg
    c = lax.broadcasted_iota(jnp.int32, (n, n), 1) // seg
    return (r == c).astype(BF16)


def _split3(x):
    hi = x.astype(BF16)
    r1 = x - hi.astype(F32)
    mid = r1.astype(BF16)
    lo = (r1 - mid.astype(F32)).astype(BF16)
    return hi, mid, lo


def _segsum_impl(x, seg):
    e = _seg_matrix(LANES, seg)
    outs = []
    for g in range(x.shape[1] // LANES):
        hi, mid, lo = _split3(x[:, g * LANES:(g + 1) * LANES])
        outs.append(jnp.dot(hi, e, preferred_element_type=F32) + jnp.dot(mid, e, preferred_element_type=F32)
                    + jnp.dot(lo, e, preferred_element_type=F32))
    return outs[0] if len(outs) == 1 else jnp.concatenate(outs, axis=1)


def _make_segsum(seg):
    @jax.custom_vjp
    def f(x):
        return _segsum_impl(x, seg)

    f.defvjp(lambda x: (_segsum_impl(x, seg), None), lambda _, ct: (_segsum_impl(ct, seg),))
    return f


_segsum64 = _make_segsum(RWKV_HEAD)
_segsum128 = _make_segsum(HGRN_K)


def _chunk_mm_impl(x, kind, transposed):
    n = x.shape[0]
    r = lax.broadcasted_iota(jnp.int32, (n, n), 1 if transposed else 0)
    c = lax.broadcasted_iota(jnp.int32, (n, n), 0 if transposed else 1)
    same = (r // CHUNK) == (c // CHUNK)
    if kind == "cumsum":
        m = same & (r >= c)
    else:
        m = same & (c % CHUNK == (CHUNK // 2 - 1 if kind == "mid" else CHUNK - 1))
    m = m.astype(BF16)
    hi, mid, lo = _split3(x)
    return (jnp.dot(m, hi, preferred_element_type=F32) + jnp.dot(m, mid, preferred_element_type=F32)
            + jnp.dot(m, lo, preferred_element_type=F32))


def _make_chunk_mm(kind):
    @jax.custom_vjp
    def f(x):
        return _chunk_mm_impl(x, kind, False)

    f.defvjp(lambda x: (_chunk_mm_impl(x, kind, False), None), lambda _, ct: (_chunk_mm_impl(ct, kind, True),))
    return f


_chunk_cumsum = _make_chunk_mm("cumsum")
_chunk_mid = _make_chunk_mm("mid")
_chunk_last = _make_chunk_mm("last")


@jax.custom_vjp
def _bdot(x, w):
    return jnp.dot(x.astype(BF16), w.astype(BF16), preferred_element_type=F32)


def _bdot_fwd(x, w):
    return _bdot(x, w), (x, w)


def _bdot_bwd(res, ct):
    x, w = res
    ctb = ct.astype(BF16)
    dx = lax.dot_general(ctb, w.astype(BF16), (((1,), (1,)), ((), ())), preferred_element_type=F32)
    dw = lax.dot_general(x.astype(BF16), ctb, (((0,), (0,)), ((), ())), preferred_element_type=F32)
    return dx, dw


_bdot.defvjp(_bdot_fwd, _bdot_bwd)


def _sigmoid(x):
    return 1.0 / (1.0 + jnp.exp(-x))


def _silu(x):
    return x * _sigmoid(x)


def _softplus(x):
    return jnp.maximum(x, 0.0) + jnp.log(1.0 + jnp.exp(-jnp.abs(x)))


def _rms(x, g):
    return x * lax.rsqrt(jnp.mean(x * x, axis=-1, keepdims=True) + EPS) * g


def _fn_norm(t, p):
    return [_rms(t[0], p[0])]


def _fn_hgates(t, p):
    hq, hf = t
    lb2 = p[0]
    m = jnp.max(lb2, axis=0, keepdims=True)
    e = jnp.exp(lb2 - m)
    first = lax.broadcasted_iota(jnp.int32, e.shape, 0) == 0
    lb = jnp.sum(jnp.where(first, e, 0.0), axis=0, keepdims=True) / jnp.sum(e, axis=0, keepdims=True)
    f = lb + (1.0 - lb) * _sigmoid(hf)
    q, k = _silu(hq) * HGRN_SCALE, 1.0 - f
    b = _chunk_cumsum(jnp.log(f))
    b_ref, b_last = _chunk_mid(b), _chunk_last(b)
    return [q * jnp.exp(b - b_ref), k * jnp.exp(b_ref - b), k * jnp.exp(b_last - b), q * jnp.exp(b), jnp.exp(b_last)]


def _fn_hpost(t, p):
    o, hg = t
    ms = _segsum128(o * o) * (1.0 / HGRN_K)
    return [o * lax.rsqrt(ms + EPS) * p[0] * _silu(hg)]


def _fn_rprep(t, p):
    kr, lora = t
    w0, w2p, a0, a2p, g2p, k_k, k_a = p
    pre_w = w0 + _bdot(jnp.tanh(lora), w2p)
    w_log = -_softplus(-pre_w) - 0.5
    decay = jnp.exp(-jnp.exp(w_log))
    a = _sigmoid(a0 + _bdot(lora, a2p))
    g = _bdot(_sigmoid(lora), g2p)
    kk = kr * k_k
    kk = kk / jnp.maximum(jnp.sqrt(_segsum64(kk * kk)), 1e-12)
    kr2 = kr * (1.0 + (a - 1.0) * k_a)
    return [decay, kr2, -kk, kk * a, g]


def _fn_rpost(t, p):
    y, r, kr2, v, g = t
    ln_w, ln_b, r_k = p
    mu = _segsum64(y) * (1.0 / RWKV_HEAD)
    yc = y - mu
    var = _segsum64(yc * yc) * (1.0 / RWKV_HEAD)
    yn = yc * lax.rsqrt(var + GN_EPS) * ln_w + ln_b
    bonus = _segsum64(r * kr2 * r_k) * v
    return [(yn + bonus) * g]


def _fn_merge(t, p):
    ga, gb, ya, yb = t
    return [_sigmoid(ga) * ya + _sigmoid(gb) * yb]


def _fn_res1(t, p):
    x, mix = t
    h1 = x + _rms(mix, p[0])
    return [h1, _rms(h1, p[1])]


def _tok_call(name, fn, toks, params, outs, red_shapes=(), tm=128, col_grid=1):
    n_t, n_p, n_o = len(toks), len(params), len(outs)
    t_len = toks[0][0].shape[0]
    tm = min(tm, t_len)

    def body(*refs):
        tv = [r[...].astype(F32) for r in refs[:n_t]]
        pv = [r[...] for r in refs[n_t:n_t + n_p]]
        o, red = fn(tv, pv)
        for ref, val in zip(refs[n_t + n_p:n_t + n_p + n_o], o):
            ref[...] = val.astype(ref.dtype)
        red_refs = refs[n_t + n_p + n_o:]
        if red_refs:
            first = pl.program_id(0) == 0

            @pl.when(first)
            def _():
                for ref, val in zip(red_refs, red):
                    ref[...] = val

            @pl.when(jnp.logical_not(first))
            def _():
                for ref, val in zip(red_refs, red):
                    ref[...] += val

    in_specs = [pl.BlockSpec((tm, w), functools.partial(lambda i, j, c: (i, c + j), c=c)) for (_, w, c) in toks]
    in_specs += [pl.BlockSpec(p.shape, lambda i, j: (0, 0)) for p in params]
    out_specs = [pl.BlockSpec((tm, w), lambda i, j: (i, j)) for (w, _) in outs]
    out_specs += [pl.BlockSpec(s, lambda i, j: (0, 0)) for s in red_shapes]
    out_shape = [jax.ShapeDtypeStruct((t_len, w * col_grid), dt) for (w, dt) in outs]
    out_shape += [jax.ShapeDtypeStruct(s, F32) for s in red_shapes]
    return pl.pallas_call(
        body, name=name, grid=(t_len // tm, col_grid), in_specs=in_specs, out_specs=out_specs, out_shape=out_shape,
        compiler_params=_params(("arbitrary", "arbitrary")),
    )(*[a for (a, _, _) in toks], *params)


def _tok_fwd(name, fn, toks, params, outs, **kw):
    return _tok_call(name, lambda tv, pv: (fn(tv, pv), []), toks, params, outs, **kw)


def _tok_bwd(name, fn, toks, params, cts, want, add_to_first=None, **kw):
    n_t = len(toks)
    flat = [c for group in cts for c in group]
    extra = [] if add_to_first is None else [add_to_first]

    def bwd(tv, pv):
        prim, rest = tv[:n_t], tv[n_t:]
        ct, at = [], 0
        for group in cts:
            ct.append(functools.reduce(lambda u, v: u + v, rest[at:at + len(group)]))
            at += len(group)
        _, vjp = jax.vjp(lambda *a: tuple(fn(list(a[:n_t]), list(a[n_t:]))), *prim, *pv)
        g = vjp(tuple(ct))
        tok_grads = [g[i] for i in range(n_t) if want[i] is not None]
        if extra:
            tok_grads[0] = tok_grads[0] + rest[at]
        return tok_grads, list(g[n_t:])

    return _tok_call(name, bwd, list(toks) + flat + extra, params, [w for w in want if w is not None],
                     red_shapes=[p.shape for p in params], **kw)


def _mm(name, a, b, mode, out_dtype=F32, tm=None, tn=None, tk=None, riders=None, a_map=None, mk=None):
    if mode == "nn":
        (m, k), (_, n) = a.shape, b.shape
    elif mode == "nt":
        (m, k), (n, _) = a.shape, b.shape
    else:
        (k, m), (_, n) = a.shape, b.shape
    if mk is not None:
        m, k = mk
    tm = (512 if mode == "tn" else 2048) if tm is None else tm
    tn = (512 if mode == "tn" else 256) if tn is None else tn
    tk = k if tk is None else tk
    tm, tn = min(tm, m), min(tn, n)
    nk = k // tk
    assert m % tm == 0 and n % tn == 0 and k % tk == 0, (name, a.shape, b.shape, tm, tn, tk)
    a_spec = pl.BlockSpec((tk, tm), lambda i, j, q: (q, i)) if mode == "tn" else pl.BlockSpec((tm, tk), lambda i, j, q: (i, q))
    if a_map is not None:
        a_spec = pl.BlockSpec(a_spec.block_shape, a_map)
    b_spec = pl.BlockSpec((tn, tk), lambda i, j, q: (j, q)) if mode == "nt" else pl.BlockSpec((tk, tn), lambda i, j, q: (q, j))
    dn = {"nn": (((1,), (0,)), ((), ())), "nt": (((1,), (1,)), ((), ())), "tn": (((0,), (0,)), ((), ()))}[mode]
    grid = (m // tm, n // tn, nk)
    nr = 0 if riders is None else len(riders)

    def body(*refs):
        a_ref, b_ref, o_ref = refs[0], refs[1], refs[2 + nr]
        acc = refs[3 + 2 * nr] if nk > 1 else None
        if nr:
            exchange = (refs[2:2 + nr], refs[3 + nr:3 + 2 * nr], *refs[-2:])
            at = [pl.program_id(ax) for ax in range(3)]

            @pl.when((at[0] == 0) & (at[1] == 0) & (at[2] == 0))
            def _():
                _rs_chips_start(*exchange)

        p = lax.dot_general(a_ref[...], b_ref[...], dn, preferred_element_type=F32)
        if nk == 1:
            o_ref[...] = p.astype(o_ref.dtype)
        else:
            q = pl.program_id(2)

            @pl.when(q == 0)
            def _():
                acc[...] = p

            @pl.when(q > 0)
            def _():
                acc[...] += p

            @pl.when(q == nk - 1)
            def _():
                o_ref[...] = acc[...].astype(o_ref.dtype)

        if nr:
            @pl.when((at[0] == grid[0] - 1) & (at[1] == grid[1] - 1) & (at[2] == grid[2] - 1))
            def _():
                _rs_chips_finish(*exchange)

    scratch = [pltpu.VMEM((tm, tn), F32)] if nk > 1 else []
    out_specs = [pl.BlockSpec((tm, tn), lambda i, j, q: (i, j))]
    out_shape = [jax.ShapeDtypeStruct((m, n), out_dtype)]
    if nr:
        scratch += [pltpu.SemaphoreType.DMA((nr, 3)), pltpu.SemaphoreType.DMA((nr, 3))]
        out_specs += [ANY] * nr
        out_shape += [jax.ShapeDtypeStruct((3,) + r.shape[1:], r.dtype) for r in riders]
    outs = pl.pallas_call(
        body, name=name, grid=grid, in_specs=[a_spec, b_spec] + [ANY] * nr, out_specs=out_specs, out_shape=out_shape,
        scratch_shapes=scratch,
        compiler_params=_params(("arbitrary",) * 3 if nr else ("parallel", "parallel", "arbitrary")),
    )(a, b, *(riders or []))
    return (outs[0], outs[1:]) if nr else outs[0]


def _shift_down(z, n):
    rows = lax.broadcasted_iota(jnp.int32, z.shape, 0)
    return jnp.where(rows < n, 0.0, pltpu.roll(z, n, 0))


def _shift_up(z, n):
    t = z.shape[0]
    rows = lax.broadcasted_iota(jnp.int32, z.shape, 0)
    return jnp.where(rows >= t - n, 0.0, pltpu.roll(z, t - n, 0))


def _lerp_fwd(z, mu):
    t = z.shape[0]
    w = 256

    def body(z_ref, mu_ref, o_ref):
        zz = z_ref[...]
        o_ref[...] = zz + mu_ref[...] * (_shift_down(zz, 1) - zz)

    return pl.pallas_call(
        body, name="lerp_fwd", grid=(R_COLS // w,),
        in_specs=[pl.BlockSpec((t, w), lambda j: (0, C_R // w + j)), pl.BlockSpec((1, w), lambda j: (0, j))],
        out_specs=pl.BlockSpec((t, w), lambda j: (0, j)), out_shape=jax.ShapeDtypeStruct((t, R_COLS), F32),
        compiler_params=_params(("parallel",)),
    )(z, mu)


def _lerp_bwd(z, mu, d_r, d_k, d_v, d_lora):
    t = z.shape[0]
    w = 256
    per = D_MODEL // w

    def body(z_ref, mu_ref, r1_ref, r2_ref, k_ref, v1_ref, v2_ref, l_ref, dz_ref, dmu_ref):
        j = pl.program_id(0)
        zz, m = z_ref[...], mu_ref[...]
        d = jnp.where(j < per, r1_ref[...] + r2_ref[...],
                      jnp.where(j < 2 * per, k_ref[...], jnp.where(j < 3 * per, v1_ref[...] + v2_ref[...], l_ref[...])))
        dz_ref[...] = (d * (1.0 - m) + _shift_up(d * m, 1)).astype(dz_ref.dtype)
        dmu_ref[...] = jnp.sum(d * (_shift_down(zz, 1) - zz), axis=0, keepdims=True)

    piece = lambda first: pl.BlockSpec((t, w), lambda j: (0, jnp.clip(j - first, 0, per - 1)))
    return pl.pallas_call(
        body, name="lerp_bwd", grid=(R_COLS // w,),
        in_specs=[pl.BlockSpec((t, w), lambda j: (0, C_R // w + j)), pl.BlockSpec((1, w), lambda j: (0, j)),
                  piece(0), piece(0), piece(per), piece(2 * per), piece(2 * per), pl.BlockSpec((t, w), lambda j: (0, 0))],
        out_specs=[pl.BlockSpec((t, w), lambda j: (0, j)), pl.BlockSpec((1, w), lambda j: (0, j))],
        out_shape=[jax.ShapeDtypeStruct((t, R_COLS), BF16), jax.ShapeDtypeStruct((1, R_COLS), F32)],
        compiler_params=_params(("arbitrary",)),
    )(z, mu, *d_r, d_k, *d_v, d_lora)


CONV_TILE = 256
N_CONV_TILES = D_FF // CONV_TILE


def _conv(h, w, b):
    return b + w[0:1, :] * _shift_down(h, 2) + w[1:2, :] * _shift_down(h, 1) + w[2:3, :] * h


def _conv_fwd(hu, conv_w, conv_b):
    t = hu.shape[0]
    n = N_CONV_TILES

    def body(hg_ref, hv_ref, wg_ref, wv_ref, bg_ref, bv_ref, o_ref):
        gate = _conv(hg_ref[...], wg_ref[...], bg_ref[...])
        val = _conv(hv_ref[...], wv_ref[...], bv_ref[...])
        o_ref[...] = (_silu(gate) * val).astype(o_ref.dtype)

    col = lambda off: pl.BlockSpec((t, CONV_TILE), lambda j: (0, j + off))
    wspec = lambda off: pl.BlockSpec((3, CONV_TILE), lambda j: (0, j + off))
    bspec = lambda off: pl.BlockSpec((1, CONV_TILE), lambda j: (0, j + off))
    return pl.pallas_call(
        body, name="conv_fwd", grid=(n,),
        in_specs=[col(0), col(n), wspec(0), wspec(n), bspec(0), bspec(n)],
        out_specs=pl.BlockSpec((t, CONV_TILE), lambda j: (0, j)), out_shape=jax.ShapeDtypeStruct((t, D_FF), BF16),
        compiler_params=_params(("parallel",)),
    )(hu, hu, conv_w, conv_w, conv_b, conv_b)


def _conv_bwd(hu, conv_w, conv_b, d_act):
    t = hu.shape[0]
    n = N_CONV_TILES

    def body(hg_ref, hv_ref, wg_ref, wv_ref, bg_ref, bv_ref, d_ref, dh_ref, dw_ref, db_ref):
        hg, hv, wg, wv = hg_ref[...], hv_ref[...], wg_ref[...], wv_ref[...]
        gate = _conv(hg, wg, bg_ref[...])
        val = _conv(hv, wv, bv_ref[...])
        d = d_ref[...]
        sg = _sigmoid(gate)
        d_gate = d * val * (sg * (1.0 + gate * (1.0 - sg)))
        d_val = d * (gate * sg)
        for half, (dc, h, w) in enumerate(((d_gate, hg, wg), (d_val, hv, wv))):
            dh = w[2:3, :] * dc + w[1:2, :] * _shift_up(dc, 1) + w[0:1, :] * _shift_up(dc, 2)
            dh_ref[half] = dh.astype(dh_ref.dtype)
            dw_ref[half, 0:1, :] = jnp.sum(dc * _shift_down(h, 2), axis=0, keepdims=True)
            dw_ref[half, 1:2, :] = jnp.sum(dc * _shift_down(h, 1), axis=0, keepdims=True)
            dw_ref[half, 2:3, :] = jnp.sum(dc * h, axis=0, keepdims=True)
            db_ref[half] = jnp.sum(dc, axis=0, keepdims=True)

    gcol = lambda rows: pl.BlockSpec((rows, CONV_TILE), lambda j: (0, j))
    vcol = lambda rows: pl.BlockSpec((rows, CONV_TILE), lambda j: (0, j + n))
    both = lambda rows: pl.BlockSpec((2, rows, CONV_TILE), lambda j: (0, 0, j))
    return pl.pallas_call(
        body, name="conv_bwd", grid=(n,),
        in_specs=[gcol(t), vcol(t), gcol(3), vcol(3), gcol(1), vcol(1), gcol(t)],
        out_specs=[both(t), both(3), both(1)],
        out_shape=[jax.ShapeDtypeStruct((2, t, D_FF), BF16), jax.ShapeDtypeStruct((2, 3, D_FF), F32),
                   jax.ShapeDtypeStruct((2, 1, D_FF), F32)],
        compiler_params=_params(("parallel",)),
    )(hu, hu, conv_w, conv_w, conv_b, conv_b, d_act)


_NN = (((1,), (0,)), ((), ()))
_NT = (((1,), (1,)), ((), ()))
_TN = (((0,), (0,)), ((), ()))
HGRN_CB = 8


def _bf_dot(a, b, dn):
    return lax.dot_general(a.astype(BF16), b.astype(BF16), dn, preferred_element_type=F32)


def _tril():
    n = HGRN_CB * CHUNK
    r = lax.broadcasted_iota(jnp.int32, (n, n), 0)
    c = lax.broadcasted_iota(jnp.int32, (n, n), 1)
    return (r // CHUNK == c // CHUNK) & (r >= c)


def _hgrn_specs(t):
    rows = HGRN_CB * CHUNK
    head = pl.BlockSpec((rows, HGRN_K), lambda h, n: (n, h))
    v_head = pl.BlockSpec((rows, HGRN_K), lambda h, n: (n, C_HI // HGRN_K + h))
    mats = pl.BlockSpec((1, HGRN_CB, HGRN_K, HGRN_K), lambda h, n: (h, n, 0, 0))
    return head, v_head, mats, (HGRN_HEADS, t // rows)


def _hgrn_local_fwd(q_in, k_in, kd, z):
    t = q_in.shape[0]
    head, v_head, mats, grid = _hgrn_specs(t)

    def body(q_ref, k_ref, kd_ref, v_ref, o_ref, u_ref):
        v = v_ref[...]
        scores = jnp.where(_tril(), _bf_dot(q_ref[...], k_ref[...], _NT), 0.0)
        o_ref[...] = _bf_dot(scores, v, _NN)
        for n in range(HGRN_CB):
            rows = slice(n * CHUNK, (n + 1) * CHUNK)
            u_ref[0, n] = _bf_dot(v[rows], kd_ref[rows, :], _TN)

    return pl.pallas_call(
        body, name="hgrn_local_fwd", grid=grid, in_specs=[head, head, head, v_head], out_specs=[head, mats],
        out_shape=[jax.ShapeDtypeStruct((t, D_MODEL), F32),
                   jax.ShapeDtypeStruct((HGRN_HEADS, t // CHUNK, HGRN_K, HGRN_K), F32)],
        compiler_params=_params(("parallel", "parallel")),
    )(q_in, k_in, kd, z)


def _hgrn_state_specs(t, reverse=False):
    rows = HGRN_CB * CHUNK
    nb = t // rows
    at = (lambda n: nb - 1 - n) if reverse else (lambda n: n)
    tok = pl.BlockSpec((rows, D_MODEL), lambda n: (at(n), 0))
    mats = pl.BlockSpec((HGRN_HEADS, HGRN_CB, HGRN_K, HGRN_K), lambda n: (0, at(n), 0, 0))
    return tok, mats, nb


def _hgrn_state_fwd(o_intra, qe, dec, u):
    t = qe.shape[0]
    tok, mats, nb = _hgrn_state_specs(t)

    def body(oi_ref, qe_ref, dec_ref, u_ref, o_ref, st_ref, s_ref):
        @pl.when(pl.program_id(0) == 0)
        def _():
            s_ref[...] = jnp.zeros_like(s_ref)

        st = [s_ref[h] for h in range(HGRN_HEADS)]
        for n in range(HGRN_CB):
            rows = slice(n * CHUNK, (n + 1) * CHUNK)
            for h in range(HGRN_HEADS):
                cols = slice(h * HGRN_K, (h + 1) * HGRN_K)
                st_ref[h, n] = st[h]
                o_ref[rows, cols] = oi_ref[rows, cols] + _bf_dot(qe_ref[rows, cols], st[h], _NT)
                st[h] = st[h] * dec_ref[n * CHUNK:n * CHUNK + 1, cols] + u_ref[h, n]
        for h in range(HGRN_HEADS):
            s_ref[h] = st[h]

    return pl.pallas_call(
        body, name="hgrn_state_fwd", grid=(nb,), in_specs=[tok, tok, tok, mats], out_specs=[tok, mats],
        out_shape=[jax.ShapeDtypeStruct((t, D_MODEL), F32),
                   jax.ShapeDtypeStruct((HGRN_HEADS, t // CHUNK, HGRN_K, HGRN_K), F32)],
        scratch_shapes=[pltpu.VMEM((HGRN_HEADS, HGRN_K, HGRN_K), F32)],
        compiler_params=_params(("arbitrary",)),
    )(o_intra, qe, dec, u)


def _hgrn_state_bwd(d_o, qe, dec, states):
    t = qe.shape[0]
    tok_r, mats_r, nb = _hgrn_state_specs(t, reverse=True)

    def body(do_ref, qe_ref, dec_ref, st_ref, dqe_ref, ddec_ref, du_ref, d_ref):
        @pl.when(pl.program_id(0) == 0)
        def _():
            d_ref[...] = jnp.zeros_like(d_ref)

        first_row = lax.broadcasted_iota(jnp.int32, (CHUNK, HGRN_K), 0) == 0
        d = [d_ref[h] for h in range(HGRN_HEADS)]
        for n in reversed(range(HGRN_CB)):
            rows = slice(n * CHUNK, (n + 1) * CHUNK)
            for h in range(HGRN_HEADS):
                cols = slice(h * HGRN_K, (h + 1) * HGRN_K)
                st, do = st_ref[h, n], do_ref[rows, cols]
                du_ref[h, n] = d[h]
                ddec_ref[rows, cols] = jnp.where(first_row, jnp.sum(d[h] * st, axis=0, keepdims=True), 0.0)
                dqe_ref[rows, cols] = _bf_dot(do, st, _NN)
                d[h] = d[h] * dec_ref[n * CHUNK:n * CHUNK + 1, cols] + _bf_dot(do, qe_ref[rows, cols], _TN)
        for h in range(HGRN_HEADS):
            d_ref[h] = d[h]

    out = jax.ShapeDtypeStruct((t, D_MODEL), F32)
    return pl.pallas_call(
        body, name="hgrn_state_bwd", grid=(nb,), in_specs=[tok_r, tok_r, tok_r, mats_r], out_specs=[tok_r, tok_r, mats_r],
        out_shape=[out, out, jax.ShapeDtypeStruct((HGRN_HEADS, t // CHUNK, HGRN_K, HGRN_K), F32)],
        scratch_shapes=[pltpu.VMEM((HGRN_HEADS, HGRN_K, HGRN_K), F32)],
        compiler_params=_params(("arbitrary",)),
    )(d_o, qe, dec, states)


def _hgrn_local_bwd(q_in, k_in, kd, z, d_o, d_u):
    t = q_in.shape[0]
    head, v_head, mats, grid = _hgrn_specs(t)

    def body(q_ref, k_ref, kd_ref, v_ref, do_ref, du_ref, dq_ref, dk_ref, dkd_ref, dv_ref):
        tril = _tril()
        q, k, v, do = q_ref[...], k_ref[...], v_ref[...], do_ref[...]
        scores = jnp.where(tril, _bf_dot(q, k, _NT), 0.0)
        d_scores = jnp.where(tril, _bf_dot(do, v, _NT), 0.0)
        dq_ref[...] = _bf_dot(d_scores, k, _NN)
        dk_ref[...] = _bf_dot(d_scores, q, _TN)
        dv = _bf_dot(scores, do, _TN)
        for n in range(HGRN_CB):
            rows = slice(n * CHUNK, (n + 1) * CHUNK)
            du = du_ref[0, n]
            dv_ref[rows, :] = dv[rows] + _bf_dot(kd_ref[rows, :], du, _NT)
            dkd_ref[rows, :] = _bf_dot(v[rows], du, _NN)

    out = jax.ShapeDtypeStruct((t, D_MODEL), F32)
    return pl.pallas_call(
        body, name="hgrn_local_bwd", grid=grid, in_specs=[head, head, head, v_head, head, mats], out_specs=[head] * 4,
        out_shape=[out] * 4, compiler_params=_params(("parallel", "parallel")),
    )(q_in, k_in, kd, z, d_o, d_u)


def _split2(x):
    hi = x.astype(BF16)
    return hi, (x - hi.astype(F32)).astype(BF16)


def _seg_bcast(xs, e, one_term=False):
    n = RWKV_HEAD
    if one_term:
        out = jnp.dot(jnp.concatenate([x.astype(BF16) for x in xs], axis=0), e, preferred_element_type=F32)
        return [out[i * n:(i + 1) * n] for i in range(len(xs))]
    parts = [t for x in xs for t in _split2(x)]
    out = jnp.dot(jnp.concatenate(parts, axis=0), e, preferred_element_type=F32)
    return [out[2 * i * n:(2 * i + 1) * n] + out[(2 * i + 1) * n:(2 * i + 2) * n] for i in range(len(xs))]


def _rows_to_cols(rows, diag, e):
    zero = jnp.zeros((), BF16)
    parts = [jnp.where(diag, t, zero) for row in rows for t in _split2(row)]
    out = jnp.dot(jnp.concatenate(parts, axis=0), e, preferred_element_type=F32)
    n = RWKV_HEAD
    return [out[2 * i * n:(2 * i + 1) * n] + out[(2 * i + 1) * n:(2 * i + 2) * n] for i in range(len(rows))]


def _col_to_row(col, diag):
    return jnp.sum(jnp.where(diag, col, 0.0), axis=0, keepdims=True)


_SCAN_PAIRS = ((0, 1), (2, 3))


def _scan_consts():
    e = _seg_matrix(SCAN_GROUP, RWKV_HEAD)
    i = lax.broadcasted_iota(jnp.int32, (RWKV_HEAD, SCAN_GROUP), 0)
    l = lax.broadcasted_iota(jnp.int32, (RWKV_HEAD, SCAN_GROUP), 1)
    groups = [slice(g * SCAN_GROUP, (g + 1) * SCAN_GROUP) for g in range(D_MODEL // SCAN_GROUP)]
    return e, (l % RWKV_HEAD) == i, groups


def _rwkv_fwd(zl, w, k, a, b, shards, placed):
    t = zl.shape[0]
    nb = t // SCAN_TB
    n = len(shards)
    steps = range(SCAN_TB)

    def body(*refs):
        scan(*refs[:6], *refs[6 + 2 * n:8 + 2 * n], refs[8 + 3 * n])
        gather = (refs[6:6 + n], refs[8 + 2 * n:8 + 3 * n], *refs[9 + 3 * n:])

        @pl.when(pl.program_id(0) == 0)
        def _():
            _gather_start(*gather)

        @pl.when(pl.program_id(0) == nb - 1)
        def _():
            _gather_finish(*gather)

    def scan(r_ref, w_ref, k_ref, v_ref, a_ref, b_ref, y_ref, st_ref, s_ref):
        @pl.when(pl.program_id(0) == 0)
        def _():
            s_ref[...] = jnp.zeros_like(s_ref)

        e, diag, groups = _scan_consts()
        v_cols = [_rows_to_cols([v_ref[i:i + 1, sl] for i in steps], diag, e) for sl in groups]
        s = [s_ref[:, sl] for sl in groups]
        for i in steps:
            for pair in _SCAN_PAIRS:
                sas = _seg_bcast([s[g] * a_ref[i:i + 1, groups[g]] for g in pair], e)
                for g, sa in zip(pair, sas):
                    sl = groups[g]
                    s[g] = s[g] * w_ref[i:i + 1, sl] + sa * b_ref[i:i + 1, sl] + v_cols[g][i] * k_ref[i:i + 1, sl]
                    st_ref[i, :, sl] = s[g]
        for g, sl in enumerate(groups):
            s_ref[:, sl] = s[g]
            y_cols = _seg_bcast([st_ref[i, :, sl] * r_ref[i:i + 1, sl] for i in steps], e, one_term=True)
            for i in steps:
                y_ref[i:i + 1, sl] = _col_to_row(y_cols[i], diag)

    blk = pl.BlockSpec((SCAN_TB, D_MODEL), lambda n: (n, 0))
    v_blk = pl.BlockSpec((SCAN_TB, D_MODEL), lambda n: (n, 2))
    outs = pl.pallas_call(
        body, name="rwkv_fwd", grid=(nb,), in_specs=[blk, blk, blk, v_blk, blk, blk] + [ANY] * (2 * n),
        out_specs=[blk, pl.BlockSpec((SCAN_TB, RWKV_HEAD, D_MODEL), lambda i: (i, 0, 0))] + [ANY] * n,
        out_shape=[jax.ShapeDtypeStruct((t, D_MODEL), F32), jax.ShapeDtypeStruct((t, RWKV_HEAD, D_MODEL), F32)]
        + [jax.ShapeDtypeStruct(p.shape, p.dtype) for p in placed],
        input_output_aliases={6 + n + i: 2 + i for i in range(n)},
        scratch_shapes=[pltpu.VMEM((RWKV_HEAD, D_MODEL), F32), pltpu.SemaphoreType.DMA((n, 6)), pltpu.SemaphoreType.DMA((n, 6))],
        compiler_params=_params(("arbitrary",)),
    )(zl, w, k, zl, a, b, *shards, *placed)
    return outs[0], outs[1], outs[2:]


def _rwkv_bwd(zl, w, k, a, b, states, d_y, parts):
    t = zl.shape[0]
    nb = t // SCAN_TB
    n = len(parts)
    steps = range(SCAN_TB)

    def body(*refs):
        scan(*refs[:9], *refs[9 + n:15 + n], refs[15 + 2 * n])
        exchange = (refs[9:9 + n], refs[15 + n:15 + 2 * n], *refs[16 + 2 * n:])

        @pl.when(pl.program_id(0) == 0)
        def _():
            _rs_chips_start(*exchange)

        @pl.when(pl.program_id(0) == nb - 1)
        def _():
            _rs_chips_finish(*exchange)

    def scan(r_ref, w_ref, k_ref, v_ref, a_ref, b_ref, st_ref, prev_ref, dy_ref,
             dr_ref, dw_ref, dk_ref, dv_ref, da_ref, db_ref, ds_ref):
        @pl.when(pl.program_id(0) == 0)
        def _():
            ds_ref[...] = jnp.zeros_like(ds_ref)

        has_prev = (pl.program_id(0) < nb - 1).astype(F32)
        e, diag, groups = _scan_consts()
        colsum = lambda x: jnp.sum(x, axis=0, keepdims=True)

        def s_prev(i, sl):
            return st_ref[i - 1, :, sl] if i > 0 else prev_ref[0, :, sl] * has_prev

        dy_cols = [_rows_to_cols([dy_ref[i:i + 1, sl] for i in steps], diag, e) for sl in groups]
        v_cols = [_rows_to_cols([v_ref[i:i + 1, sl] for i in steps], diag, e) for sl in groups]
        sa_cols = [_seg_bcast([s_prev(i, sl) * a_ref[i:i + 1, sl] for i in steps], e, one_term=True) for sl in groups]
        ds = [ds_ref[:, sl] for sl in groups]
        dsk = [[None] * SCAN_TB for _ in groups]
        for i in reversed(steps):
            for pair in _SCAN_PAIRS:
                d = {}
                for g in pair:
                    sl = groups[g]
                    d[g] = ds[g] + dy_cols[g][i] * r_ref[i:i + 1, sl]
                    dr_ref[i:i + 1, sl] = colsum(st_ref[i, :, sl] * dy_cols[g][i])
                    dw_ref[i:i + 1, sl] = colsum(d[g] * s_prev(i, sl))
                    db_ref[i:i + 1, sl] = colsum(d[g] * sa_cols[g][i])
                    dk_ref[i:i + 1, sl] = colsum(d[g] * v_cols[g][i])
                    dsk[g][i] = d[g] * k_ref[i:i + 1, sl]
                dsas = _seg_bcast([d[g] * b_ref[i:i + 1, groups[g]] for g in pair], e)
                for g, dsa in zip(pair, dsas):
                    sl = groups[g]
                    da_ref[i:i + 1, sl] = colsum(s_prev(i, sl) * dsa)
                    ds[g] = d[g] * w_ref[i:i + 1, sl] + dsa * a_ref[i:i + 1, sl]
        for g, sl in enumerate(groups):
            ds_ref[:, sl] = ds[g]
            dv_cols = _seg_bcast(dsk[g], e, one_term=True)
            for i in steps:
                dv_ref[i:i + 1, sl] = _col_to_row(dv_cols[i], diag)

    blk = pl.BlockSpec((SCAN_TB, D_MODEL), lambda n: (nb - 1 - n, 0))
    v_blk = pl.BlockSpec((SCAN_TB, D_MODEL), lambda n: (nb - 1 - n, 2))
    out = jax.ShapeDtypeStruct((t, D_MODEL), F32)
    outs = pl.pallas_call(
        body, name="rwkv_bwd", grid=(nb,),
        in_specs=[blk, blk, blk, v_blk, blk, blk] + [
            pl.BlockSpec((SCAN_TB, RWKV_HEAD, D_MODEL), lambda i: (nb - 1 - i, 0, 0)),
            pl.BlockSpec((1, RWKV_HEAD, D_MODEL), lambda i: (jnp.maximum((nb - 1 - i) * SCAN_TB - 1, 0), 0, 0)),
            blk] + [ANY] * n,
        out_specs=[blk] * 6 + [ANY] * n,
        out_shape=[out] * 6 + [jax.ShapeDtypeStruct((3,) + p.shape[1:], p.dtype) for p in parts],
        scratch_shapes=[pltpu.VMEM((RWKV_HEAD, D_MODEL), F32), pltpu.SemaphoreType.DMA((n, 3)), pltpu.SemaphoreType.DMA((n, 3))],
        compiler_params=_params(("arbitrary",)),
    )(zl, w, k, zl, a, b, states, states, d_y, *parts)
    return outs[:6], outs[6:]


def _loss_head(h1, ff, target, g_post):
    def fn(tv, pv):
        a, f, tgt = tv
        h2, vjp = jax.vjp(lambda a_, f_, g_: a_ + _rms(f_, g_), a, f, pv[0])
        err = h2 - tgt
        loss = 0.5 * jnp.sum(jnp.mean(err * err, axis=-1, keepdims=True), axis=0, keepdims=True)
        d_a, d_f, d_g = vjp(err * (1.0 / D_MODEL))
        return [d_a, d_f], [loss, d_g]

    return _tok_call("loss_head", fn, [(h1, D_MODEL, 0), (ff, D_MODEL, 0), (target, D_MODEL, 0)], [g_post],
                     [(D_MODEL, F32), (D_MODEL, BF16)], red_shapes=[(1, 1), (1, D_MODEL)])


def _sum_call(name, terms, rows_per_block=None):
    a0, i0 = terms[0]
    r, c = a0.shape[-2:]
    tr = rows_per_block or r

    def body(*refs):
        acc = refs[0][...].reshape(tr, c)
        for ref in refs[1:-1]:
            acc = acc + ref[...].reshape(tr, c)
        refs[-1][...] = acc

    def spec(arr, idx):
        if arr.ndim == 2:
            return pl.BlockSpec((tr, c), lambda i: (i, 0))
        return pl.BlockSpec((1, tr, c), functools.partial(lambda i, idx: (idx, i, 0), idx=idx))

    return pl.pallas_call(
        body, name=name, grid=(r // tr,), in_specs=[spec(a, i) for a, i in terms],
        out_specs=pl.BlockSpec((tr, c), lambda i: (i, 0)), out_shape=jax.ShapeDtypeStruct((r, c), F32),
        compiler_params=_params(("parallel",)),
    )(*[a for a, _ in terms])


def _adamw_math(w, g, m, v):
    m2 = ADAM_B1 * m + (1.0 - ADAM_B1) * g
    v2 = ADAM_B2 * v + (1.0 - ADAM_B2) * (g * g)
    m_hat = m2 / (1.0 - ADAM_B1 ** ADAM_STEP)
    v_hat = v2 / (1.0 - ADAM_B2 ** ADAM_STEP)
    return -ADAM_LR * (m_hat / (jnp.sqrt(v_hat) + ADAM_EPS) + ADAM_WD * w), m2, v2


def _adamw(name, w, g, m, v, bm, bn, g_transposed=False):
    r, c = w.shape

    def body(w_ref, g_ref, m_ref, v_ref, go_ref, d_ref, mo_ref, vo_ref):
        g = g_ref[...].T if g_transposed else g_ref[...]
        d, m2, v2 = _adamw_math(w_ref[...], g, m_ref[...], v_ref[...])
        go_ref[...] = g
        d_ref[...] = d
        mo_ref[...] = m2
        vo_ref[...] = v2

    blk = pl.BlockSpec((bm, bn), lambda i, j: (i, j))
    g_blk = pl.BlockSpec((bn, bm), lambda i, j: (j, i)) if g_transposed else blk
    out = jax.ShapeDtypeStruct((r, c), F32)
    return pl.pallas_call(
        body, name=name, grid=(pl.cdiv(r, bm), pl.cdiv(c, bn)), in_specs=[blk, g_blk, blk, blk],
        out_specs=[blk] * 4, out_shape=[out] * 4, compiler_params=_params(("parallel", "parallel")),
    )(w, g, m, v)


ANY = pl.BlockSpec(memory_space=pl.ANY)


def _place():
    x, y, c = lax.axis_index("x"), lax.axis_index("y"), lax.axis_index("c")
    chips = [(1 - x, y), (x, 1 - y), (1 - x, 1 - y)]
    return x, y, c, chips


def _sibling():
    return (lax.axis_index("x"), lax.axis_index("y"), 1 - lax.axis_index("c"))


def _wait_all(local, remote):
    for cp in local:
        cp.wait()
    for cp in remote:
        cp.wait_send()


def _place_shard(name, shard, place):
    r, cols = shard.shape
    tr = r // 4

    def body(s_ref, in_ref, out_ref):
        out_ref[...] = in_ref[...]

    return pl.pallas_call(
        body, name=name,
        grid_spec=pltpu.PrefetchScalarGridSpec(
            num_scalar_prefetch=1, grid=(4,), in_specs=[pl.BlockSpec((tr, cols), lambda i, s: (i, 0))],
            out_specs=pl.BlockSpec((tr, cols), lambda i, s: (4 * s[1] + i, 0))),
        out_shape=jax.ShapeDtypeStruct((N_SHARD * r, cols), shard.dtype), compiler_params=_params(("arbitrary",)),
    )(place, shard)


def _gather_copies(ins, outs, send_sems, recv_sems):
    x, y, c, chips = _place()
    me, sibling = (x, y, c), _sibling()

    def rows(k, px, py, pc):
        h = ins[k].shape[0] // 2
        return outs[k].at[pl.ds((2 * px + py) * 2 * h + pc * h, h), :]

    def copy(k, j, block, to, src=None):
        return pltpu.make_async_remote_copy(
            src_ref=rows(k, *block) if src is None else src, dst_ref=rows(k, *block),
            send_sem=send_sems.at[k, j], recv_sem=recv_sems.at[k, j], device_id=to, device_id_type=MESH)

    each = [(k, j, chip) for k in range(len(ins)) for j, chip in enumerate(chips)]
    half = lambda k: ins[k].at[pl.ds(c * (ins[k].shape[0] // 2), ins[k].shape[0] // 2), :]
    first = [copy(k, j, me, (*chip, c), src=half(k)) for k, j, chip in each]
    arrive = [copy(k, j, (*chip, c), me) for k, j, chip in each]
    passed = [copy(k, 3 + j, (*chip, c), sibling) for k, j, chip in each]
    landed = [copy(k, 3 + j, (*chip, 1 - c), me) for k, j, chip in each]
    return first, arrive, passed, landed


def _gather_start(ins, outs, send_sems, recv_sems):
    for cp in _gather_copies(ins, outs, send_sems, recv_sems)[0]:
        cp.start()


def _gather_finish(ins, outs, send_sems, recv_sems):
    first, arrive, passed, landed = _gather_copies(ins, outs, send_sems, recv_sems)
    for arrival, forward in zip(arrive, passed):
        arrival.wait_recv()
        forward.start()
    for cp in landed:
        cp.wait_recv()
    _wait_all([], first + passed)


def _gather_shards(shards, placed):
    n = len(shards)

    def body(*refs):
        ins, outs = refs[:n], refs[2 * n:3 * n]
        _gather_start(ins, outs, *refs[3 * n:])
        _gather_finish(ins, outs, *refs[3 * n:])

    return pl.pallas_call(
        body, name="gather_shards", in_specs=[ANY] * (2 * n), out_specs=[ANY] * n,
        out_shape=[jax.ShapeDtypeStruct(a.shape, a.dtype) for a in placed],
        input_output_aliases={n + k: k for k in range(n)},
        scratch_shapes=[pltpu.SemaphoreType.DMA((n, 6)), pltpu.SemaphoreType.DMA((n, 6))],
    )(*shards, *placed)


def _exchange8(parts, landing):
    def body(in_ref, _, out_ref, send_sems, recv_sems):
        x, y, c, _ = _place()
        sends = []
        for rel in range(1, 8):
            dx, dy, dc = rel >> 2 & 1, rel >> 1 & 1, rel & 1
            cp = pltpu.make_async_remote_copy(
                src_ref=in_ref.at[2 * (x ^ dx) + (y ^ dy)], dst_ref=out_ref.at[4 * x + 2 * y + c],
                send_sem=send_sems.at[rel - 1], recv_sem=recv_sems.at[rel - 1],
                device_id=(x ^ dx, y ^ dy, c ^ dc), device_id_type=MESH)
            cp.start()
            sends.append(cp)
        for rel in range(1, 8):
            dx, dy, dc = rel >> 2 & 1, rel >> 1 & 1, rel & 1
            pltpu.make_async_remote_copy(
                src_ref=in_ref.at[0], dst_ref=out_ref.at[4 * (x ^ dx) + 2 * (y ^ dy) + (c ^ dc)],
                send_sem=send_sems.at[rel - 1], recv_sem=recv_sems.at[rel - 1],
                device_id=(x, y, c), device_id_type=MESH).wait_recv()
        _wait_all([], sends)

    return pl.pallas_call(
        body, name="exchange8", in_specs=[ANY, ANY], out_specs=ANY, out_shape=jax.ShapeDtypeStruct(landing.shape, landing.dtype),
        input_output_aliases={1: 0}, scratch_shapes=[pltpu.SemaphoreType.DMA((7,)), pltpu.SemaphoreType.DMA((7,))],
    )(parts, landing)


def _rs_sibling(grads):
    n = len(grads)

    def body(*refs):
        ins, outs = refs[:n], refs[n:2 * n]
        send_sems, recv_sems = refs[2 * n:]
        c = lax.axis_index("c")
        sends = []
        for k in range(n):
            for s in range(N_SHARD):
                cp = pltpu.make_async_remote_copy(
                    src_ref=ins[k].at[2 * s + 1 - c], dst_ref=outs[k].at[s], send_sem=send_sems.at[k, s],
                    recv_sem=recv_sems.at[k, s], device_id=_sibling(), device_id_type=MESH)
                cp.start()
                sends.append(cp)
        for cp in sends:
            cp.wait_recv()
        _wait_all([], sends)

    return pl.pallas_call(
        body, name="rs_sibling", in_specs=[ANY] * n, out_specs=[ANY] * n,
        out_shape=[jax.ShapeDtypeStruct((N_SHARD,) + a.shape[1:], a.dtype) for a in grads],
        scratch_shapes=[pltpu.SemaphoreType.DMA((n, N_SHARD)), pltpu.SemaphoreType.DMA((n, N_SHARD))],
    )(*grads)


def _rs_chips_copies(ins, outs, send_sems, recv_sems):
    x, y, c, chips = _place()
    return [pltpu.make_async_remote_copy(
        src_ref=ins[k].at[2 * px + py], dst_ref=outs[k].at[j], send_sem=send_sems.at[k, j], recv_sem=recv_sems.at[k, j],
        device_id=(px, py, c), device_id_type=MESH) for k in range(len(ins)) for j, (px, py) in enumerate(chips)]


def _rs_chips_start(ins, outs, send_sems, recv_sems):
    for cp in _rs_chips_copies(ins, outs, send_sems, recv_sems):
        cp.start()


def _rs_chips_finish(ins, outs, send_sems, recv_sems):
    sends = _rs_chips_copies(ins, outs, send_sems, recv_sems)
    for cp in sends:
        cp.wait_recv()
    _wait_all([], sends)


def _rs_finish(bufs):
    n = len(bufs)

    def body(*refs):
        outs = refs[n:2 * n]
        send_sems, recv_sems = refs[2 * n:]
        c = lax.axis_index("c")
        sends = []
        for k in range(n):
            cp = pltpu.make_async_remote_copy(
                src_ref=outs[k].at[c], dst_ref=outs[k].at[c], send_sem=send_sems.at[k], recv_sem=recv_sems.at[k],
                device_id=_sibling(), device_id_type=MESH)
            cp.start()
            sends.append(cp)
        for k in range(n):
            pltpu.make_async_remote_copy(
                src_ref=outs[k].at[c], dst_ref=outs[k].at[1 - c], send_sem=send_sems.at[k], recv_sem=recv_sems.at[k],
                device_id=_sibling(), device_id_type=MESH).wait_recv()
        _wait_all([], sends)

    return pl.pallas_call(
        body, name="rs_finish", in_specs=[ANY] * n, out_specs=[ANY] * n,
        out_shape=[jax.ShapeDtypeStruct(a.shape, a.dtype) for a in bufs], input_output_aliases={k: k for k in range(n)},
        scratch_shapes=[pltpu.SemaphoreType.DMA((n,)), pltpu.SemaphoreType.DMA((n,))],
    )(*bufs)


def _sum3d(name, terms, scalars, grid_lead, out_lead, out_index, tr=None, out_dtype=F32):
    h, c = terms[0][0].shape[1:]
    tr = tr or h

    def body(s_ref, *refs):
        acc = refs[0][...].astype(F32)
        for ref in refs[1:-1]:
            acc = acc + ref[...].astype(F32)
        refs[-1][...] = acc.astype(refs[-1].dtype)

    in_specs = [pl.BlockSpec((1, tr, c), functools.partial(lambda l, i, s_ref, f: (f(l, s_ref), i, 0), f=f)) for _, f in terms]
    out_spec = pl.BlockSpec((1, tr, c), lambda l, i, s_ref: (out_index(l, s_ref), i, 0))
    return pl.pallas_call(
        body, name=name,
        grid_spec=pltpu.PrefetchScalarGridSpec(num_scalar_prefetch=1, grid=(grid_lead, h // tr), in_specs=in_specs, out_specs=out_spec),
        out_shape=jax.ShapeDtypeStruct((out_lead, h, c), out_dtype), compiler_params=_params(("arbitrary", "arbitrary")),
    )(scalars, *[a for a, _ in terms])


def _rs_stage1(grads, place, tag):
    g8 = [a.reshape(2 * N_SHARD, a.shape[0] // (2 * N_SHARD), a.shape[1]) for a in grads]
    from_sibling = _rs_sibling(g8)
    parts = [_sum3d(f"rs_add1_{tag}{k}", [(g8[k], lambda l, s: 2 * l + s[0]), (from_sibling[k], lambda l, s: l)], place,
                    N_SHARD, N_SHARD, lambda l, s: l, tr=g8[k].shape[1] // 2, out_dtype=BF16)
             for k in range(len(grads))]
    return g8, from_sibling, parts


def _rs_stage3(stage1, from_chips, place, tag):
    g8, from_sibling, _ = stage1
    mine = [(lambda l, s: 2 * s[1] + s[0]), (lambda l, s: s[1])]
    bufs = [_sum3d(f"rs_add2_{tag}{k}", [(g8[k], mine[0]), (from_sibling[k], mine[1])]
                   + [(from_chips[k], functools.partial(lambda l, s, j: j, j=j)) for j in range(3)],
                   place, 1, 2, lambda l, s: s[0], tr=g8[k].shape[1] // 2)
            for k in range(len(g8))]
    whole = _rs_finish(bufs)
    return [w.reshape(2 * w.shape[1], w.shape[2]) for w in whole]


_LATE = ["w_a", "w_b", "w_out", "w_down", "w_up_t"]


def _device_step(x, target, p, late_shards, late_placed, place):
    t = x.shape[0]
    d = D_MODEL
    tok = lambda arr, c=0, w=d: (arr, w, c)
    f32x = lambda n: [(d, F32)] * n
    rp_params = [p["w0"], p["w2p"], p["a0"], p["a2p"], p["g2p"], p["k_k"], p["k_a"]]
    post_params = [p["ln_w"], p["ln_b"], p["r_k"]]
    g = {}

    (xn,) = _tok_fwd("norm1_fwd", _fn_norm, [tok(x)], [p["g1"]], [(d, BF16)])
    z = _mm("in_proj", xn, p["w_in_t"], "nt", tm=t, tn=256)
    q_in, k_in, kd, qe, dec = _tok_fwd("hgates_fwd", _fn_hgates, [tok(z, 0), tok(z, 1)], [p["lb2"]], f32x(5))
    o_intra, u = _hgrn_local_fwd(q_in, k_in, kd, z)
    o_raw, h_states = _hgrn_state_fwd(o_intra, qe, dec, u)
    zl = _lerp_fwd(z, p["mu"])
    lora = tok(zl, 3 * d // LORA, LORA)
    decay, kr2, avec, bvec, gate = _tok_fwd("rprep_fwd", _fn_rprep, [tok(zl, 1), lora], rp_params, f32x(5))
    y, r_states, late = _rwkv_fwd(zl, decay, kr2, avec, bvec, late_shards, late_placed)
    p = dict(p, **dict(zip(_LATE, late)))
    (o_a,) = _tok_fwd("hpost_fwd", _fn_hpost, [tok(o_raw), tok(z, 3)], [p["gnorm"]], [(d, BF16)])
    post_toks = [tok(y), tok(zl, 0), tok(kr2), tok(zl, 2), tok(gate)]
    (o_b,) = _tok_fwd("rpost_fwd", _fn_rpost, post_toks, post_params, [(d, BF16)])
    y_a = _mm("branch_a", o_a, p["w_a"], "nn")
    y_b = _mm("branch_b", o_b, p["w_b"], "nn")
    merge_toks = [tok(z, C_G // 256, 256), tok(z, (C_G + d) // 256, 256), tok(y_a, 0, 256), tok(y_b, 0, 256)]
    (merged,) = _tok_fwd("merge_fwd", _fn_merge, merge_toks, [], [(256, BF16)], col_grid=4, tm=512)
    mix = _mm("out_proj", merged, p["w_out"], "nn")
    h1, xn2 = _tok_fwd("res1_fwd", _fn_res1, [tok(x), tok(mix)], [p["g_post1"], p["g_pre2"]], [(d, F32), (d, BF16)])
    hu = _mm("up_proj", xn2, p["w_up_t"], "nt", tm=t, tn=512)
    act = _conv_fwd(hu, p["conv_w"], p["conv_b"])
    ff = _mm("down_proj", act, p["w_down"], "nn")
    d_h1, d_ff, loss, g["g_post2"] = _loss_head(h1, ff, target, p["g_post2"])

    d_act = _mm("d_act", d_ff, p["w_down"], "nt")
    g["w_down"] = _mm("dw_down", act, d_ff, "tn", tm=256, tn=1024)
    d_hu, d_cw, d_cb = _conv_bwd(hu, p["conv_w"], p["conv_b"], d_act)
    g["conv_w"], g["conv_b"] = d_cw.transpose(1, 0, 2).reshape(3, 2 * D_FF), d_cb.reshape(1, 2 * D_FF)
    d_hu = d_hu.reshape(2 * t, D_FF)
    d_xn2 = _mm("d_xn2", d_hu, p["w_up_t"], "nn", tm=t, tk=D_FF, mk=(t, 2 * D_FF), a_map=lambda i, j, q: (q, 0))
    g["w_up_t"] = _mm("dw_up", d_hu, xn2, "tn", tm=CONV_TILE, tn=1024, mk=(2 * D_FF, t),
                      a_map=lambda i, j, q: (i // N_CONV_TILES, i % N_CONV_TILES))
    d_x_res, d_mix, g["g_post1"], g["g_pre2"] = _tok_bwd(
        "res1_bwd", _fn_res1, [tok(x), tok(mix)], [p["g_post1"], p["g_pre2"]], [[tok(d_h1)], [tok(d_xn2)]],
        [(d, F32), (d, BF16)])
    d_merged = _mm("d_merged", d_mix, p["w_out"], "nt")
    g["w_out"] = _mm("dw_out", merged, d_mix, "tn")
    d_ga, d_gb, d_ya, d_yb = _tok_bwd("merge_bwd", _fn_merge, merge_toks, [], [[tok(d_merged, 0, 256)]],
                                      [(256, BF16)] * 4, col_grid=4, tm=512)
    d_oa = _mm("d_oa", d_ya, p["w_a"], "nt")
    g["w_a"] = _mm("dw_a", o_a, d_ya, "tn")
    d_ob = _mm("d_ob", d_yb, p["w_b"], "nt")
    g["w_b"] = _mm("dw_b", o_b, d_yb, "tn")
    d_oraw, d_hg, g["gnorm"] = _tok_bwd("hpost_bwd", _fn_hpost, [tok(o_raw), tok(z, 3)], [p["gnorm"]], [[tok(d_oa)]],
                                        [(d, F32), (d, BF16)])
    d_y, d_r1, d_kr2_1, d_v1, d_gate, g["ln_w"], g["ln_b"], g["r_k"] = _tok_bwd(
        "rpost_bwd", _fn_rpost, post_toks, post_params, [[tok(d_ob)]], f32x(5))
    stage1 = _rs_stage1([g[n] for n in _LATE], place, "late")
    (d_r2, d_decay, d_kr2_2, d_v2, d_avec, d_bvec), from_chips = _rwkv_bwd(
        zl, decay, kr2, avec, bvec, r_states, d_y, stage1[2])
    g.update(zip(_LATE, _rs_stage3(stage1, from_chips, place, "late")))
    prep = _tok_bwd("rprep_bwd", _fn_rprep, [tok(zl, 1), lora], rp_params,
                    [[tok(d_decay)], [tok(d_kr2_1), tok(d_kr2_2)], [tok(d_avec)], [tok(d_bvec)], [tok(d_gate)]],
                    [(d, F32), (LORA, F32)])
    d_kr, d_lora = prep[:2]
    g["w0"], g["w2p"], g["a0"], g["a2p"], g["g2p"], g["k_k"], g["k_a"] = prep[2:]
    dz_r, g["mu"] = _lerp_bwd(z, p["mu"], (d_r1, d_r2), d_kr, (d_v1, d_v2), d_lora)
    d_qe, d_dec, d_u = _hgrn_state_bwd(d_oraw, qe, dec, h_states)
    d_q_in, d_k_in, d_kd, d_vi = _hgrn_local_bwd(q_in, k_in, kd, z, d_oraw, d_u)
    d_hq, d_hf, g["lb2"] = _tok_bwd("hgates_bwd", _fn_hgates, [tok(z, 0), tok(z, 1)], [p["lb2"]],
                                    [[tok(d_q_in)], [tok(d_k_in)], [tok(d_kd)], [tok(d_qe)], [tok(d_dec)]], [(d, BF16)] * 2)
    dz = jnp.concatenate([d_hq, d_hf, d_vi.astype(BF16), d_hg, dz_r, d_ga, d_gb], axis=1)
    stage1 = _rs_stage1([_mm("dw_in", dz, xn, "tn", tm=256, tn=1024)], place, "w_in")
    d_xn, from_chips = _mm("d_xn", dz, p["w_in_t"], "nn", tm=1024, tn=512, tk=IN_COLS // 2, riders=stage1[2])
    (g["w_in_t"],) = _rs_stage3(stage1, from_chips, place, "w_in")
    grad_x, g["g1"] = _tok_bwd("norm1_bwd", _fn_norm, [tok(x)], [p["g1"]], [[tok(d_xn)]], [(d, F32)],
                               add_to_first=tok(d_x_res))
    return loss, grad_x, g


_WEIGHTS = ["attn_pre_norm", "w_in", "hgrn_lb", "hgrn_gnorm", "w_branch_a", "rwkv_mu", "rwkv_w0", "rwkv_w2", "rwkv_a0",
            "rwkv_a2", "rwkv_g2", "rwkv_k_k", "rwkv_k_a", "rwkv_r_k", "rwkv_ln_w", "rwkv_ln_b", "w_branch_b", "w_out",
            "attn_post_norm", "ffn_pre_norm", "w_up", "conv_w", "conv_b", "w_down", "ffn_post_norm"]
_REPLICATED = [("attn_pre_norm", "g1"), ("hgrn_lb", "lb2"), ("hgrn_gnorm", "gnorm"), ("rwkv_mu", "mu"), ("rwkv_w0", "w0"),
               ("rwkv_a0", "a0"), ("rwkv_k_k", "k_k"), ("rwkv_k_a", "k_a"), ("rwkv_r_k", "r_k"), ("rwkv_ln_w", "ln_w"),
               ("rwkv_ln_b", "ln_b"), ("attn_post_norm", "g_post1"), ("ffn_pre_norm", "g_pre2"), ("conv_b", "conv_b"),
               ("ffn_post_norm", "g_post2")]
SLAB_COLS = 1024


def _pack(arrays):
    pieces, total = [], 0
    for a in arrays:
        flat = a.reshape(-1)
        rows = -(-flat.shape[0] // SLAB_COLS)
        pieces.append(jnp.pad(flat, (0, rows * SLAB_COLS - flat.shape[0])).reshape(rows, SLAB_COLS))
        total += rows
    if total % 8:
        pieces.append(jnp.zeros((8 - total % 8, SLAB_COLS), F32))
    return jnp.concatenate(pieces, axis=0)


def _unpack(slab, shapes):
    out, at = [], 0
    for s in shapes:
        size = 1
        for dim in s:
            size *= dim
        rows = -(-size // SLAB_COLS)
        out.append(slab[at:at + rows].reshape(-1)[:size].reshape(s))
        at += rows
    return out


def kernel(x, attn_pre_norm, w_in, hgrn_lb, hgrn_gnorm, w_branch_a, rwkv_mu, rwkv_w0, rwkv_w2, rwkv_a0, rwkv_a2, rwkv_g2, rwkv_k_k, rwkv_k_a, rwkv_r_k, rwkv_ln_w, rwkv_ln_b, w_branch_b, w_out, attn_post_norm, ffn_pre_norm, w_up, conv_w, conv_b, w_down, ffn_post_norm, loss_target, m_attn_pre_norm, m_w_in, m_hgrn_lb, m_hgrn_gnorm, m_w_branch_a, m_rwkv_mu, m_rwkv_w0, m_rwkv_w2, m_rwkv_a0, m_rwkv_a2, m_rwkv_g2, m_rwkv_k_k, m_rwkv_k_a, m_rwkv_r_k, m_rwkv_ln_w, m_rwkv_ln_b, m_w_branch_b, m_w_out, m_attn_post_norm, m_ffn_pre_norm, m_w_up, m_conv_w, m_conv_b, m_w_down, m_ffn_post_norm, v_attn_pre_norm, v_w_in, v_hgrn_lb, v_hgrn_gnorm, v_w_branch_a, v_rwkv_mu, v_rwkv_w0, v_rwkv_w2, v_rwkv_a0, v_rwkv_a2, v_rwkv_g2, v_rwkv_k_k, v_rwkv_k_a, v_rwkv_r_k, v_rwkv_ln_w, v_rwkv_ln_b, v_w_branch_b, v_w_out, v_attn_post_norm, v_ffn_pre_norm, v_w_up, v_conv_w, v_conv_b, v_w_down, v_ffn_post_norm):
    given = dict(locals())
    w = {n: given[n] for n in _WEIGHTS}
    mom = {n: given["m_" + n] for n in _WEIGHTS}
    var = {n: given["v_" + n] for n in _WEIGHTS}
    shard = 2 * lax.axis_index("x") + lax.axis_index("y")
    place = jnp.stack([lax.axis_index("c"), shard]).astype(jnp.int32)
    row = lambda a: a.reshape(1, -1)
    lora_of = lambda d: jnp.concatenate([d["rwkv_w2"][0], d["rwkv_a2"][0], d["rwkv_g2"][0]], axis=0)

    shards = [w["w_in"][0].T.astype(BF16), lora_of(w), jnp.pad(w["conv_w"][0], ((0, 29), (0, 0)))]
    late_shards = [w["w_branch_a"][0].astype(BF16), w["w_branch_b"][0].astype(BF16), w["w_out"][0].astype(BF16),
                   w["w_down"][0].astype(BF16), w["w_up"][0].T.astype(BF16)]
    placed = [_place_shard(f"place_{k}", a, place) for k, a in enumerate(shards)]
    late_placed = [_place_shard(f"place_late_{k}", a, place) for k, a in enumerate(late_shards)]
    w_in_t, lora_g, conv_g = _gather_shards(shards, placed)
    lora_full = lora_g.reshape(N_SHARD, LORA, 256).transpose(1, 0, 2).reshape(LORA, D_MODEL)
    conv_full = conv_g.reshape(N_SHARD, 32, 2 * D_FF // N_SHARD)[:, :3].transpose(1, 0, 2).reshape(3, 2 * D_FF)
    lrow = lax.broadcasted_iota(jnp.int32, (LORA, 1), 0)
    p = {
        "g1": row(w["attn_pre_norm"]), "lb2": w["hgrn_lb"], "gnorm": row(w["hgrn_gnorm"]), "w_in_t": w_in_t,
        "mu": row(w["rwkv_mu"]), "w0": row(w["rwkv_w0"]), "a0": row(w["rwkv_a0"]),
        "w2p": jnp.where(lrow < 64, lora_full, 0.0), "a2p": jnp.where((lrow >= 64) & (lrow < 128), lora_full, 0.0),
        "g2p": jnp.where(lrow >= 128, lora_full, 0.0),
        "k_k": row(w["rwkv_k_k"]), "k_a": row(w["rwkv_k_a"]), "r_k": row(w["rwkv_r_k"]), "ln_w": row(w["rwkv_ln_w"]),
        "ln_b": row(w["rwkv_ln_b"]), "g_post1": row(w["attn_post_norm"]),
        "g_pre2": row(w["ffn_pre_norm"]), "conv_w": conv_full, "conv_b": row(w["conv_b"]),
        "g_post2": row(w["ffn_post_norm"]),
    }

    loss, grad_x, g = _device_step(x[0], loss_target[0], p, late_shards, late_placed, place)

    g_in_t = g["w_in_t"]
    g_a, g_b, g_o, g_dn, g_up_t = [g[n] for n in _LATE]
    rep_shapes = [w[n].shape for n, _ in _REPLICATED]
    rep = _pack([g[key] for _, key in _REPLICATED])
    n_rep_rows = rep.shape[0]
    cw = 2 * D_FF // N_SHARD
    lora_rows, conv_rows = LORA * 256 // SLAB_COLS, -(-3 * cw // SLAB_COLS)
    lora_g = jnp.concatenate([g["w2p"][0:64], g["a2p"][64:128], g["g2p"][128:256]], axis=0)
    lora_parts = lora_g.reshape(LORA, N_SHARD, 256).transpose(1, 0, 2).reshape(N_SHARD, lora_rows, SLAB_COLS)
    conv_parts = g["conv_w"].reshape(3, N_SHARD, cw).transpose(1, 0, 2).reshape(N_SHARD, 3 * cw)
    conv_parts = jnp.pad(conv_parts, ((0, 0), (0, conv_rows * SLAB_COLS - 3 * cw))).reshape(N_SHARD, conv_rows, SLAB_COLS)
    n_rows = n_rep_rows + lora_rows + conv_rows
    fill = jnp.zeros((N_SHARD, -n_rows % 8, SLAB_COLS), F32)
    parts = jnp.concatenate([jnp.broadcast_to(rep, (N_SHARD,) + rep.shape), lora_parts, conv_parts, fill], axis=1)
    me = 4 * lax.axis_index("x") + 2 * lax.axis_index("y") + lax.axis_index("c")
    landing = lax.dynamic_update_slice(jnp.zeros((8,) + parts.shape[1:], F32),
                                       lax.dynamic_index_in_dim(parts, shard, 0, keepdims=True), (me, 0, 0))
    gathered = _exchange8(parts, landing)
    summed = _sum3d("small_sum", [(gathered, functools.partial(lambda l, s, i: i, i=i)) for i in range(8)], place, 1, 1,
                    lambda l, s: 0)[0]
    lora_grad = summed[n_rep_rows:n_rep_rows + lora_rows].reshape(LORA, 256)
    conv_grad = summed[n_rep_rows + lora_rows:n_rows].reshape(-1)[:3 * cw].reshape(3, cw)

    res = {}

    def put(name, outs, shape=None):
        res[name] = [o.reshape(w[name].shape if shape is None else shape) for o in outs]

    put("w_in", _adamw("adamw_w_in", w["w_in"][0], g_in_t, mom["w_in"][0], var["w_in"][0], 1024, 128, g_transposed=True))
    put("w_up", _adamw("adamw_w_up", w["w_up"][0], g_up_t, mom["w_up"][0], var["w_up"][0], 1024, 128, g_transposed=True))
    for name, grad in (("w_branch_a", g_a), ("w_branch_b", g_b), ("w_out", g_o)):
        put(name, _adamw("adamw_" + name, w[name][0], grad, mom[name][0], var[name][0], 256, 1024))
    put("w_down", _adamw("adamw_w_down", w["w_down"][0], g_dn, mom["w_down"][0], var["w_down"][0], 176, 1024))
    put("conv_w", _adamw("adamw_conv_w", w["conv_w"][0], conv_grad, mom["conv_w"][0], var["conv_w"][0], 3, 2 * D_FF // N_SHARD))
    lora_out = _adamw("adamw_lora", lora_of(w), lora_grad, lora_of(mom), lora_of(var), LORA, 256)
    for name, lo, hi in (("rwkv_w2", 0, 64), ("rwkv_a2", 64, 128), ("rwkv_g2", 128, 256)):
        put(name, [o[lo:hi] for o in lora_out])
    rep_names = [n for n, _ in _REPLICATED]
    rep_out = _adamw("adamw_small", _pack([w[n] for n in rep_names]), summed[:n_rep_rows], _pack([mom[n] for n in rep_names]),
                     _pack([var[n] for n in rep_names]), n_rep_rows, SLAB_COLS)
    for name, parts in zip(rep_names, zip(*[_unpack(o, rep_shapes) for o in rep_out])):
        put(name, list(parts))

    loss = lax.psum(loss[0, 0], ("x", "y", "c"))
    return (loss, grad_x[None], *[res[n][0] for n in _WEIGHTS], *[res[n][1] for n in _WEIGHTS],
            *[res[n][2] for n in _WEIGHTS], *[res[n][3] for n in _WEIGHTS])
```

```python
import functools

import jax
import jax.numpy as jnp
from jax import lax
from jax.experimental import pallas as pl
from jax.experimental.pallas import tpu as pltpu

F32, BF16 = jnp.float32, jnp.bfloat16
MESH = pl.DeviceIdType.MESH

D_MODEL = 1024
HGRN_HEADS = 8
HGRN_K = 128
HGRN_SCALE = HGRN_K ** -0.5
CHUNK = 32
RWKV_HEAD = 64
LORA = 256
D_FF = 2816
EPS = 1e-6
GN_EPS = 1e-5 * RWKV_HEAD
N_SHARD = 4
ADAM_LR, ADAM_B1, ADAM_B2, ADAM_EPS, ADAM_WD, ADAM_STEP = 0.001, 0.9, 0.999, 1e-08, 0.01, 10

LANES = 128
VMEM_LIMIT = 56 * 1024 * 1024
SCAN_TB = 16
SCAN_GROUP = 256
LIGHT_TM = 256

C_HQ, C_HF, C_HI, C_HG = 0, 1024, 2048, 3072
C_R = 4096
R_COLS = 3328
C_G = 7424
IN_COLS = 9472


def _params(sem=None, **kw):
    return pltpu.CompilerParams(dimension_semantics=sem, vmem_limit_bytes=VMEM_LIMIT, **kw)


def _seg_matrix(n, seg):
    r = lax.broadcasted_iota(jnp.int32, (n, n), 0) // seg
    c = lax.broadcasted_iota(jnp.int32, (n, n), 1) // seg
    return (r == c).astype(BF16)


def _split3(x):
    hi = x.astype(BF16)
    r1 = x - hi.astype(F32)
    mid = r1.astype(BF16)
    lo = (r1 - mid.astype(F32)).astype(BF16)
    return hi, mid, lo


def _segsum_impl(x, seg):
    e = _seg_matrix(LANES, seg)
    outs = []
    for g in range(x.shape[1] // LANES):
        hi, mid, lo = _split3(x[:, g * LANES:(g + 1) * LANES])
        outs.append(jnp.dot(hi, e, preferred_element_type=F32) + jnp.dot(mid, e, preferred_element_type=F32)
                    + jnp.dot(lo, e, preferred_element_type=F32))
    return outs[0] if len(outs) == 1 else jnp.concatenate(outs, axis=1)


def _make_segsum(seg):
    @jax.custom_vjp
    def f(x):
        return _segsum_impl(x, seg)

    f.defvjp(lambda x: (_segsum_impl(x, seg), None), lambda _, ct: (_segsum_impl(ct, seg),))
    return f


_segsum64 = _make_segsum(RWKV_HEAD)
_segsum128 = _make_segsum(HGRN_K)


def _chunk_mm_impl(x, kind, transposed):
    n = x.shape[0]
    r = lax.broadcasted_iota(jnp.int32, (n, n), 1 if transposed else 0)
    c = lax.broadcasted_iota(jnp.int32, (n, n), 0 if transposed else 1)
    same = (r // CHUNK) == (c // CHUNK)
    if kind == "cumsum":
        m = same & (r >= c)
    else:
        m = same & (c % CHUNK == (CHUNK // 2 - 1 if kind == "mid" else CHUNK - 1))
    m = m.astype(BF16)
    hi, mid, lo = _split3(x)
    return (jnp.dot(m, hi, preferred_element_type=F32) + jnp.dot(m, mid, preferred_element_type=F32)
            + jnp.dot(m, lo, preferred_element_type=F32))


def _make_chunk_mm(kind):
    @jax.custom_vjp
    def f(x):
        return _chunk_mm_impl(x, kind, False)

    f.defvjp(lambda x: (_chunk_mm_impl(x, kind, False), None), lambda _, ct: (_chunk_mm_impl(ct, kind, True),))
    return f


_chunk_cumsum = _make_chunk_mm("cumsum")
_chunk_mid = _make_chunk_mm("mid")
_chunk_last = _make_chunk_mm("last")


@jax.custom_vjp
def _bdot(x, w):
    return jnp.dot(x.astype(BF16), w.astype(BF16), preferred_element_type=F32)


def _bdot_fwd(x, w):
    return _bdot(x, w), (x, w)


def _bdot_bwd(res, ct):
    x, w = res
    ctb = ct.astype(BF16)
    dx = lax.dot_general(ctb, w.astype(BF16), (((1,), (1,)), ((), ())), preferred_element_type=F32)
    dw = lax.dot_general(x.astype(BF16), ctb, (((0,), (0,)), ((), ())), preferred_element_type=F32)
    return dx, dw


_bdot.defvjp(_bdot_fwd, _bdot_bwd)


def _sigmoid(x):
    return 1.0 / (1.0 + jnp.exp(-x))


def _silu(x):
    return x * _sigmoid(x)


def _softplus(x):
    return jnp.maximum(x, 0.0) + jnp.log(1.0 + jnp.exp(-jnp.abs(x)))


def _rms(x, g):
    return x * lax.rsqrt(jnp.mean(x * x, axis=-1, keepdims=True) + EPS) * g


def _fn_norm(t, p):
    return [_rms(t[0], p[0])]


def _fn_hgates(t, p):
    hq, hf = t
    lb2 = p[0]
    m = jnp.max(lb2, axis=0, keepdims=True)
    e = jnp.exp(lb2 - m)
    first = lax.broadcasted_iota(jnp.int32, e.shape, 0) == 0
    lb = jnp.sum(jnp.where(first, e, 0.0), axis=0, keepdims=True) / jnp.sum(e, axis=0, keepdims=True)
    f = lb + (1.0 - lb) * _sigmoid(hf)
    q, k = _silu(hq) * HGRN_SCALE, 1.0 - f
    b = _chunk_cumsum(jnp.log(f))
    b_ref, b_last = _chunk_mid(b), _chunk_last(b)
    return [q * jnp.exp(b - b_ref), k * jnp.exp(b_ref - b), k * jnp.exp(b_last - b), q * jnp.exp(b), jnp.exp(b_last)]


def _fn_hpost(t, p):
    o, hg = t
    ms = _segsum128(o * o) * (1.0 / HGRN_K)
    return [o * lax.rsqrt(ms + EPS) * p[0] * _silu(hg)]


def _fn_rprep(t, p):
    kr, lora = t
    w0, w2p, a0, a2p, g2p, k_k, k_a = p
    pre_w = w0 + _bdot(jnp.tanh(lora), w2p)
    w_log = -_softplus(-pre_w) - 0.5
    decay = jnp.exp(-jnp.exp(w_log))
    a = _sigmoid(a0 + _bdot(lora, a2p))
    g = _bdot(_sigmoid(lora), g2p)
    kk = kr * k_k
    kk = kk / jnp.maximum(jnp.sqrt(_segsum64(kk * kk)), 1e-12)
    kr2 = kr * (1.0 + (a - 1.0) * k_a)
    return [decay, kr2, -kk, kk * a, g]


def _fn_rpost(t, p):
    y, r, kr2, v, g = t
    ln_w, ln_b, r_k = p
    mu = _segsum64(y) * (1.0 / RWKV_HEAD)
    yc = y - mu
    var = _segsum64(yc * yc) * (1.0 / RWKV_HEAD)
    yn = yc * lax.rsqrt(var + GN_EPS) * ln_w + ln_b
    bonus = _segsum64(r * kr2 * r_k) * v
    return [(yn + bonus) * g]


def _fn_merge(t, p):
    ga, gb, ya, yb = t
    return [_sigmoid(ga) * ya + _sigmoid(gb) * yb]


def _fn_res1(t, p):
    x, mix = t
    h1 = x + _rms(mix, p[0])
    return [h1, _rms(h1, p[1])]


def _tok_call(name, fn, toks, params, outs, red_shapes=(), tm=128, col_grid=1):
    n_t, n_p, n_o = len(toks), len(params), len(outs)
    t_len = toks[0][0].shape[0]
    tm = min(tm, t_len)

    def body(*refs):
        tv = [r[...].astype(F32) for r in refs[:n_t]]
        pv = [r[...] for r in refs[n_t:n_t + n_p]]
        o, red = fn(tv, pv)
        for ref, val in zip(refs[n_t + n_p:n_t + n_p + n_o], o):
            ref[...] = val.astype(ref.dtype)
        red_refs = refs[n_t + n_p + n_o:]
        if red_refs:
            first = pl.program_id(0) == 0

            @pl.when(first)
            def _():
                for ref, val in zip(red_refs, red):
                    ref[...] = val

            @pl.when(jnp.logical_not(first))
            def _():
                for ref, val in zip(red_refs, red):
                    ref[...] += val

    in_specs = [pl.BlockSpec((tm, w), functools.partial(lambda i, j, c: (i, c + j), c=c)) for (_, w, c) in toks]
    in_specs += [pl.BlockSpec(p.shape, lambda i, j: (0, 0)) for p in params]
    out_specs = [pl.BlockSpec((tm, w), lambda i, j: (i, j)) for (w, _) in outs]
    out_specs += [pl.BlockSpec(s, lambda i, j: (0, 0)) for s in red_shapes]
    out_shape = [jax.ShapeDtypeStruct((t_len, w * col_grid), dt) for (w, dt) in outs]
    out_shape += [jax.ShapeDtypeStruct(s, F32) for s in red_shapes]
    return pl.pallas_call(
        body, name=name, grid=(t_len // tm, col_grid), in_specs=in_specs, out_specs=out_specs, out_shape=out_shape,
        compiler_params=_params(("arbitrary", "arbitrary")),
    )(*[a for (a, _, _) in toks], *params)


def _tok_fwd(name, fn, toks, params, outs, **kw):
    return _tok_call(name, lambda tv, pv: (fn(tv, pv), []), toks, params, outs, **kw)


def _tok_bwd(name, fn, toks, params, cts, want, add_to_first=None, **kw):
    n_t = len(toks)
    flat = [c for group in cts for c in group]
    extra = [] if add_to_first is None else [add_to_first]

    def bwd(tv, pv):
        prim, rest = tv[:n_t], tv[n_t:]
        ct, at = [], 0
        for group in cts:
            ct.append(functools.reduce(lambda u, v: u + v, rest[at:at + len(group)]))
            at += len(group)
        _, vjp = jax.vjp(lambda *a: tuple(fn(list(a[:n_t]), list(a[n_t:]))), *prim, *pv)
        g = vjp(tuple(ct))
        tok_grads = [g[i] for i in range(n_t) if want[i] is not None]
        if extra:
            tok_grads[0] = tok_grads[0] + rest[at]
        return tok_grads, list(g[n_t:])

    return _tok_call(name, bwd, list(toks) + flat + extra, params, [w for w in want if w is not None],
                     red_shapes=[p.shape for p in params], **kw)


def _mm(name, a, b, mode, out_dtype=F32, tm=None, tn=None, tk=None, riders=None, a_map=None, mk=None):
    if mode == "nn":
        (m, k), (_, n) = a.shape, b.shape
    elif mode == "nt":
        (m, k), (n, _) = a.shape, b.shape
    else:
        (k, m), (_, n) = a.shape, b.shape
    if mk is not None:
        m, k = mk
    tm = (512 if mode == "tn" else 2048) if tm is None else tm
    tn = (512 if mode == "tn" else 256) if tn is None else tn
    tk = k if tk is None else tk
    tm, tn = min(tm, m), min(tn, n)
    nk = k // tk
    assert m % tm == 0 and n % tn == 0 and k % tk == 0, (name, a.shape, b.shape, tm, tn, tk)
    a_spec = pl.BlockSpec((tk, tm), lambda i, j, q: (q, i)) if mode == "tn" else pl.BlockSpec((tm, tk), lambda i, j, q: (i, q))
    if a_map is not None:
        a_spec = pl.BlockSpec(a_spec.block_shape, a_map)
    b_spec = pl.BlockSpec((tn, tk), lambda i, j, q: (j, q)) if mode == "nt" else pl.BlockSpec((tk, tn), lambda i, j, q: (q, j))
    dn = {"nn": (((1,), (0,)), ((), ())), "nt": (((1,), (1,)), ((), ())), "tn": (((0,), (0,)), ((), ()))}[mode]
    grid = (m // tm, n // tn, nk)
    nr = 0 if riders is None else len(riders)

    def body(*refs):
        a_ref, b_ref, o_ref = refs[0], refs[1], refs[2 + nr]
        acc = refs[3 + 2 * nr] if nk > 1 else None
        if nr:
            exchange = (refs[2:2 + nr], refs[3 + nr:3 + 2 * nr], *refs[-2:])
            at = [pl.program_id(ax) for ax in range(3)]

            @pl.when((at[0] == 0) & (at[1] == 0) & (at[2] == 0))
            def _():
                _rs_chips_start(*exchange)

        p = lax.dot_general(a_ref[...], b_ref[...], dn, preferred_element_type=F32)
        if nk == 1:
            o_ref[...] = p.astype(o_ref.dtype)
        else:
            q = pl.program_id(2)

            @pl.when(q == 0)
            def _():
                acc[...] = p

            @pl.when(q > 0)
            def _():
                acc[...] += p

            @pl.when(q == nk - 1)
            def _():
                o_ref[...] = acc[...].astype(o_ref.dtype)

        if nr:
            @pl.when((at[0] == grid[0] - 1) & (at[1] == grid[1] - 1) & (at[2] == grid[2] - 1))
            def _():
                _rs_chips_finish(*exchange)

    scratch = [pltpu.VMEM((tm, tn), F32)] if nk > 1 else []
    out_specs = [pl.BlockSpec((tm, tn), lambda i, j, q: (i, j))]
    out_shape = [jax.ShapeDtypeStruct((m, n), out_dtype)]
    if nr:
        scratch += [pltpu.SemaphoreType.DMA((nr, 3)), pltpu.SemaphoreType.DMA((nr, 3))]
        out_specs += [ANY] * nr
        out_shape += [jax.ShapeDtypeStruct((3,) + r.shape[1:], r.dtype) for r in riders]
    outs = pl.pallas_call(
        body, name=name, grid=grid, in_specs=[a_spec, b_spec] + [ANY] * nr, out_specs=out_specs, out_shape=out_shape,
        scratch_shapes=scratch,
        compiler_params=_params(("arbitrary",) * 3 if nr else ("parallel", "parallel", "arbitrary")),
    )(a, b, *(riders or []))
    return (outs[0], outs[1:]) if nr else outs[0]


def _shift_down(z, n):
    rows = lax.broadcasted_iota(jnp.int32, z.shape, 0)
    return jnp.where(rows < n, 0.0, pltpu.roll(z, n, 0))


def _shift_up(z, n):
    t = z.shape[0]
    rows = lax.broadcasted_iota(jnp.int32, z.shape, 0)
    return jnp.where(rows >= t - n, 0.0, pltpu.roll(z, t - n, 0))


def _lerp_fwd(z, mu):
    t = z.shape[0]
    w = 256

    def body(z_ref, mu_ref, o_ref):
        zz = z_ref[...]
        o_ref[...] = zz + mu_ref[...] * (_shift_down(zz, 1) - zz)

    return pl.pallas_call(
        body, name="lerp_fwd", grid=(R_COLS // w,),
        in_specs=[pl.BlockSpec((t, w), lambda j: (0, C_R // w + j)), pl.BlockSpec((1, w), lambda j: (0, j))],
        out_specs=pl.BlockSpec((t, w), lambda j: (0, j)), out_shape=jax.ShapeDtypeStruct((t, R_COLS), F32),
        compiler_params=_params(("parallel",)),
    )(z, mu)


def _lerp_bwd(z, mu, d_r, d_k, d_v, d_lora):
    t = z.shape[0]
    w = 256
    per = D_MODEL // w

    def body(z_ref, mu_ref, r1_ref, r2_ref, k_ref, v1_ref, v2_ref, l_ref, dz_ref, dmu_ref):
        j = pl.program_id(0)
        zz, m = z_ref[...], mu_ref[...]
        d = jnp.where(j < per, r1_ref[...] + r2_ref[...],
                      jnp.where(j < 2 * per, k_ref[...], jnp.where(j < 3 * per, v1_ref[...] + v2_ref[...], l_ref[...])))
        dz_ref[...] = (d * (1.0 - m) + _shift_up(d * m, 1)).astype(dz_ref.dtype)
        dmu_ref[...] = jnp.sum(d * (_shift_down(zz, 1) - zz), axis=0, keepdims=True)

    piece = lambda first: pl.BlockSpec((t, w), lambda j: (0, jnp.clip(j - first, 0, per - 1)))
    return pl.pallas_call(
        body, name="lerp_bwd", grid=(R_COLS // w,),
        in_specs=[pl.BlockSpec((t, w), lambda j: (0, C_R // w + j)), pl.BlockSpec((1, w), lambda j: (0, j)),
                  piece(0), piece(0), piece(per), piece(2 * per), piece(2 * per), pl.BlockSpec((t, w), lambda j: (0, 0))],
        out_specs=[pl.BlockSpec((t, w), lambda j: (0, j)), pl.BlockSpec((1, w), lambda j: (0, j))],
        out_shape=[jax.ShapeDtypeStruct((t, R_COLS), BF16), jax.ShapeDtypeStruct((1, R_COLS), F32)],
        compiler_params=_params(("arbitrary",)),
    )(z, mu, *d_r, d_k, *d_v, d_lora)


CONV_TILE = 256
N_CONV_TILES = D_FF // CONV_TILE


def _conv(h, w, b):
    return b + w[0:1, :] * _shift_down(h, 2) + w[1:2, :] * _shift_down(h, 1) + w[2:3, :] * h


def _conv_fwd(hu, conv_w, conv_b):
    t = hu.shape[0]
    n = N_CONV_TILES

    def body(hg_ref, hv_ref, wg_ref, wv_ref, bg_ref, bv_ref, o_ref):
        gate = _conv(hg_ref[...], wg_ref[...], bg_ref[...])
        val = _conv(hv_ref[...], wv_ref[...], bv_ref[...])
        o_ref[...] = (_silu(gate) * val).astype(o_ref.dtype)

    col = lambda off: pl.BlockSpec((t, CONV_TILE), lambda j: (0, j + off))
    wspec = lambda off: pl.BlockSpec((3, CONV_TILE), lambda j: (0, j + off))
    bspec = lambda off: pl.BlockSpec((1, CONV_TILE), lambda j: (0, j + off))
    return pl.pallas_call(
        body, name="conv_fwd", grid=(n,),
        in_specs=[col(0), col(n), wspec(0), wspec(n), bspec(0), bspec(n)],
        out_specs=pl.BlockSpec((t, CONV_TILE), lambda j: (0, j)), out_shape=jax.ShapeDtypeStruct((t, D_FF), BF16),
        compiler_params=_params(("parallel",)),
    )(hu, hu, conv_w, conv_w, conv_b, conv_b)


def _conv_bwd(hu, conv_w, conv_b, d_act):
    t = hu.shape[0]
    n = N_CONV_TILES

    def body(hg_ref, hv_ref, wg_ref, wv_ref, bg_ref, bv_ref, d_ref, dh_ref, dw_ref, db_ref):
        hg, hv, wg, wv = hg_ref[...], hv_ref[...], wg_ref[...], wv_ref[...]
        gate = _conv(hg, wg, bg_ref[...])
        val = _conv(hv, wv, bv_ref[...])
        d = d_ref[...]
        sg = _sigmoid(gate)
        d_gate = d * val * (sg * (1.0 + gate * (1.0 - sg)))
        d_val = d * (gate * sg)
        for half, (dc, h, w) in enumerate(((d_gate, hg, wg), (d_val, hv, wv))):
            dh = w[2:3, :] * dc + w[1:2, :] * _shift_up(dc, 1) + w[0:1, :] * _shift_up(dc, 2)
            dh_ref[half] = dh.astype(dh_ref.dtype)
            dw_ref[half, 0:1, :] = jnp.sum(dc * _shift_down(h, 2), axis=0, keepdims=True)
            dw_ref[half, 1:2, :] = jnp.sum(dc * _shift_down(h, 1), axis=0, keepdims=True)
            dw_ref[half, 2:3, :] = jnp.sum(dc * h, axis=0, keepdims=True)
            db_ref[half] = jnp.sum(dc, axis=0, keepdims=True)

    gcol = lambda rows: pl.BlockSpec((rows, CONV_TILE), lambda j: (0, j))
    vcol = lambda rows: pl.BlockSpec((rows, CONV_TILE), lambda j: (0, j + n))
    both = lambda rows: pl.BlockSpec((2, rows, CONV_TILE), lambda j: (0, 0, j))
    return pl.pallas_call(
        body, name="conv_bwd", grid=(n,),
        in_specs=[gcol(t), vcol(t), gcol(3), vcol(3), gcol(1), vcol(1), gcol(t)],
        out_specs=[both(t), both(3), both(1)],
        out_shape=[jax.ShapeDtypeStruct((2, t, D_FF), BF16), jax.ShapeDtypeStruct((2, 3, D_FF), F32),
                   jax.ShapeDtypeStruct((2, 1, D_FF), F32)],
        compiler_params=_params(("parallel",)),
    )(hu, hu, conv_w, conv_w, conv_b, conv_b, d_act)


_NN = (((1,), (0,)), ((), ()))
_NT = (((1,), (1,)), ((), ()))
_TN = (((0,), (0,)), ((), ()))
HGRN_CB = 8
HGRN_LOCAL_CB = 16


def _bf_dot(a, b, dn):
    return lax.dot_general(a.astype(BF16), b.astype(BF16), dn, preferred_element_type=F32)


def _tril():
    n = HGRN_LOCAL_CB * CHUNK
    r = lax.broadcasted_iota(jnp.int32, (n, n), 0)
    c = lax.broadcasted_iota(jnp.int32, (n, n), 1)
    return (r // CHUNK == c // CHUNK) & (r >= c)


def _hgrn_specs(t):
    rows = HGRN_LOCAL_CB * CHUNK
    head = pl.BlockSpec((rows, HGRN_K), lambda h, n: (n, h))
    v_head = pl.BlockSpec((rows, HGRN_K), lambda h, n: (n, C_HI // HGRN_K + h))
    mats = pl.BlockSpec((1, HGRN_LOCAL_CB, HGRN_K, HGRN_K), lambda h, n: (h, n, 0, 0))
    return head, v_head, mats, (HGRN_HEADS, t // rows)


def _hgrn_local_fwd(q_in, k_in, kd, z):
    t = q_in.shape[0]
    head, v_head, mats, grid = _hgrn_specs(t)

    def body(q_ref, k_ref, kd_ref, v_ref, o_ref, u_ref):
        v = v_ref[...]
        scores = jnp.where(_tril(), _bf_dot(q_ref[...], k_ref[...], _NT), 0.0)
        o_ref[...] = _bf_dot(scores, v, _NN)
        for n in range(HGRN_LOCAL_CB):
            rows = slice(n * CHUNK, (n + 1) * CHUNK)
            u_ref[0, n] = _bf_dot(v[rows], kd_ref[rows, :], _TN)

    return pl.pallas_call(
        body, name="hgrn_local_fwd", grid=grid, in_specs=[head, head, head, v_head], out_specs=[head, mats],
        out_shape=[jax.ShapeDtypeStruct((t, D_MODEL), F32),
                   jax.ShapeDtypeStruct((HGRN_HEADS, t // CHUNK, HGRN_K, HGRN_K), F32)],
        compiler_params=_params(("parallel", "parallel")),
    )(q_in, k_in, kd, z)


def _hgrn_state_specs(t, reverse=False):
    rows = HGRN_CB * CHUNK
    nb = t // rows
    at = (lambda n: nb - 1 - n) if reverse else (lambda n: n)
    tok = pl.BlockSpec((rows, D_MODEL), lambda n: (at(n), 0))
    mats = pl.BlockSpec((HGRN_HEADS, HGRN_CB, HGRN_K, HGRN_K), lambda n: (0, at(n), 0, 0))
    return tok, mats, nb


def _hgrn_state_fwd(o_intra, qe, dec, u):
    t = qe.shape[0]
    tok, mats, nb = _hgrn_state_specs(t)

    def body(oi_ref, qe_ref, dec_ref, u_ref, o_ref, st_ref, s_ref):
        @pl.when(pl.program_id(0) == 0)
        def _():
            s_ref[...] = jnp.zeros_like(s_ref)

        st = [s_ref[h] for h in range(HGRN_HEADS)]
        for n in range(HGRN_CB):
            rows = slice(n * CHUNK, (n + 1) * CHUNK)
            for h in range(HGRN_HEADS):
                cols = slice(h * HGRN_K, (h + 1) * HGRN_K)
                st_ref[h, n] = st[h]
                o_ref[rows, cols] = oi_ref[rows, cols] + _bf_dot(qe_ref[rows, cols], st[h], _NT)
                st[h] = st[h] * dec_ref[n * CHUNK:n * CHUNK + 1, cols] + u_ref[h, n]
        for h in range(HGRN_HEADS):
            s_ref[h] = st[h]

    return pl.pallas_call(
        body, name="hgrn_state_fwd", grid=(nb,), in_specs=[tok, tok, tok, mats], out_specs=[tok, mats],
        out_shape=[jax.ShapeDtypeStruct((t, D_MODEL), F32),
                   jax.ShapeDtypeStruct((HGRN_HEADS, t // CHUNK, HGRN_K, HGRN_K), F32)],
        scratch_shapes=[pltpu.VMEM((HGRN_HEADS, HGRN_K, HGRN_K), F32)],
        compiler_params=_params(("arbitrary",)),
    )(o_intra, qe, dec, u)


def _hgrn_state_bwd(d_o, qe, dec, states):
    t = qe.shape[0]
    tok_r, mats_r, nb = _hgrn_state_specs(t, reverse=True)

    def body(do_ref, qe_ref, dec_ref, st_ref, dqe_ref, ddec_ref, du_ref, d_ref):
        @pl.when(pl.program_id(0) == 0)
        def _():
            d_ref[...] = jnp.zeros_like(d_ref)

        first_row = lax.broadcasted_iota(jnp.int32, (CHUNK, HGRN_K), 0) == 0
        d = [d_ref[h] for h in range(HGRN_HEADS)]
        for n in reversed(range(HGRN_CB)):
            rows = slice(n * CHUNK, (n + 1) * CHUNK)
            for h in range(HGRN_HEADS):
                cols = slice(h * HGRN_K, (h + 1) * HGRN_K)
                st, do = st_ref[h, n], do_ref[rows, cols]
                du_ref[h, n] = d[h]
                ddec_ref[rows, cols] = jnp.where(first_row, jnp.sum(d[h] * st, axis=0, keepdims=True), 0.0)
                dqe_ref[rows, cols] = _bf_dot(do, st, _NN)
                d[h] = d[h] * dec_ref[n * CHUNK:n * CHUNK + 1, cols] + _bf_dot(do, qe_ref[rows, cols], _TN)
        for h in range(HGRN_HEADS):
            d_ref[h] = d[h]

    out = jax.ShapeDtypeStruct((t, D_MODEL), F32)
    return pl.pallas_call(
        body, name="hgrn_state_bwd", grid=(nb,), in_specs=[tok_r, tok_r, tok_r, mats_r], out_specs=[tok_r, tok_r, mats_r],
        out_shape=[out, out, jax.ShapeDtypeStruct((HGRN_HEADS, t // CHUNK, HGRN_K, HGRN_K), F32)],
        scratch_shapes=[pltpu.VMEM((HGRN_HEADS, HGRN_K, HGRN_K), F32)],
        compiler_params=_params(("arbitrary",)),
    )(d_o, qe, dec, states)


def _hgrn_local_bwd(q_in, k_in, kd, z, d_o, d_u):
    t = q_in.shape[0]
    head, v_head, mats, grid = _hgrn_specs(t)

    def body(q_ref, k_ref, kd_ref, v_ref, do_ref, du_ref, dq_ref, dk_ref, dkd_ref, dv_ref):
        tril = _tril()
        q, k, v, do = q_ref[...], k_ref[...], v_ref[...], do_ref[...]
        scores = jnp.where(tril, _bf_dot(q, k, _NT), 0.0)
        d_scores = jnp.where(tril, _bf_dot(do, v, _NT), 0.0)
        dq_ref[...] = _bf_dot(d_scores, k, _NN)
        dk_ref[...] = _bf_dot(d_scores, q, _TN)
        dv = _bf_dot(scores, do, _TN)
        for n in range(HGRN_LOCAL_CB):
            rows = slice(n * CHUNK, (n + 1) * CHUNK)
            du = du_ref[0, n]
            dv_ref[rows, :] = dv[rows] + _bf_dot(kd_ref[rows, :], du, _NT)
            dkd_ref[rows, :] = _bf_dot(v[rows], du, _NN)

    out = jax.ShapeDtypeStruct((t, D_MODEL), F32)
    return pl.pallas_call(
        body, name="hgrn_local_bwd", grid=grid, in_specs=[head, head, head, v_head, head, mats], out_specs=[head] * 4,
        out_shape=[out] * 4, compiler_params=_params(("parallel", "parallel")),
    )(q_in, k_in, kd, z, d_o, d_u)


def _split2(x):
    hi = x.astype(BF16)
    return hi, (x - hi.astype(F32)).astype(BF16)


def _seg_bcast(xs, e, one_term=False):
    n = RWKV_HEAD
    if one_term:
        out = jnp.dot(jnp.concatenate([x.astype(BF16) for x in xs], axis=0), e, preferred_element_type=F32)
        return [out[i * n:(i + 1) * n] for i in range(len(xs))]
    parts = [t for x in xs for t in _split2(x)]
    out = jnp.dot(jnp.concatenate(parts, axis=0), e, preferred_element_type=F32)
    return [out[2 * i * n:(2 * i + 1) * n] + out[(2 * i + 1) * n:(2 * i + 2) * n] for i in range(len(xs))]


def _rows_to_cols(rows, diag, e):
    zero = jnp.zeros((), BF16)
    parts = [jnp.where(diag, row.astype(BF16), zero) for row in rows]
    out = jnp.dot(jnp.concatenate(parts, axis=0), e, preferred_element_type=F32)
    n = RWKV_HEAD
    return [out[i * n:(i + 1) * n] for i in range(len(rows))]


def _col_to_row(col, diag):
    return jnp.sum(jnp.where(diag, col, 0.0), axis=0, keepdims=True)


_SCAN_PAIRS = ((0, 1), (2, 3))


def _scan_consts():
    e = _seg_matrix(SCAN_GROUP, RWKV_HEAD)
    i = lax.broadcasted_iota(jnp.int32, (RWKV_HEAD, SCAN_GROUP), 0)
    l = lax.broadcasted_iota(jnp.int32, (RWKV_HEAD, SCAN_GROUP), 1)
    groups = [slice(g * SCAN_GROUP, (g + 1) * SCAN_GROUP) for g in range(D_MODEL // SCAN_GROUP)]
    return e, (l % RWKV_HEAD) == i, groups


def _rwkv_fwd(zl, w, k, a, b, shards, placed):
    t = zl.shape[0]
    nb = t // SCAN_TB
    n = len(shards)
    steps = range(SCAN_TB)

    def body(*refs):
        scan(*refs[:6], *refs[6 + 2 * n:8 + 2 * n], refs[8 + 3 * n])
        gather = (refs[6:6 + n], refs[8 + 2 * n:8 + 3 * n], *refs[9 + 3 * n:])

        @pl.when(pl.program_id(0) == 0)
        def _():
            _gather_start(*gather)

        @pl.when(pl.program_id(0) == nb - 1)
        def _():
            _gather_finish(*gather)

    def scan(r_ref, w_ref, k_ref, v_ref, a_ref, b_ref, y_ref, st_ref, s_ref):
        @pl.when(pl.program_id(0) == 0)
        def _():
            s_ref[...] = jnp.zeros_like(s_ref)

        e, diag, groups = _scan_consts()
        v_cols = [_rows_to_cols([v_ref[i:i + 1, sl] for i in steps], diag, e) for sl in groups]
        s = [s_ref[:, sl] for sl in groups]
        for i in steps:
            for pair in _SCAN_PAIRS:
                sas = _seg_bcast([s[g] * a_ref[i:i + 1, groups[g]] for g in pair], e)
                for g, sa in zip(pair, sas):
                    sl = groups[g]
                    s[g] = s[g] * w_ref[i:i + 1, sl] + sa * b_ref[i:i + 1, sl] + v_cols[g][i] * k_ref[i:i + 1, sl]
                    st_ref[i, :, sl] = s[g]
        for g, sl in enumerate(groups):
            s_ref[:, sl] = s[g]
            y_cols = _seg_bcast([st_ref[i, :, sl] * r_ref[i:i + 1, sl] for i in steps], e, one_term=True)
            for i in steps:
                y_ref[i:i + 1, sl] = _col_to_row(y_cols[i], diag)

    blk = pl.BlockSpec((SCAN_TB, D_MODEL), lambda n: (n, 0))
    v_blk = pl.BlockSpec((SCAN_TB, D_MODEL), lambda n: (n, 2))
    outs = pl.pallas_call(
        body, name="rwkv_fwd", grid=(nb,), in_specs=[blk, blk, blk, v_blk, blk, blk] + [ANY] * (2 * n),
        out_specs=[blk, pl.BlockSpec((SCAN_TB, RWKV_HEAD, D_MODEL), lambda i: (i, 0, 0))] + [ANY] * n,
        out_shape=[jax.ShapeDtypeStruct((t, D_MODEL), F32), jax.ShapeDtypeStruct((t, RWKV_HEAD, D_MODEL), F32)]
        + [jax.ShapeDtypeStruct(p.shape, p.dtype) for p in placed],
        input_output_aliases={6 + n + i: 2 + i for i in range(n)},
        scratch_shapes=[pltpu.VMEM((RWKV_HEAD, D_MODEL), F32), pltpu.SemaphoreType.DMA((n, 6)), pltpu.SemaphoreType.DMA((n, 6))],
        compiler_params=_params(("arbitrary",)),
    )(zl, w, k, zl, a, b, *shards, *placed)
    return outs[0], outs[1], outs[2:]


def _rwkv_bwd(zl, w, k, a, b, states, d_y, parts):
    t = zl.shape[0]
    nb = t // SCAN_TB
    n = len(parts)
    steps = range(SCAN_TB)

    def body(*refs):
        scan(*refs[:9], *refs[9 + n:15 + n], refs[15 + 2 * n])
        exchange = (refs[9:9 + n], refs[15 + n:15 + 2 * n], *refs[16 + 2 * n:])

        @pl.when(pl.program_id(0) == 0)
        def _():
            _rs_chips_start(*exchange)

        @pl.when(pl.program_id(0) == nb - 1)
        def _():
            _rs_chips_finish(*exchange)

    def scan(r_ref, w_ref, k_ref, v_ref, a_ref, b_ref, st_ref, prev_ref, dy_ref,
             dr_ref, dw_ref, dk_ref, dv_ref, da_ref, db_ref, ds_ref):
        @pl.when(pl.program_id(0) == 0)
        def _():
            ds_ref[...] = jnp.zeros_like(ds_ref)

        has_prev = (pl.program_id(0) < nb - 1).astype(F32)
        e, diag, groups = _scan_consts()
        colsum = lambda x: jnp.sum(x, axis=0, keepdims=True)

        def s_prev(i, sl):
            return st_ref[i - 1, :, sl] if i > 0 else prev_ref[0, :, sl] * has_prev

        dy_cols = [_rows_to_cols([dy_ref[i:i + 1, sl] for i in steps], diag, e) for sl in groups]
        v_cols = [_rows_to_cols([v_ref[i:i + 1, sl] for i in steps], diag, e) for sl in groups]
        sa_cols = [_seg_bcast([s_prev(i, sl) * a_ref[i:i + 1, sl] for i in steps], e, one_term=True) for sl in groups]
        ds = [ds_ref[:, sl] for sl in groups]
        dsk = [[None] * SCAN_TB for _ in groups]
        for i in reversed(steps):
            for pair in _SCAN_PAIRS:
                d = {}
                for g in pair:
                    sl = groups[g]
                    d[g] = ds[g] + dy_cols[g][i] * r_ref[i:i + 1, sl]
                    dr_ref[i:i + 1, sl] = colsum(st_ref[i, :, sl] * dy_cols[g][i])
                    dw_ref[i:i + 1, sl] = colsum(d[g] * s_prev(i, sl))
                    db_ref[i:i + 1, sl] = colsum(d[g] * sa_cols[g][i])
                    dk_ref[i:i + 1, sl] = colsum(d[g] * v_cols[g][i])
                    dsk[g][i] = d[g] * k_ref[i:i + 1, sl]
                dsas = _seg_bcast([d[g] * b_ref[i:i + 1, groups[g]] for g in pair], e)
                for g, dsa in zip(pair, dsas):
                    sl = groups[g]
                    da_ref[i:i + 1, sl] = colsum(s_prev(i, sl) * dsa)
                    ds[g] = d[g] * w_ref[i:i + 1, sl] + dsa * a_ref[i:i + 1, sl]
        for g, sl in enumerate(groups):
            ds_ref[:, sl] = ds[g]
            dv_cols = _seg_bcast(dsk[g], e, one_term=True)
            for i in steps:
                dv_ref[i:i + 1, sl] = _col_to_row(dv_cols[i], diag)

    blk = pl.BlockSpec((SCAN_TB, D_MODEL), lambda n: (nb - 1 - n, 0))
    v_blk = pl.BlockSpec((SCAN_TB, D_MODEL), lambda n: (nb - 1 - n, 2))
    out = jax.ShapeDtypeStruct((t, D_MODEL), F32)
    outs = pl.pallas_call(
        body, name="rwkv_bwd", grid=(nb,),
        in_specs=[blk, blk, blk, v_blk, blk, blk] + [
            pl.BlockSpec((SCAN_TB, RWKV_HEAD, D_MODEL), lambda i: (nb - 1 - i, 0, 0)),
            pl.BlockSpec((1, RWKV_HEAD, D_MODEL), lambda i: (jnp.maximum((nb - 1 - i) * SCAN_TB - 1, 0), 0, 0)),
            blk] + [ANY] * n,
        out_specs=[blk] * 6 + [ANY] * n,
        out_shape=[out] * 6 + [jax.ShapeDtypeStruct((3,) + p.shape[1:], p.dtype) for p in parts],
        scratch_shapes=[pltpu.VMEM((RWKV_HEAD, D_MODEL), F32), pltpu.SemaphoreType.DMA((n, 3)), pltpu.SemaphoreType.DMA((n, 3))],
        compiler_params=_params(("arbitrary",)),
    )(zl, w, k, zl, a, b, states, states, d_y, *parts)
    return outs[:6], outs[6:]


def _loss_head(h1, ff, target, g_post):
    def fn(tv, pv):
        a, f, tgt = tv
        h2, vjp = jax.vjp(lambda a_, f_, g_: a_ + _rms(f_, g_), a, f, pv[0])
        err = h2 - tgt
        loss = 0.5 * jnp.sum(jnp.mean(err * err, axis=-1, keepdims=True), axis=0, keepdims=True)
        d_a, d_f, d_g = vjp(err * (1.0 / D_MODEL))
        return [d_a, d_f], [loss, d_g]

    return _tok_call("loss_head", fn, [(h1, D_MODEL, 0), (ff, D_MODEL, 0), (target, D_MODEL, 0)], [g_post],
                     [(D_MODEL, F32), (D_MODEL, BF16)], red_shapes=[(1, 1), (1, D_MODEL)], tm=LIGHT_TM)


def _sum_call(name, terms, rows_per_block=None):
    a0, i0 = terms[0]
    r, c = a0.shape[-2:]
    tr = rows_per_block or r

    def body(*refs):
        acc = refs[0][...].reshape(tr, c)
        for ref in refs[1:-1]:
            acc = acc + ref[...].reshape(tr, c)
        refs[-1][...] = acc

    def spec(arr, idx):
        if arr.ndim == 2:
            return pl.BlockSpec((tr, c), lambda i: (i, 0))
        return pl.BlockSpec((1, tr, c), functools.partial(lambda i, idx: (idx, i, 0), idx=idx))

    return pl.pallas_call(
        body, name=name, grid=(r // tr,), in_specs=[spec(a, i) for a, i in terms],
        out_specs=pl.BlockSpec((tr, c), lambda i: (i, 0)), out_shape=jax.ShapeDtypeStruct((r, c), F32),
        compiler_params=_params(("parallel",)),
    )(*[a for a, _ in terms])


def _adamw_math(w, g, m, v):
    m2 = ADAM_B1 * m + (1.0 - ADAM_B1) * g
    v2 = ADAM_B2 * v + (1.0 - ADAM_B2) * (g * g)
    m_hat = m2 / (1.0 - ADAM_B1 ** ADAM_STEP)
    v_hat = v2 / (1.0 - ADAM_B2 ** ADAM_STEP)
    return -ADAM_LR * (m_hat / (jnp.sqrt(v_hat) + ADAM_EPS) + ADAM_WD * w), m2, v2


def _adamw(name, w, g, m, v, bm, bn, g_transposed=False):
    r, c = w.shape

    def body(w_ref, g_ref, m_ref, v_ref, go_ref, d_ref, mo_ref, vo_ref):
        g = g_ref[...].T if g_transposed else g_ref[...]
        d, m2, v2 = _adamw_math(w_ref[...], g, m_ref[...], v_ref[...])
        go_ref[...] = g
        d_ref[...] = d
        mo_ref[...] = m2
        vo_ref[...] = v2

    blk = pl.BlockSpec((bm, bn), lambda i, j: (i, j))
    g_blk = pl.BlockSpec((bn, bm), lambda i, j: (j, i)) if g_transposed else blk
    out = jax.ShapeDtypeStruct((r, c), F32)
    return pl.pallas_call(
        body, name=name, grid=(pl.cdiv(r, bm), pl.cdiv(c, bn)), in_specs=[blk, g_blk, blk, blk],
        out_specs=[blk] * 4, out_shape=[out] * 4, compiler_params=_params(("parallel", "parallel")),
    )(w, g, m, v)


ANY = pl.BlockSpec(memory_space=pl.ANY)


def _place():
    x, y, c = lax.axis_index("x"), lax.axis_index("y"), lax.axis_index("c")
    chips = [(1 - x, y), (x, 1 - y), (1 - x, 1 - y)]
    return x, y, c, chips


def _sibling():
    return (lax.axis_index("x"), lax.axis_index("y"), 1 - lax.axis_index("c"))


def _wait_all(local, remote):
    for cp in local:
        cp.wait()
    for cp in remote:
        cp.wait_send()


def _place_shard(name, shard, place):
    r, cols = shard.shape
    tr = r // 4

    def body(s_ref, in_ref, out_ref):
        out_ref[...] = in_ref[...]

    return pl.pallas_call(
        body, name=name,
        grid_spec=pltpu.PrefetchScalarGridSpec(
            num_scalar_prefetch=1, grid=(4,), in_specs=[pl.BlockSpec((tr, cols), lambda i, s: (i, 0))],
            out_specs=pl.BlockSpec((tr, cols), lambda i, s: (4 * s[1] + i, 0))),
        out_shape=jax.ShapeDtypeStruct((N_SHARD * r, cols), shard.dtype), compiler_params=_params(("arbitrary",)),
    )(place, shard)


def _gather_copies(ins, outs, send_sems, recv_sems):
    x, y, c, chips = _place()
    me, sibling = (x, y, c), _sibling()

    def rows(k, px, py, pc):
        h = ins[k].shape[0] // 2
        return outs[k].at[pl.ds((2 * px + py) * 2 * h + pc * h, h), :]

    def copy(k, j, block, to, src=None):
        return pltpu.make_async_remote_copy(
            src_ref=rows(k, *block) if src is None else src, dst_ref=rows(k, *block),
            send_sem=send_sems.at[k, j], recv_sem=recv_sems.at[k, j], device_id=to, device_id_type=MESH)

    each = [(k, j, chip) for k in range(len(ins)) for j, chip in enumerate(chips)]
    half = lambda k: ins[k].at[pl.ds(c * (ins[k].shape[0] // 2), ins[k].shape[0] // 2), :]
    first = [copy(k, j, me, (*chip, c), src=half(k)) for k, j, chip in each]
    arrive = [copy(k, j, (*chip, c), me) for k, j, chip in each]
    passed = [copy(k, 3 + j, (*chip, c), sibling) for k, j, chip in each]
    landed = [copy(k, 3 + j, (*chip, 1 - c), me) for k, j, chip in each]
    return first, arrive, passed, landed


def _gather_start(ins, outs, send_sems, recv_sems):
    for cp in _gather_copies(ins, outs, send_sems, recv_sems)[0]:
        cp.start()


def _gather_finish(ins, outs, send_sems, recv_sems):
    first, arrive, passed, landed = _gather_copies(ins, outs, send_sems, recv_sems)
    for arrival, forward in zip(arrive, passed):
        arrival.wait_recv()
        forward.start()
    for cp in landed:
        cp.wait_recv()
    _wait_all([], first + passed)


def _gather_shards(shards, placed):
    n = len(shards)

    def body(*refs):
        ins, outs = refs[:n], refs[2 * n:3 * n]
        _gather_start(ins, outs, *refs[3 * n:])
        _gather_finish(ins, outs, *refs[3 * n:])

    return pl.pallas_call(
        body, name="gather_shards", in_specs=[ANY] * (2 * n), out_specs=[ANY] * n,
        out_shape=[jax.ShapeDtypeStruct(a.shape, a.dtype) for a in placed],
        input_output_aliases={n + k: k for k in range(n)},
        scratch_shapes=[pltpu.SemaphoreType.DMA((n, 6)), pltpu.SemaphoreType.DMA((n, 6))],
    )(*shards, *placed)


def _exchange8(parts, landing):
    def body(in_ref, _, out_ref, send_sems, recv_sems):
        x, y, c, _ = _place()
        sends = []
        for rel in range(1, 8):
            dx, dy, dc = rel >> 2 & 1, rel >> 1 & 1, rel & 1
            cp = pltpu.make_async_remote_copy(
                src_ref=in_ref.at[2 * (x ^ dx) + (y ^ dy)], dst_ref=out_ref.at[4 * x + 2 * y + c],
                send_sem=send_sems.at[rel - 1], recv_sem=recv_sems.at[rel - 1],
                device_id=(x ^ dx, y ^ dy, c ^ dc), device_id_type=MESH)
            cp.start()
            sends.append(cp)
        for rel in range(1, 8):
            dx, dy, dc = rel >> 2 & 1, rel >> 1 & 1, rel & 1
            pltpu.make_async_remote_copy(
                src_ref=in_ref.at[0], dst_ref=out_ref.at[4 * (x ^ dx) + 2 * (y ^ dy) + (c ^ dc)],
                send_sem=send_sems.at[rel - 1], recv_sem=recv_sems.at[rel - 1],
                device_id=(x, y, c), device_id_type=MESH).wait_recv()
        _wait_all([], sends)

    return pl.pallas_call(
        body, name="exchange8", in_specs=[ANY, ANY], out_specs=ANY, out_shape=jax.ShapeDtypeStruct(landing.shape, landing.dtype),
        input_output_aliases={1: 0}, scratch_shapes=[pltpu.SemaphoreType.DMA((7,)), pltpu.SemaphoreType.DMA((7,))],
    )(parts, landing)


def _rs_sibling(grads):
    n = len(grads)

    def body(*refs):
        ins, outs = refs[:n], refs[n:2 * n]
        send_sems, recv_sems = refs[2 * n:]
        c = lax.axis_index("c")
        sends = []
        for k in range(n):
            for s in range(N_SHARD):
                cp = pltpu.make_async_remote_copy(
                    src_ref=ins[k].at[2 * s + 1 - c], dst_ref=outs[k].at[s], send_sem=send_sems.at[k, s],
                    recv_sem=recv_sems.at[k, s], device_id=_sibling(), device_id_type=MESH)
                cp.start()
                sends.append(cp)
        for cp in sends:
            cp.wait_recv()
        _wait_all([], sends)

    return pl.pallas_call(
        body, name="rs_sibling", in_specs=[ANY] * n, out_specs=[ANY] * n,
        out_shape=[jax.ShapeDtypeStruct((N_SHARD,) + a.shape[1:], a.dtype) for a in grads],
        scratch_shapes=[pltpu.SemaphoreType.DMA((n, N_SHARD)), pltpu.SemaphoreType.DMA((n, N_SHARD))],
    )(*grads)


def _rs_chips_copies(ins, outs, send_sems, recv_sems):
    x, y, c, chips = _place()
    return [pltpu.make_async_remote_copy(
        src_ref=ins[k].at[2 * px + py], dst_ref=outs[k].at[j], send_sem=send_sems.at[k, j], recv_sem=recv_sems.at[k, j],
        device_id=(px, py, c), device_id_type=MESH) for k in range(len(ins)) for j, (px, py) in enumerate(chips)]


def _rs_chips_start(ins, outs, send_sems, recv_sems):
    for cp in _rs_chips_copies(ins, outs, send_sems, recv_sems):
        cp.start()


def _rs_chips_finish(ins, outs, send_sems, recv_sems):
    sends = _rs_chips_copies(ins, outs, send_sems, recv_sems)
    for cp in sends:
        cp.wait_recv()
    _wait_all([], sends)


def _rs_finish(bufs):
    n = len(bufs)

    def body(*refs):
        outs = refs[n:2 * n]
        send_sems, recv_sems = refs[2 * n:]
        c = lax.axis_index("c")
        sends = []
        for k in range(n):
            cp = pltpu.make_async_remote_copy(
                src_ref=outs[k].at[c], dst_ref=outs[k].at[c], send_sem=send_sems.at[k], recv_sem=recv_sems.at[k],
                device_id=_sibling(), device_id_type=MESH)
            cp.start()
            sends.append(cp)
        for k in range(n):
            pltpu.make_async_remote_copy(
                src_ref=outs[k].at[c], dst_ref=outs[k].at[1 - c], send_sem=send_sems.at[k], recv_sem=recv_sems.at[k],
                device_id=_sibling(), device_id_type=MESH).wait_recv()
        _wait_all([], sends)

    return pl.pallas_call(
        body, name="rs_finish", in_specs=[ANY] * n, out_specs=[ANY] * n,
        out_shape=[jax.ShapeDtypeStruct(a.shape, a.dtype) for a in bufs], input_output_aliases={k: k for k in range(n)},
        scratch_shapes=[pltpu.SemaphoreType.DMA((n,)), pltpu.SemaphoreType.DMA((n,))],
    )(*bufs)


def _sum3d(name, terms, scalars, grid_lead, out_lead, out_index, tr=None, out_dtype=F32):
    h, c = terms[0][0].shape[1:]
    tr = tr or h

    def body(s_ref, *refs):
        acc = refs[0][...].astype(F32)
        for ref in refs[1:-1]:
            acc = acc + ref[...].astype(F32)
        refs[-1][...] = acc.astype(refs[-1].dtype)

    in_specs = [pl.BlockSpec((1, tr, c), functools.partial(lambda l, i, s_ref, f: (f(l, s_ref), i, 0), f=f)) for _, f in terms]
    out_spec = pl.BlockSpec((1, tr, c), lambda l, i, s_ref: (out_index(l, s_ref), i, 0))
    return pl.pallas_call(
        body, name=name,
        grid_spec=pltpu.PrefetchScalarGridSpec(num_scalar_prefetch=1, grid=(grid_lead, h // tr), in_specs=in_specs, out_specs=out_spec),
        out_shape=jax.ShapeDtypeStruct((out_lead, h, c), out_dtype), compiler_params=_params(("arbitrary", "arbitrary")),
    )(scalars, *[a for a, _ in terms])


def _rs_stage1(grads, place, tag):
    g8 = [a.reshape(2 * N_SHARD, a.shape[0] // (2 * N_SHARD), a.shape[1]) for a in grads]
    from_sibling = _rs_sibling(g8)
    parts = [_sum3d(f"rs_add1_{tag}{k}", [(g8[k], lambda l, s: 2 * l + s[0]), (from_sibling[k], lambda l, s: l)], place,
                    N_SHARD, N_SHARD, lambda l, s: l, tr=g8[k].shape[1] // 2, out_dtype=BF16)
             for k in range(len(grads))]
    return g8, from_sibling, parts


def _rs_stage3(stage1, from_chips, place, tag):
    g8, from_sibling, _ = stage1
    mine = [(lambda l, s: 2 * s[1] + s[0]), (lambda l, s: s[1])]
    bufs = [_sum3d(f"rs_add2_{tag}{k}", [(g8[k], mine[0]), (from_sibling[k], mine[1])]
                   + [(from_chips[k], functools.partial(lambda l, s, j: j, j=j)) for j in range(3)],
                   place, 1, 2, lambda l, s: s[0], tr=g8[k].shape[1] // 2)
            for k in range(len(g8))]
    whole = _rs_finish(bufs)
    return [w.reshape(2 * w.shape[1], w.shape[2]) for w in whole]


_LATE = ["w_a", "w_b", "w_out", "w_down", "w_up_t"]


def _device_step(x, target, p, late_shards, late_placed, place):
    t = x.shape[0]
    d = D_MODEL
    tok = lambda arr, c=0, w=d: (arr, w, c)
    f32x = lambda n: [(d, F32)] * n
    rp_params = [p["w0"], p["w2p"], p["a0"], p["a2p"], p["g2p"], p["k_k"], p["k_a"]]
    post_params = [p["ln_w"], p["ln_b"], p["r_k"]]
    g = {}

    (xn,) = _tok_fwd("norm1_fwd", _fn_norm, [tok(x)], [p["g1"]], [(d, BF16)], tm=LIGHT_TM)
    z = _mm("in_proj", xn, p["w_in_t"], "nt", tm=t, tn=256)
    q_in, k_in, kd, qe, dec = _tok_fwd("hgates_fwd", _fn_hgates, [tok(z, 0), tok(z, 1)], [p["lb2"]], f32x(5))
    o_intra, u = _hgrn_local_fwd(q_in, k_in, kd, z)
    o_raw, h_states = _hgrn_state_fwd(o_intra, qe, dec, u)
    zl = _lerp_fwd(z, p["mu"])
    lora = tok(zl, 3 * d // LORA, LORA)
    decay, kr2, avec, bvec, gate = _tok_fwd("rprep_fwd", _fn_rprep, [tok(zl, 1), lora], rp_params, f32x(5))
    y, r_states, late = _rwkv_fwd(zl, decay, kr2, avec, bvec, late_shards, late_placed)
    p = dict(p, **dict(zip(_LATE, late)))
    (o_a,) = _tok_fwd("hpost_fwd", _fn_hpost, [tok(o_raw), tok(z, 3)], [p["gnorm"]], [(d, BF16)])
    post_toks = [tok(y), tok(zl, 0), tok(kr2), tok(zl, 2), tok(gate)]
    (o_b,) = _tok_fwd("rpost_fwd", _fn_rpost, post_toks, post_params, [(d, BF16)])
    y_a = _mm("branch_a", o_a, p["w_a"], "nn")
    y_b = _mm("branch_b", o_b, p["w_b"], "nn")
    merge_toks = [tok(z, C_G // 256, 256), tok(z, (C_G + d) // 256, 256), tok(y_a, 0, 256), tok(y_b, 0, 256)]
    (merged,) = _tok_fwd("merge_fwd", _fn_merge, merge_toks, [], [(256, BF16)], col_grid=4, tm=512)
    mix = _mm("out_proj", merged, p["w_out"], "nn")
    h1, xn2 = _tok_fwd("res1_fwd", _fn_res1, [tok(x), tok(mix)], [p["g_post1"], p["g_pre2"]], [(d, F32), (d, BF16)],
                       tm=LIGHT_TM)
    hu = _mm("up_proj", xn2, p["w_up_t"], "nt", tm=t, tn=512)
    act = _conv_fwd(hu, p["conv_w"], p["conv_b"])
    ff = _mm("down_proj", act, p["w_down"], "nn")
    d_h1, d_ff, loss, g["g_post2"] = _loss_head(h1, ff, target, p["g_post2"])

    d_act = _mm("d_act", d_ff, p["w_down"], "nt")
    g["w_down"] = _mm("dw_down", act, d_ff, "tn", tm=256, tn=1024)
    d_hu, d_cw, d_cb = _conv_bwd(hu, p["conv_w"], p["conv_b"], d_act)
    g["conv_w"], g["conv_b"] = d_cw.transpose(1, 0, 2).reshape(3, 2 * D_FF), d_cb.reshape(1, 2 * D_FF)
    d_hu = d_hu.reshape(2 * t, D_FF)
    d_xn2 = _mm("d_xn2", d_hu, p["w_up_t"], "nn", tm=t, tk=D_FF, mk=(t, 2 * D_FF), a_map=lambda i, j, q: (q, 0))
    g["w_up_t"] = _mm("dw_up", d_hu, xn2, "tn", tm=CONV_TILE, tn=1024, mk=(2 * D_FF, t),
                      a_map=lambda i, j, q: (i // N_CONV_TILES, i % N_CONV_TILES))
    d_x_res, d_mix, g["g_post1"], g["g_pre2"] = _tok_bwd(
        "res1_bwd", _fn_res1, [tok(x), tok(mix)], [p["g_post1"], p["g_pre2"]], [[tok(d_h1)], [tok(d_xn2)]],
        [(d, F32), (d, BF16)], tm=LIGHT_TM)
    d_merged = _mm("d_merged", d_mix, p["w_out"], "nt")
    g["w_out"] = _mm("dw_out", merged, d_mix, "tn")
    d_ga, d_gb, d_ya, d_yb = _tok_bwd("merge_bwd", _fn_merge, merge_toks, [], [[tok(d_merged, 0, 256)]],
                                      [(256, BF16)] * 4, col_grid=4, tm=512)
    d_oa = _mm("d_oa", d_ya, p["w_a"], "nt")
    g["w_a"] = _mm("dw_a", o_a, d_ya, "tn")
    d_ob = _mm("d_ob", d_yb, p["w_b"], "nt")
    g["w_b"] = _mm("dw_b", o_b, d_yb, "tn")
    d_oraw, d_hg, g["gnorm"] = _tok_bwd("hpost_bwd", _fn_hpost, [tok(o_raw), tok(z, 3)], [p["gnorm"]], [[tok(d_oa)]],
                                        [(d, F32), (d, BF16)])
    d_y, d_r1, d_kr2_1, d_v1, d_gate, g["ln_w"], g["ln_b"], g["r_k"] = _tok_bwd(
        "rpost_bwd", _fn_rpost, post_toks, post_params, [[tok(d_ob)]], f32x(5))
    stage1 = _rs_stage1([g[n] for n in _LATE], place, "late")
    (d_r2, d_decay, d_kr2_2, d_v2, d_avec, d_bvec), from_chips = _rwkv_bwd(
        zl, decay, kr2, avec, bvec, r_states, d_y, stage1[2])
    g.update(zip(_LATE, _rs_stage3(stage1, from_chips, place, "late")))
    prep = _tok_bwd("rprep_bwd", _fn_rprep, [tok(zl, 1), lora], rp_params,
                    [[tok(d_decay)], [tok(d_kr2_1), tok(d_kr2_2)], [tok(d_avec)], [tok(d_bvec)], [tok(d_gate)]],
                    [(d, F32), (LORA, F32)])
    d_kr, d_lora = prep[:2]
    g["w0"], g["w2p"], g["a0"], g["a2p"], g["g2p"], g["k_k"], g["k_a"] = prep[2:]
    dz_r, g["mu"] = _lerp_bwd(z, p["mu"], (d_r1, d_r2), d_kr, (d_v1, d_v2), d_lora)
    d_qe, d_dec, d_u = _hgrn_state_bwd(d_oraw, qe, dec, h_states)
    d_q_in, d_k_in, d_kd, d_vi = _hgrn_local_bwd(q_in, k_in, kd, z, d_oraw, d_u)
    d_hq, d_hf, g["lb2"] = _tok_bwd("hgates_bwd", _fn_hgates, [tok(z, 0), tok(z, 1)], [p["lb2"]],
                                    [[tok(d_q_in)], [tok(d_k_in)], [tok(d_kd)], [tok(d_qe)], [tok(d_dec)]], [(d, BF16)] * 2)
    dz = jnp.concatenate([d_hq, d_hf, d_vi.astype(BF16), d_hg, dz_r, d_ga, d_gb], axis=1)
    stage1 = _rs_stage1([_mm("dw_in", dz, xn, "tn", tm=256, tn=1024)], place, "w_in")
    d_xn, from_chips = _mm("d_xn", dz, p["w_in_t"], "nn", tm=1024, tn=512, tk=IN_COLS // 2, riders=stage1[2])
    (g["w_in_t"],) = _rs_stage3(stage1, from_chips, place, "w_in")
    grad_x, g["g1"] = _tok_bwd("norm1_bwd", _fn_norm, [tok(x)], [p["g1"]], [[tok(d_xn)]], [(d, F32)],
                               add_to_first=tok(d_x_res), tm=LIGHT_TM)
    return loss, grad_x, g


_WEIGHTS = ["attn_pre_norm", "w_in", "hgrn_lb", "hgrn_gnorm", "w_branch_a", "rwkv_mu", "rwkv_w0", "rwkv_w2", "rwkv_a0",
            "rwkv_a2", "rwkv_g2", "rwkv_k_k", "rwkv_k_a", "rwkv_r_k", "rwkv_ln_w", "rwkv_ln_b", "w_branch_b", "w_out",
            "attn_post_norm", "ffn_pre_norm", "w_up", "conv_w", "conv_b", "w_down", "ffn_post_norm"]
_REPLICATED = [("attn_pre_norm", "g1"), ("hgrn_lb", "lb2"), ("hgrn_gnorm", "gnorm"), ("rwkv_mu", "mu"), ("rwkv_w0", "w0"),
               ("rwkv_a0", "a0"), ("rwkv_k_k", "k_k"), ("rwkv_k_a", "k_a"), ("rwkv_r_k", "r_k"), ("rwkv_ln_w", "ln_w"),
               ("rwkv_ln_b", "ln_b"), ("attn_post_norm", "g_post1"), ("ffn_pre_norm", "g_pre2"), ("conv_b", "conv_b"),
               ("ffn_post_norm", "g_post2")]
SLAB_COLS = 1024


def _pack(arrays):
    pieces, total = [], 0
    for a in arrays:
        flat = a.reshape(-1)
        rows = -(-flat.shape[0] // SLAB_COLS)
        pieces.append(jnp.pad(flat, (0, rows * SLAB_COLS - flat.shape[0])).reshape(rows, SLAB_COLS))
        total += rows
    if total % 8:
        pieces.append(jnp.zeros((8 - total % 8, SLAB_COLS), F32))
    return jnp.concatenate(pieces, axis=0)


def _unpack(slab, shapes):
    out, at = [], 0
    for s in shapes:
        size = 1
        for dim in s:
            size *= dim
        rows = -(-size // SLAB_COLS)
        out.append(slab[at:at + rows].reshape(-1)[:size].reshape(s))
        at += rows
    return out


def kernel(x, attn_pre_norm, w_in, hgrn_lb, hgrn_gnorm, w_branch_a, rwkv_mu, rwkv_w0, rwkv_w2, rwkv_a0, rwkv_a2, rwkv_g2, rwkv_k_k, rwkv_k_a, rwkv_r_k, rwkv_ln_w, rwkv_ln_b, w_branch_b, w_out, attn_post_norm, ffn_pre_norm, w_up, conv_w, conv_b, w_down, ffn_post_norm, loss_target, m_attn_pre_norm, m_w_in, m_hgrn_lb, m_hgrn_gnorm, m_w_branch_a, m_rwkv_mu, m_rwkv_w0, m_rwkv_w2, m_rwkv_a0, m_rwkv_a2, m_rwkv_g2, m_rwkv_k_k, m_rwkv_k_a, m_rwkv_r_k, m_rwkv_ln_w, m_rwkv_ln_b, m_w_branch_b, m_w_out, m_attn_post_norm, m_ffn_pre_norm, m_w_up, m_conv_w, m_conv_b, m_w_down, m_ffn_post_norm, v_attn_pre_norm, v_w_in, v_hgrn_lb, v_hgrn_gnorm, v_w_branch_a, v_rwkv_mu, v_rwkv_w0, v_rwkv_w2, v_rwkv_a0, v_rwkv_a2, v_rwkv_g2, v_rwkv_k_k, v_rwkv_k_a, v_rwkv_r_k, v_rwkv_ln_w, v_rwkv_ln_b, v_w_branch_b, v_w_out, v_attn_post_norm, v_ffn_pre_norm, v_w_up, v_conv_w, v_conv_b, v_w_down, v_ffn_post_norm):
    given = dict(locals())
    w = {n: given[n] for n in _WEIGHTS}
    mom = {n: given["m_" + n] for n in _WEIGHTS}
    var = {n: given["v_" + n] for n in _WEIGHTS}
    shard = 2 * lax.axis_index("x") + lax.axis_index("y")
    place = jnp.stack([lax.axis_index("c"), shard]).astype(jnp.int32)
    row = lambda a: a.reshape(1, -1)
    lora_of = lambda d: jnp.concatenate([d["rwkv_w2"][0], d["rwkv_a2"][0], d["rwkv_g2"][0]], axis=0)

    shards = [w["w_in"][0].T.astype(BF16), lora_of(w), jnp.pad(w["conv_w"][0], ((0, 29), (0, 0)))]
    late_shards = [w["w_branch_a"][0].astype(BF16), w["w_branch_b"][0].astype(BF16), w["w_out"][0].astype(BF16),
                   w["w_down"][0].astype(BF16), w["w_up"][0].T.astype(BF16)]
    placed = [_place_shard(f"place_{k}", a, place) for k, a in enumerate(shards)]
    late_placed = [_place_shard(f"place_late_{k}", a, place) for k, a in enumerate(late_shards)]
    w_in_t, lora_g, conv_g = _gather_shards(shards, placed)
    lora_full = lora_g.reshape(N_SHARD, LORA, 256).transpose(1, 0, 2).reshape(LORA, D_MODEL)
    conv_full = conv_g.reshape(N_SHARD, 32, 2 * D_FF // N_SHARD)[:, :3].transpose(1, 0, 2).reshape(3, 2 * D_FF)
    lrow = lax.broadcasted_iota(jnp.int32, (LORA, 1), 0)
    p = {
        "g1": row(w["attn_pre_norm"]), "lb2": w["hgrn_lb"], "gnorm": row(w["hgrn_gnorm"]), "w_in_t": w_in_t,
        "mu": row(w["rwkv_mu"]), "w0": row(w["rwkv_w0"]), "a0": row(w["rwkv_a0"]),
        "w2p": jnp.where(lrow < 64, lora_full, 0.0), "a2p": jnp.where((lrow >= 64) & (lrow < 128), lora_full, 0.0),
        "g2p": jnp.where(lrow >= 128, lora_full, 0.0),
        "k_k": row(w["rwkv_k_k"]), "k_a": row(w["rwkv_k_a"]), "r_k": row(w["rwkv_r_k"]), "ln_w": row(w["rwkv_ln_w"]),
        "ln_b": row(w["rwkv_ln_b"]), "g_post1": row(w["attn_post_norm"]),
        "g_pre2": row(w["ffn_pre_norm"]), "conv_w": conv_full, "conv_b": row(w["conv_b"]),
        "g_post2": row(w["ffn_post_norm"]),
    }

    loss, grad_x, g = _device_step(x[0], loss_target[0], p, late_shards, late_placed, place)

    g_in_t = g["w_in_t"]
    g_a, g_b, g_o, g_dn, g_up_t = [g[n] for n in _LATE]
    rep_shapes = [w[n].shape for n, _ in _REPLICATED]
    rep = _pack([g[key] for _, key in _REPLICATED])
    n_rep_rows = rep.shape[0]
    cw = 2 * D_FF // N_SHARD
    lora_rows, conv_rows = LORA * 256 // SLAB_COLS, -(-3 * cw // SLAB_COLS)
    lora_g = jnp.concatenate([g["w2p"][0:64], g["a2p"][64:128], g["g2p"][128:256]], axis=0)
    lora_parts = lora_g.reshape(LORA, N_SHARD, 256).transpose(1, 0, 2).reshape(N_SHARD, lora_rows, SLAB_COLS)
    conv_parts = g["conv_w"].reshape(3, N_SHARD, cw).transpose(1, 0, 2).reshape(N_SHARD, 3 * cw)
    conv_parts = jnp.pad(conv_parts, ((0, 0), (0, conv_rows * SLAB_COLS - 3 * cw))).reshape(N_SHARD, conv_rows, SLAB_COLS)
    n_rows = n_rep_rows + lora_rows + conv_rows
    fill = jnp.zeros((N_SHARD, -n_rows % 8, SLAB_COLS), F32)
    parts = jnp.concatenate([jnp.broadcast_to(rep, (N_SHARD,) + rep.shape), lora_parts, conv_parts, fill], axis=1)
    me = 4 * lax.axis_index("x") + 2 * lax.axis_index("y") + lax.axis_index("c")
    landing = lax.dynamic_update_slice(jnp.zeros((8,) + parts.shape[1:], F32),
                                       lax.dynamic_index_in_dim(parts, shard, 0, keepdims=True), (me, 0, 0))
    gathered = _exchange8(parts, landing)
    summed = _sum3d("small_sum", [(gathered, functools.partial(lambda l, s, i: i, i=i)) for i in range(8)], place, 1, 1,
                    lambda l, s: 0)[0]
    lora_grad = summed[n_rep_rows:n_rep_rows + lora_rows].reshape(LORA, 256)
    conv_grad = summed[n_rep_rows + lora_rows:n_rows].reshape(-1)[:3 * cw].reshape(3, cw)

    res = {}

    def put(name, outs, shape=None):
        res[name] = [o.reshape(w[name].shape if shape is None else shape) for o in outs]

    put("w_in", _adamw("adamw_w_in", w["w_in"][0], g_in_t, mom["w_in"][0], var["w_in"][0], 1024, 128, g_transposed=True))
    put("w_up", _adamw("adamw_w_up", w["w_up"][0], g_up_t, mom["w_up"][0], var["w_up"][0], 1024, 128, g_transposed=True))
    for name, grad in (("w_branch_a", g_a), ("w_branch_b", g_b), ("w_out", g_o)):
        put(name, _adamw("adamw_" + name, w[name][0], grad, mom[name][0], var[name][0], 256, 1024))
    put("w_down", _adamw("adamw_w_down", w["w_down"][0], g_dn, mom["w_down"][0], var["w_down"][0], 176, 1024))
    put("conv_w", _adamw("adamw_conv_w", w["conv_w"][0], conv_grad, mom["conv_w"][0], var["conv_w"][0], 3, 2 * D_FF // N_SHARD))
    lora_out = _adamw("adamw_lora", lora_of(w), lora_grad, lora_of(mom), lora_of(var), LORA, 256)
    for name, lo, hi in (("rwkv_w2", 0, 64), ("rwkv_a2", 64, 128), ("rwkv_g2", 128, 256)):
        put(name, [o[lo:hi] for o in lora_out])
    rep_names = [n for n, _ in _REPLICATED]
    rep_out = _adamw("adamw_small", _pack([w[n] for n in rep_names]), summed[:n_rep_rows], _pack([mom[n] for n in rep_names]),
                     _pack([var[n] for n in rep_names]), n_rep_rows, SLAB_COLS)
    for name, parts in zip(rep_names, zip(*[_unpack(o, rep_shapes) for o in rep_out])):
        put(name, list(parts))

    loss = lax.psum(loss[0, 0], ("x", "y", "c"))
    return (loss, grad_x[None], *[res[n][0] for n in _WEIGHTS], *[res[n][1] for n in _WEIGHTS],
            *[res[n][2] for n in _WEIGHTS], *[res[n][3] for n in _WEIGHTS])
```

```python
import functools

import jax
import jax.numpy as jnp
from jax import lax
from jax.experimental import pallas as pl
from jax.experimental.pallas import tpu as pltpu

F32, BF16 = jnp.float32, jnp.bfloat16
MESH = pl.DeviceIdType.MESH

D_MODEL = 1024
HGRN_HEADS = 8
HGRN_K = 128
HGRN_SCALE = HGRN_K ** -0.5
CHUNK = 32
RWKV_HEAD = 64
LORA = 256
D_FF = 2816
EPS = 1e-6
GN_EPS = 1e-5 * RWKV_HEAD
N_SHARD = 4
ADAM_LR, ADAM_B1, ADAM_B2, ADAM_EPS, ADAM_WD, ADAM_STEP = 0.001, 0.9, 0.999, 1e-08, 0.01, 10

LANES = 128
VMEM_LIMIT = 56 * 1024 * 1024
SCAN_TB = 16
SCAN_GROUP = 256
LIGHT_TM = 256

C_HQ, C_HF, C_HI, C_HG = 0, 1024, 2048, 3072
C_R = 4096
R_COLS = 3328
C_G = 7424
IN_COLS = 9472


def _params(sem=None, **kw):
    return pltpu.CompilerParams(dimension_semantics=sem, vmem_limit_bytes=VMEM_LIMIT, **kw)


def _seg_matrix(n, seg):
    r = lax.broadcasted_iota(jnp.int32, (n, n), 0) // seg
    c = lax.broadcasted_iota(jnp.int32, (n, n), 1) // seg
    return (r == c).astype(BF16)


def _split3(x):
    hi = x.astype(BF16)
    r1 = x - hi.astype(F32)
    mid = r1.astype(BF16)
    lo = (r1 - mid.astype(F32)).astype(BF16)
    return hi, mid, lo


def _segsum_impl(x, seg):
    e = _seg_matrix(LANES, seg)
    outs = []
    for g in range(x.shape[1] // LANES):
        hi, mid, lo = _split3(x[:, g * LANES:(g + 1) * LANES])
        outs.append(jnp.dot(hi, e, preferred_element_type=F32) + jnp.dot(mid, e, preferred_element_type=F32)
                    + jnp.dot(lo, e, preferred_element_type=F32))
    return outs[0] if len(outs) == 1 else jnp.concatenate(outs, axis=1)


def _make_segsum(seg):
    @jax.custom_vjp
    def f(x):
        return _segsum_impl(x, seg)

    f.defvjp(lambda x: (_segsum_impl(x, seg), None), lambda _, ct: (_segsum_impl(ct, seg),))
    return f


_segsum64 = _make_segsum(RWKV_HEAD)
_segsum128 = _make_segsum(HGRN_K)


def _chunk_mm_impl(x, kind, transposed):
    n = x.shape[0]
    r = lax.broadcasted_iota(jnp.int32, (n, n), 1 if transposed else 0)
    c = lax.broadcasted_iota(jnp.int32, (n, n), 0 if transposed else 1)
    same = (r // CHUNK) == (c // CHUNK)
    if kind == "cumsum":
        m = same & (r >= c)
    else:
        m = same & (c % CHUNK == (CHUNK // 2 - 1 if kind == "mid" else CHUNK - 1))
    m = m.astype(BF16)
    hi, mid, lo = _split3(x)
    return (jnp.dot(m, hi, preferred_element_type=F32) + jnp.dot(m, mid, preferred_element_type=F32)
            + jnp.dot(m, lo, preferred_element_type=F32))


def _make_chunk_mm(kind):
    @jax.custom_vjp
    def f(x):
        return _chunk_mm_impl(x, kind, False)

    f.defvjp(lambda x: (_chunk_mm_impl(x, kind, False), None), lambda _, ct: (_chunk_mm_impl(ct, kind, True),))
    return f


_chunk_cumsum = _make_chunk_mm("cumsum")
_chunk_mid = _make_chunk_mm("mid")
_chunk_last = _make_chunk_mm("last")


@jax.custom_vjp
def _bdot(x, w):
    return jnp.dot(x.astype(BF16), w.astype(BF16), preferred_element_type=F32)


def _bdot_fwd(x, w):
    return _bdot(x, w), (x, w)


def _bdot_bwd(res, ct):
    x, w = res
    ctb = ct.astype(BF16)
    dx = lax.dot_general(ctb, w.astype(BF16), (((1,), (1,)), ((), ())), preferred_element_type=F32)
    dw = lax.dot_general(x.astype(BF16), ctb, (((0,), (0,)), ((), ())), preferred_element_type=F32)
    return dx, dw


_bdot.defvjp(_bdot_fwd, _bdot_bwd)


def _sigmoid(x):
    return 1.0 / (1.0 + jnp.exp(-x))


def _silu(x):
    return x * _sigmoid(x)


def _softplus(x):
    return jnp.maximum(x, 0.0) + jnp.log(1.0 + jnp.exp(-jnp.abs(x)))


def _rms(x, g):
    return x * lax.rsqrt(jnp.mean(x * x, axis=-1, keepdims=True) + EPS) * g


def _fn_norm(t, p):
    return [_rms(t[0], p[0])]


def _fn_hgates(t, p):
    hq, hf = t
    lb2 = p[0]
    m = jnp.max(lb2, axis=0, keepdims=True)
    e = jnp.exp(lb2 - m)
    first = lax.broadcasted_iota(jnp.int32, e.shape, 0) == 0
    lb = jnp.sum(jnp.where(first, e, 0.0), axis=0, keepdims=True) / jnp.sum(e, axis=0, keepdims=True)
    f = lb + (1.0 - lb) * _sigmoid(hf)
    q, k = _silu(hq) * HGRN_SCALE, 1.0 - f
    b = _chunk_cumsum(jnp.log(f))
    b_ref, b_last = _chunk_mid(b), _chunk_last(b)
    return [q * jnp.exp(b - b_ref), k * jnp.exp(b_ref - b), k * jnp.exp(b_last - b), q * jnp.exp(b), jnp.exp(b_last)]


def _fn_hpost(t, p):
    o, hg = t
    ms = _segsum128(o * o) * (1.0 / HGRN_K)
    return [o * lax.rsqrt(ms + EPS) * p[0] * _silu(hg)]


def _fn_rprep(t, p):
    kr, lora = t
    w0, w2p, a0, a2p, g2p, k_k, k_a = p
    pre_w = w0 + _bdot(jnp.tanh(lora), w2p)
    w_log = -_softplus(-pre_w) - 0.5
    decay = jnp.exp(-jnp.exp(w_log))
    a = _sigmoid(a0 + _bdot(lora, a2p))
    g = _bdot(_sigmoid(lora), g2p)
    kk = kr * k_k
    kk = kk / jnp.maximum(jnp.sqrt(_segsum64(kk * kk)), 1e-12)
    kr2 = kr * (1.0 + (a - 1.0) * k_a)
    return [decay, kr2, -kk, kk * a, g]


def _fn_rpost(t, p):
    y, r, kr2, v, g = t
    ln_w, ln_b, r_k = p
    mu = _segsum64(y) * (1.0 / RWKV_HEAD)
    yc = y - mu
    var = _segsum64(yc * yc) * (1.0 / RWKV_HEAD)
    yn = yc * lax.rsqrt(var + GN_EPS) * ln_w + ln_b
    bonus = _segsum64(r * kr2 * r_k) * v
    return [(yn + bonus) * g]


def _fn_merge(t, p):
    ga, gb, ya, yb = t
    return [_sigmoid(ga) * ya + _sigmoid(gb) * yb]


def _fn_res1(t, p):
    x, mix = t
    h1 = x + _rms(mix, p[0])
    return [h1, _rms(h1, p[1])]


def _tok_call(name, fn, toks, params, outs, red_shapes=(), tm=128, col_grid=1):
    n_t, n_p, n_o = len(toks), len(params), len(outs)
    t_len = toks[0][0].shape[0]
    tm = min(tm, t_len)

    def body(*refs):
        tv = [r[...].astype(F32) for r in refs[:n_t]]
        pv = [r[...] for r in refs[n_t:n_t + n_p]]
        o, red = fn(tv, pv)
        for ref, val in zip(refs[n_t + n_p:n_t + n_p + n_o], o):
            ref[...] = val.astype(ref.dtype)
        red_refs = refs[n_t + n_p + n_o:]
        if red_refs:
            first = pl.program_id(0) == 0

            @pl.when(first)
            def _():
                for ref, val in zip(red_refs, red):
                    ref[...] = val

            @pl.when(jnp.logical_not(first))
            def _():
                for ref, val in zip(red_refs, red):
                    ref[...] += val

    in_specs = [pl.BlockSpec((tm, w), functools.partial(lambda i, j, c: (i, c + j), c=c)) for (_, w, c) in toks]
    in_specs += [pl.BlockSpec(p.shape, lambda i, j: (0, 0)) for p in params]
    out_specs = [pl.BlockSpec((tm, w), lambda i, j: (i, j)) for (w, _) in outs]
    out_specs += [pl.BlockSpec(s, lambda i, j: (0, 0)) for s in red_shapes]
    out_shape = [jax.ShapeDtypeStruct((t_len, w * col_grid), dt) for (w, dt) in outs]
    out_shape += [jax.ShapeDtypeStruct(s, F32) for s in red_shapes]
    return pl.pallas_call(
        body, name=name, grid=(t_len // tm, col_grid), in_specs=in_specs, out_specs=out_specs, out_shape=out_shape,
        compiler_params=_params(("arbitrary", "arbitrary")),
    )(*[a for (a, _, _) in toks], *params)


def _tok_fwd(name, fn, toks, params, outs, **kw):
    return _tok_call(name, lambda tv, pv: (fn(tv, pv), []), toks, params, outs, **kw)


def _tok_bwd(name, fn, toks, params, cts, want, add_to_first=None, **kw):
    n_t = len(toks)
    flat = [c for group in cts for c in group]
    extra = [] if add_to_first is None else [add_to_first]

    def bwd(tv, pv):
        prim, rest = tv[:n_t], tv[n_t:]
        ct, at = [], 0
        for group in cts:
            ct.append(functools.reduce(lambda u, v: u + v, rest[at:at + len(group)]))
            at += len(group)
        _, vjp = jax.vjp(lambda *a: tuple(fn(list(a[:n_t]), list(a[n_t:]))), *prim, *pv)
        g = vjp(tuple(ct))
        tok_grads = [g[i] for i in range(n_t) if want[i] is not None]
        if extra:
            tok_grads[0] = tok_grads[0] + rest[at]
        return tok_grads, list(g[n_t:])

    return _tok_call(name, bwd, list(toks) + flat + extra, params, [w for w in want if w is not None],
                     red_shapes=[p.shape for p in params], **kw)


def _mm(name, a, b, mode, out_dtype=F32, tm=None, tn=None, tk=None, riders=None, a_map=None, mk=None):
    if mode == "nn":
        (m, k), (_, n) = a.shape, b.shape
    elif mode == "nt":
        (m, k), (n, _) = a.shape, b.shape
    else:
        (k, m), (_, n) = a.shape, b.shape
    if mk is not None:
        m, k = mk
    tm = (512 if mode == "tn" else 2048) if tm is None else tm
    tn = (512 if mode == "tn" else 256) if tn is None else tn
    tk = k if tk is None else tk
    tm, tn = min(tm, m), min(tn, n)
    nk = k // tk
    assert m % tm == 0 and n % tn == 0 and k % tk == 0, (name, a.shape, b.shape, tm, tn, tk)
    a_spec = pl.BlockSpec((tk, tm), lambda i, j, q: (q, i)) if mode == "tn" else pl.BlockSpec((tm, tk), lambda i, j, q: (i, q))
    if a_map is not None:
        a_spec = pl.BlockSpec(a_spec.block_shape, a_map)
    b_spec = pl.BlockSpec((tn, tk), lambda i, j, q: (j, q)) if mode == "nt" else pl.BlockSpec((tk, tn), lambda i, j, q: (q, j))
    dn = {"nn": (((1,), (0,)), ((), ())), "nt": (((1,), (1,)), ((), ())), "tn": (((0,), (0,)), ((), ()))}[mode]
    grid = (m // tm, n // tn, nk)
    nr = 0 if riders is None else len(riders)

    def body(*refs):
        a_ref, b_ref, o_ref = refs[0], refs[1], refs[2 + nr]
        acc = refs[3 + 2 * nr] if nk > 1 else None
        if nr:
            exchange = (refs[2:2 + nr], refs[3 + nr:3 + 2 * nr], *refs[-2:])
            at = [pl.program_id(ax) for ax in range(3)]

            @pl.when((at[0] == 0) & (at[1] == 0) & (at[2] == 0))
            def _():
                _rs_chips_start(*exchange)

        p = lax.dot_general(a_ref[...], b_ref[...], dn, preferred_element_type=F32)
        if nk == 1:
            o_ref[...] = p.astype(o_ref.dtype)
        else:
            q = pl.program_id(2)

            @pl.when(q == 0)
            def _():
                acc[...] = p

            @pl.when(q > 0)
            def _():
                acc[...] += p

            @pl.when(q == nk - 1)
            def _():
                o_ref[...] = acc[...].astype(o_ref.dtype)

        if nr:
            @pl.when((at[0] == grid[0] - 1) & (at[1] == grid[1] - 1) & (at[2] == grid[2] - 1))
            def _():
                _rs_chips_finish(*exchange)

    scratch = [pltpu.VMEM((tm, tn), F32)] if nk > 1 else []
    out_specs = [pl.BlockSpec((tm, tn), lambda i, j, q: (i, j))]
    out_shape = [jax.ShapeDtypeStruct((m, n), out_dtype)]
    if nr:
        scratch += [pltpu.SemaphoreType.DMA((nr, 3)), pltpu.SemaphoreType.DMA((nr, 3))]
        out_specs += [ANY] * nr
        out_shape += [jax.ShapeDtypeStruct((3,) + r.shape[1:], r.dtype) for r in riders]
    outs = pl.pallas_call(
        body, name=name, grid=grid, in_specs=[a_spec, b_spec] + [ANY] * nr, out_specs=out_specs, out_shape=out_shape,
        scratch_shapes=scratch,
        compiler_params=_params(("arbitrary",) * 3 if nr else ("parallel", "parallel", "arbitrary")),
    )(a, b, *(riders or []))
    return (outs[0], outs[1:]) if nr else outs[0]


def _shift_down(z, n):
    rows = lax.broadcasted_iota(jnp.int32, z.shape, 0)
    return jnp.where(rows < n, 0.0, pltpu.roll(z, n, 0))


def _shift_up(z, n):
    t = z.shape[0]
    rows = lax.broadcasted_iota(jnp.int32, z.shape, 0)
    return jnp.where(rows >= t - n, 0.0, pltpu.roll(z, t - n, 0))


def _lerp_fwd(z, mu):
    t = z.shape[0]
    w = 256

    def body(z_ref, mu_ref, o_ref):
        zz = z_ref[...]
        o_ref[...] = zz + mu_ref[...] * (_shift_down(zz, 1) - zz)

    return pl.pallas_call(
        body, name="lerp_fwd", grid=(R_COLS // w,),
        in_specs=[pl.BlockSpec((t, w), lambda j: (0, C_R // w + j)), pl.BlockSpec((1, w), lambda j: (0, j))],
        out_specs=pl.BlockSpec((t, w), lambda j: (0, j)), out_shape=jax.ShapeDtypeStruct((t, R_COLS), F32),
        compiler_params=_params(("parallel",)),
    )(z, mu)


def _lerp_bwd(z, mu, d_r, d_k, d_v, d_lora):
    t = z.shape[0]
    w = 256
    per = D_MODEL // w

    def body(z_ref, mu_ref, r1_ref, r2_ref, k_ref, v1_ref, v2_ref, l_ref, dz_ref, dmu_ref):
        j = pl.program_id(0)
        zz, m = z_ref[...], mu_ref[...]
        d = jnp.where(j < per, r1_ref[...] + r2_ref[...],
                      jnp.where(j < 2 * per, k_ref[...], jnp.where(j < 3 * per, v1_ref[...] + v2_ref[...], l_ref[...])))
        dz_ref[...] = (d * (1.0 - m) + _shift_up(d * m, 1)).astype(dz_ref.dtype)
        dmu_ref[...] = jnp.sum(d * (_shift_down(zz, 1) - zz), axis=0, keepdims=True)

    piece = lambda first: pl.BlockSpec((t, w), lambda j: (0, jnp.clip(j - first, 0, per - 1)))
    return pl.pallas_call(
        body, name="lerp_bwd", grid=(R_COLS // w,),
        in_specs=[pl.BlockSpec((t, w), lambda j: (0, C_R // w + j)), pl.BlockSpec((1, w), lambda j: (0, j)),
                  piece(0), piece(0), piece(per), piece(2 * per), piece(2 * per), pl.BlockSpec((t, w), lambda j: (0, 0))],
        out_specs=[pl.BlockSpec((t, w), lambda j: (0, j)), pl.BlockSpec((1, w), lambda j: (0, j))],
        out_shape=[jax.ShapeDtypeStruct((t, R_COLS), BF16), jax.ShapeDtypeStruct((1, R_COLS), F32)],
        compiler_params=_params(("arbitrary",)),
    )(z, mu, *d_r, d_k, *d_v, d_lora)


CONV_TILE = 256
N_CONV_TILES = D_FF // CONV_TILE


def _conv(h, w, b):
    return b + w[0:1, :] * _shift_down(h, 2) + w[1:2, :] * _shift_down(h, 1) + w[2:3, :] * h


def _conv_fwd(hu, conv_w, conv_b):
    t = hu.shape[0]
    n = N_CONV_TILES

    def body(hg_ref, hv_ref, wg_ref, wv_ref, bg_ref, bv_ref, o_ref):
        gate = _conv(hg_ref[...], wg_ref[...], bg_ref[...])
        val = _conv(hv_ref[...], wv_ref[...], bv_ref[...])
        o_ref[...] = (_silu(gate) * val).astype(o_ref.dtype)

    col = lambda off: pl.BlockSpec((t, CONV_TILE), lambda j: (0, j + off))
    wspec = lambda off: pl.BlockSpec((3, CONV_TILE), lambda j: (0, j + off))
    bspec = lambda off: pl.BlockSpec((1, CONV_TILE), lambda j: (0, j + off))
    return pl.pallas_call(
        body, name="conv_fwd", grid=(n,),
        in_specs=[col(0), col(n), wspec(0), wspec(n), bspec(0), bspec(n)],
        out_specs=pl.BlockSpec((t, CONV_TILE), lambda j: (0, j)), out_shape=jax.ShapeDtypeStruct((t, D_FF), BF16),
        compiler_params=_params(("parallel",)),
    )(hu, hu, conv_w, conv_w, conv_b, conv_b)


def _conv_bwd(hu, conv_w, conv_b, d_act):
    t = hu.shape[0]
    n = N_CONV_TILES

    def body(hg_ref, hv_ref, wg_ref, wv_ref, bg_ref, bv_ref, d_ref, dh_ref, dw_ref, db_ref):
        hg, hv, wg, wv = hg_ref[...], hv_ref[...], wg_ref[...], wv_ref[...]
        gate = _conv(hg, wg, bg_ref[...])
        val = _conv(hv, wv, bv_ref[...])
        d = d_ref[...]
        sg = _sigmoid(gate)
        d_gate = d * val * (sg * (1.0 + gate * (1.0 - sg)))
        d_val = d * (gate * sg)
        for half, (dc, h, w) in enumerate(((d_gate, hg, wg), (d_val, hv, wv))):
            dh = w[2:3, :] * dc + w[1:2, :] * _shift_up(dc, 1) + w[0:1, :] * _shift_up(dc, 2)
            dh_ref[half] = dh.astype(dh_ref.dtype)
            dw_ref[half, 0:1, :] = jnp.sum(dc * _shift_down(h, 2), axis=0, keepdims=True)
            dw_ref[half, 1:2, :] = jnp.sum(dc * _shift_down(h, 1), axis=0, keepdims=True)
            dw_ref[half, 2:3, :] = jnp.sum(dc * h, axis=0, keepdims=True)
            db_ref[half] = jnp.sum(dc, axis=0, keepdims=True)

    gcol = lambda rows: pl.BlockSpec((rows, CONV_TILE), lambda j: (0, j))
    vcol = lambda rows: pl.BlockSpec((rows, CONV_TILE), lambda j: (0, j + n))
    both = lambda rows: pl.BlockSpec((2, rows, CONV_TILE), lambda j: (0, 0, j))
    return pl.pallas_call(
        body, name="conv_bwd", grid=(n,),
        in_specs=[gcol(t), vcol(t), gcol(3), vcol(3), gcol(1), vcol(1), gcol(t)],
        out_specs=[both(t), both(3), both(1)],
        out_shape=[jax.ShapeDtypeStruct((2, t, D_FF), BF16), jax.ShapeDtypeStruct((2, 3, D_FF), F32),
                   jax.ShapeDtypeStruct((2, 1, D_FF), F32)],
        compiler_params=_params(("parallel",)),
    )(hu, hu, conv_w, conv_w, conv_b, conv_b, d_act)


_NN = (((1,), (0,)), ((), ()))
_NT = (((1,), (1,)), ((), ()))
_TN = (((0,), (0,)), ((), ()))
HGRN_CB = 8
HGRN_LOCAL_CB = 16


def _bf_dot(a, b, dn):
    return lax.dot_general(a.astype(BF16), b.astype(BF16), dn, preferred_element_type=F32)


def _tril():
    n = HGRN_LOCAL_CB * CHUNK
    r = lax.broadcasted_iota(jnp.int32, (n, n), 0)
    c = lax.broadcasted_iota(jnp.int32, (n, n), 1)
    return (r // CHUNK == c // CHUNK) & (r >= c)


def _hgrn_specs(t):
    rows = HGRN_LOCAL_CB * CHUNK
    head = pl.BlockSpec((rows, HGRN_K), lambda h, n: (n, h))
    v_head = pl.BlockSpec((rows, HGRN_K), lambda h, n: (n, C_HI // HGRN_K + h))
    mats = pl.BlockSpec((1, HGRN_LOCAL_CB, HGRN_K, HGRN_K), lambda h, n: (h, n, 0, 0))
    return head, v_head, mats, (HGRN_HEADS, t // rows)


def _hgrn_local_fwd(q_in, k_in, kd, z):
    t = q_in.shape[0]
    head, v_head, mats, grid = _hgrn_specs(t)

    def body(q_ref, k_ref, kd_ref, v_ref, o_ref, u_ref):
        v = v_ref[...]
        scores = jnp.where(_tril(), _bf_dot(q_ref[...], k_ref[...], _NT), 0.0)
        o_ref[...] = _bf_dot(scores, v, _NN)
        for n in range(HGRN_LOCAL_CB):
            rows = slice(n * CHUNK, (n + 1) * CHUNK)
            u_ref[0, n] = _bf_dot(v[rows], kd_ref[rows, :], _TN)

    return pl.pallas_call(
        body, name="hgrn_local_fwd", grid=grid, in_specs=[head, head, head, v_head], out_specs=[head, mats],
        out_shape=[jax.ShapeDtypeStruct((t, D_MODEL), F32),
                   jax.ShapeDtypeStruct((HGRN_HEADS, t // CHUNK, HGRN_K, HGRN_K), F32)],
        compiler_params=_params(("parallel", "parallel")),
    )(q_in, k_in, kd, z)


def _hgrn_state_specs(t, reverse=False):
    rows = HGRN_CB * CHUNK
    nb = t // rows
    at = (lambda n: nb - 1 - n) if reverse else (lambda n: n)
    tok = pl.BlockSpec((rows, D_MODEL), lambda n: (at(n), 0))
    mats = pl.BlockSpec((HGRN_HEADS, HGRN_CB, HGRN_K, HGRN_K), lambda n: (0, at(n), 0, 0))
    return tok, mats, nb


def _hgrn_state_fwd(o_intra, qe, dec, u):
    t = qe.shape[0]
    tok, mats, nb = _hgrn_state_specs(t)

    def body(oi_ref, qe_ref, dec_ref, u_ref, o_ref, st_ref, s_ref):
        @pl.when(pl.program_id(0) == 0)
        def _():
            s_ref[...] = jnp.zeros_like(s_ref)

        st = [s_ref[h] for h in range(HGRN_HEADS)]
        for n in range(HGRN_CB):
            rows = slice(n * CHUNK, (n + 1) * CHUNK)
            for h in range(HGRN_HEADS):
                cols = slice(h * HGRN_K, (h + 1) * HGRN_K)
                st_ref[h, n] = st[h]
                o_ref[rows, cols] = oi_ref[rows, cols] + _bf_dot(qe_ref[rows, cols], st[h], _NT)
                st[h] = st[h] * dec_ref[n * CHUNK:n * CHUNK + 1, cols] + u_ref[h, n]
        for h in range(HGRN_HEADS):
            s_ref[h] = st[h]

    return pl.pallas_call(
        body, name="hgrn_state_fwd", grid=(nb,), in_specs=[tok, tok, tok, mats], out_specs=[tok, mats],
        out_shape=[jax.ShapeDtypeStruct((t, D_MODEL), F32),
                   jax.ShapeDtypeStruct((HGRN_HEADS, t // CHUNK, HGRN_K, HGRN_K), F32)],
        scratch_shapes=[pltpu.VMEM((HGRN_HEADS, HGRN_K, HGRN_K), F32)],
        compiler_params=_params(("arbitrary",)),
    )(o_intra, qe, dec, u)


def _hgrn_state_bwd(d_o, qe, dec, states):
    t = qe.shape[0]
    tok_r, mats_r, nb = _hgrn_state_specs(t, reverse=True)

    def body(do_ref, qe_ref, dec_ref, st_ref, dqe_ref, ddec_ref, du_ref, d_ref):
        @pl.when(pl.program_id(0) == 0)
        def _():
            d_ref[...] = jnp.zeros_like(d_ref)

        first_row = lax.broadcasted_iota(jnp.int32, (CHUNK, HGRN_K), 0) == 0
        d = [d_ref[h] for h in range(HGRN_HEADS)]
        for n in reversed(range(HGRN_CB)):
            rows = slice(n * CHUNK, (n + 1) * CHUNK)
            for h in range(HGRN_HEADS):
                cols = slice(h * HGRN_K, (h + 1) * HGRN_K)
                st, do = st_ref[h, n], do_ref[rows, cols]
                du_ref[h, n] = d[h]
                ddec_ref[rows, cols] = jnp.where(first_row, jnp.sum(d[h] * st, axis=0, keepdims=True), 0.0)
                dqe_ref[rows, cols] = _bf_dot(do, st, _NN)
                d[h] = d[h] * dec_ref[n * CHUNK:n * CHUNK + 1, cols] + _bf_dot(do, qe_ref[rows, cols], _TN)
        for h in range(HGRN_HEADS):
            d_ref[h] = d[h]

    out = jax.ShapeDtypeStruct((t, D_MODEL), F32)
    return pl.pallas_call(
        body, name="hgrn_state_bwd", grid=(nb,), in_specs=[tok_r, tok_r, tok_r, mats_r], out_specs=[tok_r, tok_r, mats_r],
        out_shape=[out, out, jax.ShapeDtypeStruct((HGRN_HEADS, t // CHUNK, HGRN_K, HGRN_K), F32)],
        scratch_shapes=[pltpu.VMEM((HGRN_HEADS, HGRN_K, HGRN_K), F32)],
        compiler_params=_params(("arbitrary",)),
    )(d_o, qe, dec, states)


def _hgrn_local_bwd(q_in, k_in, kd, z, d_o, d_u):
    t = q_in.shape[0]
    head, v_head, mats, grid = _hgrn_specs(t)

    def body(q_ref, k_ref, kd_ref, v_ref, do_ref, du_ref, dq_ref, dk_ref, dkd_ref, dv_ref):
        tril = _tril()
        q, k, v, do = q_ref[...], k_ref[...], v_ref[...], do_ref[...]
        scores = jnp.where(tril, _bf_dot(q, k, _NT), 0.0)
        d_scores = jnp.where(tril, _bf_dot(do, v, _NT), 0.0)
        dq_ref[...] = _bf_dot(d_scores, k, _NN)
        dk_ref[...] = _bf_dot(d_scores, q, _TN)
        dv = _bf_dot(scores, do, _TN)
        for n in range(HGRN_LOCAL_CB):
            rows = slice(n * CHUNK, (n + 1) * CHUNK)
            du = du_ref[0, n]
            dv_ref[rows, :] = dv[rows] + _bf_dot(kd_ref[rows, :], du, _NT)
            dkd_ref[rows, :] = _bf_dot(v[rows], du, _NN)

    out = jax.ShapeDtypeStruct((t, D_MODEL), F32)
    return pl.pallas_call(
        body, name="hgrn_local_bwd", grid=grid, in_specs=[head, head, head, v_head, head, mats], out_specs=[head] * 4,
        out_shape=[out] * 4, compiler_params=_params(("parallel", "parallel")),
    )(q_in, k_in, kd, z, d_o, d_u)


def _split2(x):
    hi = x.astype(BF16)
    return hi, (x - hi.astype(F32)).astype(BF16)


def _seg_bcast(xs, e, one_term=False):
    n = RWKV_HEAD
    if one_term:
        out = jnp.dot(jnp.concatenate([x.astype(BF16) for x in xs], axis=0), e, preferred_element_type=F32)
        return [out[i * n:(i + 1) * n] for i in range(len(xs))]
    parts = [t for x in xs for t in _split2(x)]
    out = jnp.dot(jnp.concatenate(parts, axis=0), e, preferred_element_type=F32)
    return [out[2 * i * n:(2 * i + 1) * n] + out[(2 * i + 1) * n:(2 * i + 2) * n] for i in range(len(xs))]


def _rows_to_cols(rows, diag, e):
    zero = jnp.zeros((), BF16)
    parts = [jnp.where(diag, row.astype(BF16), zero) for row in rows]
    out = jnp.dot(jnp.concatenate(parts, axis=0), e, preferred_element_type=F32)
    n = RWKV_HEAD
    return [out[i * n:(i + 1) * n] for i in range(len(rows))]


def _col_to_row(col, diag):
    return jnp.sum(jnp.where(diag, col, 0.0), axis=0, keepdims=True)


_SCAN_PAIRS = ((0, 1), (2, 3))


def _scan_consts():
    e = _seg_matrix(SCAN_GROUP, RWKV_HEAD)
    i = lax.broadcasted_iota(jnp.int32, (RWKV_HEAD, SCAN_GROUP), 0)
    l = lax.broadcasted_iota(jnp.int32, (RWKV_HEAD, SCAN_GROUP), 1)
    groups = [slice(g * SCAN_GROUP, (g + 1) * SCAN_GROUP) for g in range(D_MODEL // SCAN_GROUP)]
    return e, (l % RWKV_HEAD) == i, groups


def _rwkv_fwd(zl, w, k, a, b, shards, placed):
    t = zl.shape[0]
    nb = t // SCAN_TB
    n = len(shards)
    steps = range(SCAN_TB)

    def body(*refs):
        scan(*refs[:6], *refs[6 + 2 * n:8 + 2 * n], refs[8 + 3 * n])
        gather = (refs[6:6 + n], refs[8 + 2 * n:8 + 3 * n], *refs[9 + 3 * n:])

        @pl.when(pl.program_id(0) == 0)
        def _():
            _gather_start(*gather)

        @pl.when(pl.program_id(0) == nb - 1)
        def _():
            _gather_finish(*gather)

    def scan(r_ref, w_ref, k_ref, v_ref, a_ref, b_ref, y_ref, st_ref, s_ref):
        @pl.when(pl.program_id(0) == 0)
        def _():
            s_ref[...] = jnp.zeros_like(s_ref)

        e, diag, groups = _scan_consts()
        v_cols = [_rows_to_cols([v_ref[i:i + 1, sl] for i in steps], diag, e) for sl in groups]
        s = [s_ref[:, sl] for sl in groups]
        for i in steps:
            for pair in _SCAN_PAIRS:
                sas = _seg_bcast([s[g] * a_ref[i:i + 1, groups[g]] for g in pair], e, one_term=True)
                for g, sa in zip(pair, sas):
                    sl = groups[g]
                    s[g] = s[g] * w_ref[i:i + 1, sl] + sa * b_ref[i:i + 1, sl] + v_cols[g][i] * k_ref[i:i + 1, sl]
                    st_ref[i, :, sl] = s[g]
        for g, sl in enumerate(groups):
            s_ref[:, sl] = s[g]
            y_cols = _seg_bcast([st_ref[i, :, sl] * r_ref[i:i + 1, sl] for i in steps], e, one_term=True)
            for i in steps:
                y_ref[i:i + 1, sl] = _col_to_row(y_cols[i], diag)

    blk = pl.BlockSpec((SCAN_TB, D_MODEL), lambda n: (n, 0))
    v_blk = pl.BlockSpec((SCAN_TB, D_MODEL), lambda n: (n, 2))
    outs = pl.pallas_call(
        body, name="rwkv_fwd", grid=(nb,), in_specs=[blk, blk, blk, v_blk, blk, blk] + [ANY] * (2 * n),
        out_specs=[blk, pl.BlockSpec((SCAN_TB, RWKV_HEAD, D_MODEL), lambda i: (i, 0, 0))] + [ANY] * n,
        out_shape=[jax.ShapeDtypeStruct((t, D_MODEL), F32), jax.ShapeDtypeStruct((t, RWKV_HEAD, D_MODEL), F32)]
        + [jax.ShapeDtypeStruct(p.shape, p.dtype) for p in placed],
        input_output_aliases={6 + n + i: 2 + i for i in range(n)},
        scratch_shapes=[pltpu.VMEM((RWKV_HEAD, D_MODEL), F32), pltpu.SemaphoreType.DMA((n, 6)), pltpu.SemaphoreType.DMA((n, 6))],
        compiler_params=_params(("arbitrary",)),
    )(zl, w, k, zl, a, b, *shards, *placed)
    return outs[0], outs[1], outs[2:]


def _rwkv_bwd(zl, w, k, a, b, states, d_y, parts):
    t = zl.shape[0]
    nb = t // SCAN_TB
    n = len(parts)
    steps = range(SCAN_TB)

    def body(*refs):
        scan(*refs[:9], *refs[9 + n:15 + n], refs[15 + 2 * n])
        exchange = (refs[9:9 + n], refs[15 + n:15 + 2 * n], *refs[16 + 2 * n:])

        @pl.when(pl.program_id(0) == 0)
        def _():
            _rs_chips_start(*exchange)

        @pl.when(pl.program_id(0) == nb - 1)
        def _():
            _rs_chips_finish(*exchange)

    def scan(r_ref, w_ref, k_ref, v_ref, a_ref, b_ref, st_ref, prev_ref, dy_ref,
             dr_ref, dw_ref, dk_ref, dv_ref, da_ref, db_ref, ds_ref):
        @pl.when(pl.program_id(0) == 0)
        def _():
            ds_ref[...] = jnp.zeros_like(ds_ref)

        has_prev = (pl.program_id(0) < nb - 1).astype(F32)
        e, diag, groups = _scan_consts()
        colsum = lambda x: jnp.sum(x, axis=0, keepdims=True)

        def s_prev(i, sl):
            return st_ref[i - 1, :, sl] if i > 0 else prev_ref[0, :, sl] * has_prev

        dy_cols = [_rows_to_cols([dy_ref[i:i + 1, sl] for i in steps], diag, e) for sl in groups]
        v_cols = [_rows_to_cols([v_ref[i:i + 1, sl] for i in steps], diag, e) for sl in groups]
        sa_cols = [_seg_bcast([s_prev(i, sl) * a_ref[i:i + 1, sl] for i in steps], e, one_term=True) for sl in groups]
        ds = [ds_ref[:, sl] for sl in groups]
        dsk = [[None] * SCAN_TB for _ in groups]
        for i in reversed(steps):
            for pair in _SCAN_PAIRS:
                d = {}
                for g in pair:
                    sl = groups[g]
                    d[g] = ds[g] + dy_cols[g][i] * r_ref[i:i + 1, sl]
                    dr_ref[i:i + 1, sl] = colsum(st_ref[i, :, sl] * dy_cols[g][i])
                    dw_ref[i:i + 1, sl] = colsum(d[g] * s_prev(i, sl))
                    db_ref[i:i + 1, sl] = colsum(d[g] * sa_cols[g][i])
                    dk_ref[i:i + 1, sl] = colsum(d[g] * v_cols[g][i])
                    dsk[g][i] = d[g] * k_ref[i:i + 1, sl]
                dsas = _seg_bcast([d[g] * b_ref[i:i + 1, groups[g]] for g in pair], e, one_term=True)
                for g, dsa in zip(pair, dsas):
                    sl = groups[g]
                    da_ref[i:i + 1, sl] = colsum(s_prev(i, sl) * dsa)
                    ds[g] = d[g] * w_ref[i:i + 1, sl] + dsa * a_ref[i:i + 1, sl]
        for g, sl in enumerate(groups):
            ds_ref[:, sl] = ds[g]
            dv_cols = _seg_bcast(dsk[g], e, one_term=True)
            for i in steps:
                dv_ref[i:i + 1, sl] = _col_to_row(dv_cols[i], diag)

    blk = pl.BlockSpec((SCAN_TB, D_MODEL), lambda n: (nb - 1 - n, 0))
    v_blk = pl.BlockSpec((SCAN_TB, D_MODEL), lambda n: (nb - 1 - n, 2))
    out = jax.ShapeDtypeStruct((t, D_MODEL), F32)
    outs = pl.pallas_call(
        body, name="rwkv_bwd", grid=(nb,),
        in_specs=[blk, blk, blk, v_blk, blk, blk] + [
            pl.BlockSpec((SCAN_TB, RWKV_HEAD, D_MODEL), lambda i: (nb - 1 - i, 0, 0)),
            pl.BlockSpec((1, RWKV_HEAD, D_MODEL), lambda i: (jnp.maximum((nb - 1 - i) * SCAN_TB - 1, 0), 0, 0)),
            blk] + [ANY] * n,
        out_specs=[blk] * 6 + [ANY] * n,
        out_shape=[out] * 6 + [jax.ShapeDtypeStruct((3,) + p.shape[1:], p.dtype) for p in parts],
        scratch_shapes=[pltpu.VMEM((RWKV_HEAD, D_MODEL), F32), pltpu.SemaphoreType.DMA((n, 3)), pltpu.SemaphoreType.DMA((n, 3))],
        compiler_params=_params(("arbitrary",)),
    )(zl, w, k, zl, a, b, states, states, d_y, *parts)
    return outs[:6], outs[6:]


def _loss_head(h1, ff, target, g_post):
    def fn(tv, pv):
        a, f, tgt = tv
        h2, vjp = jax.vjp(lambda a_, f_, g_: a_ + _rms(f_, g_), a, f, pv[0])
        err = h2 - tgt
        loss = 0.5 * jnp.sum(jnp.mean(err * err, axis=-1, keepdims=True), axis=0, keepdims=True)
        d_a, d_f, d_g = vjp(err * (1.0 / D_MODEL))
        return [d_a, d_f], [loss, d_g]

    return _tok_call("loss_head", fn, [(h1, D_MODEL, 0), (ff, D_MODEL, 0), (target, D_MODEL, 0)], [g_post],
                     [(D_MODEL, F32), (D_MODEL, BF16)], red_shapes=[(1, 1), (1, D_MODEL)], tm=LIGHT_TM)


def _sum_call(name, terms, rows_per_block=None):
    a0, i0 = terms[0]
    r, c = a0.shape[-2:]
    tr = rows_per_block or r

    def body(*refs):
        acc = refs[0][...].reshape(tr, c)
        for ref in refs[1:-1]:
            acc = acc + ref[...].reshape(tr, c)
        refs[-1][...] = acc

    def spec(arr, idx):
        if arr.ndim == 2:
            return pl.BlockSpec((tr, c), lambda i: (i, 0))
        return pl.BlockSpec((1, tr, c), functools.partial(lambda i, idx: (idx, i, 0), idx=idx))

    return pl.pallas_call(
        body, name=name, grid=(r // tr,), in_specs=[spec(a, i) for a, i in terms],
        out_specs=pl.BlockSpec((tr, c), lambda i: (i, 0)), out_shape=jax.ShapeDtypeStruct((r, c), F32),
        compiler_params=_params(("parallel",)),
    )(*[a for a, _ in terms])


def _adamw_math(w, g, m, v):
    m2 = ADAM_B1 * m + (1.0 - ADAM_B1) * g
    v2 = ADAM_B2 * v + (1.0 - ADAM_B2) * (g * g)
    m_hat = m2 / (1.0 - ADAM_B1 ** ADAM_STEP)
    v_hat = v2 / (1.0 - ADAM_B2 ** ADAM_STEP)
    return -ADAM_LR * (m_hat / (jnp.sqrt(v_hat) + ADAM_EPS) + ADAM_WD * w), m2, v2


def _adamw(name, w, g, m, v, bm, bn, g_transposed=False, exchange=None):
    r, c = w.shape
    grid = (pl.cdiv(r, bm), pl.cdiv(c, bn))

    def body(*refs):
        w_ref, g_ref, m_ref, v_ref = refs[:4]
        go_ref, d_ref, mo_ref, vo_ref = refs[-6:-2] if exchange else refs[4:8]
        if exchange:
            riders = (refs[4], refs[-7], refs[-2], refs[-1])
            first = (pl.program_id(0) == 0) & (pl.program_id(1) == 0)

            @pl.when(first)
            def _():
                _exchange8_start(*riders)

        g = g_ref[...].T if g_transposed else g_ref[...]
        d, m2, v2 = _adamw_math(w_ref[...], g, m_ref[...], v_ref[...])
        go_ref[...] = g
        d_ref[...] = d
        mo_ref[...] = m2
        vo_ref[...] = v2
        if exchange:
            @pl.when((pl.program_id(0) == grid[0] - 1) & (pl.program_id(1) == grid[1] - 1))
            def _():
                _exchange8_finish(*riders)

    blk = pl.BlockSpec((bm, bn), lambda i, j: (i, j))
    g_blk = pl.BlockSpec((bn, bm), lambda i, j: (j, i)) if g_transposed else blk
    out = jax.ShapeDtypeStruct((r, c), F32)
    if not exchange:
        return pl.pallas_call(
            body, name=name, grid=grid, in_specs=[blk, g_blk, blk, blk],
            out_specs=[blk] * 4, out_shape=[out] * 4, compiler_params=_params(("parallel", "parallel")),
        )(w, g, m, v)
    parts, landing = exchange
    outs = pl.pallas_call(
        body, name=name, grid=grid, in_specs=[blk, g_blk, blk, blk, ANY, ANY],
        out_specs=[ANY] + [blk] * 4, out_shape=[jax.ShapeDtypeStruct(landing.shape, landing.dtype)] + [out] * 4,
        input_output_aliases={5: 0}, scratch_shapes=[pltpu.SemaphoreType.DMA((7,)), pltpu.SemaphoreType.DMA((7,))],
        compiler_params=_params(("arbitrary", "arbitrary")),
    )(w, g, m, v, parts, landing)
    return outs[1:], outs[0]


ANY = pl.BlockSpec(memory_space=pl.ANY)


def _place():
    x, y, c = lax.axis_index("x"), lax.axis_index("y"), lax.axis_index("c")
    chips = [(1 - x, y), (x, 1 - y), (1 - x, 1 - y)]
    return x, y, c, chips


def _sibling():
    return (lax.axis_index("x"), lax.axis_index("y"), 1 - lax.axis_index("c"))


def _wait_all(local, remote):
    for cp in local:
        cp.wait()
    for cp in remote:
        cp.wait_send()


def _place_shard(name, shard, place):
    r, cols = shard.shape
    tr = r // 4

    def body(s_ref, in_ref, out_ref):
        out_ref[...] = in_ref[...]

    return pl.pallas_call(
        body, name=name,
        grid_spec=pltpu.PrefetchScalarGridSpec(
            num_scalar_prefetch=1, grid=(4,), in_specs=[pl.BlockSpec((tr, cols), lambda i, s: (i, 0))],
            out_specs=pl.BlockSpec((tr, cols), lambda i, s: (4 * s[1] + i, 0))),
        out_shape=jax.ShapeDtypeStruct((N_SHARD * r, cols), shard.dtype), compiler_params=_params(("arbitrary",)),
    )(place, shard)


def _gather_copies(ins, outs, send_sems, recv_sems):
    x, y, c, chips = _place()
    me, sibling = (x, y, c), _sibling()

    def rows(k, px, py, pc):
        h = ins[k].shape[0] // 2
        return outs[k].at[pl.ds((2 * px + py) * 2 * h + pc * h, h), :]

    def copy(k, j, block, to, src=None):
        return pltpu.make_async_remote_copy(
            src_ref=rows(k, *block) if src is None else src, dst_ref=rows(k, *block),
            send_sem=send_sems.at[k, j], recv_sem=recv_sems.at[k, j], device_id=to, device_id_type=MESH)

    each = [(k, j, chip) for k in range(len(ins)) for j, chip in enumerate(chips)]
    half = lambda k: ins[k].at[pl.ds(c * (ins[k].shape[0] // 2), ins[k].shape[0] // 2), :]
    first = [copy(k, j, me, (*chip, c), src=half(k)) for k, j, chip in each]
    arrive = [copy(k, j, (*chip, c), me) for k, j, chip in each]
    passed = [copy(k, 3 + j, (*chip, c), sibling) for k, j, chip in each]
    landed = [copy(k, 3 + j, (*chip, 1 - c), me) for k, j, chip in each]
    return first, arrive, passed, landed


def _gather_start(ins, outs, send_sems, recv_sems):
    for cp in _gather_copies(ins, outs, send_sems, recv_sems)[0]:
        cp.start()


def _gather_finish(ins, outs, send_sems, recv_sems):
    first, arrive, passed, landed = _gather_copies(ins, outs, send_sems, recv_sems)
    for arrival, forward in zip(arrive, passed):
        arrival.wait_recv()
        forward.start()
    for cp in landed:
        cp.wait_recv()
    _wait_all([], first + passed)


def _gather_shards(shards, placed):
    n = len(shards)

    def body(*refs):
        ins, outs = refs[:n], refs[2 * n:3 * n]
        _gather_start(ins, outs, *refs[3 * n:])
        _gather_finish(ins, outs, *refs[3 * n:])

    return pl.pallas_call(
        body, name="gather_shards", in_specs=[ANY] * (2 * n), out_specs=[ANY] * n,
        out_shape=[jax.ShapeDtypeStruct(a.shape, a.dtype) for a in placed],
        input_output_aliases={n + k: k for k in range(n)},
        scratch_shapes=[pltpu.SemaphoreType.DMA((n, 6)), pltpu.SemaphoreType.DMA((n, 6))],
    )(*shards, *placed)


def _exchange8_copies(in_ref, out_ref, send_sems, recv_sems):
    x, y, c, _ = _place()
    sends, arrivals = [], []
    for rel in range(1, 8):
        dx, dy, dc = rel >> 2 & 1, rel >> 1 & 1, rel & 1
        sends.append(pltpu.make_async_remote_copy(
            src_ref=in_ref.at[2 * (x ^ dx) + (y ^ dy)], dst_ref=out_ref.at[4 * x + 2 * y + c],
            send_sem=send_sems.at[rel - 1], recv_sem=recv_sems.at[rel - 1],
            device_id=(x ^ dx, y ^ dy, c ^ dc), device_id_type=MESH))
        arrivals.append(pltpu.make_async_remote_copy(
            src_ref=in_ref.at[0], dst_ref=out_ref.at[4 * (x ^ dx) + 2 * (y ^ dy) + (c ^ dc)],
            send_sem=send_sems.at[rel - 1], recv_sem=recv_sems.at[rel - 1],
            device_id=(x, y, c), device_id_type=MESH))
    return sends, arrivals


def _exchange8_start(in_ref, out_ref, send_sems, recv_sems):
    for cp in _exchange8_copies(in_ref, out_ref, send_sems, recv_sems)[0]:
        cp.start()


def _exchange8_finish(in_ref, out_ref, send_sems, recv_sems):
    sends, arrivals = _exchange8_copies(in_ref, out_ref, send_sems, recv_sems)
    for cp in arrivals:
        cp.wait_recv()
    _wait_all([], sends)


def _rs_sibling(grads):
    n = len(grads)

    def body(*refs):
        ins, outs = refs[:n], refs[n:2 * n]
        send_sems, recv_sems = refs[2 * n:]
        c = lax.axis_index("c")
        sends = []
        for k in range(n):
            for s in range(N_SHARD):
                cp = pltpu.make_async_remote_copy(
                    src_ref=ins[k].at[2 * s + 1 - c], dst_ref=outs[k].at[s], send_sem=send_sems.at[k, s],
                    recv_sem=recv_sems.at[k, s], device_id=_sibling(), device_id_type=MESH)
                cp.start()
                sends.append(cp)
        for cp in sends:
            cp.wait_recv()
        _wait_all([], sends)

    return pl.pallas_call(
        body, name="rs_sibling", in_specs=[ANY] * n, out_specs=[ANY] * n,
        out_shape=[jax.ShapeDtypeStruct((N_SHARD,) + a.shape[1:], a.dtype) for a in grads],
        scratch_shapes=[pltpu.SemaphoreType.DMA((n, N_SHARD)), pltpu.SemaphoreType.DMA((n, N_SHARD))],
    )(*grads)


def _rs_chips_copies(ins, outs, send_sems, recv_sems):
    x, y, c, chips = _place()
    return [pltpu.make_async_remote_copy(
        src_ref=ins[k].at[2 * px + py], dst_ref=outs[k].at[j], send_sem=send_sems.at[k, j], recv_sem=recv_sems.at[k, j],
        device_id=(px, py, c), device_id_type=MESH) for k in range(len(ins)) for j, (px, py) in enumerate(chips)]


def _rs_chips_start(ins, outs, send_sems, recv_sems):
    for cp in _rs_chips_copies(ins, outs, send_sems, recv_sems):
        cp.start()


def _rs_chips_finish(ins, outs, send_sems, recv_sems):
    sends = _rs_chips_copies(ins, outs, send_sems, recv_sems)
    for cp in sends:
        cp.wait_recv()
    _wait_all([], sends)


def _rs_finish(bufs):
    n = len(bufs)

    def body(*refs):
        outs = refs[n:2 * n]
        send_sems, recv_sems = refs[2 * n:]
        c = lax.axis_index("c")
        sends = []
        for k in range(n):
            cp = pltpu.make_async_remote_copy(
                src_ref=outs[k].at[c], dst_ref=outs[k].at[c], send_sem=send_sems.at[k], recv_sem=recv_sems.at[k],
                device_id=_sibling(), device_id_type=MESH)
            cp.start()
            sends.append(cp)
        for k in range(n):
            pltpu.make_async_remote_copy(
                src_ref=outs[k].at[c], dst_ref=outs[k].at[1 - c], send_sem=send_sems.at[k], recv_sem=recv_sems.at[k],
                device_id=_sibling(), device_id_type=MESH).wait_recv()
        _wait_all([], sends)

    return pl.pallas_call(
        body, name="rs_finish", in_specs=[ANY] * n, out_specs=[ANY] * n,
        out_shape=[jax.ShapeDtypeStruct(a.shape, a.dtype) for a in bufs], input_output_aliases={k: k for k in range(n)},
        scratch_shapes=[pltpu.SemaphoreType.DMA((n,)), pltpu.SemaphoreType.DMA((n,))],
    )(*bufs)


def _sum3d(name, terms, scalars, grid_lead, out_lead, out_index, tr=None, out_dtype=F32):
    h, c = terms[0][0].shape[1:]
    tr = tr or h

    def body(s_ref, *refs):
        acc = refs[0][...].astype(F32)
        for ref in refs[1:-1]:
            acc = acc + ref[...].astype(F32)
        refs[-1][...] = acc.astype(refs[-1].dtype)

    in_specs = [pl.BlockSpec((1, tr, c), functools.partial(lambda l, i, s_ref, f: (f(l, s_ref), i, 0), f=f)) for _, f in terms]
    out_spec = pl.BlockSpec((1, tr, c), lambda l, i, s_ref: (out_index(l, s_ref), i, 0))
    return pl.pallas_call(
        body, name=name,
        grid_spec=pltpu.PrefetchScalarGridSpec(num_scalar_prefetch=1, grid=(grid_lead, h // tr), in_specs=in_specs, out_specs=out_spec),
        out_shape=jax.ShapeDtypeStruct((out_lead, h, c), out_dtype), compiler_params=_params(("arbitrary", "arbitrary")),
    )(scalars, *[a for a, _ in terms])


def _rs_stage1(grads, place, tag):
    g8 = [a.reshape(2 * N_SHARD, a.shape[0] // (2 * N_SHARD), a.shape[1]) for a in grads]
    from_sibling = _rs_sibling(g8)
    parts = [_sum3d(f"rs_add1_{tag}{k}", [(g8[k], lambda l, s: 2 * l + s[0]), (from_sibling[k], lambda l, s: l)], place,
                    N_SHARD, N_SHARD, lambda l, s: l, tr=g8[k].shape[1] // 2, out_dtype=BF16)
             for k in range(len(grads))]
    return g8, from_sibling, parts


def _rs_stage3(stage1, from_chips, place, tag):
    g8, from_sibling, _ = stage1
    mine = [(lambda l, s: 2 * s[1] + s[0]), (lambda l, s: s[1])]
    bufs = [_sum3d(f"rs_add2_{tag}{k}", [(g8[k], mine[0]), (from_sibling[k], mine[1])]
                   + [(from_chips[k], functools.partial(lambda l, s, j: j, j=j)) for j in range(3)],
                   place, 1, 2, lambda l, s: s[0], tr=g8[k].shape[1] // 2)
            for k in range(len(g8))]
    whole = _rs_finish(bufs)
    return [w.reshape(2 * w.shape[1], w.shape[2]) for w in whole]


_LATE = ["w_a", "w_b", "w_out", "w_down", "w_up_t"]


def _device_step(x, target, p, late_shards, late_placed, place):
    t = x.shape[0]
    d = D_MODEL
    tok = lambda arr, c=0, w=d: (arr, w, c)
    f32x = lambda n: [(d, F32)] * n
    rp_params = [p["w0"], p["w2p"], p["a0"], p["a2p"], p["g2p"], p["k_k"], p["k_a"]]
    post_params = [p["ln_w"], p["ln_b"], p["r_k"]]
    g = {}

    (xn,) = _tok_fwd("norm1_fwd", _fn_norm, [tok(x)], [p["g1"]], [(d, BF16)], tm=LIGHT_TM)
    z = _mm("in_proj", xn, p["w_in_t"], "nt", tm=t, tn=256)
    q_in, k_in, kd, qe, dec = _tok_fwd("hgates_fwd", _fn_hgates, [tok(z, 0), tok(z, 1)], [p["lb2"]], f32x(5))
    o_intra, u = _hgrn_local_fwd(q_in, k_in, kd, z)
    o_raw, h_states = _hgrn_state_fwd(o_intra, qe, dec, u)
    zl = _lerp_fwd(z, p["mu"])
    lora = tok(zl, 3 * d // LORA, LORA)
    decay, kr2, avec, bvec, gate = _tok_fwd("rprep_fwd", _fn_rprep, [tok(zl, 1), lora], rp_params, f32x(5))
    y, r_states, late = _rwkv_fwd(zl, decay, kr2, avec, bvec, late_shards, late_placed)
    p = dict(p, **dict(zip(_LATE, late)))
    (o_a,) = _tok_fwd("hpost_fwd", _fn_hpost, [tok(o_raw), tok(z, 3)], [p["gnorm"]], [(d, BF16)])
    post_toks = [tok(y), tok(zl, 0), tok(kr2), tok(zl, 2), tok(gate)]
    (o_b,) = _tok_fwd("rpost_fwd", _fn_rpost, post_toks, post_params, [(d, BF16)])
    y_a = _mm("branch_a", o_a, p["w_a"], "nn")
    y_b = _mm("branch_b", o_b, p["w_b"], "nn")
    merge_toks = [tok(z, C_G // 256, 256), tok(z, (C_G + d) // 256, 256), tok(y_a, 0, 256), tok(y_b, 0, 256)]
    (merged,) = _tok_fwd("merge_fwd", _fn_merge, merge_toks, [], [(256, BF16)], col_grid=4, tm=512)
    mix = _mm("out_proj", merged, p["w_out"], "nn")
    h1, xn2 = _tok_fwd("res1_fwd", _fn_res1, [tok(x), tok(mix)], [p["g_post1"], p["g_pre2"]], [(d, F32), (d, BF16)],
                       tm=LIGHT_TM)
    hu = _mm("up_proj", xn2, p["w_up_t"], "nt", tm=t, tn=512)
    act = _conv_fwd(hu, p["conv_w"], p["conv_b"])
    ff = _mm("down_proj", act, p["w_down"], "nn")
    d_h1, d_ff, loss, g["g_post2"] = _loss_head(h1, ff, target, p["g_post2"])

    d_act = _mm("d_act", d_ff, p["w_down"], "nt")
    g["w_down"] = _mm("dw_down", act, d_ff, "tn", tm=256, tn=1024)
    d_hu, d_cw, d_cb = _conv_bwd(hu, p["conv_w"], p["conv_b"], d_act)
    g["conv_w"], g["conv_b"] = d_cw.transpose(1, 0, 2).reshape(3, 2 * D_FF), d_cb.reshape(1, 2 * D_FF)
    d_hu = d_hu.reshape(2 * t, D_FF)
    d_xn2 = _mm("d_xn2", d_hu, p["w_up_t"], "nn", tm=t, tk=D_FF, mk=(t, 2 * D_FF), a_map=lambda i, j, q: (q, 0))
    g["w_up_t"] = _mm("dw_up", d_hu, xn2, "tn", tm=CONV_TILE, tn=1024, mk=(2 * D_FF, t),
                      a_map=lambda i, j, q: (i // N_CONV_TILES, i % N_CONV_TILES))
    d_x_res, d_mix, g["g_post1"], g["g_pre2"] = _tok_bwd(
        "res1_bwd", _fn_res1, [tok(x), tok(mix)], [p["g_post1"], p["g_pre2"]], [[tok(d_h1)], [tok(d_xn2)]],
        [(d, F32), (d, BF16)], tm=LIGHT_TM)
    d_merged = _mm("d_merged", d_mix, p["w_out"], "nt")
    g["w_out"] = _mm("dw_out", merged, d_mix, "tn")
    d_ga, d_gb, d_ya, d_yb = _tok_bwd("merge_bwd", _fn_merge, merge_toks, [], [[tok(d_merged, 0, 256)]],
                                      [(256, BF16)] * 4, col_grid=4, tm=512)
    d_oa = _mm("d_oa", d_ya, p["w_a"], "nt")
    g["w_a"] = _mm("dw_a", o_a, d_ya, "tn")
    d_ob = _mm("d_ob", d_yb, p["w_b"], "nt")
    g["w_b"] = _mm("dw_b", o_b, d_yb, "tn")
    d_oraw, d_hg, g["gnorm"] = _tok_bwd("hpost_bwd", _fn_hpost, [tok(o_raw), tok(z, 3)], [p["gnorm"]], [[tok(d_oa)]],
                                        [(d, F32), (d, BF16)])
    d_y, d_r1, d_kr2_1, d_v1, d_gate, g["ln_w"], g["ln_b"], g["r_k"] = _tok_bwd(
        "rpost_bwd", _fn_rpost, post_toks, post_params, [[tok(d_ob)]], f32x(5))
    stage1 = _rs_stage1([g[n] for n in _LATE], place, "late")
    (d_r2, d_decay, d_kr2_2, d_v2, d_avec, d_bvec), from_chips = _rwkv_bwd(
        zl, decay, kr2, avec, bvec, r_states, d_y, stage1[2])
    g.update(zip(_LATE, _rs_stage3(stage1, from_chips, place, "late")))
    prep = _tok_bwd("rprep_bwd", _fn_rprep, [tok(zl, 1), lora], rp_params,
                    [[tok(d_decay)], [tok(d_kr2_1), tok(d_kr2_2)], [tok(d_avec)], [tok(d_bvec)], [tok(d_gate)]],
                    [(d, F32), (LORA, F32)])
    d_kr, d_lora = prep[:2]
    g["w0"], g["w2p"], g["a0"], g["a2p"], g["g2p"], g["k_k"], g["k_a"] = prep[2:]
    dz_r, g["mu"] = _lerp_bwd(z, p["mu"], (d_r1, d_r2), d_kr, (d_v1, d_v2), d_lora)
    d_qe, d_dec, d_u = _hgrn_state_bwd(d_oraw, qe, dec, h_states)
    d_q_in, d_k_in, d_kd, d_vi = _hgrn_local_bwd(q_in, k_in, kd, z, d_oraw, d_u)
    d_hq, d_hf, g["lb2"] = _tok_bwd("hgates_bwd", _fn_hgates, [tok(z, 0), tok(z, 1)], [p["lb2"]],
                                    [[tok(d_q_in)], [tok(d_k_in)], [tok(d_kd)], [tok(d_qe)], [tok(d_dec)]], [(d, BF16)] * 2)
    dz = jnp.concatenate([d_hq, d_hf, d_vi.astype(BF16), d_hg, dz_r, d_ga, d_gb], axis=1)
    stage1 = _rs_stage1([_mm("dw_in", dz, xn, "tn", tm=256, tn=1024)], place, "w_in")
    d_xn, from_chips = _mm("d_xn", dz, p["w_in_t"], "nn", tm=1024, tn=512, tk=IN_COLS // 2, riders=stage1[2])
    (g["w_in_t"],) = _rs_stage3(stage1, from_chips, place, "w_in")
    grad_x, g["g1"] = _tok_bwd("norm1_bwd", _fn_norm, [tok(x)], [p["g1"]], [[tok(d_xn)]], [(d, F32)],
                               add_to_first=tok(d_x_res), tm=LIGHT_TM)
    return loss, grad_x, g


_WEIGHTS = ["attn_pre_norm", "w_in", "hgrn_lb", "hgrn_gnorm", "w_branch_a", "rwkv_mu", "rwkv_w0", "rwkv_w2", "rwkv_a0",
            "rwkv_a2", "rwkv_g2", "rwkv_k_k", "rwkv_k_a", "rwkv_r_k", "rwkv_ln_w", "rwkv_ln_b", "w_branch_b", "w_out",
            "attn_post_norm", "ffn_pre_norm", "w_up", "conv_w", "conv_b", "w_down", "ffn_post_norm"]
_REPLICATED = [("attn_pre_norm", "g1"), ("hgrn_lb", "lb2"), ("hgrn_gnorm", "gnorm"), ("rwkv_mu", "mu"), ("rwkv_w0", "w0"),
               ("rwkv_a0", "a0"), ("rwkv_k_k", "k_k"), ("rwkv_k_a", "k_a"), ("rwkv_r_k", "r_k"), ("rwkv_ln_w", "ln_w"),
               ("rwkv_ln_b", "ln_b"), ("attn_post_norm", "g_post1"), ("ffn_pre_norm", "g_pre2"), ("conv_b", "conv_b"),
               ("ffn_post_norm", "g_post2")]
SLAB_COLS = 1024


def _pack(arrays):
    pieces, total = [], 0
    for a in arrays:
        flat = a.reshape(-1)
        rows = -(-flat.shape[0] // SLAB_COLS)
        pieces.append(jnp.pad(flat, (0, rows * SLAB_COLS - flat.shape[0])).reshape(rows, SLAB_COLS))
        total += rows
    if total % 8:
        pieces.append(jnp.zeros((8 - total % 8, SLAB_COLS), F32))
    return jnp.concatenate(pieces, axis=0)


def _unpack(slab, shapes):
    out, at = [], 0
    for s in shapes:
        size = 1
        for dim in s:
            size *= dim
        rows = -(-size // SLAB_COLS)
        out.append(slab[at:at + rows].reshape(-1)[:size].reshape(s))
        at += rows
    return out


def kernel(x, attn_pre_norm, w_in, hgrn_lb, hgrn_gnorm, w_branch_a, rwkv_mu, rwkv_w0, rwkv_w2, rwkv_a0, rwkv_a2, rwkv_g2, rwkv_k_k, rwkv_k_a, rwkv_r_k, rwkv_ln_w, rwkv_ln_b, w_branch_b, w_out, attn_post_norm, ffn_pre_norm, w_up, conv_w, conv_b, w_down, ffn_post_norm, loss_target, m_attn_pre_norm, m_w_in, m_hgrn_lb, m_hgrn_gnorm, m_w_branch_a, m_rwkv_mu, m_rwkv_w0, m_rwkv_w2, m_rwkv_a0, m_rwkv_a2, m_rwkv_g2, m_rwkv_k_k, m_rwkv_k_a, m_rwkv_r_k, m_rwkv_ln_w, m_rwkv_ln_b, m_w_branch_b, m_w_out, m_attn_post_norm, m_ffn_pre_norm, m_w_up, m_conv_w, m_conv_b, m_w_down, m_ffn_post_norm, v_attn_pre_norm, v_w_in, v_hgrn_lb, v_hgrn_gnorm, v_w_branch_a, v_rwkv_mu, v_rwkv_w0, v_rwkv_w2, v_rwkv_a0, v_rwkv_a2, v_rwkv_g2, v_rwkv_k_k, v_rwkv_k_a, v_rwkv_r_k, v_rwkv_ln_w, v_rwkv_ln_b, v_w_branch_b, v_w_out, v_attn_post_norm, v_ffn_pre_norm, v_w_up, v_conv_w, v_conv_b, v_w_down, v_ffn_post_norm):
    given = dict(locals())
    w = {n: given[n] for n in _WEIGHTS}
    mom = {n: given["m_" + n] for n in _WEIGHTS}
    var = {n: given["v_" + n] for n in _WEIGHTS}
    shard = 2 * lax.axis_index("x") + lax.axis_index("y")
    place = jnp.stack([lax.axis_index("c"), shard]).astype(jnp.int32)
    row = lambda a: a.reshape(1, -1)
    lora_of = lambda d: jnp.concatenate([d["rwkv_w2"][0], d["rwkv_a2"][0], d["rwkv_g2"][0]], axis=0)

    shards = [w["w_in"][0].T.astype(BF16), lora_of(w), jnp.pad(w["conv_w"][0], ((0, 29), (0, 0)))]
    late_shards = [w["w_branch_a"][0].astype(BF16), w["w_branch_b"][0].astype(BF16), w["w_out"][0].astype(BF16),
                   w["w_down"][0].astype(BF16), w["w_up"][0].T.astype(BF16)]
    placed = [_place_shard(f"place_{k}", a, place) for k, a in enumerate(shards)]
    late_placed = [_place_shard(f"place_late_{k}", a, place) for k, a in enumerate(late_shards)]
    w_in_t, lora_g, conv_g = _gather_shards(shards, placed)
    lora_full = lora_g.reshape(N_SHARD, LORA, 256).transpose(1, 0, 2).reshape(LORA, D_MODEL)
    conv_full = conv_g.reshape(N_SHARD, 32, 2 * D_FF // N_SHARD)[:, :3].transpose(1, 0, 2).reshape(3, 2 * D_FF)
    lrow = lax.broadcasted_iota(jnp.int32, (LORA, 1), 0)
    p = {
        "g1": row(w["attn_pre_norm"]), "lb2": w["hgrn_lb"], "gnorm": row(w["hgrn_gnorm"]), "w_in_t": w_in_t,
        "mu": row(w["rwkv_mu"]), "w0": row(w["rwkv_w0"]), "a0": row(w["rwkv_a0"]),
        "w2p": jnp.where(lrow < 64, lora_full, 0.0), "a2p": jnp.where((lrow >= 64) & (lrow < 128), lora_full, 0.0),
        "g2p": jnp.where(lrow >= 128, lora_full, 0.0),
        "k_k": row(w["rwkv_k_k"]), "k_a": row(w["rwkv_k_a"]), "r_k": row(w["rwkv_r_k"]), "ln_w": row(w["rwkv_ln_w"]),
        "ln_b": row(w["rwkv_ln_b"]), "g_post1": row(w["attn_post_norm"]),
        "g_pre2": row(w["ffn_pre_norm"]), "conv_w": conv_full, "conv_b": row(w["conv_b"]),
        "g_post2": row(w["ffn_post_norm"]),
    }

    loss, grad_x, g = _device_step(x[0], loss_target[0], p, late_shards, late_placed, place)

    g_in_t = g["w_in_t"]
    g_a, g_b, g_o, g_dn, g_up_t = [g[n] for n in _LATE]
    rep_shapes = [w[n].shape for n, _ in _REPLICATED]
    rep = _pack([g[key] for _, key in _REPLICATED])
    n_rep_rows = rep.shape[0]
    cw = 2 * D_FF // N_SHARD
    lora_rows, conv_rows = LORA * 256 // SLAB_COLS, -(-3 * cw // SLAB_COLS)
    lora_g = jnp.concatenate([g["w2p"][0:64], g["a2p"][64:128], g["g2p"][128:256]], axis=0)
    lora_parts = lora_g.reshape(LORA, N_SHARD, 256).transpose(1, 0, 2).reshape(N_SHARD, lora_rows, SLAB_COLS)
    conv_parts = g["conv_w"].reshape(3, N_SHARD, cw).transpose(1, 0, 2).reshape(N_SHARD, 3 * cw)
    conv_parts = jnp.pad(conv_parts, ((0, 0), (0, conv_rows * SLAB_COLS - 3 * cw))).reshape(N_SHARD, conv_rows, SLAB_COLS)
    n_rows = n_rep_rows + lora_rows + conv_rows
    fill = jnp.zeros((N_SHARD, -n_rows % 8, SLAB_COLS), F32)
    parts = jnp.concatenate([jnp.broadcast_to(rep, (N_SHARD,) + rep.shape), lora_parts, conv_parts, fill], axis=1)
    me = 4 * lax.axis_index("x") + 2 * lax.axis_index("y") + lax.axis_index("c")
    landing = lax.dynamic_update_slice(jnp.zeros((8,) + parts.shape[1:], F32),
                                       lax.dynamic_index_in_dim(parts, shard, 0, keepdims=True), (me, 0, 0))

    res = {}

    def put(name, outs, shape=None):
        res[name] = [o.reshape(w[name].shape if shape is None else shape) for o in outs]

    w_in_out, gathered = _adamw("adamw_w_in", w["w_in"][0], g_in_t, mom["w_in"][0], var["w_in"][0], 1024, 128,
                                g_transposed=True, exchange=(parts, landing))
    put("w_in", w_in_out)
    summed = _sum3d("small_sum", [(gathered, functools.partial(lambda l, s, i: i, i=i)) for i in range(8)], place, 1, 1,
                    lambda l, s: 0)[0]
    lora_grad = summed[n_rep_rows:n_rep_rows + lora_rows].reshape(LORA, 256)
    conv_grad = summed[n_rep_rows + lora_rows:n_rows].reshape(-1)[:3 * cw].reshape(3, cw)
    put("w_up", _adamw("adamw_w_up", w["w_up"][0], g_up_t, mom["w_up"][0], var["w_up"][0], 1024, 128, g_transposed=True))
    for name, grad in (("w_branch_a", g_a), ("w_branch_b", g_b), ("w_out", g_o)):
        put(name, _adamw("adamw_" + name, w[name][0], grad, mom[name][0], var[name][0], 256, 1024))
    put("w_down", _adamw("adamw_w_down", w["w_down"][0], g_dn, mom["w_down"][0], var["w_down"][0], 176, 1024))
    put("conv_w", _adamw("adamw_conv_w", w["conv_w"][0], conv_grad, mom["conv_w"][0], var["conv_w"][0], 3, 2 * D_FF // N_SHARD))
    lora_out = _adamw("adamw_lora", lora_of(w), lora_grad, lora_of(mom), lora_of(var), LORA, 256)
    for name, lo, hi in (("rwkv_w2", 0, 64), ("rwkv_a2", 64, 128), ("rwkv_g2", 128, 256)):
        put(name, [o[lo:hi] for o in lora_out])
    rep_names = [n for n, _ in _REPLICATED]
    rep_out = _adamw("adamw_small", _pack([w[n] for n in rep_names]), summed[:n_rep_rows], _pack([mom[n] for n in rep_names]),
                     _pack([var[n] for n in rep_names]), n_rep_rows, SLAB_COLS)
    for name, parts in zip(rep_names, zip(*[_unpack(o, rep_shapes) for o in rep_out])):
        put(name, list(parts))

    loss = lax.psum(loss[0, 0], ("x", "y", "c"))
    return (loss, grad_x[None], *[res[n][0] for n in _WEIGHTS], *[res[n][1] for n in _WEIGHTS],
            *[res[n][2] for n in _WEIGHTS], *[res[n][3] for n in _WEIGHTS])
```

```python
import functools

import jax
import jax.numpy as jnp
from jax import lax
from jax.experimental import pallas as pl
from jax.experimental.pallas import tpu as pltpu

F32, BF16 = jnp.float32, jnp.bfloat16
MESH = pl.DeviceIdType.MESH

D_MODEL = 1024
HGRN_HEADS = 8
HGRN_K = 128
HGRN_SCALE = HGRN_K ** -0.5
CHUNK = 32
RWKV_HEAD = 64
LORA = 256
D_FF = 2816
EPS = 1e-6
GN_EPS = 1e-5 * RWKV_HEAD
N_SHARD = 4
ADAM_LR, ADAM_B1, ADAM_B2, ADAM_EPS, ADAM_WD, ADAM_STEP = 0.001, 0.9, 0.999, 1e-08, 0.01, 10

LANES = 128
VMEM_LIMIT = 56 * 1024 * 1024
SCAN_TB = 16
SCAN_GROUP = 256
LIGHT_TM = 256

C_HQ, C_HF, C_HI, C_HG = 0, 1024, 2048, 3072
C_R = 4096
R_COLS = 3328
C_G = 7424
IN_COLS = 9472


def _params(sem=None, **kw):
    return pltpu.CompilerParams(dimension_semantics=sem, vmem_limit_bytes=VMEM_LIMIT, **kw)


def _seg_matrix(n, seg):
    r = lax.broadcasted_iota(jnp.int32, (n, n), 0) // seg
    c = lax.broadcasted_iota(jnp.int32, (n, n), 1) // seg
    return (r == c).astype(BF16)


def _split3(x):
    hi = x.astype(BF16)
    r1 = x - hi.astype(F32)
    mid = r1.astype(BF16)
    lo = (r1 - mid.astype(F32)).astype(BF16)
    return hi, mid, lo


def _segsum_impl(x, seg):
    e = _seg_matrix(LANES, seg)
    outs = []
    for g in range(x.shape[1] // LANES):
        hi, mid, lo = _split3(x[:, g * LANES:(g + 1) * LANES])
        outs.append(jnp.dot(hi, e, preferred_element_type=F32) + jnp.dot(mid, e, preferred_element_type=F32)
                    + jnp.dot(lo, e, preferred_element_type=F32))
    return outs[0] if len(outs) == 1 else jnp.concatenate(outs, axis=1)


def _make_segsum(seg):
    @jax.custom_vjp
    def f(x):
        return _segsum_impl(x, seg)

    f.defvjp(lambda x: (_segsum_impl(x, seg), None), lambda _, ct: (_segsum_impl(ct, seg),))
    return f


_segsum64 = _make_segsum(RWKV_HEAD)
_segsum128 = _make_segsum(HGRN_K)


def _chunk_mm_impl(x, kind, transposed):
    n = x.shape[0]
    r = lax.broadcasted_iota(jnp.int32, (n, n), 1 if transposed else 0)
    c = lax.broadcasted_iota(jnp.int32, (n, n), 0 if transposed else 1)
    same = (r // CHUNK) == (c // CHUNK)
    if kind == "cumsum":
        m = same & (r >= c)
    else:
        m = same & (c % CHUNK == (CHUNK // 2 - 1 if kind == "mid" else CHUNK - 1))
    m = m.astype(BF16)
    hi, mid, lo = _split3(x)
    return (jnp.dot(m, hi, preferred_element_type=F32) + jnp.dot(m, mid, preferred_element_type=F32)
            + jnp.dot(m, lo, preferred_element_type=F32))


def _make_chunk_mm(kind):
    @jax.custom_vjp
    def f(x):
        return _chunk_mm_impl(x, kind, False)

    f.defvjp(lambda x: (_chunk_mm_impl(x, kind, False), None), lambda _, ct: (_chunk_mm_impl(ct, kind, True),))
    return f


_chunk_cumsum = _make_chunk_mm("cumsum")
_chunk_mid = _make_chunk_mm("mid")
_chunk_last = _make_chunk_mm("last")


@jax.custom_vjp
def _bdot(x, w):
    return jnp.dot(x.astype(BF16), w.astype(BF16), preferred_element_type=F32)


def _bdot_fwd(x, w):
    return _bdot(x, w), (x, w)


def _bdot_bwd(res, ct):
    x, w = res
    ctb = ct.astype(BF16)
    dx = lax.dot_general(ctb, w.astype(BF16), (((1,), (1,)), ((), ())), preferred_element_type=F32)
    dw = lax.dot_general(x.astype(BF16), ctb, (((0,), (0,)), ((), ())), preferred_element_type=F32)
    return dx, dw


_bdot.defvjp(_bdot_fwd, _bdot_bwd)


def _sigmoid(x):
    return 1.0 / (1.0 + jnp.exp(-x))


def _silu(x):
    return x * _sigmoid(x)


def _softplus(x):
    return jnp.maximum(x, 0.0) + jnp.log(1.0 + jnp.exp(-jnp.abs(x)))


def _rms(x, g):
    return x * lax.rsqrt(jnp.mean(x * x, axis=-1, keepdims=True) + EPS) * g


def _fn_norm(t, p):
    return [_rms(t[0], p[0])]


def _fn_hgates(t, p):
    hq, hf = t
    lb2 = p[0]
    m = jnp.max(lb2, axis=0, keepdims=True)
    e = jnp.exp(lb2 - m)
    first = lax.broadcasted_iota(jnp.int32, e.shape, 0) == 0
    lb = jnp.sum(jnp.where(first, e, 0.0), axis=0, keepdims=True) / jnp.sum(e, axis=0, keepdims=True)
    f = lb + (1.0 - lb) * _sigmoid(hf)
    q, k = _silu(hq) * HGRN_SCALE, 1.0 - f
    b = _chunk_cumsum(jnp.log(f))
    b_ref, b_last = _chunk_mid(b), _chunk_last(b)
    return [q * jnp.exp(b - b_ref), k * jnp.exp(b_ref - b), k * jnp.exp(b_last - b), q * jnp.exp(b), jnp.exp(b_last)]


def _fn_hpost(t, p):
    o, hg = t
    ms = _segsum128(o * o) * (1.0 / HGRN_K)
    return [o * lax.rsqrt(ms + EPS) * p[0] * _silu(hg)]


def _fn_rprep(t, p):
    kr, lora = t
    w0, w2p, a0, a2p, g2p, k_k, k_a = p
    pre_w = w0 + _bdot(jnp.tanh(lora), w2p)
    w_log = -_softplus(-pre_w) - 0.5
    decay = jnp.exp(-jnp.exp(w_log))
    a = _sigmoid(a0 + _bdot(lora, a2p))
    g = _bdot(_sigmoid(lora), g2p)
    kk = kr * k_k
    kk = kk / jnp.maximum(jnp.sqrt(_segsum64(kk * kk)), 1e-12)
    kr2 = kr * (1.0 + (a - 1.0) * k_a)
    return [decay, kr2, -kk, kk * a, g]


def _fn_rpost(t, p):
    y, r, kr2, v, g = t
    ln_w, ln_b, r_k = p
    mu = _segsum64(y) * (1.0 / RWKV_HEAD)
    yc = y - mu
    var = _segsum64(yc * yc) * (1.0 / RWKV_HEAD)
    yn = yc * lax.rsqrt(var + GN_EPS) * ln_w + ln_b
    bonus = _segsum64(r * kr2 * r_k) * v
    return [(yn + bonus) * g]


def _fn_merge(t, p):
    ga, gb, ya, yb = t
    return [_sigmoid(ga) * ya + _sigmoid(gb) * yb]


def _fn_res1(t, p):
    x, mix = t
    h1 = x + _rms(mix, p[0])
    return [h1, _rms(h1, p[1])]


def _tok_call(name, fn, toks, params, outs, red_shapes=(), tm=128, col_grid=1, sibling_rider=None):
    n_t, n_p, n_o, n_red = len(toks), len(params), len(outs), len(red_shapes)
    n_r = 0 if sibling_rider is None else len(sibling_rider)
    t_len = toks[0][0].shape[0]
    tm = min(tm, t_len)
    grid = (t_len // tm, col_grid)

    def body(*refs):
        first_out = n_t + n_p + n_r
        if n_r:
            rider = (refs[n_t + n_p:first_out], refs[first_out + n_o + n_red:first_out + n_o + n_red + n_r], *refs[-2:])

            @pl.when((pl.program_id(0) == 0) & (pl.program_id(1) == 0))
            def _():
                _rs_sibling_start(*rider)

        tv = [r[...].astype(F32) for r in refs[:n_t]]
        pv = [r[...] for r in refs[n_t:n_t + n_p]]
        o, red = fn(tv, pv)
        for ref, val in zip(refs[first_out:first_out + n_o], o):
            ref[...] = val.astype(ref.dtype)
        if n_r:
            @pl.when((pl.program_id(0) == grid[0] - 1) & (pl.program_id(1) == grid[1] - 1))
            def _():
                _rs_sibling_finish(*rider)

        red_refs = refs[first_out + n_o:first_out + n_o + n_red]
        if red_refs:
            first = pl.program_id(0) == 0

            @pl.when(first)
            def _():
                for ref, val in zip(red_refs, red):
                    ref[...] = val

            @pl.when(jnp.logical_not(first))
            def _():
                for ref, val in zip(red_refs, red):
                    ref[...] += val

    in_specs = [pl.BlockSpec((tm, w), functools.partial(lambda i, j, c: (i, c + j), c=c)) for (_, w, c) in toks]
    in_specs += [pl.BlockSpec(p.shape, lambda i, j: (0, 0)) for p in params]
    out_specs = [pl.BlockSpec((tm, w), lambda i, j: (i, j)) for (w, _) in outs]
    out_specs += [pl.BlockSpec(s, lambda i, j: (0, 0)) for s in red_shapes]
    out_shape = [jax.ShapeDtypeStruct((t_len, w * col_grid), dt) for (w, dt) in outs]
    out_shape += [jax.ShapeDtypeStruct(s, F32) for s in red_shapes]
    scratch = []
    if n_r:
        in_specs += [ANY] * n_r
        out_specs += [ANY] * n_r
        out_shape += [jax.ShapeDtypeStruct((N_SHARD,) + a.shape[1:], a.dtype) for a in sibling_rider]
        scratch = [pltpu.SemaphoreType.DMA((n_r, N_SHARD)), pltpu.SemaphoreType.DMA((n_r, N_SHARD))]
    return pl.pallas_call(
        body, name=name, grid=grid, in_specs=in_specs, out_specs=out_specs, out_shape=out_shape, scratch_shapes=scratch,
        compiler_params=_params(("arbitrary", "arbitrary")),
    )(*[a for (a, _, _) in toks], *params, *(sibling_rider or []))


def _tok_fwd(name, fn, toks, params, outs, **kw):
    return _tok_call(name, lambda tv, pv: (fn(tv, pv), []), toks, params, outs, **kw)


def _tok_bwd(name, fn, toks, params, cts, want, add_to_first=None, **kw):
    n_t = len(toks)
    flat = [c for group in cts for c in group]
    extra = [] if add_to_first is None else [add_to_first]

    def bwd(tv, pv):
        prim, rest = tv[:n_t], tv[n_t:]
        ct, at = [], 0
        for group in cts:
            ct.append(functools.reduce(lambda u, v: u + v, rest[at:at + len(group)]))
            at += len(group)
        _, vjp = jax.vjp(lambda *a: tuple(fn(list(a[:n_t]), list(a[n_t:]))), *prim, *pv)
        g = vjp(tuple(ct))
        tok_grads = [g[i] for i in range(n_t) if want[i] is not None]
        if extra:
            tok_grads[0] = tok_grads[0] + rest[at]
        return tok_grads, list(g[n_t:])

    return _tok_call(name, bwd, list(toks) + flat + extra, params, [w for w in want if w is not None],
                     red_shapes=[p.shape for p in params], **kw)


def _mm(name, a, b, mode, out_dtype=F32, tm=None, tn=None, tk=None, riders=None, a_map=None, mk=None):
    if mode == "nn":
        (m, k), (_, n) = a.shape, b.shape
    elif mode == "nt":
        (m, k), (n, _) = a.shape, b.shape
    else:
        (k, m), (_, n) = a.shape, b.shape
    if mk is not None:
        m, k = mk
    tm = (512 if mode == "tn" else 2048) if tm is None else tm
    tn = (512 if mode == "tn" else 256) if tn is None else tn
    tk = k if tk is None else tk
    tm, tn = min(tm, m), min(tn, n)
    nk = k // tk
    assert m % tm == 0 and n % tn == 0 and k % tk == 0, (name, a.shape, b.shape, tm, tn, tk)
    a_spec = pl.BlockSpec((tk, tm), lambda i, j, q: (q, i)) if mode == "tn" else pl.BlockSpec((tm, tk), lambda i, j, q: (i, q))
    if a_map is not None:
        a_spec = pl.BlockSpec(a_spec.block_shape, a_map)
    b_spec = pl.BlockSpec((tn, tk), lambda i, j, q: (j, q)) if mode == "nt" else pl.BlockSpec((tk, tn), lambda i, j, q: (q, j))
    dn = {"nn": (((1,), (0,)), ((), ())), "nt": (((1,), (1,)), ((), ())), "tn": (((0,), (0,)), ((), ()))}[mode]
    grid = (m // tm, n // tn, nk)
    nr = 0 if riders is None else len(riders)

    def body(*refs):
        a_ref, b_ref, o_ref = refs[0], refs[1], refs[2 + nr]
        acc = refs[3 + 2 * nr] if nk > 1 else None
        if nr:
            exchange = (refs[2:2 + nr], refs[3 + nr:3 + 2 * nr], *refs[-2:])
            at = [pl.program_id(ax) for ax in range(3)]

            @pl.when((at[0] == 0) & (at[1] == 0) & (at[2] == 0))
            def _():
                _rs_chips_start(*exchange)

        p = lax.dot_general(a_ref[...], b_ref[...], dn, preferred_element_type=F32)
        if nk == 1:
            o_ref[...] = p.astype(o_ref.dtype)
        else:
            q = pl.program_id(2)

            @pl.when(q == 0)
            def _():
                acc[...] = p

            @pl.when(q > 0)
            def _():
                acc[...] += p

            @pl.when(q == nk - 1)
            def _():
                o_ref[...] = acc[...].astype(o_ref.dtype)

        if nr:
            @pl.when((at[0] == grid[0] - 1) & (at[1] == grid[1] - 1) & (at[2] == grid[2] - 1))
            def _():
                _rs_chips_finish(*exchange)

    scratch = [pltpu.VMEM((tm, tn), F32)] if nk > 1 else []
    out_specs = [pl.BlockSpec((tm, tn), lambda i, j, q: (i, j))]
    out_shape = [jax.ShapeDtypeStruct((m, n), out_dtype)]
    if nr:
        scratch += [pltpu.SemaphoreType.DMA((nr, 3)), pltpu.SemaphoreType.DMA((nr, 3))]
        out_specs += [ANY] * nr
        out_shape += [jax.ShapeDtypeStruct((3,) + r.shape[1:], r.dtype) for r in riders]
    outs = pl.pallas_call(
        body, name=name, grid=grid, in_specs=[a_spec, b_spec] + [ANY] * nr, out_specs=out_specs, out_shape=out_shape,
        scratch_shapes=scratch,
        compiler_params=_params(("arbitrary",) * 3 if nr else ("parallel", "parallel", "arbitrary")),
    )(a, b, *(riders or []))
    return (outs[0], outs[1:]) if nr else outs[0]


def _shift_down(z, n):
    rows = lax.broadcasted_iota(jnp.int32, z.shape, 0)
    return jnp.where(rows < n, 0.0, pltpu.roll(z, n, 0))


def _shift_up(z, n):
    t = z.shape[0]
    rows = lax.broadcasted_iota(jnp.int32, z.shape, 0)
    return jnp.where(rows >= t - n, 0.0, pltpu.roll(z, t - n, 0))


def _lerp_fwd(z, mu):
    t = z.shape[0]
    w = 256

    def body(z_ref, mu_ref, o_ref):
        zz = z_ref[...]
        o_ref[...] = zz + mu_ref[...] * (_shift_down(zz, 1) - zz)

    return pl.pallas_call(
        body, name="lerp_fwd", grid=(R_COLS // w,),
        in_specs=[pl.BlockSpec((t, w), lambda j: (0, C_R // w + j)), pl.BlockSpec((1, w), lambda j: (0, j))],
        out_specs=pl.BlockSpec((t, w), lambda j: (0, j)), out_shape=jax.ShapeDtypeStruct((t, R_COLS), F32),
        compiler_params=_params(("parallel",)),
    )(z, mu)


def _lerp_bwd(z, mu, d_r, d_k, d_v, d_lora):
    t = z.shape[0]
    w = 256
    per = D_MODEL // w

    def body(z_ref, mu_ref, r1_ref, r2_ref, k_ref, v1_ref, v2_ref, l_ref, dz_ref, dmu_ref):
        j = pl.program_id(0)
        zz, m = z_ref[...], mu_ref[...]
        d = jnp.where(j < per, r1_ref[...] + r2_ref[...],
                      jnp.where(j < 2 * per, k_ref[...], jnp.where(j < 3 * per, v1_ref[...] + v2_ref[...], l_ref[...])))
        dz_ref[...] = (d * (1.0 - m) + _shift_up(d * m, 1)).astype(dz_ref.dtype)
        dmu_ref[...] = jnp.sum(d * (_shift_down(zz, 1) - zz), axis=0, keepdims=True)

    piece = lambda first: pl.BlockSpec((t, w), lambda j: (0, jnp.clip(j - first, 0, per - 1)))
    return pl.pallas_call(
        body, name="lerp_bwd", grid=(R_COLS // w,),
        in_specs=[pl.BlockSpec((t, w), lambda j: (0, C_R // w + j)), pl.BlockSpec((1, w), lambda j: (0, j)),
                  piece(0), piece(0), piece(per), piece(2 * per), piece(2 * per), pl.BlockSpec((t, w), lambda j: (0, 0))],
        out_specs=[pl.BlockSpec((t, w), lambda j: (0, j)), pl.BlockSpec((1, w), lambda j: (0, j))],
        out_shape=[jax.ShapeDtypeStruct((t, R_COLS), BF16), jax.ShapeDtypeStruct((1, R_COLS), F32)],
        compiler_params=_params(("arbitrary",)),
    )(z, mu, *d_r, d_k, *d_v, d_lora)


CONV_TILE = 256
N_CONV_TILES = D_FF // CONV_TILE


def _conv(h, w, b):
    return b + w[0:1, :] * _shift_down(h, 2) + w[1:2, :] * _shift_down(h, 1) + w[2:3, :] * h


def _conv_fwd(hu, conv_w, conv_b):
    t = hu.shape[0]
    n = N_CONV_TILES

    def body(hg_ref, hv_ref, wg_ref, wv_ref, bg_ref, bv_ref, o_ref):
        gate = _conv(hg_ref[...], wg_ref[...], bg_ref[...])
        val = _conv(hv_ref[...], wv_ref[...], bv_ref[...])
        o_ref[...] = (_silu(gate) * val).astype(o_ref.dtype)

    col = lambda off: pl.BlockSpec((t, CONV_TILE), lambda j: (0, j + off))
    wspec = lambda off: pl.BlockSpec((3, CONV_TILE), lambda j: (0, j + off))
    bspec = lambda off: pl.BlockSpec((1, CONV_TILE), lambda j: (0, j + off))
    return pl.pallas_call(
        body, name="conv_fwd", grid=(n,),
        in_specs=[col(0), col(n), wspec(0), wspec(n), bspec(0), bspec(n)],
        out_specs=pl.BlockSpec((t, CONV_TILE), lambda j: (0, j)), out_shape=jax.ShapeDtypeStruct((t, D_FF), BF16),
        compiler_params=_params(("parallel",)),
    )(hu, hu, conv_w, conv_w, conv_b, conv_b)


def _conv_bwd(hu, conv_w, conv_b, d_act):
    t = hu.shape[0]
    n = N_CONV_TILES

    def body(hg_ref, hv_ref, wg_ref, wv_ref, bg_ref, bv_ref, d_ref, dh_ref, dw_ref, db_ref):
        hg, hv, wg, wv = hg_ref[...], hv_ref[...], wg_ref[...], wv_ref[...]
        gate = _conv(hg, wg, bg_ref[...])
        val = _conv(hv, wv, bv_ref[...])
        d = d_ref[...]
        sg = _sigmoid(gate)
        d_gate = d * val * (sg * (1.0 + gate * (1.0 - sg)))
        d_val = d * (gate * sg)
        for half, (dc, h, w) in enumerate(((d_gate, hg, wg), (d_val, hv, wv))):
            dh = w[2:3, :] * dc + w[1:2, :] * _shift_up(dc, 1) + w[0:1, :] * _shift_up(dc, 2)
            dh_ref[half] = dh.astype(dh_ref.dtype)
            dw_ref[half, 0:1, :] = jnp.sum(dc * _shift_down(h, 2), axis=0, keepdims=True)
            dw_ref[half, 1:2, :] = jnp.sum(dc * _shift_down(h, 1), axis=0, keepdims=True)
            dw_ref[half, 2:3, :] = jnp.sum(dc * h, axis=0, keepdims=True)
            db_ref[half] = jnp.sum(dc, axis=0, keepdims=True)

    gcol = lambda rows: pl.BlockSpec((rows, CONV_TILE), lambda j: (0, j))
    vcol = lambda rows: pl.BlockSpec((rows, CONV_TILE), lambda j: (0, j + n))
    both = lambda rows: pl.BlockSpec((2, rows, CONV_TILE), lambda j: (0, 0, j))
    return pl.pallas_call(
        body, name="conv_bwd", grid=(n,),
        in_specs=[gcol(t), vcol(t), gcol(3), vcol(3), gcol(1), vcol(1), gcol(t)],
        out_specs=[both(t), both(3), both(1)],
        out_shape=[jax.ShapeDtypeStruct((2, t, D_FF), BF16), jax.ShapeDtypeStruct((2, 3, D_FF), F32),
                   jax.ShapeDtypeStruct((2, 1, D_FF), F32)],
        compiler_params=_params(("parallel",)),
    )(hu, hu, conv_w, conv_w, conv_b, conv_b, d_act)


_NN = (((1,), (0,)), ((), ()))
_NT = (((1,), (1,)), ((), ()))
_TN = (((0,), (0,)), ((), ()))
HGRN_CB = 8
HGRN_LOCAL_CB = 16


def _bf_dot(a, b, dn):
    return lax.dot_general(a.astype(BF16), b.astype(BF16), dn, preferred_element_type=F32)


def _tril():
    n = HGRN_LOCAL_CB * CHUNK
    r = lax.broadcasted_iota(jnp.int32, (n, n), 0)
    c = lax.broadcasted_iota(jnp.int32, (n, n), 1)
    return (r // CHUNK == c // CHUNK) & (r >= c)


def _hgrn_specs(t):
    rows = HGRN_LOCAL_CB * CHUNK
    head = pl.BlockSpec((rows, HGRN_K), lambda h, n: (n, h))
    v_head = pl.BlockSpec((rows, HGRN_K), lambda h, n: (n, C_HI // HGRN_K + h))
    mats = pl.BlockSpec((1, HGRN_LOCAL_CB, HGRN_K, HGRN_K), lambda h, n: (h, n, 0, 0))
    return head, v_head, mats, (HGRN_HEADS, t // rows)


def _hgrn_local_fwd(q_in, k_in, kd, z):
    t = q_in.shape[0]
    head, v_head, mats, grid = _hgrn_specs(t)

    def body(q_ref, k_ref, kd_ref, v_ref, o_ref, u_ref):
        v = v_ref[...]
        scores = jnp.where(_tril(), _bf_dot(q_ref[...], k_ref[...], _NT), 0.0)
        o_ref[...] = _bf_dot(scores, v, _NN)
        for n in range(HGRN_LOCAL_CB):
            rows = slice(n * CHUNK, (n + 1) * CHUNK)
            u_ref[0, n] = _bf_dot(v[rows], kd_ref[rows, :], _TN)

    return pl.pallas_call(
        body, name="hgrn_local_fwd", grid=grid, in_specs=[head, head, head, v_head], out_specs=[head, mats],
        out_shape=[jax.ShapeDtypeStruct((t, D_MODEL), F32),
                   jax.ShapeDtypeStruct((HGRN_HEADS, t // CHUNK, HGRN_K, HGRN_K), F32)],
        compiler_params=_params(("parallel", "parallel")),
    )(q_in, k_in, kd, z)


def _hgrn_state_specs(t, reverse=False):
    rows = HGRN_CB * CHUNK
    nb = t // rows
    at = (lambda n: nb - 1 - n) if reverse else (lambda n: n)
    tok = pl.BlockSpec((rows, D_MODEL), lambda n: (at(n), 0))
    mats = pl.BlockSpec((HGRN_HEADS, HGRN_CB, HGRN_K, HGRN_K), lambda n: (0, at(n), 0, 0))
    return tok, mats, nb


def _hgrn_state_fwd(o_intra, qe, dec, u):
    t = qe.shape[0]
    tok, mats, nb = _hgrn_state_specs(t)

    def body(oi_ref, qe_ref, dec_ref, u_ref, o_ref, st_ref, s_ref):
        @pl.when(pl.program_id(0) == 0)
        def _():
            s_ref[...] = jnp.zeros_like(s_ref)

        st = [s_ref[h] for h in range(HGRN_HEADS)]
        for n in range(HGRN_CB):
            rows = slice(n * CHUNK, (n + 1) * CHUNK)
            for h in range(HGRN_HEADS):
                cols = slice(h * HGRN_K, (h + 1) * HGRN_K)
                st_ref[h, n] = st[h]
                o_ref[rows, cols] = oi_ref[rows, cols] + _bf_dot(qe_ref[rows, cols], st[h], _NT)
                st[h] = st[h] * dec_ref[n * CHUNK:n * CHUNK + 1, cols] + u_ref[h, n]
        for h in range(HGRN_HEADS):
            s_ref[h] = st[h]

    return pl.pallas_call(
        body, name="hgrn_state_fwd", grid=(nb,), in_specs=[tok, tok, tok, mats], out_specs=[tok, mats],
        out_shape=[jax.ShapeDtypeStruct((t, D_MODEL), F32),
                   jax.ShapeDtypeStruct((HGRN_HEADS, t // CHUNK, HGRN_K, HGRN_K), F32)],
        scratch_shapes=[pltpu.VMEM((HGRN_HEADS, HGRN_K, HGRN_K), F32)],
        compiler_params=_params(("arbitrary",)),
    )(o_intra, qe, dec, u)


def _hgrn_state_bwd(d_o, qe, dec, states):
    t = qe.shape[0]
    tok_r, mats_r, nb = _hgrn_state_specs(t, reverse=True)

    def body(do_ref, qe_ref, dec_ref, st_ref, dqe_ref, ddec_ref, du_ref, d_ref):
        @pl.when(pl.program_id(0) == 0)
        def _():
            d_ref[...] = jnp.zeros_like(d_ref)

        first_row = lax.broadcasted_iota(jnp.int32, (CHUNK, HGRN_K), 0) == 0
        d = [d_ref[h] for h in range(HGRN_HEADS)]
        for n in reversed(range(HGRN_CB)):
            rows = slice(n * CHUNK, (n + 1) * CHUNK)
            for h in range(HGRN_HEADS):
                cols = slice(h * HGRN_K, (h + 1) * HGRN_K)
                st, do = st_ref[h, n], do_ref[rows, cols]
                du_ref[h, n] = d[h]
                ddec_ref[rows, cols] = jnp.where(first_row, jnp.sum(d[h] * st, axis=0, keepdims=True), 0.0)
                dqe_ref[rows, cols] = _bf_dot(do, st, _NN)
                d[h] = d[h] * dec_ref[n * CHUNK:n * CHUNK + 1, cols] + _bf_dot(do, qe_ref[rows, cols], _TN)
        for h in range(HGRN_HEADS):
            d_ref[h] = d[h]

    out = jax.ShapeDtypeStruct((t, D_MODEL), F32)
    return pl.pallas_call(
        body, name="hgrn_state_bwd", grid=(nb,), in_specs=[tok_r, tok_r, tok_r, mats_r], out_specs=[tok_r, tok_r, mats_r],
        out_shape=[out, out, jax.ShapeDtypeStruct((HGRN_HEADS, t // CHUNK, HGRN_K, HGRN_K), F32)],
        scratch_shapes=[pltpu.VMEM((HGRN_HEADS, HGRN_K, HGRN_K), F32)],
        compiler_params=_params(("arbitrary",)),
    )(d_o, qe, dec, states)


def _hgrn_local_bwd(q_in, k_in, kd, z, d_o, d_u):
    t = q_in.shape[0]
    head, v_head, mats, grid = _hgrn_specs(t)

    def body(q_ref, k_ref, kd_ref, v_ref, do_ref, du_ref, dq_ref, dk_ref, dkd_ref, dv_ref):
        tril = _tril()
        q, k, v, do = q_ref[...], k_ref[...], v_ref[...], do_ref[...]
        scores = jnp.where(tril, _bf_dot(q, k, _NT), 0.0)
        d_scores = jnp.where(tril, _bf_dot(do, v, _NT), 0.0)
        dq_ref[...] = _bf_dot(d_scores, k, _NN)
        dk_ref[...] = _bf_dot(d_scores, q, _TN)
        dv = _bf_dot(scores, do, _TN)
        for n in range(HGRN_LOCAL_CB):
            rows = slice(n * CHUNK, (n + 1) * CHUNK)
            du = du_ref[0, n]
            dv_ref[rows, :] = dv[rows] + _bf_dot(kd_ref[rows, :], du, _NT)
            dkd_ref[rows, :] = _bf_dot(v[rows], du, _NN)

    out = jax.ShapeDtypeStruct((t, D_MODEL), F32)
    return pl.pallas_call(
        body, name="hgrn_local_bwd", grid=grid, in_specs=[head, head, head, v_head, head, mats], out_specs=[head] * 4,
        out_shape=[out] * 4, compiler_params=_params(("parallel", "parallel")),
    )(q_in, k_in, kd, z, d_o, d_u)


def _seg_bcast(xs, e):
    n = RWKV_HEAD
    lhs = [x.astype(BF16) for x in xs]
    out = jnp.dot(lhs[0] if len(lhs) == 1 else jnp.concatenate(lhs, axis=0), e, preferred_element_type=F32)
    return [out[i * n:(i + 1) * n] for i in range(len(xs))]


def _rows_to_cols(rows, diag, e):
    zero = jnp.zeros((), BF16)
    parts = [jnp.where(diag, row.astype(BF16), zero) for row in rows]
    out = jnp.dot(jnp.concatenate(parts, axis=0), e, preferred_element_type=F32)
    n = RWKV_HEAD
    return [out[i * n:(i + 1) * n] for i in range(len(rows))]


def _col_to_row(col, diag):
    return jnp.sum(jnp.where(diag, col, 0.0), axis=0, keepdims=True)


def _scan_consts():
    e = _seg_matrix(SCAN_GROUP, RWKV_HEAD)
    i = lax.broadcasted_iota(jnp.int32, (RWKV_HEAD, SCAN_GROUP), 0)
    l = lax.broadcasted_iota(jnp.int32, (RWKV_HEAD, SCAN_GROUP), 1)
    groups = [slice(g * SCAN_GROUP, (g + 1) * SCAN_GROUP) for g in range(D_MODEL // SCAN_GROUP)]
    return e, (l % RWKV_HEAD) == i, groups


def _rwkv_fwd(zl, w, k, a, b, shards, placed):
    t = zl.shape[0]
    nb = t // SCAN_TB
    n = len(shards)
    steps = range(SCAN_TB)

    def body(*refs):
        scan(*refs[:6], *refs[6 + 2 * n:8 + 2 * n], refs[8 + 3 * n])
        gather = (refs[6:6 + n], refs[8 + 2 * n:8 + 3 * n], *refs[9 + 3 * n:])

        @pl.when(pl.program_id(0) == 0)
        def _():
            _gather_start(*gather)

        @pl.when(pl.program_id(0) == nb - 1)
        def _():
            _gather_finish(*gather)

    def scan(r_ref, w_ref, k_ref, v_ref, a_ref, b_ref, y_ref, st_ref, s_ref):
        @pl.when(pl.program_id(0) == 0)
        def _():
            s_ref[...] = jnp.zeros_like(s_ref)

        e, diag, groups = _scan_consts()
        v_cols = [_rows_to_cols([v_ref[i:i + 1, sl] for i in steps], diag, e) for sl in groups]
        s = [s_ref[:, sl] for sl in groups]
        for i in steps:
            for g, sl in enumerate(groups):
                (sa,) = _seg_bcast([s[g] * a_ref[i:i + 1, sl]], e)
                s[g] = s[g] * w_ref[i:i + 1, sl] + sa * b_ref[i:i + 1, sl] + v_cols[g][i] * k_ref[i:i + 1, sl]
                st_ref[i, :, sl] = s[g]
        for g, sl in enumerate(groups):
            s_ref[:, sl] = s[g]
            y_cols = _seg_bcast([st_ref[i, :, sl] * r_ref[i:i + 1, sl] for i in steps], e)
            for i in steps:
                y_ref[i:i + 1, sl] = _col_to_row(y_cols[i], diag)

    blk = pl.BlockSpec((SCAN_TB, D_MODEL), lambda n: (n, 0))
    v_blk = pl.BlockSpec((SCAN_TB, D_MODEL), lambda n: (n, 2))
    outs = pl.pallas_call(
        body, name="rwkv_fwd", grid=(nb,), in_specs=[blk, blk, blk, v_blk, blk, blk] + [ANY] * (2 * n),
        out_specs=[blk, pl.BlockSpec((SCAN_TB, RWKV_HEAD, D_MODEL), lambda i: (i, 0, 0))] + [ANY] * n,
        out_shape=[jax.ShapeDtypeStruct((t, D_MODEL), F32), jax.ShapeDtypeStruct((t, RWKV_HEAD, D_MODEL), F32)]
        + [jax.ShapeDtypeStruct(p.shape, p.dtype) for p in placed],
        input_output_aliases={6 + n + i: 2 + i for i in range(n)},
        scratch_shapes=[pltpu.VMEM((RWKV_HEAD, D_MODEL), F32), pltpu.SemaphoreType.DMA((n, 6)), pltpu.SemaphoreType.DMA((n, 6))],
        compiler_params=_params(("arbitrary",)),
    )(zl, w, k, zl, a, b, *shards, *placed)
    return outs[0], outs[1], outs[2:]


def _rwkv_bwd(zl, w, k, a, b, states, d_y, parts):
    t = zl.shape[0]
    nb = t // SCAN_TB
    n = len(parts)
    steps = range(SCAN_TB)

    def body(*refs):
        scan(*refs[:9], *refs[9 + n:15 + n], refs[15 + 2 * n])
        exchange = (refs[9:9 + n], refs[15 + n:15 + 2 * n], *refs[16 + 2 * n:])

        @pl.when(pl.program_id(0) == 0)
        def _():
            _rs_chips_start(*exchange)

        @pl.when(pl.program_id(0) == nb - 1)
        def _():
            _rs_chips_finish(*exchange)

    def scan(r_ref, w_ref, k_ref, v_ref, a_ref, b_ref, st_ref, prev_ref, dy_ref,
             dr_ref, dw_ref, dk_ref, dv_ref, da_ref, db_ref, ds_ref):
        @pl.when(pl.program_id(0) == 0)
        def _():
            ds_ref[...] = jnp.zeros_like(ds_ref)

        has_prev = (pl.program_id(0) < nb - 1).astype(F32)
        e, diag, groups = _scan_consts()
        colsum = lambda x: jnp.sum(x, axis=0, keepdims=True)

        def s_prev(i, sl):
            return st_ref[i - 1, :, sl] if i > 0 else prev_ref[0, :, sl] * has_prev

        dy_cols = [_rows_to_cols([dy_ref[i:i + 1, sl] for i in steps], diag, e) for sl in groups]
        v_cols = [_rows_to_cols([v_ref[i:i + 1, sl] for i in steps], diag, e) for sl in groups]
        sa_cols = [_seg_bcast([s_prev(i, sl) * a_ref[i:i + 1, sl] for i in steps], e) for sl in groups]
        ds = [ds_ref[:, sl] for sl in groups]
        dsk = [[None] * SCAN_TB for _ in groups]
        for i in reversed(steps):
            for g, sl in enumerate(groups):
                d = ds[g] + dy_cols[g][i] * r_ref[i:i + 1, sl]
                dr_ref[i:i + 1, sl] = colsum(st_ref[i, :, sl] * dy_cols[g][i])
                dw_ref[i:i + 1, sl] = colsum(d * s_prev(i, sl))
                db_ref[i:i + 1, sl] = colsum(d * sa_cols[g][i])
                dk_ref[i:i + 1, sl] = colsum(d * v_cols[g][i])
                dsk[g][i] = d * k_ref[i:i + 1, sl]
                (dsa,) = _seg_bcast([d * b_ref[i:i + 1, sl]], e)
                da_ref[i:i + 1, sl] = colsum(s_prev(i, sl) * dsa)
                ds[g] = d * w_ref[i:i + 1, sl] + dsa * a_ref[i:i + 1, sl]
        for g, sl in enumerate(groups):
            ds_ref[:, sl] = ds[g]
            dv_cols = _seg_bcast(dsk[g], e)
            for i in steps:
                dv_ref[i:i + 1, sl] = _col_to_row(dv_cols[i], diag)

    blk = pl.BlockSpec((SCAN_TB, D_MODEL), lambda n: (nb - 1 - n, 0))
    v_blk = pl.BlockSpec((SCAN_TB, D_MODEL), lambda n: (nb - 1 - n, 2))
    out = jax.ShapeDtypeStruct((t, D_MODEL), F32)
    outs = pl.pallas_call(
        body, name="rwkv_bwd", grid=(nb,),
        in_specs=[blk, blk, blk, v_blk, blk, blk] + [
            pl.BlockSpec((SCAN_TB, RWKV_HEAD, D_MODEL), lambda i: (nb - 1 - i, 0, 0)),
            pl.BlockSpec((1, RWKV_HEAD, D_MODEL), lambda i: (jnp.maximum((nb - 1 - i) * SCAN_TB - 1, 0), 0, 0)),
            blk] + [ANY] * n,
        out_specs=[blk] * 6 + [ANY] * n,
        out_shape=[out] * 6 + [jax.ShapeDtypeStruct((3,) + p.shape[1:], p.dtype) for p in parts],
        scratch_shapes=[pltpu.VMEM((RWKV_HEAD, D_MODEL), F32), pltpu.SemaphoreType.DMA((n, 3)), pltpu.SemaphoreType.DMA((n, 3))],
        compiler_params=_params(("arbitrary",)),
    )(zl, w, k, zl, a, b, states, states, d_y, *parts)
    return outs[:6], outs[6:]


def _loss_head(h1, ff, target, g_post):
    def fn(tv, pv):
        a, f, tgt = tv
        h2, vjp = jax.vjp(lambda a_, f_, g_: a_ + _rms(f_, g_), a, f, pv[0])
        err = h2 - tgt
        loss = 0.5 * jnp.sum(jnp.mean(err * err, axis=-1, keepdims=True), axis=0, keepdims=True)
        d_a, d_f, d_g = vjp(err * (1.0 / D_MODEL))
        return [d_a, d_f], [loss, d_g]

    return _tok_call("loss_head", fn, [(h1, D_MODEL, 0), (ff, D_MODEL, 0), (target, D_MODEL, 0)], [g_post],
                     [(D_MODEL, F32), (D_MODEL, BF16)], red_shapes=[(1, 1), (1, D_MODEL)], tm=LIGHT_TM)


def _sum_call(name, terms, rows_per_block=None):
    a0, i0 = terms[0]
    r, c = a0.shape[-2:]
    tr = rows_per_block or r

    def body(*refs):
        acc = refs[0][...].reshape(tr, c)
        for ref in refs[1:-1]:
            acc = acc + ref[...].reshape(tr, c)
        refs[-1][...] = acc

    def spec(arr, idx):
        if arr.ndim == 2:
            return pl.BlockSpec((tr, c), lambda i: (i, 0))
        return pl.BlockSpec((1, tr, c), functools.partial(lambda i, idx: (idx, i, 0), idx=idx))

    return pl.pallas_call(
        body, name=name, grid=(r // tr,), in_specs=[spec(a, i) for a, i in terms],
        out_specs=pl.BlockSpec((tr, c), lambda i: (i, 0)), out_shape=jax.ShapeDtypeStruct((r, c), F32),
        compiler_params=_params(("parallel",)),
    )(*[a for a, _ in terms])


def _adamw_math(w, g, m, v):
    m2 = ADAM_B1 * m + (1.0 - ADAM_B1) * g
    v2 = ADAM_B2 * v + (1.0 - ADAM_B2) * (g * g)
    m_hat = m2 / (1.0 - ADAM_B1 ** ADAM_STEP)
    v_hat = v2 / (1.0 - ADAM_B2 ** ADAM_STEP)
    return -ADAM_LR * (m_hat / (jnp.sqrt(v_hat) + ADAM_EPS) + ADAM_WD * w), m2, v2


def _adamw(name, w, g, m, v, bm, bn, g_transposed=False, exchange=None):
    r, c = w.shape
    grid = (pl.cdiv(r, bm), pl.cdiv(c, bn))

    def body(*refs):
        w_ref, g_ref, m_ref, v_ref = refs[:4]
        go_ref, d_ref, mo_ref, vo_ref = refs[-6:-2] if exchange else refs[4:8]
        if exchange:
            riders = (refs[4], refs[-7], refs[-2], refs[-1])
            first = (pl.program_id(0) == 0) & (pl.program_id(1) == 0)

            @pl.when(first)
            def _():
                _exchange8_start(*riders)

        g = g_ref[...].T if g_transposed else g_ref[...]
        d, m2, v2 = _adamw_math(w_ref[...], g, m_ref[...], v_ref[...])
        go_ref[...] = g
        d_ref[...] = d
        mo_ref[...] = m2
        vo_ref[...] = v2
        if exchange:
            @pl.when((pl.program_id(0) == grid[0] - 1) & (pl.program_id(1) == grid[1] - 1))
            def _():
                _exchange8_finish(*riders)

    blk = pl.BlockSpec((bm, bn), lambda i, j: (i, j))
    g_blk = pl.BlockSpec((bn, bm), lambda i, j: (j, i)) if g_transposed else blk
    out = jax.ShapeDtypeStruct((r, c), F32)
    if not exchange:
        return pl.pallas_call(
            body, name=name, grid=grid, in_specs=[blk, g_blk, blk, blk],
            out_specs=[blk] * 4, out_shape=[out] * 4, compiler_params=_params(("parallel", "parallel")),
        )(w, g, m, v)
    parts, landing = exchange
    outs = pl.pallas_call(
        body, name=name, grid=grid, in_specs=[blk, g_blk, blk, blk, ANY, ANY],
        out_specs=[ANY] + [blk] * 4, out_shape=[jax.ShapeDtypeStruct(landing.shape, landing.dtype)] + [out] * 4,
        input_output_aliases={5: 0}, scratch_shapes=[pltpu.SemaphoreType.DMA((7,)), pltpu.SemaphoreType.DMA((7,))],
        compiler_params=_params(("arbitrary", "arbitrary")),
    )(w, g, m, v, parts, landing)
    return outs[1:], outs[0]


ANY = pl.BlockSpec(memory_space=pl.ANY)


def _place():
    x, y, c = lax.axis_index("x"), lax.axis_index("y"), lax.axis_index("c")
    chips = [(1 - x, y), (x, 1 - y), (1 - x, 1 - y)]
    return x, y, c, chips


def _sibling():
    return (lax.axis_index("x"), lax.axis_index("y"), 1 - lax.axis_index("c"))


def _wait_all(local, remote):
    for cp in local:
        cp.wait()
    for cp in remote:
        cp.wait_send()


def _place_shard(name, shard, place):
    r, cols = shard.shape
    tr = r // 4

    def body(s_ref, in_ref, out_ref):
        out_ref[...] = in_ref[...]

    return pl.pallas_call(
        body, name=name,
        grid_spec=pltpu.PrefetchScalarGridSpec(
            num_scalar_prefetch=1, grid=(4,), in_specs=[pl.BlockSpec((tr, cols), lambda i, s: (i, 0))],
            out_specs=pl.BlockSpec((tr, cols), lambda i, s: (4 * s[1] + i, 0))),
        out_shape=jax.ShapeDtypeStruct((N_SHARD * r, cols), shard.dtype), compiler_params=_params(("arbitrary",)),
    )(place, shard)


def _gather_copies(ins, outs, send_sems, recv_sems, only_first=False):
    x, y, c, chips = _place()
    me, sibling = (x, y, c), _sibling()

    def rows(k, px, py, pc):
        h = ins[k].shape[0] // 2
        return outs[k].at[pl.ds((2 * px + py) * 2 * h + pc * h, h), :]

    def copy(k, j, block, to, src=None):
        return pltpu.make_async_remote_copy(
            src_ref=rows(k, *block) if src is None else src, dst_ref=rows(k, *block),
            send_sem=send_sems.at[k, j], recv_sem=recv_sems.at[k, j], device_id=to, device_id_type=MESH)

    each = [(k, j, chip) for k in range(len(ins)) for j, chip in enumerate(chips)]
    half = lambda k: ins[k].at[pl.ds(c * (ins[k].shape[0] // 2), ins[k].shape[0] // 2), :]
    first = [copy(k, j, me, (*chip, c), src=half(k)) for k, j, chip in each]
    if only_first:
        return first
    arrive = [copy(k, j, (*chip, c), me) for k, j, chip in each]
    passed = [copy(k, 3 + j, (*chip, c), sibling) for k, j, chip in each]
    landed = [copy(k, 3 + j, (*chip, 1 - c), me) for k, j, chip in each]
    return first, arrive, passed, landed


def _gather_start(ins, outs, send_sems, recv_sems):
    for cp in _gather_copies(ins, outs, send_sems, recv_sems, only_first=True):
        cp.start()


def _gather_finish(ins, outs, send_sems, recv_sems):
    first, arrive, passed, landed = _gather_copies(ins, outs, send_sems, recv_sems)
    for arrival, forward in zip(arrive, passed):
        arrival.wait_recv()
        forward.start()
    for cp in landed:
        cp.wait_recv()
    _wait_all([], first + passed)


def _gather_shards(shards, placed):
    n = len(shards)

    def body(*refs):
        ins, outs = refs[:n], refs[2 * n:3 * n]
        _gather_start(ins, outs, *refs[3 * n:])
        _gather_finish(ins, outs, *refs[3 * n:])

    return pl.pallas_call(
        body, name="gather_shards", in_specs=[ANY] * (2 * n), out_specs=[ANY] * n,
        out_shape=[jax.ShapeDtypeStruct(a.shape, a.dtype) for a in placed],
        input_output_aliases={n + k: k for k in range(n)},
        scratch_shapes=[pltpu.SemaphoreType.DMA((n, 6)), pltpu.SemaphoreType.DMA((n, 6))],
    )(*shards, *placed)


def _exchange8_copies(in_ref, out_ref, send_sems, recv_sems, only_sends=False):
    x, y, c, _ = _place()
    sends, arrivals = [], []
    for rel in range(1, 8):
        dx, dy, dc = rel >> 2 & 1, rel >> 1 & 1, rel & 1
        sends.append(pltpu.make_async_remote_copy(
            src_ref=in_ref.at[2 * (x ^ dx) + (y ^ dy)], dst_ref=out_ref.at[4 * x + 2 * y + c],
            send_sem=send_sems.at[rel - 1], recv_sem=recv_sems.at[rel - 1],
            device_id=(x ^ dx, y ^ dy, c ^ dc), device_id_type=MESH))
        if not only_sends:
            arrivals.append(pltpu.make_async_remote_copy(
                src_ref=in_ref.at[0], dst_ref=out_ref.at[4 * (x ^ dx) + 2 * (y ^ dy) + (c ^ dc)],
                send_sem=send_sems.at[rel - 1], recv_sem=recv_sems.at[rel - 1],
                device_id=(x, y, c), device_id_type=MESH))
    return sends, arrivals


def _exchange8_start(in_ref, out_ref, send_sems, recv_sems):
    for cp in _exchange8_copies(in_ref, out_ref, send_sems, recv_sems, only_sends=True)[0]:
        cp.start()


def _exchange8_finish(in_ref, out_ref, send_sems, recv_sems):
    sends, arrivals = _exchange8_copies(in_ref, out_ref, send_sems, recv_sems)
    for cp in arrivals:
        cp.wait_recv()
    _wait_all([], sends)


def _rs_sibling_copies(ins, outs, send_sems, recv_sems):
    c = lax.axis_index("c")
    return [pltpu.make_async_remote_copy(
        src_ref=ins[k].at[2 * s + 1 - c], dst_ref=outs[k].at[s], send_sem=send_sems.at[k, s], recv_sem=recv_sems.at[k, s],
        device_id=_sibling(), device_id_type=MESH) for k in range(len(ins)) for s in range(N_SHARD)]


def _rs_sibling_start(ins, outs, send_sems, recv_sems):
    for cp in _rs_sibling_copies(ins, outs, send_sems, recv_sems):
        cp.start()


def _rs_sibling_finish(ins, outs, send_sems, recv_sems):
    sends = _rs_sibling_copies(ins, outs, send_sems, recv_sems)
    for cp in sends:
        cp.wait_recv()
    _wait_all([], sends)


def _rs_sibling(grads):
    n = len(grads)

    def body(*refs):
        _rs_sibling_start(refs[:n], refs[n:2 * n], *refs[2 * n:])
        _rs_sibling_finish(refs[:n], refs[n:2 * n], *refs[2 * n:])

    return pl.pallas_call(
        body, name="rs_sibling", in_specs=[ANY] * n, out_specs=[ANY] * n,
        out_shape=[jax.ShapeDtypeStruct((N_SHARD,) + a.shape[1:], a.dtype) for a in grads],
        scratch_shapes=[pltpu.SemaphoreType.DMA((n, N_SHARD)), pltpu.SemaphoreType.DMA((n, N_SHARD))],
    )(*grads)


def _rs_chips_copies(ins, outs, send_sems, recv_sems):
    x, y, c, chips = _place()
    return [pltpu.make_async_remote_copy(
        src_ref=ins[k].at[2 * px + py], dst_ref=outs[k].at[j], send_sem=send_sems.at[k, j], recv_sem=recv_sems.at[k, j],
        device_id=(px, py, c), device_id_type=MESH) for k in range(len(ins)) for j, (px, py) in enumerate(chips)]


def _rs_chips_start(ins, outs, send_sems, recv_sems):
    for cp in _rs_chips_copies(ins, outs, send_sems, recv_sems):
        cp.start()


def _rs_chips_finish(ins, outs, send_sems, recv_sems):
    sends = _rs_chips_copies(ins, outs, send_sems, recv_sems)
    for cp in sends:
        cp.wait_recv()
    _wait_all([], sends)


def _rs_finish(bufs):
    n = len(bufs)

    def body(*refs):
        outs = refs[n:2 * n]
        send_sems, recv_sems = refs[2 * n:]
        c = lax.axis_index("c")
        sends = []
        for k in range(n):
            cp = pltpu.make_async_remote_copy(
                src_ref=outs[k].at[c], dst_ref=outs[k].at[c], send_sem=send_sems.at[k], recv_sem=recv_sems.at[k],
                device_id=_sibling(), device_id_type=MESH)
            cp.start()
            sends.append(cp)
        for k in range(n):
            pltpu.make_async_remote_copy(
                src_ref=outs[k].at[c], dst_ref=outs[k].at[1 - c], send_sem=send_sems.at[k], recv_sem=recv_sems.at[k],
                device_id=_sibling(), device_id_type=MESH).wait_recv()
        _wait_all([], sends)

    return pl.pallas_call(
        body, name="rs_finish", in_specs=[ANY] * n, out_specs=[ANY] * n,
        out_shape=[jax.ShapeDtypeStruct(a.shape, a.dtype) for a in bufs], input_output_aliases={k: k for k in range(n)},
        scratch_shapes=[pltpu.SemaphoreType.DMA((n,)), pltpu.SemaphoreType.DMA((n,))],
    )(*bufs)


def _sum3d(name, terms, scalars, grid_lead, out_lead, out_index, tr=None, out_dtype=F32):
    h, c = terms[0][0].shape[1:]
    tr = tr or h

    def body(s_ref, *refs):
        acc = refs[0][...].astype(F32)
        for ref in refs[1:-1]:
            acc = acc + ref[...].astype(F32)
        refs[-1][...] = acc.astype(refs[-1].dtype)

    in_specs = [pl.BlockSpec((1, tr, c), functools.partial(lambda l, i, s_ref, f: (f(l, s_ref), i, 0), f=f)) for _, f in terms]
    out_spec = pl.BlockSpec((1, tr, c), lambda l, i, s_ref: (out_index(l, s_ref), i, 0))
    return pl.pallas_call(
        body, name=name,
        grid_spec=pltpu.PrefetchScalarGridSpec(num_scalar_prefetch=1, grid=(grid_lead, h // tr), in_specs=in_specs, out_specs=out_spec),
        out_shape=jax.ShapeDtypeStruct((out_lead, h, c), out_dtype), compiler_params=_params(("arbitrary", "arbitrary")),
    )(scalars, *[a for a, _ in terms])


def _halves(grads):
    return [a.reshape(2 * N_SHARD, a.shape[0] // (2 * N_SHARD), a.shape[1]) for a in grads]


def _rs_stage1(g8, place, tag, from_sibling=None):
    from_sibling = _rs_sibling(g8) if from_sibling is None else from_sibling
    parts = [_sum3d(f"rs_add1_{tag}{k}", [(g8[k], lambda l, s: 2 * l + s[0]), (from_sibling[k], lambda l, s: l)], place,
                    N_SHARD, N_SHARD, lambda l, s: l, tr=g8[k].shape[1] // 2, out_dtype=BF16)
             for k in range(len(g8))]
    return g8, from_sibling, parts


def _rs_stage3(stage1, from_chips, place, tag):
    g8, from_sibling, _ = stage1
    mine = [(lambda l, s: 2 * s[1] + s[0]), (lambda l, s: s[1])]
    bufs = [_sum3d(f"rs_add2_{tag}{k}", [(g8[k], mine[0]), (from_sibling[k], mine[1])]
                   + [(from_chips[k], functools.partial(lambda l, s, j: j, j=j)) for j in range(3)],
                   place, 1, 2, lambda l, s: s[0], tr=g8[k].shape[1] // 2)
            for k in range(len(g8))]
    whole = _rs_finish(bufs)
    return [w.reshape(2 * w.shape[1], w.shape[2]) for w in whole]


_LATE = ["w_a", "w_b", "w_out", "w_down", "w_up_t"]


def _device_step(x, target, p, late_shards, late_placed, place):
    t = x.shape[0]
    d = D_MODEL
    tok = lambda arr, c=0, w=d: (arr, w, c)
    f32x = lambda n: [(d, F32)] * n
    rp_params = [p["w0"], p["w2p"], p["a0"], p["a2p"], p["g2p"], p["k_k"], p["k_a"]]
    post_params = [p["ln_w"], p["ln_b"], p["r_k"]]
    g = {}

    (xn,) = _tok_fwd("norm1_fwd", _fn_norm, [tok(x)], [p["g1"]], [(d, BF16)], tm=LIGHT_TM)
    z = _mm("in_proj", xn, p["w_in_t"], "nt", tm=t, tn=256)
    q_in, k_in, kd, qe, dec = _tok_fwd("hgates_fwd", _fn_hgates, [tok(z, 0), tok(z, 1)], [p["lb2"]], f32x(5))
    o_intra, u = _hgrn_local_fwd(q_in, k_in, kd, z)
    o_raw, h_states = _hgrn_state_fwd(o_intra, qe, dec, u)
    zl = _lerp_fwd(z, p["mu"])
    lora = tok(zl, 3 * d // LORA, LORA)
    decay, kr2, avec, bvec, gate = _tok_fwd("rprep_fwd", _fn_rprep, [tok(zl, 1), lora], rp_params, f32x(5))
    y, r_states, late = _rwkv_fwd(zl, decay, kr2, avec, bvec, late_shards, late_placed)
    p = dict(p, **dict(zip(_LATE, late)))
    (o_a,) = _tok_fwd("hpost_fwd", _fn_hpost, [tok(o_raw), tok(z, 3)], [p["gnorm"]], [(d, BF16)])
    post_toks = [tok(y), tok(zl, 0), tok(kr2), tok(zl, 2), tok(gate)]
    (o_b,) = _tok_fwd("rpost_fwd", _fn_rpost, post_toks, post_params, [(d, BF16)])
    y_a = _mm("branch_a", o_a, p["w_a"], "nn")
    y_b = _mm("branch_b", o_b, p["w_b"], "nn")
    merge_toks = [tok(z, C_G // 256, 256), tok(z, (C_G + d) // 256, 256), tok(y_a, 0, 256), tok(y_b, 0, 256)]
    (merged,) = _tok_fwd("merge_fwd", _fn_merge, merge_toks, [], [(256, BF16)], col_grid=4, tm=512)
    mix = _mm("out_proj", merged, p["w_out"], "nn")
    h1, xn2 = _tok_fwd("res1_fwd", _fn_res1, [tok(x), tok(mix)], [p["g_post1"], p["g_pre2"]], [(d, F32), (d, BF16)],
                       tm=LIGHT_TM)
    hu = _mm("up_proj", xn2, p["w_up_t"], "nt", tm=t, tn=512)
    act = _conv_fwd(hu, p["conv_w"], p["conv_b"])
    ff = _mm("down_proj", act, p["w_down"], "nn")
    d_h1, d_ff, loss, g["g_post2"] = _loss_head(h1, ff, target, p["g_post2"])

    d_act = _mm("d_act", d_ff, p["w_down"], "nt")
    g["w_down"] = _mm("dw_down", act, d_ff, "tn", tm=256, tn=1024)
    d_hu, d_cw, d_cb = _conv_bwd(hu, p["conv_w"], p["conv_b"], d_act)
    g["conv_w"], g["conv_b"] = d_cw.transpose(1, 0, 2).reshape(3, 2 * D_FF), d_cb.reshape(1, 2 * D_FF)
    d_hu = d_hu.reshape(2 * t, D_FF)
    d_xn2 = _mm("d_xn2", d_hu, p["w_up_t"], "nn", tm=t, tk=D_FF, mk=(t, 2 * D_FF), a_map=lambda i, j, q: (q, 0))
    g["w_up_t"] = _mm("dw_up", d_hu, xn2, "tn", tm=CONV_TILE, tn=1024, mk=(2 * D_FF, t),
                      a_map=lambda i, j, q: (i // N_CONV_TILES, i % N_CONV_TILES))
    d_x_res, d_mix, g["g_post1"], g["g_pre2"] = _tok_bwd(
        "res1_bwd", _fn_res1, [tok(x), tok(mix)], [p["g_post1"], p["g_pre2"]], [[tok(d_h1)], [tok(d_xn2)]],
        [(d, F32), (d, BF16)], tm=LIGHT_TM)
    d_merged = _mm("d_merged", d_mix, p["w_out"], "nt")
    g["w_out"] = _mm("dw_out", merged, d_mix, "tn")
    d_ga, d_gb, d_ya, d_yb = _tok_bwd("merge_bwd", _fn_merge, merge_toks, [], [[tok(d_merged, 0, 256)]],
                                      [(256, BF16)] * 4, col_grid=4, tm=512)
    d_oa = _mm("d_oa", d_ya, p["w_a"], "nt")
    g["w_a"] = _mm("dw_a", o_a, d_ya, "tn")
    d_ob = _mm("d_ob", d_yb, p["w_b"], "nt")
    g["w_b"] = _mm("dw_b", o_b, d_yb, "tn")
    d_oraw, d_hg, g["gnorm"] = _tok_bwd("hpost_bwd", _fn_hpost, [tok(o_raw), tok(z, 3)], [p["gnorm"]], [[tok(d_oa)]],
                                        [(d, F32), (d, BF16)])
    late_g8 = _halves([g[n] for n in _LATE])
    rpost = _tok_bwd("rpost_bwd", _fn_rpost, post_toks, post_params, [[tok(d_ob)]], f32x(5), sibling_rider=late_g8)
    d_y, d_r1, d_kr2_1, d_v1, d_gate, g["ln_w"], g["ln_b"], g["r_k"] = rpost[:8]
    stage1 = _rs_stage1(late_g8, place, "late", from_sibling=rpost[8:])
    (d_r2, d_decay, d_kr2_2, d_v2, d_avec, d_bvec), from_chips = _rwkv_bwd(
        zl, decay, kr2, avec, bvec, r_states, d_y, stage1[2])
    g.update(zip(_LATE, _rs_stage3(stage1, from_chips, place, "late")))
    prep = _tok_bwd("rprep_bwd", _fn_rprep, [tok(zl, 1), lora], rp_params,
                    [[tok(d_decay)], [tok(d_kr2_1), tok(d_kr2_2)], [tok(d_avec)], [tok(d_bvec)], [tok(d_gate)]],
                    [(d, F32), (LORA, F32)])
    d_kr, d_lora = prep[:2]
    g["w0"], g["w2p"], g["a0"], g["a2p"], g["g2p"], g["k_k"], g["k_a"] = prep[2:]
    dz_r, g["mu"] = _lerp_bwd(z, p["mu"], (d_r1, d_r2), d_kr, (d_v1, d_v2), d_lora)
    d_qe, d_dec, d_u = _hgrn_state_bwd(d_oraw, qe, dec, h_states)
    d_q_in, d_k_in, d_kd, d_vi = _hgrn_local_bwd(q_in, k_in, kd, z, d_oraw, d_u)
    d_hq, d_hf, g["lb2"] = _tok_bwd("hgates_bwd", _fn_hgates, [tok(z, 0), tok(z, 1)], [p["lb2"]],
                                    [[tok(d_q_in)], [tok(d_k_in)], [tok(d_kd)], [tok(d_qe)], [tok(d_dec)]], [(d, BF16)] * 2)
    dz = jnp.concatenate([d_hq, d_hf, d_vi.astype(BF16), d_hg, dz_r, d_ga, d_gb], axis=1)
    stage1 = _rs_stage1(_halves([_mm("dw_in", dz, xn, "tn", tm=256, tn=1024)]), place, "w_in")
    d_xn, from_chips = _mm("d_xn", dz, p["w_in_t"], "nn", tm=1024, tn=512, tk=IN_COLS // 2, riders=stage1[2])
    (g["w_in_t"],) = _rs_stage3(stage1, from_chips, place, "w_in")
    grad_x, g["g1"] = _tok_bwd("norm1_bwd", _fn_norm, [tok(x)], [p["g1"]], [[tok(d_xn)]], [(d, F32)],
                               add_to_first=tok(d_x_res), tm=LIGHT_TM)
    return loss, grad_x, g


_WEIGHTS = ["attn_pre_norm", "w_in", "hgrn_lb", "hgrn_gnorm", "w_branch_a", "rwkv_mu", "rwkv_w0", "rwkv_w2", "rwkv_a0",
            "rwkv_a2", "rwkv_g2", "rwkv_k_k", "rwkv_k_a", "rwkv_r_k", "rwkv_ln_w", "rwkv_ln_b", "w_branch_b", "w_out",
            "attn_post_norm", "ffn_pre_norm", "w_up", "conv_w", "conv_b", "w_down", "ffn_post_norm"]
_REPLICATED = [("attn_pre_norm", "g1"), ("hgrn_lb", "lb2"), ("hgrn_gnorm", "gnorm"), ("rwkv_mu", "mu"), ("rwkv_w0", "w0"),
               ("rwkv_a0", "a0"), ("rwkv_k_k", "k_k"), ("rwkv_k_a", "k_a"), ("rwkv_r_k", "r_k"), ("rwkv_ln_w", "ln_w"),
               ("rwkv_ln_b", "ln_b"), ("attn_post_norm", "g_post1"), ("ffn_pre_norm", "g_pre2"), ("conv_b", "conv_b"),
               ("ffn_post_norm", "g_post2")]
SLAB_COLS = 1024


def _pack(arrays):
    pieces, total = [], 0
    for a in arrays:
        flat = a.reshape(-1)
        rows = -(-flat.shape[0] // SLAB_COLS)
        pieces.append(jnp.pad(flat, (0, rows * SLAB_COLS - flat.shape[0])).reshape(rows, SLAB_COLS))
        total += rows
    if total % 8:
        pieces.append(jnp.zeros((8 - total % 8, SLAB_COLS), F32))
    return jnp.concatenate(pieces, axis=0)


def _unpack(slab, shapes):
    out, at = [], 0
    for s in shapes:
        size = 1
        for dim in s:
            size *= dim
        rows = -(-size // SLAB_COLS)
        out.append(slab[at:at + rows].reshape(-1)[:size].reshape(s))
        at += rows
    return out


def kernel(x, attn_pre_norm, w_in, hgrn_lb, hgrn_gnorm, w_branch_a, rwkv_mu, rwkv_w0, rwkv_w2, rwkv_a0, rwkv_a2, rwkv_g2, rwkv_k_k, rwkv_k_a, rwkv_r_k, rwkv_ln_w, rwkv_ln_b, w_branch_b, w_out, attn_post_norm, ffn_pre_norm, w_up, conv_w, conv_b, w_down, ffn_post_norm, loss_target, m_attn_pre_norm, m_w_in, m_hgrn_lb, m_hgrn_gnorm, m_w_branch_a, m_rwkv_mu, m_rwkv_w0, m_rwkv_w2, m_rwkv_a0, m_rwkv_a2, m_rwkv_g2, m_rwkv_k_k, m_rwkv_k_a, m_rwkv_r_k, m_rwkv_ln_w, m_rwkv_ln_b, m_w_branch_b, m_w_out, m_attn_post_norm, m_ffn_pre_norm, m_w_up, m_conv_w, m_conv_b, m_w_down, m_ffn_post_norm, v_attn_pre_norm, v_w_in, v_hgrn_lb, v_hgrn_gnorm, v_w_branch_a, v_rwkv_mu, v_rwkv_w0, v_rwkv_w2, v_rwkv_a0, v_rwkv_a2, v_rwkv_g2, v_rwkv_k_k, v_rwkv_k_a, v_rwkv_r_k, v_rwkv_ln_w, v_rwkv_ln_b, v_w_branch_b, v_w_out, v_attn_post_norm, v_ffn_pre_norm, v_w_up, v_conv_w, v_conv_b, v_w_down, v_ffn_post_norm):
    given = dict(locals())
    w = {n: given[n] for n in _WEIGHTS}
    mom = {n: given["m_" + n] for n in _WEIGHTS}
    var = {n: given["v_" + n] for n in _WEIGHTS}
    shard = 2 * lax.axis_index("x") + lax.axis_index("y")
    place = jnp.stack([lax.axis_index("c"), shard]).astype(jnp.int32)
    row = lambda a: a.reshape(1, -1)
    lora_of = lambda d: jnp.concatenate([d["rwkv_w2"][0], d["rwkv_a2"][0], d["rwkv_g2"][0]], axis=0)

    shards = [w["w_in"][0].T.astype(BF16), lora_of(w), jnp.pad(w["conv_w"][0], ((0, 29), (0, 0)))]
    late_shards = [w["w_branch_a"][0].astype(BF16), w["w_branch_b"][0].astype(BF16), w["w_out"][0].astype(BF16),
                   w["w_down"][0].astype(BF16), w["w_up"][0].T.astype(BF16)]
    placed = [_place_shard(f"place_{k}", a, place) for k, a in enumerate(shards)]
    late_placed = [_place_shard(f"place_late_{k}", a, place) for k, a in enumerate(late_shards)]
    w_in_t, lora_g, conv_g = _gather_shards(shards, placed)
    lora_full = lora_g.reshape(N_SHARD, LORA, 256).transpose(1, 0, 2).reshape(LORA, D_MODEL)
    conv_full = conv_g.reshape(N_SHARD, 32, 2 * D_FF // N_SHARD)[:, :3].transpose(1, 0, 2).reshape(3, 2 * D_FF)
    lrow = lax.broadcasted_iota(jnp.int32, (LORA, 1), 0)
    p = {
        "g1": row(w["attn_pre_norm"]), "lb2": w["hgrn_lb"], "gnorm": row(w["hgrn_gnorm"]), "w_in_t": w_in_t,
        "mu": row(w["rwkv_mu"]), "w0": row(w["rwkv_w0"]), "a0": row(w["rwkv_a0"]),
        "w2p": jnp.where(lrow < 64, lora_full, 0.0), "a2p": jnp.where((lrow >= 64) & (lrow < 128), lora_full, 0.0),
        "g2p": jnp.where(lrow >= 128, lora_full, 0.0),
        "k_k": row(w["rwkv_k_k"]), "k_a": row(w["rwkv_k_a"]), "r_k": row(w["rwkv_r_k"]), "ln_w": row(w["rwkv_ln_w"]),
        "ln_b": row(w["rwkv_ln_b"]), "g_post1": row(w["attn_post_norm"]),
        "g_pre2": row(w["ffn_pre_norm"]), "conv_w": conv_full, "conv_b": row(w["conv_b"]),
        "g_post2": row(w["ffn_post_norm"]),
    }

    loss, grad_x, g = _device_step(x[0], loss_target[0], p, late_shards, late_placed, place)

    g_in_t = g["w_in_t"]
    g_a, g_b, g_o, g_dn, g_up_t = [g[n] for n in _LATE]
    rep_shapes = [w[n].shape for n, _ in _REPLICATED]
    rep = _pack([g[key] for _, key in _REPLICATED])
    n_rep_rows = rep.shape[0]
    cw = 2 * D_FF // N_SHARD
    lora_rows, conv_rows = LORA * 256 // SLAB_COLS, -(-3 * cw // SLAB_COLS)
    lora_g = jnp.concatenate([g["w2p"][0:64], g["a2p"][64:128], g["g2p"][128:256]], axis=0)
    lora_parts = lora_g.reshape(LORA, N_SHARD, 256).transpose(1, 0, 2).reshape(N_SHARD, lora_rows, SLAB_COLS)
    conv_parts = g["conv_w"].reshape(3, N_SHARD, cw).transpose(1, 0, 2).reshape(N_SHARD, 3 * cw)
    conv_parts = jnp.pad(conv_parts, ((0, 0), (0, conv_rows * SLAB_COLS - 3 * cw))).reshape(N_SHARD, conv_rows, SLAB_COLS)
    n_rows = n_rep_rows + lora_rows + conv_rows
    fill = jnp.zeros((N_SHARD, -n_rows % 8, SLAB_COLS), F32)
    parts = jnp.concatenate([jnp.broadcast_to(rep, (N_SHARD,) + rep.shape), lora_parts, conv_parts, fill], axis=1)
    me = 4 * lax.axis_index("x") + 2 * lax.axis_index("y") + lax.axis_index("c")
    landing = lax.dynamic_update_slice(jnp.zeros((8,) + parts.shape[1:], F32),
                                       lax.dynamic_index_in_dim(parts, shard, 0, keepdims=True), (me, 0, 0))

    res = {}

    def put(name, outs, shape=None):
        res[name] = [o.reshape(w[name].shape if shape is None else shape) for o in outs]

    w_in_out, gathered = _adamw("adamw_w_in", w["w_in"][0], g_in_t, mom["w_in"][0], var["w_in"][0], 1024, 128,
                                g_transposed=True, exchange=(parts, landing))
    put("w_in", w_in_out)
    summed = _sum3d("small_sum", [(gathered, functools.partial(lambda l, s, i: i, i=i)) for i in range(8)], place, 1, 1,
                    lambda l, s: 0)[0]
    lora_grad = summed[n_rep_rows:n_rep_rows + lora_rows].reshape(LORA, 256)
    conv_grad = summed[n_rep_rows + lora_rows:n_rows].reshape(-1)[:3 * cw].reshape(3, cw)
    put("w_up", _adamw("adamw_w_up", w["w_up"][0], g_up_t, mom["w_up"][0], var["w_up"][0], 1024, 128, g_transposed=True))
    for name, grad in (("w_branch_a", g_a), ("w_branch_b", g_b), ("w_out", g_o)):
        put(name, _adamw("adamw_" + name, w[name][0], grad, mom[name][0], var[name][0], 256, 1024))
    put("w_down", _adamw("adamw_w_down", w["w_down"][0], g_dn, mom["w_down"][0], var["w_down"][0], 176, 1024))
    put("conv_w", _adamw("adamw_conv_w", w["conv_w"][0], conv_grad, mom["conv_w"][0], var["conv_w"][0], 3, 2 * D_FF // N_SHARD))
    lora_out = _adamw("adamw_lora", lora_of(w), lora_grad, lora_of(mom), lora_of(var), LORA, 256)
    for name, lo, hi in (("rwkv_w2", 0, 64), ("rwkv_a2", 64, 128), ("rwkv_g2", 128, 256)):
        put(name, [o[lo:hi] for o in lora_out])
    rep_names = [n for n, _ in _REPLICATED]
    rep_out = _adamw("adamw_small", _pack([w[n] for n in rep_names]), summed[:n_rep_rows], _pack([mom[n] for n in rep_names]),
                     _pack([var[n] for n in rep_names]), n_rep_rows, SLAB_COLS)
    for name, parts in zip(rep_names, zip(*[_unpack(o, rep_shapes) for o in rep_out])):
        put(name, list(parts))

    loss = lax.psum(loss[0, 0], ("x", "y", "c"))
    return (loss, grad_x[None], *[res[n][0] for n in _WEIGHTS], *[res[n][1] for n in _WEIGHTS],
            *[res[n][2] for n in _WEIGHTS], *[res[n][3] for n in _WEIGHTS])
```

```python
import functools

import jax
import jax.numpy as jnp
from jax import lax
from jax.experimental import pallas as pl
from jax.experimental.pallas import tpu as pltpu

F32, BF16 = jnp.float32, jnp.bfloat16
MESH = pl.DeviceIdType.MESH

D_MODEL = 1024
HGRN_HEADS = 8
HGRN_K = 128
HGRN_SCALE = HGRN_K ** -0.5
CHUNK = 32
RWKV_HEAD = 64
LORA = 256
D_FF = 2816
EPS = 1e-6
GN_EPS = 1e-5 * RWKV_HEAD
N_SHARD = 4
ADAM_LR, ADAM_B1, ADAM_B2, ADAM_EPS, ADAM_WD, ADAM_STEP = 0.001, 0.9, 0.999, 1e-08, 0.01, 10

LANES = 128
VMEM_LIMIT = 56 * 1024 * 1024
SCAN_TB = 16
SCAN_GROUP = 256
LIGHT_TM = 256

C_HQ, C_HF, C_HI, C_HG = 0, 1024, 2048, 3072
C_R = 4096
R_COLS = 3328
C_G = 7424
IN_COLS = 9472


def _params(sem=None, **kw):
    return pltpu.CompilerParams(dimension_semantics=sem, vmem_limit_bytes=VMEM_LIMIT, **kw)


def _seg_matrix(n, seg):
    r = lax.broadcasted_iota(jnp.int32, (n, n), 0) // seg
    c = lax.broadcasted_iota(jnp.int32, (n, n), 1) // seg
    return (r == c).astype(BF16)


def _split3(x):
    hi = x.astype(BF16)
    r1 = x - hi.astype(F32)
    mid = r1.astype(BF16)
    lo = (r1 - mid.astype(F32)).astype(BF16)
    return hi, mid, lo


def _segsum_impl(x, seg):
    e = _seg_matrix(LANES, seg)
    outs = []
    for g in range(x.shape[1] // LANES):
        hi, mid, lo = _split3(x[:, g * LANES:(g + 1) * LANES])
        outs.append(jnp.dot(hi, e, preferred_element_type=F32) + jnp.dot(mid, e, preferred_element_type=F32)
                    + jnp.dot(lo, e, preferred_element_type=F32))
    return outs[0] if len(outs) == 1 else jnp.concatenate(outs, axis=1)


def _make_segsum(seg):
    @jax.custom_vjp
    def f(x):
        return _segsum_impl(x, seg)

    f.defvjp(lambda x: (_segsum_impl(x, seg), None), lambda _, ct: (_segsum_impl(ct, seg),))
    return f


_segsum64 = _make_segsum(RWKV_HEAD)
_segsum128 = _make_segsum(HGRN_K)


def _chunk_mm_impl(x, kind, transposed):
    n = x.shape[0]
    r = lax.broadcasted_iota(jnp.int32, (n, n), 1 if transposed else 0)
    c = lax.broadcasted_iota(jnp.int32, (n, n), 0 if transposed else 1)
    same = (r // CHUNK) == (c // CHUNK)
    if kind == "cumsum":
        m = same & (r >= c)
    else:
        m = same & (c % CHUNK == (CHUNK // 2 - 1 if kind == "mid" else CHUNK - 1))
    m = m.astype(BF16)
    hi, mid, lo = _split3(x)
    return (jnp.dot(m, hi, preferred_element_type=F32) + jnp.dot(m, mid, preferred_element_type=F32)
            + jnp.dot(m, lo, preferred_element_type=F32))


def _make_chunk_mm(kind):
    @jax.custom_vjp
    def f(x):
        return _chunk_mm_impl(x, kind, False)

    f.defvjp(lambda x: (_chunk_mm_impl(x, kind, False), None), lambda _, ct: (_chunk_mm_impl(ct, kind, True),))
    return f


_chunk_cumsum = _make_chunk_mm("cumsum")
_chunk_mid = _make_chunk_mm("mid")
_chunk_last = _make_chunk_mm("last")


@jax.custom_vjp
def _bdot(x, w):
    return jnp.dot(x.astype(BF16), w.astype(BF16), preferred_element_type=F32)


def _bdot_fwd(x, w):
    return _bdot(x, w), (x, w)


def _bdot_bwd(res, ct):
    x, w = res
    ctb = ct.astype(BF16)
    dx = lax.dot_general(ctb, w.astype(BF16), (((1,), (1,)), ((), ())), preferred_element_type=F32)
    dw = lax.dot_general(x.astype(BF16), ctb, (((0,), (0,)), ((), ())), preferred_element_type=F32)
    return dx, dw


_bdot.defvjp(_bdot_fwd, _bdot_bwd)


def _sigmoid(x):
    return 1.0 / (1.0 + jnp.exp(-x))


def _silu(x):
    return x * _sigmoid(x)


def _softplus(x):
    return jnp.maximum(x, 0.0) + jnp.log(1.0 + jnp.exp(-jnp.abs(x)))


def _rms(x, g):
    return x * lax.rsqrt(jnp.mean(x * x, axis=-1, keepdims=True) + EPS) * g


def _fn_norm(t, p):
    return [_rms(t[0], p[0])]


def _fn_hgates(t, p):
    hq, hf = t
    lb2 = p[0]
    m = jnp.max(lb2, axis=0, keepdims=True)
    e = jnp.exp(lb2 - m)
    first = lax.broadcasted_iota(jnp.int32, e.shape, 0) == 0
    lb = jnp.sum(jnp.where(first, e, 0.0), axis=0, keepdims=True) / jnp.sum(e, axis=0, keepdims=True)
    f = lb + (1.0 - lb) * _sigmoid(hf)
    q, k = _silu(hq) * HGRN_SCALE, 1.0 - f
    b = _chunk_cumsum(jnp.log(f))
    b_ref, b_last = _chunk_mid(b), _chunk_last(b)
    return [q * jnp.exp(b - b_ref), k * jnp.exp(b_ref - b), k * jnp.exp(b_last - b), q * jnp.exp(b), jnp.exp(b_last)]


def _fn_hpost(t, p):
    o, hg = t
    ms = _segsum128(o * o) * (1.0 / HGRN_K)
    return [o * lax.rsqrt(ms + EPS) * p[0] * _silu(hg)]


def _fn_rprep(t, p):
    kr, lora = t
    w0, w2p, a0, a2p, g2p, k_k, k_a = p
    pre_w = w0 + _bdot(jnp.tanh(lora), w2p)
    w_log = -_softplus(-pre_w) - 0.5
    decay = jnp.exp(-jnp.exp(w_log))
    a = _sigmoid(a0 + _bdot(lora, a2p))
    g = _bdot(_sigmoid(lora), g2p)
    kk = kr * k_k
    kk = kk / jnp.maximum(jnp.sqrt(_segsum64(kk * kk)), 1e-12)
    kr2 = kr * (1.0 + (a - 1.0) * k_a)
    return [decay, kr2, -kk, kk * a, g]


def _fn_rpost(t, p):
    y, r, kr2, v, g = t
    ln_w, ln_b, r_k = p
    mu = _segsum64(y) * (1.0 / RWKV_HEAD)
    yc = y - mu
    var = _segsum64(yc * yc) * (1.0 / RWKV_HEAD)
    yn = yc * lax.rsqrt(var + GN_EPS) * ln_w + ln_b
    bonus = _segsum64(r * kr2 * r_k) * v
    return [(yn + bonus) * g]


def _fn_merge(t, p):
    ga, gb, ya, yb = t
    return [_sigmoid(ga) * ya + _sigmoid(gb) * yb]


def _fn_res1(t, p):
    x, mix = t
    h1 = x + _rms(mix, p[0])
    return [h1, _rms(h1, p[1])]


def _tok_call(name, fn, toks, params, outs, red_shapes=(), tm=128, col_grid=1, sibling_rider=None):
    n_t, n_p, n_o, n_red = len(toks), len(params), len(outs), len(red_shapes)
    n_r = 0 if sibling_rider is None else len(sibling_rider)
    t_len = toks[0][0].shape[0]
    tm = min(tm, t_len)
    grid = (t_len // tm, col_grid)

    def body(*refs):
        first_out = n_t + n_p + n_r
        if n_r:
            rider = (refs[n_t + n_p:first_out], refs[first_out + n_o + n_red:first_out + n_o + n_red + n_r], *refs[-2:])

            @pl.when((pl.program_id(0) == 0) & (pl.program_id(1) == 0))
            def _():
                _rs_sibling_start(*rider)

        tv = [r[...].astype(F32) for r in refs[:n_t]]
        pv = [r[...] for r in refs[n_t:n_t + n_p]]
        o, red = fn(tv, pv)
        for ref, val in zip(refs[first_out:first_out + n_o], o):
            ref[...] = val.astype(ref.dtype)
        if n_r:
            @pl.when((pl.program_id(0) == grid[0] - 1) & (pl.program_id(1) == grid[1] - 1))
            def _():
                _rs_sibling_finish(*rider)

        red_refs = refs[first_out + n_o:first_out + n_o + n_red]
        if red_refs:
            first = pl.program_id(0) == 0

            @pl.when(first)
            def _():
                for ref, val in zip(red_refs, red):
                    ref[...] = val

            @pl.when(jnp.logical_not(first))
            def _():
                for ref, val in zip(red_refs, red):
                    ref[...] += val

    in_specs = [pl.BlockSpec((tm, w), functools.partial(lambda i, j, c: (i, c + j), c=c)) for (_, w, c) in toks]
    in_specs += [pl.BlockSpec(p.shape, lambda i, j: (0, 0)) for p in params]
    out_specs = [pl.BlockSpec((tm, w), lambda i, j: (i, j)) for (w, _) in outs]
    out_specs += [pl.BlockSpec(s, lambda i, j: (0, 0)) for s in red_shapes]
    out_shape = [jax.ShapeDtypeStruct((t_len, w * col_grid), dt) for (w, dt) in outs]
    out_shape += [jax.ShapeDtypeStruct(s, F32) for s in red_shapes]
    scratch = []
    if n_r:
        in_specs += [ANY] * n_r
        out_specs += [ANY] * n_r
        out_shape += [jax.ShapeDtypeStruct((N_SHARD,) + a.shape[1:], a.dtype) for a in sibling_rider]
        scratch = [pltpu.SemaphoreType.DMA((n_r, N_SHARD)), pltpu.SemaphoreType.DMA((n_r, N_SHARD))]
    return pl.pallas_call(
        body, name=name, grid=grid, in_specs=in_specs, out_specs=out_specs, out_shape=out_shape, scratch_shapes=scratch,
        compiler_params=_params(("arbitrary", "arbitrary")),
    )(*[a for (a, _, _) in toks], *params, *(sibling_rider or []))


def _tok_fwd(name, fn, toks, params, outs, **kw):
    return _tok_call(name, lambda tv, pv: (fn(tv, pv), []), toks, params, outs, **kw)


def _tok_bwd(name, fn, toks, params, cts, want, add_to_first=None, **kw):
    n_t = len(toks)
    flat = [c for group in cts for c in group]
    extra = [] if add_to_first is None else [add_to_first]

    def bwd(tv, pv):
        prim, rest = tv[:n_t], tv[n_t:]
        ct, at = [], 0
        for group in cts:
            ct.append(functools.reduce(lambda u, v: u + v, rest[at:at + len(group)]))
            at += len(group)
        _, vjp = jax.vjp(lambda *a: tuple(fn(list(a[:n_t]), list(a[n_t:]))), *prim, *pv)
        g = vjp(tuple(ct))
        tok_grads = [g[i] for i in range(n_t) if want[i] is not None]
        if extra:
            tok_grads[0] = tok_grads[0] + rest[at]
        return tok_grads, list(g[n_t:])

    return _tok_call(name, bwd, list(toks) + flat + extra, params, [w for w in want if w is not None],
                     red_shapes=[p.shape for p in params], **kw)


def _mm(name, a, b, mode, out_dtype=F32, tm=None, tn=None, tk=None, riders=None, a_map=None, mk=None):
    if mode == "nn":
        (m, k), (_, n) = a.shape, b.shape
    elif mode == "nt":
        (m, k), (n, _) = a.shape, b.shape
    else:
        (k, m), (_, n) = a.shape, b.shape
    if mk is not None:
        m, k = mk
    tm = (512 if mode == "tn" else 2048) if tm is None else tm
    tn = (512 if mode == "tn" else 256) if tn is None else tn
    tk = k if tk is None else tk
    tm, tn = min(tm, m), min(tn, n)
    nk = k // tk
    assert m % tm == 0 and n % tn == 0 and k % tk == 0, (name, a.shape, b.shape, tm, tn, tk)
    a_spec = pl.BlockSpec((tk, tm), lambda i, j, q: (q, i)) if mode == "tn" else pl.BlockSpec((tm, tk), lambda i, j, q: (i, q))
    if a_map is not None:
        a_spec = pl.BlockSpec(a_spec.block_shape, a_map)
    b_spec = pl.BlockSpec((tn, tk), lambda i, j, q: (j, q)) if mode == "nt" else pl.BlockSpec((tk, tn), lambda i, j, q: (q, j))
    dn = {"nn": (((1,), (0,)), ((), ())), "nt": (((1,), (1,)), ((), ())), "tn": (((0,), (0,)), ((), ()))}[mode]
    grid = (m // tm, n // tn, nk)
    nr = 0 if riders is None else len(riders)

    def body(*refs):
        a_ref, b_ref, o_ref = refs[0], refs[1], refs[2 + nr]
        acc = refs[3 + 2 * nr] if nk > 1 else None
        if nr:
            exchange = (refs[2:2 + nr], refs[3 + nr:3 + 2 * nr], *refs[-2:])
            at = [pl.program_id(ax) for ax in range(3)]

            @pl.when((at[0] == 0) & (at[1] == 0) & (at[2] == 0))
            def _():
                _rs_chips_start(*exchange)

        p = lax.dot_general(a_ref[...], b_ref[...], dn, preferred_element_type=F32)
        if nk == 1:
            o_ref[...] = p.astype(o_ref.dtype)
        else:
            q = pl.program_id(2)

            @pl.when(q == 0)
            def _():
                acc[...] = p

            @pl.when(q > 0)
            def _():
                acc[...] += p

            @pl.when(q == nk - 1)
            def _():
                o_ref[...] = acc[...].astype(o_ref.dtype)

        if nr:
            @pl.when((at[0] == grid[0] - 1) & (at[1] == grid[1] - 1) & (at[2] == grid[2] - 1))
            def _():
                _rs_chips_finish(*exchange)

    scratch = [pltpu.VMEM((tm, tn), F32)] if nk > 1 else []
    out_specs = [pl.BlockSpec((tm, tn), lambda i, j, q: (i, j))]
    out_shape = [jax.ShapeDtypeStruct((m, n), out_dtype)]
    if nr:
        scratch += [pltpu.SemaphoreType.DMA((nr, 3)), pltpu.SemaphoreType.DMA((nr, 3))]
        out_specs += [ANY] * nr
        out_shape += [jax.ShapeDtypeStruct((3,) + r.shape[1:], r.dtype) for r in riders]
    outs = pl.pallas_call(
        body, name=name, grid=grid, in_specs=[a_spec, b_spec] + [ANY] * nr, out_specs=out_specs, out_shape=out_shape,
        scratch_shapes=scratch,
        compiler_params=_params(("arbitrary",) * 3 if nr else ("parallel", "parallel", "arbitrary")),
    )(a, b, *(riders or []))
    return (outs[0], outs[1:]) if nr else outs[0]


def _shift_down(z, n):
    rows = lax.broadcasted_iota(jnp.int32, z.shape, 0)
    return jnp.where(rows < n, 0.0, pltpu.roll(z, n, 0))


def _shift_up(z, n):
    t = z.shape[0]
    rows = lax.broadcasted_iota(jnp.int32, z.shape, 0)
    return jnp.where(rows >= t - n, 0.0, pltpu.roll(z, t - n, 0))


def _lerp_fwd(z, mu):
    t = z.shape[0]
    w = 256

    def body(z_ref, mu_ref, o_ref):
        zz = z_ref[...]
        o_ref[...] = zz + mu_ref[...] * (_shift_down(zz, 1) - zz)

    return pl.pallas_call(
        body, name="lerp_fwd", grid=(R_COLS // w,),
        in_specs=[pl.BlockSpec((t, w), lambda j: (0, C_R // w + j)), pl.BlockSpec((1, w), lambda j: (0, j))],
        out_specs=pl.BlockSpec((t, w), lambda j: (0, j)), out_shape=jax.ShapeDtypeStruct((t, R_COLS), F32),
        compiler_params=_params(("parallel",)),
    )(z, mu)


def _lerp_bwd(z, mu, d_r, d_k, d_v, d_lora):
    t = z.shape[0]
    w = 256
    per = D_MODEL // w

    def body(z_ref, mu_ref, r1_ref, r2_ref, k_ref, v1_ref, v2_ref, l_ref, dz_ref, dmu_ref):
        j = pl.program_id(0)
        zz, m = z_ref[...], mu_ref[...]
        d = jnp.where(j < per, r1_ref[...] + r2_ref[...],
                      jnp.where(j < 2 * per, k_ref[...], jnp.where(j < 3 * per, v1_ref[...] + v2_ref[...], l_ref[...])))
        dz_ref[...] = (d * (1.0 - m) + _shift_up(d * m, 1)).astype(dz_ref.dtype)
        dmu_ref[...] = jnp.sum(d * (_shift_down(zz, 1) - zz), axis=0, keepdims=True)

    piece = lambda first: pl.BlockSpec((t, w), lambda j: (0, jnp.clip(j - first, 0, per - 1)))
    return pl.pallas_call(
        body, name="lerp_bwd", grid=(R_COLS // w,),
        in_specs=[pl.BlockSpec((t, w), lambda j: (0, C_R // w + j)), pl.BlockSpec((1, w), lambda j: (0, j)),
                  piece(0), piece(0), piece(per), piece(2 * per), piece(2 * per), pl.BlockSpec((t, w), lambda j: (0, 0))],
        out_specs=[pl.BlockSpec((t, w), lambda j: (0, j)), pl.BlockSpec((1, w), lambda j: (0, j))],
        out_shape=[jax.ShapeDtypeStruct((t, R_COLS), BF16), jax.ShapeDtypeStruct((1, R_COLS), F32)],
        compiler_params=_params(("arbitrary",)),
    )(z, mu, *d_r, d_k, *d_v, d_lora)


CONV_TILE = 256
N_CONV_TILES = D_FF // CONV_TILE


def _conv(h, w, b):
    return b + w[0:1, :] * _shift_down(h, 2) + w[1:2, :] * _shift_down(h, 1) + w[2:3, :] * h


def _conv_fwd(hu, conv_w, conv_b):
    t = hu.shape[0]
    n = N_CONV_TILES

    def body(hg_ref, hv_ref, wg_ref, wv_ref, bg_ref, bv_ref, o_ref):
        gate = _conv(hg_ref[...], wg_ref[...], bg_ref[...])
        val = _conv(hv_ref[...], wv_ref[...], bv_ref[...])
        o_ref[...] = (_silu(gate) * val).astype(o_ref.dtype)

    col = lambda off: pl.BlockSpec((t, CONV_TILE), lambda j: (0, j + off))
    wspec = lambda off: pl.BlockSpec((3, CONV_TILE), lambda j: (0, j + off))
    bspec = lambda off: pl.BlockSpec((1, CONV_TILE), lambda j: (0, j + off))
    return pl.pallas_call(
        body, name="conv_fwd", grid=(n,),
        in_specs=[col(0), col(n), wspec(0), wspec(n), bspec(0), bspec(n)],
        out_specs=pl.BlockSpec((t, CONV_TILE), lambda j: (0, j)), out_shape=jax.ShapeDtypeStruct((t, D_FF), BF16),
        compiler_params=_params(("parallel",)),
    )(hu, hu, conv_w, conv_w, conv_b, conv_b)


def _conv_bwd(hu, conv_w, conv_b, d_act):
    t = hu.shape[0]
    n = N_CONV_TILES

    def body(hg_ref, hv_ref, wg_ref, wv_ref, bg_ref, bv_ref, d_ref, dh_ref, dw_ref, db_ref):
        hg, hv, wg, wv = hg_ref[...], hv_ref[...], wg_ref[...], wv_ref[...]
        gate = _conv(hg, wg, bg_ref[...])
        val = _conv(hv, wv, bv_ref[...])
        d = d_ref[...]
        sg = _sigmoid(gate)
        d_gate = d * val * (sg * (1.0 + gate * (1.0 - sg)))
        d_val = d * (gate * sg)
        for half, (dc, h, w) in enumerate(((d_gate, hg, wg), (d_val, hv, wv))):
            dh = w[2:3, :] * dc + w[1:2, :] * _shift_up(dc, 1) + w[0:1, :] * _shift_up(dc, 2)
            dh_ref[half] = dh.astype(dh_ref.dtype)
            dw_ref[half, 0:1, :] = jnp.sum(dc * _shift_down(h, 2), axis=0, keepdims=True)
            dw_ref[half, 1:2, :] = jnp.sum(dc * _shift_down(h, 1), axis=0, keepdims=True)
            dw_ref[half, 2:3, :] = jnp.sum(dc * h, axis=0, keepdims=True)
            db_ref[half] = jnp.sum(dc, axis=0, keepdims=True)

    gcol = lambda rows: pl.BlockSpec((rows, CONV_TILE), lambda j: (0, j))
    vcol = lambda rows: pl.BlockSpec((rows, CONV_TILE), lambda j: (0, j + n))
    both = lambda rows: pl.BlockSpec((2, rows, CONV_TILE), lambda j: (0, 0, j))
    return pl.pallas_call(
        body, name="conv_bwd", grid=(n,),
        in_specs=[gcol(t), vcol(t), gcol(3), vcol(3), gcol(1), vcol(1), gcol(t)],
        out_specs=[both(t), both(3), both(1)],
        out_shape=[jax.ShapeDtypeStruct((2, t, D_FF), BF16), jax.ShapeDtypeStruct((2, 3, D_FF), F32),
                   jax.ShapeDtypeStruct((2, 1, D_FF), F32)],
        compiler_params=_params(("parallel",)),
    )(hu, hu, conv_w, conv_w, conv_b, conv_b, d_act)


_NN = (((1,), (0,)), ((), ()))
_NT = (((1,), (1,)), ((), ()))
_TN = (((0,), (0,)), ((), ()))
HGRN_CB = 8
HGRN_LOCAL_CB = 16


def _bf_dot(a, b, dn):
    return lax.dot_general(a.astype(BF16), b.astype(BF16), dn, preferred_element_type=F32)


def _tril():
    n = HGRN_LOCAL_CB * CHUNK
    r = lax.broadcasted_iota(jnp.int32, (n, n), 0)
    c = lax.broadcasted_iota(jnp.int32, (n, n), 1)
    return (r // CHUNK == c // CHUNK) & (r >= c)


def _hgrn_specs(t):
    rows = HGRN_LOCAL_CB * CHUNK
    head = pl.BlockSpec((rows, HGRN_K), lambda h, n: (n, h))
    v_head = pl.BlockSpec((rows, HGRN_K), lambda h, n: (n, C_HI // HGRN_K + h))
    mats = pl.BlockSpec((1, HGRN_LOCAL_CB, HGRN_K, HGRN_K), lambda h, n: (h, n, 0, 0))
    return head, v_head, mats, (HGRN_HEADS, t // rows)


def _hgrn_local_fwd(q_in, k_in, kd, z):
    t = q_in.shape[0]
    head, v_head, mats, grid = _hgrn_specs(t)

    def body(q_ref, k_ref, kd_ref, v_ref, o_ref, u_ref):
        v = v_ref[...]
        scores = jnp.where(_tril(), _bf_dot(q_ref[...], k_ref[...], _NT), 0.0)
        o_ref[...] = _bf_dot(scores, v, _NN)
        for n in range(HGRN_LOCAL_CB):
            rows = slice(n * CHUNK, (n + 1) * CHUNK)
            u_ref[0, n] = _bf_dot(v[rows], kd_ref[rows, :], _TN)

    return pl.pallas_call(
        body, name="hgrn_local_fwd", grid=grid, in_specs=[head, head, head, v_head], out_specs=[head, mats],
        out_shape=[jax.ShapeDtypeStruct((t, D_MODEL), F32),
                   jax.ShapeDtypeStruct((HGRN_HEADS, t // CHUNK, HGRN_K, HGRN_K), F32)],
        compiler_params=_params(("parallel", "parallel")),
    )(q_in, k_in, kd, z)


def _hgrn_state_specs(t, reverse=False):
    rows = HGRN_CB * CHUNK
    nb = t // rows
    at = (lambda n: nb - 1 - n) if reverse else (lambda n: n)
    tok = pl.BlockSpec((rows, D_MODEL), lambda n: (at(n), 0))
    mats = pl.BlockSpec((HGRN_HEADS, HGRN_CB, HGRN_K, HGRN_K), lambda n: (0, at(n), 0, 0))
    return tok, mats, nb


def _hgrn_state_fwd(o_intra, qe, dec, u):
    t = qe.shape[0]
    tok, mats, nb = _hgrn_state_specs(t)

    def body(oi_ref, qe_ref, dec_ref, u_ref, o_ref, st_ref, s_ref):
        @pl.when(pl.program_id(0) == 0)
        def _():
            s_ref[...] = jnp.zeros_like(s_ref)

        st = [s_ref[h] for h in range(HGRN_HEADS)]
        for n in range(HGRN_CB):
            rows = slice(n * CHUNK, (n + 1) * CHUNK)
            for h in range(HGRN_HEADS):
                cols = slice(h * HGRN_K, (h + 1) * HGRN_K)
                st_ref[h, n] = st[h]
                o_ref[rows, cols] = oi_ref[rows, cols] + _bf_dot(qe_ref[rows, cols], st[h], _NT)
                st[h] = st[h] * dec_ref[n * CHUNK:n * CHUNK + 1, cols] + u_ref[h, n]
        for h in range(HGRN_HEADS):
            s_ref[h] = st[h]

    return pl.pallas_call(
        body, name="hgrn_state_fwd", grid=(nb,), in_specs=[tok, tok, tok, mats], out_specs=[tok, mats],
        out_shape=[jax.ShapeDtypeStruct((t, D_MODEL), F32),
                   jax.ShapeDtypeStruct((HGRN_HEADS, t // CHUNK, HGRN_K, HGRN_K), F32)],
        scratch_shapes=[pltpu.VMEM((HGRN_HEADS, HGRN_K, HGRN_K), F32)],
        compiler_params=_params(("arbitrary",)),
    )(o_intra, qe, dec, u)


def _hgrn_state_bwd(d_o, qe, dec, states):
    t = qe.shape[0]
    tok_r, mats_r, nb = _hgrn_state_specs(t, reverse=True)

    def body(do_ref, qe_ref, dec_ref, st_ref, dqe_ref, ddec_ref, du_ref, d_ref):
        @pl.when(pl.program_id(0) == 0)
        def _():
            d_ref[...] = jnp.zeros_like(d_ref)

        first_row = lax.broadcasted_iota(jnp.int32, (CHUNK, HGRN_K), 0) == 0
        d = [d_ref[h] for h in range(HGRN_HEADS)]
        for n in reversed(range(HGRN_CB)):
            rows = slice(n * CHUNK, (n + 1) * CHUNK)
            for h in range(HGRN_HEADS):
                cols = slice(h * HGRN_K, (h + 1) * HGRN_K)
                st, do = st_ref[h, n], do_ref[rows, cols]
                du_ref[h, n] = d[h]
                ddec_ref[rows, cols] = jnp.where(first_row, jnp.sum(d[h] * st, axis=0, keepdims=True), 0.0)
                dqe_ref[rows, cols] = _bf_dot(do, st, _NN)
                d[h] = d[h] * dec_ref[n * CHUNK:n * CHUNK + 1, cols] + _bf_dot(do, qe_ref[rows, cols], _TN)
        for h in range(HGRN_HEADS):
            d_ref[h] = d[h]

    out = jax.ShapeDtypeStruct((t, D_MODEL), F32)
    return pl.pallas_call(
        body, name="hgrn_state_bwd", grid=(nb,), in_specs=[tok_r, tok_r, tok_r, mats_r], out_specs=[tok_r, tok_r, mats_r],
        out_shape=[out, out, jax.ShapeDtypeStruct((HGRN_HEADS, t // CHUNK, HGRN_K, HGRN_K), F32)],
        scratch_shapes=[pltpu.VMEM((HGRN_HEADS, HGRN_K, HGRN_K), F32)],
        compiler_params=_params(("arbitrary",)),
    )(d_o, qe, dec, states)


def _hgrn_local_bwd(q_in, k_in, kd, z, d_o, d_u):
    t = q_in.shape[0]
    head, v_head, mats, grid = _hgrn_specs(t)

    def body(q_ref, k_ref, kd_ref, v_ref, do_ref, du_ref, dq_ref, dk_ref, dkd_ref, dv_ref):
        tril = _tril()
        q, k, v, do = q_ref[...], k_ref[...], v_ref[...], do_ref[...]
        scores = jnp.where(tril, _bf_dot(q, k, _NT), 0.0)
        d_scores = jnp.where(tril, _bf_dot(do, v, _NT), 0.0)
        dq_ref[...] = _bf_dot(d_scores, k, _NN)
        dk_ref[...] = _bf_dot(d_scores, q, _TN)
        dv = _bf_dot(scores, do, _TN)
        for n in range(HGRN_LOCAL_CB):
            rows = slice(n * CHUNK, (n + 1) * CHUNK)
            du = du_ref[0, n]
            dv_ref[rows, :] = dv[rows] + _bf_dot(kd_ref[rows, :], du, _NT)
            dkd_ref[rows, :] = _bf_dot(v[rows], du, _NN)

    out = jax.ShapeDtypeStruct((t, D_MODEL), F32)
    return pl.pallas_call(
        body, name="hgrn_local_bwd", grid=grid, in_specs=[head, head, head, v_head, head, mats], out_specs=[head] * 4,
        out_shape=[out] * 4, compiler_params=_params(("parallel", "parallel")),
    )(q_in, k_in, kd, z, d_o, d_u)


def _seg_bcast(xs, e):
    n = RWKV_HEAD
    lhs = [x.astype(BF16) for x in xs]
    out = jnp.dot(lhs[0] if len(lhs) == 1 else jnp.concatenate(lhs, axis=0), e, preferred_element_type=F32)
    return [out[i * n:(i + 1) * n] for i in range(len(xs))]


def _rows_to_cols(rows, diag, e):
    zero = jnp.zeros((), BF16)
    parts = [jnp.where(diag, row.astype(BF16), zero) for row in rows]
    out = jnp.dot(jnp.concatenate(parts, axis=0), e, preferred_element_type=F32)
    n = RWKV_HEAD
    return [out[i * n:(i + 1) * n] for i in range(len(rows))]


def _col_to_row(col, diag):
    return jnp.sum(jnp.where(diag, col, 0.0), axis=0, keepdims=True)


_SCAN_PAIRS = ((0, 1), (2, 3))


def _scan_consts():
    e = _seg_matrix(SCAN_GROUP, RWKV_HEAD)
    i = lax.broadcasted_iota(jnp.int32, (RWKV_HEAD, SCAN_GROUP), 0)
    l = lax.broadcasted_iota(jnp.int32, (RWKV_HEAD, SCAN_GROUP), 1)
    groups = [slice(g * SCAN_GROUP, (g + 1) * SCAN_GROUP) for g in range(D_MODEL // SCAN_GROUP)]
    return e, (l % RWKV_HEAD) == i, groups


def _rwkv_fwd(zl, w, k, a, b, shards, placed):
    t = zl.shape[0]
    nb = t // SCAN_TB
    n = len(shards)
    steps = range(SCAN_TB)

    def body(*refs):
        scan(*refs[:6], *refs[6 + 2 * n:8 + 2 * n], refs[8 + 3 * n])
        gather = (refs[6:6 + n], refs[8 + 2 * n:8 + 3 * n], *refs[9 + 3 * n:])

        @pl.when(pl.program_id(0) == 0)
        def _():
            _gather_start(*gather)

        @pl.when(pl.program_id(0) == nb - 1)
        def _():
            _gather_finish(*gather)

    def scan(r_ref, w_ref, k_ref, v_ref, a_ref, b_ref, y_ref, st_ref, s_ref):
        @pl.when(pl.program_id(0) == 0)
        def _():
            s_ref[...] = jnp.zeros_like(s_ref)

        e, diag, groups = _scan_consts()
        v_cols = [_rows_to_cols([v_ref[i:i + 1, sl] for i in steps], diag, e) for sl in groups]
        s = [s_ref[:, sl] for sl in groups]
        for i in steps:
            for pair in _SCAN_PAIRS:
                sas = _seg_bcast([s[g] * a_ref[i:i + 1, groups[g]] for g in pair], e)
                for g, sa in zip(pair, sas):
                    sl = groups[g]
                    s[g] = s[g] * w_ref[i:i + 1, sl] + sa * b_ref[i:i + 1, sl] + v_cols[g][i] * k_ref[i:i + 1, sl]
                    st_ref[i, :, sl] = s[g]
        for g, sl in enumerate(groups):
            s_ref[:, sl] = s[g]
            y_cols = _seg_bcast([st_ref[i, :, sl] * r_ref[i:i + 1, sl] for i in steps], e)
            for i in steps:
                y_ref[i:i + 1, sl] = _col_to_row(y_cols[i], diag)

    blk = pl.BlockSpec((SCAN_TB, D_MODEL), lambda n: (n, 0))
    v_blk = pl.BlockSpec((SCAN_TB, D_MODEL), lambda n: (n, 2))
    outs = pl.pallas_call(
        body, name="rwkv_fwd", grid=(nb,), in_specs=[blk, blk, blk, v_blk, blk, blk] + [ANY] * (2 * n),
        out_specs=[blk, pl.BlockSpec((SCAN_TB, RWKV_HEAD, D_MODEL), lambda i: (i, 0, 0))] + [ANY] * n,
        out_shape=[jax.ShapeDtypeStruct((t, D_MODEL), F32), jax.ShapeDtypeStruct((t, RWKV_HEAD, D_MODEL), F32)]
        + [jax.ShapeDtypeStruct(p.shape, p.dtype) for p in placed],
        input_output_aliases={6 + n + i: 2 + i for i in range(n)},
        scratch_shapes=[pltpu.VMEM((RWKV_HEAD, D_MODEL), F32), pltpu.SemaphoreType.DMA((n, 6)), pltpu.SemaphoreType.DMA((n, 6))],
        compiler_params=_params(("arbitrary",)),
    )(zl, w, k, zl, a, b, *shards, *placed)
    return outs[0], outs[1], outs[2:]


def _rwkv_bwd(zl, w, k, a, b, states, d_y, parts):
    t = zl.shape[0]
    nb = t // SCAN_TB
    n = len(parts)
    steps = range(SCAN_TB)

    def body(*refs):
        scan(*refs[:9], *refs[9 + n:15 + n], refs[15 + 2 * n])
        exchange = (refs[9:9 + n], refs[15 + n:15 + 2 * n], *refs[16 + 2 * n:])

        @pl.when(pl.program_id(0) == 0)
        def _():
            _rs_chips_start(*exchange)

        @pl.when(pl.program_id(0) == nb - 1)
        def _():
            _rs_chips_finish(*exchange)

    def scan(r_ref, w_ref, k_ref, v_ref, a_ref, b_ref, st_ref, prev_ref, dy_ref,
             dr_ref, dw_ref, dk_ref, dv_ref, da_ref, db_ref, ds_ref):
        @pl.when(pl.program_id(0) == 0)
        def _():
            ds_ref[...] = jnp.zeros_like(ds_ref)

        has_prev = (pl.program_id(0) < nb - 1).astype(F32)
        e, diag, groups = _scan_consts()
        colsum = lambda x: jnp.sum(x, axis=0, keepdims=True)

        def s_prev(i, sl):
            return st_ref[i - 1, :, sl] if i > 0 else prev_ref[0, :, sl] * has_prev

        dy_cols = [_rows_to_cols([dy_ref[i:i + 1, sl] for i in steps], diag, e) for sl in groups]
        v_cols = [_rows_to_cols([v_ref[i:i + 1, sl] for i in steps], diag, e) for sl in groups]
        sa_cols = [_seg_bcast([s_prev(i, sl) * a_ref[i:i + 1, sl] for i in steps], e) for sl in groups]
        ds = [ds_ref[:, sl] for sl in groups]
        dsk = [[None] * SCAN_TB for _ in groups]
        for i in reversed(steps):
            for pair in _SCAN_PAIRS:
                d = {}
                for g in pair:
                    sl = groups[g]
                    d[g] = ds[g] + dy_cols[g][i] * r_ref[i:i + 1, sl]
                    dr_ref[i:i + 1, sl] = colsum(st_ref[i, :, sl] * dy_cols[g][i])
                    dw_ref[i:i + 1, sl] = colsum(d[g] * s_prev(i, sl))
                    db_ref[i:i + 1, sl] = colsum(d[g] * sa_cols[g][i])
                    dk_ref[i:i + 1, sl] = colsum(d[g] * v_cols[g][i])
                    dsk[g][i] = d[g] * k_ref[i:i + 1, sl]
                dsas = _seg_bcast([d[g] * b_ref[i:i + 1, groups[g]] for g in pair], e)
                for g, dsa in zip(pair, dsas):
                    sl = groups[g]
                    da_ref[i:i + 1, sl] = colsum(s_prev(i, sl) * dsa)
                    ds[g] = d[g] * w_ref[i:i + 1, sl] + dsa * a_ref[i:i + 1, sl]
        for g, sl in enumerate(groups):
            ds_ref[:, sl] = ds[g]
            dv_cols = _seg_bcast(dsk[g], e)
            for i in steps:
                dv_ref[i:i + 1, sl] = _col_to_row(dv_cols[i], diag)

    blk = pl.BlockSpec((SCAN_TB, D_MODEL), lambda n: (nb - 1 - n, 0))
    v_blk = pl.BlockSpec((SCAN_TB, D_MODEL), lambda n: (nb - 1 - n, 2))
    out = jax.ShapeDtypeStruct((t, D_MODEL), F32)
    outs = pl.pallas_call(
        body, name="rwkv_bwd", grid=(nb,),
        in_specs=[blk, blk, blk, v_blk, blk, blk] + [
            pl.BlockSpec((SCAN_TB, RWKV_HEAD, D_MODEL), lambda i: (nb - 1 - i, 0, 0)),
            pl.BlockSpec((1, RWKV_HEAD, D_MODEL), lambda i: (jnp.maximum((nb - 1 - i) * SCAN_TB - 1, 0), 0, 0)),
            blk] + [ANY] * n,
        out_specs=[blk] * 6 + [ANY] * n,
        out_shape=[out] * 6 + [jax.ShapeDtypeStruct((3,) + p.shape[1:], p.dtype) for p in parts],
        scratch_shapes=[pltpu.VMEM((RWKV_HEAD, D_MODEL), F32), pltpu.SemaphoreType.DMA((n, 3)), pltpu.SemaphoreType.DMA((n, 3))],
        compiler_params=_params(("arbitrary",)),
    )(zl, w, k, zl, a, b, states, states, d_y, *parts)
    return outs[:6], outs[6:]


def _loss_head(h1, ff, target, g_post):
    def fn(tv, pv):
        a, f, tgt = tv
        h2, vjp = jax.vjp(lambda a_, f_, g_: a_ + _rms(f_, g_), a, f, pv[0])
        err = h2 - tgt
        loss = 0.5 * jnp.sum(jnp.mean(err * err, axis=-1, keepdims=True), axis=0, keepdims=True)
        d_a, d_f, d_g = vjp(err * (1.0 / D_MODEL))
        return [d_a, d_f], [loss, d_g]

    return _tok_call("loss_head", fn, [(h1, D_MODEL, 0), (ff, D_MODEL, 0), (target, D_MODEL, 0)], [g_post],
                     [(D_MODEL, F32), (D_MODEL, BF16)], red_shapes=[(1, 1), (1, D_MODEL)], tm=LIGHT_TM)


def _sum_call(name, terms, rows_per_block=None):
    a0, i0 = terms[0]
    r, c = a0.shape[-2:]
    tr = rows_per_block or r

    def body(*refs):
        acc = refs[0][...].reshape(tr, c)
        for ref in refs[1:-1]:
            acc = acc + ref[...].reshape(tr, c)
        refs[-1][...] = acc

    def spec(arr, idx):
        if arr.ndim == 2:
            return pl.BlockSpec((tr, c), lambda i: (i, 0))
        return pl.BlockSpec((1, tr, c), functools.partial(lambda i, idx: (idx, i, 0), idx=idx))

    return pl.pallas_call(
        body, name=name, grid=(r // tr,), in_specs=[spec(a, i) for a, i in terms],
        out_specs=pl.BlockSpec((tr, c), lambda i: (i, 0)), out_shape=jax.ShapeDtypeStruct((r, c), F32),
        compiler_params=_params(("parallel",)),
    )(*[a for a, _ in terms])


def _adamw_math(w, g, m, v):
    m2 = ADAM_B1 * m + (1.0 - ADAM_B1) * g
    v2 = ADAM_B2 * v + (1.0 - ADAM_B2) * (g * g)
    m_hat = m2 / (1.0 - ADAM_B1 ** ADAM_STEP)
    v_hat = v2 / (1.0 - ADAM_B2 ** ADAM_STEP)
    return -ADAM_LR * (m_hat / (jnp.sqrt(v_hat) + ADAM_EPS) + ADAM_WD * w), m2, v2


def _adamw(name, w, g, m, v, bm, bn, g_transposed=False, exchange=None):
    r, c = w.shape
    grid = (pl.cdiv(r, bm), pl.cdiv(c, bn))

    def body(*refs):
        w_ref, g_ref, m_ref, v_ref = refs[:4]
        go_ref, d_ref, mo_ref, vo_ref = refs[-6:-2] if exchange else refs[4:8]
        if exchange:
            riders = (refs[4], refs[-7], refs[-2], refs[-1])
            first = (pl.program_id(0) == 0) & (pl.program_id(1) == 0)

            @pl.when(first)
            def _():
                _exchange8_start(*riders)

        g = g_ref[...].T if g_transposed else g_ref[...]
        d, m2, v2 = _adamw_math(w_ref[...], g, m_ref[...], v_ref[...])
        go_ref[...] = g
        d_ref[...] = d
        mo_ref[...] = m2
        vo_ref[...] = v2
        if exchange:
            @pl.when((pl.program_id(0) == grid[0] - 1) & (pl.program_id(1) == grid[1] - 1))
            def _():
                _exchange8_finish(*riders)

    blk = pl.BlockSpec((bm, bn), lambda i, j: (i, j))
    g_blk = pl.BlockSpec((bn, bm), lambda i, j: (j, i)) if g_transposed else blk
    out = jax.ShapeDtypeStruct((r, c), F32)
    if not exchange:
        return pl.pallas_call(
            body, name=name, grid=grid, in_specs=[blk, g_blk, blk, blk],
            out_specs=[blk] * 4, out_shape=[out] * 4, compiler_params=_params(("parallel", "parallel")),
        )(w, g, m, v)
    parts, landing = exchange
    outs = pl.pallas_call(
        body, name=name, grid=grid, in_specs=[blk, g_blk, blk, blk, ANY, ANY],
        out_specs=[ANY] + [blk] * 4, out_shape=[jax.ShapeDtypeStruct(landing.shape, landing.dtype)] + [out] * 4,
        input_output_aliases={5: 0}, scratch_shapes=[pltpu.SemaphoreType.DMA((7,)), pltpu.SemaphoreType.DMA((7,))],
        compiler_params=_params(("arbitrary", "arbitrary")),
    )(w, g, m, v, parts, landing)
    return outs[1:], outs[0]


ANY = pl.BlockSpec(memory_space=pl.ANY)


def _place():
    x, y, c = lax.axis_index("x"), lax.axis_index("y"), lax.axis_index("c")
    chips = [(1 - x, y), (x, 1 - y), (1 - x, 1 - y)]
    return x, y, c, chips


def _sibling():
    return (lax.axis_index("x"), lax.axis_index("y"), 1 - lax.axis_index("c"))


def _wait_all(local, remote):
    for cp in local:
        cp.wait()
    for cp in remote:
        cp.wait_send()


def _place_shard(name, shard, place):
    r, cols = shard.shape
    tr = r // 4

    def body(s_ref, in_ref, out_ref):
        out_ref[...] = in_ref[...]

    return pl.pallas_call(
        body, name=name,
        grid_spec=pltpu.PrefetchScalarGridSpec(
            num_scalar_prefetch=1, grid=(4,), in_specs=[pl.BlockSpec((tr, cols), lambda i, s: (i, 0))],
            out_specs=pl.BlockSpec((tr, cols), lambda i, s: (4 * s[1] + i, 0))),
        out_shape=jax.ShapeDtypeStruct((N_SHARD * r, cols), shard.dtype), compiler_params=_params(("arbitrary",)),
    )(place, shard)


def _gather_copies(ins, outs, send_sems, recv_sems, only_first=False):
    x, y, c, chips = _place()
    me, sibling = (x, y, c), _sibling()

    def rows(k, px, py, pc):
        h = ins[k].shape[0] // 2
        return outs[k].at[pl.ds((2 * px + py) * 2 * h + pc * h, h), :]

    def copy(k, j, block, to, src=None):
        return pltpu.make_async_remote_copy(
            src_ref=rows(k, *block) if src is None else src, dst_ref=rows(k, *block),
            send_sem=send_sems.at[k, j], recv_sem=recv_sems.at[k, j], device_id=to, device_id_type=MESH)

    each = [(k, j, chip) for k in range(len(ins)) for j, chip in enumerate(chips)]
    half = lambda k: ins[k].at[pl.ds(c * (ins[k].shape[0] // 2), ins[k].shape[0] // 2), :]
    first = [copy(k, j, me, (*chip, c), src=half(k)) for k, j, chip in each]
    if only_first:
        return first
    arrive = [copy(k, j, (*chip, c), me) for k, j, chip in each]
    passed = [copy(k, 3 + j, (*chip, c), sibling) for k, j, chip in each]
    landed = [copy(k, 3 + j, (*chip, 1 - c), me) for k, j, chip in each]
    return first, arrive, passed, landed


def _gather_start(ins, outs, send_sems, recv_sems):
    for cp in _gather_copies(ins, outs, send_sems, recv_sems, only_first=True):
        cp.start()


def _gather_finish(ins, outs, send_sems, recv_sems):
    first, arrive, passed, landed = _gather_copies(ins, outs, send_sems, recv_sems)
    for arrival, forward in zip(arrive, passed):
        arrival.wait_recv()
        forward.start()
    for cp in landed:
        cp.wait_recv()
    _wait_all([], first + passed)


def _gather_shards(shards, placed):
    n = len(shards)

    def body(*refs):
        ins, outs = refs[:n], refs[2 * n:3 * n]
        _gather_start(ins, outs, *refs[3 * n:])
        _gather_finish(ins, outs, *refs[3 * n:])

    return pl.pallas_call(
        body, name="gather_shards", in_specs=[ANY] * (2 * n), out_specs=[ANY] * n,
        out_shape=[jax.ShapeDtypeStruct(a.shape, a.dtype) for a in placed],
        input_output_aliases={n + k: k for k in range(n)},
        scratch_shapes=[pltpu.SemaphoreType.DMA((n, 6)), pltpu.SemaphoreType.DMA((n, 6))],
    )(*shards, *placed)


def _exchange8_copies(in_ref, out_ref, send_sems, recv_sems, only_sends=False):
    x, y, c, _ = _place()
    sends, arrivals = [], []
    for rel in range(1, 8):
        dx, dy, dc = rel >> 2 & 1, rel >> 1 & 1, rel & 1
        sends.append(pltpu.make_async_remote_copy(
            src_ref=in_ref.at[2 * (x ^ dx) + (y ^ dy)], dst_ref=out_ref.at[4 * x + 2 * y + c],
            send_sem=send_sems.at[rel - 1], recv_sem=recv_sems.at[rel - 1],
            device_id=(x ^ dx, y ^ dy, c ^ dc), device_id_type=MESH))
        if not only_sends:
            arrivals.append(pltpu.make_async_remote_copy(
                src_ref=in_ref.at[0], dst_ref=out_ref.at[4 * (x ^ dx) + 2 * (y ^ dy) + (c ^ dc)],
                send_sem=send_sems.at[rel - 1], recv_sem=recv_sems.at[rel - 1],
                device_id=(x, y, c), device_id_type=MESH))
    return sends, arrivals


def _exchange8_start(in_ref, out_ref, send_sems, recv_sems):
    for cp in _exchange8_copies(in_ref, out_ref, send_sems, recv_sems, only_sends=True)[0]:
        cp.start()


def _exchange8_finish(in_ref, out_ref, send_sems, recv_sems):
    sends, arrivals = _exchange8_copies(in_ref, out_ref, send_sems, recv_sems)
    for cp in arrivals:
        cp.wait_recv()
    _wait_all([], sends)


def _rs_sibling_copies(ins, outs, send_sems, recv_sems):
    c = lax.axis_index("c")
    return [pltpu.make_async_remote_copy(
        src_ref=ins[k].at[2 * s + 1 - c], dst_ref=outs[k].at[s], send_sem=send_sems.at[k, s], recv_sem=recv_sems.at[k, s],
        device_id=_sibling(), device_id_type=MESH) for k in range(len(ins)) for s in range(N_SHARD)]


def _rs_sibling_start(ins, outs, send_sems, recv_sems):
    for cp in _rs_sibling_copies(ins, outs, send_sems, recv_sems):
        cp.start()


def _rs_sibling_finish(ins, outs, send_sems, recv_sems):
    sends = _rs_sibling_copies(ins, outs, send_sems, recv_sems)
    for cp in sends:
        cp.wait_recv()
    _wait_all([], sends)


def _rs_sibling(grads):
    n = len(grads)

    def body(*refs):
        _rs_sibling_start(refs[:n], refs[n:2 * n], *refs[2 * n:])
        _rs_sibling_finish(refs[:n], refs[n:2 * n], *refs[2 * n:])

    return pl.pallas_call(
        body, name="rs_sibling", in_specs=[ANY] * n, out_specs=[ANY] * n,
        out_shape=[jax.ShapeDtypeStruct((N_SHARD,) + a.shape[1:], a.dtype) for a in grads],
        scratch_shapes=[pltpu.SemaphoreType.DMA((n, N_SHARD)), pltpu.SemaphoreType.DMA((n, N_SHARD))],
    )(*grads)


def _rs_chips_copies(ins, outs, send_sems, recv_sems):
    x, y, c, chips = _place()
    return [pltpu.make_async_remote_copy(
        src_ref=ins[k].at[2 * px + py], dst_ref=outs[k].at[j], send_sem=send_sems.at[k, j], recv_sem=recv_sems.at[k, j],
        device_id=(px, py, c), device_id_type=MESH) for k in range(len(ins)) for j, (px, py) in enumerate(chips)]


def _rs_chips_start(ins, outs, send_sems, recv_sems):
    for cp in _rs_chips_copies(ins, outs, send_sems, recv_sems):
        cp.start()


def _rs_chips_finish(ins, outs, send_sems, recv_sems):
    sends = _rs_chips_copies(ins, outs, send_sems, recv_sems)
    for cp in sends:
        cp.wait_recv()
    _wait_all([], sends)


def _rs_finish(bufs):
    n = len(bufs)

    def body(*refs):
        outs = refs[n:2 * n]
        send_sems, recv_sems = refs[2 * n:]
        c = lax.axis_index("c")
        sends = []
        for k in range(n):
            cp = pltpu.make_async_remote_copy(
                src_ref=outs[k].at[c], dst_ref=outs[k].at[c], send_sem=send_sems.at[k], recv_sem=recv_sems.at[k],
                device_id=_sibling(), device_id_type=MESH)
            cp.start()
            sends.append(cp)
        for k in range(n):
            pltpu.make_async_remote_copy(
                src_ref=outs[k].at[c], dst_ref=outs[k].at[1 - c], send_sem=send_sems.at[k], recv_sem=recv_sems.at[k],
                device_id=_sibling(), device_id_type=MESH).wait_recv()
        _wait_all([], sends)

    return pl.pallas_call(
        body, name="rs_finish", in_specs=[ANY] * n, out_specs=[ANY] * n,
        out_shape=[jax.ShapeDtypeStruct(a.shape, a.dtype) for a in bufs], input_output_aliases={k: k for k in range(n)},
        scratch_shapes=[pltpu.SemaphoreType.DMA((n,)), pltpu.SemaphoreType.DMA((n,))],
    )(*bufs)


def _sum3d(name, terms, scalars, grid_lead, out_lead, out_index, tr=None, out_dtype=F32):
    h, c = terms[0][0].shape[1:]
    tr = tr or h

    def body(s_ref, *refs):
        acc = refs[0][...].astype(F32)
        for ref in refs[1:-1]:
            acc = acc + ref[...].astype(F32)
        refs[-1][...] = acc.astype(refs[-1].dtype)

    in_specs = [pl.BlockSpec((1, tr, c), functools.partial(lambda l, i, s_ref, f: (f(l, s_ref), i, 0), f=f)) for _, f in terms]
    out_spec = pl.BlockSpec((1, tr, c), lambda l, i, s_ref: (out_index(l, s_ref), i, 0))
    return pl.pallas_call(
        body, name=name,
        grid_spec=pltpu.PrefetchScalarGridSpec(num_scalar_prefetch=1, grid=(grid_lead, h // tr), in_specs=in_specs, out_specs=out_spec),
        out_shape=jax.ShapeDtypeStruct((out_lead, h, c), out_dtype), compiler_params=_params(("arbitrary", "arbitrary")),
    )(scalars, *[a for a, _ in terms])


def _halves(grads):
    return [a.reshape(2 * N_SHARD, a.shape[0] // (2 * N_SHARD), a.shape[1]) for a in grads]


def _rs_stage1(g8, place, tag, from_sibling=None):
    from_sibling = _rs_sibling(g8) if from_sibling is None else from_sibling
    parts = [_sum3d(f"rs_add1_{tag}{k}", [(g8[k], lambda l, s: 2 * l + s[0]), (from_sibling[k], lambda l, s: l)], place,
                    N_SHARD, N_SHARD, lambda l, s: l, tr=g8[k].shape[1] // 2, out_dtype=BF16)
             for k in range(len(g8))]
    return g8, from_sibling, parts


def _rs_stage3(stage1, from_chips, place, tag):
    g8, from_sibling, _ = stage1
    mine = [(lambda l, s: 2 * s[1] + s[0]), (lambda l, s: s[1])]
    bufs = [_sum3d(f"rs_add2_{tag}{k}", [(g8[k], mine[0]), (from_sibling[k], mine[1])]
                   + [(from_chips[k], functools.partial(lambda l, s, j: j, j=j)) for j in range(3)],
                   place, 1, 2, lambda l, s: s[0], tr=g8[k].shape[1] // 2)
            for k in range(len(g8))]
    whole = _rs_finish(bufs)
    return [w.reshape(2 * w.shape[1], w.shape[2]) for w in whole]


_LATE = ["w_a", "w_b", "w_out", "w_down", "w_up_t"]


def _device_step(x, target, p, late_shards, late_placed, place):
    t = x.shape[0]
    d = D_MODEL
    tok = lambda arr, c=0, w=d: (arr, w, c)
    f32x = lambda n: [(d, F32)] * n
    rp_params = [p["w0"], p["w2p"], p["a0"], p["a2p"], p["g2p"], p["k_k"], p["k_a"]]
    post_params = [p["ln_w"], p["ln_b"], p["r_k"]]
    g = {}

    (xn,) = _tok_fwd("norm1_fwd", _fn_norm, [tok(x)], [p["g1"]], [(d, BF16)], tm=LIGHT_TM)
    z = _mm("in_proj", xn, p["w_in_t"], "nt", tm=t, tn=256)
    q_in, k_in, kd, qe, dec = _tok_fwd("hgates_fwd", _fn_hgates, [tok(z, 0), tok(z, 1)], [p["lb2"]], f32x(5))
    o_intra, u = _hgrn_local_fwd(q_in, k_in, kd, z)
    o_raw, h_states = _hgrn_state_fwd(o_intra, qe, dec, u)
    zl = _lerp_fwd(z, p["mu"])
    lora = tok(zl, 3 * d // LORA, LORA)
    decay, kr2, avec, bvec, gate = _tok_fwd("rprep_fwd", _fn_rprep, [tok(zl, 1), lora], rp_params, f32x(5))
    y, r_states, late = _rwkv_fwd(zl, decay, kr2, avec, bvec, late_shards, late_placed)
    p = dict(p, **dict(zip(_LATE, late)))
    (o_a,) = _tok_fwd("hpost_fwd", _fn_hpost, [tok(o_raw), tok(z, 3)], [p["gnorm"]], [(d, BF16)])
    post_toks = [tok(y), tok(zl, 0), tok(kr2), tok(zl, 2), tok(gate)]
    (o_b,) = _tok_fwd("rpost_fwd", _fn_rpost, post_toks, post_params, [(d, BF16)])
    y_a = _mm("branch_a", o_a, p["w_a"], "nn")
    y_b = _mm("branch_b", o_b, p["w_b"], "nn")
    merge_toks = [tok(z, C_G // 256, 256), tok(z, (C_G + d) // 256, 256), tok(y_a, 0, 256), tok(y_b, 0, 256)]
    (merged,) = _tok_fwd("merge_fwd", _fn_merge, merge_toks, [], [(256, BF16)], col_grid=4, tm=512)
    mix = _mm("out_proj", merged, p["w_out"], "nn")
    h1, xn2 = _tok_fwd("res1_fwd", _fn_res1, [tok(x), tok(mix)], [p["g_post1"], p["g_pre2"]], [(d, F32), (d, BF16)],
                       tm=LIGHT_TM)
    hu = _mm("up_proj", xn2, p["w_up_t"], "nt", tm=t, tn=512)
    act = _conv_fwd(hu, p["conv_w"], p["conv_b"])
    ff = _mm("down_proj", act, p["w_down"], "nn")
    d_h1, d_ff, loss, g["g_post2"] = _loss_head(h1, ff, target, p["g_post2"])

    d_act = _mm("d_act", d_ff, p["w_down"], "nt")
    g["w_down"] = _mm("dw_down", act, d_ff, "tn", tm=256, tn=1024)
    d_hu, d_cw, d_cb = _conv_bwd(hu, p["conv_w"], p["conv_b"], d_act)
    g["conv_w"], g["conv_b"] = d_cw.transpose(1, 0, 2).reshape(3, 2 * D_FF), d_cb.reshape(1, 2 * D_FF)
    d_hu = d_hu.reshape(2 * t, D_FF)
    d_xn2 = _mm("d_xn2", d_hu, p["w_up_t"], "nn", tm=t, tk=D_FF, mk=(t, 2 * D_FF), a_map=lambda i, j, q: (q, 0))
    g["w_up_t"] = _mm("dw_up", d_hu, xn2, "tn", tm=CONV_TILE, tn=1024, mk=(2 * D_FF, t),
                      a_map=lambda i, j, q: (i // N_CONV_TILES, i % N_CONV_TILES))
    d_x_res, d_mix, g["g_post1"], g["g_pre2"] = _tok_bwd(
        "res1_bwd", _fn_res1, [tok(x), tok(mix)], [p["g_post1"], p["g_pre2"]], [[tok(d_h1)], [tok(d_xn2)]],
        [(d, F32), (d, BF16)], tm=LIGHT_TM)
    d_merged = _mm("d_merged", d_mix, p["w_out"], "nt")
    g["w_out"] = _mm("dw_out", merged, d_mix, "tn")
    d_ga, d_gb, d_ya, d_yb = _tok_bwd("merge_bwd", _fn_merge, merge_toks, [], [[tok(d_merged, 0, 256)]],
                                      [(256, BF16)] * 4, col_grid=4, tm=512)
    d_oa = _mm("d_oa", d_ya, p["w_a"], "nt")
    g["w_a"] = _mm("dw_a", o_a, d_ya, "tn")
    d_ob = _mm("d_ob", d_yb, p["w_b"], "nt")
    g["w_b"] = _mm("dw_b", o_b, d_yb, "tn")
    d_oraw, d_hg, g["gnorm"] = _tok_bwd("hpost_bwd", _fn_hpost, [tok(o_raw), tok(z, 3)], [p["gnorm"]], [[tok(d_oa)]],
                                        [(d, F32), (d, BF16)])
    late_g8 = _halves([g[n] for n in _LATE])
    rpost = _tok_bwd("rpost_bwd", _fn_rpost, post_toks, post_params, [[tok(d_ob)]], f32x(5), sibling_rider=late_g8)
    d_y, d_r1, d_kr2_1, d_v1, d_gate, g["ln_w"], g["ln_b"], g["r_k"] = rpost[:8]
    stage1 = _rs_stage1(late_g8, place, "late", from_sibling=rpost[8:])
    (d_r2, d_decay, d_kr2_2, d_v2, d_avec, d_bvec), from_chips = _rwkv_bwd(
        zl, decay, kr2, avec, bvec, r_states, d_y, stage1[2])
    g.update(zip(_LATE, _rs_stage3(stage1, from_chips, place, "late")))
    prep = _tok_bwd("rprep_bwd", _fn_rprep, [tok(zl, 1), lora], rp_params,
                    [[tok(d_decay)], [tok(d_kr2_1), tok(d_kr2_2)], [tok(d_avec)], [tok(d_bvec)], [tok(d_gate)]],
                    [(d, F32), (LORA, F32)])
    d_kr, d_lora = prep[:2]
    g["w0"], g["w2p"], g["a0"], g["a2p"], g["g2p"], g["k_k"], g["k_a"] = prep[2:]
    dz_r, g["mu"] = _lerp_bwd(z, p["mu"], (d_r1, d_r2), d_kr, (d_v1, d_v2), d_lora)
    d_qe, d_dec, d_u = _hgrn_state_bwd(d_oraw, qe, dec, h_states)
    d_q_in, d_k_in, d_kd, d_vi = _hgrn_local_bwd(q_in, k_in, kd, z, d_oraw, d_u)
    d_hq, d_hf, g["lb2"] = _tok_bwd("hgates_bwd", _fn_hgates, [tok(z, 0), tok(z, 1)], [p["lb2"]],
                                    [[tok(d_q_in)], [tok(d_k_in)], [tok(d_kd)], [tok(d_qe)], [tok(d_dec)]], [(d, BF16)] * 2)
    dz = jnp.concatenate([d_hq, d_hf, d_vi.astype(BF16), d_hg, dz_r, d_ga, d_gb], axis=1)
    stage1 = _rs_stage1(_halves([_mm("dw_in", dz, xn, "tn", tm=256, tn=1024)]), place, "w_in")
    d_xn, from_chips = _mm("d_xn", dz, p["w_in_t"], "nn", tm=1024, tn=512, tk=IN_COLS // 2, riders=stage1[2])
    (g["w_in_t"],) = _rs_stage3(stage1, from_chips, place, "w_in")
    grad_x, g["g1"] = _tok_bwd("norm1_bwd", _fn_norm, [tok(x)], [p["g1"]], [[tok(d_xn)]], [(d, F32)],
                               add_to_first=tok(d_x_res), tm=LIGHT_TM)
    return loss, grad_x, g


_WEIGHTS = ["attn_pre_norm", "w_in", "hgrn_lb", "hgrn_gnorm", "w_branch_a", "rwkv_mu", "rwkv_w0", "rwkv_w2", "rwkv_a0",
            "rwkv_a2", "rwkv_g2", "rwkv_k_k", "rwkv_k_a", "rwkv_r_k", "rwkv_ln_w", "rwkv_ln_b", "w_branch_b", "w_out",
            "attn_post_norm", "ffn_pre_norm", "w_up", "conv_w", "conv_b", "w_down", "ffn_post_norm"]
_REPLICATED = [("attn_pre_norm", "g1"), ("hgrn_lb", "lb2"), ("hgrn_gnorm", "gnorm"), ("rwkv_mu", "mu"), ("rwkv_w0", "w0"),
               ("rwkv_a0", "a0"), ("rwkv_k_k", "k_k"), ("rwkv_k_a", "k_a"), ("rwkv_r_k", "r_k"), ("rwkv_ln_w", "ln_w"),
               ("rwkv_ln_b", "ln_b"), ("attn_post_norm", "g_post1"), ("ffn_pre_norm", "g_pre2"), ("conv_b", "conv_b"),
               ("ffn_post_norm", "g_post2")]
SLAB_COLS = 1024


def _pack(arrays):
    pieces, total = [], 0
    for a in arrays:
        flat = a.reshape(-1)
        rows = -(-flat.shape[0] // SLAB_COLS)
        pieces.append(jnp.pad(flat, (0, rows * SLAB_COLS - flat.shape[0])).reshape(rows, SLAB_COLS))
        total += rows
    if total % 8:
        pieces.append(jnp.zeros((8 - total % 8, SLAB_COLS), F32))
    return jnp.concatenate(pieces, axis=0)


def _unpack(slab, shapes):
    out, at = [], 0
    for s in shapes:
        size = 1
        for dim in s:
            size *= dim
        rows = -(-size // SLAB_COLS)
        out.append(slab[at:at + rows].reshape(-1)[:size].reshape(s))
        at += rows
    return out


def kernel(x, attn_pre_norm, w_in, hgrn_lb, hgrn_gnorm, w_branch_a, rwkv_mu, rwkv_w0, rwkv_w2, rwkv_a0, rwkv_a2, rwkv_g2, rwkv_k_k, rwkv_k_a, rwkv_r_k, rwkv_ln_w, rwkv_ln_b, w_branch_b, w_out, attn_post_norm, ffn_pre_norm, w_up, conv_w, conv_b, w_down, ffn_post_norm, loss_target, m_attn_pre_norm, m_w_in, m_hgrn_lb, m_hgrn_gnorm, m_w_branch_a, m_rwkv_mu, m_rwkv_w0, m_rwkv_w2, m_rwkv_a0, m_rwkv_a2, m_rwkv_g2, m_rwkv_k_k, m_rwkv_k_a, m_rwkv_r_k, m_rwkv_ln_w, m_rwkv_ln_b, m_w_branch_b, m_w_out, m_attn_post_norm, m_ffn_pre_norm, m_w_up, m_conv_w, m_conv_b, m_w_down, m_ffn_post_norm, v_attn_pre_norm, v_w_in, v_hgrn_lb, v_hgrn_gnorm, v_w_branch_a, v_rwkv_mu, v_rwkv_w0, v_rwkv_w2, v_rwkv_a0, v_rwkv_a2, v_rwkv_g2, v_rwkv_k_k, v_rwkv_k_a, v_rwkv_r_k, v_rwkv_ln_w, v_rwkv_ln_b, v_w_branch_b, v_w_out, v_attn_post_norm, v_ffn_pre_norm, v_w_up, v_conv_w, v_conv_b, v_w_down, v_ffn_post_norm):
    given = dict(locals())
    w = {n: given[n] for n in _WEIGHTS}
    mom = {n: given["m_" + n] for n in _WEIGHTS}
    var = {n: given["v_" + n] for n in _WEIGHTS}
    shard = 2 * lax.axis_index("x") + lax.axis_index("y")
    place = jnp.stack([lax.axis_index("c"), shard]).astype(jnp.int32)
    row = lambda a: a.reshape(1, -1)
    lora_of = lambda d: jnp.concatenate([d["rwkv_w2"][0], d["rwkv_a2"][0], d["rwkv_g2"][0]], axis=0)

    shards = [w["w_in"][0].T.astype(BF16), lora_of(w), jnp.pad(w["conv_w"][0], ((0, 29), (0, 0)))]
    late_shards = [w["w_branch_a"][0].astype(BF16), w["w_branch_b"][0].astype(BF16), w["w_out"][0].astype(BF16),
                   w["w_down"][0].astype(BF16), w["w_up"][0].T.astype(BF16)]
    placed = [_place_shard(f"place_{k}", a, place) for k, a in enumerate(shards)]
    late_placed = [_place_shard(f"place_late_{k}", a, place) for k, a in enumerate(late_shards)]
    w_in_t, lora_g, conv_g = _gather_shards(shards, placed)
    lora_full = lora_g.reshape(N_SHARD, LORA, 256).transpose(1, 0, 2).reshape(LORA, D_MODEL)
    conv_full = conv_g.reshape(N_SHARD, 32, 2 * D_FF // N_SHARD)[:, :3].transpose(1, 0, 2).reshape(3, 2 * D_FF)
    lrow = lax.broadcasted_iota(jnp.int32, (LORA, 1), 0)
    p = {
        "g1": row(w["attn_pre_norm"]), "lb2": w["hgrn_lb"], "gnorm": row(w["hgrn_gnorm"]), "w_in_t": w_in_t,
        "mu": row(w["rwkv_mu"]), "w0": row(w["rwkv_w0"]), "a0": row(w["rwkv_a0"]),
        "w2p": jnp.where(lrow < 64, lora_full, 0.0), "a2p": jnp.where((lrow >= 64) & (lrow < 128), lora_full, 0.0),
        "g2p": jnp.where(lrow >= 128, lora_full, 0.0),
        "k_k": row(w["rwkv_k_k"]), "k_a": row(w["rwkv_k_a"]), "r_k": row(w["rwkv_r_k"]), "ln_w": row(w["rwkv_ln_w"]),
        "ln_b": row(w["rwkv_ln_b"]), "g_post1": row(w["attn_post_norm"]),
        "g_pre2": row(w["ffn_pre_norm"]), "conv_w": conv_full, "conv_b": row(w["conv_b"]),
        "g_post2": row(w["ffn_post_norm"]),
    }

    loss, grad_x, g = _device_step(x[0], loss_target[0], p, late_shards, late_placed, place)

    g_in_t = g["w_in_t"]
    g_a, g_b, g_o, g_dn, g_up_t = [g[n] for n in _LATE]
    rep_shapes = [w[n].shape for n, _ in _REPLICATED]
    rep = _pack([g[key] for _, key in _REPLICATED])
    n_rep_rows = rep.shape[0]
    cw = 2 * D_FF // N_SHARD
    lora_rows, conv_rows = LORA * 256 // SLAB_COLS, -(-3 * cw // SLAB_COLS)
    lora_g = jnp.concatenate([g["w2p"][0:64], g["a2p"][64:128], g["g2p"][128:256]], axis=0)
    lora_parts = lora_g.reshape(LORA, N_SHARD, 256).transpose(1, 0, 2).reshape(N_SHARD, lora_rows, SLAB_COLS)
    conv_parts = g["conv_w"].reshape(3, N_SHARD, cw).transpose(1, 0, 2).reshape(N_SHARD, 3 * cw)
    conv_parts = jnp.pad(conv_parts, ((0, 0), (0, conv_rows * SLAB_COLS - 3 * cw))).reshape(N_SHARD, conv_rows, SLAB_COLS)
    n_rows = n_rep_rows + lora_rows + conv_rows
    fill = jnp.zeros((N_SHARD, -n_rows % 8, SLAB_COLS), F32)
    parts = jnp.concatenate([jnp.broadcast_to(rep, (N_SHARD,) + rep.shape), lora_parts, conv_parts, fill], axis=1)
    me = 4 * lax.axis_index("x") + 2 * lax.axis_index("y") + lax.axis_index("c")
    landing = lax.dynamic_update_slice(jnp.zeros((8,) + parts.shape[1:], F32),
                                       lax.dynamic_index_in_dim(parts, shard, 0, keepdims=True), (me, 0, 0))

    res = {}

    def put(name, outs, shape=None):
        res[name] = [o.reshape(w[name].shape if shape is None else shape) for o in outs]

    w_in_out, gathered = _adamw("adamw_w_in", w["w_in"][0], g_in_t, mom["w_in"][0], var["w_in"][0], 1024, 128,
                                g_transposed=True, exchange=(parts, landing))
    put("w_in", w_in_out)
    summed = _sum3d("small_sum", [(gathered, functools.partial(lambda l, s, i: i, i=i)) for i in range(8)], place, 1, 1,
                    lambda l, s: 0)[0]
    lora_grad = summed[n_rep_rows:n_rep_rows + lora_rows].reshape(LORA, 256)
    conv_grad = summed[n_rep_rows + lora_rows:n_rows].reshape(-1)[:3 * cw].reshape(3, cw)
    put("w_up", _adamw("adamw_w_up", w["w_up"][0], g_up_t, mom["w_up"][0], var["w_up"][0], 1024, 128, g_transposed=True))
    for name, grad in (("w_branch_a", g_a), ("w_branch_b", g_b), ("w_out", g_o)):
        put(name, _adamw("adamw_" + name, w[name][0], grad, mom[name][0], var[name][0], 256, 1024))
    put("w_down", _adamw("adamw_w_down", w["w_down"][0], g_dn, mom["w_down"][0], var["w_down"][0], 176, 1024))
    put("conv_w", _adamw("adamw_conv_w", w["conv_w"][0], conv_grad, mom["conv_w"][0], var["conv_w"][0], 3, 2 * D_FF // N_SHARD))
    lora_out = _adamw("adamw_lora", lora_of(w), lora_grad, lora_of(mom), lora_of(var), LORA, 256)
    for name, lo, hi in (("rwkv_w2", 0, 64), ("rwkv_a2", 64, 128), ("rwkv_g2", 128, 256)):
        put(name, [o[lo:hi] for o in lora_out])
    rep_names = [n for n, _ in _REPLICATED]
    rep_out = _adamw("adamw_small", _pack([w[n] for n in rep_names]), summed[:n_rep_rows], _pack([mom[n] for n in rep_names]),
                     _pack([var[n] for n in rep_names]), n_rep_rows, SLAB_COLS)
    for name, parts in zip(rep_names, zip(*[_unpack(o, rep_shapes) for o in rep_out])):
        put(name, list(parts))

    loss = lax.psum(loss[0, 0], ("x", "y", "c"))
    return (loss, grad_x[None], *[res[n][0] for n in _WEIGHTS], *[res[n][1] for n in _WEIGHTS],
            *[res[n][2] for n in _WEIGHTS], *[res[n][3] for n in _WEIGHTS])
```

```python
import functools

import jax
import jax.numpy as jnp
from jax import lax
from jax.experimental import pallas as pl
from jax.experimental.pallas import tpu as pltpu

F32, BF16 = jnp.float32, jnp.bfloat16
MESH = pl.DeviceIdType.MESH

D_MODEL = 1024
HGRN_HEADS = 8
HGRN_K = 128
HGRN_SCALE = HGRN_K ** -0.5
CHUNK = 32
RWKV_HEAD = 64
LORA = 256
D_FF = 2816
EPS = 1e-6
GN_EPS = 1e-5 * RWKV_HEAD
N_SHARD = 4
ADAM_LR, ADAM_B1, ADAM_B2, ADAM_EPS, ADAM_WD, ADAM_STEP = 0.001, 0.9, 0.999, 1e-08, 0.01, 10

LANES = 128
VMEM_LIMIT = 56 * 1024 * 1024
SCAN_TB = 16
SCAN_GROUP = 256
LIGHT_TM = 256

C_HQ, C_HF, C_HI, C_HG = 0, 1024, 2048, 3072
C_R = 4096
R_COLS = 3328
C_G = 7424
IN_COLS = 9472


def _params(sem=None, **kw):
    return pltpu.CompilerParams(dimension_semantics=sem, vmem_limit_bytes=VMEM_LIMIT, **kw)


def _seg_matrix(n, seg):
    r = lax.broadcasted_iota(jnp.int32, (n, n), 0) // seg
    c = lax.broadcasted_iota(jnp.int32, (n, n), 1) // seg
    return (r == c).astype(BF16)


def _split3(x):
    hi = x.astype(BF16)
    r1 = x - hi.astype(F32)
    mid = r1.astype(BF16)
    lo = (r1 - mid.astype(F32)).astype(BF16)
    return hi, mid, lo


def _segsum_impl(x, seg):
    e = _seg_matrix(LANES, seg)
    outs = []
    for g in range(x.shape[1] // LANES):
        hi, mid, lo = _split3(x[:, g * LANES:(g + 1) * LANES])
        outs.append(jnp.dot(hi, e, preferred_element_type=F32) + jnp.dot(mid, e, preferred_element_type=F32)
                    + jnp.dot(lo, e, preferred_element_type=F32))
    return outs[0] if len(outs) == 1 else jnp.concatenate(outs, axis=1)


def _make_segsum(seg):
    @jax.custom_vjp
    def f(x):
        return _segsum_impl(x, seg)

    f.defvjp(lambda x: (_segsum_impl(x, seg), None), lambda _, ct: (_segsum_impl(ct, seg),))
    return f


_segsum64 = _make_segsum(RWKV_HEAD)
_segsum128 = _make_segsum(HGRN_K)


def _chunk_mm_impl(x, kind, transposed):
    n = x.shape[0]
    r = lax.broadcasted_iota(jnp.int32, (n, n), 1 if transposed else 0)
    c = lax.broadcasted_iota(jnp.int32, (n, n), 0 if transposed else 1)
    same = (r // CHUNK) == (c // CHUNK)
    if kind == "cumsum":
        m = same & (r >= c)
    else:
        m = same & (c % CHUNK == (CHUNK // 2 - 1 if kind == "mid" else CHUNK - 1))
    m = m.astype(BF16)
    hi, mid, lo = _split3(x)
    return (jnp.dot(m, hi, preferred_element_type=F32) + jnp.dot(m, mid, preferred_element_type=F32)
            + jnp.dot(m, lo, preferred_element_type=F32))


def _make_chunk_mm(kind):
    @jax.custom_vjp
    def f(x):
        return _chunk_mm_impl(x, kind, False)

    f.defvjp(lambda x: (_chunk_mm_impl(x, kind, False), None), lambda _, ct: (_chunk_mm_impl(ct, kind, True),))
    return f


_chunk_cumsum = _make_chunk_mm("cumsum")
_chunk_mid = _make_chunk_mm("mid")
_chunk_last = _make_chunk_mm("last")


@jax.custom_vjp
def _bdot(x, w):
    return jnp.dot(x.astype(BF16), w.astype(BF16), preferred_element_type=F32)


def _bdot_fwd(x, w):
    return _bdot(x, w), (x, w)


def _bdot_bwd(res, ct):
    x, w = res
    ctb = ct.astype(BF16)
    dx = lax.dot_general(ctb, w.astype(BF16), (((1,), (1,)), ((), ())), preferred_element_type=F32)
    dw = lax.dot_general(x.astype(BF16), ctb, (((0,), (0,)), ((), ())), preferred_element_type=F32)
    return dx, dw


_bdot.defvjp(_bdot_fwd, _bdot_bwd)


def _sigmoid(x):
    return 1.0 / (1.0 + jnp.exp(-x))


def _silu(x):
    return x * _sigmoid(x)


def _softplus(x):
    return jnp.maximum(x, 0.0) + jnp.log(1.0 + jnp.exp(-jnp.abs(x)))


def _rms(x, g):
    return x * lax.rsqrt(jnp.mean(x * x, axis=-1, keepdims=True) + EPS) * g


def _fn_norm(t, p):
    return [_rms(t[0], p[0])]


def _fn_hgates(t, p):
    hq, hf = t
    lb2 = p[0]
    m = jnp.max(lb2, axis=0, keepdims=True)
    e = jnp.exp(lb2 - m)
    first = lax.broadcasted_iota(jnp.int32, e.shape, 0) == 0
    lb = jnp.sum(jnp.where(first, e, 0.0), axis=0, keepdims=True) / jnp.sum(e, axis=0, keepdims=True)
    f = lb + (1.0 - lb) * _sigmoid(hf)
    q, k = _silu(hq) * HGRN_SCALE, 1.0 - f
    b = _chunk_cumsum(jnp.log(f))
    b_ref, b_last = _chunk_mid(b), _chunk_last(b)
    return [q * jnp.exp(b - b_ref), k * jnp.exp(b_ref - b), k * jnp.exp(b_last - b), q * jnp.exp(b), jnp.exp(b_last)]


def _fn_hpost(t, p):
    o, hg = t
    ms = _segsum128(o * o) * (1.0 / HGRN_K)
    return [o * lax.rsqrt(ms + EPS) * p[0] * _silu(hg)]


def _fn_rprep(t, p):
    kr, lora = t
    w0, w2p, a0, a2p, g2p, k_k, k_a = p
    pre_w = w0 + _bdot(jnp.tanh(lora), w2p)
    w_log = -_softplus(-pre_w) - 0.5
    decay = jnp.exp(-jnp.exp(w_log))
    a = _sigmoid(a0 + _bdot(lora, a2p))
    g = _bdot(_sigmoid(lora), g2p)
    kk = kr * k_k
    kk = kk / jnp.maximum(jnp.sqrt(_segsum64(kk * kk)), 1e-12)
    kr2 = kr * (1.0 + (a - 1.0) * k_a)
    return [decay, kr2, -kk, kk * a, g]


def _fn_rpost(t, p):
    y, r, kr2, v, g = t
    ln_w, ln_b, r_k = p
    mu = _segsum64(y) * (1.0 / RWKV_HEAD)
    yc = y - mu
    var = _segsum64(yc * yc) * (1.0 / RWKV_HEAD)
    yn = yc * lax.rsqrt(var + GN_EPS) * ln_w + ln_b
    bonus = _segsum64(r * kr2 * r_k) * v
    return [(yn + bonus) * g]


def _fn_merge(t, p):
    ga, gb, ya, yb = t
    return [_sigmoid(ga) * ya + _sigmoid(gb) * yb]


def _fn_res1(t, p):
    x, mix = t
    h1 = x + _rms(mix, p[0])
    return [h1, _rms(h1, p[1])]


def _tok_call(name, fn, toks, params, outs, red_shapes=(), tm=128, col_grid=1, sibling_rider=None):
    n_t, n_p, n_o, n_red = len(toks), len(params), len(outs), len(red_shapes)
    n_r = 0 if sibling_rider is None else len(sibling_rider)
    t_len = toks[0][0].shape[0]
    tm = min(tm, t_len)
    grid = (t_len // tm, col_grid)

    def body(*refs):
        first_out = n_t + n_p + n_r
        if n_r:
            rider = (refs[n_t + n_p:first_out], refs[first_out + n_o + n_red:first_out + n_o + n_red + n_r], *refs[-2:])

            @pl.when((pl.program_id(0) == 0) & (pl.program_id(1) == 0))
            def _():
                _rs_sibling_start(*rider)

        tv = [r[...].astype(F32) for r in refs[:n_t]]
        pv = [r[...] for r in refs[n_t:n_t + n_p]]
        o, red = fn(tv, pv)
        for ref, val in zip(refs[first_out:first_out + n_o], o):
            ref[...] = val.astype(ref.dtype)
        if n_r:
            @pl.when((pl.program_id(0) == grid[0] - 1) & (pl.program_id(1) == grid[1] - 1))
            def _():
                _rs_sibling_finish(*rider)

        red_refs = refs[first_out + n_o:first_out + n_o + n_red]
        if red_refs:
            first = pl.program_id(0) == 0

            @pl.when(first)
            def _():
                for ref, val in zip(red_refs, red):
                    ref[...] = val

            @pl.when(jnp.logical_not(first))
            def _():
                for ref, val in zip(red_refs, red):
                    ref[...] += val

    in_specs = [pl.BlockSpec((tm, w), functools.partial(lambda i, j, c: (i, c + j), c=c)) for (_, w, c) in toks]
    in_specs += [pl.BlockSpec(p.shape, lambda i, j: (0, 0)) for p in params]
    out_specs = [pl.BlockSpec((tm, w), lambda i, j: (i, j)) for (w, _) in outs]
    out_specs += [pl.BlockSpec(s, lambda i, j: (0, 0)) for s in red_shapes]
    out_shape = [jax.ShapeDtypeStruct((t_len, w * col_grid), dt) for (w, dt) in outs]
    out_shape += [jax.ShapeDtypeStruct(s, F32) for s in red_shapes]
    scratch = []
    if n_r:
        in_specs += [ANY] * n_r
        out_specs += [ANY] * n_r
        out_shape += [jax.ShapeDtypeStruct((N_SHARD,) + a.shape[1:], a.dtype) for a in sibling_rider]
        scratch = [pltpu.SemaphoreType.DMA((n_r, N_SHARD)), pltpu.SemaphoreType.DMA((n_r, N_SHARD))]
    return pl.pallas_call(
        body, name=name, grid=grid, in_specs=in_specs, out_specs=out_specs, out_shape=out_shape, scratch_shapes=scratch,
        compiler_params=_params(("arbitrary", "arbitrary")),
    )(*[a for (a, _, _) in toks], *params, *(sibling_rider or []))


def _tok_fwd(name, fn, toks, params, outs, **kw):
    return _tok_call(name, lambda tv, pv: (fn(tv, pv), []), toks, params, outs, **kw)


def _tok_bwd(name, fn, toks, params, cts, want, add_to_first=None, **kw):
    n_t = len(toks)
    flat = [c for group in cts for c in group]
    extra = [] if add_to_first is None else [add_to_first]

    def bwd(tv, pv):
        prim, rest = tv[:n_t], tv[n_t:]
        ct, at = [], 0
        for group in cts:
            ct.append(functools.reduce(lambda u, v: u + v, rest[at:at + len(group)]))
            at += len(group)
        _, vjp = jax.vjp(lambda *a: tuple(fn(list(a[:n_t]), list(a[n_t:]))), *prim, *pv)
        g = vjp(tuple(ct))
        tok_grads = [g[i] for i in range(n_t) if want[i] is not None]
        if extra:
            tok_grads[0] = tok_grads[0] + rest[at]
        return tok_grads, list(g[n_t:])

    return _tok_call(name, bwd, list(toks) + flat + extra, params, [w for w in want if w is not None],
                     red_shapes=[p.shape for p in params], **kw)


def _mm(name, a, b, mode, out_dtype=F32, tm=None, tn=None, tk=None, riders=None, a_map=None, mk=None):
    if mode == "nn":
        (m, k), (_, n) = a.shape, b.shape
    elif mode == "nt":
        (m, k), (n, _) = a.shape, b.shape
    else:
        (k, m), (_, n) = a.shape, b.shape
    if mk is not None:
        m, k = mk
    tm = (512 if mode == "tn" else 2048) if tm is None else tm
    tn = (512 if mode == "tn" else 256) if tn is None else tn
    tk = k if tk is None else tk
    tm, tn = min(tm, m), min(tn, n)
    nk = k // tk
    assert m % tm == 0 and n % tn == 0 and k % tk == 0, (name, a.shape, b.shape, tm, tn, tk)
    a_spec = pl.BlockSpec((tk, tm), lambda i, j, q: (q, i)) if mode == "tn" else pl.BlockSpec((tm, tk), lambda i, j, q: (i, q))
    if a_map is not None:
        a_spec = pl.BlockSpec(a_spec.block_shape, a_map)
    b_spec = pl.BlockSpec((tn, tk), lambda i, j, q: (j, q)) if mode == "nt" else pl.BlockSpec((tk, tn), lambda i, j, q: (q, j))
    dn = {"nn": (((1,), (0,)), ((), ())), "nt": (((1,), (1,)), ((), ())), "tn": (((0,), (0,)), ((), ()))}[mode]
    grid = (m // tm, n // tn, nk)
    nr = 0 if riders is None else len(riders)

    def body(*refs):
        a_ref, b_ref, o_ref = refs[0], refs[1], refs[2 + nr]
        acc = refs[3 + 2 * nr] if nk > 1 else None
        if nr:
            exchange = (refs[2:2 + nr], refs[3 + nr:3 + 2 * nr], *refs[-2:])
            at = [pl.program_id(ax) for ax in range(3)]

            @pl.when((at[0] == 0) & (at[1] == 0) & (at[2] == 0))
            def _():
                _rs_chips_start(*exchange)

        p = lax.dot_general(a_ref[...], b_ref[...], dn, preferred_element_type=F32)
        if nk == 1:
            o_ref[...] = p.astype(o_ref.dtype)
        else:
            q = pl.program_id(2)

            @pl.when(q == 0)
            def _():
                acc[...] = p

            @pl.when(q > 0)
            def _():
                acc[...] += p

            @pl.when(q == nk - 1)
            def _():
                o_ref[...] = acc[...].astype(o_ref.dtype)

        if nr:
            @pl.when((at[0] == grid[0] - 1) & (at[1] == grid[1] - 1) & (at[2] == grid[2] - 1))
            def _():
                _rs_chips_finish(*exchange)

    scratch = [pltpu.VMEM((tm, tn), F32)] if nk > 1 else []
    out_specs = [pl.BlockSpec((tm, tn), lambda i, j, q: (i, j))]
    out_shape = [jax.ShapeDtypeStruct((m, n), out_dtype)]
    if nr:
        scratch += [pltpu.SemaphoreType.DMA((nr, 3)), pltpu.SemaphoreType.DMA((nr, 3))]
        out_specs += [ANY] * nr
        out_shape += [jax.ShapeDtypeStruct((3,) + r.shape[1:], r.dtype) for r in riders]
    outs = pl.pallas_call(
        body, name=name, grid=grid, in_specs=[a_spec, b_spec] + [ANY] * nr, out_specs=out_specs, out_shape=out_shape,
        scratch_shapes=scratch,
        compiler_params=_params(("arbitrary",) * 3 if nr else ("parallel", "parallel", "arbitrary")),
    )(a, b, *(riders or []))
    return (outs[0], outs[1:]) if nr else outs[0]


def _shift_down(z, n):
    rows = lax.broadcasted_iota(jnp.int32, z.shape, 0)
    return jnp.where(rows < n, 0.0, pltpu.roll(z, n, 0))


def _shift_up(z, n):
    t = z.shape[0]
    rows = lax.broadcasted_iota(jnp.int32, z.shape, 0)
    return jnp.where(rows >= t - n, 0.0, pltpu.roll(z, t - n, 0))


def _lerp_fwd(z, mu):
    t = z.shape[0]
    w = 256

    def body(z_ref, mu_ref, o_ref):
        zz = z_ref[...]
        o_ref[...] = zz + mu_ref[...] * (_shift_down(zz, 1) - zz)

    return pl.pallas_call(
        body, name="lerp_fwd", grid=(R_COLS // w,),
        in_specs=[pl.BlockSpec((t, w), lambda j: (0, C_R // w + j)), pl.BlockSpec((1, w), lambda j: (0, j))],
        out_specs=pl.BlockSpec((t, w), lambda j: (0, j)), out_shape=jax.ShapeDtypeStruct((t, R_COLS), F32),
        compiler_params=_params(("parallel",)),
    )(z, mu)


def _lerp_bwd(z, mu, d_r, d_k, d_v, d_lora):
    t = z.shape[0]
    w = 256
    per = D_MODEL // w

    def body(z_ref, mu_ref, r1_ref, r2_ref, k_ref, v1_ref, v2_ref, l_ref, dz_ref, dmu_ref):
        j = pl.program_id(0)
        zz, m = z_ref[...], mu_ref[...]
        d = jnp.where(j < per, r1_ref[...] + r2_ref[...],
                      jnp.where(j < 2 * per, k_ref[...], jnp.where(j < 3 * per, v1_ref[...] + v2_ref[...], l_ref[...])))
        dz_ref[...] = (d * (1.0 - m) + _shift_up(d * m, 1)).astype(dz_ref.dtype)
        dmu_ref[...] = jnp.sum(d * (_shift_down(zz, 1) - zz), axis=0, keepdims=True)

    piece = lambda first: pl.BlockSpec((t, w), lambda j: (0, jnp.clip(j - first, 0, per - 1)))
    return pl.pallas_call(
        body, name="lerp_bwd", grid=(R_COLS // w,),
        in_specs=[pl.BlockSpec((t, w), lambda j: (0, C_R // w + j)), pl.BlockSpec((1, w), lambda j: (0, j)),
                  piece(0), piece(0), piece(per), piece(2 * per), piece(2 * per), pl.BlockSpec((t, w), lambda j: (0, 0))],
        out_specs=[pl.BlockSpec((t, w), lambda j: (0, j)), pl.BlockSpec((1, w), lambda j: (0, j))],
        out_shape=[jax.ShapeDtypeStruct((t, R_COLS), BF16), jax.ShapeDtypeStruct((1, R_COLS), F32)],
        compiler_params=_params(("arbitrary",)),
    )(z, mu, *d_r, d_k, *d_v, d_lora)


CONV_TILE = 256
N_CONV_TILES = D_FF // CONV_TILE


def _conv(h, w, b):
    return b + w[0:1, :] * _shift_down(h, 2) + w[1:2, :] * _shift_down(h, 1) + w[2:3, :] * h


def _conv_fwd(hu, conv_w, conv_b):
    t = hu.shape[0]
    n = N_CONV_TILES

    def body(hg_ref, hv_ref, wg_ref, wv_ref, bg_ref, bv_ref, o_ref):
        gate = _conv(hg_ref[...], wg_ref[...], bg_ref[...])
        val = _conv(hv_ref[...], wv_ref[...], bv_ref[...])
        o_ref[...] = (_silu(gate) * val).astype(o_ref.dtype)

    col = lambda off: pl.BlockSpec((t, CONV_TILE), lambda j: (0, j + off))
    wspec = lambda off: pl.BlockSpec((3, CONV_TILE), lambda j: (0, j + off))
    bspec = lambda off: pl.BlockSpec((1, CONV_TILE), lambda j: (0, j + off))
    return pl.pallas_call(
        body, name="conv_fwd", grid=(n,),
        in_specs=[col(0), col(n), wspec(0), wspec(n), bspec(0), bspec(n)],
        out_specs=pl.BlockSpec((t, CONV_TILE), lambda j: (0, j)), out_shape=jax.ShapeDtypeStruct((t, D_FF), BF16),
        compiler_params=_params(("parallel",)),
    )(hu, hu, conv_w, conv_w, conv_b, conv_b)


def _conv_bwd(hu, conv_w, conv_b, d_act):
    t = hu.shape[0]
    n = N_CONV_TILES

    def body(hg_ref, hv_ref, wg_ref, wv_ref, bg_ref, bv_ref, d_ref, dh_ref, dw_ref, db_ref):
        hg, hv, wg, wv = hg_ref[...], hv_ref[...], wg_ref[...], wv_ref[...]
        gate = _conv(hg, wg, bg_ref[...])
        val = _conv(hv, wv, bv_ref[...])
        d = d_ref[...]
        sg = _sigmoid(gate)
        d_gate = d * val * (sg * (1.0 + gate * (1.0 - sg)))
        d_val = d * (gate * sg)
        for half, (dc, h, w) in enumerate(((d_gate, hg, wg), (d_val, hv, wv))):
            dh = w[2:3, :] * dc + w[1:2, :] * _shift_up(dc, 1) + w[0:1, :] * _shift_up(dc, 2)
            dh_ref[half] = dh.astype(dh_ref.dtype)
            dw_ref[half, 0:1, :] = jnp.sum(dc * _shift_down(h, 2), axis=0, keepdims=True)
            dw_ref[half, 1:2, :] = jnp.sum(dc * _shift_down(h, 1), axis=0, keepdims=True)
            dw_ref[half, 2:3, :] = jnp.sum(dc * h, axis=0, keepdims=True)
            db_ref[half] = jnp.sum(dc, axis=0, keepdims=True)

    gcol = lambda rows: pl.BlockSpec((rows, CONV_TILE), lambda j: (0, j))
    vcol = lambda rows: pl.BlockSpec((rows, CONV_TILE), lambda j: (0, j + n))
    both = lambda rows: pl.BlockSpec((2, rows, CONV_TILE), lambda j: (0, 0, j))
    return pl.pallas_call(
        body, name="conv_bwd", grid=(n,),
        in_specs=[gcol(t), vcol(t), gcol(3), vcol(3), gcol(1), vcol(1), gcol(t)],
        out_specs=[both(t), both(3), both(1)],
        out_shape=[jax.ShapeDtypeStruct((2, t, D_FF), BF16), jax.ShapeDtypeStruct((2, 3, D_FF), F32),
                   jax.ShapeDtypeStruct((2, 1, D_FF), F32)],
        compiler_params=_params(("parallel",)),
    )(hu, hu, conv_w, conv_w, conv_b, conv_b, d_act)


_NN = (((1,), (0,)), ((), ()))
_NT = (((1,), (1,)), ((), ()))
_TN = (((0,), (0,)), ((), ()))
HGRN_CB = 8
HGRN_LOCAL_CB = 16


def _bf_dot(a, b, dn):
    return lax.dot_general(a.astype(BF16), b.astype(BF16), dn, preferred_element_type=F32)


def _tril():
    n = HGRN_LOCAL_CB * CHUNK
    r = lax.broadcasted_iota(jnp.int32, (n, n), 0)
    c = lax.broadcasted_iota(jnp.int32, (n, n), 1)
    return (r // CHUNK == c // CHUNK) & (r >= c)


def _hgrn_specs(t):
    rows = HGRN_LOCAL_CB * CHUNK
    head = pl.BlockSpec((rows, HGRN_K), lambda h, n: (n, h))
    v_head = pl.BlockSpec((rows, HGRN_K), lambda h, n: (n, C_HI // HGRN_K + h))
    mats = pl.BlockSpec((1, HGRN_LOCAL_CB, HGRN_K, HGRN_K), lambda h, n: (h, n, 0, 0))
    return head, v_head, mats, (HGRN_HEADS, t // rows)


def _hgrn_local_fwd(q_in, k_in, kd, z):
    t = q_in.shape[0]
    head, v_head, mats, grid = _hgrn_specs(t)

    def body(q_ref, k_ref, kd_ref, v_ref, o_ref, u_ref):
        v = v_ref[...]
        scores = jnp.where(_tril(), _bf_dot(q_ref[...], k_ref[...], _NT), 0.0)
        o_ref[...] = _bf_dot(scores, v, _NN)
        for n in range(HGRN_LOCAL_CB):
            rows = slice(n * CHUNK, (n + 1) * CHUNK)
            u_ref[0, n] = _bf_dot(v[rows], kd_ref[rows, :], _TN)

    return pl.pallas_call(
        body, name="hgrn_local_fwd", grid=grid, in_specs=[head, head, head, v_head], out_specs=[head, mats],
        out_shape=[jax.ShapeDtypeStruct((t, D_MODEL), F32),
                   jax.ShapeDtypeStruct((HGRN_HEADS, t // CHUNK, HGRN_K, HGRN_K), F32)],
        compiler_params=_params(("parallel", "parallel")),
    )(q_in, k_in, kd, z)


def _hgrn_state_specs(t, reverse=False):
    rows = HGRN_CB * CHUNK
    nb = t // rows
    at = (lambda n: nb - 1 - n) if reverse else (lambda n: n)
    tok = pl.BlockSpec((rows, D_MODEL), lambda n: (at(n), 0))
    mats = pl.BlockSpec((HGRN_HEADS, HGRN_CB, HGRN_K, HGRN_K), lambda n: (0, at(n), 0, 0))
    return tok, mats, nb


def _hgrn_state_fwd(o_intra, qe, dec, u):
    t = qe.shape[0]
    tok, mats, nb = _hgrn_state_specs(t)

    def body(oi_ref, qe_ref, dec_ref, u_ref, o_ref, st_ref, s_ref):
        @pl.when(pl.program_id(0) == 0)
        def _():
            s_ref[...] = jnp.zeros_like(s_ref)

        st = [s_ref[h] for h in range(HGRN_HEADS)]
        for n in range(HGRN_CB):
            rows = slice(n * CHUNK, (n + 1) * CHUNK)
            for h in range(HGRN_HEADS):
                cols = slice(h * HGRN_K, (h + 1) * HGRN_K)
                st_ref[h, n] = st[h]
                o_ref[rows, cols] = oi_ref[rows, cols] + _bf_dot(qe_ref[rows, cols], st[h], _NT)
                st[h] = st[h] * dec_ref[n * CHUNK:n * CHUNK + 1, cols] + u_ref[h, n]
        for h in range(HGRN_HEADS):
            s_ref[h] = st[h]

    return pl.pallas_call(
        body, name="hgrn_state_fwd", grid=(nb,), in_specs=[tok, tok, tok, mats], out_specs=[tok, mats],
        out_shape=[jax.ShapeDtypeStruct((t, D_MODEL), F32),
                   jax.ShapeDtypeStruct((HGRN_HEADS, t // CHUNK, HGRN_K, HGRN_K), F32)],
        scratch_shapes=[pltpu.VMEM((HGRN_HEADS, HGRN_K, HGRN_K), F32)],
        compiler_params=_params(("arbitrary",)),
    )(o_intra, qe, dec, u)


def _hgrn_state_bwd(d_o, qe, dec, states):
    t = qe.shape[0]
    tok_r, mats_r, nb = _hgrn_state_specs(t, reverse=True)

    def body(do_ref, qe_ref, dec_ref, st_ref, dqe_ref, ddec_ref, du_ref, d_ref):
        @pl.when(pl.program_id(0) == 0)
        def _():
            d_ref[...] = jnp.zeros_like(d_ref)

        first_row = lax.broadcasted_iota(jnp.int32, (CHUNK, HGRN_K), 0) == 0
        d = [d_ref[h] for h in range(HGRN_HEADS)]
        for n in reversed(range(HGRN_CB)):
            rows = slice(n * CHUNK, (n + 1) * CHUNK)
            for h in range(HGRN_HEADS):
                cols = slice(h * HGRN_K, (h + 1) * HGRN_K)
                st, do = st_ref[h, n], do_ref[rows, cols]
                du_ref[h, n] = d[h]
                ddec_ref[rows, cols] = jnp.where(first_row, jnp.sum(d[h] * st, axis=0, keepdims=True), 0.0)
                dqe_ref[rows, cols] = _bf_dot(do, st, _NN)
                d[h] = d[h] * dec_ref[n * CHUNK:n * CHUNK + 1, cols] + _bf_dot(do, qe_ref[rows, cols], _TN)
        for h in range(HGRN_HEADS):
            d_ref[h] = d[h]

    out = jax.ShapeDtypeStruct((t, D_MODEL), F32)
    return pl.pallas_call(
        body, name="hgrn_state_bwd", grid=(nb,), in_specs=[tok_r, tok_r, tok_r, mats_r], out_specs=[tok_r, tok_r, mats_r],
        out_shape=[out, out, jax.ShapeDtypeStruct((HGRN_HEADS, t // CHUNK, HGRN_K, HGRN_K), F32)],
        scratch_shapes=[pltpu.VMEM((HGRN_HEADS, HGRN_K, HGRN_K), F32)],
        compiler_params=_params(("arbitrary",)),
    )(d_o, qe, dec, states)


def _hgrn_local_bwd(q_in, k_in, kd, z, d_o, d_u):
    t = q_in.shape[0]
    head, v_head, mats, grid = _hgrn_specs(t)

    def body(q_ref, k_ref, kd_ref, v_ref, do_ref, du_ref, dq_ref, dk_ref, dkd_ref, dv_ref):
        tril = _tril()
        q, k, v, do = q_ref[...], k_ref[...], v_ref[...], do_ref[...]
        scores = jnp.where(tril, _bf_dot(q, k, _NT), 0.0)
        d_scores = jnp.where(tril, _bf_dot(do, v, _NT), 0.0)
        dq_ref[...] = _bf_dot(d_scores, k, _NN)
        dk_ref[...] = _bf_dot(d_scores, q, _TN)
        dv = _bf_dot(scores, do, _TN)
        for n in range(HGRN_LOCAL_CB):
            rows = slice(n * CHUNK, (n + 1) * CHUNK)
            du = du_ref[0, n]
            dv_ref[rows, :] = dv[rows] + _bf_dot(kd_ref[rows, :], du, _NT)
            dkd_ref[rows, :] = _bf_dot(v[rows], du, _NN)

    out = jax.ShapeDtypeStruct((t, D_MODEL), F32)
    return pl.pallas_call(
        body, name="hgrn_local_bwd", grid=grid, in_specs=[head, head, head, v_head, head, mats], out_specs=[head] * 4,
        out_shape=[out] * 4, compiler_params=_params(("parallel", "parallel")),
    )(q_in, k_in, kd, z, d_o, d_u)


def _seg_bcast(xs, e):
    n = RWKV_HEAD
    lhs = [x.astype(BF16) for x in xs]
    out = jnp.dot(lhs[0] if len(lhs) == 1 else jnp.concatenate(lhs, axis=0), e, preferred_element_type=F32)
    return [out[i * n:(i + 1) * n] for i in range(len(xs))]


def _rows_to_cols(rows, diag, e):
    zero = jnp.zeros((), BF16)
    parts = [jnp.where(diag, row.astype(BF16), zero) for row in rows]
    out = jnp.dot(jnp.concatenate(parts, axis=0), e, preferred_element_type=F32)
    n = RWKV_HEAD
    return [out[i * n:(i + 1) * n] for i in range(len(rows))]


def _col_to_row(col, diag):
    return jnp.sum(jnp.where(diag, col, 0.0), axis=0, keepdims=True)


_SCAN_PAIRS = ((0, 1, 2, 3),)


def _scan_consts():
    e = _seg_matrix(SCAN_GROUP, RWKV_HEAD)
    i = lax.broadcasted_iota(jnp.int32, (RWKV_HEAD, SCAN_GROUP), 0)
    l = lax.broadcasted_iota(jnp.int32, (RWKV_HEAD, SCAN_GROUP), 1)
    groups = [slice(g * SCAN_GROUP, (g + 1) * SCAN_GROUP) for g in range(D_MODEL // SCAN_GROUP)]
    return e, (l % RWKV_HEAD) == i, groups


def _rwkv_fwd(zl, w, k, a, b, shards, placed):
    t = zl.shape[0]
    nb = t // SCAN_TB
    n = len(shards)
    steps = range(SCAN_TB)

    def body(*refs):
        scan(*refs[:6], *refs[6 + 2 * n:8 + 2 * n], refs[8 + 3 * n])
        gather = (refs[6:6 + n], refs[8 + 2 * n:8 + 3 * n], *refs[9 + 3 * n:])

        @pl.when(pl.program_id(0) == 0)
        def _():
            _gather_start(*gather)

        @pl.when(pl.program_id(0) == nb - 1)
        def _():
            _gather_finish(*gather)

    def scan(r_ref, w_ref, k_ref, v_ref, a_ref, b_ref, y_ref, st_ref, s_ref):
        @pl.when(pl.program_id(0) == 0)
        def _():
            s_ref[...] = jnp.zeros_like(s_ref)

        e, diag, groups = _scan_consts()
        v_cols = [_rows_to_cols([v_ref[i:i + 1, sl] for i in steps], diag, e) for sl in groups]
        s = [s_ref[:, sl] for sl in groups]
        for i in steps:
            for pair in _SCAN_PAIRS:
                sas = _seg_bcast([s[g] * a_ref[i:i + 1, groups[g]] for g in pair], e)
                for g, sa in zip(pair, sas):
                    sl = groups[g]
                    s[g] = s[g] * w_ref[i:i + 1, sl] + sa * b_ref[i:i + 1, sl] + v_cols[g][i] * k_ref[i:i + 1, sl]
                    st_ref[i, :, sl] = s[g]
        for g, sl in enumerate(groups):
            s_ref[:, sl] = s[g]
            y_cols = _seg_bcast([st_ref[i, :, sl] * r_ref[i:i + 1, sl] for i in steps], e)
            for i in steps:
                y_ref[i:i + 1, sl] = _col_to_row(y_cols[i], diag)

    blk = pl.BlockSpec((SCAN_TB, D_MODEL), lambda n: (n, 0))
    v_blk = pl.BlockSpec((SCAN_TB, D_MODEL), lambda n: (n, 2))
    outs = pl.pallas_call(
        body, name="rwkv_fwd", grid=(nb,), in_specs=[blk, blk, blk, v_blk, blk, blk] + [ANY] * (2 * n),
        out_specs=[blk, pl.BlockSpec((SCAN_TB, RWKV_HEAD, D_MODEL), lambda i: (i, 0, 0))] + [ANY] * n,
        out_shape=[jax.ShapeDtypeStruct((t, D_MODEL), F32), jax.ShapeDtypeStruct((t, RWKV_HEAD, D_MODEL), F32)]
        + [jax.ShapeDtypeStruct(p.shape, p.dtype) for p in placed],
        input_output_aliases={6 + n + i: 2 + i for i in range(n)},
        scratch_shapes=[pltpu.VMEM((RWKV_HEAD, D_MODEL), F32), pltpu.SemaphoreType.DMA((n, 6)), pltpu.SemaphoreType.DMA((n, 6))],
        compiler_params=_params(("arbitrary",)),
    )(zl, w, k, zl, a, b, *shards, *placed)
    return outs[0], outs[1], outs[2:]


def _rwkv_bwd(zl, w, k, a, b, states, d_y, parts):
    t = zl.shape[0]
    nb = t // SCAN_TB
    n = len(parts)
    steps = range(SCAN_TB)

    def body(*refs):
        scan(*refs[:9], *refs[9 + n:15 + n], refs[15 + 2 * n])
        exchange = (refs[9:9 + n], refs[15 + n:15 + 2 * n], *refs[16 + 2 * n:])

        @pl.when(pl.program_id(0) == 0)
        def _():
            _rs_chips_start(*exchange)

        @pl.when(pl.program_id(0) == nb - 1)
        def _():
            _rs_chips_finish(*exchange)

    def scan(r_ref, w_ref, k_ref, v_ref, a_ref, b_ref, st_ref, prev_ref, dy_ref,
             dr_ref, dw_ref, dk_ref, dv_ref, da_ref, db_ref, ds_ref):
        @pl.when(pl.program_id(0) == 0)
        def _():
            ds_ref[...] = jnp.zeros_like(ds_ref)

        has_prev = (pl.program_id(0) < nb - 1).astype(F32)
        e, diag, groups = _scan_consts()
        colsum = lambda x: jnp.sum(x, axis=0, keepdims=True)

        def s_prev(i, sl):
            return st_ref[i - 1, :, sl] if i > 0 else prev_ref[0, :, sl] * has_prev

        dy_cols = [_rows_to_cols([dy_ref[i:i + 1, sl] for i in steps], diag, e) for sl in groups]
        v_cols = [_rows_to_cols([v_ref[i:i + 1, sl] for i in steps], diag, e) for sl in groups]
        sa_cols = [_seg_bcast([s_prev(i, sl) * a_ref[i:i + 1, sl] for i in steps], e) for sl in groups]
        ds = [ds_ref[:, sl] for sl in groups]
        dsk = [[None] * SCAN_TB for _ in groups]
        for i in reversed(steps):
            for pair in _SCAN_PAIRS:
                d = {}
                for g in pair:
                    sl = groups[g]
                    d[g] = ds[g] + dy_cols[g][i] * r_ref[i:i + 1, sl]
                    dr_ref[i:i + 1, sl] = colsum(st_ref[i, :, sl] * dy_cols[g][i])
                    dw_ref[i:i + 1, sl] = colsum(d[g] * s_prev(i, sl))
                    db_ref[i:i + 1, sl] = colsum(d[g] * sa_cols[g][i])
                    dk_ref[i:i + 1, sl] = colsum(d[g] * v_cols[g][i])
                    dsk[g][i] = d[g] * k_ref[i:i + 1, sl]
                dsas = _seg_bcast([d[g] * b_ref[i:i + 1, groups[g]] for g in pair], e)
                for g, dsa in zip(pair, dsas):
                    sl = groups[g]
                    da_ref[i:i + 1, sl] = colsum(s_prev(i, sl) * dsa)
                    ds[g] = d[g] * w_ref[i:i + 1, sl] + dsa * a_ref[i:i + 1, sl]
        for g, sl in enumerate(groups):
            ds_ref[:, sl] = ds[g]
            dv_cols = _seg_bcast(dsk[g], e)
            for i in steps:
                dv_ref[i:i + 1, sl] = _col_to_row(dv_cols[i], diag)

    blk = pl.BlockSpec((SCAN_TB, D_MODEL), lambda n: (nb - 1 - n, 0))
    v_blk = pl.BlockSpec((SCAN_TB, D_MODEL), lambda n: (nb - 1 - n, 2))
    out = jax.ShapeDtypeStruct((t, D_MODEL), F32)
    outs = pl.pallas_call(
        body, name="rwkv_bwd", grid=(nb,),
        in_specs=[blk, blk, blk, v_blk, blk, blk] + [
            pl.BlockSpec((SCAN_TB, RWKV_HEAD, D_MODEL), lambda i: (nb - 1 - i, 0, 0)),
            pl.BlockSpec((1, RWKV_HEAD, D_MODEL), lambda i: (jnp.maximum((nb - 1 - i) * SCAN_TB - 1, 0), 0, 0)),
            blk] + [ANY] * n,
        out_specs=[blk] * 6 + [ANY] * n,
        out_shape=[out] * 6 + [jax.ShapeDtypeStruct((3,) + p.shape[1:], p.dtype) for p in parts],
        scratch_shapes=[pltpu.VMEM((RWKV_HEAD, D_MODEL), F32), pltpu.SemaphoreType.DMA((n, 3)), pltpu.SemaphoreType.DMA((n, 3))],
        compiler_params=_params(("arbitrary",)),
    )(zl, w, k, zl, a, b, states, states, d_y, *parts)
    return outs[:6], outs[6:]


def _loss_head(h1, ff, target, g_post):
    def fn(tv, pv):
        a, f, tgt = tv
        h2, vjp = jax.vjp(lambda a_, f_, g_: a_ + _rms(f_, g_), a, f, pv[0])
        err = h2 - tgt
        loss = 0.5 * jnp.sum(jnp.mean(err * err, axis=-1, keepdims=True), axis=0, keepdims=True)
        d_a, d_f, d_g = vjp(err * (1.0 / D_MODEL))
        return [d_a, d_f], [loss, d_g]

    return _tok_call("loss_head", fn, [(h1, D_MODEL, 0), (ff, D_MODEL, 0), (target, D_MODEL, 0)], [g_post],
                     [(D_MODEL, F32), (D_MODEL, BF16)], red_shapes=[(1, 1), (1, D_MODEL)], tm=LIGHT_TM)


def _sum_call(name, terms, rows_per_block=None):
    a0, i0 = terms[0]
    r, c = a0.shape[-2:]
    tr = rows_per_block or r

    def body(*refs):
        acc = refs[0][...].reshape(tr, c)
        for ref in refs[1:-1]:
            acc = acc + ref[...].reshape(tr, c)
        refs[-1][...] = acc

    def spec(arr, idx):
        if arr.ndim == 2:
            return pl.BlockSpec((tr, c), lambda i: (i, 0))
        return pl.BlockSpec((1, tr, c), functools.partial(lambda i, idx: (idx, i, 0), idx=idx))

    return pl.pallas_call(
        body, name=name, grid=(r // tr,), in_specs=[spec(a, i) for a, i in terms],
        out_specs=pl.BlockSpec((tr, c), lambda i: (i, 0)), out_shape=jax.ShapeDtypeStruct((r, c), F32),
        compiler_params=_params(("parallel",)),
    )(*[a for a, _ in terms])


def _adamw_math(w, g, m, v):
    m2 = ADAM_B1 * m + (1.0 - ADAM_B1) * g
    v2 = ADAM_B2 * v + (1.0 - ADAM_B2) * (g * g)
    m_hat = m2 / (1.0 - ADAM_B1 ** ADAM_STEP)
    v_hat = v2 / (1.0 - ADAM_B2 ** ADAM_STEP)
    return -ADAM_LR * (m_hat / (jnp.sqrt(v_hat) + ADAM_EPS) + ADAM_WD * w), m2, v2


def _adamw(name, w, g, m, v, bm, bn, g_transposed=False, exchange=None):
    r, c = w.shape
    grid = (pl.cdiv(r, bm), pl.cdiv(c, bn))

    def body(*refs):
        w_ref, g_ref, m_ref, v_ref = refs[:4]
        go_ref, d_ref, mo_ref, vo_ref = refs[-6:-2] if exchange else refs[4:8]
        if exchange:
            riders = (refs[4], refs[-7], refs[-2], refs[-1])
            first = (pl.program_id(0) == 0) & (pl.program_id(1) == 0)

            @pl.when(first)
            def _():
                _exchange8_start(*riders)

        g = g_ref[...].T if g_transposed else g_ref[...]
        d, m2, v2 = _adamw_math(w_ref[...], g, m_ref[...], v_ref[...])
        go_ref[...] = g
        d_ref[...] = d
        mo_ref[...] = m2
        vo_ref[...] = v2
        if exchange:
            @pl.when((pl.program_id(0) == grid[0] - 1) & (pl.program_id(1) == grid[1] - 1))
            def _():
                _exchange8_finish(*riders)

    blk = pl.BlockSpec((bm, bn), lambda i, j: (i, j))
    g_blk = pl.BlockSpec((bn, bm), lambda i, j: (j, i)) if g_transposed else blk
    out = jax.ShapeDtypeStruct((r, c), F32)
    if not exchange:
        return pl.pallas_call(
            body, name=name, grid=grid, in_specs=[blk, g_blk, blk, blk],
            out_specs=[blk] * 4, out_shape=[out] * 4, compiler_params=_params(("parallel", "parallel")),
        )(w, g, m, v)
    parts, landing = exchange
    outs = pl.pallas_call(
        body, name=name, grid=grid, in_specs=[blk, g_blk, blk, blk, ANY, ANY],
        out_specs=[ANY] + [blk] * 4, out_shape=[jax.ShapeDtypeStruct(landing.shape, landing.dtype)] + [out] * 4,
        input_output_aliases={5: 0}, scratch_shapes=[pltpu.SemaphoreType.DMA((7,)), pltpu.SemaphoreType.DMA((7,))],
        compiler_params=_params(("arbitrary", "arbitrary")),
    )(w, g, m, v, parts, landing)
    return outs[1:], outs[0]


ANY = pl.BlockSpec(memory_space=pl.ANY)


def _place():
    x, y, c = lax.axis_index("x"), lax.axis_index("y"), lax.axis_index("c")
    chips = [(1 - x, y), (x, 1 - y), (1 - x, 1 - y)]
    return x, y, c, chips


def _sibling():
    return (lax.axis_index("x"), lax.axis_index("y"), 1 - lax.axis_index("c"))


def _wait_all(local, remote):
    for cp in local:
        cp.wait()
    for cp in remote:
        cp.wait_send()


def _place_shard(name, shard, place):
    r, cols = shard.shape
    tr = r // 4

    def body(s_ref, in_ref, out_ref):
        out_ref[...] = in_ref[...]

    return pl.pallas_call(
        body, name=name,
        grid_spec=pltpu.PrefetchScalarGridSpec(
            num_scalar_prefetch=1, grid=(4,), in_specs=[pl.BlockSpec((tr, cols), lambda i, s: (i, 0))],
            out_specs=pl.BlockSpec((tr, cols), lambda i, s: (4 * s[1] + i, 0))),
        out_shape=jax.ShapeDtypeStruct((N_SHARD * r, cols), shard.dtype), compiler_params=_params(("arbitrary",)),
    )(place, shard)


def _gather_copies(ins, outs, send_sems, recv_sems, only_first=False):
    x, y, c, chips = _place()
    me, sibling = (x, y, c), _sibling()

    def rows(k, px, py, pc):
        h = ins[k].shape[0] // 2
        return outs[k].at[pl.ds((2 * px + py) * 2 * h + pc * h, h), :]

    def copy(k, j, block, to, src=None):
        return pltpu.make_async_remote_copy(
            src_ref=rows(k, *block) if src is None else src, dst_ref=rows(k, *block),
            send_sem=send_sems.at[k, j], recv_sem=recv_sems.at[k, j], device_id=to, device_id_type=MESH)

    each = [(k, j, chip) for k in range(len(ins)) for j, chip in enumerate(chips)]
    half = lambda k: ins[k].at[pl.ds(c * (ins[k].shape[0] // 2), ins[k].shape[0] // 2), :]
    first = [copy(k, j, me, (*chip, c), src=half(k)) for k, j, chip in each]
    if only_first:
        return first
    arrive = [copy(k, j, (*chip, c), me) for k, j, chip in each]
    passed = [copy(k, 3 + j, (*chip, c), sibling) for k, j, chip in each]
    landed = [copy(k, 3 + j, (*chip, 1 - c), me) for k, j, chip in each]
    return first, arrive, passed, landed


def _gather_start(ins, outs, send_sems, recv_sems):
    for cp in _gather_copies(ins, outs, send_sems, recv_sems, only_first=True):
        cp.start()


def _gather_finish(ins, outs, send_sems, recv_sems):
    first, arrive, passed, landed = _gather_copies(ins, outs, send_sems, recv_sems)
    for arrival, forward in zip(arrive, passed):
        arrival.wait_recv()
        forward.start()
    for cp in landed:
        cp.wait_recv()
    _wait_all([], first + passed)


def _gather_shards(shards, placed):
    n = len(shards)

    def body(*refs):
        ins, outs = refs[:n], refs[2 * n:3 * n]
        _gather_start(ins, outs, *refs[3 * n:])
        _gather_finish(ins, outs, *refs[3 * n:])

    return pl.pallas_call(
        body, name="gather_shards", in_specs=[ANY] * (2 * n), out_specs=[ANY] * n,
        out_shape=[jax.ShapeDtypeStruct(a.shape, a.dtype) for a in placed],
        input_output_aliases={n + k: k for k in range(n)},
        scratch_shapes=[pltpu.SemaphoreType.DMA((n, 6)), pltpu.SemaphoreType.DMA((n, 6))],
    )(*shards, *placed)


def _exchange8_copies(in_ref, out_ref, send_sems, recv_sems, only_sends=False):
    x, y, c, _ = _place()
    sends, arrivals = [], []
    for rel in range(1, 8):
        dx, dy, dc = rel >> 2 & 1, rel >> 1 & 1, rel & 1
        sends.append(pltpu.make_async_remote_copy(
            src_ref=in_ref.at[2 * (x ^ dx) + (y ^ dy)], dst_ref=out_ref.at[4 * x + 2 * y + c],
            send_sem=send_sems.at[rel - 1], recv_sem=recv_sems.at[rel - 1],
            device_id=(x ^ dx, y ^ dy, c ^ dc), device_id_type=MESH))
        if not only_sends:
            arrivals.append(pltpu.make_async_remote_copy(
                src_ref=in_ref.at[0], dst_ref=out_ref.at[4 * (x ^ dx) + 2 * (y ^ dy) + (c ^ dc)],
                send_sem=send_sems.at[rel - 1], recv_sem=recv_sems.at[rel - 1],
                device_id=(x, y, c), device_id_type=MESH))
    return sends, arrivals


def _exchange8_start(in_ref, out_ref, send_sems, recv_sems):
    for cp in _exchange8_copies(in_ref, out_ref, send_sems, recv_sems, only_sends=True)[0]:
        cp.start()


def _exchange8_finish(in_ref, out_ref, send_sems, recv_sems):
    sends, arrivals = _exchange8_copies(in_ref, out_ref, send_sems, recv_sems)
    for cp in arrivals:
        cp.wait_recv()
    _wait_all([], sends)


def _rs_sibling_copies(ins, outs, send_sems, recv_sems):
    c = lax.axis_index("c")
    return [pltpu.make_async_remote_copy(
        src_ref=ins[k].at[2 * s + 1 - c], dst_ref=outs[k].at[s], send_sem=send_sems.at[k, s], recv_sem=recv_sems.at[k, s],
        device_id=_sibling(), device_id_type=MESH) for k in range(len(ins)) for s in range(N_SHARD)]


def _rs_sibling_start(ins, outs, send_sems, recv_sems):
    for cp in _rs_sibling_copies(ins, outs, send_sems, recv_sems):
        cp.start()


def _rs_sibling_finish(ins, outs, send_sems, recv_sems):
    sends = _rs_sibling_copies(ins, outs, send_sems, recv_sems)
    for cp in sends:
        cp.wait_recv()
    _wait_all([], sends)


def _rs_sibling(grads):
    n = len(grads)

    def body(*refs):
        _rs_sibling_start(refs[:n], refs[n:2 * n], *refs[2 * n:])
        _rs_sibling_finish(refs[:n], refs[n:2 * n], *refs[2 * n:])

    return pl.pallas_call(
        body, name="rs_sibling", in_specs=[ANY] * n, out_specs=[ANY] * n,
        out_shape=[jax.ShapeDtypeStruct((N_SHARD,) + a.shape[1:], a.dtype) for a in grads],
        scratch_shapes=[pltpu.SemaphoreType.DMA((n, N_SHARD)), pltpu.SemaphoreType.DMA((n, N_SHARD))],
    )(*grads)


def _rs_chips_copies(ins, outs, send_sems, recv_sems):
    x, y, c, chips = _place()
    return [pltpu.make_async_remote_copy(
        src_ref=ins[k].at[2 * px + py], dst_ref=outs[k].at[j], send_sem=send_sems.at[k, j], recv_sem=recv_sems.at[k, j],
        device_id=(px, py, c), device_id_type=MESH) for k in range(len(ins)) for j, (px, py) in enumerate(chips)]


def _rs_chips_start(ins, outs, send_sems, recv_sems):
    for cp in _rs_chips_copies(ins, outs, send_sems, recv_sems):
        cp.start()


def _rs_chips_finish(ins, outs, send_sems, recv_sems):
    sends = _rs_chips_copies(ins, outs, send_sems, recv_sems)
    for cp in sends:
        cp.wait_recv()
    _wait_all([], sends)


def _rs_finish(bufs):
    n = len(bufs)

    def body(*refs):
        outs = refs[n:2 * n]
        send_sems, recv_sems = refs[2 * n:]
        c = lax.axis_index("c")
        sends = []
        for k in range(n):
            cp = pltpu.make_async_remote_copy(
                src_ref=outs[k].at[c], dst_ref=outs[k].at[c], send_sem=send_sems.at[k], recv_sem=recv_sems.at[k],
                device_id=_sibling(), device_id_type=MESH)
            cp.start()
            sends.append(cp)
        for k in range(n):
            pltpu.make_async_remote_copy(
                src_ref=outs[k].at[c], dst_ref=outs[k].at[1 - c], send_sem=send_sems.at[k], recv_sem=recv_sems.at[k],
                device_id=_sibling(), device_id_type=MESH).wait_recv()
        _wait_all([], sends)

    return pl.pallas_call(
        body, name="rs_finish", in_specs=[ANY] * n, out_specs=[ANY] * n,
        out_shape=[jax.ShapeDtypeStruct(a.shape, a.dtype) for a in bufs], input_output_aliases={k: k for k in range(n)},
        scratch_shapes=[pltpu.SemaphoreType.DMA((n,)), pltpu.SemaphoreType.DMA((n,))],
    )(*bufs)


def _sum3d(name, terms, scalars, grid_lead, out_lead, out_index, tr=None, out_dtype=F32):
    h, c = terms[0][0].shape[1:]
    tr = tr or h

    def body(s_ref, *refs):
        acc = refs[0][...].astype(F32)
        for ref in refs[1:-1]:
            acc = acc + ref[...].astype(F32)
        refs[-1][...] = acc.astype(refs[-1].dtype)

    in_specs = [pl.BlockSpec((1, tr, c), functools.partial(lambda l, i, s_ref, f: (f(l, s_ref), i, 0), f=f)) for _, f in terms]
    out_spec = pl.BlockSpec((1, tr, c), lambda l, i, s_ref: (out_index(l, s_ref), i, 0))
    return pl.pallas_call(
        body, name=name,
        grid_spec=pltpu.PrefetchScalarGridSpec(num_scalar_prefetch=1, grid=(grid_lead, h // tr), in_specs=in_specs, out_specs=out_spec),
        out_shape=jax.ShapeDtypeStruct((out_lead, h, c), out_dtype), compiler_params=_params(("arbitrary", "arbitrary")),
    )(scalars, *[a for a, _ in terms])


def _halves(grads):
    return [a.reshape(2 * N_SHARD, a.shape[0] // (2 * N_SHARD), a.shape[1]) for a in grads]


def _rs_stage1(g8, place, tag, from_sibling=None):
    from_sibling = _rs_sibling(g8) if from_sibling is None else from_sibling
    parts = [_sum3d(f"rs_add1_{tag}{k}", [(g8[k], lambda l, s: 2 * l + s[0]), (from_sibling[k], lambda l, s: l)], place,
                    N_SHARD, N_SHARD, lambda l, s: l, tr=g8[k].shape[1] // 2, out_dtype=BF16)
             for k in range(len(g8))]
    return g8, from_sibling, parts


def _rs_stage3(stage1, from_chips, place, tag):
    g8, from_sibling, _ = stage1
    mine = [(lambda l, s: 2 * s[1] + s[0]), (lambda l, s: s[1])]
    bufs = [_sum3d(f"rs_add2_{tag}{k}", [(g8[k], mine[0]), (from_sibling[k], mine[1])]
                   + [(from_chips[k], functools.partial(lambda l, s, j: j, j=j)) for j in range(3)],
                   place, 1, 2, lambda l, s: s[0], tr=g8[k].shape[1] // 2)
            for k in range(len(g8))]
    whole = _rs_finish(bufs)
    return [w.reshape(2 * w.shape[1], w.shape[2]) for w in whole]


_LATE = ["w_a", "w_b", "w_out", "w_down", "w_up_t"]


def _device_step(x, target, p, late_shards, late_placed, place):
    t = x.shape[0]
    d = D_MODEL
    tok = lambda arr, c=0, w=d: (arr, w, c)
    f32x = lambda n: [(d, F32)] * n
    rp_params = [p["w0"], p["w2p"], p["a0"], p["a2p"], p["g2p"], p["k_k"], p["k_a"]]
    post_params = [p["ln_w"], p["ln_b"], p["r_k"]]
    g = {}

    (xn,) = _tok_fwd("norm1_fwd", _fn_norm, [tok(x)], [p["g1"]], [(d, BF16)], tm=LIGHT_TM)
    z = _mm("in_proj", xn, p["w_in_t"], "nt", tm=t, tn=256)
    q_in, k_in, kd, qe, dec = _tok_fwd("hgates_fwd", _fn_hgates, [tok(z, 0), tok(z, 1)], [p["lb2"]], f32x(5))
    o_intra, u = _hgrn_local_fwd(q_in, k_in, kd, z)
    o_raw, h_states = _hgrn_state_fwd(o_intra, qe, dec, u)
    zl = _lerp_fwd(z, p["mu"])
    lora = tok(zl, 3 * d // LORA, LORA)
    decay, kr2, avec, bvec, gate = _tok_fwd("rprep_fwd", _fn_rprep, [tok(zl, 1), lora], rp_params, f32x(5))
    y, r_states, late = _rwkv_fwd(zl, decay, kr2, avec, bvec, late_shards, late_placed)
    p = dict(p, **dict(zip(_LATE, late)))
    (o_a,) = _tok_fwd("hpost_fwd", _fn_hpost, [tok(o_raw), tok(z, 3)], [p["gnorm"]], [(d, BF16)])
    post_toks = [tok(y), tok(zl, 0), tok(kr2), tok(zl, 2), tok(gate)]
    (o_b,) = _tok_fwd("rpost_fwd", _fn_rpost, post_toks, post_params, [(d, BF16)])
    y_a = _mm("branch_a", o_a, p["w_a"], "nn")
    y_b = _mm("branch_b", o_b, p["w_b"], "nn")
    merge_toks = [tok(z, C_G // 256, 256), tok(z, (C_G + d) // 256, 256), tok(y_a, 0, 256), tok(y_b, 0, 256)]
    (merged,) = _tok_fwd("merge_fwd", _fn_merge, merge_toks, [], [(256, BF16)], col_grid=4, tm=512)
    mix = _mm("out_proj", merged, p["w_out"], "nn")
    h1, xn2 = _tok_fwd("res1_fwd", _fn_res1, [tok(x), tok(mix)], [p["g_post1"], p["g_pre2"]], [(d, F32), (d, BF16)],
                       tm=LIGHT_TM)
    hu = _mm("up_proj", xn2, p["w_up_t"], "nt", tm=t, tn=512)
    act = _conv_fwd(hu, p["conv_w"], p["conv_b"])
    ff = _mm("down_proj", act, p["w_down"], "nn")
    d_h1, d_ff, loss, g["g_post2"] = _loss_head(h1, ff, target, p["g_post2"])

    d_act = _mm("d_act", d_ff, p["w_down"], "nt")
    g["w_down"] = _mm("dw_down", act, d_ff, "tn", tm=256, tn=1024)
    d_hu, d_cw, d_cb = _conv_bwd(hu, p["conv_w"], p["conv_b"], d_act)
    g["conv_w"], g["conv_b"] = d_cw.transpose(1, 0, 2).reshape(3, 2 * D_FF), d_cb.reshape(1, 2 * D_FF)
    d_hu = d_hu.reshape(2 * t, D_FF)
    d_xn2 = _mm("d_xn2", d_hu, p["w_up_t"], "nn", tm=t, tk=D_FF, mk=(t, 2 * D_FF), a_map=lambda i, j, q: (q, 0))
    g["w_up_t"] = _mm("dw_up", d_hu, xn2, "tn", tm=CONV_TILE, tn=1024, mk=(2 * D_FF, t),
                      a_map=lambda i, j, q: (i // N_CONV_TILES, i % N_CONV_TILES))
    d_x_res, d_mix, g["g_post1"], g["g_pre2"] = _tok_bwd(
        "res1_bwd", _fn_res1, [tok(x), tok(mix)], [p["g_post1"], p["g_pre2"]], [[tok(d_h1)], [tok(d_xn2)]],
        [(d, F32), (d, BF16)], tm=LIGHT_TM)
    d_merged = _mm("d_merged", d_mix, p["w_out"], "nt")
    g["w_out"] = _mm("dw_out", merged, d_mix, "tn")
    d_ga, d_gb, d_ya, d_yb = _tok_bwd("merge_bwd", _fn_merge, merge_toks, [], [[tok(d_merged, 0, 256)]],
                                      [(256, BF16)] * 4, col_grid=4, tm=512)
    d_oa = _mm("d_oa", d_ya, p["w_a"], "nt")
    g["w_a"] = _mm("dw_a", o_a, d_ya, "tn")
    d_ob = _mm("d_ob", d_yb, p["w_b"], "nt")
    g["w_b"] = _mm("dw_b", o_b, d_yb, "tn")
    d_oraw, d_hg, g["gnorm"] = _tok_bwd("hpost_bwd", _fn_hpost, [tok(o_raw), tok(z, 3)], [p["gnorm"]], [[tok(d_oa)]],
                                        [(d, F32), (d, BF16)])
    late_g8 = _halves([g[n] for n in _LATE])
    rpost = _tok_bwd("rpost_bwd", _fn_rpost, post_toks, post_params, [[tok(d_ob)]], f32x(5), sibling_rider=late_g8)
    d_y, d_r1, d_kr2_1, d_v1, d_gate, g["ln_w"], g["ln_b"], g["r_k"] = rpost[:8]
    stage1 = _rs_stage1(late_g8, place, "late", from_sibling=rpost[8:])
    (d_r2, d_decay, d_kr2_2, d_v2, d_avec, d_bvec), from_chips = _rwkv_bwd(
        zl, decay, kr2, avec, bvec, r_states, d_y, stage1[2])
    g.update(zip(_LATE, _rs_stage3(stage1, from_chips, place, "late")))
    prep = _tok_bwd("rprep_bwd", _fn_rprep, [tok(zl, 1), lora], rp_params,
                    [[tok(d_decay)], [tok(d_kr2_1), tok(d_kr2_2)], [tok(d_avec)], [tok(d_bvec)], [tok(d_gate)]],
                    [(d, F32), (LORA, F32)])
    d_kr, d_lora = prep[:2]
    g["w0"], g["w2p"], g["a0"], g["a2p"], g["g2p"], g["k_k"], g["k_a"] = prep[2:]
    dz_r, g["mu"] = _lerp_bwd(z, p["mu"], (d_r1, d_r2), d_kr, (d_v1, d_v2), d_lora)
    d_qe, d_dec, d_u = _hgrn_state_bwd(d_oraw, qe, dec, h_states)
    d_q_in, d_k_in, d_kd, d_vi = _hgrn_local_bwd(q_in, k_in, kd, z, d_oraw, d_u)
    d_hq, d_hf, g["lb2"] = _tok_bwd("hgates_bwd", _fn_hgates, [tok(z, 0), tok(z, 1)], [p["lb2"]],
                                    [[tok(d_q_in)], [tok(d_k_in)], [tok(d_kd)], [tok(d_qe)], [tok(d_dec)]], [(d, BF16)] * 2)
    dz = jnp.concatenate([d_hq, d_hf, d_vi.astype(BF16), d_hg, dz_r, d_ga, d_gb], axis=1)
    stage1 = _rs_stage1(_halves([_mm("dw_in", dz, xn, "tn", tm=256, tn=1024)]), place, "w_in")
    d_xn, from_chips = _mm("d_xn", dz, p["w_in_t"], "nn", tm=1024, tn=512, tk=IN_COLS // 2, riders=stage1[2])
    (g["w_in_t"],) = _rs_stage3(stage1, from_chips, place, "w_in")
    grad_x, g["g1"] = _tok_bwd("norm1_bwd", _fn_norm, [tok(x)], [p["g1"]], [[tok(d_xn)]], [(d, F32)],
                               add_to_first=tok(d_x_res), tm=LIGHT_TM)
    return loss, grad_x, g


_WEIGHTS = ["attn_pre_norm", "w_in", "hgrn_lb", "hgrn_gnorm", "w_branch_a", "rwkv_mu", "rwkv_w0", "rwkv_w2", "rwkv_a0",
            "rwkv_a2", "rwkv_g2", "rwkv_k_k", "rwkv_k_a", "rwkv_r_k", "rwkv_ln_w", "rwkv_ln_b", "w_branch_b", "w_out",
            "attn_post_norm", "ffn_pre_norm", "w_up", "conv_w", "conv_b", "w_down", "ffn_post_norm"]
_REPLICATED = [("attn_pre_norm", "g1"), ("hgrn_lb", "lb2"), ("hgrn_gnorm", "gnorm"), ("rwkv_mu", "mu"), ("rwkv_w0", "w0"),
               ("rwkv_a0", "a0"), ("rwkv_k_k", "k_k"), ("rwkv_k_a", "k_a"), ("rwkv_r_k", "r_k"), ("rwkv_ln_w", "ln_w"),
               ("rwkv_ln_b", "ln_b"), ("attn_post_norm", "g_post1"), ("ffn_pre_norm", "g_pre2"), ("conv_b", "conv_b"),
               ("ffn_post_norm", "g_post2")]
SLAB_COLS = 1024


def _pack(arrays):
    pieces, total = [], 0
    for a in arrays:
        flat = a.reshape(-1)
        rows = -(-flat.shape[0] // SLAB_COLS)
        pieces.append(jnp.pad(flat, (0, rows * SLAB_COLS - flat.shape[0])).reshape(rows, SLAB_COLS))
        total += rows
    if total % 8:
        pieces.append(jnp.zeros((8 - total % 8, SLAB_COLS), F32))
    return jnp.concatenate(pieces, axis=0)


def _unpack(slab, shapes):
    out, at = [], 0
    for s in shapes:
        size = 1
        for dim in s:
            size *= dim
        rows = -(-size // SLAB_COLS)
        out.append(slab[at:at + rows].reshape(-1)[:size].reshape(s))
        at += rows
    return out


def kernel(x, attn_pre_norm, w_in, hgrn_lb, hgrn_gnorm, w_branch_a, rwkv_mu, rwkv_w0, rwkv_w2, rwkv_a0, rwkv_a2, rwkv_g2, rwkv_k_k, rwkv_k_a, rwkv_r_k, rwkv_ln_w, rwkv_ln_b, w_branch_b, w_out, attn_post_norm, ffn_pre_norm, w_up, conv_w, conv_b, w_down, ffn_post_norm, loss_target, m_attn_pre_norm, m_w_in, m_hgrn_lb, m_hgrn_gnorm, m_w_branch_a, m_rwkv_mu, m_rwkv_w0, m_rwkv_w2, m_rwkv_a0, m_rwkv_a2, m_rwkv_g2, m_rwkv_k_k, m_rwkv_k_a, m_rwkv_r_k, m_rwkv_ln_w, m_rwkv_ln_b, m_w_branch_b, m_w_out, m_attn_post_norm, m_ffn_pre_norm, m_w_up, m_conv_w, m_conv_b, m_w_down, m_ffn_post_norm, v_attn_pre_norm, v_w_in, v_hgrn_lb, v_hgrn_gnorm, v_w_branch_a, v_rwkv_mu, v_rwkv_w0, v_rwkv_w2, v_rwkv_a0, v_rwkv_a2, v_rwkv_g2, v_rwkv_k_k, v_rwkv_k_a, v_rwkv_r_k, v_rwkv_ln_w, v_rwkv_ln_b, v_w_branch_b, v_w_out, v_attn_post_norm, v_ffn_pre_norm, v_w_up, v_conv_w, v_conv_b, v_w_down, v_ffn_post_norm):
    given = dict(locals())
    w = {n: given[n] for n in _WEIGHTS}
    mom = {n: given["m_" + n] for n in _WEIGHTS}
    var = {n: given["v_" + n] for n in _WEIGHTS}
    shard = 2 * lax.axis_index("x") + lax.axis_index("y")
    place = jnp.stack([lax.axis_index("c"), shard]).astype(jnp.int32)
    row = lambda a: a.reshape(1, -1)
    lora_of = lambda d: jnp.concatenate([d["rwkv_w2"][0], d["rwkv_a2"][0], d["rwkv_g2"][0]], axis=0)

    shards = [w["w_in"][0].T.astype(BF16), lora_of(w), jnp.pad(w["conv_w"][0], ((0, 29), (0, 0)))]
    late_shards = [w["w_branch_a"][0].astype(BF16), w["w_branch_b"][0].astype(BF16), w["w_out"][0].astype(BF16),
                   w["w_down"][0].astype(BF16), w["w_up"][0].T.astype(BF16)]
    placed = [_place_shard(f"place_{k}", a, place) for k, a in enumerate(shards)]
    late_placed = [_place_shard(f"place_late_{k}", a, place) for k, a in enumerate(late_shards)]
    w_in_t, lora_g, conv_g = _gather_shards(shards, placed)
    lora_full = lora_g.reshape(N_SHARD, LORA, 256).transpose(1, 0, 2).reshape(LORA, D_MODEL)
    conv_full = conv_g.reshape(N_SHARD, 32, 2 * D_FF // N_SHARD)[:, :3].transpose(1, 0, 2).reshape(3, 2 * D_FF)
    lrow = lax.broadcasted_iota(jnp.int32, (LORA, 1), 0)
    p = {
        "g1": row(w["attn_pre_norm"]), "lb2": w["hgrn_lb"], "gnorm": row(w["hgrn_gnorm"]), "w_in_t": w_in_t,
        "mu": row(w["rwkv_mu"]), "w0": row(w["rwkv_w0"]), "a0": row(w["rwkv_a0"]),
        "w2p": jnp.where(lrow < 64, lora_full, 0.0), "a2p": jnp.where((lrow >= 64) & (lrow < 128), lora_full, 0.0),
        "g2p": jnp.where(lrow >= 128, lora_full, 0.0),
        "k_k": row(w["rwkv_k_k"]), "k_a": row(w["rwkv_k_a"]), "r_k": row(w["rwkv_r_k"]), "ln_w": row(w["rwkv_ln_w"]),
        "ln_b": row(w["rwkv_ln_b"]), "g_post1": row(w["attn_post_norm"]),
        "g_pre2": row(w["ffn_pre_norm"]), "conv_w": conv_full, "conv_b": row(w["conv_b"]),
        "g_post2": row(w["ffn_post_norm"]),
    }

    loss, grad_x, g = _device_step(x[0], loss_target[0], p, late_shards, late_placed, place)

    g_in_t = g["w_in_t"]
    g_a, g_b, g_o, g_dn, g_up_t = [g[n] for n in _LATE]
    rep_shapes = [w[n].shape for n, _ in _REPLICATED]
    rep = _pack([g[key] for _, key in _REPLICATED])
    n_rep_rows = rep.shape[0]
    cw = 2 * D_FF // N_SHARD
    lora_rows, conv_rows = LORA * 256 // SLAB_COLS, -(-3 * cw // SLAB_COLS)
    lora_g = jnp.concatenate([g["w2p"][0:64], g["a2p"][64:128], g["g2p"][128:256]], axis=0)
    lora_parts = lora_g.reshape(LORA, N_SHARD, 256).transpose(1, 0, 2).reshape(N_SHARD, lora_rows, SLAB_COLS)
    conv_parts = g["conv_w"].reshape(3, N_SHARD, cw).transpose(1, 0, 2).reshape(N_SHARD, 3 * cw)
    conv_parts = jnp.pad(conv_parts, ((0, 0), (0, conv_rows * SLAB_COLS - 3 * cw))).reshape(N_SHARD, conv_rows, SLAB_COLS)
    n_rows = n_rep_rows + lora_rows + conv_rows
    fill = jnp.zeros((N_SHARD, -n_rows % 8, SLAB_COLS), F32)
    parts = jnp.concatenate([jnp.broadcast_to(rep, (N_SHARD,) + rep.shape), lora_parts, conv_parts, fill], axis=1)
    me = 4 * lax.axis_index("x") + 2 * lax.axis_index("y") + lax.axis_index("c")
    landing = lax.dynamic_update_slice(jnp.zeros((8,) + parts.shape[1:], F32),
                                       lax.dynamic_index_in_dim(parts, shard, 0, keepdims=True), (me, 0, 0))

    res = {}

    def put(name, outs, shape=None):
        res[name] = [o.reshape(w[name].shape if shape is None else shape) for o in outs]

    w_in_out, gathered = _adamw("adamw_w_in", w["w_in"][0], g_in_t, mom["w_in"][0], var["w_in"][0], 1024, 128,
                                g_transposed=True, exchange=(parts, landing))
    put("w_in", w_in_out)
    summed = _sum3d("small_sum", [(gathered, functools.partial(lambda l, s, i: i, i=i)) for i in range(8)], place, 1, 1,
                    lambda l, s: 0)[0]
    lora_grad = summed[n_rep_rows:n_rep_rows + lora_rows].reshape(LORA, 256)
    conv_grad = summed[n_rep_rows + lora_rows:n_rows].reshape(-1)[:3 * cw].reshape(3, cw)
    put("w_up", _adamw("adamw_w_up", w["w_up"][0], g_up_t, mom["w_up"][0], var["w_up"][0], 1024, 128, g_transposed=True))
    for name, grad in (("w_branch_a", g_a), ("w_branch_b", g_b), ("w_out", g_o)):
        put(name, _adamw("adamw_" + name, w[name][0], grad, mom[name][0], var[name][0], 256, 1024))
    put("w_down", _adamw("adamw_w_down", w["w_down"][0], g_dn, mom["w_down"][0], var["w_down"][0], 176, 1024))
    put("conv_w", _adamw("adamw_conv_w", w["conv_w"][0], conv_grad, mom["conv_w"][0], var["conv_w"][0], 3, 2 * D_FF // N_SHARD))
    lora_out = _adamw("adamw_lora", lora_of(w), lora_grad, lora_of(mom), lora_of(var), LORA, 256)
    for name, lo, hi in (("rwkv_w2", 0, 64), ("rwkv_a2", 64, 128), ("rwkv_g2", 128, 256)):
        put(name, [o[lo:hi] for o in lora_out])
    rep_names = [n for n, _ in _REPLICATED]
    rep_out = _adamw("adamw_small", _pack([w[n] for n in rep_names]), summed[:n_rep_rows], _pack([mom[n] for n in rep_names]),
                     _pack([var[n] for n in rep_names]), n_rep_rows, SLAB_COLS)
    for name, parts in zip(rep_names, zip(*[_unpack(o, rep_shapes) for o in rep_out])):
        put(name, list(parts))

    loss = lax.psum(loss[0, 0], ("x", "y", "c"))
    return (loss, grad_x[None], *[res[n][0] for n in _WEIGHTS], *[res[n][1] for n in _WEIGHTS],
            *[res[n][2] for n in _WEIGHTS], *[res[n][3] for n in _WEIGHTS])
```

```python
import functools

import jax
import jax.numpy as jnp
from jax import lax
from jax.experimental import pallas as pl
from jax.experimental.pallas import tpu as pltpu

F32, BF16 = jnp.float32, jnp.bfloat16
MESH = pl.DeviceIdType.MESH

D_MODEL = 1024
HGRN_HEADS = 8
HGRN_K = 128
HGRN_SCALE = HGRN_K ** -0.5
CHUNK = 32
RWKV_HEAD = 64
LORA = 256
D_FF = 2816
EPS = 1e-6
GN_EPS = 1e-5 * RWKV_HEAD
N_SHARD = 4
ADAM_LR, ADAM_B1, ADAM_B2, ADAM_EPS, ADAM_WD, ADAM_STEP = 0.001, 0.9, 0.999, 1e-08, 0.01, 10

LANES = 128
VMEM_LIMIT = 56 * 1024 * 1024
SCAN_TB = 32
SCAN_TB_FWD = 32
SCAN_GROUP = 256
LIGHT_TM = 256

C_HQ, C_HF, C_HI, C_HG = 0, 1024, 2048, 3072
C_R = 4096
R_COLS = 3328
C_G = 7424
IN_COLS = 9472


def _params(sem=None, **kw):
    return pltpu.CompilerParams(dimension_semantics=sem, vmem_limit_bytes=VMEM_LIMIT, **kw)


def _seg_matrix(n, seg):
    r = lax.broadcasted_iota(jnp.int32, (n, n), 0) // seg
    c = lax.broadcasted_iota(jnp.int32, (n, n), 1) // seg
    return (r == c).astype(BF16)


def _split3(x):
    hi = x.astype(BF16)
    r1 = x - hi.astype(F32)
    mid = r1.astype(BF16)
    lo = (r1 - mid.astype(F32)).astype(BF16)
    return hi, mid, lo


def _segsum_impl(x, seg):
    e = _seg_matrix(LANES, seg)
    outs = []
    for g in range(x.shape[1] // LANES):
        hi, mid, lo = _split3(x[:, g * LANES:(g + 1) * LANES])
        outs.append(jnp.dot(hi, e, preferred_element_type=F32) + jnp.dot(mid, e, preferred_element_type=F32)
                    + jnp.dot(lo, e, preferred_element_type=F32))
    return outs[0] if len(outs) == 1 else jnp.concatenate(outs, axis=1)


def _make_segsum(seg):
    @jax.custom_vjp
    def f(x):
        return _segsum_impl(x, seg)

    f.defvjp(lambda x: (_segsum_impl(x, seg), None), lambda _, ct: (_segsum_impl(ct, seg),))
    return f


_segsum64 = _make_segsum(RWKV_HEAD)
_segsum128 = _make_segsum(HGRN_K)


def _chunk_mm_impl(x, kind, transposed):
    n = x.shape[0]
    r = lax.broadcasted_iota(jnp.int32, (n, n), 1 if transposed else 0)
    c = lax.broadcasted_iota(jnp.int32, (n, n), 0 if transposed else 1)
    same = (r // CHUNK) == (c // CHUNK)
    if kind == "cumsum":
        m = same & (r >= c)
    else:
        m = same & (c % CHUNK == (CHUNK // 2 - 1 if kind == "mid" else CHUNK - 1))
    m = m.astype(BF16)
    hi, mid, lo = _split3(x)
    return (jnp.dot(m, hi, preferred_element_type=F32) + jnp.dot(m, mid, preferred_element_type=F32)
            + jnp.dot(m, lo, preferred_element_type=F32))


def _make_chunk_mm(kind):
    @jax.custom_vjp
    def f(x):
        return _chunk_mm_impl(x, kind, False)

    f.defvjp(lambda x: (_chunk_mm_impl(x, kind, False), None), lambda _, ct: (_chunk_mm_impl(ct, kind, True),))
    return f


_chunk_cumsum = _make_chunk_mm("cumsum")
_chunk_mid = _make_chunk_mm("mid")
_chunk_last = _make_chunk_mm("last")


@jax.custom_vjp
def _bdot(x, w):
    return jnp.dot(x.astype(BF16), w.astype(BF16), preferred_element_type=F32)


def _bdot_fwd(x, w):
    return _bdot(x, w), (x, w)


def _bdot_bwd(res, ct):
    x, w = res
    ctb = ct.astype(BF16)
    dx = lax.dot_general(ctb, w.astype(BF16), (((1,), (1,)), ((), ())), preferred_element_type=F32)
    dw = lax.dot_general(x.astype(BF16), ctb, (((0,), (0,)), ((), ())), preferred_element_type=F32)
    return dx, dw


_bdot.defvjp(_bdot_fwd, _bdot_bwd)


def _sigmoid(x):
    return 1.0 / (1.0 + jnp.exp(-x))


def _silu(x):
    return x * _sigmoid(x)


def _softplus(x):
    return jnp.maximum(x, 0.0) + jnp.log(1.0 + jnp.exp(-jnp.abs(x)))


def _rms(x, g):
    return x * lax.rsqrt(jnp.mean(x * x, axis=-1, keepdims=True) + EPS) * g


def _fn_norm(t, p):
    return [_rms(t[0], p[0])]


def _fn_hgates(t, p):
    hq, hf = t
    lb2 = p[0]
    m = jnp.max(lb2, axis=0, keepdims=True)
    e = jnp.exp(lb2 - m)
    first = lax.broadcasted_iota(jnp.int32, e.shape, 0) == 0
    lb = jnp.sum(jnp.where(first, e, 0.0), axis=0, keepdims=True) / jnp.sum(e, axis=0, keepdims=True)
    f = lb + (1.0 - lb) * _sigmoid(hf)
    q, k = _silu(hq) * HGRN_SCALE, 1.0 - f
    b = _chunk_cumsum(jnp.log(f))
    b_ref, b_last = _chunk_mid(b), _chunk_last(b)
    return [q * jnp.exp(b - b_ref), k * jnp.exp(b_ref - b), k * jnp.exp(b_last - b), q * jnp.exp(b), jnp.exp(b_last)]


def _fn_hpost(t, p):
    o, hg = t
    ms = _segsum128(o * o) * (1.0 / HGRN_K)
    return [o * lax.rsqrt(ms + EPS) * p[0] * _silu(hg)]


def _fn_rprep(t, p):
    kr, lora = t
    w0, w2p, a0, a2p, g2p, k_k, k_a = p
    pre_w = w0 + _bdot(jnp.tanh(lora), w2p)
    w_log = -_softplus(-pre_w) - 0.5
    decay = jnp.exp(-jnp.exp(w_log))
    a = _sigmoid(a0 + _bdot(lora, a2p))
    g = _bdot(_sigmoid(lora), g2p)
    kk = kr * k_k
    kk = kk / jnp.maximum(jnp.sqrt(_segsum64(kk * kk)), 1e-12)
    kr2 = kr * (1.0 + (a - 1.0) * k_a)
    return [decay, kr2, -kk, kk * a, g]


def _fn_rpost(t, p):
    y, r, kr2, v, g = t
    ln_w, ln_b, r_k = p
    mu = _segsum64(y) * (1.0 / RWKV_HEAD)
    yc = y - mu
    var = _segsum64(yc * yc) * (1.0 / RWKV_HEAD)
    yn = yc * lax.rsqrt(var + GN_EPS) * ln_w + ln_b
    bonus = _segsum64(r * kr2 * r_k) * v
    return [(yn + bonus) * g]


def _fn_merge(t, p):
    ga, gb, ya, yb = t
    return [_sigmoid(ga) * ya + _sigmoid(gb) * yb]


def _fn_res1(t, p):
    x, mix = t
    h1 = x + _rms(mix, p[0])
    return [h1, _rms(h1, p[1])]


def _tok_call(name, fn, toks, params, outs, red_shapes=(), tm=128, col_grid=1, sibling_rider=None):
    n_t, n_p, n_o, n_red = len(toks), len(params), len(outs), len(red_shapes)
    n_r = 0 if sibling_rider is None else len(sibling_rider)
    t_len = toks[0][0].shape[0]
    tm = min(tm, t_len)
    grid = (t_len // tm, col_grid)

    def body(*refs):
        first_out = n_t + n_p + n_r
        if n_r:
            rider = (refs[n_t + n_p:first_out], refs[first_out + n_o + n_red:first_out + n_o + n_red + n_r], *refs[-2:])

            @pl.when((pl.program_id(0) == 0) & (pl.program_id(1) == 0))
            def _():
                _rs_sibling_start(*rider)

        tv = [r[...].astype(F32) for r in refs[:n_t]]
        pv = [r[...] for r in refs[n_t:n_t + n_p]]
        o, red = fn(tv, pv)
        for ref, val in zip(refs[first_out:first_out + n_o], o):
            ref[...] = val.astype(ref.dtype)
        if n_r:
            @pl.when((pl.program_id(0) == grid[0] - 1) & (pl.program_id(1) == grid[1] - 1))
            def _():
                _rs_sibling_finish(*rider)

        red_refs = refs[first_out + n_o:first_out + n_o + n_red]
        if red_refs:
            first = pl.program_id(0) == 0

            @pl.when(first)
            def _():
                for ref, val in zip(red_refs, red):
                    ref[...] = val

            @pl.when(jnp.logical_not(first))
            def _():
                for ref, val in zip(red_refs, red):
                    ref[...] += val

    in_specs = [pl.BlockSpec((tm, w), functools.partial(lambda i, j, c: (i, c + j), c=c)) for (_, w, c) in toks]
    in_specs += [pl.BlockSpec(p.shape, lambda i, j: (0, 0)) for p in params]
    out_specs = [pl.BlockSpec((tm, w), lambda i, j: (i, j)) for (w, _) in outs]
    out_specs += [pl.BlockSpec(s, lambda i, j: (0, 0)) for s in red_shapes]
    out_shape = [jax.ShapeDtypeStruct((t_len, w * col_grid), dt) for (w, dt) in outs]
    out_shape += [jax.ShapeDtypeStruct(s, F32) for s in red_shapes]
    scratch = []
    if n_r:
        in_specs += [ANY] * n_r
        out_specs += [ANY] * n_r
        out_shape += [jax.ShapeDtypeStruct((N_SHARD,) + a.shape[1:], a.dtype) for a in sibling_rider]
        scratch = [pltpu.SemaphoreType.DMA((n_r, N_SHARD)), pltpu.SemaphoreType.DMA((n_r, N_SHARD))]
    return pl.pallas_call(
        body, name=name, grid=grid, in_specs=in_specs, out_specs=out_specs, out_shape=out_shape, scratch_shapes=scratch,
        compiler_params=_params(("arbitrary", "arbitrary")),
    )(*[a for (a, _, _) in toks], *params, *(sibling_rider or []))


def _tok_fwd(name, fn, toks, params, outs, **kw):
    return _tok_call(name, lambda tv, pv: (fn(tv, pv), []), toks, params, outs, **kw)


def _tok_bwd(name, fn, toks, params, cts, want, add_to_first=None, **kw):
    n_t = len(toks)
    flat = [c for group in cts for c in group]
    extra = [] if add_to_first is None else [add_to_first]

    def bwd(tv, pv):
        prim, rest = tv[:n_t], tv[n_t:]
        ct, at = [], 0
        for group in cts:
            ct.append(functools.reduce(lambda u, v: u + v, rest[at:at + len(group)]))
            at += len(group)
        _, vjp = jax.vjp(lambda *a: tuple(fn(list(a[:n_t]), list(a[n_t:]))), *prim, *pv)
        g = vjp(tuple(ct))
        tok_grads = [g[i] for i in range(n_t) if want[i] is not None]
        if extra:
            tok_grads[0] = tok_grads[0] + rest[at]
        return tok_grads, list(g[n_t:])

    return _tok_call(name, bwd, list(toks) + flat + extra, params, [w for w in want if w is not None],
                     red_shapes=[p.shape for p in params], **kw)


def _mm(name, a, b, mode, out_dtype=F32, tm=None, tn=None, tk=None, riders=None, a_map=None, mk=None):
    if mode == "nn":
        (m, k), (_, n) = a.shape, b.shape
    elif mode == "nt":
        (m, k), (n, _) = a.shape, b.shape
    else:
        (k, m), (_, n) = a.shape, b.shape
    if mk is not None:
        m, k = mk
    tm = (512 if mode == "tn" else 2048) if tm is None else tm
    tn = (512 if mode == "tn" else 256) if tn is None else tn
    tk = k if tk is None else tk
    tm, tn = min(tm, m), min(tn, n)
    nk = k // tk
    assert m % tm == 0 and n % tn == 0 and k % tk == 0, (name, a.shape, b.shape, tm, tn, tk)
    a_spec = pl.BlockSpec((tk, tm), lambda i, j, q: (q, i)) if mode == "tn" else pl.BlockSpec((tm, tk), lambda i, j, q: (i, q))
    if a_map is not None:
        a_spec = pl.BlockSpec(a_spec.block_shape, a_map)
    b_spec = pl.BlockSpec((tn, tk), lambda i, j, q: (j, q)) if mode == "nt" else pl.BlockSpec((tk, tn), lambda i, j, q: (q, j))
    dn = {"nn": (((1,), (0,)), ((), ())), "nt": (((1,), (1,)), ((), ())), "tn": (((0,), (0,)), ((), ()))}[mode]
    grid = (m // tm, n // tn, nk)
    nr = 0 if riders is None else len(riders)

    def body(*refs):
        a_ref, b_ref, o_ref = refs[0], refs[1], refs[2 + nr]
        acc = refs[3 + 2 * nr] if nk > 1 else None
        if nr:
            exchange = (refs[2:2 + nr], refs[3 + nr:3 + 2 * nr], *refs[-2:])
            at = [pl.program_id(ax) for ax in range(3)]

            @pl.when((at[0] == 0) & (at[1] == 0) & (at[2] == 0))
            def _():
                _rs_chips_start(*exchange)

        p = lax.dot_general(a_ref[...], b_ref[...], dn, preferred_element_type=F32)
        if nk == 1:
            o_ref[...] = p.astype(o_ref.dtype)
        else:
            q = pl.program_id(2)

            @pl.when(q == 0)
            def _():
                acc[...] = p

            @pl.when(q > 0)
            def _():
                acc[...] += p

            @pl.when(q == nk - 1)
            def _():
                o_ref[...] = acc[...].astype(o_ref.dtype)

        if nr:
            @pl.when((at[0] == grid[0] - 1) & (at[1] == grid[1] - 1) & (at[2] == grid[2] - 1))
            def _():
                _rs_chips_finish(*exchange)

    scratch = [pltpu.VMEM((tm, tn), F32)] if nk > 1 else []
    out_specs = [pl.BlockSpec((tm, tn), lambda i, j, q: (i, j))]
    out_shape = [jax.ShapeDtypeStruct((m, n), out_dtype)]
    if nr:
        scratch += [pltpu.SemaphoreType.DMA((nr, 3)), pltpu.SemaphoreType.DMA((nr, 3))]
        out_specs += [ANY] * nr
        out_shape += [jax.ShapeDtypeStruct((3,) + r.shape[1:], r.dtype) for r in riders]
    outs = pl.pallas_call(
        body, name=name, grid=grid, in_specs=[a_spec, b_spec] + [ANY] * nr, out_specs=out_specs, out_shape=out_shape,
        scratch_shapes=scratch,
        compiler_params=_params(("arbitrary",) * 3 if nr else ("parallel", "parallel", "arbitrary")),
    )(a, b, *(riders or []))
    return (outs[0], outs[1:]) if nr else outs[0]


def _shift_down(z, n):
    rows = lax.broadcasted_iota(jnp.int32, z.shape, 0)
    return jnp.where(rows < n, 0.0, pltpu.roll(z, n, 0))


def _shift_up(z, n):
    t = z.shape[0]
    rows = lax.broadcasted_iota(jnp.int32, z.shape, 0)
    return jnp.where(rows >= t - n, 0.0, pltpu.roll(z, t - n, 0))


def _lerp_fwd(z, mu):
    t = z.shape[0]
    w = 256

    def body(z_ref, mu_ref, o_ref):
        zz = z_ref[...]
        o_ref[...] = zz + mu_ref[...] * (_shift_down(zz, 1) - zz)

    return pl.pallas_call(
        body, name="lerp_fwd", grid=(R_COLS // w,),
        in_specs=[pl.BlockSpec((t, w), lambda j: (0, C_R // w + j)), pl.BlockSpec((1, w), lambda j: (0, j))],
        out_specs=pl.BlockSpec((t, w), lambda j: (0, j)), out_shape=jax.ShapeDtypeStruct((t, R_COLS), F32),
        compiler_params=_params(("parallel",)),
    )(z, mu)


def _lerp_bwd(z, mu, d_r, d_k, d_v, d_lora):
    t = z.shape[0]
    w = 256
    per = D_MODEL // w

    def body(z_ref, mu_ref, r1_ref, r2_ref, k_ref, v1_ref, v2_ref, l_ref, dz_ref, dmu_ref):
        j = pl.program_id(0)
        zz, m = z_ref[...], mu_ref[...]
        d = jnp.where(j < per, r1_ref[...] + r2_ref[...],
                      jnp.where(j < 2 * per, k_ref[...], jnp.where(j < 3 * per, v1_ref[...] + v2_ref[...], l_ref[...])))
        dz_ref[...] = (d * (1.0 - m) + _shift_up(d * m, 1)).astype(dz_ref.dtype)
        dmu_ref[...] = jnp.sum(d * (_shift_down(zz, 1) - zz), axis=0, keepdims=True)

    piece = lambda first: pl.BlockSpec((t, w), lambda j: (0, jnp.clip(j - first, 0, per - 1)))
    return pl.pallas_call(
        body, name="lerp_bwd", grid=(R_COLS // w,),
        in_specs=[pl.BlockSpec((t, w), lambda j: (0, C_R // w + j)), pl.BlockSpec((1, w), lambda j: (0, j)),
                  piece(0), piece(0), piece(per), piece(2 * per), piece(2 * per), pl.BlockSpec((t, w), lambda j: (0, 0))],
        out_specs=[pl.BlockSpec((t, w), lambda j: (0, j)), pl.BlockSpec((1, w), lambda j: (0, j))],
        out_shape=[jax.ShapeDtypeStruct((t, R_COLS), BF16), jax.ShapeDtypeStruct((1, R_COLS), F32)],
        compiler_params=_params(("arbitrary",)),
    )(z, mu, *d_r, d_k, *d_v, d_lora)


CONV_TILE = 256
N_CONV_TILES = D_FF // CONV_TILE


def _conv(h, w, b):
    return b + w[0:1, :] * _shift_down(h, 2) + w[1:2, :] * _shift_down(h, 1) + w[2:3, :] * h


def _conv_fwd(hu, conv_w, conv_b):
    t = hu.shape[0]
    n = N_CONV_TILES

    def body(hg_ref, hv_ref, wg_ref, wv_ref, bg_ref, bv_ref, o_ref):
        gate = _conv(hg_ref[...], wg_ref[...], bg_ref[...])
        val = _conv(hv_ref[...], wv_ref[...], bv_ref[...])
        o_ref[...] = (_silu(gate) * val).astype(o_ref.dtype)

    col = lambda off: pl.BlockSpec((t, CONV_TILE), lambda j: (0, j + off))
    wspec = lambda off: pl.BlockSpec((3, CONV_TILE), lambda j: (0, j + off))
    bspec = lambda off: pl.BlockSpec((1, CONV_TILE), lambda j: (0, j + off))
    return pl.pallas_call(
        body, name="conv_fwd", grid=(n,),
        in_specs=[col(0), col(n), wspec(0), wspec(n), bspec(0), bspec(n)],
        out_specs=pl.BlockSpec((t, CONV_TILE), lambda j: (0, j)), out_shape=jax.ShapeDtypeStruct((t, D_FF), BF16),
        compiler_params=_params(("parallel",)),
    )(hu, hu, conv_w, conv_w, conv_b, conv_b)


def _conv_bwd(hu, conv_w, conv_b, d_act):
    t = hu.shape[0]
    n = N_CONV_TILES

    def body(hg_ref, hv_ref, wg_ref, wv_ref, bg_ref, bv_ref, d_ref, dh_ref, dw_ref, db_ref):
        hg, hv, wg, wv = hg_ref[...], hv_ref[...], wg_ref[...], wv_ref[...]
        gate = _conv(hg, wg, bg_ref[...])
        val = _conv(hv, wv, bv_ref[...])
        d = d_ref[...]
        sg = _sigmoid(gate)
        d_gate = d * val * (sg * (1.0 + gate * (1.0 - sg)))
        d_val = d * (gate * sg)
        for half, (dc, h, w) in enumerate(((d_gate, hg, wg), (d_val, hv, wv))):
            dh = w[2:3, :] * dc + w[1:2, :] * _shift_up(dc, 1) + w[0:1, :] * _shift_up(dc, 2)
            dh_ref[half] = dh.astype(dh_ref.dtype)
            dw_ref[half, 0:1, :] = jnp.sum(dc * _shift_down(h, 2), axis=0, keepdims=True)
            dw_ref[half, 1:2, :] = jnp.sum(dc * _shift_down(h, 1), axis=0, keepdims=True)
            dw_ref[half, 2:3, :] = jnp.sum(dc * h, axis=0, keepdims=True)
            db_ref[half] = jnp.sum(dc, axis=0, keepdims=True)

    gcol = lambda rows: pl.BlockSpec((rows, CONV_TILE), lambda j: (0, j))
    vcol = lambda rows: pl.BlockSpec((rows, CONV_TILE), lambda j: (0, j + n))
    both = lambda rows: pl.BlockSpec((2, rows, CONV_TILE), lambda j: (0, 0, j))
    return pl.pallas_call(
        body, name="conv_bwd", grid=(n,),
        in_specs=[gcol(t), vcol(t), gcol(3), vcol(3), gcol(1), vcol(1), gcol(t)],
        out_specs=[both(t), both(3), both(1)],
        out_shape=[jax.ShapeDtypeStruct((2, t, D_FF), BF16), jax.ShapeDtypeStruct((2, 3, D_FF), F32),
                   jax.ShapeDtypeStruct((2, 1, D_FF), F32)],
        compiler_params=_params(("parallel",)),
    )(hu, hu, conv_w, conv_w, conv_b, conv_b, d_act)


_NN = (((1,), (0,)), ((), ()))
_NT = (((1,), (1,)), ((), ()))
_TN = (((0,), (0,)), ((), ()))
HGRN_CB = 8
HGRN_LOCAL_CB = 16


def _bf_dot(a, b, dn):
    return lax.dot_general(a.astype(BF16), b.astype(BF16), dn, preferred_element_type=F32)


def _tril():
    n = HGRN_LOCAL_CB * CHUNK
    r = lax.broadcasted_iota(jnp.int32, (n, n), 0)
    c = lax.broadcasted_iota(jnp.int32, (n, n), 1)
    return (r // CHUNK == c // CHUNK) & (r >= c)


def _hgrn_specs(t):
    rows = HGRN_LOCAL_CB * CHUNK
    head = pl.BlockSpec((rows, HGRN_K), lambda h, n: (n, h))
    v_head = pl.BlockSpec((rows, HGRN_K), lambda h, n: (n, C_HI // HGRN_K + h))
    mats = pl.BlockSpec((1, HGRN_LOCAL_CB, HGRN_K, HGRN_K), lambda h, n: (h, n, 0, 0))
    return head, v_head, mats, (HGRN_HEADS, t // rows)


def _hgrn_local_fwd(q_in, k_in, kd, z):
    t = q_in.shape[0]
    head, v_head, mats, grid = _hgrn_specs(t)

    def body(q_ref, k_ref, kd_ref, v_ref, o_ref, u_ref):
        v = v_ref[...]
        scores = jnp.where(_tril(), _bf_dot(q_ref[...], k_ref[...], _NT), 0.0)
        o_ref[...] = _bf_dot(scores, v, _NN)
        for n in range(HGRN_LOCAL_CB):
            rows = slice(n * CHUNK, (n + 1) * CHUNK)
            u_ref[0, n] = _bf_dot(v[rows], kd_ref[rows, :], _TN)

    return pl.pallas_call(
        body, name="hgrn_local_fwd", grid=grid, in_specs=[head, head, head, v_head], out_specs=[head, mats],
        out_shape=[jax.ShapeDtypeStruct((t, D_MODEL), F32),
                   jax.ShapeDtypeStruct((HGRN_HEADS, t // CHUNK, HGRN_K, HGRN_K), F32)],
        compiler_params=_params(("parallel", "parallel")),
    )(q_in, k_in, kd, z)


def _hgrn_state_specs(t, reverse=False):
    rows = HGRN_CB * CHUNK
    nb = t // rows
    at = (lambda n: nb - 1 - n) if reverse else (lambda n: n)
    tok = pl.BlockSpec((rows, D_MODEL), lambda n: (at(n), 0))
    mats = pl.BlockSpec((HGRN_HEADS, HGRN_CB, HGRN_K, HGRN_K), lambda n: (0, at(n), 0, 0))
    return tok, mats, nb


def _hgrn_state_fwd(o_intra, qe, dec, u):
    t = qe.shape[0]
    tok, mats, nb = _hgrn_state_specs(t)

    def body(oi_ref, qe_ref, dec_ref, u_ref, o_ref, st_ref, s_ref):
        @pl.when(pl.program_id(0) == 0)
        def _():
            s_ref[...] = jnp.zeros_like(s_ref)

        st = [s_ref[h] for h in range(HGRN_HEADS)]
        for n in range(HGRN_CB):
            rows = slice(n * CHUNK, (n + 1) * CHUNK)
            for h in range(HGRN_HEADS):
                cols = slice(h * HGRN_K, (h + 1) * HGRN_K)
                st_ref[h, n] = st[h]
                o_ref[rows, cols] = oi_ref[rows, cols] + _bf_dot(qe_ref[rows, cols], st[h], _NT)
                st[h] = st[h] * dec_ref[n * CHUNK:n * CHUNK + 1, cols] + u_ref[h, n]
        for h in range(HGRN_HEADS):
            s_ref[h] = st[h]

    return pl.pallas_call(
        body, name="hgrn_state_fwd", grid=(nb,), in_specs=[tok, tok, tok, mats], out_specs=[tok, mats],
        out_shape=[jax.ShapeDtypeStruct((t, D_MODEL), F32),
                   jax.ShapeDtypeStruct((HGRN_HEADS, t // CHUNK, HGRN_K, HGRN_K), F32)],
        scratch_shapes=[pltpu.VMEM((HGRN_HEADS, HGRN_K, HGRN_K), F32)],
        compiler_params=_params(("arbitrary",)),
    )(o_intra, qe, dec, u)


def _hgrn_state_bwd(d_o, qe, dec, states):
    t = qe.shape[0]
    tok_r, mats_r, nb = _hgrn_state_specs(t, reverse=True)

    def body(do_ref, qe_ref, dec_ref, st_ref, dqe_ref, ddec_ref, du_ref, d_ref):
        @pl.when(pl.program_id(0) == 0)
        def _():
            d_ref[...] = jnp.zeros_like(d_ref)

        first_row = lax.broadcasted_iota(jnp.int32, (CHUNK, HGRN_K), 0) == 0
        d = [d_ref[h] for h in range(HGRN_HEADS)]
        for n in reversed(range(HGRN_CB)):
            rows = slice(n * CHUNK, (n + 1) * CHUNK)
            for h in range(HGRN_HEADS):
                cols = slice(h * HGRN_K, (h + 1) * HGRN_K)
                st, do = st_ref[h, n], do_ref[rows, cols]
                du_ref[h, n] = d[h]
                ddec_ref[rows, cols] = jnp.where(first_row, jnp.sum(d[h] * st, axis=0, keepdims=True), 0.0)
                dqe_ref[rows, cols] = _bf_dot(do, st, _NN)
                d[h] = d[h] * dec_ref[n * CHUNK:n * CHUNK + 1, cols] + _bf_dot(do, qe_ref[rows, cols], _TN)
        for h in range(HGRN_HEADS):
            d_ref[h] = d[h]

    out = jax.ShapeDtypeStruct((t, D_MODEL), F32)
    return pl.pallas_call(
        body, name="hgrn_state_bwd", grid=(nb,), in_specs=[tok_r, tok_r, tok_r, mats_r], out_specs=[tok_r, tok_r, mats_r],
        out_shape=[out, out, jax.ShapeDtypeStruct((HGRN_HEADS, t // CHUNK, HGRN_K, HGRN_K), F32)],
        scratch_shapes=[pltpu.VMEM((HGRN_HEADS, HGRN_K, HGRN_K), F32)],
        compiler_params=_params(("arbitrary",)),
    )(d_o, qe, dec, states)


def _hgrn_local_bwd(q_in, k_in, kd, z, d_o, d_u):
    t = q_in.shape[0]
    head, v_head, mats, grid = _hgrn_specs(t)

    def body(q_ref, k_ref, kd_ref, v_ref, do_ref, du_ref, dq_ref, dk_ref, dkd_ref, dv_ref):
        tril = _tril()
        q, k, v, do = q_ref[...], k_ref[...], v_ref[...], do_ref[...]
        scores = jnp.where(tril, _bf_dot(q, k, _NT), 0.0)
        d_scores = jnp.where(tril, _bf_dot(do, v, _NT), 0.0)
        dq_ref[...] = _bf_dot(d_scores, k, _NN)
        dk_ref[...] = _bf_dot(d_scores, q, _TN)
        dv = _bf_dot(scores, do, _TN)
        for n in range(HGRN_LOCAL_CB):
            rows = slice(n * CHUNK, (n + 1) * CHUNK)
            du = du_ref[0, n]
            dv_ref[rows, :] = dv[rows] + _bf_dot(kd_ref[rows, :], du, _NT)
            dkd_ref[rows, :] = _bf_dot(v[rows], du, _NN)

    out = jax.ShapeDtypeStruct((t, D_MODEL), F32)
    return pl.pallas_call(
        body, name="hgrn_local_bwd", grid=grid, in_specs=[head, head, head, v_head, head, mats], out_specs=[head] * 4,
        out_shape=[out] * 4, compiler_params=_params(("parallel", "parallel")),
    )(q_in, k_in, kd, z, d_o, d_u)


def _seg_bcast(xs, e):
    n = RWKV_HEAD
    lhs = [x.astype(BF16) for x in xs]
    out = jnp.dot(lhs[0] if len(lhs) == 1 else jnp.concatenate(lhs, axis=0), e, preferred_element_type=F32)
    return [out[i * n:(i + 1) * n] for i in range(len(xs))]


def _rows_to_cols(rows, diag, e):
    zero = jnp.zeros((), BF16)
    parts = [jnp.where(diag, row.astype(BF16), zero) for row in rows]
    out = jnp.dot(jnp.concatenate(parts, axis=0), e, preferred_element_type=F32)
    n = RWKV_HEAD
    return [out[i * n:(i + 1) * n] for i in range(len(rows))]


def _col_to_row(col, diag):
    return jnp.sum(jnp.where(diag, col, 0.0), axis=0, keepdims=True)


_SCAN_PAIRS = ((0, 1, 2, 3),)


def _scan_consts():
    e = _seg_matrix(SCAN_GROUP, RWKV_HEAD)
    i = lax.broadcasted_iota(jnp.int32, (RWKV_HEAD, SCAN_GROUP), 0)
    l = lax.broadcasted_iota(jnp.int32, (RWKV_HEAD, SCAN_GROUP), 1)
    groups = [slice(g * SCAN_GROUP, (g + 1) * SCAN_GROUP) for g in range(D_MODEL // SCAN_GROUP)]
    return e, (l % RWKV_HEAD) == i, groups


def _rwkv_fwd(zl, w, k, a, b, shards, placed):
    t = zl.shape[0]
    tb = SCAN_TB_FWD
    nb = t // tb
    n = len(shards)
    steps = range(tb)

    def body(*refs):
        scan(*refs[:6], *refs[6 + 2 * n:8 + 2 * n], refs[8 + 3 * n])
        gather = (refs[6:6 + n], refs[8 + 2 * n:8 + 3 * n], *refs[9 + 3 * n:])

        @pl.when(pl.program_id(0) == 0)
        def _():
            _gather_start(*gather)

        @pl.when(pl.program_id(0) == nb - 1)
        def _():
            _gather_finish(*gather)

    def scan(r_ref, w_ref, k_ref, v_ref, a_ref, b_ref, y_ref, st_ref, s_ref):
        @pl.when(pl.program_id(0) == 0)
        def _():
            s_ref[...] = jnp.zeros_like(s_ref)

        e, diag, groups = _scan_consts()
        v_cols = [_rows_to_cols([v_ref[i:i + 1, sl] for i in steps], diag, e) for sl in groups]
        s = [s_ref[:, sl] for sl in groups]
        for i in steps:
            for pair in _SCAN_PAIRS:
                sas = _seg_bcast([s[g] * a_ref[i:i + 1, groups[g]] for g in pair], e)
                for g, sa in zip(pair, sas):
                    sl = groups[g]
                    s[g] = s[g] * w_ref[i:i + 1, sl] + sa * b_ref[i:i + 1, sl] + v_cols[g][i] * k_ref[i:i + 1, sl]
                    st_ref[i, :, sl] = s[g]
        for g, sl in enumerate(groups):
            s_ref[:, sl] = s[g]
            y_cols = _seg_bcast([st_ref[i, :, sl] * r_ref[i:i + 1, sl] for i in steps], e)
            for i in steps:
                y_ref[i:i + 1, sl] = _col_to_row(y_cols[i], diag)

    blk = pl.BlockSpec((tb, D_MODEL), lambda n: (n, 0))
    v_blk = pl.BlockSpec((tb, D_MODEL), lambda n: (n, 2))
    outs = pl.pallas_call(
        body, name="rwkv_fwd", grid=(nb,), in_specs=[blk, blk, blk, v_blk, blk, blk] + [ANY] * (2 * n),
        out_specs=[blk, pl.BlockSpec((tb, RWKV_HEAD, D_MODEL), lambda i: (i, 0, 0))] + [ANY] * n,
        out_shape=[jax.ShapeDtypeStruct((t, D_MODEL), F32), jax.ShapeDtypeStruct((t, RWKV_HEAD, D_MODEL), F32)]
        + [jax.ShapeDtypeStruct(p.shape, p.dtype) for p in placed],
        input_output_aliases={6 + n + i: 2 + i for i in range(n)},
        scratch_shapes=[pltpu.VMEM((RWKV_HEAD, D_MODEL), F32), pltpu.SemaphoreType.DMA((n, 6)), pltpu.SemaphoreType.DMA((n, 6))],
        compiler_params=_params(("arbitrary",)),
    )(zl, w, k, zl, a, b, *shards, *placed)
    return outs[0], outs[1], outs[2:]


def _rwkv_bwd(zl, w, k, a, b, states, d_y, parts):
    t = zl.shape[0]
    nb = t // SCAN_TB
    n = len(parts)
    steps = range(SCAN_TB)

    def body(*refs):
        scan(*refs[:9], *refs[9 + n:15 + n], refs[15 + 2 * n])
        exchange = (refs[9:9 + n], refs[15 + n:15 + 2 * n], *refs[16 + 2 * n:])

        @pl.when(pl.program_id(0) == 0)
        def _():
            _rs_chips_start(*exchange)

        @pl.when(pl.program_id(0) == nb - 1)
        def _():
            _rs_chips_finish(*exchange)

    def scan(r_ref, w_ref, k_ref, v_ref, a_ref, b_ref, st_ref, prev_ref, dy_ref,
             dr_ref, dw_ref, dk_ref, dv_ref, da_ref, db_ref, ds_ref):
        @pl.when(pl.program_id(0) == 0)
        def _():
            ds_ref[...] = jnp.zeros_like(ds_ref)

        has_prev = (pl.program_id(0) < nb - 1).astype(F32)
        e, diag, groups = _scan_consts()
        colsum = lambda x: jnp.sum(x, axis=0, keepdims=True)

        def s_prev(i, sl):
            return st_ref[i - 1, :, sl] if i > 0 else prev_ref[0, :, sl] * has_prev

        dy_cols = [_rows_to_cols([dy_ref[i:i + 1, sl] for i in steps], diag, e) for sl in groups]
        v_cols = [_rows_to_cols([v_ref[i:i + 1, sl] for i in steps], diag, e) for sl in groups]
        sa_cols = [_seg_bcast([s_prev(i, sl) * a_ref[i:i + 1, sl] for i in steps], e) for sl in groups]
        ds = [ds_ref[:, sl] for sl in groups]
        dsk = [[None] * SCAN_TB for _ in groups]
        for i in reversed(steps):
            for pair in _SCAN_PAIRS:
                d = {}
                for g in pair:
                    sl = groups[g]
                    d[g] = ds[g] + dy_cols[g][i] * r_ref[i:i + 1, sl]
                    dr_ref[i:i + 1, sl] = colsum(st_ref[i, :, sl] * dy_cols[g][i])
                    dw_ref[i:i + 1, sl] = colsum(d[g] * s_prev(i, sl))
                    db_ref[i:i + 1, sl] = colsum(d[g] * sa_cols[g][i])
                    dk_ref[i:i + 1, sl] = colsum(d[g] * v_cols[g][i])
                    dsk[g][i] = d[g] * k_ref[i:i + 1, sl]
                dsas = _seg_bcast([d[g] * b_ref[i:i + 1, groups[g]] for g in pair], e)
                for g, dsa in zip(pair, dsas):
                    sl = groups[g]
                    da_ref[i:i + 1, sl] = colsum(s_prev(i, sl) * dsa)
                    ds[g] = d[g] * w_ref[i:i + 1, sl] + dsa * a_ref[i:i + 1, sl]
        for g, sl in enumerate(groups):
            ds_ref[:, sl] = ds[g]
            dv_cols = _seg_bcast(dsk[g], e)
            for i in steps:
                dv_ref[i:i + 1, sl] = _col_to_row(dv_cols[i], diag)

    blk = pl.BlockSpec((SCAN_TB, D_MODEL), lambda n: (nb - 1 - n, 0))
    v_blk = pl.BlockSpec((SCAN_TB, D_MODEL), lambda n: (nb - 1 - n, 2))
    out = jax.ShapeDtypeStruct((t, D_MODEL), F32)
    outs = pl.pallas_call(
        body, name="rwkv_bwd", grid=(nb,),
        in_specs=[blk, blk, blk, v_blk, blk, blk] + [
            pl.BlockSpec((SCAN_TB, RWKV_HEAD, D_MODEL), lambda i: (nb - 1 - i, 0, 0)),
            pl.BlockSpec((1, RWKV_HEAD, D_MODEL), lambda i: (jnp.maximum((nb - 1 - i) * SCAN_TB - 1, 0), 0, 0)),
            blk] + [ANY] * n,
        out_specs=[blk] * 6 + [ANY] * n,
        out_shape=[out] * 6 + [jax.ShapeDtypeStruct((3,) + p.shape[1:], p.dtype) for p in parts],
        scratch_shapes=[pltpu.VMEM((RWKV_HEAD, D_MODEL), F32), pltpu.SemaphoreType.DMA((n, 3)), pltpu.SemaphoreType.DMA((n, 3))],
        compiler_params=_params(("arbitrary",)),
    )(zl, w, k, zl, a, b, states, states, d_y, *parts)
    return outs[:6], outs[6:]


def _loss_head(h1, ff, target, g_post):
    def fn(tv, pv):
        a, f, tgt = tv
        h2, vjp = jax.vjp(lambda a_, f_, g_: a_ + _rms(f_, g_), a, f, pv[0])
        err = h2 - tgt
        loss = 0.5 * jnp.sum(jnp.mean(err * err, axis=-1, keepdims=True), axis=0, keepdims=True)
        d_a, d_f, d_g = vjp(err * (1.0 / D_MODEL))
        return [d_a, d_f], [loss, d_g]

    return _tok_call("loss_head", fn, [(h1, D_MODEL, 0), (ff, D_MODEL, 0), (target, D_MODEL, 0)], [g_post],
                     [(D_MODEL, F32), (D_MODEL, BF16)], red_shapes=[(1, 1), (1, D_MODEL)], tm=LIGHT_TM)


def _sum_call(name, terms, rows_per_block=None):
    a0, i0 = terms[0]
    r, c = a0.shape[-2:]
    tr = rows_per_block or r

    def body(*refs):
        acc = refs[0][...].reshape(tr, c)
        for ref in refs[1:-1]:
            acc = acc + ref[...].reshape(tr, c)
        refs[-1][...] = acc

    def spec(arr, idx):
        if arr.ndim == 2:
            return pl.BlockSpec((tr, c), lambda i: (i, 0))
        return pl.BlockSpec((1, tr, c), functools.partial(lambda i, idx: (idx, i, 0), idx=idx))

    return pl.pallas_call(
        body, name=name, grid=(r // tr,), in_specs=[spec(a, i) for a, i in terms],
        out_specs=pl.BlockSpec((tr, c), lambda i: (i, 0)), out_shape=jax.ShapeDtypeStruct((r, c), F32),
        compiler_params=_params(("parallel",)),
    )(*[a for a, _ in terms])


def _adamw_math(w, g, m, v):
    m2 = ADAM_B1 * m + (1.0 - ADAM_B1) * g
    v2 = ADAM_B2 * v + (1.0 - ADAM_B2) * (g * g)
    m_hat = m2 / (1.0 - ADAM_B1 ** ADAM_STEP)
    v_hat = v2 / (1.0 - ADAM_B2 ** ADAM_STEP)
    return -ADAM_LR * (m_hat / (jnp.sqrt(v_hat) + ADAM_EPS) + ADAM_WD * w), m2, v2


def _adamw(name, w, g, m, v, bm, bn, g_transposed=False, exchange=None):
    r, c = w.shape
    grid = (pl.cdiv(r, bm), pl.cdiv(c, bn))

    def body(*refs):
        w_ref, g_ref, m_ref, v_ref = refs[:4]
        go_ref, d_ref, mo_ref, vo_ref = refs[-6:-2] if exchange else refs[4:8]
        if exchange:
            riders = (refs[4], refs[-7], refs[-2], refs[-1])
            first = (pl.program_id(0) == 0) & (pl.program_id(1) == 0)

            @pl.when(first)
            def _():
                _exchange8_start(*riders)

        g = g_ref[...].T if g_transposed else g_ref[...]
        d, m2, v2 = _adamw_math(w_ref[...], g, m_ref[...], v_ref[...])
        go_ref[...] = g
        d_ref[...] = d
        mo_ref[...] = m2
        vo_ref[...] = v2
        if exchange:
            @pl.when((pl.program_id(0) == grid[0] - 1) & (pl.program_id(1) == grid[1] - 1))
            def _():
                _exchange8_finish(*riders)

    blk = pl.BlockSpec((bm, bn), lambda i, j: (i, j))
    g_blk = pl.BlockSpec((bn, bm), lambda i, j: (j, i)) if g_transposed else blk
    out = jax.ShapeDtypeStruct((r, c), F32)
    if not exchange:
        return pl.pallas_call(
            body, name=name, grid=grid, in_specs=[blk, g_blk, blk, blk],
            out_specs=[blk] * 4, out_shape=[out] * 4, compiler_params=_params(("parallel", "parallel")),
        )(w, g, m, v)
    parts, landing = exchange
    outs = pl.pallas_call(
        body, name=name, grid=grid, in_specs=[blk, g_blk, blk, blk, ANY, ANY],
        out_specs=[ANY] + [blk] * 4, out_shape=[jax.ShapeDtypeStruct(landing.shape, landing.dtype)] + [out] * 4,
        input_output_aliases={5: 0}, scratch_shapes=[pltpu.SemaphoreType.DMA((7,)), pltpu.SemaphoreType.DMA((7,))],
        compiler_params=_params(("arbitrary", "arbitrary")),
    )(w, g, m, v, parts, landing)
    return outs[1:], outs[0]


ANY = pl.BlockSpec(memory_space=pl.ANY)


def _place():
    x, y, c = lax.axis_index("x"), lax.axis_index("y"), lax.axis_index("c")
    chips = [(1 - x, y), (x, 1 - y), (1 - x, 1 - y)]
    return x, y, c, chips


def _sibling():
    return (lax.axis_index("x"), lax.axis_index("y"), 1 - lax.axis_index("c"))


def _wait_all(local, remote):
    for cp in local:
        cp.wait()
    for cp in remote:
        cp.wait_send()


def _place_shard(name, shard, place):
    r, cols = shard.shape
    tr = r // 4

    def body(s_ref, in_ref, out_ref):
        out_ref[...] = in_ref[...]

    return pl.pallas_call(
        body, name=name,
        grid_spec=pltpu.PrefetchScalarGridSpec(
            num_scalar_prefetch=1, grid=(4,), in_specs=[pl.BlockSpec((tr, cols), lambda i, s: (i, 0))],
            out_specs=pl.BlockSpec((tr, cols), lambda i, s: (4 * s[1] + i, 0))),
        out_shape=jax.ShapeDtypeStruct((N_SHARD * r, cols), shard.dtype), compiler_params=_params(("arbitrary",)),
    )(place, shard)


def _gather_copies(ins, outs, send_sems, recv_sems, only_first=False):
    x, y, c, chips = _place()
    me, sibling = (x, y, c), _sibling()

    def rows(k, px, py, pc):
        h = ins[k].shape[0] // 2
        return outs[k].at[pl.ds((2 * px + py) * 2 * h + pc * h, h), :]

    def copy(k, j, block, to, src=None):
        return pltpu.make_async_remote_copy(
            src_ref=rows(k, *block) if src is None else src, dst_ref=rows(k, *block),
            send_sem=send_sems.at[k, j], recv_sem=recv_sems.at[k, j], device_id=to, device_id_type=MESH)

    each = [(k, j, chip) for k in range(len(ins)) for j, chip in enumerate(chips)]
    half = lambda k: ins[k].at[pl.ds(c * (ins[k].shape[0] // 2), ins[k].shape[0] // 2), :]
    first = [copy(k, j, me, (*chip, c), src=half(k)) for k, j, chip in each]
    if only_first:
        return first
    arrive = [copy(k, j, (*chip, c), me) for k, j, chip in each]
    passed = [copy(k, 3 + j, (*chip, c), sibling) for k, j, chip in each]
    landed = [copy(k, 3 + j, (*chip, 1 - c), me) for k, j, chip in each]
    return first, arrive, passed, landed


def _gather_start(ins, outs, send_sems, recv_sems):
    for cp in _gather_copies(ins, outs, send_sems, recv_sems, only_first=True):
        cp.start()


def _gather_finish(ins, outs, send_sems, recv_sems):
    first, arrive, passed, landed = _gather_copies(ins, outs, send_sems, recv_sems)
    for arrival, forward in zip(arrive, passed):
        arrival.wait_recv()
        forward.start()
    for cp in landed:
        cp.wait_recv()
    _wait_all([], first + passed)


def _gather_shards(shards, placed):
    n = len(shards)

    def body(*refs):
        ins, outs = refs[:n], refs[2 * n:3 * n]
        _gather_start(ins, outs, *refs[3 * n:])
        _gather_finish(ins, outs, *refs[3 * n:])

    return pl.pallas_call(
        body, name="gather_shards", in_specs=[ANY] * (2 * n), out_specs=[ANY] * n,
        out_shape=[jax.ShapeDtypeStruct(a.shape, a.dtype) for a in placed],
        input_output_aliases={n + k: k for k in range(n)},
        scratch_shapes=[pltpu.SemaphoreType.DMA((n, 6)), pltpu.SemaphoreType.DMA((n, 6))],
    )(*shards, *placed)


def _exchange8_copies(in_ref, out_ref, send_sems, recv_sems, only_sends=False):
    x, y, c, _ = _place()
    sends, arrivals = [], []
    for rel in range(1, 8):
        dx, dy, dc = rel >> 2 & 1, rel >> 1 & 1, rel & 1
        sends.append(pltpu.make_async_remote_copy(
            src_ref=in_ref.at[2 * (x ^ dx) + (y ^ dy)], dst_ref=out_ref.at[4 * x + 2 * y + c],
            send_sem=send_sems.at[rel - 1], recv_sem=recv_sems.at[rel - 1],
            device_id=(x ^ dx, y ^ dy, c ^ dc), device_id_type=MESH))
        if not only_sends:
            arrivals.append(pltpu.make_async_remote_copy(
                src_ref=in_ref.at[0], dst_ref=out_ref.at[4 * (x ^ dx) + 2 * (y ^ dy) + (c ^ dc)],
                send_sem=send_sems.at[rel - 1], recv_sem=recv_sems.at[rel - 1],
                device_id=(x, y, c), device_id_type=MESH))
    return sends, arrivals


def _exchange8_start(in_ref, out_ref, send_sems, recv_sems):
    for cp in _exchange8_copies(in_ref, out_ref, send_sems, recv_sems, only_sends=True)[0]:
        cp.start()


def _exchange8_finish(in_ref, out_ref, send_sems, recv_sems):
    sends, arrivals = _exchange8_copies(in_ref, out_ref, send_sems, recv_sems)
    for cp in arrivals:
        cp.wait_recv()
    _wait_all([], sends)


def _rs_sibling_copies(ins, outs, send_sems, recv_sems):
    c = lax.axis_index("c")
    return [pltpu.make_async_remote_copy(
        src_ref=ins[k].at[2 * s + 1 - c], dst_ref=outs[k].at[s], send_sem=send_sems.at[k, s], recv_sem=recv_sems.at[k, s],
        device_id=_sibling(), device_id_type=MESH) for k in range(len(ins)) for s in range(N_SHARD)]


def _rs_sibling_start(ins, outs, send_sems, recv_sems):
    for cp in _rs_sibling_copies(ins, outs, send_sems, recv_sems):
        cp.start()


def _rs_sibling_finish(ins, outs, send_sems, recv_sems):
    sends = _rs_sibling_copies(ins, outs, send_sems, recv_sems)
    for cp in sends:
        cp.wait_recv()
    _wait_all([], sends)


def _rs_sibling(grads):
    n = len(grads)

    def body(*refs):
        _rs_sibling_start(refs[:n], refs[n:2 * n], *refs[2 * n:])
        _rs_sibling_finish(refs[:n], refs[n:2 * n], *refs[2 * n:])

    return pl.pallas_call(
        body, name="rs_sibling", in_specs=[ANY] * n, out_specs=[ANY] * n,
        out_shape=[jax.ShapeDtypeStruct((N_SHARD,) + a.shape[1:], a.dtype) for a in grads],
        scratch_shapes=[pltpu.SemaphoreType.DMA((n, N_SHARD)), pltpu.SemaphoreType.DMA((n, N_SHARD))],
    )(*grads)


def _rs_chips_copies(ins, outs, send_sems, recv_sems):
    x, y, c, chips = _place()
    return [pltpu.make_async_remote_copy(
        src_ref=ins[k].at[2 * px + py], dst_ref=outs[k].at[j], send_sem=send_sems.at[k, j], recv_sem=recv_sems.at[k, j],
        device_id=(px, py, c), device_id_type=MESH) for k in range(len(ins)) for j, (px, py) in enumerate(chips)]


def _rs_chips_start(ins, outs, send_sems, recv_sems):
    for cp in _rs_chips_copies(ins, outs, send_sems, recv_sems):
        cp.start()


def _rs_chips_finish(ins, outs, send_sems, recv_sems):
    sends = _rs_chips_copies(ins, outs, send_sems, recv_sems)
    for cp in sends:
        cp.wait_recv()
    _wait_all([], sends)


def _rs_finish(bufs):
    n = len(bufs)

    def body(*refs):
        outs = refs[n:2 * n]
        send_sems, recv_sems = refs[2 * n:]
        c = lax.axis_index("c")
        sends = []
        for k in range(n):
            cp = pltpu.make_async_remote_copy(
                src_ref=outs[k].at[c], dst_ref=outs[k].at[c], send_sem=send_sems.at[k], recv_sem=recv_sems.at[k],
                device_id=_sibling(), device_id_type=MESH)
            cp.start()
            sends.append(cp)
        for k in range(n):
            pltpu.make_async_remote_copy(
                src_ref=outs[k].at[c], dst_ref=outs[k].at[1 - c], send_sem=send_sems.at[k], recv_sem=recv_sems.at[k],
                device_id=_sibling(), device_id_type=MESH).wait_recv()
        _wait_all([], sends)

    return pl.pallas_call(
        body, name="rs_finish", in_specs=[ANY] * n, out_specs=[ANY] * n,
        out_shape=[jax.ShapeDtypeStruct(a.shape, a.dtype) for a in bufs], input_output_aliases={k: k for k in range(n)},
        scratch_shapes=[pltpu.SemaphoreType.DMA((n,)), pltpu.SemaphoreType.DMA((n,))],
    )(*bufs)


def _sum3d(name, terms, scalars, grid_lead, out_lead, out_index, tr=None, out_dtype=F32):
    h, c = terms[0][0].shape[1:]
    tr = tr or h

    def body(s_ref, *refs):
        acc = refs[0][...].astype(F32)
        for ref in refs[1:-1]:
            acc = acc + ref[...].astype(F32)
        refs[-1][...] = acc.astype(refs[-1].dtype)

    in_specs = [pl.BlockSpec((1, tr, c), functools.partial(lambda l, i, s_ref, f: (f(l, s_ref), i, 0), f=f)) for _, f in terms]
    out_spec = pl.BlockSpec((1, tr, c), lambda l, i, s_ref: (out_index(l, s_ref), i, 0))
    return pl.pallas_call(
        body, name=name,
        grid_spec=pltpu.PrefetchScalarGridSpec(num_scalar_prefetch=1, grid=(grid_lead, h // tr), in_specs=in_specs, out_specs=out_spec),
        out_shape=jax.ShapeDtypeStruct((out_lead, h, c), out_dtype), compiler_params=_params(("arbitrary", "arbitrary")),
    )(scalars, *[a for a, _ in terms])


def _halves(grads):
    return [a.reshape(2 * N_SHARD, a.shape[0] // (2 * N_SHARD), a.shape[1]) for a in grads]


def _rs_stage1(g8, place, tag, from_sibling=None):
    from_sibling = _rs_sibling(g8) if from_sibling is None else from_sibling
    parts = [_sum3d(f"rs_add1_{tag}{k}", [(g8[k], lambda l, s: 2 * l + s[0]), (from_sibling[k], lambda l, s: l)], place,
                    N_SHARD, N_SHARD, lambda l, s: l, tr=g8[k].shape[1] // 2, out_dtype=BF16)
             for k in range(len(g8))]
    return g8, from_sibling, parts


def _rs_stage3(stage1, from_chips, place, tag):
    g8, from_sibling, _ = stage1
    mine = [(lambda l, s: 2 * s[1] + s[0]), (lambda l, s: s[1])]
    bufs = [_sum3d(f"rs_add2_{tag}{k}", [(g8[k], mine[0]), (from_sibling[k], mine[1])]
                   + [(from_chips[k], functools.partial(lambda l, s, j: j, j=j)) for j in range(3)],
                   place, 1, 2, lambda l, s: s[0], tr=g8[k].shape[1] // 2)
            for k in range(len(g8))]
    whole = _rs_finish(bufs)
    return [w.reshape(2 * w.shape[1], w.shape[2]) for w in whole]


_LATE = ["w_a", "w_b", "w_out", "w_down", "w_up_t"]


def _device_step(x, target, p, late_shards, late_placed, place):
    t = x.shape[0]
    d = D_MODEL
    tok = lambda arr, c=0, w=d: (arr, w, c)
    f32x = lambda n: [(d, F32)] * n
    rp_params = [p["w0"], p["w2p"], p["a0"], p["a2p"], p["g2p"], p["k_k"], p["k_a"]]
    post_params = [p["ln_w"], p["ln_b"], p["r_k"]]
    g = {}

    (xn,) = _tok_fwd("norm1_fwd", _fn_norm, [tok(x)], [p["g1"]], [(d, BF16)], tm=LIGHT_TM)
    z = _mm("in_proj", xn, p["w_in_t"], "nt", tm=t, tn=256)
    q_in, k_in, kd, qe, dec = _tok_fwd("hgates_fwd", _fn_hgates, [tok(z, 0), tok(z, 1)], [p["lb2"]], f32x(5))
    o_intra, u = _hgrn_local_fwd(q_in, k_in, kd, z)
    o_raw, h_states = _hgrn_state_fwd(o_intra, qe, dec, u)
    zl = _lerp_fwd(z, p["mu"])
    lora = tok(zl, 3 * d // LORA, LORA)
    decay, kr2, avec, bvec, gate = _tok_fwd("rprep_fwd", _fn_rprep, [tok(zl, 1), lora], rp_params, f32x(5))
    y, r_states, late = _rwkv_fwd(zl, decay, kr2, avec, bvec, late_shards, late_placed)
    p = dict(p, **dict(zip(_LATE, late)))
    (o_a,) = _tok_fwd("hpost_fwd", _fn_hpost, [tok(o_raw), tok(z, 3)], [p["gnorm"]], [(d, BF16)])
    post_toks = [tok(y), tok(zl, 0), tok(kr2), tok(zl, 2), tok(gate)]
    (o_b,) = _tok_fwd("rpost_fwd", _fn_rpost, post_toks, post_params, [(d, BF16)])
    y_a = _mm("branch_a", o_a, p["w_a"], "nn")
    y_b = _mm("branch_b", o_b, p["w_b"], "nn")
    merge_toks = [tok(z, C_G // 256, 256), tok(z, (C_G + d) // 256, 256), tok(y_a, 0, 256), tok(y_b, 0, 256)]
    (merged,) = _tok_fwd("merge_fwd", _fn_merge, merge_toks, [], [(256, BF16)], col_grid=4, tm=512)
    mix = _mm("out_proj", merged, p["w_out"], "nn")
    h1, xn2 = _tok_fwd("res1_fwd", _fn_res1, [tok(x), tok(mix)], [p["g_post1"], p["g_pre2"]], [(d, F32), (d, BF16)],
                       tm=LIGHT_TM)
    hu = _mm("up_proj", xn2, p["w_up_t"], "nt", tm=t, tn=512)
    act = _conv_fwd(hu, p["conv_w"], p["conv_b"])
    ff = _mm("down_proj", act, p["w_down"], "nn")
    d_h1, d_ff, loss, g["g_post2"] = _loss_head(h1, ff, target, p["g_post2"])

    d_act = _mm("d_act", d_ff, p["w_down"], "nt")
    g["w_down"] = _mm("dw_down", act, d_ff, "tn", tm=256, tn=1024)
    d_hu, d_cw, d_cb = _conv_bwd(hu, p["conv_w"], p["conv_b"], d_act)
    g["conv_w"], g["conv_b"] = d_cw.transpose(1, 0, 2).reshape(3, 2 * D_FF), d_cb.reshape(1, 2 * D_FF)
    d_hu = d_hu.reshape(2 * t, D_FF)
    d_xn2 = _mm("d_xn2", d_hu, p["w_up_t"], "nn", tm=t, tk=D_FF, mk=(t, 2 * D_FF), a_map=lambda i, j, q: (q, 0))
    g["w_up_t"] = _mm("dw_up", d_hu, xn2, "tn", tm=CONV_TILE, tn=1024, mk=(2 * D_FF, t),
                      a_map=lambda i, j, q: (i // N_CONV_TILES, i % N_CONV_TILES))
    d_x_res, d_mix, g["g_post1"], g["g_pre2"] = _tok_bwd(
        "res1_bwd", _fn_res1, [tok(x), tok(mix)], [p["g_post1"], p["g_pre2"]], [[tok(d_h1)], [tok(d_xn2)]],
        [(d, F32), (d, BF16)], tm=LIGHT_TM)
    d_merged = _mm("d_merged", d_mix, p["w_out"], "nt")
    g["w_out"] = _mm("dw_out", merged, d_mix, "tn")
    d_ga, d_gb, d_ya, d_yb = _tok_bwd("merge_bwd", _fn_merge, merge_toks, [], [[tok(d_merged, 0, 256)]],
                                      [(256, BF16)] * 4, col_grid=4, tm=512)
    d_oa = _mm("d_oa", d_ya, p["w_a"], "nt")
    g["w_a"] = _mm("dw_a", o_a, d_ya, "tn")
    d_ob = _mm("d_ob", d_yb, p["w_b"], "nt")
    g["w_b"] = _mm("dw_b", o_b, d_yb, "tn")
    d_oraw, d_hg, g["gnorm"] = _tok_bwd("hpost_bwd", _fn_hpost, [tok(o_raw), tok(z, 3)], [p["gnorm"]], [[tok(d_oa)]],
                                        [(d, F32), (d, BF16)])
    late_g8 = _halves([g[n] for n in _LATE])
    rpost = _tok_bwd("rpost_bwd", _fn_rpost, post_toks, post_params, [[tok(d_ob)]], f32x(5), sibling_rider=late_g8)
    d_y, d_r1, d_kr2_1, d_v1, d_gate, g["ln_w"], g["ln_b"], g["r_k"] = rpost[:8]
    stage1 = _rs_stage1(late_g8, place, "late", from_sibling=rpost[8:])
    (d_r2, d_decay, d_kr2_2, d_v2, d_avec, d_bvec), from_chips = _rwkv_bwd(
        zl, decay, kr2, avec, bvec, r_states, d_y, stage1[2])
    g.update(zip(_LATE, _rs_stage3(stage1, from_chips, place, "late")))
    prep = _tok_bwd("rprep_bwd", _fn_rprep, [tok(zl, 1), lora], rp_params,
                    [[tok(d_decay)], [tok(d_kr2_1), tok(d_kr2_2)], [tok(d_avec)], [tok(d_bvec)], [tok(d_gate)]],
                    [(d, F32), (LORA, F32)])
    d_kr, d_lora = prep[:2]
    g["w0"], g["w2p"], g["a0"], g["a2p"], g["g2p"], g["k_k"], g["k_a"] = prep[2:]
    dz_r, g["mu"] = _lerp_bwd(z, p["mu"], (d_r1, d_r2), d_kr, (d_v1, d_v2), d_lora)
    d_qe, d_dec, d_u = _hgrn_state_bwd(d_oraw, qe, dec, h_states)
    d_q_in, d_k_in, d_kd, d_vi = _hgrn_local_bwd(q_in, k_in, kd, z, d_oraw, d_u)
    d_hq, d_hf, g["lb2"] = _tok_bwd("hgates_bwd", _fn_hgates, [tok(z, 0), tok(z, 1)], [p["lb2"]],
                                    [[tok(d_q_in)], [tok(d_k_in)], [tok(d_kd)], [tok(d_qe)], [tok(d_dec)]], [(d, BF16)] * 2)
    dz = jnp.concatenate([d_hq, d_hf, d_vi.astype(BF16), d_hg, dz_r, d_ga, d_gb], axis=1)
    stage1 = _rs_stage1(_halves([_mm("dw_in", dz, xn, "tn", tm=256, tn=1024)]), place, "w_in")
    d_xn, from_chips = _mm("d_xn", dz, p["w_in_t"], "nn", tm=1024, tn=512, tk=IN_COLS // 2, riders=stage1[2])
    (g["w_in_t"],) = _rs_stage3(stage1, from_chips, place, "w_in")
    grad_x, g["g1"] = _tok_bwd("norm1_bwd", _fn_norm, [tok(x)], [p["g1"]], [[tok(d_xn)]], [(d, F32)],
                               add_to_first=tok(d_x_res), tm=LIGHT_TM)
    return loss, grad_x, g


_WEIGHTS = ["attn_pre_norm", "w_in", "hgrn_lb", "hgrn_gnorm", "w_branch_a", "rwkv_mu", "rwkv_w0", "rwkv_w2", "rwkv_a0",
            "rwkv_a2", "rwkv_g2", "rwkv_k_k", "rwkv_k_a", "rwkv_r_k", "rwkv_ln_w", "rwkv_ln_b", "w_branch_b", "w_out",
            "attn_post_norm", "ffn_pre_norm", "w_up", "conv_w", "conv_b", "w_down", "ffn_post_norm"]
_REPLICATED = [("attn_pre_norm", "g1"), ("hgrn_lb", "lb2"), ("hgrn_gnorm", "gnorm"), ("rwkv_mu", "mu"), ("rwkv_w0", "w0"),
               ("rwkv_a0", "a0"), ("rwkv_k_k", "k_k"), ("rwkv_k_a", "k_a"), ("rwkv_r_k", "r_k"), ("rwkv_ln_w", "ln_w"),
               ("rwkv_ln_b", "ln_b"), ("attn_post_norm", "g_post1"), ("ffn_pre_norm", "g_pre2"), ("conv_b", "conv_b"),
               ("ffn_post_norm", "g_post2")]
SLAB_COLS = 1024


def _pack(arrays):
    pieces, total = [], 0
    for a in arrays:
        flat = a.reshape(-1)
        rows = -(-flat.shape[0] // SLAB_COLS)
        pieces.append(jnp.pad(flat, (0, rows * SLAB_COLS - flat.shape[0])).reshape(rows, SLAB_COLS))
        total += rows
    if total % 8:
        pieces.append(jnp.zeros((8 - total % 8, SLAB_COLS), F32))
    return jnp.concatenate(pieces, axis=0)


def _unpack(slab, shapes):
    out, at = [], 0
    for s in shapes:
        size = 1
        for dim in s:
            size *= dim
        rows = -(-size // SLAB_COLS)
        out.append(slab[at:at + rows].reshape(-1)[:size].reshape(s))
        at += rows
    return out


def kernel(x, attn_pre_norm, w_in, hgrn_lb, hgrn_gnorm, w_branch_a, rwkv_mu, rwkv_w0, rwkv_w2, rwkv_a0, rwkv_a2, rwkv_g2, rwkv_k_k, rwkv_k_a, rwkv_r_k, rwkv_ln_w, rwkv_ln_b, w_branch_b, w_out, attn_post_norm, ffn_pre_norm, w_up, conv_w, conv_b, w_down, ffn_post_norm, loss_target, m_attn_pre_norm, m_w_in, m_hgrn_lb, m_hgrn_gnorm, m_w_branch_a, m_rwkv_mu, m_rwkv_w0, m_rwkv_w2, m_rwkv_a0, m_rwkv_a2, m_rwkv_g2, m_rwkv_k_k, m_rwkv_k_a, m_rwkv_r_k, m_rwkv_ln_w, m_rwkv_ln_b, m_w_branch_b, m_w_out, m_attn_post_norm, m_ffn_pre_norm, m_w_up, m_conv_w, m_conv_b, m_w_down, m_ffn_post_norm, v_attn_pre_norm, v_w_in, v_hgrn_lb, v_hgrn_gnorm, v_w_branch_a, v_rwkv_mu, v_rwkv_w0, v_rwkv_w2, v_rwkv_a0, v_rwkv_a2, v_rwkv_g2, v_rwkv_k_k, v_rwkv_k_a, v_rwkv_r_k, v_rwkv_ln_w, v_rwkv_ln_b, v_w_branch_b, v_w_out, v_attn_post_norm, v_ffn_pre_norm, v_w_up, v_conv_w, v_conv_b, v_w_down, v_ffn_post_norm):
    given = dict(locals())
    w = {n: given[n] for n in _WEIGHTS}
    mom = {n: given["m_" + n] for n in _WEIGHTS}
    var = {n: given["v_" + n] for n in _WEIGHTS}
    shard = 2 * lax.axis_index("x") + lax.axis_index("y")
    place = jnp.stack([lax.axis_index("c"), shard]).astype(jnp.int32)
    row = lambda a: a.reshape(1, -1)
    lora_of = lambda d: jnp.concatenate([d["rwkv_w2"][0], d["rwkv_a2"][0], d["rwkv_g2"][0]], axis=0)

    shards = [w["w_in"][0].T.astype(BF16), lora_of(w), jnp.pad(w["conv_w"][0], ((0, 29), (0, 0)))]
    late_shards = [w["w_branch_a"][0].astype(BF16), w["w_branch_b"][0].astype(BF16), w["w_out"][0].astype(BF16),
                   w["w_down"][0].astype(BF16), w["w_up"][0].T.astype(BF16)]
    placed = [_place_shard(f"place_{k}", a, place) for k, a in enumerate(shards)]
    late_placed = [_place_shard(f"place_late_{k}", a, place) for k, a in enumerate(late_shards)]
    w_in_t, lora_g, conv_g = _gather_shards(shards, placed)
    lora_full = lora_g.reshape(N_SHARD, LORA, 256).transpose(1, 0, 2).reshape(LORA, D_MODEL)
    conv_full = conv_g.reshape(N_SHARD, 32, 2 * D_FF // N_SHARD)[:, :3].transpose(1, 0, 2).reshape(3, 2 * D_FF)
    lrow = lax.broadcasted_iota(jnp.int32, (LORA, 1), 0)
    p = {
        "g1": row(w["attn_pre_norm"]), "lb2": w["hgrn_lb"], "gnorm": row(w["hgrn_gnorm"]), "w_in_t": w_in_t,
        "mu": row(w["rwkv_mu"]), "w0": row(w["rwkv_w0"]), "a0": row(w["rwkv_a0"]),
        "w2p": jnp.where(lrow < 64, lora_full, 0.0), "a2p": jnp.where((lrow >= 64) & (lrow < 128), lora_full, 0.0),
        "g2p": jnp.where(lrow >= 128, lora_full, 0.0),
        "k_k": row(w["rwkv_k_k"]), "k_a": row(w["rwkv_k_a"]), "r_k": row(w["rwkv_r_k"]), "ln_w": row(w["rwkv_ln_w"]),
        "ln_b": row(w["rwkv_ln_b"]), "g_post1": row(w["attn_post_norm"]),
        "g_pre2": row(w["ffn_pre_norm"]), "conv_w": conv_full, "conv_b": row(w["conv_b"]),
        "g_post2": row(w["ffn_post_norm"]),
    }

    loss, grad_x, g = _device_step(x[0], loss_target[0], p, late_shards, late_placed, place)

    g_in_t = g["w_in_t"]
    g_a, g_b, g_o, g_dn, g_up_t = [g[n] for n in _LATE]
    rep_shapes = [w[n].shape for n, _ in _REPLICATED]
    rep = _pack([g[key] for _, key in _REPLICATED])
    n_rep_rows = rep.shape[0]
    cw = 2 * D_FF // N_SHARD
    lora_rows, conv_rows = LORA * 256 // SLAB_COLS, -(-3 * cw // SLAB_COLS)
    lora_g = jnp.concatenate([g["w2p"][0:64], g["a2p"][64:128], g["g2p"][128:256]], axis=0)
    lora_parts = lora_g.reshape(LORA, N_SHARD, 256).transpose(1, 0, 2).reshape(N_SHARD, lora_rows, SLAB_COLS)
    conv_parts = g["conv_w"].reshape(3, N_SHARD, cw).transpose(1, 0, 2).reshape(N_SHARD, 3 * cw)
    conv_parts = jnp.pad(conv_parts, ((0, 0), (0, conv_rows * SLAB_COLS - 3 * cw))).reshape(N_SHARD, conv_rows, SLAB_COLS)
    n_rows = n_rep_rows + lora_rows + conv_rows
    fill = jnp.zeros((N_SHARD, -n_rows % 8, SLAB_COLS), F32)
    parts = jnp.concatenate([jnp.broadcast_to(rep, (N_SHARD,) + rep.shape), lora_parts, conv_parts, fill], axis=1)
    me = 4 * lax.axis_index("x") + 2 * lax.axis_index("y") + lax.axis_index("c")
    landing = lax.dynamic_update_slice(jnp.zeros((8,) + parts.shape[1:], F32),
                                       lax.dynamic_index_in_dim(parts, shard, 0, keepdims=True), (me, 0, 0))

    res = {}

    def put(name, outs, shape=None):
        res[name] = [o.reshape(w[name].shape if shape is None else shape) for o in outs]

    w_in_out, gathered = _adamw("adamw_w_in", w["w_in"][0], g_in_t, mom["w_in"][0], var["w_in"][0], 1024, 128,
                                g_transposed=True, exchange=(parts, landing))
    put("w_in", w_in_out)
    summed = _sum3d("small_sum", [(gathered, functools.partial(lambda l, s, i: i, i=i)) for i in range(8)], place, 1, 1,
                    lambda l, s: 0)[0]
    lora_grad = summed[n_rep_rows:n_rep_rows + lora_rows].reshape(LORA, 256)
    conv_grad = summed[n_rep_rows + lora_rows:n_rows].reshape(-1)[:3 * cw].reshape(3, cw)
    put("w_up", _adamw("adamw_w_up", w["w_up"][0], g_up_t, mom["w_up"][0], var["w_up"][0], 1024, 128, g_transposed=True))
    for name, grad in (("w_branch_a", g_a), ("w_branch_b", g_b), ("w_out", g_o)):
        put(name, _adamw("adamw_" + name, w[name][0], grad, mom[name][0], var[name][0], 256, 1024))
    put("w_down", _adamw("adamw_w_down", w["w_down"][0], g_dn, mom["w_down"][0], var["w_down"][0], 176, 1024))
    put("conv_w", _adamw("adamw_conv_w", w["conv_w"][0], conv_grad, mom["conv_w"][0], var["conv_w"][0], 3, 2 * D_FF // N_SHARD))
    lora_out = _adamw("adamw_lora", lora_of(w), lora_grad, lora_of(mom), lora_of(var), LORA, 256)
    for name, lo, hi in (("rwkv_w2", 0, 64), ("rwkv_a2", 64, 128), ("rwkv_g2", 128, 256)):
        put(name, [o[lo:hi] for o in lora_out])
    rep_names = [n for n, _ in _REPLICATED]
    rep_out = _adamw("adamw_small", _pack([w[n] for n in rep_names]), summed[:n_rep_rows], _pack([mom[n] for n in rep_names]),
                     _pack([var[n] for n in rep_names]), n_rep_rows, SLAB_COLS)
    for name, parts in zip(rep_names, zip(*[_unpack(o, rep_shapes) for o in rep_out])):
        put(name, list(parts))

    loss = lax.psum(loss[0, 0], ("x", "y", "c"))
    return (loss, grad_x[None], *[res[n][0] for n in _WEIGHTS], *[res[n][1] for n in _WEIGHTS],
            *[res[n][2] for n in _WEIGHTS], *[res[n][3] for n in _WEIGHTS])
```

```python
import functools

import jax
import jax.numpy as jnp
from jax import lax
from jax.experimental import pallas as pl
from jax.experimental.pallas import tpu as pltpu

F32, BF16 = jnp.float32, jnp.bfloat16
MESH = pl.DeviceIdType.MESH

D_MODEL = 1024
HGRN_HEADS = 8
HGRN_K = 128
HGRN_SCALE = HGRN_K ** -0.5
CHUNK = 32
RWKV_HEAD = 64
LORA = 256
D_FF = 2816
EPS = 1e-6
GN_EPS = 1e-5 * RWKV_HEAD
N_SHARD = 4
ADAM_LR, ADAM_B1, ADAM_B2, ADAM_EPS, ADAM_WD, ADAM_STEP = 0.001, 0.9, 0.999, 1e-08, 0.01, 10

LANES = 128
VMEM_LIMIT = 56 * 1024 * 1024
SCAN_TB = 32
SCAN_TB_FWD = 64
SCAN_GROUP = 256
LIGHT_TM = 256

C_HQ, C_HF, C_HI, C_HG = 0, 1024, 2048, 3072
C_R = 4096
R_COLS = 3328
C_G = 7424
IN_COLS = 9472


def _params(sem=None, **kw):
    return pltpu.CompilerParams(dimension_semantics=sem, vmem_limit_bytes=VMEM_LIMIT, **kw)


def _seg_matrix(n, seg):
    r = lax.broadcasted_iota(jnp.int32, (n, n), 0) // seg
    c = lax.broadcasted_iota(jnp.int32, (n, n), 1) // seg
    return (r == c).astype(BF16)


def _split3(x):
    hi = x.astype(BF16)
    r1 = x - hi.astype(F32)
    mid = r1.astype(BF16)
    lo = (r1 - mid.astype(F32)).astype(BF16)
    return hi, mid, lo


def _segsum_impl(x, seg):
    e = _seg_matrix(LANES, seg)
    outs = []
    for g in range(x.shape[1] // LANES):
        hi, mid, lo = _split3(x[:, g * LANES:(g + 1) * LANES])
        outs.append(jnp.dot(hi, e, preferred_element_type=F32) + jnp.dot(mid, e, preferred_element_type=F32)
                    + jnp.dot(lo, e, preferred_element_type=F32))
    return outs[0] if len(outs) == 1 else jnp.concatenate(outs, axis=1)


def _make_segsum(seg):
    @jax.custom_vjp
    def f(x):
        return _segsum_impl(x, seg)

    f.defvjp(lambda x: (_segsum_impl(x, seg), None), lambda _, ct: (_segsum_impl(ct, seg),))
    return f


_segsum64 = _make_segsum(RWKV_HEAD)
_segsum128 = _make_segsum(HGRN_K)


def _chunk_mm_impl(x, kind, transposed):
    n = x.shape[0]
    r = lax.broadcasted_iota(jnp.int32, (n, n), 1 if transposed else 0)
    c = lax.broadcasted_iota(jnp.int32, (n, n), 0 if transposed else 1)
    same = (r // CHUNK) == (c // CHUNK)
    if kind == "cumsum":
        m = same & (r >= c)
    else:
        m = same & (c % CHUNK == (CHUNK // 2 - 1 if kind == "mid" else CHUNK - 1))
    m = m.astype(BF16)
    hi, mid, lo = _split3(x)
    return (jnp.dot(m, hi, preferred_element_type=F32) + jnp.dot(m, mid, preferred_element_type=F32)
            + jnp.dot(m, lo, preferred_element_type=F32))


def _make_chunk_mm(kind):
    @jax.custom_vjp
    def f(x):
        return _chunk_mm_impl(x, kind, False)

    f.defvjp(lambda x: (_chunk_mm_impl(x, kind, False), None), lambda _, ct: (_chunk_mm_impl(ct, kind, True),))
    return f


_chunk_cumsum = _make_chunk_mm("cumsum")
_chunk_mid = _make_chunk_mm("mid")
_chunk_last = _make_chunk_mm("last")


@jax.custom_vjp
def _bdot(x, w):
    return jnp.dot(x.astype(BF16), w.astype(BF16), preferred_element_type=F32)


def _bdot_fwd(x, w):
    return _bdot(x, w), (x, w)


def _bdot_bwd(res, ct):
    x, w = res
    ctb = ct.astype(BF16)
    dx = lax.dot_general(ctb, w.astype(BF16), (((1,), (1,)), ((), ())), preferred_element_type=F32)
    dw = lax.dot_general(x.astype(BF16), ctb, (((0,), (0,)), ((), ())), preferred_element_type=F32)
    return dx, dw


_bdot.defvjp(_bdot_fwd, _bdot_bwd)


def _sigmoid(x):
    return 1.0 / (1.0 + jnp.exp(-x))


def _silu(x):
    return x * _sigmoid(x)


def _softplus(x):
    return jnp.maximum(x, 0.0) + jnp.log(1.0 + jnp.exp(-jnp.abs(x)))


def _rms(x, g):
    return x * lax.rsqrt(jnp.mean(x * x, axis=-1, keepdims=True) + EPS) * g


def _fn_norm(t, p):
    return [_rms(t[0], p[0])]


def _fn_hgates(t, p):
    hq, hf = t
    lb2 = p[0]
    m = jnp.max(lb2, axis=0, keepdims=True)
    e = jnp.exp(lb2 - m)
    first = lax.broadcasted_iota(jnp.int32, e.shape, 0) == 0
    lb = jnp.sum(jnp.where(first, e, 0.0), axis=0, keepdims=True) / jnp.sum(e, axis=0, keepdims=True)
    f = lb + (1.0 - lb) * _sigmoid(hf)
    q, k = _silu(hq) * HGRN_SCALE, 1.0 - f
    b = _chunk_cumsum(jnp.log(f))
    b_ref, b_last = _chunk_mid(b), _chunk_last(b)
    return [q * jnp.exp(b - b_ref), k * jnp.exp(b_ref - b), k * jnp.exp(b_last - b), q * jnp.exp(b), jnp.exp(b_last)]


def _fn_hpost(t, p):
    o, hg = t
    ms = _segsum128(o * o) * (1.0 / HGRN_K)
    return [o * lax.rsqrt(ms + EPS) * p[0] * _silu(hg)]


def _fn_rprep(t, p):
    kr, lora = t
    w0, w2p, a0, a2p, g2p, k_k, k_a = p
    pre_w = w0 + _bdot(jnp.tanh(lora), w2p)
    w_log = -_softplus(-pre_w) - 0.5
    decay = jnp.exp(-jnp.exp(w_log))
    a = _sigmoid(a0 + _bdot(lora, a2p))
    g = _bdot(_sigmoid(lora), g2p)
    kk = kr * k_k
    kk = kk / jnp.maximum(jnp.sqrt(_segsum64(kk * kk)), 1e-12)
    kr2 = kr * (1.0 + (a - 1.0) * k_a)
    return [decay, kr2, -kk, kk * a, g]


def _fn_rpost(t, p):
    y, r, kr2, v, g = t
    ln_w, ln_b, r_k = p
    mu = _segsum64(y) * (1.0 / RWKV_HEAD)
    yc = y - mu
    var = _segsum64(yc * yc) * (1.0 / RWKV_HEAD)
    yn = yc * lax.rsqrt(var + GN_EPS) * ln_w + ln_b
    bonus = _segsum64(r * kr2 * r_k) * v
    return [(yn + bonus) * g]


def _fn_merge(t, p):
    ga, gb, ya, yb = t
    return [_sigmoid(ga) * ya + _sigmoid(gb) * yb]


def _fn_res1(t, p):
    x, mix = t
    h1 = x + _rms(mix, p[0])
    return [h1, _rms(h1, p[1])]


def _tok_call(name, fn, toks, params, outs, red_shapes=(), tm=128, col_grid=1, sibling_rider=None):
    n_t, n_p, n_o, n_red = len(toks), len(params), len(outs), len(red_shapes)
    n_r = 0 if sibling_rider is None else len(sibling_rider)
    t_len = toks[0][0].shape[0]
    tm = min(tm, t_len)
    grid = (t_len // tm, col_grid)

    def body(*refs):
        first_out = n_t + n_p + n_r
        if n_r:
            rider = (refs[n_t + n_p:first_out], refs[first_out + n_o + n_red:first_out + n_o + n_red + n_r], *refs[-2:])

            @pl.when((pl.program_id(0) == 0) & (pl.program_id(1) == 0))
            def _():
                _rs_sibling_start(*rider)

        tv = [r[...].astype(F32) for r in refs[:n_t]]
        pv = [r[...] for r in refs[n_t:n_t + n_p]]
        o, red = fn(tv, pv)
        for ref, val in zip(refs[first_out:first_out + n_o], o):
            ref[...] = val.astype(ref.dtype)
        if n_r:
            @pl.when((pl.program_id(0) == grid[0] - 1) & (pl.program_id(1) == grid[1] - 1))
            def _():
                _rs_sibling_finish(*rider)

        red_refs = refs[first_out + n_o:first_out + n_o + n_red]
        if red_refs:
            first = pl.program_id(0) == 0

            @pl.when(first)
            def _():
                for ref, val in zip(red_refs, red):
                    ref[...] = val

            @pl.when(jnp.logical_not(first))
            def _():
                for ref, val in zip(red_refs, red):
                    ref[...] += val

    in_specs = [pl.BlockSpec((tm, w), functools.partial(lambda i, j, c: (i, c + j), c=c)) for (_, w, c) in toks]
    in_specs += [pl.BlockSpec(p.shape, lambda i, j: (0, 0)) for p in params]
    out_specs = [pl.BlockSpec((tm, w), lambda i, j: (i, j)) for (w, _) in outs]
    out_specs += [pl.BlockSpec(s, lambda i, j: (0, 0)) for s in red_shapes]
    out_shape = [jax.ShapeDtypeStruct((t_len, w * col_grid), dt) for (w, dt) in outs]
    out_shape += [jax.ShapeDtypeStruct(s, F32) for s in red_shapes]
    scratch = []
    if n_r:
        in_specs += [ANY] * n_r
        out_specs += [ANY] * n_r
        out_shape += [jax.ShapeDtypeStruct((N_SHARD,) + a.shape[1:], a.dtype) for a in sibling_rider]
        scratch = [pltpu.SemaphoreType.DMA((n_r, N_SHARD)), pltpu.SemaphoreType.DMA((n_r, N_SHARD))]
    return pl.pallas_call(
        body, name=name, grid=grid, in_specs=in_specs, out_specs=out_specs, out_shape=out_shape, scratch_shapes=scratch,
        compiler_params=_params(("arbitrary", "arbitrary")),
    )(*[a for (a, _, _) in toks], *params, *(sibling_rider or []))


def _tok_fwd(name, fn, toks, params, outs, **kw):
    return _tok_call(name, lambda tv, pv: (fn(tv, pv), []), toks, params, outs, **kw)


def _tok_bwd(name, fn, toks, params, cts, want, add_to_first=None, **kw):
    n_t = len(toks)
    flat = [c for group in cts for c in group]
    extra = [] if add_to_first is None else [add_to_first]

    def bwd(tv, pv):
        prim, rest = tv[:n_t], tv[n_t:]
        ct, at = [], 0
        for group in cts:
            ct.append(functools.reduce(lambda u, v: u + v, rest[at:at + len(group)]))
            at += len(group)
        _, vjp = jax.vjp(lambda *a: tuple(fn(list(a[:n_t]), list(a[n_t:]))), *prim, *pv)
        g = vjp(tuple(ct))
        tok_grads = [g[i] for i in range(n_t) if want[i] is not None]
        if extra:
            tok_grads[0] = tok_grads[0] + rest[at]
        return tok_grads, list(g[n_t:])

    return _tok_call(name, bwd, list(toks) + flat + extra, params, [w for w in want if w is not None],
                     red_shapes=[p.shape for p in params], **kw)


def _mm(name, a, b, mode, out_dtype=F32, tm=None, tn=None, tk=None, riders=None, a_map=None, mk=None):
    if mode == "nn":
        (m, k), (_, n) = a.shape, b.shape
    elif mode == "nt":
        (m, k), (n, _) = a.shape, b.shape
    else:
        (k, m), (_, n) = a.shape, b.shape
    if mk is not None:
        m, k = mk
    tm = (512 if mode == "tn" else 2048) if tm is None else tm
    tn = (512 if mode == "tn" else 256) if tn is None else tn
    tk = k if tk is None else tk
    tm, tn = min(tm, m), min(tn, n)
    nk = k // tk
    assert m % tm == 0 and n % tn == 0 and k % tk == 0, (name, a.shape, b.shape, tm, tn, tk)
    a_spec = pl.BlockSpec((tk, tm), lambda i, j, q: (q, i)) if mode == "tn" else pl.BlockSpec((tm, tk), lambda i, j, q: (i, q))
    if a_map is not None:
        a_spec = pl.BlockSpec(a_spec.block_shape, a_map)
    b_spec = pl.BlockSpec((tn, tk), lambda i, j, q: (j, q)) if mode == "nt" else pl.BlockSpec((tk, tn), lambda i, j, q: (q, j))
    dn = {"nn": (((1,), (0,)), ((), ())), "nt": (((1,), (1,)), ((), ())), "tn": (((0,), (0,)), ((), ()))}[mode]
    grid = (m // tm, n // tn, nk)
    nr = 0 if riders is None else len(riders)

    def body(*refs):
        a_ref, b_ref, o_ref = refs[0], refs[1], refs[2 + nr]
        acc = refs[3 + 2 * nr] if nk > 1 else None
        if nr:
            exchange = (refs[2:2 + nr], refs[3 + nr:3 + 2 * nr], *refs[-2:])
            at = [pl.program_id(ax) for ax in range(3)]

            @pl.when((at[0] == 0) & (at[1] == 0) & (at[2] == 0))
            def _():
                _rs_chips_start(*exchange)

        p = lax.dot_general(a_ref[...], b_ref[...], dn, preferred_element_type=F32)
        if nk == 1:
            o_ref[...] = p.astype(o_ref.dtype)
        else:
            q = pl.program_id(2)

            @pl.when(q == 0)
            def _():
                acc[...] = p

            @pl.when(q > 0)
            def _():
                acc[...] += p

            @pl.when(q == nk - 1)
            def _():
                o_ref[...] = acc[...].astype(o_ref.dtype)

        if nr:
            @pl.when((at[0] == grid[0] - 1) & (at[1] == grid[1] - 1) & (at[2] == grid[2] - 1))
            def _():
                _rs_chips_finish(*exchange)

    scratch = [pltpu.VMEM((tm, tn), F32)] if nk > 1 else []
    out_specs = [pl.BlockSpec((tm, tn), lambda i, j, q: (i, j))]
    out_shape = [jax.ShapeDtypeStruct((m, n), out_dtype)]
    if nr:
        scratch += [pltpu.SemaphoreType.DMA((nr, 3)), pltpu.SemaphoreType.DMA((nr, 3))]
        out_specs += [ANY] * nr
        out_shape += [jax.ShapeDtypeStruct((3,) + r.shape[1:], r.dtype) for r in riders]
    outs = pl.pallas_call(
        body, name=name, grid=grid, in_specs=[a_spec, b_spec] + [ANY] * nr, out_specs=out_specs, out_shape=out_shape,
        scratch_shapes=scratch,
        compiler_params=_params(("arbitrary",) * 3 if nr else ("parallel", "parallel", "arbitrary")),
    )(a, b, *(riders or []))
    return (outs[0], outs[1:]) if nr else outs[0]


def _shift_down(z, n):
    rows = lax.broadcasted_iota(jnp.int32, z.shape, 0)
    return jnp.where(rows < n, 0.0, pltpu.roll(z, n, 0))


def _shift_up(z, n):
    t = z.shape[0]
    rows = lax.broadcasted_iota(jnp.int32, z.shape, 0)
    return jnp.where(rows >= t - n, 0.0, pltpu.roll(z, t - n, 0))


def _lerp_fwd(z, mu):
    t = z.shape[0]
    w = 256

    def body(z_ref, mu_ref, o_ref):
        zz = z_ref[...]
        o_ref[...] = zz + mu_ref[...] * (_shift_down(zz, 1) - zz)

    return pl.pallas_call(
        body, name="lerp_fwd", grid=(R_COLS // w,),
        in_specs=[pl.BlockSpec((t, w), lambda j: (0, C_R // w + j)), pl.BlockSpec((1, w), lambda j: (0, j))],
        out_specs=pl.BlockSpec((t, w), lambda j: (0, j)), out_shape=jax.ShapeDtypeStruct((t, R_COLS), F32),
        compiler_params=_params(("parallel",)),
    )(z, mu)


def _lerp_bwd(z, mu, d_r, d_k, d_v, d_lora):
    t = z.shape[0]
    w = 256
    per = D_MODEL // w

    def body(z_ref, mu_ref, r1_ref, r2_ref, k_ref, v1_ref, v2_ref, l_ref, dz_ref, dmu_ref):
        j = pl.program_id(0)
        zz, m = z_ref[...], mu_ref[...]
        d = jnp.where(j < per, r1_ref[...] + r2_ref[...],
                      jnp.where(j < 2 * per, k_ref[...], jnp.where(j < 3 * per, v1_ref[...] + v2_ref[...], l_ref[...])))
        dz_ref[...] = (d * (1.0 - m) + _shift_up(d * m, 1)).astype(dz_ref.dtype)
        dmu_ref[...] = jnp.sum(d * (_shift_down(zz, 1) - zz), axis=0, keepdims=True)

    piece = lambda first: pl.BlockSpec((t, w), lambda j: (0, jnp.clip(j - first, 0, per - 1)))
    return pl.pallas_call(
        body, name="lerp_bwd", grid=(R_COLS // w,),
        in_specs=[pl.BlockSpec((t, w), lambda j: (0, C_R // w + j)), pl.BlockSpec((1, w), lambda j: (0, j)),
                  piece(0), piece(0), piece(per), piece(2 * per), piece(2 * per), pl.BlockSpec((t, w), lambda j: (0, 0))],
        out_specs=[pl.BlockSpec((t, w), lambda j: (0, j)), pl.BlockSpec((1, w), lambda j: (0, j))],
        out_shape=[jax.ShapeDtypeStruct((t, R_COLS), BF16), jax.ShapeDtypeStruct((1, R_COLS), F32)],
        compiler_params=_params(("arbitrary",)),
    )(z, mu, *d_r, d_k, *d_v, d_lora)


CONV_TILE = 256
N_CONV_TILES = D_FF // CONV_TILE


def _conv(h, w, b):
    return b + w[0:1, :] * _shift_down(h, 2) + w[1:2, :] * _shift_down(h, 1) + w[2:3, :] * h


def _conv_fwd(hu, conv_w, conv_b):
    t = hu.shape[0]
    n = N_CONV_TILES

    def body(hg_ref, hv_ref, wg_ref, wv_ref, bg_ref, bv_ref, o_ref):
        gate = _conv(hg_ref[...], wg_ref[...], bg_ref[...])
        val = _conv(hv_ref[...], wv_ref[...], bv_ref[...])
        o_ref[...] = (_silu(gate) * val).astype(o_ref.dtype)

    col = lambda off: pl.BlockSpec((t, CONV_TILE), lambda j: (0, j + off))
    wspec = lambda off: pl.BlockSpec((3, CONV_TILE), lambda j: (0, j + off))
    bspec = lambda off: pl.BlockSpec((1, CONV_TILE), lambda j: (0, j + off))
    return pl.pallas_call(
        body, name="conv_fwd", grid=(n,),
        in_specs=[col(0), col(n), wspec(0), wspec(n), bspec(0), bspec(n)],
        out_specs=pl.BlockSpec((t, CONV_TILE), lambda j: (0, j)), out_shape=jax.ShapeDtypeStruct((t, D_FF), BF16),
        compiler_params=_params(("parallel",)),
    )(hu, hu, conv_w, conv_w, conv_b, conv_b)


def _conv_bwd(hu, conv_w, conv_b, d_act):
    t = hu.shape[0]
    n = N_CONV_TILES

    def body(hg_ref, hv_ref, wg_ref, wv_ref, bg_ref, bv_ref, d_ref, dh_ref, dw_ref, db_ref):
        hg, hv, wg, wv = hg_ref[...], hv_ref[...], wg_ref[...], wv_ref[...]
        gate = _conv(hg, wg, bg_ref[...])
        val = _conv(hv, wv, bv_ref[...])
        d = d_ref[...]
        sg = _sigmoid(gate)
        d_gate = d * val * (sg * (1.0 + gate * (1.0 - sg)))
        d_val = d * (gate * sg)
        for half, (dc, h, w) in enumerate(((d_gate, hg, wg), (d_val, hv, wv))):
            dh = w[2:3, :] * dc + w[1:2, :] * _shift_up(dc, 1) + w[0:1, :] * _shift_up(dc, 2)
            dh_ref[half] = dh.astype(dh_ref.dtype)
            dw_ref[half, 0:1, :] = jnp.sum(dc * _shift_down(h, 2), axis=0, keepdims=True)
            dw_ref[half, 1:2, :] = jnp.sum(dc * _shift_down(h, 1), axis=0, keepdims=True)
            dw_ref[half, 2:3, :] = jnp.sum(dc * h, axis=0, keepdims=True)
            db_ref[half] = jnp.sum(dc, axis=0, keepdims=True)

    gcol = lambda rows: pl.BlockSpec((rows, CONV_TILE), lambda j: (0, j))
    vcol = lambda rows: pl.BlockSpec((rows, CONV_TILE), lambda j: (0, j + n))
    both = lambda rows: pl.BlockSpec((2, rows, CONV_TILE), lambda j: (0, 0, j))
    return pl.pallas_call(
        body, name="conv_bwd", grid=(n,),
        in_specs=[gcol(t), vcol(t), gcol(3), vcol(3), gcol(1), vcol(1), gcol(t)],
        out_specs=[both(t), both(3), both(1)],
        out_shape=[jax.ShapeDtypeStruct((2, t, D_FF), BF16), jax.ShapeDtypeStruct((2, 3, D_FF), F32),
                   jax.ShapeDtypeStruct((2, 1, D_FF), F32)],
        compiler_params=_params(("parallel",)),
    )(hu, hu, conv_w, conv_w, conv_b, conv_b, d_act)


_NN = (((1,), (0,)), ((), ()))
_NT = (((1,), (1,)), ((), ()))
_TN = (((0,), (0,)), ((), ()))
HGRN_CB = 8
HGRN_LOCAL_CB = 16


def _bf_dot(a, b, dn):
    return lax.dot_general(a.astype(BF16), b.astype(BF16), dn, preferred_element_type=F32)


def _tril():
    n = HGRN_LOCAL_CB * CHUNK
    r = lax.broadcasted_iota(jnp.int32, (n, n), 0)
    c = lax.broadcasted_iota(jnp.int32, (n, n), 1)
    return (r // CHUNK == c // CHUNK) & (r >= c)


def _hgrn_specs(t):
    rows = HGRN_LOCAL_CB * CHUNK
    head = pl.BlockSpec((rows, HGRN_K), lambda h, n: (n, h))
    v_head = pl.BlockSpec((rows, HGRN_K), lambda h, n: (n, C_HI // HGRN_K + h))
    mats = pl.BlockSpec((1, HGRN_LOCAL_CB, HGRN_K, HGRN_K), lambda h, n: (h, n, 0, 0))
    return head, v_head, mats, (HGRN_HEADS, t // rows)


def _hgrn_local_fwd(q_in, k_in, kd, z):
    t = q_in.shape[0]
    head, v_head, mats, grid = _hgrn_specs(t)

    def body(q_ref, k_ref, kd_ref, v_ref, o_ref, u_ref):
        v = v_ref[...]
        scores = jnp.where(_tril(), _bf_dot(q_ref[...], k_ref[...], _NT), 0.0)
        o_ref[...] = _bf_dot(scores, v, _NN)
        for n in range(HGRN_LOCAL_CB):
            rows = slice(n * CHUNK, (n + 1) * CHUNK)
            u_ref[0, n] = _bf_dot(v[rows], kd_ref[rows, :], _TN)

    return pl.pallas_call(
        body, name="hgrn_local_fwd", grid=grid, in_specs=[head, head, head, v_head], out_specs=[head, mats],
        out_shape=[jax.ShapeDtypeStruct((t, D_MODEL), F32),
                   jax.ShapeDtypeStruct((HGRN_HEADS, t // CHUNK, HGRN_K, HGRN_K), F32)],
        compiler_params=_params(("parallel", "parallel")),
    )(q_in, k_in, kd, z)


def _hgrn_state_specs(t, reverse=False):
    rows = HGRN_CB * CHUNK
    nb = t // rows
    at = (lambda n: nb - 1 - n) if reverse else (lambda n: n)
    tok = pl.BlockSpec((rows, D_MODEL), lambda n: (at(n), 0))
    mats = pl.BlockSpec((HGRN_HEADS, HGRN_CB, HGRN_K, HGRN_K), lambda n: (0, at(n), 0, 0))
    return tok, mats, nb


def _hgrn_state_fwd(o_intra, qe, dec, u):
    t = qe.shape[0]
    tok, mats, nb = _hgrn_state_specs(t)

    def body(oi_ref, qe_ref, dec_ref, u_ref, o_ref, st_ref, s_ref):
        @pl.when(pl.program_id(0) == 0)
        def _():
            s_ref[...] = jnp.zeros_like(s_ref)

        st = [s_ref[h] for h in range(HGRN_HEADS)]
        for n in range(HGRN_CB):
            rows = slice(n * CHUNK, (n + 1) * CHUNK)
            for h in range(HGRN_HEADS):
                cols = slice(h * HGRN_K, (h + 1) * HGRN_K)
                st_ref[h, n] = st[h]
                o_ref[rows, cols] = oi_ref[rows, cols] + _bf_dot(qe_ref[rows, cols], st[h], _NT)
                st[h] = st[h] * dec_ref[n * CHUNK:n * CHUNK + 1, cols] + u_ref[h, n]
        for h in range(HGRN_HEADS):
            s_ref[h] = st[h]

    return pl.pallas_call(
        body, name="hgrn_state_fwd", grid=(nb,), in_specs=[tok, tok, tok, mats], out_specs=[tok, mats],
        out_shape=[jax.ShapeDtypeStruct((t, D_MODEL), F32),
                   jax.ShapeDtypeStruct((HGRN_HEADS, t // CHUNK, HGRN_K, HGRN_K), F32)],
        scratch_shapes=[pltpu.VMEM((HGRN_HEADS, HGRN_K, HGRN_K), F32)],
        compiler_params=_params(("arbitrary",)),
    )(o_intra, qe, dec, u)


def _hgrn_state_bwd(d_o, qe, dec, states):
    t = qe.shape[0]
    tok_r, mats_r, nb = _hgrn_state_specs(t, reverse=True)

    def body(do_ref, qe_ref, dec_ref, st_ref, dqe_ref, ddec_ref, du_ref, d_ref):
        @pl.when(pl.program_id(0) == 0)
        def _():
            d_ref[...] = jnp.zeros_like(d_ref)

        first_row = lax.broadcasted_iota(jnp.int32, (CHUNK, HGRN_K), 0) == 0
        d = [d_ref[h] for h in range(HGRN_HEADS)]
        for n in reversed(range(HGRN_CB)):
            rows = slice(n * CHUNK, (n + 1) * CHUNK)
            for h in range(HGRN_HEADS):
                cols = slice(h * HGRN_K, (h + 1) * HGRN_K)
                st, do = st_ref[h, n], do_ref[rows, cols]
                du_ref[h, n] = d[h]
                ddec_ref[rows, cols] = jnp.where(first_row, jnp.sum(d[h] * st, axis=0, keepdims=True), 0.0)
                dqe_ref[rows, cols] = _bf_dot(do, st, _NN)
                d[h] = d[h] * dec_ref[n * CHUNK:n * CHUNK + 1, cols] + _bf_dot(do, qe_ref[rows, cols], _TN)
        for h in range(HGRN_HEADS):
            d_ref[h] = d[h]

    out = jax.ShapeDtypeStruct((t, D_MODEL), F32)
    return pl.pallas_call(
        body, name="hgrn_state_bwd", grid=(nb,), in_specs=[tok_r, tok_r, tok_r, mats_r], out_specs=[tok_r, tok_r, mats_r],
        out_shape=[out, out, jax.ShapeDtypeStruct((HGRN_HEADS, t // CHUNK, HGRN_K, HGRN_K), F32)],
        scratch_shapes=[pltpu.VMEM((HGRN_HEADS, HGRN_K, HGRN_K), F32)],
        compiler_params=_params(("arbitrary",)),
    )(d_o, qe, dec, states)


def _hgrn_local_bwd(q_in, k_in, kd, z, d_o, d_u):
    t = q_in.shape[0]
    head, v_head, mats, grid = _hgrn_specs(t)

    def body(q_ref, k_ref, kd_ref, v_ref, do_ref, du_ref, dq_ref, dk_ref, dkd_ref, dv_ref):
        tril = _tril()
        q, k, v, do = q_ref[...], k_ref[...], v_ref[...], do_ref[...]
        scores = jnp.where(tril, _bf_dot(q, k, _NT), 0.0)
        d_scores = jnp.where(tril, _bf_dot(do, v, _NT), 0.0)
        dq_ref[...] = _bf_dot(d_scores, k, _NN)
        dk_ref[...] = _bf_dot(d_scores, q, _TN)
        dv = _bf_dot(scores, do, _TN)
        for n in range(HGRN_LOCAL_CB):
            rows = slice(n * CHUNK, (n + 1) * CHUNK)
            du = du_ref[0, n]
            dv_ref[rows, :] = dv[rows] + _bf_dot(kd_ref[rows, :], du, _NT)
            dkd_ref[rows, :] = _bf_dot(v[rows], du, _NN)

    out = jax.ShapeDtypeStruct((t, D_MODEL), F32)
    return pl.pallas_call(
        body, name="hgrn_local_bwd", grid=grid, in_specs=[head, head, head, v_head, head, mats], out_specs=[head] * 4,
        out_shape=[out] * 4, compiler_params=_params(("parallel", "parallel")),
    )(q_in, k_in, kd, z, d_o, d_u)


def _seg_bcast(xs, e):
    n = RWKV_HEAD
    lhs = [x.astype(BF16) for x in xs]
    out = jnp.dot(lhs[0] if len(lhs) == 1 else jnp.concatenate(lhs, axis=0), e, preferred_element_type=F32)
    return [out[i * n:(i + 1) * n] for i in range(len(xs))]


def _rows_to_cols(rows, diag, e):
    zero = jnp.zeros((), BF16)
    parts = [jnp.where(diag, row.astype(BF16), zero) for row in rows]
    out = jnp.dot(jnp.concatenate(parts, axis=0), e, preferred_element_type=F32)
    n = RWKV_HEAD
    return [out[i * n:(i + 1) * n] for i in range(len(rows))]


def _col_to_row(col, diag):
    return jnp.sum(jnp.where(diag, col, 0.0), axis=0, keepdims=True)


_SCAN_PAIRS = ((0, 1, 2, 3),)


def _scan_consts():
    e = _seg_matrix(SCAN_GROUP, RWKV_HEAD)
    i = lax.broadcasted_iota(jnp.int32, (RWKV_HEAD, SCAN_GROUP), 0)
    l = lax.broadcasted_iota(jnp.int32, (RWKV_HEAD, SCAN_GROUP), 1)
    groups = [slice(g * SCAN_GROUP, (g + 1) * SCAN_GROUP) for g in range(D_MODEL // SCAN_GROUP)]
    return e, (l % RWKV_HEAD) == i, groups


def _rwkv_fwd(zl, w, k, a, b, shards, placed):
    t = zl.shape[0]
    tb = SCAN_TB_FWD
    nb = t // tb
    n = len(shards)
    steps = range(tb)

    def body(*refs):
        scan(*refs[:6], *refs[6 + 2 * n:8 + 2 * n], refs[8 + 3 * n])
        gather = (refs[6:6 + n], refs[8 + 2 * n:8 + 3 * n], *refs[9 + 3 * n:])

        @pl.when(pl.program_id(0) == 0)
        def _():
            _gather_start(*gather)

        @pl.when(pl.program_id(0) == nb - 1)
        def _():
            _gather_finish(*gather)

    def scan(r_ref, w_ref, k_ref, v_ref, a_ref, b_ref, y_ref, st_ref, s_ref):
        @pl.when(pl.program_id(0) == 0)
        def _():
            s_ref[...] = jnp.zeros_like(s_ref)

        e, diag, groups = _scan_consts()
        v_cols = [_rows_to_cols([v_ref[i:i + 1, sl] for i in steps], diag, e) for sl in groups]
        s = [s_ref[:, sl] for sl in groups]
        for i in steps:
            for pair in _SCAN_PAIRS:
                sas = _seg_bcast([s[g] * a_ref[i:i + 1, groups[g]] for g in pair], e)
                for g, sa in zip(pair, sas):
                    sl = groups[g]
                    s[g] = s[g] * w_ref[i:i + 1, sl] + sa * b_ref[i:i + 1, sl] + v_cols[g][i] * k_ref[i:i + 1, sl]
                    st_ref[i, :, sl] = s[g]
        for g, sl in enumerate(groups):
            s_ref[:, sl] = s[g]
            y_cols = _seg_bcast([st_ref[i, :, sl] * r_ref[i:i + 1, sl] for i in steps], e)
            for i in steps:
                y_ref[i:i + 1, sl] = _col_to_row(y_cols[i], diag)

    blk = pl.BlockSpec((tb, D_MODEL), lambda n: (n, 0))
    v_blk = pl.BlockSpec((tb, D_MODEL), lambda n: (n, 2))
    outs = pl.pallas_call(
        body, name="rwkv_fwd", grid=(nb,), in_specs=[blk, blk, blk, v_blk, blk, blk] + [ANY] * (2 * n),
        out_specs=[blk, pl.BlockSpec((tb, RWKV_HEAD, D_MODEL), lambda i: (i, 0, 0))] + [ANY] * n,
        out_shape=[jax.ShapeDtypeStruct((t, D_MODEL), F32), jax.ShapeDtypeStruct((t, RWKV_HEAD, D_MODEL), F32)]
        + [jax.ShapeDtypeStruct(p.shape, p.dtype) for p in placed],
        input_output_aliases={6 + n + i: 2 + i for i in range(n)},
        scratch_shapes=[pltpu.VMEM((RWKV_HEAD, D_MODEL), F32), pltpu.SemaphoreType.DMA((n, 6)), pltpu.SemaphoreType.DMA((n, 6))],
        compiler_params=_params(("arbitrary",)),
    )(zl, w, k, zl, a, b, *shards, *placed)
    return outs[0], outs[1], outs[2:]


def _rwkv_bwd(zl, w, k, a, b, states, d_y, parts):
    t = zl.shape[0]
    nb = t // SCAN_TB
    n = len(parts)
    steps = range(SCAN_TB)

    def body(*refs):
        scan(*refs[:9], *refs[9 + n:15 + n], refs[15 + 2 * n])
        exchange = (refs[9:9 + n], refs[15 + n:15 + 2 * n], *refs[16 + 2 * n:])

        @pl.when(pl.program_id(0) == 0)
        def _():
            _rs_chips_start(*exchange)

        @pl.when(pl.program_id(0) == nb - 1)
        def _():
            _rs_chips_finish(*exchange)

    def scan(r_ref, w_ref, k_ref, v_ref, a_ref, b_ref, st_ref, prev_ref, dy_ref,
             dr_ref, dw_ref, dk_ref, dv_ref, da_ref, db_ref, ds_ref):
        @pl.when(pl.program_id(0) == 0)
        def _():
            ds_ref[...] = jnp.zeros_like(ds_ref)

        has_prev = (pl.program_id(0) < nb - 1).astype(F32)
        e, diag, groups = _scan_consts()
        colsum = lambda x: jnp.sum(x, axis=0, keepdims=True)

        def s_prev(i, sl):
            return st_ref[i - 1, :, sl] if i > 0 else prev_ref[0, :, sl] * has_prev

        dy_cols = [_rows_to_cols([dy_ref[i:i + 1, sl] for i in steps], diag, e) for sl in groups]
        v_cols = [_rows_to_cols([v_ref[i:i + 1, sl] for i in steps], diag, e) for sl in groups]
        sa_cols = [_seg_bcast([s_prev(i, sl) * a_ref[i:i + 1, sl] for i in steps], e) for sl in groups]
        ds = [ds_ref[:, sl] for sl in groups]
        dsk = [[None] * SCAN_TB for _ in groups]
        for i in reversed(steps):
            for pair in _SCAN_PAIRS:
                d = {}
                for g in pair:
                    sl = groups[g]
                    d[g] = ds[g] + dy_cols[g][i] * r_ref[i:i + 1, sl]
                    dr_ref[i:i + 1, sl] = colsum(st_ref[i, :, sl] * dy_cols[g][i])
                    dw_ref[i:i + 1, sl] = colsum(d[g] * s_prev(i, sl))
                    db_ref[i:i + 1, sl] = colsum(d[g] * sa_cols[g][i])
                    dk_ref[i:i + 1, sl] = colsum(d[g] * v_cols[g][i])
                    dsk[g][i] = d[g] * k_ref[i:i + 1, sl]
                dsas = _seg_bcast([d[g] * b_ref[i:i + 1, groups[g]] for g in pair], e)
                for g, dsa in zip(pair, dsas):
                    sl = groups[g]
                    da_ref[i:i + 1, sl] = colsum(s_prev(i, sl) * dsa)
                    ds[g] = d[g] * w_ref[i:i + 1, sl] + dsa * a_ref[i:i + 1, sl]
        for g, sl in enumerate(groups):
            ds_ref[:, sl] = ds[g]
            dv_cols = _seg_bcast(dsk[g], e)
            for i in steps:
                dv_ref[i:i + 1, sl] = _col_to_row(dv_cols[i], diag)

    blk = pl.BlockSpec((SCAN_TB, D_MODEL), lambda n: (nb - 1 - n, 0))
    v_blk = pl.BlockSpec((SCAN_TB, D_MODEL), lambda n: (nb - 1 - n, 2))
    out = jax.ShapeDtypeStruct((t, D_MODEL), F32)
    outs = pl.pallas_call(
        body, name="rwkv_bwd", grid=(nb,),
        in_specs=[blk, blk, blk, v_blk, blk, blk] + [
            pl.BlockSpec((SCAN_TB, RWKV_HEAD, D_MODEL), lambda i: (nb - 1 - i, 0, 0)),
            pl.BlockSpec((1, RWKV_HEAD, D_MODEL), lambda i: (jnp.maximum((nb - 1 - i) * SCAN_TB - 1, 0), 0, 0)),
            blk] + [ANY] * n,
        out_specs=[blk] * 6 + [ANY] * n,
        out_shape=[out] * 6 + [jax.ShapeDtypeStruct((3,) + p.shape[1:], p.dtype) for p in parts],
        scratch_shapes=[pltpu.VMEM((RWKV_HEAD, D_MODEL), F32), pltpu.SemaphoreType.DMA((n, 3)), pltpu.SemaphoreType.DMA((n, 3))],
        compiler_params=_params(("arbitrary",)),
    )(zl, w, k, zl, a, b, states, states, d_y, *parts)
    return outs[:6], outs[6:]


def _loss_head(h1, ff, target, g_post):
    def fn(tv, pv):
        a, f, tgt = tv
        h2, vjp = jax.vjp(lambda a_, f_, g_: a_ + _rms(f_, g_), a, f, pv[0])
        err = h2 - tgt
        loss = 0.5 * jnp.sum(jnp.mean(err * err, axis=-1, keepdims=True), axis=0, keepdims=True)
        d_a, d_f, d_g = vjp(err * (1.0 / D_MODEL))
        return [d_a, d_f], [loss, d_g]

    return _tok_call("loss_head", fn, [(h1, D_MODEL, 0), (ff, D_MODEL, 0), (target, D_MODEL, 0)], [g_post],
                     [(D_MODEL, F32), (D_MODEL, BF16)], red_shapes=[(1, 1), (1, D_MODEL)], tm=LIGHT_TM)


def _sum_call(name, terms, rows_per_block=None):
    a0, i0 = terms[0]
    r, c = a0.shape[-2:]
    tr = rows_per_block or r

    def body(*refs):
        acc = refs[0][...].reshape(tr, c)
        for ref in refs[1:-1]:
            acc = acc + ref[...].reshape(tr, c)
        refs[-1][...] = acc

    def spec(arr, idx):
        if arr.ndim == 2:
            return pl.BlockSpec((tr, c), lambda i: (i, 0))
        return pl.BlockSpec((1, tr, c), functools.partial(lambda i, idx: (idx, i, 0), idx=idx))

    return pl.pallas_call(
        body, name=name, grid=(r // tr,), in_specs=[spec(a, i) for a, i in terms],
        out_specs=pl.BlockSpec((tr, c), lambda i: (i, 0)), out_shape=jax.ShapeDtypeStruct((r, c), F32),
        compiler_params=_params(("parallel",)),
    )(*[a for a, _ in terms])


def _adamw_math(w, g, m, v):
    m2 = ADAM_B1 * m + (1.0 - ADAM_B1) * g
    v2 = ADAM_B2 * v + (1.0 - ADAM_B2) * (g * g)
    m_hat = m2 / (1.0 - ADAM_B1 ** ADAM_STEP)
    v_hat = v2 / (1.0 - ADAM_B2 ** ADAM_STEP)
    return -ADAM_LR * (m_hat / (jnp.sqrt(v_hat) + ADAM_EPS) + ADAM_WD * w), m2, v2


def _adamw(name, w, g, m, v, bm, bn, g_transposed=False, exchange=None):
    r, c = w.shape
    grid = (pl.cdiv(r, bm), pl.cdiv(c, bn))

    def body(*refs):
        w_ref, g_ref, m_ref, v_ref = refs[:4]
        go_ref, d_ref, mo_ref, vo_ref = refs[-6:-2] if exchange else refs[4:8]
        if exchange:
            riders = (refs[4], refs[-7], refs[-2], refs[-1])
            first = (pl.program_id(0) == 0) & (pl.program_id(1) == 0)

            @pl.when(first)
            def _():
                _exchange8_start(*riders)

        g = g_ref[...].T if g_transposed else g_ref[...]
        d, m2, v2 = _adamw_math(w_ref[...], g, m_ref[...], v_ref[...])
        go_ref[...] = g
        d_ref[...] = d
        mo_ref[...] = m2
        vo_ref[...] = v2
        if exchange:
            @pl.when((pl.program_id(0) == grid[0] - 1) & (pl.program_id(1) == grid[1] - 1))
            def _():
                _exchange8_finish(*riders)

    blk = pl.BlockSpec((bm, bn), lambda i, j: (i, j))
    g_blk = pl.BlockSpec((bn, bm), lambda i, j: (j, i)) if g_transposed else blk
    out = jax.ShapeDtypeStruct((r, c), F32)
    if not exchange:
        return pl.pallas_call(
            body, name=name, grid=grid, in_specs=[blk, g_blk, blk, blk],
            out_specs=[blk] * 4, out_shape=[out] * 4, compiler_params=_params(("parallel", "parallel")),
        )(w, g, m, v)
    parts, landing = exchange
    outs = pl.pallas_call(
        body, name=name, grid=grid, in_specs=[blk, g_blk, blk, blk, ANY, ANY],
        out_specs=[ANY] + [blk] * 4, out_shape=[jax.ShapeDtypeStruct(landing.shape, landing.dtype)] + [out] * 4,
        input_output_aliases={5: 0}, scratch_shapes=[pltpu.SemaphoreType.DMA((7,)), pltpu.SemaphoreType.DMA((7,))],
        compiler_params=_params(("arbitrary", "arbitrary")),
    )(w, g, m, v, parts, landing)
    return outs[1:], outs[0]


ANY = pl.BlockSpec(memory_space=pl.ANY)


def _place():
    x, y, c = lax.axis_index("x"), lax.axis_index("y"), lax.axis_index("c")
    chips = [(1 - x, y), (x, 1 - y), (1 - x, 1 - y)]
    return x, y, c, chips


def _sibling():
    return (lax.axis_index("x"), lax.axis_index("y"), 1 - lax.axis_index("c"))


def _wait_all(local, remote):
    for cp in local:
        cp.wait()
    for cp in remote:
        cp.wait_send()


def _place_shard(name, shard, place):
    r, cols = shard.shape
    tr = r // 4

    def body(s_ref, in_ref, out_ref):
        out_ref[...] = in_ref[...]

    return pl.pallas_call(
        body, name=name,
        grid_spec=pltpu.PrefetchScalarGridSpec(
            num_scalar_prefetch=1, grid=(4,), in_specs=[pl.BlockSpec((tr, cols), lambda i, s: (i, 0))],
            out_specs=pl.BlockSpec((tr, cols), lambda i, s: (4 * s[1] + i, 0))),
        out_shape=jax.ShapeDtypeStruct((N_SHARD * r, cols), shard.dtype), compiler_params=_params(("arbitrary",)),
    )(place, shard)


def _gather_copies(ins, outs, send_sems, recv_sems, only_first=False):
    x, y, c, chips = _place()
    me, sibling = (x, y, c), _sibling()

    def rows(k, px, py, pc):
        h = ins[k].shape[0] // 2
        return outs[k].at[pl.ds((2 * px + py) * 2 * h + pc * h, h), :]

    def copy(k, j, block, to, src=None):
        return pltpu.make_async_remote_copy(
            src_ref=rows(k, *block) if src is None else src, dst_ref=rows(k, *block),
            send_sem=send_sems.at[k, j], recv_sem=recv_sems.at[k, j], device_id=to, device_id_type=MESH)

    each = [(k, j, chip) for k in range(len(ins)) for j, chip in enumerate(chips)]
    half = lambda k: ins[k].at[pl.ds(c * (ins[k].shape[0] // 2), ins[k].shape[0] // 2), :]
    first = [copy(k, j, me, (*chip, c), src=half(k)) for k, j, chip in each]
    if only_first:
        return first
    arrive = [copy(k, j, (*chip, c), me) for k, j, chip in each]
    passed = [copy(k, 3 + j, (*chip, c), sibling) for k, j, chip in each]
    landed = [copy(k, 3 + j, (*chip, 1 - c), me) for k, j, chip in each]
    return first, arrive, passed, landed


def _gather_start(ins, outs, send_sems, recv_sems):
    for cp in _gather_copies(ins, outs, send_sems, recv_sems, only_first=True):
        cp.start()


def _gather_finish(ins, outs, send_sems, recv_sems):
    first, arrive, passed, landed = _gather_copies(ins, outs, send_sems, recv_sems)
    for arrival, forward in zip(arrive, passed):
        arrival.wait_recv()
        forward.start()
    for cp in landed:
        cp.wait_recv()
    _wait_all([], first + passed)


def _gather_shards(shards, placed):
    n = len(shards)

    def body(*refs):
        ins, outs = refs[:n], refs[2 * n:3 * n]
        _gather_start(ins, outs, *refs[3 * n:])
        _gather_finish(ins, outs, *refs[3 * n:])

    return pl.pallas_call(
        body, name="gather_shards", in_specs=[ANY] * (2 * n), out_specs=[ANY] * n,
        out_shape=[jax.ShapeDtypeStruct(a.shape, a.dtype) for a in placed],
        input_output_aliases={n + k: k for k in range(n)},
        scratch_shapes=[pltpu.SemaphoreType.DMA((n, 6)), pltpu.SemaphoreType.DMA((n, 6))],
    )(*shards, *placed)


def _exchange8_copies(in_ref, out_ref, send_sems, recv_sems, only_sends=False):
    x, y, c, _ = _place()
    sends, arrivals = [], []
    for rel in range(1, 8):
        dx, dy, dc = rel >> 2 & 1, rel >> 1 & 1, rel & 1
        sends.append(pltpu.make_async_remote_copy(
            src_ref=in_ref.at[2 * (x ^ dx) + (y ^ dy)], dst_ref=out_ref.at[4 * x + 2 * y + c],
            send_sem=send_sems.at[rel - 1], recv_sem=recv_sems.at[rel - 1],
            device_id=(x ^ dx, y ^ dy, c ^ dc), device_id_type=MESH))
        if not only_sends:
            arrivals.append(pltpu.make_async_remote_copy(
                src_ref=in_ref.at[0], dst_ref=out_ref.at[4 * (x ^ dx) + 2 * (y ^ dy) + (c ^ dc)],
                send_sem=send_sems.at[rel - 1], recv_sem=recv_sems.at[rel - 1],
                device_id=(x, y, c), device_id_type=MESH))
    return sends, arrivals


def _exchange8_start(in_ref, out_ref, send_sems, recv_sems):
    for cp in _exchange8_copies(in_ref, out_ref, send_sems, recv_sems, only_sends=True)[0]:
        cp.start()


def _exchange8_finish(in_ref, out_ref, send_sems, recv_sems):
    sends, arrivals = _exchange8_copies(in_ref, out_ref, send_sems, recv_sems)
    for cp in arrivals:
        cp.wait_recv()
    _wait_all([], sends)


def _rs_sibling_copies(ins, outs, send_sems, recv_sems):
    c = lax.axis_index("c")
    return [pltpu.make_async_remote_copy(
        src_ref=ins[k].at[2 * s + 1 - c], dst_ref=outs[k].at[s], send_sem=send_sems.at[k, s], recv_sem=recv_sems.at[k, s],
        device_id=_sibling(), device_id_type=MESH) for k in range(len(ins)) for s in range(N_SHARD)]


def _rs_sibling_start(ins, outs, send_sems, recv_sems):
    for cp in _rs_sibling_copies(ins, outs, send_sems, recv_sems):
        cp.start()


def _rs_sibling_finish(ins, outs, send_sems, recv_sems):
    sends = _rs_sibling_copies(ins, outs, send_sems, recv_sems)
    for cp in sends:
        cp.wait_recv()
    _wait_all([], sends)


def _rs_sibling(grads):
    n = len(grads)

    def body(*refs):
        _rs_sibling_start(refs[:n], refs[n:2 * n], *refs[2 * n:])
        _rs_sibling_finish(refs[:n], refs[n:2 * n], *refs[2 * n:])

    return pl.pallas_call(
        body, name="rs_sibling", in_specs=[ANY] * n, out_specs=[ANY] * n,
        out_shape=[jax.ShapeDtypeStruct((N_SHARD,) + a.shape[1:], a.dtype) for a in grads],
        scratch_shapes=[pltpu.SemaphoreType.DMA((n, N_SHARD)), pltpu.SemaphoreType.DMA((n, N_SHARD))],
    )(*grads)


def _rs_chips_copies(ins, outs, send_sems, recv_sems):
    x, y, c, chips = _place()
    return [pltpu.make_async_remote_copy(
        src_ref=ins[k].at[2 * px + py], dst_ref=outs[k].at[j], send_sem=send_sems.at[k, j], recv_sem=recv_sems.at[k, j],
        device_id=(px, py, c), device_id_type=MESH) for k in range(len(ins)) for j, (px, py) in enumerate(chips)]


def _rs_chips_start(ins, outs, send_sems, recv_sems):
    for cp in _rs_chips_copies(ins, outs, send_sems, recv_sems):
        cp.start()


def _rs_chips_finish(ins, outs, send_sems, recv_sems):
    sends = _rs_chips_copies(ins, outs, send_sems, recv_sems)
    for cp in sends:
        cp.wait_recv()
    _wait_all([], sends)


def _rs_finish(bufs):
    n = len(bufs)

    def body(*refs):
        outs = refs[n:2 * n]
        send_sems, recv_sems = refs[2 * n:]
        c = lax.axis_index("c")
        sends = []
        for k in range(n):
            cp = pltpu.make_async_remote_copy(
                src_ref=outs[k].at[c], dst_ref=outs[k].at[c], send_sem=send_sems.at[k], recv_sem=recv_sems.at[k],
                device_id=_sibling(), device_id_type=MESH)
            cp.start()
            sends.append(cp)
        for k in range(n):
            pltpu.make_async_remote_copy(
                src_ref=outs[k].at[c], dst_ref=outs[k].at[1 - c], send_sem=send_sems.at[k], recv_sem=recv_sems.at[k],
                device_id=_sibling(), device_id_type=MESH).wait_recv()
        _wait_all([], sends)

    return pl.pallas_call(
        body, name="rs_finish", in_specs=[ANY] * n, out_specs=[ANY] * n,
        out_shape=[jax.ShapeDtypeStruct(a.shape, a.dtype) for a in bufs], input_output_aliases={k: k for k in range(n)},
        scratch_shapes=[pltpu.SemaphoreType.DMA((n,)), pltpu.SemaphoreType.DMA((n,))],
    )(*bufs)


def _sum3d(name, terms, scalars, grid_lead, out_lead, out_index, tr=None, out_dtype=F32):
    h, c = terms[0][0].shape[1:]
    tr = tr or h

    def body(s_ref, *refs):
        acc = refs[0][...].astype(F32)
        for ref in refs[1:-1]:
            acc = acc + ref[...].astype(F32)
        refs[-1][...] = acc.astype(refs[-1].dtype)

    in_specs = [pl.BlockSpec((1, tr, c), functools.partial(lambda l, i, s_ref, f: (f(l, s_ref), i, 0), f=f)) for _, f in terms]
    out_spec = pl.BlockSpec((1, tr, c), lambda l, i, s_ref: (out_index(l, s_ref), i, 0))
    return pl.pallas_call(
        body, name=name,
        grid_spec=pltpu.PrefetchScalarGridSpec(num_scalar_prefetch=1, grid=(grid_lead, h // tr), in_specs=in_specs, out_specs=out_spec),
        out_shape=jax.ShapeDtypeStruct((out_lead, h, c), out_dtype), compiler_params=_params(("arbitrary", "arbitrary")),
    )(scalars, *[a for a, _ in terms])


def _halves(grads):
    return [a.reshape(2 * N_SHARD, a.shape[0] // (2 * N_SHARD), a.shape[1]) for a in grads]


def _rs_stage1(g8, place, tag, from_sibling=None):
    from_sibling = _rs_sibling(g8) if from_sibling is None else from_sibling
    parts = [_sum3d(f"rs_add1_{tag}{k}", [(g8[k], lambda l, s: 2 * l + s[0]), (from_sibling[k], lambda l, s: l)], place,
                    N_SHARD, N_SHARD, lambda l, s: l, tr=g8[k].shape[1] // 2, out_dtype=BF16)
             for k in range(len(g8))]
    return g8, from_sibling, parts


def _rs_stage3(stage1, from_chips, place, tag):
    g8, from_sibling, _ = stage1
    mine = [(lambda l, s: 2 * s[1] + s[0]), (lambda l, s: s[1])]
    bufs = [_sum3d(f"rs_add2_{tag}{k}", [(g8[k], mine[0]), (from_sibling[k], mine[1])]
                   + [(from_chips[k], functools.partial(lambda l, s, j: j, j=j)) for j in range(3)],
                   place, 1, 2, lambda l, s: s[0], tr=g8[k].shape[1] // 2)
            for k in range(len(g8))]
    whole = _rs_finish(bufs)
    return [w.reshape(2 * w.shape[1], w.shape[2]) for w in whole]


_LATE = ["w_a", "w_b", "w_out", "w_down", "w_up_t"]


def _device_step(x, target, p, late_shards, late_placed, place):
    t = x.shape[0]
    d = D_MODEL
    tok = lambda arr, c=0, w=d: (arr, w, c)
    f32x = lambda n: [(d, F32)] * n
    rp_params = [p["w0"], p["w2p"], p["a0"], p["a2p"], p["g2p"], p["k_k"], p["k_a"]]
    post_params = [p["ln_w"], p["ln_b"], p["r_k"]]
    g = {}

    (xn,) = _tok_fwd("norm1_fwd", _fn_norm, [tok(x)], [p["g1"]], [(d, BF16)], tm=LIGHT_TM)
    z = _mm("in_proj", xn, p["w_in_t"], "nt", tm=t, tn=256)
    q_in, k_in, kd, qe, dec = _tok_fwd("hgates_fwd", _fn_hgates, [tok(z, 0), tok(z, 1)], [p["lb2"]], f32x(5))
    o_intra, u = _hgrn_local_fwd(q_in, k_in, kd, z)
    o_raw, h_states = _hgrn_state_fwd(o_intra, qe, dec, u)
    zl = _lerp_fwd(z, p["mu"])
    lora = tok(zl, 3 * d // LORA, LORA)
    decay, kr2, avec, bvec, gate = _tok_fwd("rprep_fwd", _fn_rprep, [tok(zl, 1), lora], rp_params, f32x(5))
    y, r_states, late = _rwkv_fwd(zl, decay, kr2, avec, bvec, late_shards, late_placed)
    p = dict(p, **dict(zip(_LATE, late)))
    (o_a,) = _tok_fwd("hpost_fwd", _fn_hpost, [tok(o_raw), tok(z, 3)], [p["gnorm"]], [(d, BF16)])
    post_toks = [tok(y), tok(zl, 0), tok(kr2), tok(zl, 2), tok(gate)]
    (o_b,) = _tok_fwd("rpost_fwd", _fn_rpost, post_toks, post_params, [(d, BF16)])
    y_a = _mm("branch_a", o_a, p["w_a"], "nn")
    y_b = _mm("branch_b", o_b, p["w_b"], "nn")
    merge_toks = [tok(z, C_G // 256, 256), tok(z, (C_G + d) // 256, 256), tok(y_a, 0, 256), tok(y_b, 0, 256)]
    (merged,) = _tok_fwd("merge_fwd", _fn_merge, merge_toks, [], [(256, BF16)], col_grid=4, tm=512)
    mix = _mm("out_proj", merged, p["w_out"], "nn")
    h1, xn2 = _tok_fwd("res1_fwd", _fn_res1, [tok(x), tok(mix)], [p["g_post1"], p["g_pre2"]], [(d, F32), (d, BF16)],
                       tm=LIGHT_TM)
    hu = _mm("up_proj", xn2, p["w_up_t"], "nt", tm=t, tn=512)
    act = _conv_fwd(hu, p["conv_w"], p["conv_b"])
    ff = _mm("down_proj", act, p["w_down"], "nn")
    d_h1, d_ff, loss, g["g_post2"] = _loss_head(h1, ff, target, p["g_post2"])

    d_act = _mm("d_act", d_ff, p["w_down"], "nt")
    g["w_down"] = _mm("dw_down", act, d_ff, "tn", tm=256, tn=1024)
    d_hu, d_cw, d_cb = _conv_bwd(hu, p["conv_w"], p["conv_b"], d_act)
    g["conv_w"], g["conv_b"] = d_cw.transpose(1, 0, 2).reshape(3, 2 * D_FF), d_cb.reshape(1, 2 * D_FF)
    d_hu = d_hu.reshape(2 * t, D_FF)
    d_xn2 = _mm("d_xn2", d_hu, p["w_up_t"], "nn", tm=t, tk=D_FF, mk=(t, 2 * D_FF), a_map=lambda i, j, q: (q, 0))
    g["w_up_t"] = _mm("dw_up", d_hu, xn2, "tn", tm=CONV_TILE, tn=1024, mk=(2 * D_FF, t),
                      a_map=lambda i, j, q: (i // N_CONV_TILES, i % N_CONV_TILES))
    d_x_res, d_mix, g["g_post1"], g["g_pre2"] = _tok_bwd(
        "res1_bwd", _fn_res1, [tok(x), tok(mix)], [p["g_post1"], p["g_pre2"]], [[tok(d_h1)], [tok(d_xn2)]],
        [(d, F32), (d, BF16)], tm=LIGHT_TM)
    d_merged = _mm("d_merged", d_mix, p["w_out"], "nt")
    g["w_out"] = _mm("dw_out", merged, d_mix, "tn")
    d_ga, d_gb, d_ya, d_yb = _tok_bwd("merge_bwd", _fn_merge, merge_toks, [], [[tok(d_merged, 0, 256)]],
                                      [(256, BF16)] * 4, col_grid=4, tm=512)
    d_oa = _mm("d_oa", d_ya, p["w_a"], "nt")
    g["w_a"] = _mm("dw_a", o_a, d_ya, "tn")
    d_ob = _mm("d_ob", d_yb, p["w_b"], "nt")
    g["w_b"] = _mm("dw_b", o_b, d_yb, "tn")
    d_oraw, d_hg, g["gnorm"] = _tok_bwd("hpost_bwd", _fn_hpost, [tok(o_raw), tok(z, 3)], [p["gnorm"]], [[tok(d_oa)]],
                                        [(d, F32), (d, BF16)])
    late_g8 = _halves([g[n] for n in _LATE])
    rpost = _tok_bwd("rpost_bwd", _fn_rpost, post_toks, post_params, [[tok(d_ob)]], f32x(5), sibling_rider=late_g8)
    d_y, d_r1, d_kr2_1, d_v1, d_gate, g["ln_w"], g["ln_b"], g["r_k"] = rpost[:8]
    stage1 = _rs_stage1(late_g8, place, "late", from_sibling=rpost[8:])
    (d_r2, d_decay, d_kr2_2, d_v2, d_avec, d_bvec), from_chips = _rwkv_bwd(
        zl, decay, kr2, avec, bvec, r_states, d_y, stage1[2])
    g.update(zip(_LATE, _rs_stage3(stage1, from_chips, place, "late")))
    prep = _tok_bwd("rprep_bwd", _fn_rprep, [tok(zl, 1), lora], rp_params,
                    [[tok(d_decay)], [tok(d_kr2_1), tok(d_kr2_2)], [tok(d_avec)], [tok(d_bvec)], [tok(d_gate)]],
                    [(d, F32), (LORA, F32)])
    d_kr, d_lora = prep[:2]
    g["w0"], g["w2p"], g["a0"], g["a2p"], g["g2p"], g["k_k"], g["k_a"] = prep[2:]
    dz_r, g["mu"] = _lerp_bwd(z, p["mu"], (d_r1, d_r2), d_kr, (d_v1, d_v2), d_lora)
    d_qe, d_dec, d_u = _hgrn_state_bwd(d_oraw, qe, dec, h_states)
    d_q_in, d_k_in, d_kd, d_vi = _hgrn_local_bwd(q_in, k_in, kd, z, d_oraw, d_u)
    d_hq, d_hf, g["lb2"] = _tok_bwd("hgates_bwd", _fn_hgates, [tok(z, 0), tok(z, 1)], [p["lb2"]],
                                    [[tok(d_q_in)], [tok(d_k_in)], [tok(d_kd)], [tok(d_qe)], [tok(d_dec)]], [(d, BF16)] * 2)
    dz = jnp.concatenate([d_hq, d_hf, d_vi.astype(BF16), d_hg, dz_r, d_ga, d_gb], axis=1)
    stage1 = _rs_stage1(_halves([_mm("dw_in", dz, xn, "tn", tm=256, tn=1024)]), place, "w_in")
    d_xn, from_chips = _mm("d_xn", dz, p["w_in_t"], "nn", tm=1024, tn=512, tk=IN_COLS // 2, riders=stage1[2])
    (g["w_in_t"],) = _rs_stage3(stage1, from_chips, place, "w_in")
    grad_x, g["g1"] = _tok_bwd("norm1_bwd", _fn_norm, [tok(x)], [p["g1"]], [[tok(d_xn)]], [(d, F32)],
                               add_to_first=tok(d_x_res), tm=LIGHT_TM)
    return loss, grad_x, g


_WEIGHTS = ["attn_pre_norm", "w_in", "hgrn_lb", "hgrn_gnorm", "w_branch_a", "rwkv_mu", "rwkv_w0", "rwkv_w2", "rwkv_a0",
            "rwkv_a2", "rwkv_g2", "rwkv_k_k", "rwkv_k_a", "rwkv_r_k", "rwkv_ln_w", "rwkv_ln_b", "w_branch_b", "w_out",
            "attn_post_norm", "ffn_pre_norm", "w_up", "conv_w", "conv_b", "w_down", "ffn_post_norm"]
_REPLICATED = [("attn_pre_norm", "g1"), ("hgrn_lb", "lb2"), ("hgrn_gnorm", "gnorm"), ("rwkv_mu", "mu"), ("rwkv_w0", "w0"),
               ("rwkv_a0", "a0"), ("rwkv_k_k", "k_k"), ("rwkv_k_a", "k_a"), ("rwkv_r_k", "r_k"), ("rwkv_ln_w", "ln_w"),
               ("rwkv_ln_b", "ln_b"), ("attn_post_norm", "g_post1"), ("ffn_pre_norm", "g_pre2"), ("conv_b", "conv_b"),
               ("ffn_post_norm", "g_post2")]
SLAB_COLS = 1024


def _pack(arrays):
    pieces, total = [], 0
    for a in arrays:
        flat = a.reshape(-1)
        rows = -(-flat.shape[0] // SLAB_COLS)
        pieces.append(jnp.pad(flat, (0, rows * SLAB_COLS - flat.shape[0])).reshape(rows, SLAB_COLS))
        total += rows
    if total % 8:
        pieces.append(jnp.zeros((8 - total % 8, SLAB_COLS), F32))
    return jnp.concatenate(pieces, axis=0)


def _unpack(slab, shapes):
    out, at = [], 0
    for s in shapes:
        size = 1
        for dim in s:
            size *= dim
        rows = -(-size // SLAB_COLS)
        out.append(slab[at:at + rows].reshape(-1)[:size].reshape(s))
        at += rows
    return out


def kernel(x, attn_pre_norm, w_in, hgrn_lb, hgrn_gnorm, w_branch_a, rwkv_mu, rwkv_w0, rwkv_w2, rwkv_a0, rwkv_a2, rwkv_g2, rwkv_k_k, rwkv_k_a, rwkv_r_k, rwkv_ln_w, rwkv_ln_b, w_branch_b, w_out, attn_post_norm, ffn_pre_norm, w_up, conv_w, conv_b, w_down, ffn_post_norm, loss_target, m_attn_pre_norm, m_w_in, m_hgrn_lb, m_hgrn_gnorm, m_w_branch_a, m_rwkv_mu, m_rwkv_w0, m_rwkv_w2, m_rwkv_a0, m_rwkv_a2, m_rwkv_g2, m_rwkv_k_k, m_rwkv_k_a, m_rwkv_r_k, m_rwkv_ln_w, m_rwkv_ln_b, m_w_branch_b, m_w_out, m_attn_post_norm, m_ffn_pre_norm, m_w_up, m_conv_w, m_conv_b, m_w_down, m_ffn_post_norm, v_attn_pre_norm, v_w_in, v_hgrn_lb, v_hgrn_gnorm, v_w_branch_a, v_rwkv_mu, v_rwkv_w0, v_rwkv_w2, v_rwkv_a0, v_rwkv_a2, v_rwkv_g2, v_rwkv_k_k, v_rwkv_k_a, v_rwkv_r_k, v_rwkv_ln_w, v_rwkv_ln_b, v_w_branch_b, v_w_out, v_attn_post_norm, v_ffn_pre_norm, v_w_up, v_conv_w, v_conv_b, v_w_down, v_ffn_post_norm):
    given = dict(locals())
    w = {n: given[n] for n in _WEIGHTS}
    mom = {n: given["m_" + n] for n in _WEIGHTS}
    var = {n: given["v_" + n] for n in _WEIGHTS}
    shard = 2 * lax.axis_index("x") + lax.axis_index("y")
    place = jnp.stack([lax.axis_index("c"), shard]).astype(jnp.int32)
    row = lambda a: a.reshape(1, -1)
    lora_of = lambda d: jnp.concatenate([d["rwkv_w2"][0], d["rwkv_a2"][0], d["rwkv_g2"][0]], axis=0)

    shards = [w["w_in"][0].T.astype(BF16), lora_of(w), jnp.pad(w["conv_w"][0], ((0, 29), (0, 0)))]
    late_shards = [w["w_branch_a"][0].astype(BF16), w["w_branch_b"][0].astype(BF16), w["w_out"][0].astype(BF16),
                   w["w_down"][0].astype(BF16), w["w_up"][0].T.astype(BF16)]
    placed = [_place_shard(f"place_{k}", a, place) for k, a in enumerate(shards)]
    late_placed = [_place_shard(f"place_late_{k}", a, place) for k, a in enumerate(late_shards)]
    w_in_t, lora_g, conv_g = _gather_shards(shards, placed)
    lora_full = lora_g.reshape(N_SHARD, LORA, 256).transpose(1, 0, 2).reshape(LORA, D_MODEL)
    conv_full = conv_g.reshape(N_SHARD, 32, 2 * D_FF // N_SHARD)[:, :3].transpose(1, 0, 2).reshape(3, 2 * D_FF)
    lrow = lax.broadcasted_iota(jnp.int32, (LORA, 1), 0)
    p = {
        "g1": row(w["attn_pre_norm"]), "lb2": w["hgrn_lb"], "gnorm": row(w["hgrn_gnorm"]), "w_in_t": w_in_t,
        "mu": row(w["rwkv_mu"]), "w0": row(w["rwkv_w0"]), "a0": row(w["rwkv_a0"]),
        "w2p": jnp.where(lrow < 64, lora_full, 0.0), "a2p": jnp.where((lrow >= 64) & (lrow < 128), lora_full, 0.0),
        "g2p": jnp.where(lrow >= 128, lora_full, 0.0),
        "k_k": row(w["rwkv_k_k"]), "k_a": row(w["rwkv_k_a"]), "r_k": row(w["rwkv_r_k"]), "ln_w": row(w["rwkv_ln_w"]),
        "ln_b": row(w["rwkv_ln_b"]), "g_post1": row(w["attn_post_norm"]),
        "g_pre2": row(w["ffn_pre_norm"]), "conv_w": conv_full, "conv_b": row(w["conv_b"]),
        "g_post2": row(w["ffn_post_norm"]),
    }

    loss, grad_x, g = _device_step(x[0], loss_target[0], p, late_shards, late_placed, place)

    g_in_t = g["w_in_t"]
    g_a, g_b, g_o, g_dn, g_up_t = [g[n] for n in _LATE]
    rep_shapes = [w[n].shape for n, _ in _REPLICATED]
    rep = _pack([g[key] for _, key in _REPLICATED])
    n_rep_rows = rep.shape[0]
    cw = 2 * D_FF // N_SHARD
    lora_rows, conv_rows = LORA * 256 // SLAB_COLS, -(-3 * cw // SLAB_COLS)
    lora_g = jnp.concatenate([g["w2p"][0:64], g["a2p"][64:128], g["g2p"][128:256]], axis=0)
    lora_parts = lora_g.reshape(LORA, N_SHARD, 256).transpose(1, 0, 2).reshape(N_SHARD, lora_rows, SLAB_COLS)
    conv_parts = g["conv_w"].reshape(3, N_SHARD, cw).transpose(1, 0, 2).reshape(N_SHARD, 3 * cw)
    conv_parts = jnp.pad(conv_parts, ((0, 0), (0, conv_rows * SLAB_COLS - 3 * cw))).reshape(N_SHARD, conv_rows, SLAB_COLS)
    n_rows = n_rep_rows + lora_rows + conv_rows
    fill = jnp.zeros((N_SHARD, -n_rows % 8, SLAB_COLS), F32)
    parts = jnp.concatenate([jnp.broadcast_to(rep, (N_SHARD,) + rep.shape), lora_parts, conv_parts, fill], axis=1)
    me = 4 * lax.axis_index("x") + 2 * lax.axis_index("y") + lax.axis_index("c")
    landing = lax.dynamic_update_slice(jnp.zeros((8,) + parts.shape[1:], F32),
                                       lax.dynamic_index_in_dim(parts, shard, 0, keepdims=True), (me, 0, 0))

    res = {}

    def put(name, outs, shape=None):
        res[name] = [o.reshape(w[name].shape if shape is None else shape) for o in outs]

    w_in_out, gathered = _adamw("adamw_w_in", w["w_in"][0], g_in_t, mom["w_in"][0], var["w_in"][0], 1024, 128,
                                g_transposed=True, exchange=(parts, landing))
    put("w_in", w_in_out)
    summed = _sum3d("small_sum", [(gathered, functools.partial(lambda l, s, i: i, i=i)) for i in range(8)], place, 1, 1,
                    lambda l, s: 0)[0]
    lora_grad = summed[n_rep_rows:n_rep_rows + lora_rows].reshape(LORA, 256)
    conv_grad = summed[n_rep_rows + lora_rows:n_rows].reshape(-1)[:3 * cw].reshape(3, cw)
    put("w_up", _adamw("adamw_w_up", w["w_up"][0], g_up_t, mom["w_up"][0], var["w_up"][0], 1024, 128, g_transposed=True))
    for name, grad in (("w_branch_a", g_a), ("w_branch_b", g_b), ("w_out", g_o)):
        put(name, _adamw("adamw_" + name, w[name][0], grad, mom[name][0], var[name][0], 256, 1024))
    put("w_down", _adamw("adamw_w_down", w["w_down"][0], g_dn, mom["w_down"][0], var["w_down"][0], 176, 1024))
    put("conv_w", _adamw("adamw_conv_w", w["conv_w"][0], conv_grad, mom["conv_w"][0], var["conv_w"][0], 3, 2 * D_FF // N_SHARD))
    lora_out = _adamw("adamw_lora", lora_of(w), lora_grad, lora_of(mom), lora_of(var), LORA, 256)
    for name, lo, hi in (("rwkv_w2", 0, 64), ("rwkv_a2", 64, 128), ("rwkv_g2", 128, 256)):
        put(name, [o[lo:hi] for o in lora_out])
    rep_names = [n for n, _ in _REPLICATED]
    rep_out = _adamw("adamw_small", _pack([w[n] for n in rep_names]), summed[:n_rep_rows], _pack([mom[n] for n in rep_names]),
                     _pack([var[n] for n in rep_names]), n_rep_rows, SLAB_COLS)
    for name, parts in zip(rep_names, zip(*[_unpack(o, rep_shapes) for o in rep_out])):
        put(name, list(parts))

    loss = lax.psum(loss[0, 0], ("x", "y", "c"))
    return (loss, grad_x[None], *[res[n][0] for n in _WEIGHTS], *[res[n][1] for n in _WEIGHTS],
            *[res[n][2] for n in _WEIGHTS], *[res[n][3] for n in _WEIGHTS])
```

```python
import functools

import jax
import jax.numpy as jnp
from jax import lax
from jax.experimental import pallas as pl
from jax.experimental.pallas import tpu as pltpu

F32, BF16 = jnp.float32, jnp.bfloat16
MESH = pl.DeviceIdType.MESH

D_MODEL = 1024
HGRN_HEADS = 8
HGRN_K = 128
HGRN_SCALE = HGRN_K ** -0.5
CHUNK = 32
RWKV_HEAD = 64
LORA = 256
D_FF = 2816
EPS = 1e-6
GN_EPS = 1e-5 * RWKV_HEAD
N_SHARD = 4
ADAM_LR, ADAM_B1, ADAM_B2, ADAM_EPS, ADAM_WD, ADAM_STEP = 0.001, 0.9, 0.999, 1e-08, 0.01, 10

LANES = 128
VMEM_LIMIT = 56 * 1024 * 1024
SCAN_TB = 32
SCAN_TB_FWD = 64
SCAN_GROUP = 256
LIGHT_TM = 256

C_HQ, C_HF, C_HI, C_HG = 0, 1024, 2048, 3072
C_R = 4096
R_COLS = 3328
C_G = 7424
IN_COLS = 9472


def _params(sem=None, **kw):
    return pltpu.CompilerParams(dimension_semantics=sem, vmem_limit_bytes=VMEM_LIMIT, **kw)


def _seg_matrix(n, seg):
    r = lax.broadcasted_iota(jnp.int32, (n, n), 0) // seg
    c = lax.broadcasted_iota(jnp.int32, (n, n), 1) // seg
    return (r == c).astype(BF16)


def _split3(x):
    hi = x.astype(BF16)
    r1 = x - hi.astype(F32)
    mid = r1.astype(BF16)
    lo = (r1 - mid.astype(F32)).astype(BF16)
    return hi, mid, lo


def _segsum_impl(x, seg):
    e = _seg_matrix(LANES, seg)
    outs = []
    for g in range(x.shape[1] // LANES):
        hi, mid, lo = _split3(x[:, g * LANES:(g + 1) * LANES])
        outs.append(jnp.dot(hi, e, preferred_element_type=F32) + jnp.dot(mid, e, preferred_element_type=F32)
                    + jnp.dot(lo, e, preferred_element_type=F32))
    return outs[0] if len(outs) == 1 else jnp.concatenate(outs, axis=1)


def _make_segsum(seg):
    @jax.custom_vjp
    def f(x):
        return _segsum_impl(x, seg)

    f.defvjp(lambda x: (_segsum_impl(x, seg), None), lambda _, ct: (_segsum_impl(ct, seg),))
    return f


_segsum64 = _make_segsum(RWKV_HEAD)
_segsum128 = _make_segsum(HGRN_K)


def _chunk_mm_impl(x, kind, transposed):
    n = x.shape[0]
    r = lax.broadcasted_iota(jnp.int32, (n, n), 1 if transposed else 0)
    c = lax.broadcasted_iota(jnp.int32, (n, n), 0 if transposed else 1)
    same = (r // CHUNK) == (c // CHUNK)
    if kind == "cumsum":
        m = same & (r >= c)
    else:
        m = same & (c % CHUNK == (CHUNK // 2 - 1 if kind == "mid" else CHUNK - 1))
    m = m.astype(BF16)
    hi, mid, lo = _split3(x)
    return (jnp.dot(m, hi, preferred_element_type=F32) + jnp.dot(m, mid, preferred_element_type=F32)
            + jnp.dot(m, lo, preferred_element_type=F32))


def _make_chunk_mm(kind):
    @jax.custom_vjp
    def f(x):
        return _chunk_mm_impl(x, kind, False)

    f.defvjp(lambda x: (_chunk_mm_impl(x, kind, False), None), lambda _, ct: (_chunk_mm_impl(ct, kind, True),))
    return f


_chunk_cumsum = _make_chunk_mm("cumsum")
_chunk_mid = _make_chunk_mm("mid")
_chunk_last = _make_chunk_mm("last")


@jax.custom_vjp
def _bdot(x, w):
    return jnp.dot(x.astype(BF16), w.astype(BF16), preferred_element_type=F32)


def _bdot_fwd(x, w):
    return _bdot(x, w), (x, w)


def _bdot_bwd(res, ct):
    x, w = res
    ctb = ct.astype(BF16)
    dx = lax.dot_general(ctb, w.astype(BF16), (((1,), (1,)), ((), ())), preferred_element_type=F32)
    dw = lax.dot_general(x.astype(BF16), ctb, (((0,), (0,)), ((), ())), preferred_element_type=F32)
    return dx, dw


_bdot.defvjp(_bdot_fwd, _bdot_bwd)


def _sigmoid(x):
    return 1.0 / (1.0 + jnp.exp(-x))


def _silu(x):
    return x * _sigmoid(x)


def _softplus(x):
    return jnp.maximum(x, 0.0) + jnp.log(1.0 + jnp.exp(-jnp.abs(x)))


def _rms(x, g):
    return x * lax.rsqrt(jnp.mean(x * x, axis=-1, keepdims=True) + EPS) * g


def _fn_norm(t, p):
    return [_rms(t[0], p[0])]


def _fn_hgates(t, p):
    hq, hf = t
    lb2 = p[0]
    m = jnp.max(lb2, axis=0, keepdims=True)
    e = jnp.exp(lb2 - m)
    first = lax.broadcasted_iota(jnp.int32, e.shape, 0) == 0
    lb = jnp.sum(jnp.where(first, e, 0.0), axis=0, keepdims=True) / jnp.sum(e, axis=0, keepdims=True)
    f = lb + (1.0 - lb) * _sigmoid(hf)
    q, k = _silu(hq) * HGRN_SCALE, 1.0 - f
    b = _chunk_cumsum(jnp.log(f))
    b_ref, b_last = _chunk_mid(b), _chunk_last(b)
    return [q * jnp.exp(b - b_ref), k * jnp.exp(b_ref - b), k * jnp.exp(b_last - b), q * jnp.exp(b), jnp.exp(b_last)]


def _fn_hpost(t, p):
    o, hg = t
    ms = _segsum128(o * o) * (1.0 / HGRN_K)
    return [o * lax.rsqrt(ms + EPS) * p[0] * _silu(hg)]


def _fn_rprep(t, p):
    kr, lora = t
    w0, w2p, a0, a2p, g2p, k_k, k_a = p
    pre_w = w0 + _bdot(jnp.tanh(lora), w2p)
    w_log = -_softplus(-pre_w) - 0.5
    decay = jnp.exp(-jnp.exp(w_log))
    a = _sigmoid(a0 + _bdot(lora, a2p))
    g = _bdot(_sigmoid(lora), g2p)
    kk = kr * k_k
    kk = kk / jnp.maximum(jnp.sqrt(_segsum64(kk * kk)), 1e-12)
    kr2 = kr * (1.0 + (a - 1.0) * k_a)
    return [decay, kr2, -kk, kk * a, g]


def _fn_rpost(t, p):
    y, r, kr2, v, g = t
    ln_w, ln_b, r_k = p
    mu = _segsum64(y) * (1.0 / RWKV_HEAD)
    yc = y - mu
    var = _segsum64(yc * yc) * (1.0 / RWKV_HEAD)
    yn = yc * lax.rsqrt(var + GN_EPS) * ln_w + ln_b
    bonus = _segsum64(r * kr2 * r_k) * v
    return [(yn + bonus) * g]


def _fn_merge(t, p):
    ga, gb, ya, yb = t
    return [_sigmoid(ga) * ya + _sigmoid(gb) * yb]


def _fn_res1(t, p):
    x, mix = t
    h1 = x + _rms(mix, p[0])
    return [h1, _rms(h1, p[1])]


def _tok_call(name, fn, toks, params, outs, red_shapes=(), tm=128, col_grid=1, sibling_rider=None):
    n_t, n_p, n_o, n_red = len(toks), len(params), len(outs), len(red_shapes)
    n_r = 0 if sibling_rider is None else len(sibling_rider)
    t_len = toks[0][0].shape[0]
    tm = min(tm, t_len)
    grid = (t_len // tm, col_grid)

    def body(*refs):
        first_out = n_t + n_p + n_r
        if n_r:
            rider = (refs[n_t + n_p:first_out], refs[first_out + n_o + n_red:first_out + n_o + n_red + n_r], *refs[-2:])

            @pl.when((pl.program_id(0) == 0) & (pl.program_id(1) == 0))
            def _():
                _rs_sibling_start(*rider)

        tv = [r[...].astype(F32) for r in refs[:n_t]]
        pv = [r[...] for r in refs[n_t:n_t + n_p]]
        o, red = fn(tv, pv)
        for ref, val in zip(refs[first_out:first_out + n_o], o):
            ref[...] = val.astype(ref.dtype)
        if n_r:
            @pl.when((pl.program_id(0) == grid[0] - 1) & (pl.program_id(1) == grid[1] - 1))
            def _():
                _rs_sibling_finish(*rider)

        red_refs = refs[first_out + n_o:first_out + n_o + n_red]
        if red_refs:
            first = pl.program_id(0) == 0

            @pl.when(first)
            def _():
                for ref, val in zip(red_refs, red):
                    ref[...] = val

            @pl.when(jnp.logical_not(first))
            def _():
                for ref, val in zip(red_refs, red):
                    ref[...] += val

    in_specs = [pl.BlockSpec((tm, w), functools.partial(lambda i, j, c: (i, c + j), c=c)) for (_, w, c) in toks]
    in_specs += [pl.BlockSpec(p.shape, lambda i, j: (0, 0)) for p in params]
    out_specs = [pl.BlockSpec((tm, w), lambda i, j: (i, j)) for (w, _) in outs]
    out_specs += [pl.BlockSpec(s, lambda i, j: (0, 0)) for s in red_shapes]
    out_shape = [jax.ShapeDtypeStruct((t_len, w * col_grid), dt) for (w, dt) in outs]
    out_shape += [jax.ShapeDtypeStruct(s, F32) for s in red_shapes]
    scratch = []
    if n_r:
        in_specs += [ANY] * n_r
        out_specs += [ANY] * n_r
        out_shape += [jax.ShapeDtypeStruct((N_SHARD,) + a.shape[1:], a.dtype) for a in sibling_rider]
        scratch = [pltpu.SemaphoreType.DMA((n_r, N_SHARD)), pltpu.SemaphoreType.DMA((n_r, N_SHARD))]
    return pl.pallas_call(
        body, name=name, grid=grid, in_specs=in_specs, out_specs=out_specs, out_shape=out_shape, scratch_shapes=scratch,
        compiler_params=_params(("arbitrary", "arbitrary")),
    )(*[a for (a, _, _) in toks], *params, *(sibling_rider or []))


def _tok_fwd(name, fn, toks, params, outs, **kw):
    return _tok_call(name, lambda tv, pv: (fn(tv, pv), []), toks, params, outs, **kw)


def _tok_bwd(name, fn, toks, params, cts, want, add_to_first=None, **kw):
    n_t = len(toks)
    flat = [c for group in cts for c in group]
    extra = [] if add_to_first is None else [add_to_first]

    def bwd(tv, pv):
        prim, rest = tv[:n_t], tv[n_t:]
        ct, at = [], 0
        for group in cts:
            ct.append(functools.reduce(lambda u, v: u + v, rest[at:at + len(group)]))
            at += len(group)
        _, vjp = jax.vjp(lambda *a: tuple(fn(list(a[:n_t]), list(a[n_t:]))), *prim, *pv)
        g = vjp(tuple(ct))
        tok_grads = [g[i] for i in range(n_t) if want[i] is not None]
        if extra:
            tok_grads[0] = tok_grads[0] + rest[at]
        return tok_grads, list(g[n_t:])

    return _tok_call(name, bwd, list(toks) + flat + extra, params, [w for w in want if w is not None],
                     red_shapes=[p.shape for p in params], **kw)


def _mm(name, a, b, mode, out_dtype=F32, tm=None, tn=None, tk=None, riders=None, a_map=None, mk=None):
    if mode == "nn":
        (m, k), (_, n) = a.shape, b.shape
    elif mode == "nt":
        (m, k), (n, _) = a.shape, b.shape
    else:
        (k, m), (_, n) = a.shape, b.shape
    if mk is not None:
        m, k = mk
    tm = (512 if mode == "tn" else 2048) if tm is None else tm
    tn = (512 if mode == "tn" else 256) if tn is None else tn
    tk = k if tk is None else tk
    tm, tn = min(tm, m), min(tn, n)
    nk = k // tk
    assert m % tm == 0 and n % tn == 0 and k % tk == 0, (name, a.shape, b.shape, tm, tn, tk)
    a_spec = pl.BlockSpec((tk, tm), lambda i, j, q: (q, i)) if mode == "tn" else pl.BlockSpec((tm, tk), lambda i, j, q: (i, q))
    if a_map is not None:
        a_spec = pl.BlockSpec(a_spec.block_shape, a_map)
    b_spec = pl.BlockSpec((tn, tk), lambda i, j, q: (j, q)) if mode == "nt" else pl.BlockSpec((tk, tn), lambda i, j, q: (q, j))
    dn = {"nn": (((1,), (0,)), ((), ())), "nt": (((1,), (1,)), ((), ())), "tn": (((0,), (0,)), ((), ()))}[mode]
    grid = (m // tm, n // tn, nk)
    nr = 0 if riders is None else len(riders)

    def body(*refs):
        a_ref, b_ref, o_ref = refs[0], refs[1], refs[2 + nr]
        acc = refs[3 + 2 * nr] if nk > 1 else None
        if nr:
            exchange = (refs[2:2 + nr], refs[3 + nr:3 + 2 * nr], *refs[-2:])
            at = [pl.program_id(ax) for ax in range(3)]

            @pl.when((at[0] == 0) & (at[1] == 0) & (at[2] == 0))
            def _():
                _rs_chips_start(*exchange)

        p = lax.dot_general(a_ref[...], b_ref[...], dn, preferred_element_type=F32)
        if nk == 1:
            o_ref[...] = p.astype(o_ref.dtype)
        else:
            q = pl.program_id(2)

            @pl.when(q == 0)
            def _():
                acc[...] = p

            @pl.when(q > 0)
            def _():
                acc[...] += p

            @pl.when(q == nk - 1)
            def _():
                o_ref[...] = acc[...].astype(o_ref.dtype)

        if nr:
            @pl.when((at[0] == grid[0] - 1) & (at[1] == grid[1] - 1) & (at[2] == grid[2] - 1))
            def _():
                _rs_chips_finish(*exchange)

    scratch = [pltpu.VMEM((tm, tn), F32)] if nk > 1 else []
    out_specs = [pl.BlockSpec((tm, tn), lambda i, j, q: (i, j))]
    out_shape = [jax.ShapeDtypeStruct((m, n), out_dtype)]
    if nr:
        scratch += [pltpu.SemaphoreType.DMA((nr, 3)), pltpu.SemaphoreType.DMA((nr, 3))]
        out_specs += [ANY] * nr
        out_shape += [jax.ShapeDtypeStruct((3,) + r.shape[1:], r.dtype) for r in riders]
    outs = pl.pallas_call(
        body, name=name, grid=grid, in_specs=[a_spec, b_spec] + [ANY] * nr, out_specs=out_specs, out_shape=out_shape,
        scratch_shapes=scratch,
        compiler_params=_params(("arbitrary",) * 3 if nr else ("parallel", "parallel", "arbitrary")),
    )(a, b, *(riders or []))
    return (outs[0], outs[1:]) if nr else outs[0]


def _shift_down(z, n):
    rows = lax.broadcasted_iota(jnp.int32, z.shape, 0)
    return jnp.where(rows < n, 0.0, pltpu.roll(z, n, 0))


def _shift_up(z, n):
    t = z.shape[0]
    rows = lax.broadcasted_iota(jnp.int32, z.shape, 0)
    return jnp.where(rows >= t - n, 0.0, pltpu.roll(z, t - n, 0))


def _lerp_fwd(z, mu):
    t = z.shape[0]
    w = 256

    def body(z_ref, mu_ref, o_ref):
        zz = z_ref[...]
        o_ref[...] = zz + mu_ref[...] * (_shift_down(zz, 1) - zz)

    return pl.pallas_call(
        body, name="lerp_fwd", grid=(R_COLS // w,),
        in_specs=[pl.BlockSpec((t, w), lambda j: (0, C_R // w + j)), pl.BlockSpec((1, w), lambda j: (0, j))],
        out_specs=pl.BlockSpec((t, w), lambda j: (0, j)), out_shape=jax.ShapeDtypeStruct((t, R_COLS), F32),
        compiler_params=_params(("parallel",)),
    )(z, mu)


def _lerp_bwd(z, mu, d_r, d_k, d_v, d_lora):
    t = z.shape[0]
    w = 256
    per = D_MODEL // w

    def body(z_ref, mu_ref, r1_ref, r2_ref, k_ref, v1_ref, v2_ref, l_ref, dz_ref, dmu_ref):
        j = pl.program_id(0)
        zz, m = z_ref[...], mu_ref[...]
        d = jnp.where(j < per, r1_ref[...] + r2_ref[...],
                      jnp.where(j < 2 * per, k_ref[...], jnp.where(j < 3 * per, v1_ref[...] + v2_ref[...], l_ref[...])))
        dz_ref[...] = (d * (1.0 - m) + _shift_up(d * m, 1)).astype(dz_ref.dtype)
        dmu_ref[...] = jnp.sum(d * (_shift_down(zz, 1) - zz), axis=0, keepdims=True)

    piece = lambda first: pl.BlockSpec((t, w), lambda j: (0, jnp.clip(j - first, 0, per - 1)))
    return pl.pallas_call(
        body, name="lerp_bwd", grid=(R_COLS // w,),
        in_specs=[pl.BlockSpec((t, w), lambda j: (0, C_R // w + j)), pl.BlockSpec((1, w), lambda j: (0, j)),
                  piece(0), piece(0), piece(per), piece(2 * per), piece(2 * per), pl.BlockSpec((t, w), lambda j: (0, 0))],
        out_specs=[pl.BlockSpec((t, w), lambda j: (0, j)), pl.BlockSpec((1, w), lambda j: (0, j))],
        out_shape=[jax.ShapeDtypeStruct((t, R_COLS), BF16), jax.ShapeDtypeStruct((1, R_COLS), F32)],
        compiler_params=_params(("arbitrary",)),
    )(z, mu, *d_r, d_k, *d_v, d_lora)


CONV_TILE = 256
N_CONV_TILES = D_FF // CONV_TILE


def _conv(h, w, b):
    return b + w[0:1, :] * _shift_down(h, 2) + w[1:2, :] * _shift_down(h, 1) + w[2:3, :] * h


def _conv_fwd(hu, conv_w, conv_b):
    t = hu.shape[0]
    n = N_CONV_TILES

    def body(hg_ref, hv_ref, wg_ref, wv_ref, bg_ref, bv_ref, o_ref):
        gate = _conv(hg_ref[...], wg_ref[...], bg_ref[...])
        val = _conv(hv_ref[...], wv_ref[...], bv_ref[...])
        o_ref[...] = (_silu(gate) * val).astype(o_ref.dtype)

    col = lambda off: pl.BlockSpec((t, CONV_TILE), lambda j: (0, j + off))
    wspec = lambda off: pl.BlockSpec((3, CONV_TILE), lambda j: (0, j + off))
    bspec = lambda off: pl.BlockSpec((1, CONV_TILE), lambda j: (0, j + off))
    return pl.pallas_call(
        body, name="conv_fwd", grid=(n,),
        in_specs=[col(0), col(n), wspec(0), wspec(n), bspec(0), bspec(n)],
        out_specs=pl.BlockSpec((t, CONV_TILE), lambda j: (0, j)), out_shape=jax.ShapeDtypeStruct((t, D_FF), BF16),
        compiler_params=_params(("parallel",)),
    )(hu, hu, conv_w, conv_w, conv_b, conv_b)


def _conv_bwd(hu, conv_w, conv_b, d_act):
    t = hu.shape[0]
    n = N_CONV_TILES

    def body(hg_ref, hv_ref, wg_ref, wv_ref, bg_ref, bv_ref, d_ref, dh_ref, dw_ref, db_ref):
        hg, hv, wg, wv = hg_ref[...], hv_ref[...], wg_ref[...], wv_ref[...]
        gate = _conv(hg, wg, bg_ref[...])
        val = _conv(hv, wv, bv_ref[...])
        d = d_ref[...]
        sg = _sigmoid(gate)
        d_gate = d * val * (sg * (1.0 + gate * (1.0 - sg)))
        d_val = d * (gate * sg)
        for half, (dc, h, w) in enumerate(((d_gate, hg, wg), (d_val, hv, wv))):
            dh = w[2:3, :] * dc + w[1:2, :] * _shift_up(dc, 1) + w[0:1, :] * _shift_up(dc, 2)
            dh_ref[half] = dh.astype(dh_ref.dtype)
            dw_ref[half, 0:1, :] = jnp.sum(dc * _shift_down(h, 2), axis=0, keepdims=True)
            dw_ref[half, 1:2, :] = jnp.sum(dc * _shift_down(h, 1), axis=0, keepdims=True)
            dw_ref[half, 2:3, :] = jnp.sum(dc * h, axis=0, keepdims=True)
            db_ref[half] = jnp.sum(dc, axis=0, keepdims=True)

    gcol = lambda rows: pl.BlockSpec((rows, CONV_TILE), lambda j: (0, j))
    vcol = lambda rows: pl.BlockSpec((rows, CONV_TILE), lambda j: (0, j + n))
    both = lambda rows: pl.BlockSpec((2, rows, CONV_TILE), lambda j: (0, 0, j))
    return pl.pallas_call(
        body, name="conv_bwd", grid=(n,),
        in_specs=[gcol(t), vcol(t), gcol(3), vcol(3), gcol(1), vcol(1), gcol(t)],
        out_specs=[both(t), both(3), both(1)],
        out_shape=[jax.ShapeDtypeStruct((2, t, D_FF), BF16), jax.ShapeDtypeStruct((2, 3, D_FF), F32),
                   jax.ShapeDtypeStruct((2, 1, D_FF), F32)],
        compiler_params=_params(("parallel",)),
    )(hu, hu, conv_w, conv_w, conv_b, conv_b, d_act)


_NN = (((1,), (0,)), ((), ()))
_NT = (((1,), (1,)), ((), ()))
_TN = (((0,), (0,)), ((), ()))
HGRN_CB = 8
HGRN_LOCAL_CB = 16


def _bf_dot(a, b, dn):
    return lax.dot_general(a.astype(BF16), b.astype(BF16), dn, preferred_element_type=F32)


def _tril():
    n = HGRN_LOCAL_CB * CHUNK
    r = lax.broadcasted_iota(jnp.int32, (n, n), 0)
    c = lax.broadcasted_iota(jnp.int32, (n, n), 1)
    return (r // CHUNK == c // CHUNK) & (r >= c)


def _hgrn_specs(t):
    rows = HGRN_LOCAL_CB * CHUNK
    head = pl.BlockSpec((rows, HGRN_K), lambda h, n: (n, h))
    v_head = pl.BlockSpec((rows, HGRN_K), lambda h, n: (n, C_HI // HGRN_K + h))
    mats = pl.BlockSpec((1, HGRN_LOCAL_CB, HGRN_K, HGRN_K), lambda h, n: (h, n, 0, 0))
    return head, v_head, mats, (HGRN_HEADS, t // rows)


def _hgrn_local_fwd(q_in, k_in, kd, z):
    t = q_in.shape[0]
    head, v_head, mats, grid = _hgrn_specs(t)

    def body(q_ref, k_ref, kd_ref, v_ref, o_ref, u_ref):
        v = v_ref[...]
        scores = jnp.where(_tril(), _bf_dot(q_ref[...], k_ref[...], _NT), 0.0)
        o_ref[...] = _bf_dot(scores, v, _NN)
        for n in range(HGRN_LOCAL_CB):
            rows = slice(n * CHUNK, (n + 1) * CHUNK)
            u_ref[0, n] = _bf_dot(v[rows], kd_ref[rows, :], _TN)

    return pl.pallas_call(
        body, name="hgrn_local_fwd", grid=grid, in_specs=[head, head, head, v_head], out_specs=[head, mats],
        out_shape=[jax.ShapeDtypeStruct((t, D_MODEL), F32),
                   jax.ShapeDtypeStruct((HGRN_HEADS, t // CHUNK, HGRN_K, HGRN_K), F32)],
        compiler_params=_params(("parallel", "parallel")),
    )(q_in, k_in, kd, z)


def _hgrn_state_specs(t, reverse=False):
    rows = HGRN_CB * CHUNK
    nb = t // rows
    at = (lambda n: nb - 1 - n) if reverse else (lambda n: n)
    tok = pl.BlockSpec((rows, D_MODEL), lambda n: (at(n), 0))
    mats = pl.BlockSpec((HGRN_HEADS, HGRN_CB, HGRN_K, HGRN_K), lambda n: (0, at(n), 0, 0))
    return tok, mats, nb


def _hgrn_state_fwd(o_intra, qe, dec, u):
    t = qe.shape[0]
    tok, mats, nb = _hgrn_state_specs(t)

    def body(oi_ref, qe_ref, dec_ref, u_ref, o_ref, st_ref, s_ref):
        @pl.when(pl.program_id(0) == 0)
        def _():
            s_ref[...] = jnp.zeros_like(s_ref)

        st = [s_ref[h] for h in range(HGRN_HEADS)]
        for n in range(HGRN_CB):
            rows = slice(n * CHUNK, (n + 1) * CHUNK)
            for h in range(HGRN_HEADS):
                cols = slice(h * HGRN_K, (h + 1) * HGRN_K)
                st_ref[h, n] = st[h]
                o_ref[rows, cols] = oi_ref[rows, cols] + _bf_dot(qe_ref[rows, cols], st[h], _NT)
                st[h] = st[h] * dec_ref[n * CHUNK:n * CHUNK + 1, cols] + u_ref[h, n]
        for h in range(HGRN_HEADS):
            s_ref[h] = st[h]

    return pl.pallas_call(
        body, name="hgrn_state_fwd", grid=(nb,), in_specs=[tok, tok, tok, mats], out_specs=[tok, mats],
        out_shape=[jax.ShapeDtypeStruct((t, D_MODEL), F32),
                   jax.ShapeDtypeStruct((HGRN_HEADS, t // CHUNK, HGRN_K, HGRN_K), F32)],
        scratch_shapes=[pltpu.VMEM((HGRN_HEADS, HGRN_K, HGRN_K), F32)],
        compiler_params=_params(("arbitrary",)),
    )(o_intra, qe, dec, u)


def _hgrn_state_bwd(d_o, qe, dec, states):
    t = qe.shape[0]
    tok_r, mats_r, nb = _hgrn_state_specs(t, reverse=True)

    def body(do_ref, qe_ref, dec_ref, st_ref, dqe_ref, ddec_ref, du_ref, d_ref):
        @pl.when(pl.program_id(0) == 0)
        def _():
            d_ref[...] = jnp.zeros_like(d_ref)

        first_row = lax.broadcasted_iota(jnp.int32, (CHUNK, HGRN_K), 0) == 0
        d = [d_ref[h] for h in range(HGRN_HEADS)]
        for n in reversed(range(HGRN_CB)):
            rows = slice(n * CHUNK, (n + 1) * CHUNK)
            for h in range(HGRN_HEADS):
                cols = slice(h * HGRN_K, (h + 1) * HGRN_K)
                st, do = st_ref[h, n], do_ref[rows, cols]
                du_ref[h, n] = d[h]
                ddec_ref[rows, cols] = jnp.where(first_row, jnp.sum(d[h] * st, axis=0, keepdims=True), 0.0)
                dqe_ref[rows, cols] = _bf_dot(do, st, _NN)
                d[h] = d[h] * dec_ref[n * CHUNK:n * CHUNK + 1, cols] + _bf_dot(do, qe_ref[rows, cols], _TN)
        for h in range(HGRN_HEADS):
            d_ref[h] = d[h]

    out = jax.ShapeDtypeStruct((t, D_MODEL), F32)
    return pl.pallas_call(
        body, name="hgrn_state_bwd", grid=(nb,), in_specs=[tok_r, tok_r, tok_r, mats_r], out_specs=[tok_r, tok_r, mats_r],
        out_shape=[out, out, jax.ShapeDtypeStruct((HGRN_HEADS, t // CHUNK, HGRN_K, HGRN_K), F32)],
        scratch_shapes=[pltpu.VMEM((HGRN_HEADS, HGRN_K, HGRN_K), F32)],
        compiler_params=_params(("arbitrary",)),
    )(d_o, qe, dec, states)


def _hgrn_local_bwd(q_in, k_in, kd, z, d_o, d_u):
    t = q_in.shape[0]
    head, v_head, mats, grid = _hgrn_specs(t)

    def body(q_ref, k_ref, kd_ref, v_ref, do_ref, du_ref, dq_ref, dk_ref, dkd_ref, dv_ref):
        tril = _tril()
        q, k, v, do = q_ref[...], k_ref[...], v_ref[...], do_ref[...]
        scores = jnp.where(tril, _bf_dot(q, k, _NT), 0.0)
        d_scores = jnp.where(tril, _bf_dot(do, v, _NT), 0.0)
        dq_ref[...] = _bf_dot(d_scores, k, _NN)
        dk_ref[...] = _bf_dot(d_scores, q, _TN)
        dv = _bf_dot(scores, do, _TN)
        for n in range(HGRN_LOCAL_CB):
            rows = slice(n * CHUNK, (n + 1) * CHUNK)
            du = du_ref[0, n]
            dv_ref[rows, :] = dv[rows] + _bf_dot(kd_ref[rows, :], du, _NT)
            dkd_ref[rows, :] = _bf_dot(v[rows], du, _NN)

    out = jax.ShapeDtypeStruct((t, D_MODEL), F32)
    return pl.pallas_call(
        body, name="hgrn_local_bwd", grid=grid, in_specs=[head, head, head, v_head, head, mats], out_specs=[head] * 4,
        out_shape=[out] * 4, compiler_params=_params(("parallel", "parallel")),
    )(q_in, k_in, kd, z, d_o, d_u)


def _seg_bcast(xs, e):
    n = RWKV_HEAD
    lhs = [x.astype(BF16) for x in xs]
    out = jnp.dot(lhs[0] if len(lhs) == 1 else jnp.concatenate(lhs, axis=0), e, preferred_element_type=F32)
    return [out[i * n:(i + 1) * n] for i in range(len(xs))]


def _rows_to_cols(rows, diag, e):
    zero = jnp.zeros((), BF16)
    parts = [jnp.where(diag, row.astype(BF16), zero) for row in rows]
    out = jnp.dot(jnp.concatenate(parts, axis=0), e, preferred_element_type=F32)
    n = RWKV_HEAD
    return [out[i * n:(i + 1) * n] for i in range(len(rows))]


def _col_to_row(col, diag):
    return jnp.sum(jnp.where(diag, col, 0.0), axis=0, keepdims=True)


_SCAN_PAIRS = ((0, 1, 2, 3),)


def _scan_consts():
    e = _seg_matrix(SCAN_GROUP, RWKV_HEAD)
    i = lax.broadcasted_iota(jnp.int32, (RWKV_HEAD, SCAN_GROUP), 0)
    l = lax.broadcasted_iota(jnp.int32, (RWKV_HEAD, SCAN_GROUP), 1)
    groups = [slice(g * SCAN_GROUP, (g + 1) * SCAN_GROUP) for g in range(D_MODEL // SCAN_GROUP)]
    return e, (l % RWKV_HEAD) == i, groups


def _rwkv_fwd(zl, w, k, a, b, shards, placed):
    t = zl.shape[0]
    tb = SCAN_TB_FWD
    nb = t // tb
    n = len(shards)
    steps = range(tb)

    def body(*refs):
        scan(*refs[:6], *refs[6 + 2 * n:8 + 2 * n], refs[8 + 3 * n])
        gather = (refs[6:6 + n], refs[8 + 2 * n:8 + 3 * n], *refs[9 + 3 * n:])

        @pl.when(pl.program_id(0) == 0)
        def _():
            _gather_start(*gather)

        @pl.when(pl.program_id(0) == nb - 1)
        def _():
            _gather_finish(*gather)

    def scan(r_ref, w_ref, k_ref, v_ref, a_ref, b_ref, y_ref, st_ref, s_ref):
        @pl.when(pl.program_id(0) == 0)
        def _():
            s_ref[...] = jnp.zeros_like(s_ref)

        e, diag, groups = _scan_consts()
        v_cols = [_rows_to_cols([v_ref[i:i + 1, sl] for i in steps], diag, e) for sl in groups]
        s = [s_ref[:, sl] for sl in groups]
        for i in steps:
            for pair in _SCAN_PAIRS:
                sas = _seg_bcast([s[g] * a_ref[i:i + 1, groups[g]] for g in pair], e)
                for g, sa in zip(pair, sas):
                    sl = groups[g]
                    s[g] = s[g] * w_ref[i:i + 1, sl] + sa * b_ref[i:i + 1, sl] + v_cols[g][i] * k_ref[i:i + 1, sl]
                    st_ref[i, :, sl] = s[g]
        for g, sl in enumerate(groups):
            s_ref[:, sl] = s[g]
            y_cols = _seg_bcast([st_ref[i, :, sl] * r_ref[i:i + 1, sl] for i in steps], e)
            for i in steps:
                y_ref[i:i + 1, sl] = _col_to_row(y_cols[i], diag)

    blk = pl.BlockSpec((tb, D_MODEL), lambda n: (n, 0))
    v_blk = pl.BlockSpec((tb, D_MODEL), lambda n: (n, 2))
    outs = pl.pallas_call(
        body, name="rwkv_fwd", grid=(nb,), in_specs=[blk, blk, blk, v_blk, blk, blk] + [ANY] * (2 * n),
        out_specs=[blk, pl.BlockSpec((tb, RWKV_HEAD, D_MODEL), lambda i: (i, 0, 0))] + [ANY] * n,
        out_shape=[jax.ShapeDtypeStruct((t, D_MODEL), F32), jax.ShapeDtypeStruct((t, RWKV_HEAD, D_MODEL), F32)]
        + [jax.ShapeDtypeStruct(p.shape, p.dtype) for p in placed],
        input_output_aliases={6 + n + i: 2 + i for i in range(n)},
        scratch_shapes=[pltpu.VMEM((RWKV_HEAD, D_MODEL), F32), pltpu.SemaphoreType.DMA((n, 7)), pltpu.SemaphoreType.DMA((n, 7))],
        compiler_params=_params(("arbitrary",)),
    )(zl, w, k, zl, a, b, *shards, *placed)
    return outs[0], outs[1], outs[2:]


def _rwkv_bwd(zl, w, k, a, b, states, d_y, parts):
    t = zl.shape[0]
    nb = t // SCAN_TB
    n = len(parts)
    steps = range(SCAN_TB)

    def body(*refs):
        scan(*refs[:9], *refs[9 + n:15 + n], refs[15 + 2 * n])
        exchange = (refs[9:9 + n], refs[15 + n:15 + 2 * n], *refs[16 + 2 * n:])

        @pl.when(pl.program_id(0) == 0)
        def _():
            _rs_chips_start(*exchange)

        @pl.when(pl.program_id(0) == nb - 1)
        def _():
            _rs_chips_finish(*exchange)

    def scan(r_ref, w_ref, k_ref, v_ref, a_ref, b_ref, st_ref, prev_ref, dy_ref,
             dr_ref, dw_ref, dk_ref, dv_ref, da_ref, db_ref, ds_ref):
        @pl.when(pl.program_id(0) == 0)
        def _():
            ds_ref[...] = jnp.zeros_like(ds_ref)

        has_prev = (pl.program_id(0) < nb - 1).astype(F32)
        e, diag, groups = _scan_consts()
        colsum = lambda x: jnp.sum(x, axis=0, keepdims=True)

        def s_prev(i, sl):
            return st_ref[i - 1, :, sl] if i > 0 else prev_ref[0, :, sl] * has_prev

        dy_cols = [_rows_to_cols([dy_ref[i:i + 1, sl] for i in steps], diag, e) for sl in groups]
        v_cols = [_rows_to_cols([v_ref[i:i + 1, sl] for i in steps], diag, e) for sl in groups]
        sa_cols = [_seg_bcast([s_prev(i, sl) * a_ref[i:i + 1, sl] for i in steps], e) for sl in groups]
        ds = [ds_ref[:, sl] for sl in groups]
        dsk = [[None] * SCAN_TB for _ in groups]
        for i in reversed(steps):
            for pair in _SCAN_PAIRS:
                d = {}
                for g in pair:
                    sl = groups[g]
                    d[g] = ds[g] + dy_cols[g][i] * r_ref[i:i + 1, sl]
                    dr_ref[i:i + 1, sl] = colsum(st_ref[i, :, sl] * dy_cols[g][i])
                    dw_ref[i:i + 1, sl] = colsum(d[g] * s_prev(i, sl))
                    db_ref[i:i + 1, sl] = colsum(d[g] * sa_cols[g][i])
                    dk_ref[i:i + 1, sl] = colsum(d[g] * v_cols[g][i])
                    dsk[g][i] = d[g] * k_ref[i:i + 1, sl]
                dsas = _seg_bcast([d[g] * b_ref[i:i + 1, groups[g]] for g in pair], e)
                for g, dsa in zip(pair, dsas):
                    sl = groups[g]
                    da_ref[i:i + 1, sl] = colsum(s_prev(i, sl) * dsa)
                    ds[g] = d[g] * w_ref[i:i + 1, sl] + dsa * a_ref[i:i + 1, sl]
        for g, sl in enumerate(groups):
            ds_ref[:, sl] = ds[g]
            dv_cols = _seg_bcast(dsk[g], e)
            for i in steps:
                dv_ref[i:i + 1, sl] = _col_to_row(dv_cols[i], diag)

    blk = pl.BlockSpec((SCAN_TB, D_MODEL), lambda n: (nb - 1 - n, 0))
    v_blk = pl.BlockSpec((SCAN_TB, D_MODEL), lambda n: (nb - 1 - n, 2))
    out = jax.ShapeDtypeStruct((t, D_MODEL), F32)
    outs = pl.pallas_call(
        body, name="rwkv_bwd", grid=(nb,),
        in_specs=[blk, blk, blk, v_blk, blk, blk] + [
            pl.BlockSpec((SCAN_TB, RWKV_HEAD, D_MODEL), lambda i: (nb - 1 - i, 0, 0)),
            pl.BlockSpec((1, RWKV_HEAD, D_MODEL), lambda i: (jnp.maximum((nb - 1 - i) * SCAN_TB - 1, 0), 0, 0)),
            blk] + [ANY] * n,
        out_specs=[blk] * 6 + [ANY] * n,
        out_shape=[out] * 6 + [jax.ShapeDtypeStruct((3,) + p.shape[1:], p.dtype) for p in parts],
        scratch_shapes=[pltpu.VMEM((RWKV_HEAD, D_MODEL), F32), pltpu.SemaphoreType.DMA((n, 3)), pltpu.SemaphoreType.DMA((n, 3))],
        compiler_params=_params(("arbitrary",)),
    )(zl, w, k, zl, a, b, states, states, d_y, *parts)
    return outs[:6], outs[6:]


def _loss_head(h1, ff, target, g_post):
    def fn(tv, pv):
        a, f, tgt = tv
        h2, vjp = jax.vjp(lambda a_, f_, g_: a_ + _rms(f_, g_), a, f, pv[0])
        err = h2 - tgt
        loss = 0.5 * jnp.sum(jnp.mean(err * err, axis=-1, keepdims=True), axis=0, keepdims=True)
        d_a, d_f, d_g = vjp(err * (1.0 / D_MODEL))
        return [d_a, d_f], [loss, d_g]

    return _tok_call("loss_head", fn, [(h1, D_MODEL, 0), (ff, D_MODEL, 0), (target, D_MODEL, 0)], [g_post],
                     [(D_MODEL, F32), (D_MODEL, BF16)], red_shapes=[(1, 1), (1, D_MODEL)], tm=LIGHT_TM)


def _sum_call(name, terms, rows_per_block=None):
    a0, i0 = terms[0]
    r, c = a0.shape[-2:]
    tr = rows_per_block or r

    def body(*refs):
        acc = refs[0][...].reshape(tr, c)
        for ref in refs[1:-1]:
            acc = acc + ref[...].reshape(tr, c)
        refs[-1][...] = acc

    def spec(arr, idx):
        if arr.ndim == 2:
            return pl.BlockSpec((tr, c), lambda i: (i, 0))
        return pl.BlockSpec((1, tr, c), functools.partial(lambda i, idx: (idx, i, 0), idx=idx))

    return pl.pallas_call(
        body, name=name, grid=(r // tr,), in_specs=[spec(a, i) for a, i in terms],
        out_specs=pl.BlockSpec((tr, c), lambda i: (i, 0)), out_shape=jax.ShapeDtypeStruct((r, c), F32),
        compiler_params=_params(("parallel",)),
    )(*[a for a, _ in terms])


def _adamw_math(w, g, m, v):
    m2 = ADAM_B1 * m + (1.0 - ADAM_B1) * g
    v2 = ADAM_B2 * v + (1.0 - ADAM_B2) * (g * g)
    m_hat = m2 / (1.0 - ADAM_B1 ** ADAM_STEP)
    v_hat = v2 / (1.0 - ADAM_B2 ** ADAM_STEP)
    return -ADAM_LR * (m_hat / (jnp.sqrt(v_hat) + ADAM_EPS) + ADAM_WD * w), m2, v2


def _adamw(name, w, g, m, v, bm, bn, g_transposed=False, exchange=None):
    r, c = w.shape
    grid = (pl.cdiv(r, bm), pl.cdiv(c, bn))

    def body(*refs):
        w_ref, g_ref, m_ref, v_ref = refs[:4]
        go_ref, d_ref, mo_ref, vo_ref = refs[-6:-2] if exchange else refs[4:8]
        if exchange:
            riders = (refs[4], refs[-7], refs[-2], refs[-1])
            first = (pl.program_id(0) == 0) & (pl.program_id(1) == 0)

            @pl.when(first)
            def _():
                _exchange8_start(*riders)

        g = g_ref[...].T if g_transposed else g_ref[...]
        d, m2, v2 = _adamw_math(w_ref[...], g, m_ref[...], v_ref[...])
        go_ref[...] = g
        d_ref[...] = d
        mo_ref[...] = m2
        vo_ref[...] = v2
        if exchange:
            @pl.when((pl.program_id(0) == grid[0] - 1) & (pl.program_id(1) == grid[1] - 1))
            def _():
                _exchange8_finish(*riders)

    blk = pl.BlockSpec((bm, bn), lambda i, j: (i, j))
    g_blk = pl.BlockSpec((bn, bm), lambda i, j: (j, i)) if g_transposed else blk
    out = jax.ShapeDtypeStruct((r, c), F32)
    if not exchange:
        return pl.pallas_call(
            body, name=name, grid=grid, in_specs=[blk, g_blk, blk, blk],
            out_specs=[blk] * 4, out_shape=[out] * 4, compiler_params=_params(("parallel", "parallel")),
        )(w, g, m, v)
    parts, landing = exchange
    outs = pl.pallas_call(
        body, name=name, grid=grid, in_specs=[blk, g_blk, blk, blk, ANY, ANY],
        out_specs=[ANY] + [blk] * 4, out_shape=[jax.ShapeDtypeStruct(landing.shape, landing.dtype)] + [out] * 4,
        input_output_aliases={5: 0}, scratch_shapes=[pltpu.SemaphoreType.DMA((7,)), pltpu.SemaphoreType.DMA((7,))],
        compiler_params=_params(("arbitrary", "arbitrary")),
    )(w, g, m, v, parts, landing)
    return outs[1:], outs[0]


ANY = pl.BlockSpec(memory_space=pl.ANY)


def _place():
    x, y, c = lax.axis_index("x"), lax.axis_index("y"), lax.axis_index("c")
    chips = [(1 - x, y), (x, 1 - y), (1 - x, 1 - y)]
    return x, y, c, chips


def _sibling():
    return (lax.axis_index("x"), lax.axis_index("y"), 1 - lax.axis_index("c"))


def _wait_all(local, remote):
    for cp in local:
        cp.wait()
    for cp in remote:
        cp.wait_send()


def _landing(shard):
    return lax.empty((N_SHARD * shard.shape[0], shard.shape[1]), shard.dtype)


def _gather_copies(ins, outs, send_sems, recv_sems, only_first=False):
    x, y, c, chips = _place()
    me, sibling = (x, y, c), _sibling()

    def rows(k, px, py, pc):
        h = ins[k].shape[0] // 2
        return outs[k].at[pl.ds((2 * px + py) * 2 * h + pc * h, h), :]

    def copy(k, j, block, to, src=None):
        return pltpu.make_async_remote_copy(
            src_ref=rows(k, *block) if src is None else src, dst_ref=rows(k, *block),
            send_sem=send_sems.at[k, j], recv_sem=recv_sems.at[k, j], device_id=to, device_id_type=MESH)

    each = [(k, j, chip) for k in range(len(ins)) for j, chip in enumerate(chips)]
    half = lambda k: ins[k].at[pl.ds(c * (ins[k].shape[0] // 2), ins[k].shape[0] // 2), :]
    first = [copy(k, j, me, (*chip, c), src=half(k)) for k, j, chip in each]
    own = [pltpu.make_async_remote_copy(
        src_ref=ins[k], dst_ref=outs[k].at[pl.ds((2 * x + y) * ins[k].shape[0], ins[k].shape[0]), :],
        send_sem=send_sems.at[k, 6], recv_sem=recv_sems.at[k, 6], device_id=sibling, device_id_type=MESH)
        for k in range(len(ins))]
    first = first + own
    if only_first:
        return first
    arrive = [copy(k, j, (*chip, c), me) for k, j, chip in each]
    passed = [copy(k, 3 + j, (*chip, c), sibling) for k, j, chip in each]
    landed = [copy(k, 3 + j, (*chip, 1 - c), me) for k, j, chip in each] + own
    return first, arrive, passed, landed


def _gather_start(ins, outs, send_sems, recv_sems):
    for cp in _gather_copies(ins, outs, send_sems, recv_sems, only_first=True):
        cp.start()


def _gather_finish(ins, outs, send_sems, recv_sems):
    first, arrive, passed, landed = _gather_copies(ins, outs, send_sems, recv_sems)
    for arrival, forward in zip(arrive, passed):
        arrival.wait_recv()
        forward.start()
    for cp in landed:
        cp.wait_recv()
    _wait_all([], first + passed)


def _gather_shards(shards, placed):
    n = len(shards)

    def body(*refs):
        ins, outs = refs[:n], refs[2 * n:3 * n]
        _gather_start(ins, outs, *refs[3 * n:])
        _gather_finish(ins, outs, *refs[3 * n:])

    return pl.pallas_call(
        body, name="gather_shards", in_specs=[ANY] * (2 * n), out_specs=[ANY] * n,
        out_shape=[jax.ShapeDtypeStruct(a.shape, a.dtype) for a in placed],
        input_output_aliases={n + k: k for k in range(n)},
        scratch_shapes=[pltpu.SemaphoreType.DMA((n, 7)), pltpu.SemaphoreType.DMA((n, 7))],
    )(*shards, *placed)


def _exchange8_copies(in_ref, out_ref, send_sems, recv_sems, only_sends=False):
    x, y, c, _ = _place()
    sends, arrivals = [], []
    for rel in range(1, 8):
        dx, dy, dc = rel >> 2 & 1, rel >> 1 & 1, rel & 1
        sends.append(pltpu.make_async_remote_copy(
            src_ref=in_ref.at[2 * (x ^ dx) + (y ^ dy)], dst_ref=out_ref.at[4 * x + 2 * y + c],
            send_sem=send_sems.at[rel - 1], recv_sem=recv_sems.at[rel - 1],
            device_id=(x ^ dx, y ^ dy, c ^ dc), device_id_type=MESH))
        if not only_sends:
            arrivals.append(pltpu.make_async_remote_copy(
                src_ref=in_ref.at[0], dst_ref=out_ref.at[4 * (x ^ dx) + 2 * (y ^ dy) + (c ^ dc)],
                send_sem=send_sems.at[rel - 1], recv_sem=recv_sems.at[rel - 1],
                device_id=(x, y, c), device_id_type=MESH))
    return sends, arrivals


def _exchange8_start(in_ref, out_ref, send_sems, recv_sems):
    for cp in _exchange8_copies(in_ref, out_ref, send_sems, recv_sems, only_sends=True)[0]:
        cp.start()


def _exchange8_finish(in_ref, out_ref, send_sems, recv_sems):
    sends, arrivals = _exchange8_copies(in_ref, out_ref, send_sems, recv_sems)
    for cp in arrivals:
        cp.wait_recv()
    _wait_all([], sends)


def _rs_sibling_copies(ins, outs, send_sems, recv_sems):
    c = lax.axis_index("c")
    return [pltpu.make_async_remote_copy(
        src_ref=ins[k].at[2 * s + 1 - c], dst_ref=outs[k].at[s], send_sem=send_sems.at[k, s], recv_sem=recv_sems.at[k, s],
        device_id=_sibling(), device_id_type=MESH) for k in range(len(ins)) for s in range(N_SHARD)]


def _rs_sibling_start(ins, outs, send_sems, recv_sems):
    for cp in _rs_sibling_copies(ins, outs, send_sems, recv_sems):
        cp.start()


def _rs_sibling_finish(ins, outs, send_sems, recv_sems):
    sends = _rs_sibling_copies(ins, outs, send_sems, recv_sems)
    for cp in sends:
        cp.wait_recv()
    _wait_all([], sends)


def _rs_sibling(grads):
    n = len(grads)

    def body(*refs):
        _rs_sibling_start(refs[:n], refs[n:2 * n], *refs[2 * n:])
        _rs_sibling_finish(refs[:n], refs[n:2 * n], *refs[2 * n:])

    return pl.pallas_call(
        body, name="rs_sibling", in_specs=[ANY] * n, out_specs=[ANY] * n,
        out_shape=[jax.ShapeDtypeStruct((N_SHARD,) + a.shape[1:], a.dtype) for a in grads],
        scratch_shapes=[pltpu.SemaphoreType.DMA((n, N_SHARD)), pltpu.SemaphoreType.DMA((n, N_SHARD))],
    )(*grads)


def _rs_chips_copies(ins, outs, send_sems, recv_sems):
    x, y, c, chips = _place()
    return [pltpu.make_async_remote_copy(
        src_ref=ins[k].at[2 * px + py], dst_ref=outs[k].at[j], send_sem=send_sems.at[k, j], recv_sem=recv_sems.at[k, j],
        device_id=(px, py, c), device_id_type=MESH) for k in range(len(ins)) for j, (px, py) in enumerate(chips)]


def _rs_chips_start(ins, outs, send_sems, recv_sems):
    for cp in _rs_chips_copies(ins, outs, send_sems, recv_sems):
        cp.start()


def _rs_chips_finish(ins, outs, send_sems, recv_sems):
    sends = _rs_chips_copies(ins, outs, send_sems, recv_sems)
    for cp in sends:
        cp.wait_recv()
    _wait_all([], sends)


def _rs_finish(bufs):
    n = len(bufs)

    def body(*refs):
        outs = refs[n:2 * n]
        send_sems, recv_sems = refs[2 * n:]
        c = lax.axis_index("c")
        sends = []
        for k in range(n):
            cp = pltpu.make_async_remote_copy(
                src_ref=outs[k].at[c], dst_ref=outs[k].at[c], send_sem=send_sems.at[k], recv_sem=recv_sems.at[k],
                device_id=_sibling(), device_id_type=MESH)
            cp.start()
            sends.append(cp)
        for k in range(n):
            pltpu.make_async_remote_copy(
                src_ref=outs[k].at[c], dst_ref=outs[k].at[1 - c], send_sem=send_sems.at[k], recv_sem=recv_sems.at[k],
                device_id=_sibling(), device_id_type=MESH).wait_recv()
        _wait_all([], sends)

    return pl.pallas_call(
        body, name="rs_finish", in_specs=[ANY] * n, out_specs=[ANY] * n,
        out_shape=[jax.ShapeDtypeStruct(a.shape, a.dtype) for a in bufs], input_output_aliases={k: k for k in range(n)},
        scratch_shapes=[pltpu.SemaphoreType.DMA((n,)), pltpu.SemaphoreType.DMA((n,))],
    )(*bufs)


def _sum3d(name, terms, scalars, grid_lead, out_lead, out_index, tr=None, out_dtype=F32):
    h, c = terms[0][0].shape[1:]
    tr = tr or h

    def body(s_ref, *refs):
        acc = refs[0][...].astype(F32)
        for ref in refs[1:-1]:
            acc = acc + ref[...].astype(F32)
        refs[-1][...] = acc.astype(refs[-1].dtype)

    in_specs = [pl.BlockSpec((1, tr, c), functools.partial(lambda l, i, s_ref, f: (f(l, s_ref), i, 0), f=f)) for _, f in terms]
    out_spec = pl.BlockSpec((1, tr, c), lambda l, i, s_ref: (out_index(l, s_ref), i, 0))
    return pl.pallas_call(
        body, name=name,
        grid_spec=pltpu.PrefetchScalarGridSpec(num_scalar_prefetch=1, grid=(grid_lead, h // tr), in_specs=in_specs, out_specs=out_spec),
        out_shape=jax.ShapeDtypeStruct((out_lead, h, c), out_dtype), compiler_params=_params(("arbitrary", "arbitrary")),
    )(scalars, *[a for a, _ in terms])


def _halves(grads):
    return [a.reshape(2 * N_SHARD, a.shape[0] // (2 * N_SHARD), a.shape[1]) for a in grads]


def _rs_stage1(g8, place, tag, from_sibling=None):
    from_sibling = _rs_sibling(g8) if from_sibling is None else from_sibling
    parts = [_sum3d(f"rs_add1_{tag}{k}", [(g8[k], lambda l, s: 2 * l + s[0]), (from_sibling[k], lambda l, s: l)], place,
                    N_SHARD, N_SHARD, lambda l, s: l, tr=g8[k].shape[1] // 2, out_dtype=BF16)
             for k in range(len(g8))]
    return g8, from_sibling, parts


def _rs_stage3(stage1, from_chips, place, tag):
    g8, from_sibling, _ = stage1
    mine = [(lambda l, s: 2 * s[1] + s[0]), (lambda l, s: s[1])]
    bufs = [_sum3d(f"rs_add2_{tag}{k}", [(g8[k], mine[0]), (from_sibling[k], mine[1])]
                   + [(from_chips[k], functools.partial(lambda l, s, j: j, j=j)) for j in range(3)],
                   place, 1, 2, lambda l, s: s[0], tr=g8[k].shape[1] // 2)
            for k in range(len(g8))]
    whole = _rs_finish(bufs)
    return [w.reshape(2 * w.shape[1], w.shape[2]) for w in whole]


_LATE = ["w_a", "w_b", "w_out", "w_down", "w_up_t"]


def _device_step(x, target, p, late_shards, late_placed, place):
    t = x.shape[0]
    d = D_MODEL
    tok = lambda arr, c=0, w=d: (arr, w, c)
    f32x = lambda n: [(d, F32)] * n
    rp_params = [p["w0"], p["w2p"], p["a0"], p["a2p"], p["g2p"], p["k_k"], p["k_a"]]
    post_params = [p["ln_w"], p["ln_b"], p["r_k"]]
    g = {}

    (xn,) = _tok_fwd("norm1_fwd", _fn_norm, [tok(x)], [p["g1"]], [(d, BF16)], tm=LIGHT_TM)
    z = _mm("in_proj", xn, p["w_in_t"], "nt", tm=t, tn=256)
    q_in, k_in, kd, qe, dec = _tok_fwd("hgates_fwd", _fn_hgates, [tok(z, 0), tok(z, 1)], [p["lb2"]], f32x(5))
    o_intra, u = _hgrn_local_fwd(q_in, k_in, kd, z)
    o_raw, h_states = _hgrn_state_fwd(o_intra, qe, dec, u)
    zl = _lerp_fwd(z, p["mu"])
    lora = tok(zl, 3 * d // LORA, LORA)
    decay, kr2, avec, bvec, gate = _tok_fwd("rprep_fwd", _fn_rprep, [tok(zl, 1), lora], rp_params, f32x(5))
    y, r_states, late = _rwkv_fwd(zl, decay, kr2, avec, bvec, late_shards, late_placed)
    p = dict(p, **dict(zip(_LATE, late)))
    (o_a,) = _tok_fwd("hpost_fwd", _fn_hpost, [tok(o_raw), tok(z, 3)], [p["gnorm"]], [(d, BF16)])
    post_toks = [tok(y), tok(zl, 0), tok(kr2), tok(zl, 2), tok(gate)]
    (o_b,) = _tok_fwd("rpost_fwd", _fn_rpost, post_toks, post_params, [(d, BF16)])
    y_a = _mm("branch_a", o_a, p["w_a"], "nn")
    y_b = _mm("branch_b", o_b, p["w_b"], "nn")
    merge_toks = [tok(z, C_G // 256, 256), tok(z, (C_G + d) // 256, 256), tok(y_a, 0, 256), tok(y_b, 0, 256)]
    (merged,) = _tok_fwd("merge_fwd", _fn_merge, merge_toks, [], [(256, BF16)], col_grid=4, tm=512)
    mix = _mm("out_proj", merged, p["w_out"], "nn")
    h1, xn2 = _tok_fwd("res1_fwd", _fn_res1, [tok(x), tok(mix)], [p["g_post1"], p["g_pre2"]], [(d, F32), (d, BF16)],
                       tm=LIGHT_TM)
    hu = _mm("up_proj", xn2, p["w_up_t"], "nt", tm=t, tn=512)
    act = _conv_fwd(hu, p["conv_w"], p["conv_b"])
    ff = _mm("down_proj", act, p["w_down"], "nn")
    d_h1, d_ff, loss, g["g_post2"] = _loss_head(h1, ff, target, p["g_post2"])

    d_act = _mm("d_act", d_ff, p["w_down"], "nt")
    g["w_down"] = _mm("dw_down", act, d_ff, "tn", tm=256, tn=1024)
    d_hu, d_cw, d_cb = _conv_bwd(hu, p["conv_w"], p["conv_b"], d_act)
    g["conv_w"], g["conv_b"] = d_cw.transpose(1, 0, 2).reshape(3, 2 * D_FF), d_cb.reshape(1, 2 * D_FF)
    d_hu = d_hu.reshape(2 * t, D_FF)
    d_xn2 = _mm("d_xn2", d_hu, p["w_up_t"], "nn", tm=t, tk=D_FF, mk=(t, 2 * D_FF), a_map=lambda i, j, q: (q, 0))
    g["w_up_t"] = _mm("dw_up", d_hu, xn2, "tn", tm=CONV_TILE, tn=1024, mk=(2 * D_FF, t),
                      a_map=lambda i, j, q: (i // N_CONV_TILES, i % N_CONV_TILES))
    d_x_res, d_mix, g["g_post1"], g["g_pre2"] = _tok_bwd(
        "res1_bwd", _fn_res1, [tok(x), tok(mix)], [p["g_post1"], p["g_pre2"]], [[tok(d_h1)], [tok(d_xn2)]],
        [(d, F32), (d, BF16)], tm=LIGHT_TM)
    d_merged = _mm("d_merged", d_mix, p["w_out"], "nt")
    g["w_out"] = _mm("dw_out", merged, d_mix, "tn")
    d_ga, d_gb, d_ya, d_yb = _tok_bwd("merge_bwd", _fn_merge, merge_toks, [], [[tok(d_merged, 0, 256)]],
                                      [(256, BF16)] * 4, col_grid=4, tm=512)
    d_oa = _mm("d_oa", d_ya, p["w_a"], "nt")
    g["w_a"] = _mm("dw_a", o_a, d_ya, "tn")
    d_ob = _mm("d_ob", d_yb, p["w_b"], "nt")
    g["w_b"] = _mm("dw_b", o_b, d_yb, "tn")
    d_oraw, d_hg, g["gnorm"] = _tok_bwd("hpost_bwd", _fn_hpost, [tok(o_raw), tok(z, 3)], [p["gnorm"]], [[tok(d_oa)]],
                                        [(d, F32), (d, BF16)])
    late_g8 = _halves([g[n] for n in _LATE])
    rpost = _tok_bwd("rpost_bwd", _fn_rpost, post_toks, post_params, [[tok(d_ob)]], f32x(5), sibling_rider=late_g8)
    d_y, d_r1, d_kr2_1, d_v1, d_gate, g["ln_w"], g["ln_b"], g["r_k"] = rpost[:8]
    stage1 = _rs_stage1(late_g8, place, "late", from_sibling=rpost[8:])
    (d_r2, d_decay, d_kr2_2, d_v2, d_avec, d_bvec), from_chips = _rwkv_bwd(
        zl, decay, kr2, avec, bvec, r_states, d_y, stage1[2])
    g.update(zip(_LATE, _rs_stage3(stage1, from_chips, place, "late")))
    prep = _tok_bwd("rprep_bwd", _fn_rprep, [tok(zl, 1), lora], rp_params,
                    [[tok(d_decay)], [tok(d_kr2_1), tok(d_kr2_2)], [tok(d_avec)], [tok(d_bvec)], [tok(d_gate)]],
                    [(d, F32), (LORA, F32)])
    d_kr, d_lora = prep[:2]
    g["w0"], g["w2p"], g["a0"], g["a2p"], g["g2p"], g["k_k"], g["k_a"] = prep[2:]
    dz_r, g["mu"] = _lerp_bwd(z, p["mu"], (d_r1, d_r2), d_kr, (d_v1, d_v2), d_lora)
    d_qe, d_dec, d_u = _hgrn_state_bwd(d_oraw, qe, dec, h_states)
    d_q_in, d_k_in, d_kd, d_vi = _hgrn_local_bwd(q_in, k_in, kd, z, d_oraw, d_u)
    d_hq, d_hf, g["lb2"] = _tok_bwd("hgates_bwd", _fn_hgates, [tok(z, 0), tok(z, 1)], [p["lb2"]],
                                    [[tok(d_q_in)], [tok(d_k_in)], [tok(d_kd)], [tok(d_qe)], [tok(d_dec)]], [(d, BF16)] * 2)
    dz = jnp.concatenate([d_hq, d_hf, d_vi.astype(BF16), d_hg, dz_r, d_ga, d_gb], axis=1)
    stage1 = _rs_stage1(_halves([_mm("dw_in", dz, xn, "tn", tm=256, tn=1024)]), place, "w_in")
    d_xn, from_chips = _mm("d_xn", dz, p["w_in_t"], "nn", tm=1024, tn=512, tk=IN_COLS // 2, riders=stage1[2])
    (g["w_in_t"],) = _rs_stage3(stage1, from_chips, place, "w_in")
    grad_x, g["g1"] = _tok_bwd("norm1_bwd", _fn_norm, [tok(x)], [p["g1"]], [[tok(d_xn)]], [(d, F32)],
                               add_to_first=tok(d_x_res), tm=LIGHT_TM)
    return loss, grad_x, g


_WEIGHTS = ["attn_pre_norm", "w_in", "hgrn_lb", "hgrn_gnorm", "w_branch_a", "rwkv_mu", "rwkv_w0", "rwkv_w2", "rwkv_a0",
            "rwkv_a2", "rwkv_g2", "rwkv_k_k", "rwkv_k_a", "rwkv_r_k", "rwkv_ln_w", "rwkv_ln_b", "w_branch_b", "w_out",
            "attn_post_norm", "ffn_pre_norm", "w_up", "conv_w", "conv_b", "w_down", "ffn_post_norm"]
_REPLICATED = [("attn_pre_norm", "g1"), ("hgrn_lb", "lb2"), ("hgrn_gnorm", "gnorm"), ("rwkv_mu", "mu"), ("rwkv_w0", "w0"),
               ("rwkv_a0", "a0"), ("rwkv_k_k", "k_k"), ("rwkv_k_a", "k_a"), ("rwkv_r_k", "r_k"), ("rwkv_ln_w", "ln_w"),
               ("rwkv_ln_b", "ln_b"), ("attn_post_norm", "g_post1"), ("ffn_pre_norm", "g_pre2"), ("conv_b", "conv_b"),
               ("ffn_post_norm", "g_post2")]
SLAB_COLS = 1024


def _pack(arrays):
    pieces, total = [], 0
    for a in arrays:
        flat = a.reshape(-1)
        rows = -(-flat.shape[0] // SLAB_COLS)
        pieces.append(jnp.pad(flat, (0, rows * SLAB_COLS - flat.shape[0])).reshape(rows, SLAB_COLS))
        total += rows
    if total % 8:
        pieces.append(jnp.zeros((8 - total % 8, SLAB_COLS), F32))
    return jnp.concatenate(pieces, axis=0)


def _unpack(slab, shapes):
    out, at = [], 0
    for s in shapes:
        size = 1
        for dim in s:
            size *= dim
        rows = -(-size // SLAB_COLS)
        out.append(slab[at:at + rows].reshape(-1)[:size].reshape(s))
        at += rows
    return out


def kernel(x, attn_pre_norm, w_in, hgrn_lb, hgrn_gnorm, w_branch_a, rwkv_mu, rwkv_w0, rwkv_w2, rwkv_a0, rwkv_a2, rwkv_g2, rwkv_k_k, rwkv_k_a, rwkv_r_k, rwkv_ln_w, rwkv_ln_b, w_branch_b, w_out, attn_post_norm, ffn_pre_norm, w_up, conv_w, conv_b, w_down, ffn_post_norm, loss_target, m_attn_pre_norm, m_w_in, m_hgrn_lb, m_hgrn_gnorm, m_w_branch_a, m_rwkv_mu, m_rwkv_w0, m_rwkv_w2, m_rwkv_a0, m_rwkv_a2, m_rwkv_g2, m_rwkv_k_k, m_rwkv_k_a, m_rwkv_r_k, m_rwkv_ln_w, m_rwkv_ln_b, m_w_branch_b, m_w_out, m_attn_post_norm, m_ffn_pre_norm, m_w_up, m_conv_w, m_conv_b, m_w_down, m_ffn_post_norm, v_attn_pre_norm, v_w_in, v_hgrn_lb, v_hgrn_gnorm, v_w_branch_a, v_rwkv_mu, v_rwkv_w0, v_rwkv_w2, v_rwkv_a0, v_rwkv_a2, v_rwkv_g2, v_rwkv_k_k, v_rwkv_k_a, v_rwkv_r_k, v_rwkv_ln_w, v_rwkv_ln_b, v_w_branch_b, v_w_out, v_attn_post_norm, v_ffn_pre_norm, v_w_up, v_conv_w, v_conv_b, v_w_down, v_ffn_post_norm):
    given = dict(locals())
    w = {n: given[n] for n in _WEIGHTS}
    mom = {n: given["m_" + n] for n in _WEIGHTS}
    var = {n: given["v_" + n] for n in _WEIGHTS}
    shard = 2 * lax.axis_index("x") + lax.axis_index("y")
    place = jnp.stack([lax.axis_index("c"), shard]).astype(jnp.int32)
    row = lambda a: a.reshape(1, -1)
    lora_of = lambda d: jnp.concatenate([d["rwkv_w2"][0], d["rwkv_a2"][0], d["rwkv_g2"][0]], axis=0)

    shards = [w["w_in"][0].T.astype(BF16), lora_of(w), jnp.pad(w["conv_w"][0], ((0, 29), (0, 0)))]
    late_shards = [w["w_branch_a"][0].astype(BF16), w["w_branch_b"][0].astype(BF16), w["w_out"][0].astype(BF16),
                   w["w_down"][0].astype(BF16), w["w_up"][0].T.astype(BF16)]
    placed = [_landing(a) for a in shards]
    late_placed = [_landing(a) for a in late_shards]
    w_in_t, lora_g, conv_g = _gather_shards(shards, placed)
    lora_full = lora_g.reshape(N_SHARD, LORA, 256).transpose(1, 0, 2).reshape(LORA, D_MODEL)
    conv_full = conv_g.reshape(N_SHARD, 32, 2 * D_FF // N_SHARD)[:, :3].transpose(1, 0, 2).reshape(3, 2 * D_FF)
    lrow = lax.broadcasted_iota(jnp.int32, (LORA, 1), 0)
    p = {
        "g1": row(w["attn_pre_norm"]), "lb2": w["hgrn_lb"], "gnorm": row(w["hgrn_gnorm"]), "w_in_t": w_in_t,
        "mu": row(w["rwkv_mu"]), "w0": row(w["rwkv_w0"]), "a0": row(w["rwkv_a0"]),
        "w2p": jnp.where(lrow < 64, lora_full, 0.0), "a2p": jnp.where((lrow >= 64) & (lrow < 128), lora_full, 0.0),
        "g2p": jnp.where(lrow >= 128, lora_full, 0.0),
        "k_k": row(w["rwkv_k_k"]), "k_a": row(w["rwkv_k_a"]), "r_k": row(w["rwkv_r_k"]), "ln_w": row(w["rwkv_ln_w"]),
        "ln_b": row(w["rwkv_ln_b"]), "g_post1": row(w["attn_post_norm"]),
        "g_pre2": row(w["ffn_pre_norm"]), "conv_w": conv_full, "conv_b": row(w["conv_b"]),
        "g_post2": row(w["ffn_post_norm"]),
    }

    loss, grad_x, g = _device_step(x[0], loss_target[0], p, late_shards, late_placed, place)

    g_in_t = g["w_in_t"]
    g_a, g_b, g_o, g_dn, g_up_t = [g[n] for n in _LATE]
    rep_shapes = [w[n].shape for n, _ in _REPLICATED]
    rep = _pack([g[key] for _, key in _REPLICATED])
    n_rep_rows = rep.shape[0]
    cw = 2 * D_FF // N_SHARD
    lora_rows, conv_rows = LORA * 256 // SLAB_COLS, -(-3 * cw // SLAB_COLS)
    lora_g = jnp.concatenate([g["w2p"][0:64], g["a2p"][64:128], g["g2p"][128:256]], axis=0)
    lora_parts = lora_g.reshape(LORA, N_SHARD, 256).transpose(1, 0, 2).reshape(N_SHARD, lora_rows, SLAB_COLS)
    conv_parts = g["conv_w"].reshape(3, N_SHARD, cw).transpose(1, 0, 2).reshape(N_SHARD, 3 * cw)
    conv_parts = jnp.pad(conv_parts, ((0, 0), (0, conv_rows * SLAB_COLS - 3 * cw))).reshape(N_SHARD, conv_rows, SLAB_COLS)
    n_rows = n_rep_rows + lora_rows + conv_rows
    fill = jnp.zeros((N_SHARD, -n_rows % 8, SLAB_COLS), F32)
    parts = jnp.concatenate([jnp.broadcast_to(rep, (N_SHARD,) + rep.shape), lora_parts, conv_parts, fill], axis=1)
    me = 4 * lax.axis_index("x") + 2 * lax.axis_index("y") + lax.axis_index("c")
    landing = lax.dynamic_update_slice(jnp.zeros((8,) + parts.shape[1:], F32),
                                       lax.dynamic_index_in_dim(parts, shard, 0, keepdims=True), (me, 0, 0))

    res = {}

    def put(name, outs, shape=None):
        res[name] = [o.reshape(w[name].shape if shape is None else shape) for o in outs]

    w_in_out, gathered = _adamw("adamw_w_in", w["w_in"][0], g_in_t, mom["w_in"][0], var["w_in"][0], 1024, 128,
                                g_transposed=True, exchange=(parts, landing))
    put("w_in", w_in_out)
    summed = _sum3d("small_sum", [(gathered, functools.partial(lambda l, s, i: i, i=i)) for i in range(8)], place, 1, 1,
                    lambda l, s: 0)[0]
    lora_grad = summed[n_rep_rows:n_rep_rows + lora_rows].reshape(LORA, 256)
    conv_grad = summed[n_rep_rows + lora_rows:n_rows].reshape(-1)[:3 * cw].reshape(3, cw)
    put("w_up", _adamw("adamw_w_up", w["w_up"][0], g_up_t, mom["w_up"][0], var["w_up"][0], 1024, 128, g_transposed=True))
    for name, grad in (("w_branch_a", g_a), ("w_branch_b", g_b), ("w_out", g_o)):
        put(name, _adamw("adamw_" + name, w[name][0], grad, mom[name][0], var[name][0], 256, 1024))
    put("w_down", _adamw("adamw_w_down", w["w_down"][0], g_dn, mom["w_down"][0], var["w_down"][0], 176, 1024))
    put("conv_w", _adamw("adamw_conv_w", w["conv_w"][0], conv_grad, mom["conv_w"][0], var["conv_w"][0], 3, 2 * D_FF // N_SHARD))
    lora_out = _adamw("adamw_lora", lora_of(w), lora_grad, lora_of(mom), lora_of(var), LORA, 256)
    for name, lo, hi in (("rwkv_w2", 0, 64), ("rwkv_a2", 64, 128), ("rwkv_g2", 128, 256)):
        put(name, [o[lo:hi] for o in lora_out])
    rep_names = [n for n, _ in _REPLICATED]
    rep_out = _adamw("adamw_small", _pack([w[n] for n in rep_names]), summed[:n_rep_rows], _pack([mom[n] for n in rep_names]),
                     _pack([var[n] for n in rep_names]), n_rep_rows, SLAB_COLS)
    for name, parts in zip(rep_names, zip(*[_unpack(o, rep_shapes) for o in rep_out])):
        put(name, list(parts))

    loss = lax.psum(loss[0, 0], ("x", "y", "c"))
    return (loss, grad_x[None], *[res[n][0] for n in _WEIGHTS], *[res[n][1] for n in _WEIGHTS],
            *[res[n][2] for n in _WEIGHTS], *[res[n][3] for n in _WEIGHTS])
```

```python
import functools

import jax
import jax.numpy as jnp
from jax import lax
from jax.experimental import pallas as pl
from jax.experimental.pallas import tpu as pltpu

F32, BF16 = jnp.float32, jnp.bfloat16
MESH = pl.DeviceIdType.MESH

D_MODEL = 1024
HGRN_HEADS = 8
HGRN_K = 128
HGRN_SCALE = HGRN_K ** -0.5
CHUNK = 32
RWKV_HEAD = 64
LORA = 256
D_FF = 2816
EPS = 1e-6
GN_EPS = 1e-5 * RWKV_HEAD
N_SHARD = 4
ADAM_LR, ADAM_B1, ADAM_B2, ADAM_EPS, ADAM_WD, ADAM_STEP = 0.001, 0.9, 0.999, 1e-08, 0.01, 10

LANES = 128
VMEM_LIMIT = 56 * 1024 * 1024
SCAN_TB = 32
SCAN_TB_FWD = 64
SCAN_GROUP = 256
LIGHT_TM = 256

C_HQ, C_HF, C_HI, C_HG = 0, 1024, 2048, 3072
C_R = 4096
R_COLS = 3328
C_G = 7424
IN_COLS = 9472


def _params(sem=None, **kw):
    return pltpu.CompilerParams(dimension_semantics=sem, vmem_limit_bytes=VMEM_LIMIT, **kw)


def _seg_matrix(n, seg):
    r = lax.broadcasted_iota(jnp.int32, (n, n), 0) // seg
    c = lax.broadcasted_iota(jnp.int32, (n, n), 1) // seg
    return (r == c).astype(BF16)


def _split3(x):
    hi = x.astype(BF16)
    r1 = x - hi.astype(F32)
    mid = r1.astype(BF16)
    lo = (r1 - mid.astype(F32)).astype(BF16)
    return hi, mid, lo


def _segsum_impl(x, seg):
    e = _seg_matrix(LANES, seg)
    outs = []
    for g in range(x.shape[1] // LANES):
        hi, mid, lo = _split3(x[:, g * LANES:(g + 1) * LANES])
        outs.append(jnp.dot(hi, e, preferred_element_type=F32) + jnp.dot(mid, e, preferred_element_type=F32)
                    + jnp.dot(lo, e, preferred_element_type=F32))
    return outs[0] if len(outs) == 1 else jnp.concatenate(outs, axis=1)


def _make_segsum(seg):
    @jax.custom_vjp
    def f(x):
        return _segsum_impl(x, seg)

    f.defvjp(lambda x: (_segsum_impl(x, seg), None), lambda _, ct: (_segsum_impl(ct, seg),))
    return f


_segsum64 = _make_segsum(RWKV_HEAD)
_segsum128 = _make_segsum(HGRN_K)


def _chunk_mm_impl(x, kind, transposed):
    n = x.shape[0]
    r = lax.broadcasted_iota(jnp.int32, (n, n), 1 if transposed else 0)
    c = lax.broadcasted_iota(jnp.int32, (n, n), 0 if transposed else 1)
    same = (r // CHUNK) == (c // CHUNK)
    if kind == "cumsum":
        m = same & (r >= c)
    else:
        m = same & (c % CHUNK == (CHUNK // 2 - 1 if kind == "mid" else CHUNK - 1))
    m = m.astype(BF16)
    hi, mid, lo = _split3(x)
    return (jnp.dot(m, hi, preferred_element_type=F32) + jnp.dot(m, mid, preferred_element_type=F32)
            + jnp.dot(m, lo, preferred_element_type=F32))


def _make_chunk_mm(kind):
    @jax.custom_vjp
    def f(x):
        return _chunk_mm_impl(x, kind, False)

    f.defvjp(lambda x: (_chunk_mm_impl(x, kind, False), None), lambda _, ct: (_chunk_mm_impl(ct, kind, True),))
    return f


_chunk_cumsum = _make_chunk_mm("cumsum")
_chunk_mid = _make_chunk_mm("mid")
_chunk_last = _make_chunk_mm("last")


@jax.custom_vjp
def _bdot(x, w):
    return jnp.dot(x.astype(BF16), w.astype(BF16), preferred_element_type=F32)


def _bdot_fwd(x, w):
    return _bdot(x, w), (x, w)


def _bdot_bwd(res, ct):
    x, w = res
    ctb = ct.astype(BF16)
    dx = lax.dot_general(ctb, w.astype(BF16), (((1,), (1,)), ((), ())), preferred_element_type=F32)
    dw = lax.dot_general(x.astype(BF16), ctb, (((0,), (0,)), ((), ())), preferred_element_type=F32)
    return dx, dw


_bdot.defvjp(_bdot_fwd, _bdot_bwd)


def _sigmoid(x):
    return 1.0 / (1.0 + jnp.exp(-x))


def _silu(x):
    return x * _sigmoid(x)


def _softplus(x):
    return jnp.maximum(x, 0.0) + jnp.log(1.0 + jnp.exp(-jnp.abs(x)))


def _rms(x, g):
    return x * lax.rsqrt(jnp.mean(x * x, axis=-1, keepdims=True) + EPS) * g


def _fn_norm(t, p):
    return [_rms(t[0], p[0])]


def _fn_hgates(t, p):
    hq, hf = t
    lb2 = p[0]
    m = jnp.max(lb2, axis=0, keepdims=True)
    e = jnp.exp(lb2 - m)
    first = lax.broadcasted_iota(jnp.int32, e.shape, 0) == 0
    lb = jnp.sum(jnp.where(first, e, 0.0), axis=0, keepdims=True) / jnp.sum(e, axis=0, keepdims=True)
    f = lb + (1.0 - lb) * _sigmoid(hf)
    q, k = _silu(hq) * HGRN_SCALE, 1.0 - f
    b = _chunk_cumsum(jnp.log(f))
    b_ref, b_last = _chunk_mid(b), _chunk_last(b)
    return [q * jnp.exp(b - b_ref), k * jnp.exp(b_ref - b), k * jnp.exp(b_last - b), q * jnp.exp(b), jnp.exp(b_last)]


def _fn_hpost(t, p):
    o, hg = t
    ms = _segsum128(o * o) * (1.0 / HGRN_K)
    return [o * lax.rsqrt(ms + EPS) * p[0] * _silu(hg)]


def _fn_rprep(t, p):
    kr, lora = t
    w0, w2p, a0, a2p, g2p, k_k, k_a = p
    pre_w = w0 + _bdot(jnp.tanh(lora), w2p)
    w_log = -_softplus(-pre_w) - 0.5
    decay = jnp.exp(-jnp.exp(w_log))
    a = _sigmoid(a0 + _bdot(lora, a2p))
    g = _bdot(_sigmoid(lora), g2p)
    kk = kr * k_k
    kk = kk / jnp.maximum(jnp.sqrt(_segsum64(kk * kk)), 1e-12)
    kr2 = kr * (1.0 + (a - 1.0) * k_a)
    return [decay, kr2, -kk, kk * a, g]


def _fn_rpost(t, p):
    y, r, kr2, v, g = t
    ln_w, ln_b, r_k = p
    mu = _segsum64(y) * (1.0 / RWKV_HEAD)
    yc = y - mu
    var = _segsum64(yc * yc) * (1.0 / RWKV_HEAD)
    yn = yc * lax.rsqrt(var + GN_EPS) * ln_w + ln_b
    bonus = _segsum64(r * kr2 * r_k) * v
    return [(yn + bonus) * g]


def _fn_merge(t, p):
    ga, gb, ya, yb = t
    return [_sigmoid(ga) * ya + _sigmoid(gb) * yb]


def _fn_res1(t, p):
    x, mix = t
    h1 = x + _rms(mix, p[0])
    return [h1, _rms(h1, p[1])]


def _tok_call(name, fn, toks, params, outs, red_shapes=(), tm=128, col_grid=1, sibling_rider=None):
    n_t, n_p, n_o, n_red = len(toks), len(params), len(outs), len(red_shapes)
    n_r = 0 if sibling_rider is None else len(sibling_rider)
    t_len = toks[0][0].shape[0]
    tm = min(tm, t_len)
    grid = (t_len // tm, col_grid)

    def body(*refs):
        first_out = n_t + n_p + n_r
        if n_r:
            rider = (refs[n_t + n_p:first_out], refs[first_out + n_o + n_red:first_out + n_o + n_red + n_r], *refs[-2:])

            @pl.when((pl.program_id(0) == 0) & (pl.program_id(1) == 0))
            def _():
                _rs_sibling_start(*rider)

        tv = [r[...].astype(F32) for r in refs[:n_t]]
        pv = [r[...] for r in refs[n_t:n_t + n_p]]
        o, red = fn(tv, pv)
        for ref, val in zip(refs[first_out:first_out + n_o], o):
            ref[...] = val.astype(ref.dtype)
        if n_r:
            @pl.when((pl.program_id(0) == grid[0] - 1) & (pl.program_id(1) == grid[1] - 1))
            def _():
                _rs_sibling_finish(*rider)

        red_refs = refs[first_out + n_o:first_out + n_o + n_red]
        if red_refs:
            first = pl.program_id(0) == 0

            @pl.when(first)
            def _():
                for ref, val in zip(red_refs, red):
                    ref[...] = val

            @pl.when(jnp.logical_not(first))
            def _():
                for ref, val in zip(red_refs, red):
                    ref[...] += val

    in_specs = [pl.BlockSpec((tm, w), functools.partial(lambda i, j, c: (i, c + j), c=c)) for (_, w, c) in toks]
    in_specs += [pl.BlockSpec(p.shape, lambda i, j: (0, 0)) for p in params]
    out_specs = [pl.BlockSpec((tm, w), lambda i, j: (i, j)) for (w, _) in outs]
    out_specs += [pl.BlockSpec(s, lambda i, j: (0, 0)) for s in red_shapes]
    out_shape = [jax.ShapeDtypeStruct((t_len, w * col_grid), dt) for (w, dt) in outs]
    out_shape += [jax.ShapeDtypeStruct(s, F32) for s in red_shapes]
    scratch = []
    if n_r:
        in_specs += [ANY] * n_r
        out_specs += [ANY] * n_r
        out_shape += [jax.ShapeDtypeStruct((N_SHARD,) + a.shape[1:], a.dtype) for a in sibling_rider]
        scratch = [pltpu.SemaphoreType.DMA((n_r, N_SHARD)), pltpu.SemaphoreType.DMA((n_r, N_SHARD))]
    return pl.pallas_call(
        body, name=name, grid=grid, in_specs=in_specs, out_specs=out_specs, out_shape=out_shape, scratch_shapes=scratch,
        compiler_params=_params(("arbitrary", "arbitrary")),
    )(*[a for (a, _, _) in toks], *params, *(sibling_rider or []))


def _tok_fwd(name, fn, toks, params, outs, **kw):
    return _tok_call(name, lambda tv, pv: (fn(tv, pv), []), toks, params, outs, **kw)


def _tok_bwd(name, fn, toks, params, cts, want, add_to_first=None, **kw):
    n_t = len(toks)
    flat = [c for group in cts for c in group]
    extra = [] if add_to_first is None else [add_to_first]

    def bwd(tv, pv):
        prim, rest = tv[:n_t], tv[n_t:]
        ct, at = [], 0
        for group in cts:
            ct.append(functools.reduce(lambda u, v: u + v, rest[at:at + len(group)]))
            at += len(group)
        _, vjp = jax.vjp(lambda *a: tuple(fn(list(a[:n_t]), list(a[n_t:]))), *prim, *pv)
        g = vjp(tuple(ct))
        tok_grads = [g[i] for i in range(n_t) if want[i] is not None]
        if extra:
            tok_grads[0] = tok_grads[0] + rest[at]
        return tok_grads, list(g[n_t:])

    return _tok_call(name, bwd, list(toks) + flat + extra, params, [w for w in want if w is not None],
                     red_shapes=[p.shape for p in params], **kw)


def _mm(name, a, b, mode, out_dtype=F32, tm=None, tn=None, tk=None, riders=None, a_map=None, mk=None):
    if mode == "nn":
        (m, k), (_, n) = a.shape, b.shape
    elif mode == "nt":
        (m, k), (n, _) = a.shape, b.shape
    else:
        (k, m), (_, n) = a.shape, b.shape
    if mk is not None:
        m, k = mk
    tm = (1024 if mode == "tn" else 2048) if tm is None else tm
    tn = (512 if mode == "tn" else 256) if tn is None else tn
    tk = k if tk is None else tk
    tm, tn = min(tm, m), min(tn, n)
    nk = k // tk
    assert m % tm == 0 and n % tn == 0 and k % tk == 0, (name, a.shape, b.shape, tm, tn, tk)
    a_spec = pl.BlockSpec((tk, tm), lambda i, j, q: (q, i)) if mode == "tn" else pl.BlockSpec((tm, tk), lambda i, j, q: (i, q))
    if a_map is not None:
        a_spec = pl.BlockSpec(a_spec.block_shape, a_map)
    b_spec = pl.BlockSpec((tn, tk), lambda i, j, q: (j, q)) if mode == "nt" else pl.BlockSpec((tk, tn), lambda i, j, q: (q, j))
    dn = {"nn": (((1,), (0,)), ((), ())), "nt": (((1,), (1,)), ((), ())), "tn": (((0,), (0,)), ((), ()))}[mode]
    grid = (m // tm, n // tn, nk)
    nr = 0 if riders is None else len(riders)

    def body(*refs):
        a_ref, b_ref, o_ref = refs[0], refs[1], refs[2 + nr]
        acc = refs[3 + 2 * nr] if nk > 1 else None
        if nr:
            exchange = (refs[2:2 + nr], refs[3 + nr:3 + 2 * nr], *refs[-2:])
            at = [pl.program_id(ax) for ax in range(3)]

            @pl.when((at[0] == 0) & (at[1] == 0) & (at[2] == 0))
            def _():
                _rs_chips_start(*exchange)

        p = lax.dot_general(a_ref[...], b_ref[...], dn, preferred_element_type=F32)
        if nk == 1:
            o_ref[...] = p.astype(o_ref.dtype)
        else:
            q = pl.program_id(2)

            @pl.when(q == 0)
            def _():
                acc[...] = p

            @pl.when(q > 0)
            def _():
                acc[...] += p

            @pl.when(q == nk - 1)
            def _():
                o_ref[...] = acc[...].astype(o_ref.dtype)

        if nr:
            @pl.when((at[0] == grid[0] - 1) & (at[1] == grid[1] - 1) & (at[2] == grid[2] - 1))
            def _():
                _rs_chips_finish(*exchange)

    scratch = [pltpu.VMEM((tm, tn), F32)] if nk > 1 else []
    out_specs = [pl.BlockSpec((tm, tn), lambda i, j, q: (i, j))]
    out_shape = [jax.ShapeDtypeStruct((m, n), out_dtype)]
    if nr:
        scratch += [pltpu.SemaphoreType.DMA((nr, 3)), pltpu.SemaphoreType.DMA((nr, 3))]
        out_specs += [ANY] * nr
        out_shape += [jax.ShapeDtypeStruct((3,) + r.shape[1:], r.dtype) for r in riders]
    outs = pl.pallas_call(
        body, name=name, grid=grid, in_specs=[a_spec, b_spec] + [ANY] * nr, out_specs=out_specs, out_shape=out_shape,
        scratch_shapes=scratch,
        compiler_params=_params(("arbitrary",) * 3 if nr else ("parallel", "parallel", "arbitrary")),
    )(a, b, *(riders or []))
    return (outs[0], outs[1:]) if nr else outs[0]


def _shift_down(z, n):
    rows = lax.broadcasted_iota(jnp.int32, z.shape, 0)
    return jnp.where(rows < n, 0.0, pltpu.roll(z, n, 0))


def _shift_up(z, n):
    t = z.shape[0]
    rows = lax.broadcasted_iota(jnp.int32, z.shape, 0)
    return jnp.where(rows >= t - n, 0.0, pltpu.roll(z, t - n, 0))


def _lerp_fwd(z, mu):
    t = z.shape[0]
    w = 256

    def body(z_ref, mu_ref, o_ref):
        zz = z_ref[...]
        o_ref[...] = zz + mu_ref[...] * (_shift_down(zz, 1) - zz)

    return pl.pallas_call(
        body, name="lerp_fwd", grid=(R_COLS // w,),
        in_specs=[pl.BlockSpec((t, w), lambda j: (0, C_R // w + j)), pl.BlockSpec((1, w), lambda j: (0, j))],
        out_specs=pl.BlockSpec((t, w), lambda j: (0, j)), out_shape=jax.ShapeDtypeStruct((t, R_COLS), F32),
        compiler_params=_params(("parallel",)),
    )(z, mu)


def _lerp_bwd(z, mu, d_r, d_k, d_v, d_lora):
    t = z.shape[0]
    w = 256
    per = D_MODEL // w

    def body(z_ref, mu_ref, r1_ref, r2_ref, k_ref, v1_ref, v2_ref, l_ref, dz_ref, dmu_ref):
        j = pl.program_id(0)
        zz, m = z_ref[...], mu_ref[...]
        d = jnp.where(j < per, r1_ref[...] + r2_ref[...],
                      jnp.where(j < 2 * per, k_ref[...], jnp.where(j < 3 * per, v1_ref[...] + v2_ref[...], l_ref[...])))
        dz_ref[...] = (d * (1.0 - m) + _shift_up(d * m, 1)).astype(dz_ref.dtype)
        dmu_ref[...] = jnp.sum(d * (_shift_down(zz, 1) - zz), axis=0, keepdims=True)

    piece = lambda first: pl.BlockSpec((t, w), lambda j: (0, jnp.clip(j - first, 0, per - 1)))
    return pl.pallas_call(
        body, name="lerp_bwd", grid=(R_COLS // w,),
        in_specs=[pl.BlockSpec((t, w), lambda j: (0, C_R // w + j)), pl.BlockSpec((1, w), lambda j: (0, j)),
                  piece(0), piece(0), piece(per), piece(2 * per), piece(2 * per), pl.BlockSpec((t, w), lambda j: (0, 0))],
        out_specs=[pl.BlockSpec((t, w), lambda j: (0, j)), pl.BlockSpec((1, w), lambda j: (0, j))],
        out_shape=[jax.ShapeDtypeStruct((t, R_COLS), BF16), jax.ShapeDtypeStruct((1, R_COLS), F32)],
        compiler_params=_params(("arbitrary",)),
    )(z, mu, *d_r, d_k, *d_v, d_lora)


CONV_TILE = 256
N_CONV_TILES = D_FF // CONV_TILE


def _conv(h, w, b):
    return b + w[0:1, :] * _shift_down(h, 2) + w[1:2, :] * _shift_down(h, 1) + w[2:3, :] * h


def _conv_fwd(hu, conv_w, conv_b):
    t = hu.shape[0]
    n = N_CONV_TILES

    def body(hg_ref, hv_ref, wg_ref, wv_ref, bg_ref, bv_ref, o_ref):
        gate = _conv(hg_ref[...], wg_ref[...], bg_ref[...])
        val = _conv(hv_ref[...], wv_ref[...], bv_ref[...])
        o_ref[...] = (_silu(gate) * val).astype(o_ref.dtype)

    col = lambda off: pl.BlockSpec((t, CONV_TILE), lambda j: (0, j + off))
    wspec = lambda off: pl.BlockSpec((3, CONV_TILE), lambda j: (0, j + off))
    bspec = lambda off: pl.BlockSpec((1, CONV_TILE), lambda j: (0, j + off))
    return pl.pallas_call(
        body, name="conv_fwd", grid=(n,),
        in_specs=[col(0), col(n), wspec(0), wspec(n), bspec(0), bspec(n)],
        out_specs=pl.BlockSpec((t, CONV_TILE), lambda j: (0, j)), out_shape=jax.ShapeDtypeStruct((t, D_FF), BF16),
        compiler_params=_params(("parallel",)),
    )(hu, hu, conv_w, conv_w, conv_b, conv_b)


def _conv_bwd(hu, conv_w, conv_b, d_act):
    t = hu.shape[0]
    n = N_CONV_TILES

    def body(hg_ref, hv_ref, wg_ref, wv_ref, bg_ref, bv_ref, d_ref, dh_ref, dw_ref, db_ref):
        hg, hv, wg, wv = hg_ref[...], hv_ref[...], wg_ref[...], wv_ref[...]
        gate = _conv(hg, wg, bg_ref[...])
        val = _conv(hv, wv, bv_ref[...])
        d = d_ref[...]
        sg = _sigmoid(gate)
        d_gate = d * val * (sg * (1.0 + gate * (1.0 - sg)))
        d_val = d * (gate * sg)
        for half, (dc, h, w) in enumerate(((d_gate, hg, wg), (d_val, hv, wv))):
            dh = w[2:3, :] * dc + w[1:2, :] * _shift_up(dc, 1) + w[0:1, :] * _shift_up(dc, 2)
            dh_ref[half] = dh.astype(dh_ref.dtype)
            dw_ref[half, 0:1, :] = jnp.sum(dc * _shift_down(h, 2), axis=0, keepdims=True)
            dw_ref[half, 1:2, :] = jnp.sum(dc * _shift_down(h, 1), axis=0, keepdims=True)
            dw_ref[half, 2:3, :] = jnp.sum(dc * h, axis=0, keepdims=True)
            db_ref[half] = jnp.sum(dc, axis=0, keepdims=True)

    gcol = lambda rows: pl.BlockSpec((rows, CONV_TILE), lambda j: (0, j))
    vcol = lambda rows: pl.BlockSpec((rows, CONV_TILE), lambda j: (0, j + n))
    both = lambda rows: pl.BlockSpec((2, rows, CONV_TILE), lambda j: (0, 0, j))
    return pl.pallas_call(
        body, name="conv_bwd", grid=(n,),
        in_specs=[gcol(t), vcol(t), gcol(3), vcol(3), gcol(1), vcol(1), gcol(t)],
        out_specs=[both(t), both(3), both(1)],
        out_shape=[jax.ShapeDtypeStruct((2, t, D_FF), BF16), jax.ShapeDtypeStruct((2, 3, D_FF), F32),
                   jax.ShapeDtypeStruct((2, 1, D_FF), F32)],
        compiler_params=_params(("parallel",)),
    )(hu, hu, conv_w, conv_w, conv_b, conv_b, d_act)


_NN = (((1,), (0,)), ((), ()))
_NT = (((1,), (1,)), ((), ()))
_TN = (((0,), (0,)), ((), ()))
HGRN_CB = 8
HGRN_LOCAL_CB = 16


def _bf_dot(a, b, dn):
    return lax.dot_general(a.astype(BF16), b.astype(BF16), dn, preferred_element_type=F32)


def _tril():
    n = HGRN_LOCAL_CB * CHUNK
    r = lax.broadcasted_iota(jnp.int32, (n, n), 0)
    c = lax.broadcasted_iota(jnp.int32, (n, n), 1)
    return (r // CHUNK == c // CHUNK) & (r >= c)


def _hgrn_specs(t):
    rows = HGRN_LOCAL_CB * CHUNK
    head = pl.BlockSpec((rows, HGRN_K), lambda h, n: (n, h))
    v_head = pl.BlockSpec((rows, HGRN_K), lambda h, n: (n, C_HI // HGRN_K + h))
    mats = pl.BlockSpec((1, HGRN_LOCAL_CB, HGRN_K, HGRN_K), lambda h, n: (h, n, 0, 0))
    return head, v_head, mats, (HGRN_HEADS, t // rows)


def _hgrn_local_fwd(q_in, k_in, kd, z):
    t = q_in.shape[0]
    head, v_head, mats, grid = _hgrn_specs(t)

    def body(q_ref, k_ref, kd_ref, v_ref, o_ref, u_ref):
        v = v_ref[...]
        scores = jnp.where(_tril(), _bf_dot(q_ref[...], k_ref[...], _NT), 0.0)
        o_ref[...] = _bf_dot(scores, v, _NN)
        for n in range(HGRN_LOCAL_CB):
            rows = slice(n * CHUNK, (n + 1) * CHUNK)
            u_ref[0, n] = _bf_dot(v[rows], kd_ref[rows, :], _TN)

    return pl.pallas_call(
        body, name="hgrn_local_fwd", grid=grid, in_specs=[head, head, head, v_head], out_specs=[head, mats],
        out_shape=[jax.ShapeDtypeStruct((t, D_MODEL), F32),
                   jax.ShapeDtypeStruct((HGRN_HEADS, t // CHUNK, HGRN_K, HGRN_K), F32)],
        compiler_params=_params(("parallel", "parallel")),
    )(q_in, k_in, kd, z)


def _hgrn_state_specs(t, reverse=False):
    rows = HGRN_CB * CHUNK
    nb = t // rows
    at = (lambda n: nb - 1 - n) if reverse else (lambda n: n)
    tok = pl.BlockSpec((rows, D_MODEL), lambda n: (at(n), 0))
    mats = pl.BlockSpec((HGRN_HEADS, HGRN_CB, HGRN_K, HGRN_K), lambda n: (0, at(n), 0, 0))
    return tok, mats, nb


def _hgrn_state_fwd(o_intra, qe, dec, u):
    t = qe.shape[0]
    tok, mats, nb = _hgrn_state_specs(t)

    def body(oi_ref, qe_ref, dec_ref, u_ref, o_ref, st_ref, s_ref):
        @pl.when(pl.program_id(0) == 0)
        def _():
            s_ref[...] = jnp.zeros_like(s_ref)

        st = [s_ref[h] for h in range(HGRN_HEADS)]
        for n in range(HGRN_CB):
            rows = slice(n * CHUNK, (n + 1) * CHUNK)
            for h in range(HGRN_HEADS):
                cols = slice(h * HGRN_K, (h + 1) * HGRN_K)
                st_ref[h, n] = st[h]
                o_ref[rows, cols] = oi_ref[rows, cols] + _bf_dot(qe_ref[rows, cols], st[h], _NT)
                st[h] = st[h] * dec_ref[n * CHUNK:n * CHUNK + 1, cols] + u_ref[h, n]
        for h in range(HGRN_HEADS):
            s_ref[h] = st[h]

    return pl.pallas_call(
        body, name="hgrn_state_fwd", grid=(nb,), in_specs=[tok, tok, tok, mats], out_specs=[tok, mats],
        out_shape=[jax.ShapeDtypeStruct((t, D_MODEL), F32),
                   jax.ShapeDtypeStruct((HGRN_HEADS, t // CHUNK, HGRN_K, HGRN_K), F32)],
        scratch_shapes=[pltpu.VMEM((HGRN_HEADS, HGRN_K, HGRN_K), F32)],
        compiler_params=_params(("arbitrary",)),
    )(o_intra, qe, dec, u)


def _hgrn_state_bwd(d_o, qe, dec, states):
    t = qe.shape[0]
    tok_r, mats_r, nb = _hgrn_state_specs(t, reverse=True)

    def body(do_ref, qe_ref, dec_ref, st_ref, dqe_ref, ddec_ref, du_ref, d_ref):
        @pl.when(pl.program_id(0) == 0)
        def _():
            d_ref[...] = jnp.zeros_like(d_ref)

        first_row = lax.broadcasted_iota(jnp.int32, (CHUNK, HGRN_K), 0) == 0
        d = [d_ref[h] for h in range(HGRN_HEADS)]
        for n in reversed(range(HGRN_CB)):
            rows = slice(n * CHUNK, (n + 1) * CHUNK)
            for h in range(HGRN_HEADS):
                cols = slice(h * HGRN_K, (h + 1) * HGRN_K)
                st, do = st_ref[h, n], do_ref[rows, cols]
                du_ref[h, n] = d[h]
                ddec_ref[rows, cols] = jnp.where(first_row, jnp.sum(d[h] * st, axis=0, keepdims=True), 0.0)
                dqe_ref[rows, cols] = _bf_dot(do, st, _NN)
                d[h] = d[h] * dec_ref[n * CHUNK:n * CHUNK + 1, cols] + _bf_dot(do, qe_ref[rows, cols], _TN)
        for h in range(HGRN_HEADS):
            d_ref[h] = d[h]

    out = jax.ShapeDtypeStruct((t, D_MODEL), F32)
    return pl.pallas_call(
        body, name="hgrn_state_bwd", grid=(nb,), in_specs=[tok_r, tok_r, tok_r, mats_r], out_specs=[tok_r, tok_r, mats_r],
        out_shape=[out, out, jax.ShapeDtypeStruct((HGRN_HEADS, t // CHUNK, HGRN_K, HGRN_K), F32)],
        scratch_shapes=[pltpu.VMEM((HGRN_HEADS, HGRN_K, HGRN_K), F32)],
        compiler_params=_params(("arbitrary",)),
    )(d_o, qe, dec, states)


def _hgrn_local_bwd(q_in, k_in, kd, z, d_o, d_u):
    t = q_in.shape[0]
    head, v_head, mats, grid = _hgrn_specs(t)

    def body(q_ref, k_ref, kd_ref, v_ref, do_ref, du_ref, dq_ref, dk_ref, dkd_ref, dv_ref):
        tril = _tril()
        q, k, v, do = q_ref[...], k_ref[...], v_ref[...], do_ref[...]
        scores = jnp.where(tril, _bf_dot(q, k, _NT), 0.0)
        d_scores = jnp.where(tril, _bf_dot(do, v, _NT), 0.0)
        dq_ref[...] = _bf_dot(d_scores, k, _NN)
        dk_ref[...] = _bf_dot(d_scores, q, _TN)
        dv = _bf_dot(scores, do, _TN)
        for n in range(HGRN_LOCAL_CB):
            rows = slice(n * CHUNK, (n + 1) * CHUNK)
            du = du_ref[0, n]
            dv_ref[rows, :] = dv[rows] + _bf_dot(kd_ref[rows, :], du, _NT)
            dkd_ref[rows, :] = _bf_dot(v[rows], du, _NN)

    out = jax.ShapeDtypeStruct((t, D_MODEL), F32)
    return pl.pallas_call(
        body, name="hgrn_local_bwd", grid=grid, in_specs=[head, head, head, v_head, head, mats], out_specs=[head] * 4,
        out_shape=[out] * 4, compiler_params=_params(("parallel", "parallel")),
    )(q_in, k_in, kd, z, d_o, d_u)


def _seg_bcast(xs, e):
    n = RWKV_HEAD
    lhs = [x.astype(BF16) for x in xs]
    out = jnp.dot(lhs[0] if len(lhs) == 1 else jnp.concatenate(lhs, axis=0), e, preferred_element_type=F32)
    return [out[i * n:(i + 1) * n] for i in range(len(xs))]


def _rows_to_cols(rows, diag, e):
    zero = jnp.zeros((), BF16)
    parts = [jnp.where(diag, row.astype(BF16), zero) for row in rows]
    out = jnp.dot(jnp.concatenate(parts, axis=0), e, preferred_element_type=F32)
    n = RWKV_HEAD
    return [out[i * n:(i + 1) * n] for i in range(len(rows))]


def _col_to_row(col, diag):
    return jnp.sum(jnp.where(diag, col, 0.0), axis=0, keepdims=True)


_SCAN_PAIRS = ((0, 1, 2, 3),)


def _scan_consts():
    e = _seg_matrix(SCAN_GROUP, RWKV_HEAD)
    i = lax.broadcasted_iota(jnp.int32, (RWKV_HEAD, SCAN_GROUP), 0)
    l = lax.broadcasted_iota(jnp.int32, (RWKV_HEAD, SCAN_GROUP), 1)
    groups = [slice(g * SCAN_GROUP, (g + 1) * SCAN_GROUP) for g in range(D_MODEL // SCAN_GROUP)]
    return e, (l % RWKV_HEAD) == i, groups


def _rwkv_fwd(zl, w, k, a, b, shards, placed):
    t = zl.shape[0]
    tb = SCAN_TB_FWD
    nb = t // tb
    n = len(shards)
    steps = range(tb)

    def body(*refs):
        scan(*refs[:6], *refs[6 + 2 * n:8 + 2 * n], refs[8 + 3 * n])
        gather = (refs[6:6 + n], refs[8 + 2 * n:8 + 3 * n], *refs[9 + 3 * n:])

        @pl.when(pl.program_id(0) == 0)
        def _():
            _gather_start(*gather)

        @pl.when(pl.program_id(0) == nb - 1)
        def _():
            _gather_finish(*gather)

    def scan(r_ref, w_ref, k_ref, v_ref, a_ref, b_ref, y_ref, st_ref, s_ref):
        @pl.when(pl.program_id(0) == 0)
        def _():
            s_ref[...] = jnp.zeros_like(s_ref)

        e, diag, groups = _scan_consts()
        v_cols = [_rows_to_cols([v_ref[i:i + 1, sl] for i in steps], diag, e) for sl in groups]
        s = [s_ref[:, sl] for sl in groups]
        for i in steps:
            for pair in _SCAN_PAIRS:
                sas = _seg_bcast([s[g] * a_ref[i:i + 1, groups[g]] for g in pair], e)
                for g, sa in zip(pair, sas):
                    sl = groups[g]
                    s[g] = s[g] * w_ref[i:i + 1, sl] + sa * b_ref[i:i + 1, sl] + v_cols[g][i] * k_ref[i:i + 1, sl]
                    st_ref[i, :, sl] = s[g]
        for g, sl in enumerate(groups):
            s_ref[:, sl] = s[g]
            y_cols = _seg_bcast([st_ref[i, :, sl] * r_ref[i:i + 1, sl] for i in steps], e)
            for i in steps:
                y_ref[i:i + 1, sl] = _col_to_row(y_cols[i], diag)

    blk = pl.BlockSpec((tb, D_MODEL), lambda n: (n, 0))
    v_blk = pl.BlockSpec((tb, D_MODEL), lambda n: (n, 2))
    outs = pl.pallas_call(
        body, name="rwkv_fwd", grid=(nb,), in_specs=[blk, blk, blk, v_blk, blk, blk] + [ANY] * (2 * n),
        out_specs=[blk, pl.BlockSpec((tb, RWKV_HEAD, D_MODEL), lambda i: (i, 0, 0))] + [ANY] * n,
        out_shape=[jax.ShapeDtypeStruct((t, D_MODEL), F32), jax.ShapeDtypeStruct((t, RWKV_HEAD, D_MODEL), F32)]
        + [jax.ShapeDtypeStruct(p.shape, p.dtype) for p in placed],
        input_output_aliases={6 + n + i: 2 + i for i in range(n)},
        scratch_shapes=[pltpu.VMEM((RWKV_HEAD, D_MODEL), F32), pltpu.SemaphoreType.DMA((n, 7)), pltpu.SemaphoreType.DMA((n, 7))],
        compiler_params=_params(("arbitrary",)),
    )(zl, w, k, zl, a, b, *shards, *placed)
    return outs[0], outs[1], outs[2:]


def _rwkv_bwd(zl, w, k, a, b, states, d_y, parts):
    t = zl.shape[0]
    nb = t // SCAN_TB
    n = len(parts)
    steps = range(SCAN_TB)

    def body(*refs):
        scan(*refs[:9], *refs[9 + n:15 + n], refs[15 + 2 * n])
        exchange = (refs[9:9 + n], refs[15 + n:15 + 2 * n], *refs[16 + 2 * n:])

        @pl.when(pl.program_id(0) == 0)
        def _():
            _rs_chips_start(*exchange)

        @pl.when(pl.program_id(0) == nb - 1)
        def _():
            _rs_chips_finish(*exchange)

    def scan(r_ref, w_ref, k_ref, v_ref, a_ref, b_ref, st_ref, prev_ref, dy_ref,
             dr_ref, dw_ref, dk_ref, dv_ref, da_ref, db_ref, ds_ref):
        @pl.when(pl.program_id(0) == 0)
        def _():
            ds_ref[...] = jnp.zeros_like(ds_ref)

        has_prev = (pl.program_id(0) < nb - 1).astype(F32)
        e, diag, groups = _scan_consts()
        colsum = lambda x: jnp.sum(x, axis=0, keepdims=True)

        def s_prev(i, sl):
            return st_ref[i - 1, :, sl] if i > 0 else prev_ref[0, :, sl] * has_prev

        dy_cols = [_rows_to_cols([dy_ref[i:i + 1, sl] for i in steps], diag, e) for sl in groups]
        v_cols = [_rows_to_cols([v_ref[i:i + 1, sl] for i in steps], diag, e) for sl in groups]
        sa_cols = [_seg_bcast([s_prev(i, sl) * a_ref[i:i + 1, sl] for i in steps], e) for sl in groups]
        ds = [ds_ref[:, sl] for sl in groups]
        dsk = [[None] * SCAN_TB for _ in groups]
        for i in reversed(steps):
            for pair in _SCAN_PAIRS:
                d = {}
                for g in pair:
                    sl = groups[g]
                    d[g] = ds[g] + dy_cols[g][i] * r_ref[i:i + 1, sl]
                    dr_ref[i:i + 1, sl] = colsum(st_ref[i, :, sl] * dy_cols[g][i])
                    dw_ref[i:i + 1, sl] = colsum(d[g] * s_prev(i, sl))
                    db_ref[i:i + 1, sl] = colsum(d[g] * sa_cols[g][i])
                    dk_ref[i:i + 1, sl] = colsum(d[g] * v_cols[g][i])
                    dsk[g][i] = d[g] * k_ref[i:i + 1, sl]
                dsas = _seg_bcast([d[g] * b_ref[i:i + 1, groups[g]] for g in pair], e)
                for g, dsa in zip(pair, dsas):
                    sl = groups[g]
                    da_ref[i:i + 1, sl] = colsum(s_prev(i, sl) * dsa)
                    ds[g] = d[g] * w_ref[i:i + 1, sl] + dsa * a_ref[i:i + 1, sl]
        for g, sl in enumerate(groups):
            ds_ref[:, sl] = ds[g]
            dv_cols = _seg_bcast(dsk[g], e)
            for i in steps:
                dv_ref[i:i + 1, sl] = _col_to_row(dv_cols[i], diag)

    blk = pl.BlockSpec((SCAN_TB, D_MODEL), lambda n: (nb - 1 - n, 0))
    v_blk = pl.BlockSpec((SCAN_TB, D_MODEL), lambda n: (nb - 1 - n, 2))
    out = jax.ShapeDtypeStruct((t, D_MODEL), F32)
    outs = pl.pallas_call(
        body, name="rwkv_bwd", grid=(nb,),
        in_specs=[blk, blk, blk, v_blk, blk, blk] + [
            pl.BlockSpec((SCAN_TB, RWKV_HEAD, D_MODEL), lambda i: (nb - 1 - i, 0, 0)),
            pl.BlockSpec((1, RWKV_HEAD, D_MODEL), lambda i: (jnp.maximum((nb - 1 - i) * SCAN_TB - 1, 0), 0, 0)),
            blk] + [ANY] * n,
        out_specs=[blk] * 6 + [ANY] * n,
        out_shape=[out] * 6 + [jax.ShapeDtypeStruct((3,) + p.shape[1:], p.dtype) for p in parts],
        scratch_shapes=[pltpu.VMEM((RWKV_HEAD, D_MODEL), F32), pltpu.SemaphoreType.DMA((n, 3)), pltpu.SemaphoreType.DMA((n, 3))],
        compiler_params=_params(("arbitrary",)),
    )(zl, w, k, zl, a, b, states, states, d_y, *parts)
    return outs[:6], outs[6:]


def _loss_head(h1, ff, target, g_post):
    def fn(tv, pv):
        a, f, tgt = tv
        h2, vjp = jax.vjp(lambda a_, f_, g_: a_ + _rms(f_, g_), a, f, pv[0])
        err = h2 - tgt
        loss = 0.5 * jnp.sum(jnp.mean(err * err, axis=-1, keepdims=True), axis=0, keepdims=True)
        d_a, d_f, d_g = vjp(err * (1.0 / D_MODEL))
        return [d_a, d_f], [loss, d_g]

    return _tok_call("loss_head", fn, [(h1, D_MODEL, 0), (ff, D_MODEL, 0), (target, D_MODEL, 0)], [g_post],
                     [(D_MODEL, F32), (D_MODEL, BF16)], red_shapes=[(1, 1), (1, D_MODEL)], tm=LIGHT_TM)


def _sum_call(name, terms, rows_per_block=None):
    a0, i0 = terms[0]
    r, c = a0.shape[-2:]
    tr = rows_per_block or r

    def body(*refs):
        acc = refs[0][...].reshape(tr, c)
        for ref in refs[1:-1]:
            acc = acc + ref[...].reshape(tr, c)
        refs[-1][...] = acc

    def spec(arr, idx):
        if arr.ndim == 2:
            return pl.BlockSpec((tr, c), lambda i: (i, 0))
        return pl.BlockSpec((1, tr, c), functools.partial(lambda i, idx: (idx, i, 0), idx=idx))

    return pl.pallas_call(
        body, name=name, grid=(r // tr,), in_specs=[spec(a, i) for a, i in terms],
        out_specs=pl.BlockSpec((tr, c), lambda i: (i, 0)), out_shape=jax.ShapeDtypeStruct((r, c), F32),
        compiler_params=_params(("parallel",)),
    )(*[a for a, _ in terms])


def _adamw_math(w, g, m, v):
    m2 = ADAM_B1 * m + (1.0 - ADAM_B1) * g
    v2 = ADAM_B2 * v + (1.0 - ADAM_B2) * (g * g)
    m_hat = m2 / (1.0 - ADAM_B1 ** ADAM_STEP)
    v_hat = v2 / (1.0 - ADAM_B2 ** ADAM_STEP)
    return -ADAM_LR * (m_hat / (jnp.sqrt(v_hat) + ADAM_EPS) + ADAM_WD * w), m2, v2


def _adamw(name, w, g, m, v, bm, bn, g_transposed=False, exchange=None):
    r, c = w.shape
    grid = (pl.cdiv(r, bm), pl.cdiv(c, bn))

    def body(*refs):
        w_ref, g_ref, m_ref, v_ref = refs[:4]
        go_ref, d_ref, mo_ref, vo_ref = refs[-6:-2] if exchange else refs[4:8]
        if exchange:
            riders = (refs[4], refs[-7], refs[-2], refs[-1])
            first = (pl.program_id(0) == 0) & (pl.program_id(1) == 0)

            @pl.when(first)
            def _():
                _exchange8_start(*riders)

        g = g_ref[...].T if g_transposed else g_ref[...]
        d, m2, v2 = _adamw_math(w_ref[...], g, m_ref[...], v_ref[...])
        go_ref[...] = g
        d_ref[...] = d
        mo_ref[...] = m2
        vo_ref[...] = v2
        if exchange:
            @pl.when((pl.program_id(0) == grid[0] - 1) & (pl.program_id(1) == grid[1] - 1))
            def _():
                _exchange8_finish(*riders)

    blk = pl.BlockSpec((bm, bn), lambda i, j: (i, j))
    g_blk = pl.BlockSpec((bn, bm), lambda i, j: (j, i)) if g_transposed else blk
    out = jax.ShapeDtypeStruct((r, c), F32)
    if not exchange:
        return pl.pallas_call(
            body, name=name, grid=grid, in_specs=[blk, g_blk, blk, blk],
            out_specs=[blk] * 4, out_shape=[out] * 4, compiler_params=_params(("parallel", "parallel")),
        )(w, g, m, v)
    parts, landing = exchange
    outs = pl.pallas_call(
        body, name=name, grid=grid, in_specs=[blk, g_blk, blk, blk, ANY, ANY],
        out_specs=[ANY] + [blk] * 4, out_shape=[jax.ShapeDtypeStruct(landing.shape, landing.dtype)] + [out] * 4,
        input_output_aliases={5: 0}, scratch_shapes=[pltpu.SemaphoreType.DMA((7,)), pltpu.SemaphoreType.DMA((7,))],
        compiler_params=_params(("arbitrary", "arbitrary")),
    )(w, g, m, v, parts, landing)
    return outs[1:], outs[0]


ANY = pl.BlockSpec(memory_space=pl.ANY)


def _place():
    x, y, c = lax.axis_index("x"), lax.axis_index("y"), lax.axis_index("c")
    chips = [(1 - x, y), (x, 1 - y), (1 - x, 1 - y)]
    return x, y, c, chips


def _sibling():
    return (lax.axis_index("x"), lax.axis_index("y"), 1 - lax.axis_index("c"))


def _wait_all(local, remote):
    for cp in local:
        cp.wait()
    for cp in remote:
        cp.wait_send()


def _landing(shard):
    return lax.empty((N_SHARD * shard.shape[0], shard.shape[1]), shard.dtype)


def _gather_copies(ins, outs, send_sems, recv_sems, only_first=False):
    x, y, c, chips = _place()
    me, sibling = (x, y, c), _sibling()

    def rows(k, px, py, pc):
        h = ins[k].shape[0] // 2
        return outs[k].at[pl.ds((2 * px + py) * 2 * h + pc * h, h), :]

    def copy(k, j, block, to, src=None):
        return pltpu.make_async_remote_copy(
            src_ref=rows(k, *block) if src is None else src, dst_ref=rows(k, *block),
            send_sem=send_sems.at[k, j], recv_sem=recv_sems.at[k, j], device_id=to, device_id_type=MESH)

    each = [(k, j, chip) for k in range(len(ins)) for j, chip in enumerate(chips)]
    half = lambda k: ins[k].at[pl.ds(c * (ins[k].shape[0] // 2), ins[k].shape[0] // 2), :]
    first = [copy(k, j, me, (*chip, c), src=half(k)) for k, j, chip in each]
    own = [pltpu.make_async_remote_copy(
        src_ref=ins[k], dst_ref=outs[k].at[pl.ds((2 * x + y) * ins[k].shape[0], ins[k].shape[0]), :],
        send_sem=send_sems.at[k, 6], recv_sem=recv_sems.at[k, 6], device_id=sibling, device_id_type=MESH)
        for k in range(len(ins))]
    first = first + own
    if only_first:
        return first
    arrive = [copy(k, j, (*chip, c), me) for k, j, chip in each]
    passed = [copy(k, 3 + j, (*chip, c), sibling) for k, j, chip in each]
    landed = [copy(k, 3 + j, (*chip, 1 - c), me) for k, j, chip in each] + own
    return first, arrive, passed, landed


def _gather_start(ins, outs, send_sems, recv_sems):
    for cp in _gather_copies(ins, outs, send_sems, recv_sems, only_first=True):
        cp.start()


def _gather_finish(ins, outs, send_sems, recv_sems):
    first, arrive, passed, landed = _gather_copies(ins, outs, send_sems, recv_sems)
    for arrival, forward in zip(arrive, passed):
        arrival.wait_recv()
        forward.start()
    for cp in landed:
        cp.wait_recv()
    _wait_all([], first + passed)


def _gather_shards(shards, placed):
    n = len(shards)

    def body(*refs):
        ins, outs = refs[:n], refs[2 * n:3 * n]
        _gather_start(ins, outs, *refs[3 * n:])
        _gather_finish(ins, outs, *refs[3 * n:])

    return pl.pallas_call(
        body, name="gather_shards", in_specs=[ANY] * (2 * n), out_specs=[ANY] * n,
        out_shape=[jax.ShapeDtypeStruct(a.shape, a.dtype) for a in placed],
        input_output_aliases={n + k: k for k in range(n)},
        scratch_shapes=[pltpu.SemaphoreType.DMA((n, 7)), pltpu.SemaphoreType.DMA((n, 7))],
    )(*shards, *placed)


def _exchange8_copies(in_ref, out_ref, send_sems, recv_sems, only_sends=False):
    x, y, c, _ = _place()
    sends, arrivals = [], []
    for rel in range(1, 8):
        dx, dy, dc = rel >> 2 & 1, rel >> 1 & 1, rel & 1
        sends.append(pltpu.make_async_remote_copy(
            src_ref=in_ref.at[2 * (x ^ dx) + (y ^ dy)], dst_ref=out_ref.at[4 * x + 2 * y + c],
            send_sem=send_sems.at[rel - 1], recv_sem=recv_sems.at[rel - 1],
            device_id=(x ^ dx, y ^ dy, c ^ dc), device_id_type=MESH))
        if not only_sends:
            arrivals.append(pltpu.make_async_remote_copy(
                src_ref=in_ref.at[0], dst_ref=out_ref.at[4 * (x ^ dx) + 2 * (y ^ dy) + (c ^ dc)],
                send_sem=send_sems.at[rel - 1], recv_sem=recv_sems.at[rel - 1],
                device_id=(x, y, c), device_id_type=MESH))
    return sends, arrivals


def _exchange8_start(in_ref, out_ref, send_sems, recv_sems):
    for cp in _exchange8_copies(in_ref, out_ref, send_sems, recv_sems, only_sends=True)[0]:
        cp.start()


def _exchange8_finish(in_ref, out_ref, send_sems, recv_sems):
    sends, arrivals = _exchange8_copies(in_ref, out_ref, send_sems, recv_sems)
    for cp in arrivals:
        cp.wait_recv()
    _wait_all([], sends)


def _rs_sibling_copies(ins, outs, send_sems, recv_sems):
    c = lax.axis_index("c")
    return [pltpu.make_async_remote_copy(
        src_ref=ins[k].at[2 * s + 1 - c], dst_ref=outs[k].at[s], send_sem=send_sems.at[k, s], recv_sem=recv_sems.at[k, s],
        device_id=_sibling(), device_id_type=MESH) for k in range(len(ins)) for s in range(N_SHARD)]


def _rs_sibling_start(ins, outs, send_sems, recv_sems):
    for cp in _rs_sibling_copies(ins, outs, send_sems, recv_sems):
        cp.start()


def _rs_sibling_finish(ins, outs, send_sems, recv_sems):
    sends = _rs_sibling_copies(ins, outs, send_sems, recv_sems)
    for cp in sends:
        cp.wait_recv()
    _wait_all([], sends)


def _rs_sibling(grads):
    n = len(grads)

    def body(*refs):
        _rs_sibling_start(refs[:n], refs[n:2 * n], *refs[2 * n:])
        _rs_sibling_finish(refs[:n], refs[n:2 * n], *refs[2 * n:])

    return pl.pallas_call(
        body, name="rs_sibling", in_specs=[ANY] * n, out_specs=[ANY] * n,
        out_shape=[jax.ShapeDtypeStruct((N_SHARD,) + a.shape[1:], a.dtype) for a in grads],
        scratch_shapes=[pltpu.SemaphoreType.DMA((n, N_SHARD)), pltpu.SemaphoreType.DMA((n, N_SHARD))],
    )(*grads)


def _rs_chips_copies(ins, outs, send_sems, recv_sems):
    x, y, c, chips = _place()
    return [pltpu.make_async_remote_copy(
        src_ref=ins[k].at[2 * px + py], dst_ref=outs[k].at[j], send_sem=send_sems.at[k, j], recv_sem=recv_sems.at[k, j],
        device_id=(px, py, c), device_id_type=MESH) for k in range(len(ins)) for j, (px, py) in enumerate(chips)]


def _rs_chips_start(ins, outs, send_sems, recv_sems):
    for cp in _rs_chips_copies(ins, outs, send_sems, recv_sems):
        cp.start()


def _rs_chips_finish(ins, outs, send_sems, recv_sems):
    sends = _rs_chips_copies(ins, outs, send_sems, recv_sems)
    for cp in sends:
        cp.wait_recv()
    _wait_all([], sends)


def _rs_finish(bufs):
    n = len(bufs)

    def body(*refs):
        outs = refs[n:2 * n]
        send_sems, recv_sems = refs[2 * n:]
        c = lax.axis_index("c")
        sends = []
        for k in range(n):
            cp = pltpu.make_async_remote_copy(
                src_ref=outs[k].at[c], dst_ref=outs[k].at[c], send_sem=send_sems.at[k], recv_sem=recv_sems.at[k],
                device_id=_sibling(), device_id_type=MESH)
            cp.start()
            sends.append(cp)
        for k in range(n):
            pltpu.make_async_remote_copy(
                src_ref=outs[k].at[c], dst_ref=outs[k].at[1 - c], send_sem=send_sems.at[k], recv_sem=recv_sems.at[k],
                device_id=_sibling(), device_id_type=MESH).wait_recv()
        _wait_all([], sends)

    return pl.pallas_call(
        body, name="rs_finish", in_specs=[ANY] * n, out_specs=[ANY] * n,
        out_shape=[jax.ShapeDtypeStruct(a.shape, a.dtype) for a in bufs], input_output_aliases={k: k for k in range(n)},
        scratch_shapes=[pltpu.SemaphoreType.DMA((n,)), pltpu.SemaphoreType.DMA((n,))],
    )(*bufs)


def _sum3d(name, terms, scalars, grid_lead, out_lead, out_index, tr=None, out_dtype=F32):
    h, c = terms[0][0].shape[1:]
    tr = tr or h

    def body(s_ref, *refs):
        acc = refs[0][...].astype(F32)
        for ref in refs[1:-1]:
            acc = acc + ref[...].astype(F32)
        refs[-1][...] = acc.astype(refs[-1].dtype)

    in_specs = [pl.BlockSpec((1, tr, c), functools.partial(lambda l, i, s_ref, f: (f(l, s_ref), i, 0), f=f)) for _, f in terms]
    out_spec = pl.BlockSpec((1, tr, c), lambda l, i, s_ref: (out_index(l, s_ref), i, 0))
    return pl.pallas_call(
        body, name=name,
        grid_spec=pltpu.PrefetchScalarGridSpec(num_scalar_prefetch=1, grid=(grid_lead, h // tr), in_specs=in_specs, out_specs=out_spec),
        out_shape=jax.ShapeDtypeStruct((out_lead, h, c), out_dtype), compiler_params=_params(("arbitrary", "arbitrary")),
    )(scalars, *[a for a, _ in terms])


def _halves(grads):
    return [a.reshape(2 * N_SHARD, a.shape[0] // (2 * N_SHARD), a.shape[1]) for a in grads]


def _rs_stage1(g8, place, tag, from_sibling=None):
    from_sibling = _rs_sibling(g8) if from_sibling is None else from_sibling
    parts = [_sum3d(f"rs_add1_{tag}{k}", [(g8[k], lambda l, s: 2 * l + s[0]), (from_sibling[k], lambda l, s: l)], place,
                    N_SHARD, N_SHARD, lambda l, s: l, tr=g8[k].shape[1] // 2, out_dtype=BF16)
             for k in range(len(g8))]
    return g8, from_sibling, parts


def _rs_stage3(stage1, from_chips, place, tag):
    g8, from_sibling, _ = stage1
    mine = [(lambda l, s: 2 * s[1] + s[0]), (lambda l, s: s[1])]
    bufs = [_sum3d(f"rs_add2_{tag}{k}", [(g8[k], mine[0]), (from_sibling[k], mine[1])]
                   + [(from_chips[k], functools.partial(lambda l, s, j: j, j=j)) for j in range(3)],
                   place, 1, 2, lambda l, s: s[0], tr=g8[k].shape[1] // 2)
            for k in range(len(g8))]
    whole = _rs_finish(bufs)
    return [w.reshape(2 * w.shape[1], w.shape[2]) for w in whole]


_LATE = ["w_a", "w_b", "w_out", "w_down", "w_up_t"]


def _device_step(x, target, p, late_shards, late_placed, place):
    t = x.shape[0]
    d = D_MODEL
    tok = lambda arr, c=0, w=d: (arr, w, c)
    f32x = lambda n: [(d, F32)] * n
    rp_params = [p["w0"], p["w2p"], p["a0"], p["a2p"], p["g2p"], p["k_k"], p["k_a"]]
    post_params = [p["ln_w"], p["ln_b"], p["r_k"]]
    g = {}

    (xn,) = _tok_fwd("norm1_fwd", _fn_norm, [tok(x)], [p["g1"]], [(d, BF16)], tm=LIGHT_TM)
    z = _mm("in_proj", xn, p["w_in_t"], "nt", tm=t, tn=256)
    q_in, k_in, kd, qe, dec = _tok_fwd("hgates_fwd", _fn_hgates, [tok(z, 0), tok(z, 1)], [p["lb2"]], f32x(5))
    o_intra, u = _hgrn_local_fwd(q_in, k_in, kd, z)
    o_raw, h_states = _hgrn_state_fwd(o_intra, qe, dec, u)
    zl = _lerp_fwd(z, p["mu"])
    lora = tok(zl, 3 * d // LORA, LORA)
    decay, kr2, avec, bvec, gate = _tok_fwd("rprep_fwd", _fn_rprep, [tok(zl, 1), lora], rp_params, f32x(5))
    y, r_states, late = _rwkv_fwd(zl, decay, kr2, avec, bvec, late_shards, late_placed)
    p = dict(p, **dict(zip(_LATE, late)))
    (o_a,) = _tok_fwd("hpost_fwd", _fn_hpost, [tok(o_raw), tok(z, 3)], [p["gnorm"]], [(d, BF16)])
    post_toks = [tok(y), tok(zl, 0), tok(kr2), tok(zl, 2), tok(gate)]
    (o_b,) = _tok_fwd("rpost_fwd", _fn_rpost, post_toks, post_params, [(d, BF16)])
    y_a = _mm("branch_a", o_a, p["w_a"], "nn")
    y_b = _mm("branch_b", o_b, p["w_b"], "nn")
    merge_toks = [tok(z, C_G // 256, 256), tok(z, (C_G + d) // 256, 256), tok(y_a, 0, 256), tok(y_b, 0, 256)]
    (merged,) = _tok_fwd("merge_fwd", _fn_merge, merge_toks, [], [(256, BF16)], col_grid=4, tm=512)
    mix = _mm("out_proj", merged, p["w_out"], "nn")
    h1, xn2 = _tok_fwd("res1_fwd", _fn_res1, [tok(x), tok(mix)], [p["g_post1"], p["g_pre2"]], [(d, F32), (d, BF16)],
                       tm=LIGHT_TM)
    hu = _mm("up_proj", xn2, p["w_up_t"], "nt", tm=t, tn=512)
    act = _conv_fwd(hu, p["conv_w"], p["conv_b"])
    ff = _mm("down_proj", act, p["w_down"], "nn")
    d_h1, d_ff, loss, g["g_post2"] = _loss_head(h1, ff, target, p["g_post2"])

    d_act = _mm("d_act", d_ff, p["w_down"], "nt")
    g["w_down"] = _mm("dw_down", act, d_ff, "tn", tm=256, tn=1024)
    d_hu, d_cw, d_cb = _conv_bwd(hu, p["conv_w"], p["conv_b"], d_act)
    g["conv_w"], g["conv_b"] = d_cw.transpose(1, 0, 2).reshape(3, 2 * D_FF), d_cb.reshape(1, 2 * D_FF)
    d_hu = d_hu.reshape(2 * t, D_FF)
    d_xn2 = _mm("d_xn2", d_hu, p["w_up_t"], "nn", tm=t, tk=D_FF, mk=(t, 2 * D_FF), a_map=lambda i, j, q: (q, 0))
    g["w_up_t"] = _mm("dw_up", d_hu, xn2, "tn", tm=CONV_TILE, tn=1024, mk=(2 * D_FF, t),
                      a_map=lambda i, j, q: (i // N_CONV_TILES, i % N_CONV_TILES))
    d_x_res, d_mix, g["g_post1"], g["g_pre2"] = _tok_bwd(
        "res1_bwd", _fn_res1, [tok(x), tok(mix)], [p["g_post1"], p["g_pre2"]], [[tok(d_h1)], [tok(d_xn2)]],
        [(d, F32), (d, BF16)], tm=LIGHT_TM)
    d_merged = _mm("d_merged", d_mix, p["w_out"], "nt")
    g["w_out"] = _mm("dw_out", merged, d_mix, "tn")
    d_ga, d_gb, d_ya, d_yb = _tok_bwd("merge_bwd", _fn_merge, merge_toks, [], [[tok(d_merged, 0, 256)]],
                                      [(256, BF16)] * 4, col_grid=4, tm=512)
    d_oa = _mm("d_oa", d_ya, p["w_a"], "nt")
    g["w_a"] = _mm("dw_a", o_a, d_ya, "tn")
    d_ob = _mm("d_ob", d_yb, p["w_b"], "nt")
    g["w_b"] = _mm("dw_b", o_b, d_yb, "tn")
    d_oraw, d_hg, g["gnorm"] = _tok_bwd("hpost_bwd", _fn_hpost, [tok(o_raw), tok(z, 3)], [p["gnorm"]], [[tok(d_oa)]],
                                        [(d, F32), (d, BF16)])
    late_g8 = _halves([g[n] for n in _LATE])
    rpost = _tok_bwd("rpost_bwd", _fn_rpost, post_toks, post_params, [[tok(d_ob)]], f32x(5), sibling_rider=late_g8)
    d_y, d_r1, d_kr2_1, d_v1, d_gate, g["ln_w"], g["ln_b"], g["r_k"] = rpost[:8]
    stage1 = _rs_stage1(late_g8, place, "late", from_sibling=rpost[8:])
    (d_r2, d_decay, d_kr2_2, d_v2, d_avec, d_bvec), from_chips = _rwkv_bwd(
        zl, decay, kr2, avec, bvec, r_states, d_y, stage1[2])
    g.update(zip(_LATE, _rs_stage3(stage1, from_chips, place, "late")))
    prep = _tok_bwd("rprep_bwd", _fn_rprep, [tok(zl, 1), lora], rp_params,
                    [[tok(d_decay)], [tok(d_kr2_1), tok(d_kr2_2)], [tok(d_avec)], [tok(d_bvec)], [tok(d_gate)]],
                    [(d, F32), (LORA, F32)])
    d_kr, d_lora = prep[:2]
    g["w0"], g["w2p"], g["a0"], g["a2p"], g["g2p"], g["k_k"], g["k_a"] = prep[2:]
    dz_r, g["mu"] = _lerp_bwd(z, p["mu"], (d_r1, d_r2), d_kr, (d_v1, d_v2), d_lora)
    d_qe, d_dec, d_u = _hgrn_state_bwd(d_oraw, qe, dec, h_states)
    d_q_in, d_k_in, d_kd, d_vi = _hgrn_local_bwd(q_in, k_in, kd, z, d_oraw, d_u)
    d_hq, d_hf, g["lb2"] = _tok_bwd("hgates_bwd", _fn_hgates, [tok(z, 0), tok(z, 1)], [p["lb2"]],
                                    [[tok(d_q_in)], [tok(d_k_in)], [tok(d_kd)], [tok(d_qe)], [tok(d_dec)]], [(d, BF16)] * 2)
    dz = jnp.concatenate([d_hq, d_hf, d_vi.astype(BF16), d_hg, dz_r, d_ga, d_gb], axis=1)
    stage1 = _rs_stage1(_halves([_mm("dw_in", dz, xn, "tn", tm=256, tn=1024)]), place, "w_in")
    d_xn, from_chips = _mm("d_xn", dz, p["w_in_t"], "nn", tm=1024, tn=512, tk=IN_COLS // 2, riders=stage1[2])
    (g["w_in_t"],) = _rs_stage3(stage1, from_chips, place, "w_in")
    grad_x, g["g1"] = _tok_bwd("norm1_bwd", _fn_norm, [tok(x)], [p["g1"]], [[tok(d_xn)]], [(d, F32)],
                               add_to_first=tok(d_x_res), tm=LIGHT_TM)
    return loss, grad_x, g


_WEIGHTS = ["attn_pre_norm", "w_in", "hgrn_lb", "hgrn_gnorm", "w_branch_a", "rwkv_mu", "rwkv_w0", "rwkv_w2", "rwkv_a0",
            "rwkv_a2", "rwkv_g2", "rwkv_k_k", "rwkv_k_a", "rwkv_r_k", "rwkv_ln_w", "rwkv_ln_b", "w_branch_b", "w_out",
            "attn_post_norm", "ffn_pre_norm", "w_up", "conv_w", "conv_b", "w_down", "ffn_post_norm"]
_REPLICATED = [("attn_pre_norm", "g1"), ("hgrn_lb", "lb2"), ("hgrn_gnorm", "gnorm"), ("rwkv_mu", "mu"), ("rwkv_w0", "w0"),
               ("rwkv_a0", "a0"), ("rwkv_k_k", "k_k"), ("rwkv_k_a", "k_a"), ("rwkv_r_k", "r_k"), ("rwkv_ln_w", "ln_w"),
               ("rwkv_ln_b", "ln_b"), ("attn_post_norm", "g_post1"), ("ffn_pre_norm", "g_pre2"), ("conv_b", "conv_b"),
               ("ffn_post_norm", "g_post2")]
SLAB_COLS = 1024


def _pack(arrays):
    pieces, total = [], 0
    for a in arrays:
        flat = a.reshape(-1)
        rows = -(-flat.shape[0] // SLAB_COLS)
        pieces.append(jnp.pad(flat, (0, rows * SLAB_COLS - flat.shape[0])).reshape(rows, SLAB_COLS))
        total += rows
    if total % 8:
        pieces.append(jnp.zeros((8 - total % 8, SLAB_COLS), F32))
    return jnp.concatenate(pieces, axis=0)


def _unpack(slab, shapes):
    out, at = [], 0
    for s in shapes:
        size = 1
        for dim in s:
            size *= dim
        rows = -(-size // SLAB_COLS)
        out.append(slab[at:at + rows].reshape(-1)[:size].reshape(s))
        at += rows
    return out


def kernel(x, attn_pre_norm, w_in, hgrn_lb, hgrn_gnorm, w_branch_a, rwkv_mu, rwkv_w0, rwkv_w2, rwkv_a0, rwkv_a2, rwkv_g2, rwkv_k_k, rwkv_k_a, rwkv_r_k, rwkv_ln_w, rwkv_ln_b, w_branch_b, w_out, attn_post_norm, ffn_pre_norm, w_up, conv_w, conv_b, w_down, ffn_post_norm, loss_target, m_attn_pre_norm, m_w_in, m_hgrn_lb, m_hgrn_gnorm, m_w_branch_a, m_rwkv_mu, m_rwkv_w0, m_rwkv_w2, m_rwkv_a0, m_rwkv_a2, m_rwkv_g2, m_rwkv_k_k, m_rwkv_k_a, m_rwkv_r_k, m_rwkv_ln_w, m_rwkv_ln_b, m_w_branch_b, m_w_out, m_attn_post_norm, m_ffn_pre_norm, m_w_up, m_conv_w, m_conv_b, m_w_down, m_ffn_post_norm, v_attn_pre_norm, v_w_in, v_hgrn_lb, v_hgrn_gnorm, v_w_branch_a, v_rwkv_mu, v_rwkv_w0, v_rwkv_w2, v_rwkv_a0, v_rwkv_a2, v_rwkv_g2, v_rwkv_k_k, v_rwkv_k_a, v_rwkv_r_k, v_rwkv_ln_w, v_rwkv_ln_b, v_w_branch_b, v_w_out, v_attn_post_norm, v_ffn_pre_norm, v_w_up, v_conv_w, v_conv_b, v_w_down, v_ffn_post_norm):
    given = dict(locals())
    w = {n: given[n] for n in _WEIGHTS}
    mom = {n: given["m_" + n] for n in _WEIGHTS}
    var = {n: given["v_" + n] for n in _WEIGHTS}
    shard = 2 * lax.axis_index("x") + lax.axis_index("y")
    place = jnp.stack([lax.axis_index("c"), shard]).astype(jnp.int32)
    row = lambda a: a.reshape(1, -1)
    lora_of = lambda d: jnp.concatenate([d["rwkv_w2"][0], d["rwkv_a2"][0], d["rwkv_g2"][0]], axis=0)

    shards = [w["w_in"][0].T.astype(BF16), lora_of(w), jnp.pad(w["conv_w"][0], ((0, 29), (0, 0)))]
    late_shards = [w["w_branch_a"][0].astype(BF16), w["w_branch_b"][0].astype(BF16), w["w_out"][0].astype(BF16),
                   w["w_down"][0].astype(BF16), w["w_up"][0].T.astype(BF16)]
    placed = [_landing(a) for a in shards]
    late_placed = [_landing(a) for a in late_shards]
    w_in_t, lora_g, conv_g = _gather_shards(shards, placed)
    lora_full = lora_g.reshape(N_SHARD, LORA, 256).transpose(1, 0, 2).reshape(LORA, D_MODEL)
    conv_full = conv_g.reshape(N_SHARD, 32, 2 * D_FF // N_SHARD)[:, :3].transpose(1, 0, 2).reshape(3, 2 * D_FF)
    lrow = lax.broadcasted_iota(jnp.int32, (LORA, 1), 0)
    p = {
        "g1": row(w["attn_pre_norm"]), "lb2": w["hgrn_lb"], "gnorm": row(w["hgrn_gnorm"]), "w_in_t": w_in_t,
        "mu": row(w["rwkv_mu"]), "w0": row(w["rwkv_w0"]), "a0": row(w["rwkv_a0"]),
        "w2p": jnp.where(lrow < 64, lora_full, 0.0), "a2p": jnp.where((lrow >= 64) & (lrow < 128), lora_full, 0.0),
        "g2p": jnp.where(lrow >= 128, lora_full, 0.0),
        "k_k": row(w["rwkv_k_k"]), "k_a": row(w["rwkv_k_a"]), "r_k": row(w["rwkv_r_k"]), "ln_w": row(w["rwkv_ln_w"]),
        "ln_b": row(w["rwkv_ln_b"]), "g_post1": row(w["attn_post_norm"]),
        "g_pre2": row(w["ffn_pre_norm"]), "conv_w": conv_full, "conv_b": row(w["conv_b"]),
        "g_post2": row(w["ffn_post_norm"]),
    }

    loss, grad_x, g = _device_step(x[0], loss_target[0], p, late_shards, late_placed, place)

    g_in_t = g["w_in_t"]
    g_a, g_b, g_o, g_dn, g_up_t = [g[n] for n in _LATE]
    rep_shapes = [w[n].shape for n, _ in _REPLICATED]
    rep = _pack([g[key] for _, key in _REPLICATED])
    n_rep_rows = rep.shape[0]
    cw = 2 * D_FF // N_SHARD
    lora_rows, conv_rows = LORA * 256 // SLAB_COLS, -(-3 * cw // SLAB_COLS)
    lora_g = jnp.concatenate([g["w2p"][0:64], g["a2p"][64:128], g["g2p"][128:256]], axis=0)
    lora_parts = lora_g.reshape(LORA, N_SHARD, 256).transpose(1, 0, 2).reshape(N_SHARD, lora_rows, SLAB_COLS)
    conv_parts = g["conv_w"].reshape(3, N_SHARD, cw).transpose(1, 0, 2).reshape(N_SHARD, 3 * cw)
    conv_parts = jnp.pad(conv_parts, ((0, 0), (0, conv_rows * SLAB_COLS - 3 * cw))).reshape(N_SHARD, conv_rows, SLAB_COLS)
    n_rows = n_rep_rows + lora_rows + conv_rows
    fill = jnp.zeros((N_SHARD, -n_rows % 8, SLAB_COLS), F32)
    parts = jnp.concatenate([jnp.broadcast_to(rep, (N_SHARD,) + rep.shape), lora_parts, conv_parts, fill], axis=1)
    me = 4 * lax.axis_index("x") + 2 * lax.axis_index("y") + lax.axis_index("c")
    landing = lax.dynamic_update_slice(jnp.zeros((8,) + parts.shape[1:], F32),
                                       lax.dynamic_index_in_dim(parts, shard, 0, keepdims=True), (me, 0, 0))

    res = {}

    def put(name, outs, shape=None):
        res[name] = [o.reshape(w[name].shape if shape is None else shape) for o in outs]

    w_in_out, gathered = _adamw("adamw_w_in", w["w_in"][0], g_in_t, mom["w_in"][0], var["w_in"][0], 1024, 128,
                                g_transposed=True, exchange=(parts, landing))
    put("w_in", w_in_out)
    summed = _sum3d("small_sum", [(gathered, functools.partial(lambda l, s, i: i, i=i)) for i in range(8)], place, 1, 1,
                    lambda l, s: 0)[0]
    lora_grad = summed[n_rep_rows:n_rep_rows + lora_rows].reshape(LORA, 256)
    conv_grad = summed[n_rep_rows + lora_rows:n_rows].reshape(-1)[:3 * cw].reshape(3, cw)
    put("w_up", _adamw("adamw_w_up", w["w_up"][0], g_up_t, mom["w_up"][0], var["w_up"][0], 1024, 128, g_transposed=True))
    for name, grad in (("w_branch_a", g_a), ("w_branch_b", g_b), ("w_out", g_o)):
        put(name, _adamw("adamw_" + name, w[name][0], grad, mom[name][0], var[name][0], 256, 1024))
    put("w_down", _adamw("adamw_w_down", w["w_down"][0], g_dn, mom["w_down"][0], var["w_down"][0], 176, 1024))
    put("conv_w", _adamw("adamw_conv_w", w["conv_w"][0], conv_grad, mom["conv_w"][0], var["conv_w"][0], 3, 2 * D_FF // N_SHARD))
    lora_out = _adamw("adamw_lora", lora_of(w), lora_grad, lora_of(mom), lora_of(var), LORA, 256)
    for name, lo, hi in (("rwkv_w2", 0, 64), ("rwkv_a2", 64, 128), ("rwkv_g2", 128, 256)):
        put(name, [o[lo:hi] for o in lora_out])
    rep_names = [n for n, _ in _REPLICATED]
    rep_out = _adamw("adamw_small", _pack([w[n] for n in rep_names]), summed[:n_rep_rows], _pack([mom[n] for n in rep_names]),
                     _pack([var[n] for n in rep_names]), n_rep_rows, SLAB_COLS)
    for name, parts in zip(rep_names, zip(*[_unpack(o, rep_shapes) for o in rep_out])):
        put(name, list(parts))

    loss = lax.psum(loss[0, 0], ("x", "y", "c"))
    return (loss, grad_x[None], *[res[n][0] for n in _WEIGHTS], *[res[n][1] for n in _WEIGHTS],
            *[res[n][2] for n in _WEIGHTS], *[res[n][3] for n in _WEIGHTS])
```
